```python
import jax, jax.numpy as jnp
from jax import lax
import numpy as np

D_MODEL = 2048
BATCH = 1
SEQ = 8192
DEPTH = 1

CHUNK = 64
CONV_WIDTH = 1024
CONV_K = 3
SGU_WIDTH = 1024
SGU_HEADS = 8
SGU_HEAD_DIM = SGU_WIDTH // SGU_HEADS
SGU_BLOCK = 128
N_BRANCH = 2
IN_COLS = 3 * CONV_WIDTH + 2 * SGU_WIDTH + N_BRANCH * D_MODEL
N_GROUPS = 4
EXPERTS_PER_GROUP = 4
N_EXPERTS = N_GROUPS * EXPERTS_PER_GROUP
TOP_K_INNER = 2
D_EXPERT = D_MODEL // 4
EPS = 1e-6

kernel_name = 'hybrid_shortconv_sgu_hiermoe_block'


def rms_norm(x, g):
    xf = x.astype(jnp.float32)
    y = xf * lax.rsqrt(jnp.mean(xf * xf, axis=-1, keepdims=True) + EPS)
    return (y * g.astype(jnp.float32)).astype(x.dtype)


def layer_norm(x, g, b):
    xf = x.astype(jnp.float32)
    mu = jnp.mean(xf, axis=-1, keepdims=True)
    xc = xf - mu
    y = xc * lax.rsqrt(jnp.mean(xc * xc, axis=-1, keepdims=True) + EPS)
    return (y * g.astype(jnp.float32) + b.astype(jnp.float32)).astype(x.dtype)


def causal_depthwise_conv(z, w):
    c = z.shape[-1]
    return lax.conv_general_dilated(
        z, w[:, None, :].astype(z.dtype), window_strides=(1,),
        padding=[(CONV_K - 1, 0)], dimension_numbers=('NWC', 'WIO', 'NWC'),
        feature_group_count=c)


def short_conv_mixer(b_gate, c_gate, h_in, conv_w):
    return b_gate * causal_depthwise_conv(c_gate * h_in, conv_w)


def spatial_gating_mixer(z, ln_g, ln_b, w_s, b_s):
    bsz, s, _ = z.shape
    z = jax.nn.gelu(z)
    u, v = jnp.split(z, 2, axis=-1)
    v = layer_norm(v, ln_g, ln_b)
    chunk_id = jnp.arange(SGU_BLOCK) // CHUNK
    mask = chunk_id[None, :] <= chunk_id[:, None]
    w = jnp.where(mask[None], w_s, jnp.zeros_like(w_s)).astype(v.dtype)
    vb = v.reshape(bsz, s // SGU_BLOCK, SGU_BLOCK, SGU_HEADS, SGU_HEAD_DIM)
    vm = jnp.einsum('hij,bnjhc->bnihc', w, vb) + b_s.T.astype(v.dtype)[:, :, None]
    return u * vm.reshape(bsz, s, SGU_WIDTH)


def hierarchical_moe(x, w_rg, b_rg, w_re, b_re, w_gate, w_up, w_down):
    bsz, s, d = x.shape
    t = x.reshape(-1, d)
    tf = t.astype(jnp.float32)
    p_group = jax.nn.softmax(tf @ w_rg.astype(jnp.float32) + b_rg.astype(jnp.float32), axis=-1)
    pg, g_idx = lax.top_k(p_group, 1)
    oh_g = jax.nn.one_hot(g_idx[:, 0], N_GROUPS, dtype=jnp.float32)
    exp_logits = (tf @ w_re.astype(jnp.float32) + b_re.astype(jnp.float32)).reshape(-1, N_GROUPS, EXPERTS_PER_GROUP)
    sel_logits = jnp.einsum('tg,tge->te', oh_g, exp_logits)
    pe, e_idx = lax.top_k(jax.nn.softmax(sel_logits, axis=-1), TOP_K_INNER)
    pe = pe / jnp.sum(pe, axis=-1, keepdims=True)
    w_inner = jnp.einsum('tk,tke->te', pe, jax.nn.one_hot(e_idx, EXPERTS_PER_GROUP, dtype=jnp.float32))
    gate = ((pg * oh_g)[:, :, None] * w_inner[:, None, :]).reshape(-1, N_EXPERTS).astype(x.dtype)
    hid = jax.nn.silu(jnp.einsum('td,edf->tef', t, w_gate)) * jnp.einsum('td,edf->tef', t, w_up)
    y = jnp.einsum('tef,efd->td', hid * gate[:, :, None], w_down)
    return y.reshape(bsz, s, d)


def setup_inputs(seed: int = 0) -> dict:
    key = jax.random.key(seed)
    ks = jax.random.split(key, 20)
    f32 = jnp.float32
    nrm = lambda k, shape, scale: jax.random.normal(k, shape, f32) * scale
    L = DEPTH
    return {
        'x': jax.random.normal(ks[0], (BATCH, SEQ, D_MODEL), f32),
        'norm_mix_g': 1.0 + nrm(ks[1], (L, D_MODEL), 0.02),
        'w_in': nrm(ks[2], (L, D_MODEL, IN_COLS), D_MODEL ** -0.5),
        'conv_w': nrm(ks[3], (L, CONV_K, CONV_WIDTH), CONV_K ** -0.5),
        'sgu_ln_g': 1.0 + nrm(ks[4], (L, SGU_WIDTH), 0.02),
        'sgu_ln_b': nrm(ks[5], (L, SGU_WIDTH), 0.02),
        'sgu_w_s': nrm(ks[6], (L, SGU_HEADS, SGU_BLOCK, SGU_BLOCK), SGU_BLOCK ** -0.5),
        'sgu_b_s': 1.0 + nrm(ks[7], (L, SGU_HEADS, SGU_BLOCK), 0.02),
        'w_up_conv': nrm(ks[8], (L, CONV_WIDTH, D_MODEL), CONV_WIDTH ** -0.5),
        'w_up_sgu': nrm(ks[9], (L, SGU_WIDTH, D_MODEL), SGU_WIDTH ** -0.5),
        'w_out': nrm(ks[10], (L, D_MODEL, D_MODEL), D_MODEL ** -0.5),
        'norm_ffn_g': 1.0 + nrm(ks[11], (L, D_MODEL), 0.02),
        'w_router_group': nrm(ks[12], (L, D_MODEL, N_GROUPS), D_MODEL ** -0.5),
        'b_router_group': nrm(ks[13], (L, N_GROUPS), 0.01),
        'w_router_expert': nrm(ks[14], (L, D_MODEL, N_EXPERTS), D_MODEL ** -0.5),
        'b_router_expert': nrm(ks[15], (L, N_EXPERTS), 0.01),
        'w_exp_gate': nrm(ks[16], (L, N_EXPERTS, D_MODEL, D_EXPERT), D_MODEL ** -0.5),
        'w_exp_up': nrm(ks[17], (L, N_EXPERTS, D_MODEL, D_EXPERT), D_MODEL ** -0.5),
        'w_exp_down': nrm(ks[18], (L, N_EXPERTS, D_EXPERT, D_MODEL), D_EXPERT ** -0.5),
        'norm_final_g': 1.0 + nrm(ks[19], (D_MODEL,), 0.02),
    }


def reference(x, norm_mix_g, w_in, conv_w, sgu_ln_g, sgu_ln_b, sgu_w_s, sgu_b_s,
              w_up_conv, w_up_sgu, w_out, norm_ffn_g, w_router_group, b_router_group,
              w_router_expert, b_router_expert, w_exp_gate, w_exp_up, w_exp_down,
              norm_final_g):
    h = x
    splits = [CONV_WIDTH, 2 * CONV_WIDTH, 3 * CONV_WIDTH, 3 * CONV_WIDTH + 2 * SGU_WIDTH]
    for l in range(DEPTH):
        xn = rms_norm(h, norm_mix_g[l])
        proj = xn @ w_in[l]
        b_c, c_c, h_c, z_s, gate_logits = jnp.split(proj, splits, axis=-1)
        y_conv = short_conv_mixer(b_c, c_c, h_c, conv_w[l]) @ w_up_conv[l]
        y_sgu = spatial_gating_mixer(z_s, sgu_ln_g[l], sgu_ln_b[l], sgu_w_s[l], sgu_b_s[l]) @ w_up_sgu[l]
        g_conv, g_sgu = jnp.split(jax.nn.sigmoid(gate_logits), 2, axis=-1)
        h = h + (g_conv * y_conv + g_sgu * y_sgu) @ w_out[l]
        h = h + hierarchical_moe(rms_norm(h, norm_ffn_g[l]), w_router_group[l], b_router_group[l],
                                 w_router_expert[l], b_router_expert[l],
                                 w_exp_gate[l], w_exp_up[l], w_exp_down[l])
    return rms_norm(h, norm_final_g)
```

```python
import functools

import jax
import jax.numpy as jnp
from jax import lax
from jax.experimental import pallas as pl
from jax.experimental.pallas import tpu as pltpu

F32 = jnp.float32
BF16 = jnp.bfloat16

EPS = 1e-6
CHUNK = 64
CONV_K = 3
SGU_HEADS = 8
SGU_BLOCK = 128
N_GROUPS = 4
EXPERTS_PER_GROUP = 4
N_PAIRS = 6
N_CLASSES = N_GROUPS * N_PAIRS
CLASS_ROWS = 32
ROUTER_ROWS = 32
LANES = 128

VMEM_LIMIT_BYTES = 56 * 1024 * 1024

MOE_TILE = 128
DMA_CHUNK = 512
DMA_UNROLL = 8


def _params(n_axes):
    return pltpu.CompilerParams(
        dimension_semantics=("arbitrary",) * n_axes,
        vmem_limit_bytes=VMEM_LIMIT_BYTES)


def _dot(a, b):
    return jnp.dot(a, b, preferred_element_type=F32)


def _rmsnorm_kernel(x_ref, g_ref, o_ref):
    x = x_ref[...]
    ms = jnp.mean(x * x, axis=-1, keepdims=True)
    o_ref[...] = (x * lax.rsqrt(ms + EPS) * g_ref[...]).astype(o_ref.dtype)


def _rmsnorm(x, g, tm=512):
    t, d = x.shape
    return pl.pallas_call(
        _rmsnorm_kernel,
        grid=(t // tm,),
        in_specs=[pl.BlockSpec((tm, d), lambda i: (i, 0)),
                  pl.BlockSpec((1, d), lambda i: (0, 0))],
        out_specs=pl.BlockSpec((tm, d), lambda i: (i, 0)),
        out_shape=jax.ShapeDtypeStruct((t, d), BF16),
        compiler_params=_params(1),
        name="rmsnorm",
    )(x, g.reshape(1, d))


def _conv_kernel(xn_ref, wb_ref, wc_ref, wh_ref, cw_ref, o_ref, wbf_ref, carry_ref):
    i = pl.program_id(1)
    tn = wb_ref.shape[1]
    tm = xn_ref.shape[0]

    @pl.when(i == 0)
    def _():
        wbf_ref[:, 0:tn] = wb_ref[...].astype(BF16)
        wbf_ref[:, tn:2 * tn] = wc_ref[...].astype(BF16)
        wbf_ref[:, 2 * tn:3 * tn] = wh_ref[...].astype(BF16)
        carry_ref[...] = jnp.zeros_like(carry_ref)

    proj = _dot(xn_ref[...], wbf_ref[...])
    b = proj[:, 0:tn]
    p = proj[:, tn:2 * tn] * proj[:, 2 * tn:3 * tn]
    prev = carry_ref[...]
    carry_ref[...] = p[tm - 8:tm, :]
    row = lax.broadcasted_iota(jnp.int32, p.shape, 0)
    p1 = jnp.where(row == 0, prev[7:8, :], pltpu.roll(p, 1, axis=0))
    p2 = jnp.where(row == 0, prev[6:7, :],
                   jnp.where(row == 1, prev[7:8, :], pltpu.roll(p, 2, axis=0)))
    cw = cw_ref[...]
    y = b * (cw[0:1, :] * p2 + cw[1:2, :] * p1 + cw[2:3, :] * p)
    o_ref[...] = y.astype(o_ref.dtype)


def _conv_branch(xn, w_in, conv_w, width, tm=1024, tn=256):
    t, d = xn.shape
    nj = width // tn
    return pl.pallas_call(
        _conv_kernel,
        grid=(nj, t // tm),
        in_specs=[pl.BlockSpec((tm, d), lambda j, i: (i, 0)),
                  pl.BlockSpec((d, tn), lambda j, i: (0, j)),
                  pl.BlockSpec((d, tn), lambda j, i: (0, nj + j)),
                  pl.BlockSpec((d, tn), lambda j, i: (0, 2 * nj + j)),
                  pl.BlockSpec((CONV_K, tn), lambda j, i: (0, j))],
        out_specs=pl.BlockSpec((tm, tn), lambda j, i: (i, j)),
        out_shape=jax.ShapeDtypeStruct((t, width), BF16),
        scratch_shapes=[pltpu.VMEM((d, 3 * tn), BF16),
                        pltpu.VMEM((8, tn), F32)],
        compiler_params=_params(2),
        name="conv_branch",
    )(xn, w_in, w_in, w_in, conv_w)


def _zproj_kernel(xn_ref, w_ref, o_ref, wbf_ref):
    @pl.when(pl.program_id(1) == 0)
    def _():
        wbf_ref[...] = w_ref[...].astype(BF16)

    z = _dot(xn_ref[...], wbf_ref[...])
    o_ref[...] = jax.nn.gelu(z).astype(o_ref.dtype)


def _zproj(xn, w_in, col0, width, tm=1024, tn=512):
    t, d = xn.shape
    c0 = col0 // tn
    return pl.pallas_call(
        _zproj_kernel,
        grid=(width // tn, t // tm),
        in_specs=[pl.BlockSpec((tm, d), lambda j, i: (i, 0)),
                  pl.BlockSpec((d, tn), lambda j, i: (0, c0 + j))],
        out_specs=pl.BlockSpec((tm, tn), lambda j, i: (i, j)),
        out_shape=jax.ShapeDtypeStruct((t, width), BF16),
        scratch_shapes=[pltpu.VMEM((d, tn), BF16)],
        compiler_params=_params(2),
        name="sgu_zproj",
    )(xn, w_in)


def _sgu_kernel(gz_ref, lng_ref, lnb_ref, ws_ref, bsx_ref, o_ref):
    tm = gz_ref.shape[0]
    w = o_ref.shape[1]
    hd = w // SGU_HEADS
    v = gz_ref[:, w:2 * w].astype(F32)
    mu = jnp.mean(v, axis=-1, keepdims=True)
    vc = v - mu
    var = jnp.mean(vc * vc, axis=-1, keepdims=True)
    vn = (vc * lax.rsqrt(var + EPS) * lng_ref[...] + lnb_ref[...]).astype(BF16)
    ii = lax.broadcasted_iota(jnp.int32, (SGU_BLOCK, SGU_BLOCK), 0)
    jj = lax.broadcasted_iota(jnp.int32, (SGU_BLOCK, SGU_BLOCK), 1)
    mask = (jj // CHUNK) <= (ii // CHUNK)
    for h in range(SGU_HEADS):
        wm = jnp.where(mask, ws_ref[h], 0.0).astype(BF16)
        cs = slice(h * hd, (h + 1) * hd)
        for n in range(tm // SGU_BLOCK):
            rs = slice(n * SGU_BLOCK, (n + 1) * SGU_BLOCK)
            vm = _dot(wm, vn[rs, cs]) + bsx_ref[:, cs]
            o_ref[rs, cs] = (gz_ref[rs, cs].astype(F32) * vm).astype(o_ref.dtype)


def _sgu_mix(gz, ln_g, ln_b, w_s, b_s, tm=512):
    t, w2 = gz.shape
    w = w2 // 2
    hd = w // SGU_HEADS
    bsx = jnp.repeat(b_s.T, hd, axis=1)
    return pl.pallas_call(
        _sgu_kernel,
        grid=(t // tm,),
        in_specs=[pl.BlockSpec((tm, w2), lambda i: (i, 0)),
                  pl.BlockSpec((1, w), lambda i: (0, 0)),
                  pl.BlockSpec((1, w), lambda i: (0, 0)),
                  pl.BlockSpec((SGU_HEADS, SGU_BLOCK, SGU_BLOCK), lambda i: (0, 0, 0)),
                  pl.BlockSpec((SGU_BLOCK, w), lambda i: (0, 0))],
        out_specs=pl.BlockSpec((tm, w), lambda i: (i, 0)),
        out_shape=jax.ShapeDtypeStruct((t, w), BF16),
        compiler_params=_params(1),
        name="sgu_mix",
    )(gz, ln_g.reshape(1, w), ln_b.reshape(1, w), w_s, bsx)


def _upgate_kernel(xn_ref, ya_ref, yb_ref, wgc_ref, wgs_ref, wua_ref, wub_ref, o_ref,
                   wgbf_ref, wuabf_ref, wubbf_ref):
    tn = wgc_ref.shape[1]

    @pl.when(pl.program_id(1) == 0)
    def _():
        wgbf_ref[:, 0:tn] = wgc_ref[...].astype(BF16)
        wgbf_ref[:, tn:2 * tn] = wgs_ref[...].astype(BF16)
        wuabf_ref[...] = wua_ref[...].astype(BF16)
        wubbf_ref[...] = wub_ref[...].astype(BF16)

    gl = _dot(xn_ref[...], wgbf_ref[...])
    a = _dot(ya_ref[...], wuabf_ref[...])
    b = _dot(yb_ref[...], wubbf_ref[...])
    m = jax.nn.sigmoid(gl[:, 0:tn]) * a + jax.nn.sigmoid(gl[:, tn:2 * tn]) * b
    o_ref[...] = m.astype(o_ref.dtype)


def _upgate(xn, ya, yb, w_in, gate_col0, w_up_a, w_up_b, tm=512, tn=512):
    t, d = xn.shape
    wa = ya.shape[1]
    wb = yb.shape[1]
    dout = w_up_a.shape[1]
    c0 = gate_col0 // tn
    nj = dout // tn
    return pl.pallas_call(
        _upgate_kernel,
        grid=(nj, t // tm),
        in_specs=[pl.BlockSpec((tm, d), lambda j, i: (i, 0)),
                  pl.BlockSpec((tm, wa), lambda j, i: (i, 0)),
                  pl.BlockSpec((tm, wb), lambda j, i: (i, 0)),
                  pl.BlockSpec((d, tn), lambda j, i: (0, c0 + j)),
                  pl.BlockSpec((d, tn), lambda j, i: (0, c0 + nj + j)),
                  pl.BlockSpec((wa, tn), lambda j, i: (0, j)),
                  pl.BlockSpec((wb, tn), lambda j, i: (0, j))],
        out_specs=pl.BlockSpec((tm, tn), lambda j, i: (i, j)),
        out_shape=jax.ShapeDtypeStruct((t, dout), BF16),
        scratch_shapes=[pltpu.VMEM((d, 2 * tn), BF16),
                        pltpu.VMEM((wa, tn), BF16),
                        pltpu.VMEM((wb, tn), BF16)],
        compiler_params=_params(2),
        name="upgate",
    )(xn, ya, yb, w_in, w_in, w_up_a, w_up_b)


def _outproj_kernel(m_ref, w_ref, x_ref, o_ref, wbf_ref):
    @pl.when(pl.program_id(1) == 0)
    def _():
        wbf_ref[...] = w_ref[...].astype(BF16)

    o_ref[...] = x_ref[...] + _dot(m_ref[...], wbf_ref[...])


def _outproj(m, w_out, x, tm=512, tn=512):
    t, d = m.shape
    dout = w_out.shape[1]
    return pl.pallas_call(
        _outproj_kernel,
        grid=(dout // tn, t // tm),
        in_specs=[pl.BlockSpec((tm, d), lambda j, i: (i, 0)),
                  pl.BlockSpec((d, tn), lambda j, i: (0, j)),
                  pl.BlockSpec((tm, tn), lambda j, i: (i, j))],
        out_specs=pl.BlockSpec((tm, tn), lambda j, i: (i, j)),
        out_shape=jax.ShapeDtypeStruct((t, dout), F32),
        scratch_shapes=[pltpu.VMEM((d, tn), BF16)],
        compiler_params=_params(2),
        name="outproj",
    )(m, w_out, x)


def _argmax_rows(rows):
    best = rows[0]
    idx = jnp.zeros(rows[0].shape, jnp.int32)
    for k in range(1, len(rows)):
        better = rows[k] > best
        best = jnp.where(better, rows[k], best)
        idx = jnp.where(better, k, idx)
    return best, idx


def _softmax_rows(rows):
    mx = functools.reduce(jnp.maximum, rows)
    ex = [jnp.exp(r - mx) for r in rows]
    den = functools.reduce(lambda a, b: a + b, ex)
    return [e / den for e in ex]


def _router_kernel(h_ref, g_ref, wr_ref, br_ref, meta_ref, cnt_ref, carry_ref):
    i = pl.program_id(0)
    tm = h_ref.shape[0]

    @pl.when(i == 0)
    def _():
        carry_ref[...] = jnp.zeros_like(carry_ref)

    x = h_ref[...]
    ms = jnp.mean(x * x, axis=-1, keepdims=True)
    xn = x * lax.rsqrt(ms + EPS) * g_ref[...]
    lt = lax.dot_general(wr_ref[...], xn, (((1,), (1,)), ((), ())),
                         precision=lax.Precision.HIGHEST,
                         preferred_element_type=F32) + br_ref[...]
    pgs = _softmax_rows([lt[k:k + 1, :] for k in range(N_GROUPS)])
    pg, gi = _argmax_rows(pgs)
    sel = []
    for k in range(EXPERTS_PER_GROUP):
        r = jnp.zeros_like(pg)
        for g in range(N_GROUPS):
            row = N_GROUPS + g * EXPERTS_PER_GROUP + k
            r = jnp.where(gi == g, lt[row:row + 1, :], r)
        sel.append(r)
    pes = _softmax_rows(sel)
    p1, e1 = _argmax_rows(pes)
    rest = [jnp.where(e1 == k, -1.0, pes[k]) for k in range(EXPERTS_PER_GROUP)]
    p2, e2 = _argmax_rows(rest)
    den = p1 + p2
    w1 = pg * (p1 / den)
    w2 = pg * (p2 / den)
    lo = jnp.minimum(e1, e2)
    hi = jnp.maximum(e1, e2)
    w_lo = jnp.where(e1 < e2, w1, w2)
    w_hi = jnp.where(e1 < e2, w2, w1)
    pair = jnp.where(lo == 0, hi - 1, jnp.where(lo == 1, hi + 1, 5))
    cls = gi * N_PAIRS + pair

    crow = lax.broadcasted_iota(jnp.int32, (CLASS_ROWS, tm), 0)
    oh = crow == cls
    a = lax.broadcasted_iota(jnp.int32, (tm, tm), 0)
    b = lax.broadcasted_iota(jnp.int32, (tm, tm), 1)
    before = (a < b).astype(BF16)
    cum = _dot(oh.astype(BF16), before)
    base = carry_ref[...][:, 0:1]
    ohf = oh.astype(F32)
    rank = jnp.sum(ohf * (cum + base), axis=0, keepdims=True)
    total = base + jnp.sum(ohf, axis=1, keepdims=True)
    carry_ref[...] = jnp.broadcast_to(total, carry_ref.shape)
    cnt_ref[...] = jnp.broadcast_to(total, cnt_ref.shape)

    meta_ref[0:1, :] = cls.astype(F32)
    meta_ref[1:2, :] = rank
    meta_ref[2:3, :] = w_lo
    meta_ref[3:4, :] = w_hi
    meta_ref[4:8, :] = jnp.zeros((4, tm), F32)


def _router(h1, g, w_rg, b_rg, w_re, b_re, tm=512):
    t, d = h1.shape
    n_log = w_rg.shape[1] + w_re.shape[1]
    wr = jnp.concatenate([w_rg, w_re], axis=1).T
    wr = jnp.pad(wr, ((0, ROUTER_ROWS - n_log), (0, 0)))
    br = jnp.pad(jnp.concatenate([b_rg, b_re]), (0, ROUTER_ROWS - n_log)).reshape(ROUTER_ROWS, 1)
    return pl.pallas_call(
        _router_kernel,
        grid=(t // tm,),
        in_specs=[pl.BlockSpec((tm, d), lambda i: (i, 0)),
                  pl.BlockSpec((1, d), lambda i: (0, 0)),
                  pl.BlockSpec((ROUTER_ROWS, d), lambda i: (0, 0)),
                  pl.BlockSpec((ROUTER_ROWS, 1), lambda i: (0, 0))],
        out_specs=[pl.BlockSpec((8, tm), lambda i: (0, i)),
                   pl.BlockSpec((CLASS_ROWS, LANES), lambda i: (0, 0))],
        out_shape=[jax.ShapeDtypeStruct((8, t), F32),
                   jax.ShapeDtypeStruct((CLASS_ROWS, LANES), F32)],
        scratch_shapes=[pltpu.VMEM((CLASS_ROWS, LANES), F32)],
        compiler_params=_params(1),
        name="router",
    )(h1, g.reshape(1, d), wr, br)


def _row_copy(src_ref, dst_ref, s, d, sem):
    return pltpu.make_async_copy(src_ref.at[pl.ds(s, 1), :], dst_ref.at[pl.ds(d, 1), :], sem)


def _chunk_wait(src_ref, dst_ref, n, sem):
    pltpu.make_async_copy(src_ref.at[pl.ds(0, n), :], dst_ref.at[pl.ds(0, n), :], sem).wait()


def _tile_copy(src_ref, dst_ref, r, sem):
    return pltpu.make_async_copy(src_ref, dst_ref.at[pl.ds(r * MOE_TILE, MOE_TILE), :], sem)


def _dispatch_kernel(pos_ref, pad_lo_ref, pad_hi_ref, nt_ref, h_ref, wm_ref, zrow_ref, zmeta_ref,
                     xs_ref, ws_ref, sems, pad_sem):
    c = pl.program_id(0)
    nc = pl.num_programs(0)
    slot = c % 2

    def issue(k, carry):
        for u in range(DMA_UNROLL):
            t = c * DMA_CHUNK + k * DMA_UNROLL + u
            p = pos_ref[t]
            _row_copy(h_ref, xs_ref, t, p, sems.at[slot]).start()
            _row_copy(wm_ref, ws_ref, t, p, sems.at[slot]).start()
        return carry

    lax.fori_loop(0, DMA_CHUNK // DMA_UNROLL, issue, 0)

    def wait_chunk(s):
        _chunk_wait(h_ref, xs_ref, DMA_CHUNK, sems.at[s])
        _chunk_wait(wm_ref, ws_ref, DMA_CHUNK, sems.at[s])

    @pl.when(c > 0)
    def _():
        wait_chunk(1 - slot)

    @pl.when(c == nc - 1)
    def _():
        wait_chunk(slot)
        def pad_row(r, carry):
            _row_copy(zrow_ref, xs_ref, 0, r, pad_sem).start()
            _row_copy(zmeta_ref, ws_ref, 0, r, pad_sem).start()
            return carry

        def pad_wait(r, carry):
            _row_copy(zrow_ref, xs_ref, 0, r, pad_sem).wait()
            _row_copy(zmeta_ref, ws_ref, 0, r, pad_sem).wait()
            return carry

        def tail_tile(r, carry):
            _tile_copy(zrow_ref, xs_ref, r, pad_sem).start()
            _tile_copy(zmeta_ref, ws_ref, r, pad_sem).start()
            return carry

        def tail_wait(r, carry):
            _tile_copy(zrow_ref, xs_ref, r, pad_sem).wait()
            _tile_copy(zmeta_ref, ws_ref, r, pad_sem).wait()
            return carry

        r_max = xs_ref.shape[0] // MOE_TILE
        for k in range(N_CLASSES):
            lax.fori_loop(pad_lo_ref[k], pad_hi_ref[k], pad_row, 0)
        lax.fori_loop(nt_ref[0], r_max, tail_tile, 0)
        for k in range(N_CLASSES):
            lax.fori_loop(pad_lo_ref[k], pad_hi_ref[k], pad_wait, 0)
        lax.fori_loop(nt_ref[0], r_max, tail_wait, 0)


def _dispatch(pos, pad_lo, pad_hi, n_tiles, h1, wmeta, n_rows):
    t, d = h1.shape
    any_spec = pl.BlockSpec(memory_space=pl.ANY)
    grid_spec = pltpu.PrefetchScalarGridSpec(
        num_scalar_prefetch=4,
        grid=(t // DMA_CHUNK,),
        in_specs=[any_spec, any_spec, any_spec, any_spec],
        out_specs=[any_spec, any_spec],
        scratch_shapes=[pltpu.SemaphoreType.DMA((2,)), pltpu.SemaphoreType.DMA(())],
    )
    return pl.pallas_call(
        _dispatch_kernel,
        grid_spec=grid_spec,
        out_shape=[jax.ShapeDtypeStruct((n_rows, d), F32),
                   jax.ShapeDtypeStruct((n_rows, LANES), F32)],
        compiler_params=_params(1),
        name="dispatch",
    )(pos, pad_lo, pad_hi, n_tiles, h1, wmeta,
      jnp.zeros((MOE_TILE, d), F32), jnp.zeros((MOE_TILE, LANES), F32))


def _combine_kernel(pos_ref, ys_ref, o_ref, sems):
    c = pl.program_id(0)
    nc = pl.num_programs(0)
    slot = c % 2

    def issue(k, carry):
        for u in range(DMA_UNROLL):
            t = c * DMA_CHUNK + k * DMA_UNROLL + u
            _row_copy(ys_ref, o_ref, pos_ref[t], t, sems.at[slot]).start()
        return carry

    lax.fori_loop(0, DMA_CHUNK // DMA_UNROLL, issue, 0)

    @pl.when(c > 0)
    def _():
        _chunk_wait(ys_ref, o_ref, DMA_CHUNK, sems.at[1 - slot])

    @pl.when(c == nc - 1)
    def _():
        _chunk_wait(ys_ref, o_ref, DMA_CHUNK, sems.at[slot])


def _combine(pos, ys, t):
    d = ys.shape[1]
    any_spec = pl.BlockSpec(memory_space=pl.ANY)
    grid_spec = pltpu.PrefetchScalarGridSpec(
        num_scalar_prefetch=1,
        grid=(t // DMA_CHUNK,),
        in_specs=[any_spec],
        out_specs=any_spec,
        scratch_shapes=[pltpu.SemaphoreType.DMA((2,))],
    )
    return pl.pallas_call(
        _combine_kernel,
        grid_spec=grid_spec,
        out_shape=jax.ShapeDtypeStruct((t, d), F32),
        compiler_params=_params(1),
        name="combine",
    )(pos, ys)


def _expert_up_kernel(ea_ref, eb_ref, nt_ref, xs_ref, ws_ref, g_ref,
                      wga_ref, wua_ref, wgb_ref, wub_ref, hid_ref,
                      wabf_ref, wbbf_ref, cur_ref):
    r = pl.program_id(0)
    f = wga_ref.shape[1]

    @pl.when(r == 0)
    def _():
        cur_ref[0] = -1
        cur_ref[1] = -1

    @pl.when(r >= nt_ref[0])
    def _():
        hid_ref[...] = jnp.zeros_like(hid_ref)

    @pl.when(r < nt_ref[0])
    def _():
        ea = ea_ref[r]
        eb = eb_ref[r]

        @pl.when(cur_ref[0] != ea)
        def _():
            wabf_ref[:, 0:f] = wga_ref[...].astype(BF16)
            wabf_ref[:, f:2 * f] = wua_ref[...].astype(BF16)
            cur_ref[0] = ea

        @pl.when(cur_ref[1] != eb)
        def _():
            wbbf_ref[:, 0:f] = wgb_ref[...].astype(BF16)
            wbbf_ref[:, f:2 * f] = wub_ref[...].astype(BF16)
            cur_ref[1] = eb

        x = xs_ref[...]
        ms = jnp.mean(x * x, axis=-1, keepdims=True)
        xn = (x * lax.rsqrt(ms + EPS) * g_ref[...]).astype(BF16)
        wts = ws_ref[...]
        ha = _dot(xn, wabf_ref[...])
        hid_ref[:, 0:f] = (jax.nn.silu(ha[:, 0:f]) * ha[:, f:2 * f] * wts[:, 0:1]).astype(hid_ref.dtype)
        hb = _dot(xn, wbbf_ref[...])
        hid_ref[:, f:2 * f] = (jax.nn.silu(hb[:, 0:f]) * hb[:, f:2 * f] * wts[:, 1:2]).astype(hid_ref.dtype)


def _expert_up(tile_ea, tile_eb, n_tiles, xs, ws, g, w_gate, w_up):
    n_rows, d = xs.shape
    f = w_gate.shape[2]
    r_max = n_rows // MOE_TILE

    def row_map(r, ea, eb, nt):
        return (jnp.minimum(r, nt[0] - 1), 0)

    def out_map(r, ea, eb, nt):
        return (r, 0)

    wa_spec = pl.BlockSpec((None, d, f), lambda r, ea, eb, nt: (ea[r], 0, 0))
    wb_spec = pl.BlockSpec((None, d, f), lambda r, ea, eb, nt: (eb[r], 0, 0))
    grid_spec = pltpu.PrefetchScalarGridSpec(
        num_scalar_prefetch=3,
        grid=(r_max,),
        in_specs=[pl.BlockSpec((MOE_TILE, d), row_map),
                  pl.BlockSpec((MOE_TILE, LANES), row_map),
                  pl.BlockSpec((1, d), lambda r, ea, eb, nt: (0, 0)),
                  wa_spec, wa_spec, wb_spec, wb_spec],
        out_specs=pl.BlockSpec((MOE_TILE, 2 * f), out_map),
        scratch_shapes=[pltpu.VMEM((d, 2 * f), BF16),
                        pltpu.VMEM((d, 2 * f), BF16),
                        pltpu.SMEM((2,), jnp.int32)],
    )
    return pl.pallas_call(
        _expert_up_kernel,
        grid_spec=grid_spec,
        out_shape=jax.ShapeDtypeStruct((n_rows, 2 * f), BF16),
        compiler_params=_params(1),
        name="expert_up",
    )(tile_ea, tile_eb, n_tiles, xs, ws, g.reshape(1, d), w_gate, w_up, w_gate, w_up)


def _expert_down_kernel(ea_ref, eb_ref, nt_ref, hid_ref, xs_ref, g_ref, wda_ref, wdb_ref,
                        ys_ref, wdbf_ref, cur_ref):
    r = pl.program_id(0)
    f = wda_ref.shape[0]

    @pl.when(r == 0)
    def _():
        cur_ref[0] = -1
        cur_ref[1] = -1

    @pl.when(r >= nt_ref[0])
    def _():
        ys_ref[...] = jnp.zeros_like(ys_ref)

    @pl.when(r < nt_ref[0])
    def _():
        ea = ea_ref[r]
        eb = eb_ref[r]

        @pl.when(cur_ref[0] != ea)
        def _():
            wdbf_ref[0:f, :] = wda_ref[...].astype(BF16)
            cur_ref[0] = ea

        @pl.when(cur_ref[1] != eb)
        def _():
            wdbf_ref[f:2 * f, :] = wdb_ref[...].astype(BF16)
            cur_ref[1] = eb

        h2 = xs_ref[...] + _dot(hid_ref[...], wdbf_ref[...])
        ms = jnp.mean(h2 * h2, axis=-1, keepdims=True)
        ys_ref[...] = h2 * lax.rsqrt(ms + EPS) * g_ref[...]


def _expert_down(tile_ea, tile_eb, n_tiles, hid, xs, g, w_down):
    n_rows, d = xs.shape
    f = w_down.shape[1]
    r_max = n_rows // MOE_TILE

    def row_map(r, ea, eb, nt):
        return (jnp.minimum(r, nt[0] - 1), 0)

    def out_map(r, ea, eb, nt):
        return (r, 0)

    grid_spec = pltpu.PrefetchScalarGridSpec(
        num_scalar_prefetch=3,
        grid=(r_max,),
        in_specs=[pl.BlockSpec((MOE_TILE, 2 * f), row_map),
                  pl.BlockSpec((MOE_TILE, d), row_map),
                  pl.BlockSpec((1, d), lambda r, ea, eb, nt: (0, 0)),
                  pl.BlockSpec((None, f, d), lambda r, ea, eb, nt: (ea[r], 0, 0)),
                  pl.BlockSpec((None, f, d), lambda r, ea, eb, nt: (eb[r], 0, 0))],
        out_specs=pl.BlockSpec((MOE_TILE, d), out_map),
        scratch_shapes=[pltpu.VMEM((2 * f, d), BF16),
                        pltpu.SMEM((2,), jnp.int32)],
    )
    return pl.pallas_call(
        _expert_down_kernel,
        grid_spec=grid_spec,
        out_shape=jax.ShapeDtypeStruct((n_rows, d), F32),
        compiler_params=_params(1),
        name="expert_down",
    )(tile_ea, tile_eb, n_tiles, hid, xs, g.reshape(1, d), w_down, w_down)


_PAIR_LO = (0, 0, 0, 1, 1, 2)
_PAIR_HI = (1, 2, 3, 2, 3, 3)


def _routing_plan(meta, counts, t):
    cls = meta[0].astype(jnp.int32)
    rank = meta[1].astype(jnp.int32)
    cnt = counts[:N_CLASSES, 0].astype(jnp.int32)
    tiles_c = (cnt + MOE_TILE - 1) // MOE_TILE
    tile_end = jnp.cumsum(tiles_c)
    tile_start = tile_end - tiles_c
    row_start = tile_start * MOE_TILE
    pos = row_start[cls] + rank
    n_tiles = tile_end[-1]
    r_max = t // MOE_TILE + N_CLASSES
    r = jnp.minimum(jnp.arange(r_max, dtype=jnp.int32), n_tiles - 1)
    tile_cls = jnp.sum((tile_end[None, :] <= r[:, None]).astype(jnp.int32), axis=1)
    grp = tile_cls // N_PAIRS
    pair = tile_cls % N_PAIRS
    tile_ea = grp * EXPERTS_PER_GROUP + jnp.asarray(_PAIR_LO, jnp.int32)[pair]
    tile_eb = grp * EXPERTS_PER_GROUP + jnp.asarray(_PAIR_HI, jnp.int32)[pair]
    pad_lo = row_start + cnt
    pad_hi = tile_end * MOE_TILE
    wmeta = jnp.pad(meta[2:4].T, ((0, 0), (0, LANES - 2)))
    return pos, pad_lo, pad_hi, tile_ea, tile_eb, n_tiles.reshape(1), wmeta, r_max * MOE_TILE


def _layer(h, norm_mix_g, w_in, conv_w, sgu_ln_g, sgu_ln_b, sgu_w_s, sgu_b_s, w_up_conv,
           w_up_sgu, w_out, norm_ffn_g, w_rg, b_rg, w_re, b_re, w_eg, w_eu, w_ed, out_g):
    t, d = h.shape
    conv_width = conv_w.shape[1]
    sgu_width = sgu_ln_g.shape[0]
    xn = _rmsnorm(h, norm_mix_g)
    ya = _conv_branch(xn, w_in, conv_w, conv_width)
    gz = _zproj(xn, w_in, 3 * conv_width, 2 * sgu_width)
    yb = _sgu_mix(gz, sgu_ln_g, sgu_ln_b, sgu_w_s, sgu_b_s)
    m = _upgate(xn, ya, yb, w_in, 3 * conv_width + 2 * sgu_width, w_up_conv, w_up_sgu)
    h1 = _outproj(m, w_out, h)
    meta, counts = _router(h1, norm_ffn_g, w_rg, b_rg, w_re, b_re)
    pos, pad_lo, pad_hi, tile_ea, tile_eb, n_tiles, wmeta, n_rows = _routing_plan(meta, counts, t)
    xs, ws = _dispatch(pos, pad_lo, pad_hi, n_tiles, h1, wmeta, n_rows)
    hid = _expert_up(tile_ea, tile_eb, n_tiles, xs, ws, norm_ffn_g, w_eg, w_eu)
    ys = _expert_down(tile_ea, tile_eb, n_tiles, hid, xs, out_g, w_ed)
    return _combine(pos, ys, t)


def kernel(x, norm_mix_g, w_in, conv_w, sgu_ln_g, sgu_ln_b, sgu_w_s, sgu_b_s, w_up_conv, w_up_sgu, w_out, norm_ffn_g, w_router_group, b_router_group, w_router_expert, b_router_expert, w_exp_gate, w_exp_up, w_exp_down, norm_final_g):
    bsz, s, d = x.shape
    depth = w_in.shape[0]
    assert bsz == 1 and depth == 1, "causal conv carry and the fused final norm assume one sequence, one layer"
    out = _layer(x.reshape(s, d), norm_mix_g[0], w_in[0], conv_w[0], sgu_ln_g[0], sgu_ln_b[0],
                 sgu_w_s[0], sgu_b_s[0], w_up_conv[0], w_up_sgu[0], w_out[0], norm_ffn_g[0],
                 w_router_group[0], b_router_group[0], w_router_expert[0], b_router_expert[0],
                 w_exp_gate[0], w_exp_up[0], w_exp_down[0], norm_final_g)
    return out.reshape(bsz, s, d)
```

```python
import functools

import jax
import jax.numpy as jnp
from jax import lax
from jax.experimental import pallas as pl
from jax.experimental.pallas import tpu as pltpu

F32 = jnp.float32
BF16 = jnp.bfloat16

EPS = 1e-6
CHUNK = 64
CONV_K = 3
SGU_HEADS = 8
SGU_BLOCK = 128
N_GROUPS = 4
EXPERTS_PER_GROUP = 4
N_PAIRS = 6
N_CLASSES = N_GROUPS * N_PAIRS
CLASS_ROWS = 32
ROUTER_ROWS = 32
LANES = 128

VMEM_LIMIT_BYTES = 56 * 1024 * 1024

MOE_TILE = 128
DMA_CHUNK = 512
DMA_UNROLL = 8


def _params(n_axes):
    return pltpu.CompilerParams(
        dimension_semantics=("arbitrary",) * n_axes,
        vmem_limit_bytes=VMEM_LIMIT_BYTES)


def _dot(a, b):
    return jnp.dot(a, b, preferred_element_type=F32)


def _load_token_major(ref, base, n_tok, slabs):
    return jnp.concatenate(
        [ref[pl.ds(base + c, n_tok, stride=slabs), :] for c in range(slabs)], axis=1)


def _store_token_major(ref, val, slabs):
    n_tok = val.shape[0]
    for c in range(slabs):
        ref[pl.ds(c, n_tok, stride=slabs), :] = val[:, c * LANES:(c + 1) * LANES]


def _rmsnorm_kernel(x_ref, g_ref, o_ref):
    x = x_ref[...]
    ms = jnp.mean(x * x, axis=-1, keepdims=True)
    o_ref[...] = (x * lax.rsqrt(ms + EPS) * g_ref[...]).astype(o_ref.dtype)


def _rmsnorm(x, g, tm=512):
    t, d = x.shape
    return pl.pallas_call(
        _rmsnorm_kernel,
        grid=(t // tm,),
        in_specs=[pl.BlockSpec((tm, d), lambda i: (i, 0)),
                  pl.BlockSpec((1, d), lambda i: (0, 0))],
        out_specs=pl.BlockSpec((tm, d), lambda i: (i, 0)),
        out_shape=jax.ShapeDtypeStruct((t, d), BF16),
        compiler_params=_params(1),
        name="rmsnorm",
    )(x, g.reshape(1, d))


def _conv_kernel(xn_ref, wb_ref, wc_ref, wh_ref, cw_ref, o_ref, wbf_ref, carry_ref):
    i = pl.program_id(1)
    tn = wb_ref.shape[1]
    tm = xn_ref.shape[0]

    @pl.when(i == 0)
    def _():
        wbf_ref[:, 0:tn] = wb_ref[...].astype(BF16)
        wbf_ref[:, tn:2 * tn] = wc_ref[...].astype(BF16)
        wbf_ref[:, 2 * tn:3 * tn] = wh_ref[...].astype(BF16)
        carry_ref[...] = jnp.zeros_like(carry_ref)

    proj = _dot(xn_ref[...], wbf_ref[...])
    b = proj[:, 0:tn]
    p = proj[:, tn:2 * tn] * proj[:, 2 * tn:3 * tn]
    prev = carry_ref[...]
    carry_ref[...] = p[tm - 8:tm, :]
    row = lax.broadcasted_iota(jnp.int32, p.shape, 0)
    p1 = jnp.where(row == 0, prev[7:8, :], pltpu.roll(p, 1, axis=0))
    p2 = jnp.where(row == 0, prev[6:7, :],
                   jnp.where(row == 1, prev[7:8, :], pltpu.roll(p, 2, axis=0)))
    cw = cw_ref[...]
    y = b * (cw[0:1, :] * p2 + cw[1:2, :] * p1 + cw[2:3, :] * p)
    o_ref[...] = y.astype(o_ref.dtype)


def _conv_branch(xn, w_in, conv_w, width, tm=1024, tn=256):
    t, d = xn.shape
    nj = width // tn
    return pl.pallas_call(
        _conv_kernel,
        grid=(nj, t // tm),
        in_specs=[pl.BlockSpec((tm, d), lambda j, i: (i, 0)),
                  pl.BlockSpec((d, tn), lambda j, i: (0, j)),
                  pl.BlockSpec((d, tn), lambda j, i: (0, nj + j)),
                  pl.BlockSpec((d, tn), lambda j, i: (0, 2 * nj + j)),
                  pl.BlockSpec((CONV_K, tn), lambda j, i: (0, j))],
        out_specs=pl.BlockSpec((tm, tn), lambda j, i: (i, j)),
        out_shape=jax.ShapeDtypeStruct((t, width), BF16),
        scratch_shapes=[pltpu.VMEM((d, 3 * tn), BF16),
                        pltpu.VMEM((8, tn), F32)],
        compiler_params=_params(2),
        name="conv_branch",
    )(xn, w_in, w_in, w_in, conv_w)


def _zproj_kernel(xn_ref, w_ref, o_ref, wbf_ref):
    @pl.when(pl.program_id(1) == 0)
    def _():
        wbf_ref[...] = w_ref[...].astype(BF16)

    z = _dot(xn_ref[...], wbf_ref[...])
    o_ref[...] = jax.nn.gelu(z).astype(o_ref.dtype)


def _zproj(xn, w_in, col0, width, tm=1024, tn=512):
    t, d = xn.shape
    c0 = col0 // tn
    return pl.pallas_call(
        _zproj_kernel,
        grid=(width // tn, t // tm),
        in_specs=[pl.BlockSpec((tm, d), lambda j, i: (i, 0)),
                  pl.BlockSpec((d, tn), lambda j, i: (0, c0 + j))],
        out_specs=pl.BlockSpec((tm, tn), lambda j, i: (i, j)),
        out_shape=jax.ShapeDtypeStruct((t, width), BF16),
        scratch_shapes=[pltpu.VMEM((d, tn), BF16)],
        compiler_params=_params(2),
        name="sgu_zproj",
    )(xn, w_in)


def _sgu_kernel(gz_ref, lng_ref, lnb_ref, ws_ref, bsx_ref, o_ref):
    tm = gz_ref.shape[0]
    w = o_ref.shape[1]
    hd = w // SGU_HEADS
    v = gz_ref[:, w:2 * w].astype(F32)
    mu = jnp.mean(v, axis=-1, keepdims=True)
    vc = v - mu
    var = jnp.mean(vc * vc, axis=-1, keepdims=True)
    vn = (vc * lax.rsqrt(var + EPS) * lng_ref[...] + lnb_ref[...]).astype(BF16)
    ii = lax.broadcasted_iota(jnp.int32, (SGU_BLOCK, SGU_BLOCK), 0)
    jj = lax.broadcasted_iota(jnp.int32, (SGU_BLOCK, SGU_BLOCK), 1)
    mask = (jj // CHUNK) <= (ii // CHUNK)
    for h in range(SGU_HEADS):
        wm = jnp.where(mask, ws_ref[h], 0.0).astype(BF16)
        cs = slice(h * hd, (h + 1) * hd)
        for n in range(tm // SGU_BLOCK):
            rs = slice(n * SGU_BLOCK, (n + 1) * SGU_BLOCK)
            vm = _dot(wm, vn[rs, cs]) + bsx_ref[:, cs]
            o_ref[rs, cs] = (gz_ref[rs, cs].astype(F32) * vm).astype(o_ref.dtype)


def _sgu_mix(gz, ln_g, ln_b, w_s, b_s, tm=512):
    t, w2 = gz.shape
    w = w2 // 2
    hd = w // SGU_HEADS
    bsx = jnp.repeat(b_s.T, hd, axis=1)
    return pl.pallas_call(
        _sgu_kernel,
        grid=(t // tm,),
        in_specs=[pl.BlockSpec((tm, w2), lambda i: (i, 0)),
                  pl.BlockSpec((1, w), lambda i: (0, 0)),
                  pl.BlockSpec((1, w), lambda i: (0, 0)),
                  pl.BlockSpec((SGU_HEADS, SGU_BLOCK, SGU_BLOCK), lambda i: (0, 0, 0)),
                  pl.BlockSpec((SGU_BLOCK, w), lambda i: (0, 0))],
        out_specs=pl.BlockSpec((tm, w), lambda i: (i, 0)),
        out_shape=jax.ShapeDtypeStruct((t, w), BF16),
        compiler_params=_params(1),
        name="sgu_mix",
    )(gz, ln_g.reshape(1, w), ln_b.reshape(1, w), w_s, bsx)


def _upgate_kernel(xn_ref, ya_ref, yb_ref, wgc_ref, wgs_ref, wua_ref, wub_ref, o_ref,
                   wgbf_ref, wuabf_ref, wubbf_ref):
    tn = wgc_ref.shape[1]

    @pl.when(pl.program_id(1) == 0)
    def _():
        wgbf_ref[:, 0:tn] = wgc_ref[...].astype(BF16)
        wgbf_ref[:, tn:2 * tn] = wgs_ref[...].astype(BF16)
        wuabf_ref[...] = wua_ref[...].astype(BF16)
        wubbf_ref[...] = wub_ref[...].astype(BF16)

    gl = _dot(xn_ref[...], wgbf_ref[...])
    a = _dot(ya_ref[...], wuabf_ref[...])
    b = _dot(yb_ref[...], wubbf_ref[...])
    m = jax.nn.sigmoid(gl[:, 0:tn]) * a + jax.nn.sigmoid(gl[:, tn:2 * tn]) * b
    o_ref[...] = m.astype(o_ref.dtype)


def _upgate(xn, ya, yb, w_in, gate_col0, w_up_a, w_up_b, tm=512, tn=512):
    t, d = xn.shape
    wa = ya.shape[1]
    wb = yb.shape[1]
    dout = w_up_a.shape[1]
    c0 = gate_col0 // tn
    nj = dout // tn
    return pl.pallas_call(
        _upgate_kernel,
        grid=(nj, t // tm),
        in_specs=[pl.BlockSpec((tm, d), lambda j, i: (i, 0)),
                  pl.BlockSpec((tm, wa), lambda j, i: (i, 0)),
                  pl.BlockSpec((tm, wb), lambda j, i: (i, 0)),
                  pl.BlockSpec((d, tn), lambda j, i: (0, c0 + j)),
                  pl.BlockSpec((d, tn), lambda j, i: (0, c0 + nj + j)),
                  pl.BlockSpec((wa, tn), lambda j, i: (0, j)),
                  pl.BlockSpec((wb, tn), lambda j, i: (0, j))],
        out_specs=pl.BlockSpec((tm, tn), lambda j, i: (i, j)),
        out_shape=jax.ShapeDtypeStruct((t, dout), BF16),
        scratch_shapes=[pltpu.VMEM((d, 2 * tn), BF16),
                        pltpu.VMEM((wa, tn), BF16),
                        pltpu.VMEM((wb, tn), BF16)],
        compiler_params=_params(2),
        name="upgate",
    )(xn, ya, yb, w_in, w_in, w_up_a, w_up_b)


def _outproj_kernel(m_ref, w_ref, x_ref, o_ref, wbf_ref):
    @pl.when(pl.program_id(1) == 0)
    def _():
        wbf_ref[...] = w_ref[...].astype(BF16)

    val = x_ref[...] + _dot(m_ref[...], wbf_ref[...])
    for k in range(o_ref.shape[1]):
        o_ref[:, k, :] = val[:, k * LANES:(k + 1) * LANES]


def _outproj(m, w_out, x, tm=512, tn=1024):
    t, d = m.shape
    dout = w_out.shape[1]
    sl = tn // LANES
    assert sl == 8, "each column block must fill whole (8,128) tiles per token"
    out = pl.pallas_call(
        _outproj_kernel,
        grid=(dout // tn, t // tm),
        in_specs=[pl.BlockSpec((tm, d), lambda j, i: (i, 0)),
                  pl.BlockSpec((d, tn), lambda j, i: (0, j)),
                  pl.BlockSpec((tm, tn), lambda j, i: (i, j))],
        out_specs=pl.BlockSpec((tm, None, sl, LANES), lambda j, i: (i, j, 0, 0)),
        out_shape=jax.ShapeDtypeStruct((t, dout // tn, sl, LANES), F32),
        scratch_shapes=[pltpu.VMEM((d, tn), BF16)],
        compiler_params=_params(2),
        name="outproj",
    )(m, w_out, x)
    return out.reshape(t * (dout // LANES), LANES)


def _argmax_rows(rows):
    best = rows[0]
    idx = jnp.zeros(rows[0].shape, jnp.int32)
    for k in range(1, len(rows)):
        better = rows[k] > best
        best = jnp.where(better, rows[k], best)
        idx = jnp.where(better, k, idx)
    return best, idx


def _softmax_rows(rows):
    mx = functools.reduce(jnp.maximum, rows)
    ex = [jnp.exp(r - mx) for r in rows]
    den = functools.reduce(lambda a, b: a + b, ex)
    return [e / den for e in ex]


def _router_kernel(h_ref, g_ref, wr_ref, br_ref, meta_ref, cnt_ref, carry_ref):
    i = pl.program_id(0)
    slabs = g_ref.shape[1] // LANES
    tm = h_ref.shape[0] // slabs

    @pl.when(i == 0)
    def _():
        carry_ref[...] = jnp.zeros_like(carry_ref)

    x = _load_token_major(h_ref, 0, tm, slabs)
    ms = jnp.mean(x * x, axis=-1, keepdims=True)
    xn = x * lax.rsqrt(ms + EPS) * g_ref[...]
    lt = lax.dot_general(wr_ref[...], xn, (((1,), (1,)), ((), ())),
                         precision=lax.Precision.HIGHEST,
                         preferred_element_type=F32) + br_ref[...]
    pgs = _softmax_rows([lt[k:k + 1, :] for k in range(N_GROUPS)])
    pg, gi = _argmax_rows(pgs)
    sel = []
    for k in range(EXPERTS_PER_GROUP):
        r = jnp.zeros_like(pg)
        for g in range(N_GROUPS):
            row = N_GROUPS + g * EXPERTS_PER_GROUP + k
            r = jnp.where(gi == g, lt[row:row + 1, :], r)
        sel.append(r)
    pes = _softmax_rows(sel)
    p1, e1 = _argmax_rows(pes)
    rest = [jnp.where(e1 == k, -1.0, pes[k]) for k in range(EXPERTS_PER_GROUP)]
    p2, e2 = _argmax_rows(rest)
    den = p1 + p2
    w1 = pg * (p1 / den)
    w2 = pg * (p2 / den)
    lo = jnp.minimum(e1, e2)
    hi = jnp.maximum(e1, e2)
    w_lo = jnp.where(e1 < e2, w1, w2)
    w_hi = jnp.where(e1 < e2, w2, w1)
    pair = jnp.where(lo == 0, hi - 1, jnp.where(lo == 1, hi + 1, 5))
    cls = gi * N_PAIRS + pair

    crow = lax.broadcasted_iota(jnp.int32, (CLASS_ROWS, tm), 0)
    oh = crow == cls
    a = lax.broadcasted_iota(jnp.int32, (tm, tm), 0)
    b = lax.broadcasted_iota(jnp.int32, (tm, tm), 1)
    before = (a < b).astype(BF16)
    cum = _dot(oh.astype(BF16), before)
    base = carry_ref[...][:, 0:1]
    ohf = oh.astype(F32)
    rank = jnp.sum(ohf * (cum + base), axis=0, keepdims=True)
    total = base + jnp.sum(ohf, axis=1, keepdims=True)
    carry_ref[...] = jnp.broadcast_to(total, carry_ref.shape)
    cnt_ref[...] = jnp.broadcast_to(total, cnt_ref.shape)

    meta_ref[0:1, :] = cls.astype(F32)
    meta_ref[1:2, :] = rank
    meta_ref[2:3, :] = w_lo
    meta_ref[3:4, :] = w_hi
    meta_ref[4:8, :] = jnp.zeros((4, tm), F32)


def _router(h1tm, g, w_rg, b_rg, w_re, b_re, tm=512):
    d = g.shape[0]
    slabs = d // LANES
    t = h1tm.shape[0] // slabs
    n_log = w_rg.shape[1] + w_re.shape[1]
    wr = jnp.concatenate([w_rg, w_re], axis=1).T
    wr = jnp.pad(wr, ((0, ROUTER_ROWS - n_log), (0, 0)))
    br = jnp.pad(jnp.concatenate([b_rg, b_re]), (0, ROUTER_ROWS - n_log)).reshape(ROUTER_ROWS, 1)
    return pl.pallas_call(
        _router_kernel,
        grid=(t // tm,),
        in_specs=[pl.BlockSpec((tm * slabs, LANES), lambda i: (i, 0)),
                  pl.BlockSpec((1, d), lambda i: (0, 0)),
                  pl.BlockSpec((ROUTER_ROWS, d), lambda i: (0, 0)),
                  pl.BlockSpec((ROUTER_ROWS, 1), lambda i: (0, 0))],
        out_specs=[pl.BlockSpec((8, tm), lambda i: (0, i)),
                   pl.BlockSpec((CLASS_ROWS, LANES), lambda i: (0, 0))],
        out_shape=[jax.ShapeDtypeStruct((8, t), F32),
                   jax.ShapeDtypeStruct((CLASS_ROWS, LANES), F32)],
        scratch_shapes=[pltpu.VMEM((CLASS_ROWS, LANES), F32)],
        compiler_params=_params(1),
        name="router",
    )(h1tm, g.reshape(1, d), wr, br)


def _rows_copy(src_ref, s_row, dst_ref, d_row, n, sem):
    def aligned(row):
        return row if isinstance(row, int) or n == 1 else pl.multiple_of(row, n)

    return pltpu.make_async_copy(src_ref.at[pl.ds(aligned(s_row), n), :],
                                 dst_ref.at[pl.ds(aligned(d_row), n), :], sem)


def _dispatch_kernel(pos_ref, pad_lo_ref, pad_hi_ref, nt_ref, h_ref, wm_ref, xs_ref, ws_ref,
                     zrow_ref, zmeta_ref, sem, pad_sem):
    c = pl.program_id(0)
    slabs = h_ref.shape[0] // DMA_CHUNK
    tile_rows = MOE_TILE * slabs

    def issue(k, carry):
        for u in range(DMA_UNROLL):
            r = k * DMA_UNROLL + u
            p = pos_ref[c * DMA_CHUNK + r]
            _rows_copy(h_ref, r * slabs, xs_ref, p * slabs, slabs, sem).start()
            _rows_copy(wm_ref, r, ws_ref, p, 1, sem).start()
        return carry

    lax.fori_loop(0, DMA_CHUNK // DMA_UNROLL, issue, 0)

    def pad_copies(r):
        return (_rows_copy(zrow_ref, 0, xs_ref, r * slabs, slabs, pad_sem),
                _rows_copy(zmeta_ref, 0, ws_ref, r, 1, pad_sem))

    def tail_copies(r):
        return (pltpu.make_async_copy(zrow_ref, xs_ref.at[pl.ds(r * tile_rows, tile_rows), :], pad_sem),
                pltpu.make_async_copy(zmeta_ref, ws_ref.at[pl.ds(r * MOE_TILE, MOE_TILE), :], pad_sem))

    def start_all(make):
        def body(r, carry):
            for cp in make(r):
                cp.start()
            return carry
        return body

    def wait_all(make):
        def body(r, carry):
            for cp in make(r):
                cp.wait()
            return carry
        return body

    r_max = ws_ref.shape[0] // MOE_TILE

    @pl.when(c == 0)
    def _():
        zrow_ref[...] = jnp.zeros_like(zrow_ref)
        zmeta_ref[...] = jnp.zeros_like(zmeta_ref)
        for k in range(N_CLASSES):
            lax.fori_loop(pad_lo_ref[k], pad_hi_ref[k], start_all(pad_copies), 0)
        lax.fori_loop(nt_ref[0], r_max, start_all(tail_copies), 0)

    pltpu.make_async_copy(h_ref, xs_ref.at[pl.ds(0, DMA_CHUNK * slabs), :], sem).wait()
    pltpu.make_async_copy(wm_ref, ws_ref.at[pl.ds(0, DMA_CHUNK), :], sem).wait()

    @pl.when(c == 0)
    def _():
        for k in range(N_CLASSES):
            lax.fori_loop(pad_lo_ref[k], pad_hi_ref[k], wait_all(pad_copies), 0)
        lax.fori_loop(nt_ref[0], r_max, wait_all(tail_copies), 0)


def _dispatch(pos, pad_lo, pad_hi, n_tiles, h1tm, wmeta, n_rows):
    t = wmeta.shape[0]
    slabs = h1tm.shape[0] // t
    any_spec = pl.BlockSpec(memory_space=pl.ANY)
    grid_spec = pltpu.PrefetchScalarGridSpec(
        num_scalar_prefetch=4,
        grid=(t // DMA_CHUNK,),
        in_specs=[pl.BlockSpec((DMA_CHUNK * slabs, LANES), lambda c, *_: (c, 0)),
                  pl.BlockSpec((DMA_CHUNK, LANES), lambda c, *_: (c, 0))],
        out_specs=[any_spec, any_spec],
        scratch_shapes=[pltpu.VMEM((MOE_TILE * slabs, LANES), F32),
                        pltpu.VMEM((MOE_TILE, LANES), F32),
                        pltpu.SemaphoreType.DMA(()),
                        pltpu.SemaphoreType.DMA(())],
    )
    return pl.pallas_call(
        _dispatch_kernel,
        grid_spec=grid_spec,
        out_shape=[jax.ShapeDtypeStruct((n_rows * slabs, LANES), F32),
                   jax.ShapeDtypeStruct((n_rows, LANES), F32)],
        compiler_params=_params(1),
        name="dispatch",
    )(pos, pad_lo, pad_hi, n_tiles, h1tm, wmeta)


def _combine_kernel(pos_ref, ys_ref, o_ref, stage_ref, sems):
    c = pl.program_id(0)
    nc = pl.num_programs(0)
    slabs = o_ref.shape[1] // LANES
    rows = DMA_CHUNK * slabs

    def issue(chunk, slot):
        def body(k, carry):
            for u in range(DMA_UNROLL):
                r = k * DMA_UNROLL + u
                p = pos_ref[chunk * DMA_CHUNK + r]
                pltpu.make_async_copy(
                    ys_ref.at[pl.ds(pl.multiple_of(p * slabs, slabs), slabs), :],
                    stage_ref.at[pl.ds(pl.multiple_of(slot * rows + r * slabs, slabs), slabs), :],
                    sems.at[slot]).start()
            return carry

        lax.fori_loop(0, DMA_CHUNK // DMA_UNROLL, body, 0)

    @pl.when(c == 0)
    def _():
        issue(0, 0)

    @pl.when(c + 1 < nc)
    def _():
        issue(c + 1, (c + 1) % 2)

    slot = c % 2
    base = pl.multiple_of(slot * rows, rows)
    pltpu.make_async_copy(ys_ref.at[pl.ds(0, rows), :], stage_ref.at[pl.ds(base, rows), :],
                          sems.at[slot]).wait()
    o_ref[...] = _load_token_major(stage_ref, base, DMA_CHUNK, slabs)


def _combine(pos, ys, t, d):
    slabs = d // LANES
    grid_spec = pltpu.PrefetchScalarGridSpec(
        num_scalar_prefetch=1,
        grid=(t // DMA_CHUNK,),
        in_specs=[pl.BlockSpec(memory_space=pl.ANY)],
        out_specs=pl.BlockSpec((DMA_CHUNK, d), lambda c, pos: (c, 0)),
        scratch_shapes=[pltpu.VMEM((2 * DMA_CHUNK * slabs, LANES), F32),
                        pltpu.SemaphoreType.DMA((2,))],
    )
    return pl.pallas_call(
        _combine_kernel,
        grid_spec=grid_spec,
        out_shape=jax.ShapeDtypeStruct((t, d), F32),
        compiler_params=_params(1),
        name="combine",
    )(pos, ys)


def _expert_up_kernel(ea_ref, eb_ref, nt_ref, xs_ref, ws_ref, g_ref,
                      wga_ref, wua_ref, wgb_ref, wub_ref, hid_ref,
                      wabf_ref, wbbf_ref, cur_ref):
    r = pl.program_id(0)
    f = wga_ref.shape[1]

    @pl.when(r == 0)
    def _():
        cur_ref[0] = -1
        cur_ref[1] = -1

    @pl.when(r >= nt_ref[0])
    def _():
        hid_ref[...] = jnp.zeros_like(hid_ref)

    @pl.when(r < nt_ref[0])
    def _():
        ea = ea_ref[r]
        eb = eb_ref[r]

        @pl.when(cur_ref[0] != ea)
        def _():
            wabf_ref[:, 0:f] = wga_ref[...].astype(BF16)
            wabf_ref[:, f:2 * f] = wua_ref[...].astype(BF16)
            cur_ref[0] = ea

        @pl.when(cur_ref[1] != eb)
        def _():
            wbbf_ref[:, 0:f] = wgb_ref[...].astype(BF16)
            wbbf_ref[:, f:2 * f] = wub_ref[...].astype(BF16)
            cur_ref[1] = eb

        slabs = g_ref.shape[1] // LANES
        x = _load_token_major(xs_ref, 0, xs_ref.shape[0] // slabs, slabs)
        ms = jnp.mean(x * x, axis=-1, keepdims=True)
        xn = (x * lax.rsqrt(ms + EPS) * g_ref[...]).astype(BF16)
        wts = ws_ref[...]
        ha = _dot(xn, wabf_ref[...])
        hid_ref[:, 0:f] = (jax.nn.silu(ha[:, 0:f]) * ha[:, f:2 * f] * wts[:, 0:1]).astype(hid_ref.dtype)
        hb = _dot(xn, wbbf_ref[...])
        hid_ref[:, f:2 * f] = (jax.nn.silu(hb[:, 0:f]) * hb[:, f:2 * f] * wts[:, 1:2]).astype(hid_ref.dtype)


def _expert_up(tile_ea, tile_eb, n_tiles, xs, ws, g, w_gate, w_up):
    n_rows = ws.shape[0]
    d = g.shape[0]
    slabs = d // LANES
    f = w_gate.shape[2]
    r_max = n_rows // MOE_TILE

    def row_map(r, ea, eb, nt):
        return (jnp.minimum(r, nt[0] - 1), 0)

    def out_map(r, ea, eb, nt):
        return (r, 0)

    wa_spec = pl.BlockSpec((None, d, f), lambda r, ea, eb, nt: (ea[r], 0, 0))
    wb_spec = pl.BlockSpec((None, d, f), lambda r, ea, eb, nt: (eb[r], 0, 0))
    grid_spec = pltpu.PrefetchScalarGridSpec(
        num_scalar_prefetch=3,
        grid=(r_max,),
        in_specs=[pl.BlockSpec((MOE_TILE * slabs, LANES), row_map),
                  pl.BlockSpec((MOE_TILE, LANES), row_map),
                  pl.BlockSpec((1, d), lambda r, ea, eb, nt: (0, 0)),
                  wa_spec, wa_spec, wb_spec, wb_spec],
        out_specs=pl.BlockSpec((MOE_TILE, 2 * f), out_map),
        scratch_shapes=[pltpu.VMEM((d, 2 * f), BF16),
                        pltpu.VMEM((d, 2 * f), BF16),
                        pltpu.SMEM((2,), jnp.int32)],
    )
    return pl.pallas_call(
        _expert_up_kernel,
        grid_spec=grid_spec,
        out_shape=jax.ShapeDtypeStruct((n_rows, 2 * f), BF16),
        compiler_params=_params(1),
        name="expert_up",
    )(tile_ea, tile_eb, n_tiles, xs, ws, g.reshape(1, d), w_gate, w_up, w_gate, w_up)


def _expert_down_kernel(ea_ref, eb_ref, nt_ref, hid_ref, xs_ref, g_ref, wda_ref, wdb_ref,
                        ys_ref, wdbf_ref, cur_ref):
    r = pl.program_id(0)
    f = wda_ref.shape[0]

    @pl.when(r == 0)
    def _():
        cur_ref[0] = -1
        cur_ref[1] = -1

    @pl.when(r >= nt_ref[0])
    def _():
        ys_ref[...] = jnp.zeros_like(ys_ref)

    @pl.when(r < nt_ref[0])
    def _():
        ea = ea_ref[r]
        eb = eb_ref[r]

        @pl.when(cur_ref[0] != ea)
        def _():
            wdbf_ref[0:f, :] = wda_ref[...].astype(BF16)
            cur_ref[0] = ea

        @pl.when(cur_ref[1] != eb)
        def _():
            wdbf_ref[f:2 * f, :] = wdb_ref[...].astype(BF16)
            cur_ref[1] = eb

        slabs = g_ref.shape[1] // LANES
        x = _load_token_major(xs_ref, 0, hid_ref.shape[0], slabs)
        h2 = x + _dot(hid_ref[...], wdbf_ref[...])
        ms = jnp.mean(h2 * h2, axis=-1, keepdims=True)
        _store_token_major(ys_ref, h2 * lax.rsqrt(ms + EPS) * g_ref[...], slabs)


def _expert_down(tile_ea, tile_eb, n_tiles, hid, xs, g, w_down):
    n_rows = hid.shape[0]
    d = g.shape[0]
    slabs = d // LANES
    f = w_down.shape[1]
    r_max = n_rows // MOE_TILE

    def row_map(r, ea, eb, nt):
        return (jnp.minimum(r, nt[0] - 1), 0)

    def out_map(r, ea, eb, nt):
        return (r, 0)

    grid_spec = pltpu.PrefetchScalarGridSpec(
        num_scalar_prefetch=3,
        grid=(r_max,),
        in_specs=[pl.BlockSpec((MOE_TILE, 2 * f), row_map),
                  pl.BlockSpec((MOE_TILE * slabs, LANES), row_map),
                  pl.BlockSpec((1, d), lambda r, ea, eb, nt: (0, 0)),
                  pl.BlockSpec((None, f, d), lambda r, ea, eb, nt: (ea[r], 0, 0)),
                  pl.BlockSpec((None, f, d), lambda r, ea, eb, nt: (eb[r], 0, 0))],
        out_specs=pl.BlockSpec((MOE_TILE * slabs, LANES), out_map),
        scratch_shapes=[pltpu.VMEM((2 * f, d), BF16),
                        pltpu.SMEM((2,), jnp.int32)],
    )
    return pl.pallas_call(
        _expert_down_kernel,
        grid_spec=grid_spec,
        out_shape=jax.ShapeDtypeStruct((n_rows * slabs, LANES), F32),
        compiler_params=_params(1),
        name="expert_down",
    )(tile_ea, tile_eb, n_tiles, hid, xs, g.reshape(1, d), w_down, w_down)


_PAIR_LO = (0, 0, 0, 1, 1, 2)
_PAIR_HI = (1, 2, 3, 2, 3, 3)


def _routing_plan(meta, counts, t):
    cls = meta[0].astype(jnp.int32)
    rank = meta[1].astype(jnp.int32)
    cnt = counts[:N_CLASSES, 0].astype(jnp.int32)
    tiles_c = (cnt + MOE_TILE - 1) // MOE_TILE
    tile_end = jnp.cumsum(tiles_c)
    tile_start = tile_end - tiles_c
    row_start = tile_start * MOE_TILE
    pos = row_start[cls] + rank
    n_tiles = tile_end[-1]
    r_max = t // MOE_TILE + N_CLASSES
    r = jnp.minimum(jnp.arange(r_max, dtype=jnp.int32), n_tiles - 1)
    tile_cls = jnp.sum((tile_end[None, :] <= r[:, None]).astype(jnp.int32), axis=1)
    grp = tile_cls // N_PAIRS
    pair = tile_cls % N_PAIRS
    tile_ea = grp * EXPERTS_PER_GROUP + jnp.asarray(_PAIR_LO, jnp.int32)[pair]
    tile_eb = grp * EXPERTS_PER_GROUP + jnp.asarray(_PAIR_HI, jnp.int32)[pair]
    pad_lo = row_start + cnt
    pad_hi = tile_end * MOE_TILE
    wmeta = jnp.pad(meta[2:4].T, ((0, 0), (0, LANES - 2)))
    return pos, pad_lo, pad_hi, tile_ea, tile_eb, n_tiles.reshape(1), wmeta, r_max * MOE_TILE


def _layer(h, norm_mix_g, w_in, conv_w, sgu_ln_g, sgu_ln_b, sgu_w_s, sgu_b_s, w_up_conv,
           w_up_sgu, w_out, norm_ffn_g, w_rg, b_rg, w_re, b_re, w_eg, w_eu, w_ed, out_g):
    t, d = h.shape
    conv_width = conv_w.shape[1]
    sgu_width = sgu_ln_g.shape[0]
    xn = _rmsnorm(h, norm_mix_g)
    ya = _conv_branch(xn, w_in, conv_w, conv_width)
    gz = _zproj(xn, w_in, 3 * conv_width, 2 * sgu_width)
    yb = _sgu_mix(gz, sgu_ln_g, sgu_ln_b, sgu_w_s, sgu_b_s)
    m = _upgate(xn, ya, yb, w_in, 3 * conv_width + 2 * sgu_width, w_up_conv, w_up_sgu)
    h1 = _outproj(m, w_out, h)
    meta, counts = _router(h1, norm_ffn_g, w_rg, b_rg, w_re, b_re)
    pos, pad_lo, pad_hi, tile_ea, tile_eb, n_tiles, wmeta, n_rows = _routing_plan(meta, counts, t)
    xs, ws = _dispatch(pos, pad_lo, pad_hi, n_tiles, h1, wmeta, n_rows)
    hid = _expert_up(tile_ea, tile_eb, n_tiles, xs, ws, norm_ffn_g, w_eg, w_eu)
    ys = _expert_down(tile_ea, tile_eb, n_tiles, hid, xs, out_g, w_ed)
    return _combine(pos, ys, t, d)


def kernel(x, norm_mix_g, w_in, conv_w, sgu_ln_g, sgu_ln_b, sgu_w_s, sgu_b_s, w_up_conv, w_up_sgu, w_out, norm_ffn_g, w_router_group, b_router_group, w_router_expert, b_router_expert, w_exp_gate, w_exp_up, w_exp_down, norm_final_g):
    bsz, s, d = x.shape
    depth = w_in.shape[0]
    assert bsz == 1 and depth == 1, "causal conv carry and the fused final norm assume one sequence, one layer"
    out = _layer(x.reshape(s, d), norm_mix_g[0], w_in[0], conv_w[0], sgu_ln_g[0], sgu_ln_b[0],
                 sgu_w_s[0], sgu_b_s[0], w_up_conv[0], w_up_sgu[0], w_out[0], norm_ffn_g[0],
                 w_router_group[0], b_router_group[0], w_router_expert[0], b_router_expert[0],
                 w_exp_gate[0], w_exp_up[0], w_exp_down[0], norm_final_g)
    return out.reshape(bsz, s, d)
```

```python
import functools

import jax
import jax.numpy as jnp
from jax import lax
from jax.experimental import pallas as pl
from jax.experimental.pallas import tpu as pltpu

F32 = jnp.float32
BF16 = jnp.bfloat16

EPS = 1e-6
CHUNK = 64
CONV_K = 3
SGU_HEADS = 8
SGU_BLOCK = 128
N_GROUPS = 4
EXPERTS_PER_GROUP = 4
N_EXPERTS = N_GROUPS * EXPERTS_PER_GROUP
ROUTER_ROWS = 32
LANES = 128

VMEM_LIMIT_BYTES = 56 * 1024 * 1024

MOE_TILE = 256
DMA_CHUNK = 512
DMA_UNROLL = 8


def _params(n_axes):
    return pltpu.CompilerParams(
        dimension_semantics=("arbitrary",) * n_axes,
        vmem_limit_bytes=VMEM_LIMIT_BYTES)


def _dot(a, b):
    return jnp.dot(a, b, preferred_element_type=F32)


def _load_token_major(ref, base, n_tok, slabs):
    return jnp.concatenate(
        [ref[pl.ds(base + c, n_tok, stride=slabs), :] for c in range(slabs)], axis=1)


def _store_token_major(ref, val, slabs):
    n_tok = val.shape[0]
    for c in range(slabs):
        ref[pl.ds(c, n_tok, stride=slabs), :] = val[:, c * LANES:(c + 1) * LANES]


def _rmsnorm_kernel(x_ref, g_ref, o_ref):
    x = x_ref[...]
    ms = jnp.mean(x * x, axis=-1, keepdims=True)
    o_ref[...] = (x * lax.rsqrt(ms + EPS) * g_ref[...]).astype(o_ref.dtype)


def _rmsnorm(x, g, tm=512):
    t, d = x.shape
    return pl.pallas_call(
        _rmsnorm_kernel,
        grid=(t // tm,),
        in_specs=[pl.BlockSpec((tm, d), lambda i: (i, 0)),
                  pl.BlockSpec((1, d), lambda i: (0, 0))],
        out_specs=pl.BlockSpec((tm, d), lambda i: (i, 0)),
        out_shape=jax.ShapeDtypeStruct((t, d), BF16),
        compiler_params=_params(1),
        name="rmsnorm",
    )(x, g.reshape(1, d))


def _conv_kernel(xn_ref, wb_ref, wc_ref, wh_ref, cw_ref, o_ref, wbf_ref, carry_ref):
    i = pl.program_id(1)
    tn = wb_ref.shape[1]
    tm = xn_ref.shape[0]

    @pl.when(i == 0)
    def _():
        wbf_ref[:, 0:tn] = wb_ref[...].astype(BF16)
        wbf_ref[:, tn:2 * tn] = wc_ref[...].astype(BF16)
        wbf_ref[:, 2 * tn:3 * tn] = wh_ref[...].astype(BF16)
        carry_ref[...] = jnp.zeros_like(carry_ref)

    proj = _dot(xn_ref[...], wbf_ref[...])
    b = proj[:, 0:tn]
    p = proj[:, tn:2 * tn] * proj[:, 2 * tn:3 * tn]
    prev = carry_ref[...]
    carry_ref[...] = p[tm - 8:tm, :]
    row = lax.broadcasted_iota(jnp.int32, p.shape, 0)
    p1 = jnp.where(row == 0, prev[7:8, :], pltpu.roll(p, 1, axis=0))
    p2 = jnp.where(row == 0, prev[6:7, :],
                   jnp.where(row == 1, prev[7:8, :], pltpu.roll(p, 2, axis=0)))
    cw = cw_ref[...]
    y = b * (cw[0:1, :] * p2 + cw[1:2, :] * p1 + cw[2:3, :] * p)
    o_ref[...] = y.astype(o_ref.dtype)


def _conv_branch(xn, w_in, conv_w, width, tm=1024, tn=256):
    t, d = xn.shape
    nj = width // tn
    return pl.pallas_call(
        _conv_kernel,
        grid=(nj, t // tm),
        in_specs=[pl.BlockSpec((tm, d), lambda j, i: (i, 0)),
                  pl.BlockSpec((d, tn), lambda j, i: (0, j)),
                  pl.BlockSpec((d, tn), lambda j, i: (0, nj + j)),
                  pl.BlockSpec((d, tn), lambda j, i: (0, 2 * nj + j)),
                  pl.BlockSpec((CONV_K, tn), lambda j, i: (0, j))],
        out_specs=pl.BlockSpec((tm, tn), lambda j, i: (i, j)),
        out_shape=jax.ShapeDtypeStruct((t, width), BF16),
        scratch_shapes=[pltpu.VMEM((d, 3 * tn), BF16),
                        pltpu.VMEM((8, tn), F32)],
        compiler_params=_params(2),
        name="conv_branch",
    )(xn, w_in, w_in, w_in, conv_w)


def _zproj_kernel(xn_ref, w_ref, o_ref, wbf_ref):
    @pl.when(pl.program_id(1) == 0)
    def _():
        wbf_ref[...] = w_ref[...].astype(BF16)

    z = _dot(xn_ref[...], wbf_ref[...])
    o_ref[...] = jax.nn.gelu(z).astype(o_ref.dtype)


def _zproj(xn, w_in, col0, width, tm=1024, tn=512):
    t, d = xn.shape
    c0 = col0 // tn
    return pl.pallas_call(
        _zproj_kernel,
        grid=(width // tn, t // tm),
        in_specs=[pl.BlockSpec((tm, d), lambda j, i: (i, 0)),
                  pl.BlockSpec((d, tn), lambda j, i: (0, c0 + j))],
        out_specs=pl.BlockSpec((tm, tn), lambda j, i: (i, j)),
        out_shape=jax.ShapeDtypeStruct((t, width), BF16),
        scratch_shapes=[pltpu.VMEM((d, tn), BF16)],
        compiler_params=_params(2),
        name="sgu_zproj",
    )(xn, w_in)


def _sgu_kernel(gz_ref, lng_ref, lnb_ref, ws_ref, bsx_ref, o_ref):
    tm = gz_ref.shape[0]
    w = o_ref.shape[1]
    hd = w // SGU_HEADS
    v = gz_ref[:, w:2 * w].astype(F32)
    mu = jnp.mean(v, axis=-1, keepdims=True)
    vc = v - mu
    var = jnp.mean(vc * vc, axis=-1, keepdims=True)
    vn = (vc * lax.rsqrt(var + EPS) * lng_ref[...] + lnb_ref[...]).astype(BF16)
    ii = lax.broadcasted_iota(jnp.int32, (SGU_BLOCK, SGU_BLOCK), 0)
    jj = lax.broadcasted_iota(jnp.int32, (SGU_BLOCK, SGU_BLOCK), 1)
    mask = (jj // CHUNK) <= (ii // CHUNK)
    for h in range(SGU_HEADS):
        wm = jnp.where(mask, ws_ref[h], 0.0).astype(BF16)
        cs = slice(h * hd, (h + 1) * hd)
        for n in range(tm // SGU_BLOCK):
            rs = slice(n * SGU_BLOCK, (n + 1) * SGU_BLOCK)
            vm = _dot(wm, vn[rs, cs]) + bsx_ref[:, cs]
            o_ref[rs, cs] = (gz_ref[rs, cs].astype(F32) * vm).astype(o_ref.dtype)


def _sgu_mix(gz, ln_g, ln_b, w_s, b_s, tm=512):
    t, w2 = gz.shape
    w = w2 // 2
    hd = w // SGU_HEADS
    bsx = jnp.repeat(b_s.T, hd, axis=1)
    return pl.pallas_call(
        _sgu_kernel,
        grid=(t // tm,),
        in_specs=[pl.BlockSpec((tm, w2), lambda i: (i, 0)),
                  pl.BlockSpec((1, w), lambda i: (0, 0)),
                  pl.BlockSpec((1, w), lambda i: (0, 0)),
                  pl.BlockSpec((SGU_HEADS, SGU_BLOCK, SGU_BLOCK), lambda i: (0, 0, 0)),
                  pl.BlockSpec((SGU_BLOCK, w), lambda i: (0, 0))],
        out_specs=pl.BlockSpec((tm, w), lambda i: (i, 0)),
        out_shape=jax.ShapeDtypeStruct((t, w), BF16),
        compiler_params=_params(1),
        name="sgu_mix",
    )(gz, ln_g.reshape(1, w), ln_b.reshape(1, w), w_s, bsx)


def _upgate_kernel(xn_ref, ya_ref, yb_ref, wgc_ref, wgs_ref, wua_ref, wub_ref, o_ref,
                   wgbf_ref, wuabf_ref, wubbf_ref):
    tn = wgc_ref.shape[1]

    @pl.when(pl.program_id(1) == 0)
    def _():
        wgbf_ref[:, 0:tn] = wgc_ref[...].astype(BF16)
        wgbf_ref[:, tn:2 * tn] = wgs_ref[...].astype(BF16)
        wuabf_ref[...] = wua_ref[...].astype(BF16)
        wubbf_ref[...] = wub_ref[...].astype(BF16)

    gl = _dot(xn_ref[...], wgbf_ref[...])
    a = _dot(ya_ref[...], wuabf_ref[...])
    b = _dot(yb_ref[...], wubbf_ref[...])
    m = jax.nn.sigmoid(gl[:, 0:tn]) * a + jax.nn.sigmoid(gl[:, tn:2 * tn]) * b
    o_ref[...] = m.astype(o_ref.dtype)


def _upgate(xn, ya, yb, w_in, gate_col0, w_up_a, w_up_b, tm=512, tn=512):
    t, d = xn.shape
    wa = ya.shape[1]
    wb = yb.shape[1]
    dout = w_up_a.shape[1]
    c0 = gate_col0 // tn
    nj = dout // tn
    return pl.pallas_call(
        _upgate_kernel,
        grid=(nj, t // tm),
        in_specs=[pl.BlockSpec((tm, d), lambda j, i: (i, 0)),
                  pl.BlockSpec((tm, wa), lambda j, i: (i, 0)),
                  pl.BlockSpec((tm, wb), lambda j, i: (i, 0)),
                  pl.BlockSpec((d, tn), lambda j, i: (0, c0 + j)),
                  pl.BlockSpec((d, tn), lambda j, i: (0, c0 + nj + j)),
                  pl.BlockSpec((wa, tn), lambda j, i: (0, j)),
                  pl.BlockSpec((wb, tn), lambda j, i: (0, j))],
        out_specs=pl.BlockSpec((tm, tn), lambda j, i: (i, j)),
        out_shape=jax.ShapeDtypeStruct((t, dout), BF16),
        scratch_shapes=[pltpu.VMEM((d, 2 * tn), BF16),
                        pltpu.VMEM((wa, tn), BF16),
                        pltpu.VMEM((wb, tn), BF16)],
        compiler_params=_params(2),
        name="upgate",
    )(xn, ya, yb, w_in, w_in, w_up_a, w_up_b)


def _outproj_kernel(m_ref, w_ref, x_ref, o_ref, wbf_ref):
    @pl.when(pl.program_id(1) == 0)
    def _():
        wbf_ref[...] = w_ref[...].astype(BF16)

    val = x_ref[...] + _dot(m_ref[...], wbf_ref[...])
    for k in range(o_ref.shape[1]):
        o_ref[:, k, :] = val[:, k * LANES:(k + 1) * LANES]


def _outproj(m, w_out, x, tm=512, tn=1024):
    t, d = m.shape
    dout = w_out.shape[1]
    sl = tn // LANES
    assert sl == 8, "each column block must fill whole (8,128) tiles per token"
    out = pl.pallas_call(
        _outproj_kernel,
        grid=(dout // tn, t // tm),
        in_specs=[pl.BlockSpec((tm, d), lambda j, i: (i, 0)),
                  pl.BlockSpec((d, tn), lambda j, i: (0, j)),
                  pl.BlockSpec((tm, tn), lambda j, i: (i, j))],
        out_specs=pl.BlockSpec((tm, None, sl, LANES), lambda j, i: (i, j, 0, 0)),
        out_shape=jax.ShapeDtypeStruct((t, dout // tn, sl, LANES), F32),
        scratch_shapes=[pltpu.VMEM((d, tn), BF16)],
        compiler_params=_params(2),
        name="outproj",
    )(m, w_out, x)
    return out.reshape(t * (dout // LANES), LANES)


def _argmax_rows(rows):
    best = rows[0]
    idx = jnp.zeros(rows[0].shape, jnp.int32)
    for k in range(1, len(rows)):
        better = rows[k] > best
        best = jnp.where(better, rows[k], best)
        idx = jnp.where(better, k, idx)
    return best, idx


def _softmax_rows(rows):
    mx = functools.reduce(jnp.maximum, rows)
    ex = [jnp.exp(r - mx) for r in rows]
    den = functools.reduce(lambda a, b: a + b, ex)
    return [e / den for e in ex]


def _router_kernel(h_ref, g_ref, wr_ref, br_ref, meta_ref, cnt_ref, carry_ref):
    i = pl.program_id(0)
    slabs = g_ref.shape[1] // LANES
    tm = h_ref.shape[0] // slabs

    @pl.when(i == 0)
    def _():
        carry_ref[...] = jnp.zeros_like(carry_ref)

    x = _load_token_major(h_ref, 0, tm, slabs)
    ms = jnp.mean(x * x, axis=-1, keepdims=True)
    xn = x * lax.rsqrt(ms + EPS) * g_ref[...]
    lt = lax.dot_general(wr_ref[...], xn, (((1,), (1,)), ((), ())),
                         precision=lax.Precision.HIGHEST,
                         preferred_element_type=F32) + br_ref[...]
    pgs = _softmax_rows([lt[k:k + 1, :] for k in range(N_GROUPS)])
    pg, gi = _argmax_rows(pgs)
    sel = []
    for k in range(EXPERTS_PER_GROUP):
        r = jnp.zeros_like(pg)
        for g in range(N_GROUPS):
            row = N_GROUPS + g * EXPERTS_PER_GROUP + k
            r = jnp.where(gi == g, lt[row:row + 1, :], r)
        sel.append(r)
    pes = _softmax_rows(sel)
    p1, e1 = _argmax_rows(pes)
    rest = [jnp.where(e1 == k, -1.0, pes[k]) for k in range(EXPERTS_PER_GROUP)]
    p2, e2 = _argmax_rows(rest)
    den = p1 + p2
    w1 = pg * (p1 / den)
    w2 = pg * (p2 / den)
    lo = jnp.minimum(e1, e2)
    hi = jnp.maximum(e1, e2)
    w_lo = jnp.where(e1 < e2, w1, w2)
    w_hi = jnp.where(e1 < e2, w2, w1)
    ea = gi * EXPERTS_PER_GROUP + lo
    eb = gi * EXPERTS_PER_GROUP + hi

    erow = lax.broadcasted_iota(jnp.int32, (N_EXPERTS, tm), 0)
    oh_a = (erow == ea).astype(F32)
    oh_b = (erow == eb).astype(F32)
    a = lax.broadcasted_iota(jnp.int32, (tm, tm), 0)
    b = lax.broadcasted_iota(jnp.int32, (tm, tm), 1)
    before = (a < b).astype(BF16)
    cum = _dot((oh_a + oh_b).astype(BF16), before)
    prior = cum + carry_ref[...][:, 0:1]
    rank_a = jnp.sum(oh_a * prior, axis=0, keepdims=True)
    rank_b = jnp.sum(oh_b * prior, axis=0, keepdims=True)
    total = carry_ref[...][:, 0:1] + jnp.sum(oh_a + oh_b, axis=1, keepdims=True)
    carry_ref[...] = jnp.broadcast_to(total, carry_ref.shape)
    cnt_ref[...] = jnp.broadcast_to(total, cnt_ref.shape)

    meta_ref[0:1, :] = ea.astype(F32)
    meta_ref[1:2, :] = eb.astype(F32)
    meta_ref[2:3, :] = rank_a
    meta_ref[3:4, :] = rank_b
    meta_ref[4:5, :] = w_lo
    meta_ref[5:6, :] = w_hi
    meta_ref[6:8, :] = jnp.zeros((2, tm), F32)


def _router(h1tm, g, w_rg, b_rg, w_re, b_re, tm=512):
    d = g.shape[0]
    slabs = d // LANES
    t = h1tm.shape[0] // slabs
    n_log = w_rg.shape[1] + w_re.shape[1]
    wr = jnp.concatenate([w_rg, w_re], axis=1).T
    wr = jnp.pad(wr, ((0, ROUTER_ROWS - n_log), (0, 0)))
    br = jnp.pad(jnp.concatenate([b_rg, b_re]), (0, ROUTER_ROWS - n_log)).reshape(ROUTER_ROWS, 1)
    return pl.pallas_call(
        _router_kernel,
        grid=(t // tm,),
        in_specs=[pl.BlockSpec((tm * slabs, LANES), lambda i: (i, 0)),
                  pl.BlockSpec((1, d), lambda i: (0, 0)),
                  pl.BlockSpec((ROUTER_ROWS, d), lambda i: (0, 0)),
                  pl.BlockSpec((ROUTER_ROWS, 1), lambda i: (0, 0))],
        out_specs=[pl.BlockSpec((8, tm), lambda i: (0, i)),
                   pl.BlockSpec((N_EXPERTS, LANES), lambda i: (0, 0))],
        out_shape=[jax.ShapeDtypeStruct((8, t), F32),
                   jax.ShapeDtypeStruct((N_EXPERTS, LANES), F32)],
        scratch_shapes=[pltpu.VMEM((N_EXPERTS, LANES), F32)],
        compiler_params=_params(1),
        name="router",
    )(h1tm, g.reshape(1, d), wr, br)


def _rows_copy(src_ref, s_row, dst_ref, d_row, n, sem):
    def aligned(row):
        return row if isinstance(row, int) or n == 1 else pl.multiple_of(row, n)

    return pltpu.make_async_copy(src_ref.at[pl.ds(aligned(s_row), n), :],
                                 dst_ref.at[pl.ds(aligned(d_row), n), :], sem)


def _dispatch_kernel(pa_ref, pb_ref, pad_lo_ref, pad_hi_ref, nt_ref, h_ref, xs_ref,
                     zrow_ref, sem, pad_sem):
    c = pl.program_id(0)
    slabs = h_ref.shape[0] // DMA_CHUNK
    tile_rows = MOE_TILE * slabs

    def issue(k, carry):
        for u in range(DMA_UNROLL):
            r = k * DMA_UNROLL + u
            t = c * DMA_CHUNK + r
            _rows_copy(h_ref, r * slabs, xs_ref, pa_ref[t] * slabs, slabs, sem).start(priority=0)
            _rows_copy(h_ref, r * slabs, xs_ref, pb_ref[t] * slabs, slabs, sem).start(priority=1)
        return carry

    lax.fori_loop(0, DMA_CHUNK // DMA_UNROLL, issue, 0)

    def pad_copy(r):
        return _rows_copy(zrow_ref, 0, xs_ref, r * slabs, slabs, pad_sem)

    def tail_copy(r):
        return pltpu.make_async_copy(
            zrow_ref, xs_ref.at[pl.ds(pl.multiple_of(r * tile_rows, tile_rows), tile_rows), :], pad_sem)

    def start_of(make):
        def body(r, carry):
            make(r).start()
            return carry
        return body

    def wait_of(make):
        def body(r, carry):
            make(r).wait()
            return carry
        return body

    r_max = xs_ref.shape[0] // tile_rows

    @pl.when(c == 0)
    def _():
        zrow_ref[...] = jnp.zeros_like(zrow_ref)
        for k in range(N_EXPERTS):
            lax.fori_loop(pad_lo_ref[k], pad_hi_ref[k], start_of(pad_copy), 0)
        lax.fori_loop(nt_ref[0], r_max, start_of(tail_copy), 0)

    for _ in range(2):
        pltpu.make_async_copy(h_ref, xs_ref.at[pl.ds(0, DMA_CHUNK * slabs), :], sem).wait()

    @pl.when(c == 0)
    def _():
        for k in range(N_EXPERTS):
            lax.fori_loop(pad_lo_ref[k], pad_hi_ref[k], wait_of(pad_copy), 0)
        lax.fori_loop(nt_ref[0], r_max, wait_of(tail_copy), 0)


def _dispatch(pos_a, pos_b, pad_lo, pad_hi, n_tiles, h1tm, n_rows):
    t = pos_a.shape[0]
    slabs = h1tm.shape[0] // t
    grid_spec = pltpu.PrefetchScalarGridSpec(
        num_scalar_prefetch=5,
        grid=(t // DMA_CHUNK,),
        in_specs=[pl.BlockSpec((DMA_CHUNK * slabs, LANES), lambda c, *_: (c, 0))],
        out_specs=pl.BlockSpec(memory_space=pl.ANY),
        scratch_shapes=[pltpu.VMEM((MOE_TILE * slabs, LANES), F32),
                        pltpu.SemaphoreType.DMA(()),
                        pltpu.SemaphoreType.DMA(())],
    )
    return pl.pallas_call(
        _dispatch_kernel,
        grid_spec=grid_spec,
        out_shape=jax.ShapeDtypeStruct((n_rows * slabs, LANES), F32),
        compiler_params=_params(1),
        name="dispatch",
    )(pos_a, pos_b, pad_lo, pad_hi, n_tiles, h1tm)


def _combine_kernel(pa_ref, pb_ref, ys_ref, h_ref, wm_ref, g_ref, o_ref, sa_ref, sb_ref, sems):
    c = pl.program_id(0)
    nc = pl.num_programs(0)
    slabs = o_ref.shape[1] // LANES
    rows = DMA_CHUNK * slabs

    def issue(chunk, slot):
        def body(k, carry):
            for u in range(DMA_UNROLL):
                r = k * DMA_UNROLL + u
                t = chunk * DMA_CHUNK + r
                dst = slot * rows + r * slabs
                _rows_copy(ys_ref, pa_ref[t] * slabs, sa_ref, dst, slabs, sems.at[slot]).start(priority=0)
                _rows_copy(ys_ref, pb_ref[t] * slabs, sb_ref, dst, slabs, sems.at[slot]).start(priority=1)
            return carry

        lax.fori_loop(0, DMA_CHUNK // DMA_UNROLL, body, 0)

    @pl.when(c == 0)
    def _():
        issue(0, 0)

    @pl.when(c + 1 < nc)
    def _():
        issue(c + 1, (c + 1) % 2)

    slot = c % 2
    base = pl.multiple_of(slot * rows, rows)
    for stage in (sa_ref, sb_ref):
        pltpu.make_async_copy(ys_ref.at[pl.ds(0, rows), :], stage.at[pl.ds(base, rows), :],
                              sems.at[slot]).wait()
    w = wm_ref[...]
    h2 = (_load_token_major(h_ref, 0, DMA_CHUNK, slabs)
          + w[:, 0:1] * _load_token_major(sa_ref, base, DMA_CHUNK, slabs)
          + w[:, 1:2] * _load_token_major(sb_ref, base, DMA_CHUNK, slabs))
    ms = jnp.mean(h2 * h2, axis=-1, keepdims=True)
    o_ref[...] = h2 * lax.rsqrt(ms + EPS) * g_ref[...]


def _combine(pos_a, pos_b, ys, h1tm, wmeta, g):
    t = pos_a.shape[0]
    d = g.shape[0]
    slabs = d // LANES
    grid_spec = pltpu.PrefetchScalarGridSpec(
        num_scalar_prefetch=2,
        grid=(t // DMA_CHUNK,),
        in_specs=[pl.BlockSpec(memory_space=pl.ANY),
                  pl.BlockSpec((DMA_CHUNK * slabs, LANES), lambda c, *_: (c, 0)),
                  pl.BlockSpec((DMA_CHUNK, LANES), lambda c, *_: (c, 0)),
                  pl.BlockSpec((1, d), lambda c, *_: (0, 0))],
        out_specs=pl.BlockSpec((DMA_CHUNK, d), lambda c, *_: (c, 0)),
        scratch_shapes=[pltpu.VMEM((2 * DMA_CHUNK * slabs, LANES), F32),
                        pltpu.VMEM((2 * DMA_CHUNK * slabs, LANES), F32),
                        pltpu.SemaphoreType.DMA((2,))],
    )
    return pl.pallas_call(
        _combine_kernel,
        grid_spec=grid_spec,
        out_shape=jax.ShapeDtypeStruct((t, d), F32),
        compiler_params=_params(1),
        name="combine",
    )(pos_a, pos_b, ys, h1tm, wmeta, g.reshape(1, d))


def _expert_kernel(te_ref, nt_ref, xs_ref, g_ref, wg_ref, wu_ref, wd_ref, ys_ref,
                   wgu_bf_ref, wd_bf_ref, cur_ref):
    r = pl.program_id(0)
    f = wg_ref.shape[1]
    slabs = g_ref.shape[1] // LANES

    @pl.when(r == 0)
    def _():
        cur_ref[0] = -1

    @pl.when(r >= nt_ref[0])
    def _():
        ys_ref[...] = jnp.zeros_like(ys_ref)

    @pl.when(r < nt_ref[0])
    def _():
        e = te_ref[r]

        @pl.when(cur_ref[0] != e)
        def _():
            wgu_bf_ref[:, 0:f] = wg_ref[...].astype(BF16)
            wgu_bf_ref[:, f:2 * f] = wu_ref[...].astype(BF16)
            wd_bf_ref[...] = wd_ref[...].astype(BF16)
            cur_ref[0] = e

        x = _load_token_major(xs_ref, 0, xs_ref.shape[0] // slabs, slabs)
        ms = jnp.mean(x * x, axis=-1, keepdims=True)
        xn = (x * lax.rsqrt(ms + EPS) * g_ref[...]).astype(BF16)
        gu = _dot(xn, wgu_bf_ref[...])
        hid = (jax.nn.silu(gu[:, 0:f]) * gu[:, f:2 * f]).astype(BF16)
        _store_token_major(ys_ref, _dot(hid, wd_bf_ref[...]), slabs)


def _experts(tile_e, n_tiles, xs, g, w_gate, w_up, w_down):
    d = g.shape[0]
    slabs = d // LANES
    f = w_gate.shape[2]
    tile_rows = MOE_TILE * slabs
    r_max = xs.shape[0] // tile_rows

    def row_map(r, te, nt):
        return (jnp.minimum(r, nt[0] - 1), 0)

    grid_spec = pltpu.PrefetchScalarGridSpec(
        num_scalar_prefetch=2,
        grid=(r_max,),
        in_specs=[pl.BlockSpec((tile_rows, LANES), row_map),
                  pl.BlockSpec((1, d), lambda r, te, nt: (0, 0)),
                  pl.BlockSpec((None, d, f), lambda r, te, nt: (te[r], 0, 0)),
                  pl.BlockSpec((None, d, f), lambda r, te, nt: (te[r], 0, 0)),
                  pl.BlockSpec((None, f, d), lambda r, te, nt: (te[r], 0, 0))],
        out_specs=pl.BlockSpec((tile_rows, LANES), lambda r, te, nt: (r, 0)),
        scratch_shapes=[pltpu.VMEM((d, 2 * f), BF16),
                        pltpu.VMEM((f, d), BF16),
                        pltpu.SMEM((1,), jnp.int32)],
    )
    return pl.pallas_call(
        _expert_kernel,
        grid_spec=grid_spec,
        out_shape=jax.ShapeDtypeStruct(xs.shape, F32),
        compiler_params=_params(1),
        name="experts",
    )(tile_e, n_tiles, xs, g.reshape(1, d), w_gate, w_up, w_down)


def _routing_plan(meta, counts, t):
    ea = meta[0].astype(jnp.int32)
    eb = meta[1].astype(jnp.int32)
    cnt = counts[:, 0].astype(jnp.int32)
    tiles_e = (cnt + MOE_TILE - 1) // MOE_TILE
    tile_end = jnp.cumsum(tiles_e)
    row_start = (tile_end - tiles_e) * MOE_TILE
    pos_a = row_start[ea] + meta[2].astype(jnp.int32)
    pos_b = row_start[eb] + meta[3].astype(jnp.int32)
    n_tiles = tile_end[-1]
    r_max = (2 * t) // MOE_TILE + N_EXPERTS
    r = jnp.minimum(jnp.arange(r_max, dtype=jnp.int32), n_tiles - 1)
    tile_e = jnp.sum((tile_end[None, :] <= r[:, None]).astype(jnp.int32), axis=1)
    pad_lo = row_start + cnt
    pad_hi = tile_end * MOE_TILE
    wmeta = jnp.pad(meta[4:6].T, ((0, 0), (0, LANES - 2)))
    return pos_a, pos_b, pad_lo, pad_hi, tile_e, n_tiles.reshape(1), wmeta, r_max * MOE_TILE


def _layer(h, norm_mix_g, w_in, conv_w, sgu_ln_g, sgu_ln_b, sgu_w_s, sgu_b_s, w_up_conv,
           w_up_sgu, w_out, norm_ffn_g, w_rg, b_rg, w_re, b_re, w_eg, w_eu, w_ed, out_g):
    t, d = h.shape
    conv_width = conv_w.shape[1]
    sgu_width = sgu_ln_g.shape[0]
    xn = _rmsnorm(h, norm_mix_g)
    ya = _conv_branch(xn, w_in, conv_w, conv_width)
    gz = _zproj(xn, w_in, 3 * conv_width, 2 * sgu_width)
    yb = _sgu_mix(gz, sgu_ln_g, sgu_ln_b, sgu_w_s, sgu_b_s)
    m = _upgate(xn, ya, yb, w_in, 3 * conv_width + 2 * sgu_width, w_up_conv, w_up_sgu)
    h1tm = _outproj(m, w_out, h)
    meta, counts = _router(h1tm, norm_ffn_g, w_rg, b_rg, w_re, b_re)
    pos_a, pos_b, pad_lo, pad_hi, tile_e, n_tiles, wmeta, n_rows = _routing_plan(meta, counts, t)
    xs = _dispatch(pos_a, pos_b, pad_lo, pad_hi, n_tiles, h1tm, n_rows)
    ys = _experts(tile_e, n_tiles, xs, norm_ffn_g, w_eg, w_eu, w_ed)
    return _combine(pos_a, pos_b, ys, h1tm, wmeta, out_g)


def kernel(x, norm_mix_g, w_in, conv_w, sgu_ln_g, sgu_ln_b, sgu_w_s, sgu_b_s, w_up_conv, w_up_sgu, w_out, norm_ffn_g, w_router_group, b_router_group, w_router_expert, b_router_expert, w_exp_gate, w_exp_up, w_exp_down, norm_final_g):
    bsz, s, d = x.shape
    depth = w_in.shape[0]
    assert bsz == 1 and depth == 1, "causal conv carry and the fused final norm assume one sequence, one layer"
    out = _layer(x.reshape(s, d), norm_mix_g[0], w_in[0], conv_w[0], sgu_ln_g[0], sgu_ln_b[0],
                 sgu_w_s[0], sgu_b_s[0], w_up_conv[0], w_up_sgu[0], w_out[0], norm_ffn_g[0],
                 w_router_group[0], b_router_group[0], w_router_expert[0], b_router_expert[0],
                 w_exp_gate[0], w_exp_up[0], w_exp_down[0], norm_final_g)
    return out.reshape(bsz, s, d)
```

```python
import functools

import jax
import jax.numpy as jnp
from jax import lax
from jax.experimental import pallas as pl
from jax.experimental.pallas import tpu as pltpu

F32 = jnp.float32
BF16 = jnp.bfloat16

EPS = 1e-6
CHUNK = 64
CONV_K = 3
SGU_HEADS = 8
SGU_BLOCK = 128
N_GROUPS = 4
EXPERTS_PER_GROUP = 4
N_EXPERTS = N_GROUPS * EXPERTS_PER_GROUP
ROUTER_ROWS = 32
LANES = 128

VMEM_LIMIT_BYTES = 56 * 1024 * 1024

MOE_TILE = 256
STAGE_PITCH = 20


def _params(n_axes):
    return pltpu.CompilerParams(
        dimension_semantics=("arbitrary",) * n_axes,
        vmem_limit_bytes=VMEM_LIMIT_BYTES)


def _dot(a, b):
    return jnp.dot(a, b, preferred_element_type=F32)


def _load_token_major(ref, base, n_tok, slabs, pitch=None):
    pitch = pitch or slabs
    return jnp.concatenate(
        [ref[pl.ds(base + c, n_tok, stride=pitch), :] for c in range(slabs)], axis=1)


def _store_token_major(ref, base, val, slabs, pitch=None):
    pitch = pitch or slabs
    n_tok = val.shape[0]
    for c in range(slabs):
        ref[pl.ds(base + c, n_tok, stride=pitch), :] = val[:, c * LANES:(c + 1) * LANES]


def _rmsnorm_kernel(x_ref, g_ref, o_ref):
    x = x_ref[...]
    ms = jnp.mean(x * x, axis=-1, keepdims=True)
    o_ref[...] = (x * lax.rsqrt(ms + EPS) * g_ref[...]).astype(o_ref.dtype)


def _rmsnorm(x, g, tm=512):
    t, d = x.shape
    return pl.pallas_call(
        _rmsnorm_kernel,
        grid=(t // tm,),
        in_specs=[pl.BlockSpec((tm, d), lambda i: (i, 0)),
                  pl.BlockSpec((1, d), lambda i: (0, 0))],
        out_specs=pl.BlockSpec((tm, d), lambda i: (i, 0)),
        out_shape=jax.ShapeDtypeStruct((t, d), BF16),
        compiler_params=_params(1),
        name="rmsnorm",
    )(x, g.reshape(1, d))


def _conv_kernel(xn_ref, wb_ref, wc_ref, wh_ref, cw_ref, o_ref, wbf_ref, carry_ref):
    i = pl.program_id(1)
    tn = wb_ref.shape[1]
    tm = xn_ref.shape[0]

    @pl.when(i == 0)
    def _():
        wbf_ref[:, 0:tn] = wb_ref[...].astype(BF16)
        wbf_ref[:, tn:2 * tn] = wc_ref[...].astype(BF16)
        wbf_ref[:, 2 * tn:3 * tn] = wh_ref[...].astype(BF16)
        carry_ref[...] = jnp.zeros_like(carry_ref)

    proj = _dot(xn_ref[...], wbf_ref[...])
    b = proj[:, 0:tn]
    p = proj[:, tn:2 * tn] * proj[:, 2 * tn:3 * tn]
    prev = carry_ref[...]
    carry_ref[...] = p[tm - 8:tm, :]
    row = lax.broadcasted_iota(jnp.int32, p.shape, 0)
    p1 = jnp.where(row == 0, prev[7:8, :], pltpu.roll(p, 1, axis=0))
    p2 = jnp.where(row == 0, prev[6:7, :],
                   jnp.where(row == 1, prev[7:8, :], pltpu.roll(p, 2, axis=0)))
    cw = cw_ref[...]
    y = b * (cw[0:1, :] * p2 + cw[1:2, :] * p1 + cw[2:3, :] * p)
    o_ref[...] = y.astype(o_ref.dtype)


def _conv_branch(xn, w_in, conv_w, width, tm=1024, tn=256):
    t, d = xn.shape
    nj = width // tn
    return pl.pallas_call(
        _conv_kernel,
        grid=(nj, t // tm),
        in_specs=[pl.BlockSpec((tm, d), lambda j, i: (i, 0)),
                  pl.BlockSpec((d, tn), lambda j, i: (0, j)),
                  pl.BlockSpec((d, tn), lambda j, i: (0, nj + j)),
                  pl.BlockSpec((d, tn), lambda j, i: (0, 2 * nj + j)),
                  pl.BlockSpec((CONV_K, tn), lambda j, i: (0, j))],
        out_specs=pl.BlockSpec((tm, tn), lambda j, i: (i, j)),
        out_shape=jax.ShapeDtypeStruct((t, width), BF16),
        scratch_shapes=[pltpu.VMEM((d, 3 * tn), BF16),
                        pltpu.VMEM((8, tn), F32)],
        compiler_params=_params(2),
        name="conv_branch",
    )(xn, w_in, w_in, w_in, conv_w)


def _zproj_kernel(xn_ref, w_ref, o_ref, wbf_ref):
    @pl.when(pl.program_id(1) == 0)
    def _():
        wbf_ref[...] = w_ref[...].astype(BF16)

    z = _dot(xn_ref[...], wbf_ref[...])
    o_ref[...] = jax.nn.gelu(z).astype(o_ref.dtype)


def _zproj(xn, w_in, col0, width, tm=1024, tn=512):
    t, d = xn.shape
    c0 = col0 // tn
    return pl.pallas_call(
        _zproj_kernel,
        grid=(width // tn, t // tm),
        in_specs=[pl.BlockSpec((tm, d), lambda j, i: (i, 0)),
                  pl.BlockSpec((d, tn), lambda j, i: (0, c0 + j))],
        out_specs=pl.BlockSpec((tm, tn), lambda j, i: (i, j)),
        out_shape=jax.ShapeDtypeStruct((t, width), BF16),
        scratch_shapes=[pltpu.VMEM((d, tn), BF16)],
        compiler_params=_params(2),
        name="sgu_zproj",
    )(xn, w_in)


def _sgu_kernel(gz_ref, lng_ref, lnb_ref, ws_ref, bsx_ref, o_ref):
    tm = gz_ref.shape[0]
    w = o_ref.shape[1]
    hd = w // SGU_HEADS
    v = gz_ref[:, w:2 * w].astype(F32)
    mu = jnp.mean(v, axis=-1, keepdims=True)
    vc = v - mu
    var = jnp.mean(vc * vc, axis=-1, keepdims=True)
    vn = (vc * lax.rsqrt(var + EPS) * lng_ref[...] + lnb_ref[...]).astype(BF16)
    ii = lax.broadcasted_iota(jnp.int32, (SGU_BLOCK, SGU_BLOCK), 0)
    jj = lax.broadcasted_iota(jnp.int32, (SGU_BLOCK, SGU_BLOCK), 1)
    mask = (jj // CHUNK) <= (ii // CHUNK)
    for h in range(SGU_HEADS):
        wm = jnp.where(mask, ws_ref[h], 0.0).astype(BF16)
        cs = slice(h * hd, (h + 1) * hd)
        for n in range(tm // SGU_BLOCK):
            rs = slice(n * SGU_BLOCK, (n + 1) * SGU_BLOCK)
            vm = _dot(wm, vn[rs, cs]) + bsx_ref[:, cs]
            o_ref[rs, cs] = (gz_ref[rs, cs].astype(F32) * vm).astype(o_ref.dtype)


def _sgu_mix(gz, ln_g, ln_b, w_s, b_s, tm=512):
    t, w2 = gz.shape
    w = w2 // 2
    hd = w // SGU_HEADS
    bsx = jnp.repeat(b_s.T, hd, axis=1)
    return pl.pallas_call(
        _sgu_kernel,
        grid=(t // tm,),
        in_specs=[pl.BlockSpec((tm, w2), lambda i: (i, 0)),
                  pl.BlockSpec((1, w), lambda i: (0, 0)),
                  pl.BlockSpec((1, w), lambda i: (0, 0)),
                  pl.BlockSpec((SGU_HEADS, SGU_BLOCK, SGU_BLOCK), lambda i: (0, 0, 0)),
                  pl.BlockSpec((SGU_BLOCK, w), lambda i: (0, 0))],
        out_specs=pl.BlockSpec((tm, w), lambda i: (i, 0)),
        out_shape=jax.ShapeDtypeStruct((t, w), BF16),
        compiler_params=_params(1),
        name="sgu_mix",
    )(gz, ln_g.reshape(1, w), ln_b.reshape(1, w), w_s, bsx)


def _upgate_kernel(xn_ref, ya_ref, yb_ref, wgc_ref, wgs_ref, wua_ref, wub_ref, o_ref,
                   wgbf_ref, wuabf_ref, wubbf_ref):
    tn = wgc_ref.shape[1]

    @pl.when(pl.program_id(1) == 0)
    def _():
        wgbf_ref[:, 0:tn] = wgc_ref[...].astype(BF16)
        wgbf_ref[:, tn:2 * tn] = wgs_ref[...].astype(BF16)
        wuabf_ref[...] = wua_ref[...].astype(BF16)
        wubbf_ref[...] = wub_ref[...].astype(BF16)

    gl = _dot(xn_ref[...], wgbf_ref[...])
    a = _dot(ya_ref[...], wuabf_ref[...])
    b = _dot(yb_ref[...], wubbf_ref[...])
    m = jax.nn.sigmoid(gl[:, 0:tn]) * a + jax.nn.sigmoid(gl[:, tn:2 * tn]) * b
    o_ref[...] = m.astype(o_ref.dtype)


def _upgate(xn, ya, yb, w_in, gate_col0, w_up_a, w_up_b, tm=512, tn=512):
    t, d = xn.shape
    wa = ya.shape[1]
    wb = yb.shape[1]
    dout = w_up_a.shape[1]
    c0 = gate_col0 // tn
    nj = dout // tn
    return pl.pallas_call(
        _upgate_kernel,
        grid=(nj, t // tm),
        in_specs=[pl.BlockSpec((tm, d), lambda j, i: (i, 0)),
                  pl.BlockSpec((tm, wa), lambda j, i: (i, 0)),
                  pl.BlockSpec((tm, wb), lambda j, i: (i, 0)),
                  pl.BlockSpec((d, tn), lambda j, i: (0, c0 + j)),
                  pl.BlockSpec((d, tn), lambda j, i: (0, c0 + nj + j)),
                  pl.BlockSpec((wa, tn), lambda j, i: (0, j)),
                  pl.BlockSpec((wb, tn), lambda j, i: (0, j))],
        out_specs=pl.BlockSpec((tm, tn), lambda j, i: (i, j)),
        out_shape=jax.ShapeDtypeStruct((t, dout), BF16),
        scratch_shapes=[pltpu.VMEM((d, 2 * tn), BF16),
                        pltpu.VMEM((wa, tn), BF16),
                        pltpu.VMEM((wb, tn), BF16)],
        compiler_params=_params(2),
        name="upgate",
    )(xn, ya, yb, w_in, w_in, w_up_a, w_up_b)


def _outproj_kernel(m_ref, w_ref, x_ref, o_ref, wbf_ref):
    @pl.when(pl.program_id(1) == 0)
    def _():
        wbf_ref[...] = w_ref[...].astype(BF16)

    val = x_ref[...] + _dot(m_ref[...], wbf_ref[...])
    for k in range(o_ref.shape[1]):
        o_ref[:, k, :] = val[:, k * LANES:(k + 1) * LANES]


def _outproj(m, w_out, x, tm=512, tn=1024):
    t, d = m.shape
    dout = w_out.shape[1]
    sl = tn // LANES
    assert sl == 8, "each column block must fill whole (8,128) tiles per token"
    out = pl.pallas_call(
        _outproj_kernel,
        grid=(dout // tn, t // tm),
        in_specs=[pl.BlockSpec((tm, d), lambda j, i: (i, 0)),
                  pl.BlockSpec((d, tn), lambda j, i: (0, j)),
                  pl.BlockSpec((tm, tn), lambda j, i: (i, j))],
        out_specs=pl.BlockSpec((tm, None, sl, LANES), lambda j, i: (i, j, 0, 0)),
        out_shape=jax.ShapeDtypeStruct((t, dout // tn, sl, LANES), F32),
        scratch_shapes=[pltpu.VMEM((d, tn), BF16)],
        compiler_params=_params(2),
        name="outproj",
    )(m, w_out, x)
    return out.reshape(t * (dout // LANES), LANES)


def _argmax_rows(rows):
    best = rows[0]
    idx = jnp.zeros(rows[0].shape, jnp.int32)
    for k in range(1, len(rows)):
        better = rows[k] > best
        best = jnp.where(better, rows[k], best)
        idx = jnp.where(better, k, idx)
    return best, idx


def _softmax_rows(rows):
    mx = functools.reduce(jnp.maximum, rows)
    ex = [jnp.exp(r - mx) for r in rows]
    den = functools.reduce(lambda a, b: a + b, ex)
    return [e / den for e in ex]


def _router_kernel(h_ref, g_ref, wr_ref, br_ref, meta_ref, cnt_ref, carry_ref):
    i = pl.program_id(0)
    slabs = g_ref.shape[1] // LANES
    tm = h_ref.shape[0] // slabs

    @pl.when(i == 0)
    def _():
        carry_ref[...] = jnp.zeros_like(carry_ref)

    x = _load_token_major(h_ref, 0, tm, slabs)
    ms = jnp.mean(x * x, axis=-1, keepdims=True)
    xn = x * lax.rsqrt(ms + EPS) * g_ref[...]
    lt = lax.dot_general(wr_ref[...], xn, (((1,), (1,)), ((), ())),
                         precision=lax.Precision.HIGHEST,
                         preferred_element_type=F32) + br_ref[...]
    pgs = _softmax_rows([lt[k:k + 1, :] for k in range(N_GROUPS)])
    pg, gi = _argmax_rows(pgs)
    sel = []
    for k in range(EXPERTS_PER_GROUP):
        r = jnp.zeros_like(pg)
        for g in range(N_GROUPS):
            row = N_GROUPS + g * EXPERTS_PER_GROUP + k
            r = jnp.where(gi == g, lt[row:row + 1, :], r)
        sel.append(r)
    pes = _softmax_rows(sel)
    p1, e1 = _argmax_rows(pes)
    rest = [jnp.where(e1 == k, -1.0, pes[k]) for k in range(EXPERTS_PER_GROUP)]
    p2, e2 = _argmax_rows(rest)
    den = p1 + p2
    w1 = pg * (p1 / den)
    w2 = pg * (p2 / den)
    lo = jnp.minimum(e1, e2)
    hi = jnp.maximum(e1, e2)
    w_lo = jnp.where(e1 < e2, w1, w2)
    w_hi = jnp.where(e1 < e2, w2, w1)
    ea = gi * EXPERTS_PER_GROUP + lo
    eb = gi * EXPERTS_PER_GROUP + hi

    erow = lax.broadcasted_iota(jnp.int32, (N_EXPERTS, tm), 0)
    oh_a = (erow == ea).astype(F32)
    oh_b = (erow == eb).astype(F32)
    a = lax.broadcasted_iota(jnp.int32, (tm, tm), 0)
    b = lax.broadcasted_iota(jnp.int32, (tm, tm), 1)
    before = (a < b).astype(BF16)
    cum = _dot((oh_a + oh_b).astype(BF16), before)
    prior = cum + carry_ref[...][:, 0:1]
    rank_a = jnp.sum(oh_a * prior, axis=0, keepdims=True)
    rank_b = jnp.sum(oh_b * prior, axis=0, keepdims=True)
    total = carry_ref[...][:, 0:1] + jnp.sum(oh_a + oh_b, axis=1, keepdims=True)
    carry_ref[...] = jnp.broadcast_to(total, carry_ref.shape)
    cnt_ref[...] = jnp.broadcast_to(total, cnt_ref.shape)

    meta_ref[0:1, :] = ea.astype(F32)
    meta_ref[1:2, :] = eb.astype(F32)
    meta_ref[2:3, :] = rank_a
    meta_ref[3:4, :] = rank_b
    meta_ref[4:5, :] = w_lo
    meta_ref[5:6, :] = w_hi
    meta_ref[6:8, :] = jnp.zeros((2, tm), F32)


def _router(h1tm, g, w_rg, b_rg, w_re, b_re, tm=512):
    d = g.shape[0]
    slabs = d // LANES
    t = h1tm.shape[0] // slabs
    n_log = w_rg.shape[1] + w_re.shape[1]
    wr = jnp.concatenate([w_rg, w_re], axis=1).T
    wr = jnp.pad(wr, ((0, ROUTER_ROWS - n_log), (0, 0)))
    br = jnp.pad(jnp.concatenate([b_rg, b_re]), (0, ROUTER_ROWS - n_log)).reshape(ROUTER_ROWS, 1)
    return pl.pallas_call(
        _router_kernel,
        grid=(t // tm,),
        in_specs=[pl.BlockSpec((tm * slabs, LANES), lambda i: (i, 0)),
                  pl.BlockSpec((1, d), lambda i: (0, 0)),
                  pl.BlockSpec((ROUTER_ROWS, d), lambda i: (0, 0)),
                  pl.BlockSpec((ROUTER_ROWS, 1), lambda i: (0, 0))],
        out_specs=[pl.BlockSpec((8, tm), lambda i: (0, i)),
                   pl.BlockSpec((N_EXPERTS, LANES), lambda i: (0, 0))],
        out_shape=[jax.ShapeDtypeStruct((8, t), F32),
                   jax.ShapeDtypeStruct((N_EXPERTS, LANES), F32)],
        scratch_shapes=[pltpu.VMEM((N_EXPERTS, LANES), F32)],
        compiler_params=_params(1),
        name="router",
    )(h1tm, g.reshape(1, d), wr, br)


def _rows_copy(src_ref, s_row, dst_ref, d_row, n, sem):
    return pltpu.make_async_copy(src_ref.at[pl.ds(s_row, n), :], dst_ref.at[pl.ds(d_row, n), :], sem)


def _expert_kernel(te_ref, nt_ref, pa_ref, pb_ref, h_ref, g_ref, wg_ref, wu_ref, wd_ref, ys_ref,
                   src_ref, gbuf_ref, ybuf_ref, wgu_bf_ref, wd_bf_ref, cur_ref, gsem, ssem):
    r = pl.program_id(0)
    nt = nt_ref[0]
    n_tok = pa_ref.shape[0]
    f = wg_ref.shape[1]
    slabs = g_ref.shape[1] // LANES
    buf_rows = MOE_TILE * STAGE_PITCH
    tile_rows = MOE_TILE * slabs
    n_spare = 2 * MOE_TILE

    def gather(q, slot, unrolled):
        def one(i):
            tok = src_ref[q * MOE_TILE + i] & (n_tok - 1)
            _rows_copy(h_ref, tok * slabs, gbuf_ref, slot * buf_rows + i * STAGE_PITCH, slabs,
                       gsem.at[slot]).start()
        if unrolled:
            for i in range(MOE_TILE):
                one(i)
        else:
            lax.fori_loop(0, MOE_TILE, lambda i, c: (one(i), c)[1], 0)

    def tile_wait(src, dst, sem):
        pltpu.make_async_copy(src.at[pl.ds(0, tile_rows), :], dst.at[pl.ds(0, tile_rows), :], sem).wait()

    @pl.when(r == 0)
    def _():
        cur_ref[0] = -1

        def spare(q, c):
            src_ref[q] = 2 * n_tok + (q & (n_spare - 1))
            return c

        lax.fori_loop(0, src_ref.shape[0], spare, 0)

        def assign(t, c):
            src_ref[pa_ref[t]] = t
            src_ref[pb_ref[t]] = n_tok + t
            return c

        lax.fori_loop(0, n_tok, assign, 0)
        ybuf_ref[...] = jnp.zeros_like(ybuf_ref)
        for s in range(2):
            pltpu.make_async_copy(
                ybuf_ref.at[pl.ds(s * buf_rows, tile_rows), :],
                ys_ref.at[pl.ds((2 * n_tok + s * MOE_TILE) * slabs, tile_rows), :], ssem.at[s]).start()
        gather(0, 0, unrolled=False)

    @pl.when(r < nt)
    def _():
        e = te_ref[r]
        slot = r % 2

        @pl.when(cur_ref[0] != e)
        def _():
            wgu_bf_ref[:, 0:f] = wg_ref[...].astype(BF16)
            wgu_bf_ref[:, f:2 * f] = wu_ref[...].astype(BF16)
            wd_bf_ref[...] = wd_ref[...].astype(BF16)
            cur_ref[0] = e

        gather(jnp.minimum(r + 1, nt - 1), 1 - slot, unrolled=True)
        tile_wait(h_ref, gbuf_ref, gsem.at[slot])
        x = _load_token_major(gbuf_ref, slot * buf_rows, MOE_TILE, slabs, STAGE_PITCH)
        ms = jnp.mean(x * x, axis=-1, keepdims=True)
        xn = (x * lax.rsqrt(ms + EPS) * g_ref[...]).astype(BF16)
        gu = _dot(xn, wgu_bf_ref[...])
        hid = (jax.nn.silu(gu[:, 0:f]) * gu[:, f:2 * f]).astype(BF16)
        y = _dot(hid, wd_bf_ref[...])
        tile_wait(ybuf_ref, ys_ref, ssem.at[slot])
        _store_token_major(ybuf_ref, slot * buf_rows, y, slabs, STAGE_PITCH)
        for i in range(MOE_TILE):
            code = src_ref[r * MOE_TILE + i]
            _rows_copy(ybuf_ref, slot * buf_rows + i * STAGE_PITCH, ys_ref, code * slabs, slabs,
                       ssem.at[slot]).start()

        @pl.when(r == nt - 1)
        def _():
            tile_wait(h_ref, gbuf_ref, gsem.at[1 - slot])
            tile_wait(ybuf_ref, ys_ref, ssem.at[0])
            tile_wait(ybuf_ref, ys_ref, ssem.at[1])


def _experts(tile_e, n_tiles, pos_a, pos_b, h1tm, g, w_gate, w_up, w_down, n_rows):
    t = pos_a.shape[0]
    assert t & (t - 1) == 0, "row codes are decoded with a power-of-two mask"
    d = g.shape[0]
    slabs = d // LANES
    f = w_gate.shape[2]
    r_max = n_rows // MOE_TILE
    any_spec = pl.BlockSpec(memory_space=pl.ANY)
    grid_spec = pltpu.PrefetchScalarGridSpec(
        num_scalar_prefetch=4,
        grid=(r_max,),
        in_specs=[any_spec,
                  pl.BlockSpec((1, d), lambda r, te, *_: (0, 0)),
                  pl.BlockSpec((None, d, f), lambda r, te, *_: (te[r], 0, 0)),
                  pl.BlockSpec((None, d, f), lambda r, te, *_: (te[r], 0, 0)),
                  pl.BlockSpec((None, f, d), lambda r, te, *_: (te[r], 0, 0))],
        out_specs=any_spec,
        scratch_shapes=[pltpu.SMEM((n_rows,), jnp.int32),
                        pltpu.VMEM((2 * MOE_TILE * STAGE_PITCH, LANES), F32),
                        pltpu.VMEM((2 * MOE_TILE * STAGE_PITCH, LANES), F32),
                        pltpu.VMEM((d, 2 * f), BF16),
                        pltpu.VMEM((f, d), BF16),
                        pltpu.SMEM((1,), jnp.int32),
                        pltpu.SemaphoreType.DMA((2,)),
                        pltpu.SemaphoreType.DMA((2,))],
    )
    return pl.pallas_call(
        _expert_kernel,
        grid_spec=grid_spec,
        out_shape=jax.ShapeDtypeStruct(((2 * t + 2 * MOE_TILE) * slabs, LANES), F32),
        compiler_params=_params(1),
        name="experts",
    )(tile_e, n_tiles, pos_a, pos_b, h1tm, g.reshape(1, d), w_gate, w_up, w_down)


def _combine_kernel(ya_ref, yb_ref, h_ref, wm_ref, g_ref, o_ref):
    tm, d = o_ref.shape
    slabs = d // LANES
    w = wm_ref[...]
    h2 = (_load_token_major(h_ref, 0, tm, slabs)
          + w[:, 0:1] * _load_token_major(ya_ref, 0, tm, slabs)
          + w[:, 1:2] * _load_token_major(yb_ref, 0, tm, slabs))
    ms = jnp.mean(h2 * h2, axis=-1, keepdims=True)
    o_ref[...] = h2 * lax.rsqrt(ms + EPS) * g_ref[...]


def _combine(ys, h1tm, wmeta, g, tm=512):
    t = wmeta.shape[0]
    d = g.shape[0]
    slabs = d // LANES
    nb = t // tm
    return pl.pallas_call(
        _combine_kernel,
        grid=(nb,),
        in_specs=[pl.BlockSpec((tm * slabs, LANES), lambda c: (c, 0)),
                  pl.BlockSpec((tm * slabs, LANES), lambda c: (nb + c, 0)),
                  pl.BlockSpec((tm * slabs, LANES), lambda c: (c, 0)),
                  pl.BlockSpec((tm, LANES), lambda c: (c, 0)),
                  pl.BlockSpec((1, d), lambda c: (0, 0))],
        out_specs=pl.BlockSpec((tm, d), lambda c: (c, 0)),
        out_shape=jax.ShapeDtypeStruct((t, d), F32),
        compiler_params=_params(1),
        name="combine",
    )(ys, ys, h1tm, wmeta, g.reshape(1, d))


def _routing_plan(meta, counts, t):
    ea = meta[0].astype(jnp.int32)
    eb = meta[1].astype(jnp.int32)
    cnt = counts[:, 0].astype(jnp.int32)
    tiles_e = (cnt + MOE_TILE - 1) // MOE_TILE
    tile_end = jnp.cumsum(tiles_e)
    row_start = (tile_end - tiles_e) * MOE_TILE
    pos_a = row_start[ea] + meta[2].astype(jnp.int32)
    pos_b = row_start[eb] + meta[3].astype(jnp.int32)
    n_tiles = tile_end[-1]
    r_max = (2 * t) // MOE_TILE + N_EXPERTS
    r = jnp.minimum(jnp.arange(r_max, dtype=jnp.int32), n_tiles - 1)
    tile_e = jnp.sum((tile_end[None, :] <= r[:, None]).astype(jnp.int32), axis=1)
    wmeta = jnp.pad(meta[4:6].T, ((0, 0), (0, LANES - 2)))
    return pos_a, pos_b, tile_e, n_tiles.reshape(1), wmeta, r_max * MOE_TILE


def _layer(h, norm_mix_g, w_in, conv_w, sgu_ln_g, sgu_ln_b, sgu_w_s, sgu_b_s, w_up_conv,
           w_up_sgu, w_out, norm_ffn_g, w_rg, b_rg, w_re, b_re, w_eg, w_eu, w_ed, out_g):
    t, d = h.shape
    conv_width = conv_w.shape[1]
    sgu_width = sgu_ln_g.shape[0]
    xn = _rmsnorm(h, norm_mix_g)
    ya = _conv_branch(xn, w_in, conv_w, conv_width)
    gz = _zproj(xn, w_in, 3 * conv_width, 2 * sgu_width)
    yb = _sgu_mix(gz, sgu_ln_g, sgu_ln_b, sgu_w_s, sgu_b_s)
    m = _upgate(xn, ya, yb, w_in, 3 * conv_width + 2 * sgu_width, w_up_conv, w_up_sgu)
    h1tm = _outproj(m, w_out, h)
    meta, counts = _router(h1tm, norm_ffn_g, w_rg, b_rg, w_re, b_re)
    pos_a, pos_b, tile_e, n_tiles, wmeta, n_rows = _routing_plan(meta, counts, t)
    ys = _experts(tile_e, n_tiles, pos_a, pos_b, h1tm, norm_ffn_g, w_eg, w_eu, w_ed, n_rows)
    return _combine(ys, h1tm, wmeta, out_g)


def kernel(x, norm_mix_g, w_in, conv_w, sgu_ln_g, sgu_ln_b, sgu_w_s, sgu_b_s, w_up_conv, w_up_sgu, w_out, norm_ffn_g, w_router_group, b_router_group, w_router_expert, b_router_expert, w_exp_gate, w_exp_up, w_exp_down, norm_final_g):
    bsz, s, d = x.shape
    depth = w_in.shape[0]
    assert bsz == 1 and depth == 1, "causal conv carry and the fused final norm assume one sequence, one layer"
    out = _layer(x.reshape(s, d), norm_mix_g[0], w_in[0], conv_w[0], sgu_ln_g[0], sgu_ln_b[0],
                 sgu_w_s[0], sgu_b_s[0], w_up_conv[0], w_up_sgu[0], w_out[0], norm_ffn_g[0],
                 w_router_group[0], b_router_group[0], w_router_expert[0], b_router_expert[0],
                 w_exp_gate[0], w_exp_up[0], w_exp_down[0], norm_final_g)
    return out.reshape(bsz, s, d)
```

```python
import functools

import jax
import jax.numpy as jnp
from jax import lax
from jax.experimental import pallas as pl
from jax.experimental.pallas import tpu as pltpu

F32 = jnp.float32
BF16 = jnp.bfloat16

EPS = 1e-6
CHUNK = 64
CONV_K = 3
SGU_HEADS = 8
SGU_BLOCK = 128
N_GROUPS = 4
EXPERTS_PER_GROUP = 4
N_EXPERTS = N_GROUPS * EXPERTS_PER_GROUP
ROUTER_ROWS = 32
LANES = 128

VMEM_LIMIT_BYTES = 56 * 1024 * 1024

ROUTE_BLOCK = 512
GROUP = 16
LOCAL_ROWS = -(-(2 * ROUTE_BLOCK + N_EXPERTS * (GROUP - 1)) // 256) * 256
GROUPS_PER_BLOCK = LOCAL_ROWS // GROUP
MOE_TILE = 256
TILE_GROUPS = MOE_TILE // GROUP


def _params(n_axes):
    return pltpu.CompilerParams(
        dimension_semantics=("arbitrary",) * n_axes,
        vmem_limit_bytes=VMEM_LIMIT_BYTES)


def _dot(a, b):
    return jnp.dot(a, b, preferred_element_type=F32)


def _rms_scale(x, g):
    ms = jnp.mean(x * x, axis=-1, keepdims=True)
    return x * lax.rsqrt(ms + EPS) * g


def _rmsnorm_kernel(x_ref, g_ref, o_ref):
    o_ref[...] = _rms_scale(x_ref[...], g_ref[...]).astype(o_ref.dtype)


def _rmsnorm(x, g, tm=512):
    t, d = x.shape
    return pl.pallas_call(
        _rmsnorm_kernel,
        grid=(t // tm,),
        in_specs=[pl.BlockSpec((tm, d), lambda i: (i, 0)),
                  pl.BlockSpec((1, d), lambda i: (0, 0))],
        out_specs=pl.BlockSpec((tm, d), lambda i: (i, 0)),
        out_shape=jax.ShapeDtypeStruct((t, d), BF16),
        compiler_params=_params(1),
        name="rmsnorm",
    )(x, g.reshape(1, d))


def _conv_kernel(xn_ref, wb_ref, wc_ref, wh_ref, cw_ref, o_ref, wbf_ref, carry_ref):
    i = pl.program_id(1)
    tn = wb_ref.shape[1]
    tm = xn_ref.shape[0]

    @pl.when(i == 0)
    def _():
        wbf_ref[:, 0:tn] = wb_ref[...].astype(BF16)
        wbf_ref[:, tn:2 * tn] = wc_ref[...].astype(BF16)
        wbf_ref[:, 2 * tn:3 * tn] = wh_ref[...].astype(BF16)
        carry_ref[...] = jnp.zeros_like(carry_ref)

    proj = _dot(xn_ref[...], wbf_ref[...])
    b = proj[:, 0:tn]
    p = proj[:, tn:2 * tn] * proj[:, 2 * tn:3 * tn]
    prev = carry_ref[...]
    carry_ref[...] = p[tm - 8:tm, :]
    row = lax.broadcasted_iota(jnp.int32, p.shape, 0)
    p1 = jnp.where(row == 0, prev[7:8, :], pltpu.roll(p, 1, axis=0))
    p2 = jnp.where(row == 0, prev[6:7, :],
                   jnp.where(row == 1, prev[7:8, :], pltpu.roll(p, 2, axis=0)))
    cw = cw_ref[...]
    y = b * (cw[0:1, :] * p2 + cw[1:2, :] * p1 + cw[2:3, :] * p)
    o_ref[...] = y.astype(o_ref.dtype)


def _conv_branch(xn, w_in, conv_w, width, tm=1024, tn=256):
    t, d = xn.shape
    nj = width // tn
    return pl.pallas_call(
        _conv_kernel,
        grid=(nj, t // tm),
        in_specs=[pl.BlockSpec((tm, d), lambda j, i: (i, 0)),
                  pl.BlockSpec((d, tn), lambda j, i: (0, j)),
                  pl.BlockSpec((d, tn), lambda j, i: (0, nj + j)),
                  pl.BlockSpec((d, tn), lambda j, i: (0, 2 * nj + j)),
                  pl.BlockSpec((CONV_K, tn), lambda j, i: (0, j))],
        out_specs=pl.BlockSpec((tm, tn), lambda j, i: (i, j)),
        out_shape=jax.ShapeDtypeStruct((t, width), BF16),
        scratch_shapes=[pltpu.VMEM((d, 3 * tn), BF16),
                        pltpu.VMEM((8, tn), F32)],
        compiler_params=_params(2),
        name="conv_branch",
    )(xn, w_in, w_in, w_in, conv_w)


def _zproj_kernel(xn_ref, w_ref, o_ref, wbf_ref):
    @pl.when(pl.program_id(1) == 0)
    def _():
        wbf_ref[...] = w_ref[...].astype(BF16)

    z = _dot(xn_ref[...], wbf_ref[...])
    o_ref[...] = jax.nn.gelu(z).astype(o_ref.dtype)


def _zproj(xn, w_in, col0, width, tm=1024, tn=512):
    t, d = xn.shape
    c0 = col0 // tn
    return pl.pallas_call(
        _zproj_kernel,
        grid=(width // tn, t // tm),
        in_specs=[pl.BlockSpec((tm, d), lambda j, i: (i, 0)),
                  pl.BlockSpec((d, tn), lambda j, i: (0, c0 + j))],
        out_specs=pl.BlockSpec((tm, tn), lambda j, i: (i, j)),
        out_shape=jax.ShapeDtypeStruct((t, width), BF16),
        scratch_shapes=[pltpu.VMEM((d, tn), BF16)],
        compiler_params=_params(2),
        name="sgu_zproj",
    )(xn, w_in)


def _sgu_kernel(gz_ref, lng_ref, lnb_ref, ws_ref, bsx_ref, o_ref):
    tm = gz_ref.shape[0]
    w = o_ref.shape[1]
    hd = w // SGU_HEADS
    v = gz_ref[:, w:2 * w].astype(F32)
    mu = jnp.mean(v, axis=-1, keepdims=True)
    vc = v - mu
    var = jnp.mean(vc * vc, axis=-1, keepdims=True)
    vn = (vc * lax.rsqrt(var + EPS) * lng_ref[...] + lnb_ref[...]).astype(BF16)
    ii = lax.broadcasted_iota(jnp.int32, (SGU_BLOCK, SGU_BLOCK), 0)
    jj = lax.broadcasted_iota(jnp.int32, (SGU_BLOCK, SGU_BLOCK), 1)
    mask = (jj // CHUNK) <= (ii // CHUNK)
    for h in range(SGU_HEADS):
        wm = jnp.where(mask, ws_ref[h], 0.0).astype(BF16)
        cs = slice(h * hd, (h + 1) * hd)
        for n in range(tm // SGU_BLOCK):
            rs = slice(n * SGU_BLOCK, (n + 1) * SGU_BLOCK)
            vm = _dot(wm, vn[rs, cs]) + bsx_ref[:, cs]
            o_ref[rs, cs] = (gz_ref[rs, cs].astype(F32) * vm).astype(o_ref.dtype)


def _sgu_mix(gz, ln_g, ln_b, w_s, b_s, tm=512):
    t, w2 = gz.shape
    w = w2 // 2
    hd = w // SGU_HEADS
    bsx = jnp.repeat(b_s.T, hd, axis=1)
    return pl.pallas_call(
        _sgu_kernel,
        grid=(t // tm,),
        in_specs=[pl.BlockSpec((tm, w2), lambda i: (i, 0)),
                  pl.BlockSpec((1, w), lambda i: (0, 0)),
                  pl.BlockSpec((1, w), lambda i: (0, 0)),
                  pl.BlockSpec((SGU_HEADS, SGU_BLOCK, SGU_BLOCK), lambda i: (0, 0, 0)),
                  pl.BlockSpec((SGU_BLOCK, w), lambda i: (0, 0))],
        out_specs=pl.BlockSpec((tm, w), lambda i: (i, 0)),
        out_shape=jax.ShapeDtypeStruct((t, w), BF16),
        compiler_params=_params(1),
        name="sgu_mix",
    )(gz, ln_g.reshape(1, w), ln_b.reshape(1, w), w_s, bsx)


def _upgate_kernel(xn_ref, ya_ref, yb_ref, wgc_ref, wgs_ref, wua_ref, wub_ref, o_ref,
                   wgbf_ref, wuabf_ref, wubbf_ref):
    tn = wgc_ref.shape[1]

    @pl.when(pl.program_id(1) == 0)
    def _():
        wgbf_ref[:, 0:tn] = wgc_ref[...].astype(BF16)
        wgbf_ref[:, tn:2 * tn] = wgs_ref[...].astype(BF16)
        wuabf_ref[...] = wua_ref[...].astype(BF16)
        wubbf_ref[...] = wub_ref[...].astype(BF16)

    gl = _dot(xn_ref[...], wgbf_ref[...])
    a = _dot(ya_ref[...], wuabf_ref[...])
    b = _dot(yb_ref[...], wubbf_ref[...])
    m = jax.nn.sigmoid(gl[:, 0:tn]) * a + jax.nn.sigmoid(gl[:, tn:2 * tn]) * b
    o_ref[...] = m.astype(o_ref.dtype)


def _upgate(xn, ya, yb, w_in, gate_col0, w_up_a, w_up_b, tm=512, tn=512):
    t, d = xn.shape
    wa = ya.shape[1]
    wb = yb.shape[1]
    dout = w_up_a.shape[1]
    c0 = gate_col0 // tn
    nj = dout // tn
    return pl.pallas_call(
        _upgate_kernel,
        grid=(nj, t // tm),
        in_specs=[pl.BlockSpec((tm, d), lambda j, i: (i, 0)),
                  pl.BlockSpec((tm, wa), lambda j, i: (i, 0)),
                  pl.BlockSpec((tm, wb), lambda j, i: (i, 0)),
                  pl.BlockSpec((d, tn), lambda j, i: (0, c0 + j)),
                  pl.BlockSpec((d, tn), lambda j, i: (0, c0 + nj + j)),
                  pl.BlockSpec((wa, tn), lambda j, i: (0, j)),
                  pl.BlockSpec((wb, tn), lambda j, i: (0, j))],
        out_specs=pl.BlockSpec((tm, tn), lambda j, i: (i, j)),
        out_shape=jax.ShapeDtypeStruct((t, dout), BF16),
        scratch_shapes=[pltpu.VMEM((d, 2 * tn), BF16),
                        pltpu.VMEM((wa, tn), BF16),
                        pltpu.VMEM((wb, tn), BF16)],
        compiler_params=_params(2),
        name="upgate",
    )(xn, ya, yb, w_in, w_in, w_up_a, w_up_b)


def _outproj_kernel(m_ref, w_ref, x_ref, o_ref, wbf_ref):
    @pl.when(pl.program_id(1) == 0)
    def _():
        wbf_ref[...] = w_ref[...].astype(BF16)

    o_ref[...] = x_ref[...] + _dot(m_ref[...], wbf_ref[...])


def _outproj(m, w_out, x, tm=512, tn=512):
    t, d = m.shape
    dout = w_out.shape[1]
    return pl.pallas_call(
        _outproj_kernel,
        grid=(dout // tn, t // tm),
        in_specs=[pl.BlockSpec((tm, d), lambda j, i: (i, 0)),
                  pl.BlockSpec((d, tn), lambda j, i: (0, j)),
                  pl.BlockSpec((tm, tn), lambda j, i: (i, j))],
        out_specs=pl.BlockSpec((tm, tn), lambda j, i: (i, j)),
        out_shape=jax.ShapeDtypeStruct((t, dout), F32),
        scratch_shapes=[pltpu.VMEM((d, tn), BF16)],
        compiler_params=_params(2),
        name="outproj",
    )(m, w_out, x)


def _argmax_rows(rows):
    best = rows[0]
    idx = jnp.zeros(rows[0].shape, jnp.int32)
    for k in range(1, len(rows)):
        better = rows[k] > best
        best = jnp.where(better, rows[k], best)
        idx = jnp.where(better, k, idx)
    return best, idx


def _softmax_rows(rows):
    mx = functools.reduce(jnp.maximum, rows)
    ex = [jnp.exp(r - mx) for r in rows]
    den = functools.reduce(lambda a, b: a + b, ex)
    return [e / den for e in ex]


def _route_sort_kernel(h_ref, g_ref, wr_ref, br_ref, xs_ref, meta_ref, cnt_ref):
    tm = h_ref.shape[0]
    xn = _rms_scale(h_ref[...], g_ref[...])
    lt = lax.dot_general(wr_ref[...], xn, (((1,), (1,)), ((), ())),
                         precision=lax.Precision.HIGHEST,
                         preferred_element_type=F32) + br_ref[...]
    pgs = _softmax_rows([lt[k:k + 1, :] for k in range(N_GROUPS)])
    pg, gi = _argmax_rows(pgs)
    sel = []
    for k in range(EXPERTS_PER_GROUP):
        r = jnp.zeros_like(pg)
        for g in range(N_GROUPS):
            row = N_GROUPS + g * EXPERTS_PER_GROUP + k
            r = jnp.where(gi == g, lt[row:row + 1, :], r)
        sel.append(r)
    pes = _softmax_rows(sel)
    p1, e1 = _argmax_rows(pes)
    rest = [jnp.where(e1 == k, -1.0, pes[k]) for k in range(EXPERTS_PER_GROUP)]
    p2, e2 = _argmax_rows(rest)
    den = p1 + p2
    w1 = pg * (p1 / den)
    w2 = pg * (p2 / den)
    lo = jnp.minimum(e1, e2)
    hi = jnp.maximum(e1, e2)
    w_lo = jnp.where(e1 < e2, w1, w2)
    w_hi = jnp.where(e1 < e2, w2, w1)
    ea = gi * EXPERTS_PER_GROUP + lo
    eb = gi * EXPERTS_PER_GROUP + hi

    erow = lax.broadcasted_iota(jnp.int32, (N_EXPERTS, tm), 0)
    oh_a = (erow == ea).astype(F32)
    oh_b = (erow == eb).astype(F32)
    a = lax.broadcasted_iota(jnp.int32, (tm, tm), 0)
    b = lax.broadcasted_iota(jnp.int32, (tm, tm), 1)
    before = (a < b).astype(BF16)
    cum = _dot((oh_a + oh_b).astype(BF16), before)
    cnt = jnp.sum(oh_a + oh_b, axis=1, keepdims=True)
    padded = jnp.floor((cnt + (GROUP - 1)) * (1.0 / GROUP)) * GROUP
    pos_a = jnp.sum(oh_a * cum + jnp.where(erow < ea, padded, 0.0), axis=0, keepdims=True)
    pos_b = jnp.sum(oh_b * cum + jnp.where(erow < eb, padded, 0.0), axis=0, keepdims=True)

    q = lax.broadcasted_iota(jnp.int32, (xs_ref.shape[0], tm), 0)
    perm = jnp.where((q == pos_a.astype(jnp.int32)) | (q == pos_b.astype(jnp.int32)), 1.0, 0.0)
    xs_ref[...] = _dot(perm.astype(BF16), xn.astype(BF16)).astype(xs_ref.dtype)

    cnt_ref[...] = jnp.broadcast_to(cnt, cnt_ref.shape)
    meta_ref[0:1, :] = pos_a
    meta_ref[1:2, :] = pos_b
    meta_ref[2:3, :] = w_lo
    meta_ref[3:4, :] = w_hi
    meta_ref[4:8, :] = jnp.zeros((4, tm), F32)


def _route_sort(h1, g, w_rg, b_rg, w_re, b_re):
    t, d = h1.shape
    tm = ROUTE_BLOCK
    nb = t // tm
    n_log = w_rg.shape[1] + w_re.shape[1]
    wr = jnp.concatenate([w_rg, w_re], axis=1).T
    wr = jnp.pad(wr, ((0, ROUTER_ROWS - n_log), (0, 0)))
    br = jnp.pad(jnp.concatenate([b_rg, b_re]), (0, ROUTER_ROWS - n_log)).reshape(ROUTER_ROWS, 1)
    return pl.pallas_call(
        _route_sort_kernel,
        grid=(nb,),
        in_specs=[pl.BlockSpec((tm, d), lambda i: (i, 0)),
                  pl.BlockSpec((1, d), lambda i: (0, 0)),
                  pl.BlockSpec((ROUTER_ROWS, d), lambda i: (0, 0)),
                  pl.BlockSpec((ROUTER_ROWS, 1), lambda i: (0, 0))],
        out_specs=[pl.BlockSpec((LOCAL_ROWS, d), lambda i: (i, 0)),
                   pl.BlockSpec((8, tm), lambda i: (0, i)),
                   pl.BlockSpec((N_EXPERTS, LANES), lambda i: (i, 0))],
        out_shape=[jax.ShapeDtypeStruct((nb * LOCAL_ROWS, d), BF16),
                   jax.ShapeDtypeStruct((8, t), F32),
                   jax.ShapeDtypeStruct((nb * N_EXPERTS, LANES), F32)],
        compiler_params=_params(1),
        name="route_sort",
    )(h1, g.reshape(1, d), wr, br)


def _group_copy(src_ref, s_group, dst_ref, d_group, sem):
    return pltpu.make_async_copy(src_ref.at[pl.ds(pl.multiple_of(s_group * GROUP, GROUP), GROUP), :],
                                 dst_ref.at[pl.ds(pl.multiple_of(d_group * GROUP, GROUP), GROUP), :], sem)


def _expert_kernel(te_ref, nt_ref, gsrc_ref, gdst_ref, tail_ref, xs_ref, wg_ref, wu_ref, wd_ref, ys_ref,
                   xbuf_ref, ybuf_ref, zero_ref, wgu_bf_ref, wd_bf_ref, cur_ref, gsem, ssem, zsem):
    r = pl.program_id(0)
    nt = nt_ref[0]
    f = wg_ref.shape[1]

    def gather(q, slot):
        for i in range(TILE_GROUPS):
            _group_copy(xs_ref, gsrc_ref[q * TILE_GROUPS + i], xbuf_ref, slot * TILE_GROUPS + i,
                        gsem.at[slot]).start()

    def tile_wait(src, dst, sem):
        pltpu.make_async_copy(src.at[pl.ds(0, MOE_TILE), :], dst.at[pl.ds(0, MOE_TILE), :], sem).wait()

    @pl.when(r == 0)
    def _():
        cur_ref[0] = -1
        zero_ref[...] = jnp.zeros_like(zero_ref)
        n_blocks = tail_ref.shape[0]

        def fill(make):
            def body(g, c):
                make(g)
                return c
            return body

        for blk in range(n_blocks):
            lax.fori_loop(tail_ref[blk], GROUPS_PER_BLOCK, fill(
                lambda g, blk=blk: _group_copy(zero_ref, 0, ys_ref, blk * GROUPS_PER_BLOCK + g, zsem).start()), 0)
        for blk in range(n_blocks):
            lax.fori_loop(tail_ref[blk], GROUPS_PER_BLOCK, fill(
                lambda g, blk=blk: _group_copy(zero_ref, 0, ys_ref, blk * GROUPS_PER_BLOCK + g, zsem).wait()), 0)
        gather(0, 0)

    @pl.when(r < nt)
    def _():
        e = te_ref[r]
        slot = r % 2

        @pl.when(cur_ref[0] != e)
        def _():
            wgu_bf_ref[:, 0:f] = wg_ref[...].astype(BF16)
            wgu_bf_ref[:, f:2 * f] = wu_ref[...].astype(BF16)
            wd_bf_ref[...] = wd_ref[...].astype(BF16)
            cur_ref[0] = e

        gather(jnp.minimum(r + 1, nt - 1), 1 - slot)
        tile_wait(xs_ref, xbuf_ref, gsem.at[slot])
        row0 = pl.multiple_of(slot * MOE_TILE, MOE_TILE)
        gu = _dot(xbuf_ref[pl.ds(row0, MOE_TILE), :], wgu_bf_ref[...])
        hid = (jax.nn.silu(gu[:, 0:f]) * gu[:, f:2 * f]).astype(BF16)
        y = _dot(hid, wd_bf_ref[...]).astype(ybuf_ref.dtype)

        @pl.when(r >= 2)
        def _():
            tile_wait(ybuf_ref, ys_ref, ssem.at[slot])

        ybuf_ref[pl.ds(row0, MOE_TILE), :] = y
        for i in range(TILE_GROUPS):
            _group_copy(ybuf_ref, slot * TILE_GROUPS + i, ys_ref, gdst_ref[r * TILE_GROUPS + i],
                        ssem.at[slot]).start()

        @pl.when(r == nt - 1)
        def _():
            tile_wait(xs_ref, xbuf_ref, gsem.at[1 - slot])
            tile_wait(ybuf_ref, ys_ref, ssem.at[slot])

            @pl.when(r >= 1)
            def _():
                tile_wait(ybuf_ref, ys_ref, ssem.at[1 - slot])


def _experts(plan, xs, w_gate, w_up, w_down):
    d = xs.shape[1]
    f = w_gate.shape[2]
    tile_e, n_tiles, gsrc, gdst, tail = plan
    r_max = tile_e.shape[0]
    n_blocks = tail.shape[0]
    any_spec = pl.BlockSpec(memory_space=pl.ANY)
    grid_spec = pltpu.PrefetchScalarGridSpec(
        num_scalar_prefetch=5,
        grid=(r_max,),
        in_specs=[any_spec,
                  pl.BlockSpec((None, d, f), lambda r, te, *_: (te[r], 0, 0)),
                  pl.BlockSpec((None, d, f), lambda r, te, *_: (te[r], 0, 0)),
                  pl.BlockSpec((None, f, d), lambda r, te, *_: (te[r], 0, 0))],
        out_specs=any_spec,
        scratch_shapes=[pltpu.VMEM((2 * MOE_TILE, d), BF16),
                        pltpu.VMEM((2 * MOE_TILE, d), BF16),
                        pltpu.VMEM((GROUP, d), BF16),
                        pltpu.VMEM((d, 2 * f), BF16),
                        pltpu.VMEM((f, d), BF16),
                        pltpu.SMEM((1,), jnp.int32),
                        pltpu.SemaphoreType.DMA((2,)),
                        pltpu.SemaphoreType.DMA((2,)),
                        pltpu.SemaphoreType.DMA(())],
    )
    return pl.pallas_call(
        _expert_kernel,
        grid_spec=grid_spec,
        out_shape=jax.ShapeDtypeStruct((n_blocks * LOCAL_ROWS, d), BF16),
        compiler_params=_params(1),
        name="experts",
    )(tile_e, n_tiles, gsrc, gdst, tail, xs, w_gate, w_up, w_down)


def _combine_kernel(ys_ref, h_ref, cm_ref, g_ref, o_ref):
    tm = h_ref.shape[0]
    cm = cm_ref[...]
    q = lax.broadcasted_iota(jnp.int32, (tm, ys_ref.shape[0]), 1)
    ys = ys_ref[...]
    sel_a = jnp.where(q == cm[:, 0:1].astype(jnp.int32), 1.0, 0.0).astype(BF16)
    sel_b = jnp.where(q == cm[:, 1:2].astype(jnp.int32), 1.0, 0.0).astype(BF16)
    h2 = h_ref[...] + cm[:, 2:3] * _dot(sel_a, ys) + cm[:, 3:4] * _dot(sel_b, ys)
    o_ref[...] = _rms_scale(h2, g_ref[...])


def _combine(ys, h1, cmeta, g):
    t, d = h1.shape
    tm = ROUTE_BLOCK
    return pl.pallas_call(
        _combine_kernel,
        grid=(t // tm,),
        in_specs=[pl.BlockSpec((LOCAL_ROWS, d), lambda i: (i, 0)),
                  pl.BlockSpec((tm, d), lambda i: (i, 0)),
                  pl.BlockSpec((tm, LANES), lambda i: (i, 0)),
                  pl.BlockSpec((1, d), lambda i: (0, 0))],
        out_specs=pl.BlockSpec((tm, d), lambda i: (i, 0)),
        out_shape=jax.ShapeDtypeStruct((t, d), F32),
        compiler_params=_params(1),
        name="combine",
    )(ys, h1, cmeta, g.reshape(1, d))


def _expert_plan(counts, n_blocks):
    cnt = counts[:, 0].astype(jnp.int32).reshape(n_blocks, N_EXPERTS)
    groups = (cnt + GROUP - 1) // GROUP
    first = jnp.cumsum(groups, axis=1) - groups
    upto = jnp.cumsum(groups, axis=0)
    per_expert = upto[-1]
    tiles_e = (per_expert + TILE_GROUPS - 1) // TILE_GROUPS
    tile_end = jnp.cumsum(tiles_e)
    n_tiles = tile_end[-1]
    max_groups = 2 * ROUTE_BLOCK * n_blocks // GROUP + n_blocks * N_EXPERTS
    r_max = max_groups // TILE_GROUPS + N_EXPERTS
    tile = jnp.minimum(jnp.arange(r_max, dtype=jnp.int32), n_tiles - 1)
    tile_e = jnp.sum((tile_end[None, :] <= tile[:, None]).astype(jnp.int32), axis=1)

    slot = jnp.arange(r_max * TILE_GROUPS, dtype=jnp.int32)
    s_tile = slot // TILE_GROUPS
    s_e = tile_e[jnp.minimum(s_tile, n_tiles - 1)]
    k = slot - ((tile_end - tiles_e) * TILE_GROUPS)[s_e]
    real = (k < per_expert[s_e]) & (s_tile < n_tiles)
    upto_e = upto.T[s_e]
    blk = jnp.sum((upto_e <= k[:, None]).astype(jnp.int32), axis=1)
    blk = jnp.minimum(blk, n_blocks - 1)
    before = jnp.take_along_axis(upto_e, blk[:, None], axis=1)[:, 0] - groups[blk, s_e]
    src = blk * GROUPS_PER_BLOCK + first[blk, s_e] + (k - before)
    zero_group = GROUPS_PER_BLOCK - 1
    spare = n_blocks * GROUPS_PER_BLOCK + (s_tile % 2) * TILE_GROUPS + slot % TILE_GROUPS
    gsrc = jnp.where(real, src, zero_group)
    gdst = jnp.where(real, src, spare)
    tail = jnp.concatenate([jnp.sum(groups, axis=1), jnp.zeros((1,), jnp.int32)])
    return tile_e, n_tiles.reshape(1), gsrc, gdst, tail


def _layer(h, norm_mix_g, w_in, conv_w, sgu_ln_g, sgu_ln_b, sgu_w_s, sgu_b_s, w_up_conv,
           w_up_sgu, w_out, norm_ffn_g, w_rg, b_rg, w_re, b_re, w_eg, w_eu, w_ed, out_g):
    t, d = h.shape
    conv_width = conv_w.shape[1]
    sgu_width = sgu_ln_g.shape[0]
    xn = _rmsnorm(h, norm_mix_g)
    ya = _conv_branch(xn, w_in, conv_w, conv_width)
    gz = _zproj(xn, w_in, 3 * conv_width, 2 * sgu_width)
    yb = _sgu_mix(gz, sgu_ln_g, sgu_ln_b, sgu_w_s, sgu_b_s)
    m = _upgate(xn, ya, yb, w_in, 3 * conv_width + 2 * sgu_width, w_up_conv, w_up_sgu)
    h1 = _outproj(m, w_out, h)
    xs, meta, counts = _route_sort(h1, norm_ffn_g, w_rg, b_rg, w_re, b_re)
    ys = _experts(_expert_plan(counts, t // ROUTE_BLOCK), xs, w_eg, w_eu, w_ed)
    cmeta = jnp.pad(meta[0:4].T, ((0, 0), (0, LANES - 4)))
    return _combine(ys, h1, cmeta, out_g)


def kernel(x, norm_mix_g, w_in, conv_w, sgu_ln_g, sgu_ln_b, sgu_w_s, sgu_b_s, w_up_conv, w_up_sgu, w_out, norm_ffn_g, w_router_group, b_router_group, w_router_expert, b_router_expert, w_exp_gate, w_exp_up, w_exp_down, norm_final_g):
    bsz, s, d = x.shape
    depth = w_in.shape[0]
    assert bsz == 1 and depth == 1, "causal conv carry and the fused final norm assume one sequence, one layer"
    assert s % ROUTE_BLOCK == 0
    out = _layer(x.reshape(s, d), norm_mix_g[0], w_in[0], conv_w[0], sgu_ln_g[0], sgu_ln_b[0],
                 sgu_w_s[0], sgu_b_s[0], w_up_conv[0], w_up_sgu[0], w_out[0], norm_ffn_g[0],
                 w_router_group[0], b_router_group[0], w_router_expert[0], b_router_expert[0],
                 w_exp_gate[0], w_exp_up[0], w_exp_down[0], norm_final_g)
    return out.reshape(bsz, s, d)
```

```python
import functools

import jax
import jax.numpy as jnp
from jax import lax
from jax.experimental import pallas as pl
from jax.experimental.pallas import tpu as pltpu

F32 = jnp.float32
BF16 = jnp.bfloat16

EPS = 1e-6
CHUNK = 64
CONV_K = 3
SGU_HEADS = 8
SGU_BLOCK = 128
N_GROUPS = 4
EXPERTS_PER_GROUP = 4
N_EXPERTS = N_GROUPS * EXPERTS_PER_GROUP
ROUTER_ROWS = 32
LANES = 128

VMEM_LIMIT_BYTES = 56 * 1024 * 1024

ROUTE_BLOCK = 512
GROUP = 16
LOCAL_ROWS = -(-(2 * ROUTE_BLOCK + N_EXPERTS * (GROUP - 1)) // 256) * 256
GROUPS_PER_BLOCK = LOCAL_ROWS // GROUP
MOE_TILE = 256
TILE_GROUPS = MOE_TILE // GROUP


def _params(n_axes):
    return pltpu.CompilerParams(
        dimension_semantics=("arbitrary",) * n_axes,
        vmem_limit_bytes=VMEM_LIMIT_BYTES)


def _dot(a, b):
    return jnp.dot(a, b, preferred_element_type=F32)


def _rms_scale(x, g):
    ms = jnp.mean(x * x, axis=-1, keepdims=True)
    return x * lax.rsqrt(ms + EPS) * g


def _rmsnorm_kernel(x_ref, g_ref, o_ref):
    o_ref[...] = _rms_scale(x_ref[...], g_ref[...]).astype(o_ref.dtype)


def _rmsnorm(x, g, tm=512):
    t, d = x.shape
    return pl.pallas_call(
        _rmsnorm_kernel,
        grid=(t // tm,),
        in_specs=[pl.BlockSpec((tm, d), lambda i: (i, 0)),
                  pl.BlockSpec((1, d), lambda i: (0, 0))],
        out_specs=pl.BlockSpec((tm, d), lambda i: (i, 0)),
        out_shape=jax.ShapeDtypeStruct((t, d), BF16),
        compiler_params=_params(1),
        name="rmsnorm",
    )(x, g.reshape(1, d))


def _conv_kernel(xn_ref, wb_ref, wc_ref, wh_ref, cw_ref, o_ref, wbf_ref, carry_ref):
    i = pl.program_id(1)
    tn = wb_ref.shape[1]
    tm = xn_ref.shape[0]

    @pl.when(i == 0)
    def _():
        wbf_ref[:, 0:tn] = wb_ref[...].astype(BF16)
        wbf_ref[:, tn:2 * tn] = wc_ref[...].astype(BF16)
        wbf_ref[:, 2 * tn:3 * tn] = wh_ref[...].astype(BF16)
        carry_ref[...] = jnp.zeros_like(carry_ref)

    proj = _dot(xn_ref[...], wbf_ref[...])
    b = proj[:, 0:tn]
    p = proj[:, tn:2 * tn] * proj[:, 2 * tn:3 * tn]
    prev = carry_ref[...]
    carry_ref[...] = p[tm - 8:tm, :]
    row = lax.broadcasted_iota(jnp.int32, p.shape, 0)
    p1 = jnp.where(row == 0, prev[7:8, :], pltpu.roll(p, 1, axis=0))
    p2 = jnp.where(row == 0, prev[6:7, :],
                   jnp.where(row == 1, prev[7:8, :], pltpu.roll(p, 2, axis=0)))
    cw = cw_ref[...]
    y = b * (cw[0:1, :] * p2 + cw[1:2, :] * p1 + cw[2:3, :] * p)
    o_ref[...] = y.astype(o_ref.dtype)


def _conv_branch(xn, w_in, conv_w, width, tm=1024, tn=256):
    t, d = xn.shape
    nj = width // tn
    return pl.pallas_call(
        _conv_kernel,
        grid=(nj, t // tm),
        in_specs=[pl.BlockSpec((tm, d), lambda j, i: (i, 0)),
                  pl.BlockSpec((d, tn), lambda j, i: (0, j)),
                  pl.BlockSpec((d, tn), lambda j, i: (0, nj + j)),
                  pl.BlockSpec((d, tn), lambda j, i: (0, 2 * nj + j)),
                  pl.BlockSpec((CONV_K, tn), lambda j, i: (0, j))],
        out_specs=pl.BlockSpec((tm, tn), lambda j, i: (i, j)),
        out_shape=jax.ShapeDtypeStruct((t, width), BF16),
        scratch_shapes=[pltpu.VMEM((d, 3 * tn), BF16),
                        pltpu.VMEM((8, tn), F32)],
        compiler_params=_params(2),
        name="conv_branch",
    )(xn, w_in, w_in, w_in, conv_w)


def _zproj_kernel(xn_ref, w_ref, o_ref, wbf_ref):
    @pl.when(pl.program_id(1) == 0)
    def _():
        wbf_ref[...] = w_ref[...].astype(BF16)

    z = _dot(xn_ref[...], wbf_ref[...])
    o_ref[...] = jax.nn.gelu(z).astype(o_ref.dtype)


def _zproj(xn, w_in, col0, width, tm=1024, tn=512):
    t, d = xn.shape
    c0 = col0 // tn
    return pl.pallas_call(
        _zproj_kernel,
        grid=(width // tn, t // tm),
        in_specs=[pl.BlockSpec((tm, d), lambda j, i: (i, 0)),
                  pl.BlockSpec((d, tn), lambda j, i: (0, c0 + j))],
        out_specs=pl.BlockSpec((tm, tn), lambda j, i: (i, j)),
        out_shape=jax.ShapeDtypeStruct((t, width), BF16),
        scratch_shapes=[pltpu.VMEM((d, tn), BF16)],
        compiler_params=_params(2),
        name="sgu_zproj",
    )(xn, w_in)


def _sgu_kernel(gz_ref, lng_ref, lnb_ref, ws_ref, bsx_ref, o_ref):
    tm = gz_ref.shape[0]
    w = o_ref.shape[1]
    hd = w // SGU_HEADS
    v = gz_ref[:, w:2 * w].astype(F32)
    mu = jnp.mean(v, axis=-1, keepdims=True)
    vc = v - mu
    var = jnp.mean(vc * vc, axis=-1, keepdims=True)
    vn = (vc * lax.rsqrt(var + EPS) * lng_ref[...] + lnb_ref[...]).astype(BF16)
    ii = lax.broadcasted_iota(jnp.int32, (SGU_BLOCK, SGU_BLOCK), 0)
    jj = lax.broadcasted_iota(jnp.int32, (SGU_BLOCK, SGU_BLOCK), 1)
    mask = (jj // CHUNK) <= (ii // CHUNK)
    for h in range(SGU_HEADS):
        wm = jnp.where(mask, ws_ref[h], 0.0).astype(BF16)
        cs = slice(h * hd, (h + 1) * hd)
        for n in range(tm // SGU_BLOCK):
            rs = slice(n * SGU_BLOCK, (n + 1) * SGU_BLOCK)
            vm = _dot(wm, vn[rs, cs]) + bsx_ref[:, cs]
            o_ref[rs, cs] = (gz_ref[rs, cs].astype(F32) * vm).astype(o_ref.dtype)


def _sgu_mix(gz, ln_g, ln_b, w_s, b_s, tm=512):
    t, w2 = gz.shape
    w = w2 // 2
    hd = w // SGU_HEADS
    bsx = jnp.repeat(b_s.T, hd, axis=1)
    return pl.pallas_call(
        _sgu_kernel,
        grid=(t // tm,),
        in_specs=[pl.BlockSpec((tm, w2), lambda i: (i, 0)),
                  pl.BlockSpec((1, w), lambda i: (0, 0)),
                  pl.BlockSpec((1, w), lambda i: (0, 0)),
                  pl.BlockSpec((SGU_HEADS, SGU_BLOCK, SGU_BLOCK), lambda i: (0, 0, 0)),
                  pl.BlockSpec((SGU_BLOCK, w), lambda i: (0, 0))],
        out_specs=pl.BlockSpec((tm, w), lambda i: (i, 0)),
        out_shape=jax.ShapeDtypeStruct((t, w), BF16),
        compiler_params=_params(1),
        name="sgu_mix",
    )(gz, ln_g.reshape(1, w), ln_b.reshape(1, w), w_s, bsx)


def _upgate_kernel(xn_ref, ya_ref, yb_ref, wgc_ref, wgs_ref, wua_ref, wub_ref, o_ref,
                   wgbf_ref, wuabf_ref, wubbf_ref):
    tn = wgc_ref.shape[1]

    @pl.when(pl.program_id(1) == 0)
    def _():
        wgbf_ref[:, 0:tn] = wgc_ref[...].astype(BF16)
        wgbf_ref[:, tn:2 * tn] = wgs_ref[...].astype(BF16)
        wuabf_ref[...] = wua_ref[...].astype(BF16)
        wubbf_ref[...] = wub_ref[...].astype(BF16)

    gl = _dot(xn_ref[...], wgbf_ref[...])
    a = _dot(ya_ref[...], wuabf_ref[...])
    b = _dot(yb_ref[...], wubbf_ref[...])
    m = jax.nn.sigmoid(gl[:, 0:tn]) * a + jax.nn.sigmoid(gl[:, tn:2 * tn]) * b
    o_ref[...] = m.astype(o_ref.dtype)


def _upgate(xn, ya, yb, w_in, gate_col0, w_up_a, w_up_b, tm=512, tn=512):
    t, d = xn.shape
    wa = ya.shape[1]
    wb = yb.shape[1]
    dout = w_up_a.shape[1]
    c0 = gate_col0 // tn
    nj = dout // tn
    return pl.pallas_call(
        _upgate_kernel,
        grid=(nj, t // tm),
        in_specs=[pl.BlockSpec((tm, d), lambda j, i: (i, 0)),
                  pl.BlockSpec((tm, wa), lambda j, i: (i, 0)),
                  pl.BlockSpec((tm, wb), lambda j, i: (i, 0)),
                  pl.BlockSpec((d, tn), lambda j, i: (0, c0 + j)),
                  pl.BlockSpec((d, tn), lambda j, i: (0, c0 + nj + j)),
                  pl.BlockSpec((wa, tn), lambda j, i: (0, j)),
                  pl.BlockSpec((wb, tn), lambda j, i: (0, j))],
        out_specs=pl.BlockSpec((tm, tn), lambda j, i: (i, j)),
        out_shape=jax.ShapeDtypeStruct((t, dout), BF16),
        scratch_shapes=[pltpu.VMEM((d, 2 * tn), BF16),
                        pltpu.VMEM((wa, tn), BF16),
                        pltpu.VMEM((wb, tn), BF16)],
        compiler_params=_params(2),
        name="upgate",
    )(xn, ya, yb, w_in, w_in, w_up_a, w_up_b)


def _outproj_kernel(m_ref, w_ref, x_ref, o_ref, wbf_ref):
    @pl.when(pl.program_id(1) == 0)
    def _():
        wbf_ref[...] = w_ref[...].astype(BF16)

    o_ref[...] = x_ref[...] + _dot(m_ref[...], wbf_ref[...])


def _outproj(m, w_out, x, tm=512, tn=1024):
    t, d = m.shape
    dout = w_out.shape[1]
    return pl.pallas_call(
        _outproj_kernel,
        grid=(dout // tn, t // tm),
        in_specs=[pl.BlockSpec((tm, d), lambda j, i: (i, 0)),
                  pl.BlockSpec((d, tn), lambda j, i: (0, j)),
                  pl.BlockSpec((tm, tn), lambda j, i: (i, j))],
        out_specs=pl.BlockSpec((tm, tn), lambda j, i: (i, j)),
        out_shape=jax.ShapeDtypeStruct((t, dout), F32),
        scratch_shapes=[pltpu.VMEM((d, tn), BF16)],
        compiler_params=_params(2),
        name="outproj",
    )(m, w_out, x)


def _argmax_rows(rows):
    best = rows[0]
    idx = jnp.zeros(rows[0].shape, jnp.int32)
    for k in range(1, len(rows)):
        better = rows[k] > best
        best = jnp.where(better, rows[k], best)
        idx = jnp.where(better, k, idx)
    return best, idx


def _softmax_rows(rows):
    mx = functools.reduce(jnp.maximum, rows)
    ex = [jnp.exp(r - mx) for r in rows]
    den = functools.reduce(lambda a, b: a + b, ex)
    return [e / den for e in ex]


def _route_sort_kernel(h_ref, g_ref, wr_ref, br_ref, xs_ref, meta_ref, cnt_ref):
    tm = h_ref.shape[0]
    xn = _rms_scale(h_ref[...], g_ref[...])
    lt = lax.dot_general(wr_ref[...], xn, (((1,), (1,)), ((), ())),
                         precision=lax.Precision.HIGHEST,
                         preferred_element_type=F32) + br_ref[...]
    pgs = _softmax_rows([lt[k:k + 1, :] for k in range(N_GROUPS)])
    pg, gi = _argmax_rows(pgs)
    sel = []
    for k in range(EXPERTS_PER_GROUP):
        r = jnp.zeros_like(pg)
        for g in range(N_GROUPS):
            row = N_GROUPS + g * EXPERTS_PER_GROUP + k
            r = jnp.where(gi == g, lt[row:row + 1, :], r)
        sel.append(r)
    pes = _softmax_rows(sel)
    p1, e1 = _argmax_rows(pes)
    rest = [jnp.where(e1 == k, -1.0, pes[k]) for k in range(EXPERTS_PER_GROUP)]
    p2, e2 = _argmax_rows(rest)
    den = p1 + p2
    w1 = pg * (p1 / den)
    w2 = pg * (p2 / den)
    lo = jnp.minimum(e1, e2)
    hi = jnp.maximum(e1, e2)
    w_lo = jnp.where(e1 < e2, w1, w2)
    w_hi = jnp.where(e1 < e2, w2, w1)
    ea = gi * EXPERTS_PER_GROUP + lo
    eb = gi * EXPERTS_PER_GROUP + hi

    erow = lax.broadcasted_iota(jnp.int32, (N_EXPERTS, tm), 0)
    oh_a = (erow == ea).astype(F32)
    oh_b = (erow == eb).astype(F32)
    a = lax.broadcasted_iota(jnp.int32, (tm, tm), 0)
    b = lax.broadcasted_iota(jnp.int32, (tm, tm), 1)
    before = (a < b).astype(BF16)
    cum = _dot((oh_a + oh_b).astype(BF16), before)
    cnt = jnp.sum(oh_a + oh_b, axis=1, keepdims=True)
    padded = jnp.floor((cnt + (GROUP - 1)) * (1.0 / GROUP)) * GROUP
    pos_a = jnp.sum(oh_a * cum + jnp.where(erow < ea, padded, 0.0), axis=0, keepdims=True)
    pos_b = jnp.sum(oh_b * cum + jnp.where(erow < eb, padded, 0.0), axis=0, keepdims=True)

    q = lax.broadcasted_iota(jnp.int32, (xs_ref.shape[0], tm), 0)
    perm = jnp.where((q == pos_a.astype(jnp.int32)) | (q == pos_b.astype(jnp.int32)), 1.0, 0.0)
    xs_ref[...] = _dot(perm.astype(BF16), xn.astype(BF16)).astype(xs_ref.dtype)

    cnt_ref[...] = jnp.broadcast_to(cnt, cnt_ref.shape)
    meta_ref[0:1, :] = pos_a
    meta_ref[1:2, :] = pos_b
    meta_ref[2:3, :] = w_lo
    meta_ref[3:4, :] = w_hi
    meta_ref[4:8, :] = jnp.zeros((4, tm), F32)


def _route_sort(h1, g, w_rg, b_rg, w_re, b_re):
    t, d = h1.shape
    tm = ROUTE_BLOCK
    nb = t // tm
    n_log = w_rg.shape[1] + w_re.shape[1]
    wr = jnp.concatenate([w_rg, w_re], axis=1).T
    wr = jnp.pad(wr, ((0, ROUTER_ROWS - n_log), (0, 0)))
    br = jnp.pad(jnp.concatenate([b_rg, b_re]), (0, ROUTER_ROWS - n_log)).reshape(ROUTER_ROWS, 1)
    return pl.pallas_call(
        _route_sort_kernel,
        grid=(nb,),
        in_specs=[pl.BlockSpec((tm, d), lambda i: (i, 0)),
                  pl.BlockSpec((1, d), lambda i: (0, 0)),
                  pl.BlockSpec((ROUTER_ROWS, d), lambda i: (0, 0)),
                  pl.BlockSpec((ROUTER_ROWS, 1), lambda i: (0, 0))],
        out_specs=[pl.BlockSpec((LOCAL_ROWS, d), lambda i: (i, 0)),
                   pl.BlockSpec((8, tm), lambda i: (0, i)),
                   pl.BlockSpec((N_EXPERTS, LANES), lambda i: (i, 0))],
        out_shape=[jax.ShapeDtypeStruct((nb * LOCAL_ROWS, d), BF16),
                   jax.ShapeDtypeStruct((8, t), F32),
                   jax.ShapeDtypeStruct((nb * N_EXPERTS, LANES), F32)],
        compiler_params=_params(1),
        name="route_sort",
    )(h1, g.reshape(1, d), wr, br)


def _group_copy(src_ref, s_group, dst_ref, d_group, sem):
    return pltpu.make_async_copy(src_ref.at[pl.ds(pl.multiple_of(s_group * GROUP, GROUP), GROUP), :],
                                 dst_ref.at[pl.ds(pl.multiple_of(d_group * GROUP, GROUP), GROUP), :], sem)


def _expert_kernel(te_ref, nxt_ref, nt_ref, gsrc_ref, gdst_ref, tail_ref,
                   xs_ref, wg_ref, wu_ref, wd_ref, ys_ref,
                   xbuf_ref, ybuf_ref, zero_ref, wg_f32_ref, wu_f32_ref, wd_f32_ref,
                   wgu_bf_ref, wd_bf_ref, cur_ref, gsem, ssem, zsem, wsem):
    r = pl.program_id(0)
    nt = nt_ref[0]
    f = wg_ref.shape[2]

    def gather(q, slot):
        for i in range(TILE_GROUPS):
            _group_copy(xs_ref, gsrc_ref[q * TILE_GROUPS + i], xbuf_ref, slot * TILE_GROUPS + i,
                        gsem.at[slot]).start(priority=1)

    def weight_copies(e, wslot):
        return (pltpu.make_async_copy(wg_ref.at[e], wg_f32_ref.at[wslot], wsem.at[wslot]),
                pltpu.make_async_copy(wu_ref.at[e], wu_f32_ref.at[wslot], wsem.at[wslot]),
                pltpu.make_async_copy(wd_ref.at[e], wd_f32_ref.at[wslot], wsem.at[wslot]))

    def tile_wait(src, dst, sem):
        pltpu.make_async_copy(src.at[pl.ds(0, MOE_TILE), :], dst.at[pl.ds(0, MOE_TILE), :], sem).wait()

    @pl.when(r == 0)
    def _():
        cur_ref[0] = -1
        cur_ref[1] = 0
        for cp in weight_copies(te_ref[0], 0):
            cp.start()
        zero_ref[...] = jnp.zeros_like(zero_ref)
        n_blocks = tail_ref.shape[0]

        def fill(make):
            def body(g, c):
                make(g)
                return c
            return body

        for blk in range(n_blocks):
            lax.fori_loop(tail_ref[blk], GROUPS_PER_BLOCK, fill(
                lambda g, blk=blk: _group_copy(zero_ref, 0, ys_ref, blk * GROUPS_PER_BLOCK + g, zsem).start()), 0)
        for blk in range(n_blocks):
            lax.fori_loop(tail_ref[blk], GROUPS_PER_BLOCK, fill(
                lambda g, blk=blk: _group_copy(zero_ref, 0, ys_ref, blk * GROUPS_PER_BLOCK + g, zsem).wait()), 0)
        gather(0, 0)

    @pl.when(r < nt)
    def _():
        e = te_ref[r]
        slot = r % 2

        gather(jnp.minimum(r + 1, nt - 1), 1 - slot)

        @pl.when(cur_ref[0] != e)
        def _():
            wslot = cur_ref[1] % 2
            for cp in weight_copies(e, wslot):
                cp.wait()
            wgu_bf_ref[:, 0:f] = wg_f32_ref[wslot].astype(BF16)
            wgu_bf_ref[:, f:2 * f] = wu_f32_ref[wslot].astype(BF16)
            wd_bf_ref[...] = wd_f32_ref[wslot].astype(BF16)
            cur_ref[0] = e
            cur_ref[1] = cur_ref[1] + 1

            @pl.when(nxt_ref[r] >= 0)
            def _():
                for cp in weight_copies(nxt_ref[r], 1 - wslot):
                    cp.start()

        tile_wait(xs_ref, xbuf_ref, gsem.at[slot])
        row0 = pl.multiple_of(slot * MOE_TILE, MOE_TILE)
        gu = _dot(xbuf_ref[pl.ds(row0, MOE_TILE), :], wgu_bf_ref[...])
        hid = (jax.nn.silu(gu[:, 0:f]) * gu[:, f:2 * f]).astype(BF16)
        y = _dot(hid, wd_bf_ref[...]).astype(ybuf_ref.dtype)

        @pl.when(r >= 2)
        def _():
            tile_wait(ybuf_ref, ys_ref, ssem.at[slot])

        ybuf_ref[pl.ds(row0, MOE_TILE), :] = y
        for i in range(TILE_GROUPS):
            _group_copy(ybuf_ref, slot * TILE_GROUPS + i, ys_ref, gdst_ref[r * TILE_GROUPS + i],
                        ssem.at[slot]).start(priority=1)

        @pl.when(r == nt - 1)
        def _():
            tile_wait(xs_ref, xbuf_ref, gsem.at[1 - slot])
            tile_wait(ybuf_ref, ys_ref, ssem.at[slot])

            @pl.when(r >= 1)
            def _():
                tile_wait(ybuf_ref, ys_ref, ssem.at[1 - slot])


def _experts(plan, xs, w_gate, w_up, w_down):
    d = xs.shape[1]
    f = w_gate.shape[2]
    tile_e, next_e, n_tiles, gsrc, gdst, tail = plan
    r_max = tile_e.shape[0]
    n_blocks = tail.shape[0]
    any_spec = pl.BlockSpec(memory_space=pl.ANY)
    grid_spec = pltpu.PrefetchScalarGridSpec(
        num_scalar_prefetch=6,
        grid=(r_max,),
        in_specs=[any_spec, any_spec, any_spec, any_spec],
        out_specs=any_spec,
        scratch_shapes=[pltpu.VMEM((2 * MOE_TILE, d), BF16),
                        pltpu.VMEM((2 * MOE_TILE, d), BF16),
                        pltpu.VMEM((GROUP, d), BF16),
                        pltpu.VMEM((2, d, f), F32),
                        pltpu.VMEM((2, d, f), F32),
                        pltpu.VMEM((2, f, d), F32),
                        pltpu.VMEM((d, 2 * f), BF16),
                        pltpu.VMEM((f, d), BF16),
                        pltpu.SMEM((2,), jnp.int32),
                        pltpu.SemaphoreType.DMA((2,)),
                        pltpu.SemaphoreType.DMA((2,)),
                        pltpu.SemaphoreType.DMA(()),
                        pltpu.SemaphoreType.DMA((2,))],
    )
    return pl.pallas_call(
        _expert_kernel,
        grid_spec=grid_spec,
        out_shape=jax.ShapeDtypeStruct((n_blocks * LOCAL_ROWS, d), BF16),
        compiler_params=_params(1),
        name="experts",
    )(tile_e, next_e, n_tiles, gsrc, gdst, tail, xs, w_gate, w_up, w_down)


def _combine_kernel(ys_ref, h_ref, cm_ref, g_ref, o_ref):
    tm = h_ref.shape[0]
    cm = cm_ref[...]
    q = lax.broadcasted_iota(jnp.int32, (tm, ys_ref.shape[0]), 1)
    ys = ys_ref[...]
    sel_a = jnp.where(q == cm[:, 0:1].astype(jnp.int32), 1.0, 0.0).astype(BF16)
    sel_b = jnp.where(q == cm[:, 1:2].astype(jnp.int32), 1.0, 0.0).astype(BF16)
    h2 = h_ref[...] + cm[:, 2:3] * _dot(sel_a, ys) + cm[:, 3:4] * _dot(sel_b, ys)
    o_ref[...] = _rms_scale(h2, g_ref[...])


def _combine(ys, h1, cmeta, g):
    t, d = h1.shape
    tm = ROUTE_BLOCK
    return pl.pallas_call(
        _combine_kernel,
        grid=(t // tm,),
        in_specs=[pl.BlockSpec((LOCAL_ROWS, d), lambda i: (i, 0)),
                  pl.BlockSpec((tm, d), lambda i: (i, 0)),
                  pl.BlockSpec((tm, LANES), lambda i: (i, 0)),
                  pl.BlockSpec((1, d), lambda i: (0, 0))],
        out_specs=pl.BlockSpec((tm, d), lambda i: (i, 0)),
        out_shape=jax.ShapeDtypeStruct((t, d), F32),
        compiler_params=_params(1),
        name="combine",
    )(ys, h1, cmeta, g.reshape(1, d))


def _expert_plan(counts, n_blocks):
    cnt = counts[:, 0].astype(jnp.int32).reshape(n_blocks, N_EXPERTS)
    groups = (cnt + GROUP - 1) // GROUP
    first = jnp.cumsum(groups, axis=1) - groups
    upto = jnp.cumsum(groups, axis=0)
    per_expert = upto[-1]
    tiles_e = (per_expert + TILE_GROUPS - 1) // TILE_GROUPS
    tile_end = jnp.cumsum(tiles_e)
    n_tiles = tile_end[-1]
    max_groups = 2 * ROUTE_BLOCK * n_blocks // GROUP + n_blocks * N_EXPERTS
    r_max = max_groups // TILE_GROUPS + N_EXPERTS
    tile_ids = jnp.arange(r_max, dtype=jnp.int32)
    tile = jnp.minimum(tile_ids, n_tiles - 1)
    tile_e = jnp.sum((tile_end[None, :] <= tile[:, None]).astype(jnp.int32), axis=1)
    later = (tile_e[None, :] > tile_e[:, None]) & (tile_ids[None, :] < n_tiles)
    next_e = jnp.min(jnp.where(later, tile_e[None, :], N_EXPERTS), axis=1)
    next_e = jnp.where(next_e == N_EXPERTS, -1, next_e)

    slot = jnp.arange(r_max * TILE_GROUPS, dtype=jnp.int32)
    s_tile = slot // TILE_GROUPS
    oh_e = jnp.repeat(tile_e, TILE_GROUPS)[:, None] == jnp.arange(N_EXPERTS, dtype=jnp.int32)[None, :]

    def by_expert(table):
        return jnp.sum(jnp.where(oh_e[:, None, :], table[None], 0), axis=-1)

    k = slot - by_expert(((tile_end - tiles_e) * TILE_GROUPS)[None, :])[:, 0]
    real = (k < by_expert(per_expert[None, :])[:, 0]) & (s_tile < n_tiles)
    upto_e = by_expert(upto)
    blk = jnp.minimum(jnp.sum((upto_e <= k[:, None]).astype(jnp.int32), axis=1), n_blocks - 1)
    oh_b = blk[:, None] == jnp.arange(n_blocks, dtype=jnp.int32)[None, :]

    def by_block(table_se):
        return jnp.sum(jnp.where(oh_b, table_se, 0), axis=1)

    before = by_block(upto_e - by_expert(groups))
    src = blk * GROUPS_PER_BLOCK + by_block(by_expert(first)) + (k - before)
    zero_group = GROUPS_PER_BLOCK - 1
    spare = n_blocks * GROUPS_PER_BLOCK + (s_tile % 2) * TILE_GROUPS + slot % TILE_GROUPS
    gsrc = jnp.where(real, src, zero_group)
    gdst = jnp.where(real, src, spare)
    tail = jnp.concatenate([jnp.sum(groups, axis=1), jnp.zeros((1,), jnp.int32)])
    return tile_e, next_e, n_tiles.reshape(1), gsrc, gdst, tail


def _layer(h, norm_mix_g, w_in, conv_w, sgu_ln_g, sgu_ln_b, sgu_w_s, sgu_b_s, w_up_conv,
           w_up_sgu, w_out, norm_ffn_g, w_rg, b_rg, w_re, b_re, w_eg, w_eu, w_ed, out_g):
    t, d = h.shape
    conv_width = conv_w.shape[1]
    sgu_width = sgu_ln_g.shape[0]
    xn = _rmsnorm(h, norm_mix_g)
    ya = _conv_branch(xn, w_in, conv_w, conv_width)
    gz = _zproj(xn, w_in, 3 * conv_width, 2 * sgu_width)
    yb = _sgu_mix(gz, sgu_ln_g, sgu_ln_b, sgu_w_s, sgu_b_s)
    m = _upgate(xn, ya, yb, w_in, 3 * conv_width + 2 * sgu_width, w_up_conv, w_up_sgu)
    h1 = _outproj(m, w_out, h)
    xs, meta, counts = _route_sort(h1, norm_ffn_g, w_rg, b_rg, w_re, b_re)
    ys = _experts(_expert_plan(counts, t // ROUTE_BLOCK), xs, w_eg, w_eu, w_ed)
    cmeta = jnp.pad(meta[0:4].T, ((0, 0), (0, LANES - 4)))
    return _combine(ys, h1, cmeta, out_g)


def kernel(x, norm_mix_g, w_in, conv_w, sgu_ln_g, sgu_ln_b, sgu_w_s, sgu_b_s, w_up_conv, w_up_sgu, w_out, norm_ffn_g, w_router_group, b_router_group, w_router_expert, b_router_expert, w_exp_gate, w_exp_up, w_exp_down, norm_final_g):
    bsz, s, d = x.shape
    depth = w_in.shape[0]
    assert bsz == 1 and depth == 1, "causal conv carry and the fused final norm assume one sequence, one layer"
    assert s % ROUTE_BLOCK == 0
    out = _layer(x.reshape(s, d), norm_mix_g[0], w_in[0], conv_w[0], sgu_ln_g[0], sgu_ln_b[0],
                 sgu_w_s[0], sgu_b_s[0], w_up_conv[0], w_up_sgu[0], w_out[0], norm_ffn_g[0],
                 w_router_group[0], b_router_group[0], w_router_expert[0], b_router_expert[0],
                 w_exp_gate[0], w_exp_up[0], w_exp_down[0], norm_final_g)
    return out.reshape(bsz, s, d)
```

```python
import functools

import jax
import jax.numpy as jnp
from jax import lax
from jax.experimental import pallas as pl
from jax.experimental.pallas import tpu as pltpu

F32 = jnp.float32
BF16 = jnp.bfloat16

EPS = 1e-6
CHUNK = 64
CONV_K = 3
SGU_HEADS = 8
SGU_BLOCK = 128
N_GROUPS = 4
EXPERTS_PER_GROUP = 4
N_EXPERTS = N_GROUPS * EXPERTS_PER_GROUP
ROUTER_ROWS = 32
LANES = 128

VMEM_LIMIT_BYTES = 56 * 1024 * 1024

ROUTE_BLOCK = 512
GROUP = 16
LOCAL_ROWS = -(-(2 * ROUTE_BLOCK + N_EXPERTS * (GROUP - 1)) // 256) * 256
GROUPS_PER_BLOCK = LOCAL_ROWS // GROUP
MOE_TILE = 256
TILE_GROUPS = MOE_TILE // GROUP


def _params(n_axes):
    return pltpu.CompilerParams(
        dimension_semantics=("arbitrary",) * n_axes,
        vmem_limit_bytes=VMEM_LIMIT_BYTES)


def _dot(a, b):
    return jnp.dot(a, b, preferred_element_type=F32)


def _rms_scale(x, g):
    ms = jnp.mean(x * x, axis=-1, keepdims=True)
    return x * lax.rsqrt(ms + EPS) * g


def _rmsnorm_kernel(x_ref, g_ref, o_ref):
    o_ref[...] = _rms_scale(x_ref[...], g_ref[...]).astype(o_ref.dtype)


def _rmsnorm(x, g, tm=512):
    t, d = x.shape
    return pl.pallas_call(
        _rmsnorm_kernel,
        grid=(t // tm,),
        in_specs=[pl.BlockSpec((tm, d), lambda i: (i, 0)),
                  pl.BlockSpec((1, d), lambda i: (0, 0))],
        out_specs=pl.BlockSpec((tm, d), lambda i: (i, 0)),
        out_shape=jax.ShapeDtypeStruct((t, d), BF16),
        compiler_params=_params(1),
        name="rmsnorm",
    )(x, g.reshape(1, d))


def _conv_kernel(xn_ref, wb_ref, wc_ref, wh_ref, cw_ref, o_ref, wbf_ref, carry_ref):
    i = pl.program_id(1)
    tn = wb_ref.shape[1]
    tm = xn_ref.shape[0]

    @pl.when(i == 0)
    def _():
        wbf_ref[:, 0:tn] = wb_ref[...].astype(BF16)
        wbf_ref[:, tn:2 * tn] = wc_ref[...].astype(BF16)
        wbf_ref[:, 2 * tn:3 * tn] = wh_ref[...].astype(BF16)
        carry_ref[...] = jnp.zeros_like(carry_ref)

    proj = _dot(xn_ref[...], wbf_ref[...])
    b = proj[:, 0:tn]
    p = proj[:, tn:2 * tn] * proj[:, 2 * tn:3 * tn]
    prev = carry_ref[...]
    carry_ref[...] = p[tm - 8:tm, :]
    row = lax.broadcasted_iota(jnp.int32, p.shape, 0)
    p1 = jnp.where(row == 0, prev[7:8, :], pltpu.roll(p, 1, axis=0))
    p2 = jnp.where(row == 0, prev[6:7, :],
                   jnp.where(row == 1, prev[7:8, :], pltpu.roll(p, 2, axis=0)))
    cw = cw_ref[...]
    y = b * (cw[0:1, :] * p2 + cw[1:2, :] * p1 + cw[2:3, :] * p)
    o_ref[...] = y.astype(o_ref.dtype)


def _conv_branch(xn, w_in, conv_w, width, tm=1024, tn=256):
    t, d = xn.shape
    nj = width // tn
    return pl.pallas_call(
        _conv_kernel,
        grid=(nj, t // tm),
        in_specs=[pl.BlockSpec((tm, d), lambda j, i: (i, 0)),
                  pl.BlockSpec((d, tn), lambda j, i: (0, j)),
                  pl.BlockSpec((d, tn), lambda j, i: (0, nj + j)),
                  pl.BlockSpec((d, tn), lambda j, i: (0, 2 * nj + j)),
                  pl.BlockSpec((CONV_K, tn), lambda j, i: (0, j))],
        out_specs=pl.BlockSpec((tm, tn), lambda j, i: (i, j)),
        out_shape=jax.ShapeDtypeStruct((t, width), BF16),
        scratch_shapes=[pltpu.VMEM((d, 3 * tn), BF16),
                        pltpu.VMEM((8, tn), F32)],
        compiler_params=_params(2),
        name="conv_branch",
    )(xn, w_in, w_in, w_in, conv_w)


def _zproj_kernel(xn_ref, w_ref, o_ref, wbf_ref):
    @pl.when(pl.program_id(1) == 0)
    def _():
        wbf_ref[...] = w_ref[...].astype(BF16)

    z = _dot(xn_ref[...], wbf_ref[...])
    o_ref[...] = jax.nn.gelu(z).astype(o_ref.dtype)


def _zproj(xn, w_in, col0, width, tm=1024, tn=512):
    t, d = xn.shape
    c0 = col0 // tn
    return pl.pallas_call(
        _zproj_kernel,
        grid=(width // tn, t // tm),
        in_specs=[pl.BlockSpec((tm, d), lambda j, i: (i, 0)),
                  pl.BlockSpec((d, tn), lambda j, i: (0, c0 + j))],
        out_specs=pl.BlockSpec((tm, tn), lambda j, i: (i, j)),
        out_shape=jax.ShapeDtypeStruct((t, width), BF16),
        scratch_shapes=[pltpu.VMEM((d, tn), BF16)],
        compiler_params=_params(2),
        name="sgu_zproj",
    )(xn, w_in)


def _sgu_kernel(gz_ref, lng_ref, lnb_ref, ws_ref, bsx_ref, o_ref):
    tm = gz_ref.shape[0]
    w = o_ref.shape[1]
    hd = w // SGU_HEADS
    v = gz_ref[:, w:2 * w].astype(F32)
    mu = jnp.mean(v, axis=-1, keepdims=True)
    vc = v - mu
    var = jnp.mean(vc * vc, axis=-1, keepdims=True)
    vn = (vc * lax.rsqrt(var + EPS) * lng_ref[...] + lnb_ref[...]).astype(BF16)
    ii = lax.broadcasted_iota(jnp.int32, (SGU_BLOCK, SGU_BLOCK), 0)
    jj = lax.broadcasted_iota(jnp.int32, (SGU_BLOCK, SGU_BLOCK), 1)
    mask = (jj // CHUNK) <= (ii // CHUNK)
    for h in range(SGU_HEADS):
        wm = jnp.where(mask, ws_ref[h], 0.0).astype(BF16)
        cs = slice(h * hd, (h + 1) * hd)
        for n in range(tm // SGU_BLOCK):
            rs = slice(n * SGU_BLOCK, (n + 1) * SGU_BLOCK)
            vm = _dot(wm, vn[rs, cs]) + bsx_ref[:, cs]
            o_ref[rs, cs] = (gz_ref[rs, cs].astype(F32) * vm).astype(o_ref.dtype)


def _sgu_mix(gz, ln_g, ln_b, w_s, b_s, tm=512):
    t, w2 = gz.shape
    w = w2 // 2
    hd = w // SGU_HEADS
    bsx = jnp.repeat(b_s.T, hd, axis=1)
    return pl.pallas_call(
        _sgu_kernel,
        grid=(t // tm,),
        in_specs=[pl.BlockSpec((tm, w2), lambda i: (i, 0)),
                  pl.BlockSpec((1, w), lambda i: (0, 0)),
                  pl.BlockSpec((1, w), lambda i: (0, 0)),
                  pl.BlockSpec((SGU_HEADS, SGU_BLOCK, SGU_BLOCK), lambda i: (0, 0, 0)),
                  pl.BlockSpec((SGU_BLOCK, w), lambda i: (0, 0))],
        out_specs=pl.BlockSpec((tm, w), lambda i: (i, 0)),
        out_shape=jax.ShapeDtypeStruct((t, w), BF16),
        compiler_params=_params(1),
        name="sgu_mix",
    )(gz, ln_g.reshape(1, w), ln_b.reshape(1, w), w_s, bsx)


def _upgate_kernel(xn_ref, ya_ref, yb_ref, wgc_ref, wgs_ref, wua_ref, wub_ref, eg_ref, eu_ref, ed_ref,
                   o_ref, eg_bf_ref, eu_bf_ref, ed_bf_ref, wgbf_ref, wuabf_ref, wubbf_ref):
    tn = wgc_ref.shape[1]
    eg_bf_ref[...] = eg_ref[...].astype(BF16)
    eu_bf_ref[...] = eu_ref[...].astype(BF16)
    ed_bf_ref[...] = ed_ref[...].astype(BF16)

    @pl.when(pl.program_id(1) == 0)
    def _():
        wgbf_ref[:, 0:tn] = wgc_ref[...].astype(BF16)
        wgbf_ref[:, tn:2 * tn] = wgs_ref[...].astype(BF16)
        wuabf_ref[...] = wua_ref[...].astype(BF16)
        wubbf_ref[...] = wub_ref[...].astype(BF16)

    gl = _dot(xn_ref[...], wgbf_ref[...])
    a = _dot(ya_ref[...], wuabf_ref[...])
    b = _dot(yb_ref[...], wubbf_ref[...])
    m = jax.nn.sigmoid(gl[:, 0:tn]) * a + jax.nn.sigmoid(gl[:, tn:2 * tn]) * b
    o_ref[...] = m.astype(o_ref.dtype)


def _upgate(xn, ya, yb, w_in, gate_col0, w_up_a, w_up_b, w_eg, w_eu, w_ed, tm=512, tn=512):
    t, d = xn.shape
    wa = ya.shape[1]
    wb = yb.shape[1]
    dout = w_up_a.shape[1]
    c0 = gate_col0 // tn
    nj = dout // tn
    ni = t // tm
    n_e, d_e, f_e = w_eg.shape
    up_rows = n_e * d_e // (nj * ni)
    down_rows = n_e * f_e // (nj * ni)
    assert up_rows * nj * ni == n_e * d_e and up_rows % 16 == 0
    assert down_rows * nj * ni == n_e * f_e and down_rows % 16 == 0
    up_spec = pl.BlockSpec((up_rows, f_e), lambda j, i: (j * ni + i, 0))
    down_spec = pl.BlockSpec((down_rows, d_e), lambda j, i: (j * ni + i, 0))
    m, eg_bf, eu_bf, ed_bf = pl.pallas_call(
        _upgate_kernel,
        grid=(nj, ni),
        in_specs=[pl.BlockSpec((tm, d), lambda j, i: (i, 0)),
                  pl.BlockSpec((tm, wa), lambda j, i: (i, 0)),
                  pl.BlockSpec((tm, wb), lambda j, i: (i, 0)),
                  pl.BlockSpec((d, tn), lambda j, i: (0, c0 + j)),
                  pl.BlockSpec((d, tn), lambda j, i: (0, c0 + nj + j)),
                  pl.BlockSpec((wa, tn), lambda j, i: (0, j)),
                  pl.BlockSpec((wb, tn), lambda j, i: (0, j)),
                  up_spec, up_spec, down_spec],
        out_specs=[pl.BlockSpec((tm, tn), lambda j, i: (i, j)), up_spec, up_spec, down_spec],
        out_shape=[jax.ShapeDtypeStruct((t, dout), BF16),
                   jax.ShapeDtypeStruct((n_e * d_e, f_e), BF16),
                   jax.ShapeDtypeStruct((n_e * d_e, f_e), BF16),
                   jax.ShapeDtypeStruct((n_e * f_e, d_e), BF16)],
        scratch_shapes=[pltpu.VMEM((d, 2 * tn), BF16),
                        pltpu.VMEM((wa, tn), BF16),
                        pltpu.VMEM((wb, tn), BF16)],
        compiler_params=_params(2),
        name="upgate",
    )(xn, ya, yb, w_in, w_in, w_up_a, w_up_b,
      w_eg.reshape(n_e * d_e, f_e), w_eu.reshape(n_e * d_e, f_e), w_ed.reshape(n_e * f_e, d_e))
    return (m, eg_bf.reshape(n_e, d_e, f_e), eu_bf.reshape(n_e, d_e, f_e),
            ed_bf.reshape(n_e, f_e, d_e))


def _outproj_kernel(m_ref, w_ref, x_ref, o_ref, wbf_ref):
    @pl.when(pl.program_id(1) == 0)
    def _():
        wbf_ref[...] = w_ref[...].astype(BF16)

    o_ref[...] = x_ref[...] + _dot(m_ref[...], wbf_ref[...])


def _outproj(m, w_out, x, tm=512, tn=1024):
    t, d = m.shape
    dout = w_out.shape[1]
    return pl.pallas_call(
        _outproj_kernel,
        grid=(dout // tn, t // tm),
        in_specs=[pl.BlockSpec((tm, d), lambda j, i: (i, 0)),
                  pl.BlockSpec((d, tn), lambda j, i: (0, j)),
                  pl.BlockSpec((tm, tn), lambda j, i: (i, j))],
        out_specs=pl.BlockSpec((tm, tn), lambda j, i: (i, j)),
        out_shape=jax.ShapeDtypeStruct((t, dout), F32),
        scratch_shapes=[pltpu.VMEM((d, tn), BF16)],
        compiler_params=_params(2),
        name="outproj",
    )(m, w_out, x)


def _argmax_rows(rows):
    best = rows[0]
    idx = jnp.zeros(rows[0].shape, jnp.int32)
    for k in range(1, len(rows)):
        better = rows[k] > best
        best = jnp.where(better, rows[k], best)
        idx = jnp.where(better, k, idx)
    return best, idx


def _softmax_rows(rows):
    mx = functools.reduce(jnp.maximum, rows)
    ex = [jnp.exp(r - mx) for r in rows]
    den = functools.reduce(lambda a, b: a + b, ex)
    return [e / den for e in ex]


def _route_sort_kernel(h_ref, g_ref, wr_ref, br_ref, xs_ref, meta_ref, cnt_ref):
    tm = h_ref.shape[0]
    xn = _rms_scale(h_ref[...], g_ref[...])
    lt = lax.dot_general(wr_ref[...], xn, (((1,), (1,)), ((), ())),
                         precision=lax.Precision.HIGHEST,
                         preferred_element_type=F32) + br_ref[...]
    pgs = _softmax_rows([lt[k:k + 1, :] for k in range(N_GROUPS)])
    pg, gi = _argmax_rows(pgs)
    sel = []
    for k in range(EXPERTS_PER_GROUP):
        r = jnp.zeros_like(pg)
        for g in range(N_GROUPS):
            row = N_GROUPS + g * EXPERTS_PER_GROUP + k
            r = jnp.where(gi == g, lt[row:row + 1, :], r)
        sel.append(r)
    pes = _softmax_rows(sel)
    p1, e1 = _argmax_rows(pes)
    rest = [jnp.where(e1 == k, -1.0, pes[k]) for k in range(EXPERTS_PER_GROUP)]
    p2, e2 = _argmax_rows(rest)
    den = p1 + p2
    w1 = pg * (p1 / den)
    w2 = pg * (p2 / den)
    lo = jnp.minimum(e1, e2)
    hi = jnp.maximum(e1, e2)
    w_lo = jnp.where(e1 < e2, w1, w2)
    w_hi = jnp.where(e1 < e2, w2, w1)
    ea = gi * EXPERTS_PER_GROUP + lo
    eb = gi * EXPERTS_PER_GROUP + hi

    erow = lax.broadcasted_iota(jnp.int32, (N_EXPERTS, tm), 0)
    oh_a = (erow == ea).astype(F32)
    oh_b = (erow == eb).astype(F32)
    a = lax.broadcasted_iota(jnp.int32, (tm, tm), 0)
    b = lax.broadcasted_iota(jnp.int32, (tm, tm), 1)
    before = (a < b).astype(BF16)
    cum = _dot((oh_a + oh_b).astype(BF16), before)
    cnt = jnp.sum(oh_a + oh_b, axis=1, keepdims=True)
    padded = jnp.floor((cnt + (GROUP - 1)) * (1.0 / GROUP)) * GROUP
    pos_a = jnp.sum(oh_a * cum + jnp.where(erow < ea, padded, 0.0), axis=0, keepdims=True)
    pos_b = jnp.sum(oh_b * cum + jnp.where(erow < eb, padded, 0.0), axis=0, keepdims=True)

    q = lax.broadcasted_iota(jnp.int32, (xs_ref.shape[0], tm), 0)
    perm = jnp.where((q == pos_a.astype(jnp.int32)) | (q == pos_b.astype(jnp.int32)), 1.0, 0.0)
    xs_ref[...] = _dot(perm.astype(BF16), xn.astype(BF16)).astype(xs_ref.dtype)

    cnt_ref[...] = jnp.broadcast_to(cnt, cnt_ref.shape)
    meta_ref[0:1, :] = pos_a
    meta_ref[1:2, :] = pos_b
    meta_ref[2:3, :] = w_lo
    meta_ref[3:4, :] = w_hi
    meta_ref[4:8, :] = jnp.zeros((4, tm), F32)


def _route_sort(h1, g, w_rg, b_rg, w_re, b_re):
    t, d = h1.shape
    tm = ROUTE_BLOCK
    nb = t // tm
    n_log = w_rg.shape[1] + w_re.shape[1]
    wr = jnp.concatenate([w_rg, w_re], axis=1).T
    wr = jnp.pad(wr, ((0, ROUTER_ROWS - n_log), (0, 0)))
    br = jnp.pad(jnp.concatenate([b_rg, b_re]), (0, ROUTER_ROWS - n_log)).reshape(ROUTER_ROWS, 1)
    return pl.pallas_call(
        _route_sort_kernel,
        grid=(nb,),
        in_specs=[pl.BlockSpec((tm, d), lambda i: (i, 0)),
                  pl.BlockSpec((1, d), lambda i: (0, 0)),
                  pl.BlockSpec((ROUTER_ROWS, d), lambda i: (0, 0)),
                  pl.BlockSpec((ROUTER_ROWS, 1), lambda i: (0, 0))],
        out_specs=[pl.BlockSpec((LOCAL_ROWS, d), lambda i: (i, 0)),
                   pl.BlockSpec((8, tm), lambda i: (0, i)),
                   pl.BlockSpec((N_EXPERTS, LANES), lambda i: (i, 0))],
        out_shape=[jax.ShapeDtypeStruct((nb * LOCAL_ROWS, d), BF16),
                   jax.ShapeDtypeStruct((8, t), F32),
                   jax.ShapeDtypeStruct((nb * N_EXPERTS, LANES), F32)],
        compiler_params=_params(1),
        name="route_sort",
    )(h1, g.reshape(1, d), wr, br)


def _group_copy(src_ref, s_group, dst_ref, d_group, sem):
    return pltpu.make_async_copy(src_ref.at[pl.ds(pl.multiple_of(s_group * GROUP, GROUP), GROUP), :],
                                 dst_ref.at[pl.ds(pl.multiple_of(d_group * GROUP, GROUP), GROUP), :], sem)


def _expert_kernel(te_ref, nxt_ref, nt_ref, gsrc_ref, gdst_ref, tail_ref,
                   xs_ref, wg_ref, wu_ref, wd_ref, ys_ref,
                   xbuf_ref, ybuf_ref, zero_ref, wg_buf_ref, wu_buf_ref, wd_buf_ref,
                   cur_ref, gsem, ssem, zsem, wsem):
    r = pl.program_id(0)
    nt = nt_ref[0]

    def gather(q, slot):
        for i in range(TILE_GROUPS):
            _group_copy(xs_ref, gsrc_ref[q * TILE_GROUPS + i], xbuf_ref, slot * TILE_GROUPS + i,
                        gsem.at[slot]).start(priority=1)

    def weight_copies(e, wslot):
        return (pltpu.make_async_copy(wg_ref.at[e], wg_buf_ref.at[wslot], wsem.at[wslot]),
                pltpu.make_async_copy(wu_ref.at[e], wu_buf_ref.at[wslot], wsem.at[wslot]),
                pltpu.make_async_copy(wd_ref.at[e], wd_buf_ref.at[wslot], wsem.at[wslot]))

    def tile_wait(src, dst, sem):
        pltpu.make_async_copy(src.at[pl.ds(0, MOE_TILE), :], dst.at[pl.ds(0, MOE_TILE), :], sem).wait()

    @pl.when(r == 0)
    def _():
        cur_ref[0] = -1
        cur_ref[1] = -1
        for cp in weight_copies(te_ref[0], 0):
            cp.start()
        zero_ref[...] = jnp.zeros_like(zero_ref)
        n_blocks = tail_ref.shape[0]

        def fill(make):
            def body(g, c):
                make(g)
                return c
            return body

        for blk in range(n_blocks):
            lax.fori_loop(tail_ref[blk], GROUPS_PER_BLOCK, fill(
                lambda g, blk=blk: _group_copy(zero_ref, 0, ys_ref, blk * GROUPS_PER_BLOCK + g, zsem).start()), 0)
        for blk in range(n_blocks):
            lax.fori_loop(tail_ref[blk], GROUPS_PER_BLOCK, fill(
                lambda g, blk=blk: _group_copy(zero_ref, 0, ys_ref, blk * GROUPS_PER_BLOCK + g, zsem).wait()), 0)
        gather(0, 0)

    @pl.when(r < nt)
    def _():
        e = te_ref[r]
        slot = r % 2

        gather(jnp.minimum(r + 1, nt - 1), 1 - slot)

        @pl.when(cur_ref[0] != e)
        def _():
            wslot = (cur_ref[1] + 1) % 2
            for cp in weight_copies(e, wslot):
                cp.wait()
            cur_ref[0] = e
            cur_ref[1] = cur_ref[1] + 1

            @pl.when(nxt_ref[r] >= 0)
            def _():
                for cp in weight_copies(nxt_ref[r], 1 - wslot):
                    cp.start()

        wslot = cur_ref[1] % 2
        tile_wait(xs_ref, xbuf_ref, gsem.at[slot])
        row0 = pl.multiple_of(slot * MOE_TILE, MOE_TILE)
        x = xbuf_ref[pl.ds(row0, MOE_TILE), :]
        hid = (jax.nn.silu(_dot(x, wg_buf_ref[wslot])) * _dot(x, wu_buf_ref[wslot])).astype(BF16)
        y = _dot(hid, wd_buf_ref[wslot]).astype(ybuf_ref.dtype)

        @pl.when(r >= 2)
        def _():
            tile_wait(ybuf_ref, ys_ref, ssem.at[slot])

        ybuf_ref[pl.ds(row0, MOE_TILE), :] = y
        for i in range(TILE_GROUPS):
            _group_copy(ybuf_ref, slot * TILE_GROUPS + i, ys_ref, gdst_ref[r * TILE_GROUPS + i],
                        ssem.at[slot]).start(priority=1)

        @pl.when(r == nt - 1)
        def _():
            tile_wait(xs_ref, xbuf_ref, gsem.at[1 - slot])
            tile_wait(ybuf_ref, ys_ref, ssem.at[slot])

            @pl.when(r >= 1)
            def _():
                tile_wait(ybuf_ref, ys_ref, ssem.at[1 - slot])


def _experts(plan, xs, w_gate, w_up, w_down):
    d = xs.shape[1]
    f = w_gate.shape[2]
    tile_e, next_e, n_tiles, gsrc, gdst, tail = plan
    r_max = tile_e.shape[0]
    n_blocks = tail.shape[0]
    any_spec = pl.BlockSpec(memory_space=pl.ANY)
    grid_spec = pltpu.PrefetchScalarGridSpec(
        num_scalar_prefetch=6,
        grid=(r_max,),
        in_specs=[any_spec, any_spec, any_spec, any_spec],
        out_specs=any_spec,
        scratch_shapes=[pltpu.VMEM((2 * MOE_TILE, d), BF16),
                        pltpu.VMEM((2 * MOE_TILE, d), BF16),
                        pltpu.VMEM((GROUP, d), BF16),
                        pltpu.VMEM((2, d, f), BF16),
                        pltpu.VMEM((2, d, f), BF16),
                        pltpu.VMEM((2, f, d), BF16),
                        pltpu.SMEM((2,), jnp.int32),
                        pltpu.SemaphoreType.DMA((2,)),
                        pltpu.SemaphoreType.DMA((2,)),
                        pltpu.SemaphoreType.DMA(()),
                        pltpu.SemaphoreType.DMA((2,))],
    )
    return pl.pallas_call(
        _expert_kernel,
        grid_spec=grid_spec,
        out_shape=jax.ShapeDtypeStruct((n_blocks * LOCAL_ROWS, d), BF16),
        compiler_params=_params(1),
        name="experts",
    )(tile_e, next_e, n_tiles, gsrc, gdst, tail, xs, w_gate, w_up, w_down)


def _combine_kernel(ys_ref, h_ref, cm_ref, g_ref, o_ref):
    tm = h_ref.shape[0]
    cm = cm_ref[...]
    q = lax.broadcasted_iota(jnp.int32, (tm, ys_ref.shape[0]), 1)
    ys = ys_ref[...]
    sel_a = jnp.where(q == cm[:, 0:1].astype(jnp.int32), 1.0, 0.0).astype(BF16)
    sel_b = jnp.where(q == cm[:, 1:2].astype(jnp.int32), 1.0, 0.0).astype(BF16)
    h2 = h_ref[...] + cm[:, 2:3] * _dot(sel_a, ys) + cm[:, 3:4] * _dot(sel_b, ys)
    o_ref[...] = _rms_scale(h2, g_ref[...])


def _combine(ys, h1, cmeta, g):
    t, d = h1.shape
    tm = ROUTE_BLOCK
    return pl.pallas_call(
        _combine_kernel,
        grid=(t // tm,),
        in_specs=[pl.BlockSpec((LOCAL_ROWS, d), lambda i: (i, 0)),
                  pl.BlockSpec((tm, d), lambda i: (i, 0)),
                  pl.BlockSpec((tm, LANES), lambda i: (i, 0)),
                  pl.BlockSpec((1, d), lambda i: (0, 0))],
        out_specs=pl.BlockSpec((tm, d), lambda i: (i, 0)),
        out_shape=jax.ShapeDtypeStruct((t, d), F32),
        compiler_params=_params(1),
        name="combine",
    )(ys, h1, cmeta, g.reshape(1, d))


def _expert_plan(counts, n_blocks):
    cnt = counts[:, 0].astype(jnp.int32).reshape(n_blocks, N_EXPERTS)
    groups = (cnt + GROUP - 1) // GROUP
    first = jnp.cumsum(groups, axis=1) - groups
    upto = jnp.cumsum(groups, axis=0)
    per_expert = upto[-1]
    tiles_e = (per_expert + TILE_GROUPS - 1) // TILE_GROUPS
    tile_end = jnp.cumsum(tiles_e)
    n_tiles = tile_end[-1]
    max_groups = 2 * ROUTE_BLOCK * n_blocks // GROUP + n_blocks * N_EXPERTS
    r_max = max_groups // TILE_GROUPS + N_EXPERTS
    tile_ids = jnp.arange(r_max, dtype=jnp.int32)
    tile = jnp.minimum(tile_ids, n_tiles - 1)
    tile_e = jnp.sum((tile_end[None, :] <= tile[:, None]).astype(jnp.int32), axis=1)
    later = (tile_e[None, :] > tile_e[:, None]) & (tile_ids[None, :] < n_tiles)
    next_e = jnp.min(jnp.where(later, tile_e[None, :], N_EXPERTS), axis=1)
    next_e = jnp.where(next_e == N_EXPERTS, -1, next_e)

    slot = jnp.arange(r_max * TILE_GROUPS, dtype=jnp.int32)
    s_tile = slot // TILE_GROUPS
    oh_e = jnp.repeat(tile_e, TILE_GROUPS)[:, None] == jnp.arange(N_EXPERTS, dtype=jnp.int32)[None, :]

    def by_expert(table):
        return jnp.sum(jnp.where(oh_e[:, None, :], table[None], 0), axis=-1)

    k = slot - by_expert(((tile_end - tiles_e) * TILE_GROUPS)[None, :])[:, 0]
    real = (k < by_expert(per_expert[None, :])[:, 0]) & (s_tile < n_tiles)
    upto_e = by_expert(upto)
    blk = jnp.minimum(jnp.sum((upto_e <= k[:, None]).astype(jnp.int32), axis=1), n_blocks - 1)
    oh_b = blk[:, None] == jnp.arange(n_blocks, dtype=jnp.int32)[None, :]

    def by_block(table_se):
        return jnp.sum(jnp.where(oh_b, table_se, 0), axis=1)

    before = by_block(upto_e - by_expert(groups))
    src = blk * GROUPS_PER_BLOCK + by_block(by_expert(first)) + (k - before)
    zero_group = GROUPS_PER_BLOCK - 1
    spare = n_blocks * GROUPS_PER_BLOCK + (s_tile % 2) * TILE_GROUPS + slot % TILE_GROUPS
    gsrc = jnp.where(real, src, zero_group)
    gdst = jnp.where(real, src, spare)
    tail = jnp.concatenate([jnp.sum(groups, axis=1), jnp.zeros((1,), jnp.int32)])
    return tile_e, next_e, n_tiles.reshape(1), gsrc, gdst, tail


def _layer(h, norm_mix_g, w_in, conv_w, sgu_ln_g, sgu_ln_b, sgu_w_s, sgu_b_s, w_up_conv,
           w_up_sgu, w_out, norm_ffn_g, w_rg, b_rg, w_re, b_re, w_eg, w_eu, w_ed, out_g):
    t, d = h.shape
    conv_width = conv_w.shape[1]
    sgu_width = sgu_ln_g.shape[0]
    xn = _rmsnorm(h, norm_mix_g)
    ya = _conv_branch(xn, w_in, conv_w, conv_width)
    gz = _zproj(xn, w_in, 3 * conv_width, 2 * sgu_width)
    yb = _sgu_mix(gz, sgu_ln_g, sgu_ln_b, sgu_w_s, sgu_b_s)
    m, eg_bf, eu_bf, ed_bf = _upgate(xn, ya, yb, w_in, 3 * conv_width + 2 * sgu_width,
                                     w_up_conv, w_up_sgu, w_eg, w_eu, w_ed)
    h1 = _outproj(m, w_out, h)
    xs, meta, counts = _route_sort(h1, norm_ffn_g, w_rg, b_rg, w_re, b_re)
    ys = _experts(_expert_plan(counts, t // ROUTE_BLOCK), xs, eg_bf, eu_bf, ed_bf)
    cmeta = jnp.pad(meta[0:4].T, ((0, 0), (0, LANES - 4)))
    return _combine(ys, h1, cmeta, out_g)


def kernel(x, norm_mix_g, w_in, conv_w, sgu_ln_g, sgu_ln_b, sgu_w_s, sgu_b_s, w_up_conv, w_up_sgu, w_out, norm_ffn_g, w_router_group, b_router_group, w_router_expert, b_router_expert, w_exp_gate, w_exp_up, w_exp_down, norm_final_g):
    bsz, s, d = x.shape
    depth = w_in.shape[0]
    assert bsz == 1 and depth == 1, "causal conv carry and the fused final norm assume one sequence, one layer"
    assert s % ROUTE_BLOCK == 0
    out = _layer(x.reshape(s, d), norm_mix_g[0], w_in[0], conv_w[0], sgu_ln_g[0], sgu_ln_b[0],
                 sgu_w_s[0], sgu_b_s[0], w_up_conv[0], w_up_sgu[0], w_out[0], norm_ffn_g[0],
                 w_router_group[0], b_router_group[0], w_router_expert[0], b_router_expert[0],
                 w_exp_gate[0], w_exp_up[0], w_exp_down[0], norm_final_g)
    return out.reshape(bsz, s, d)
```

```python
import functools

import jax
import jax.numpy as jnp
from jax import lax
from jax.experimental import pallas as pl
from jax.experimental.pallas import tpu as pltpu

F32 = jnp.float32
BF16 = jnp.bfloat16

EPS = 1e-6
CHUNK = 64
CONV_K = 3
SGU_HEADS = 8
SGU_BLOCK = 128
N_GROUPS = 4
EXPERTS_PER_GROUP = 4
N_EXPERTS = N_GROUPS * EXPERTS_PER_GROUP
ROUTER_ROWS = 32
LANES = 128

VMEM_LIMIT_BYTES = 56 * 1024 * 1024

ROUTE_BLOCK = 256
GROUP = 16
LOCAL_ROWS = -(-(2 * ROUTE_BLOCK + N_EXPERTS * (GROUP - 1)) // 256) * 256
GROUPS_PER_BLOCK = LOCAL_ROWS // GROUP
MOE_TILE = 256
TILE_GROUPS = MOE_TILE // GROUP


def _params(n_axes):
    return pltpu.CompilerParams(
        dimension_semantics=("arbitrary",) * n_axes,
        vmem_limit_bytes=VMEM_LIMIT_BYTES)


def _dot(a, b):
    return jnp.dot(a, b, preferred_element_type=F32)


def _rms_scale(x, g):
    ms = jnp.mean(x * x, axis=-1, keepdims=True)
    return x * lax.rsqrt(ms + EPS) * g


def _rmsnorm_kernel(x_ref, g_ref, o_ref):
    o_ref[...] = _rms_scale(x_ref[...], g_ref[...]).astype(o_ref.dtype)


def _rmsnorm(x, g, tm=512):
    t, d = x.shape
    return pl.pallas_call(
        _rmsnorm_kernel,
        grid=(t // tm,),
        in_specs=[pl.BlockSpec((tm, d), lambda i: (i, 0)),
                  pl.BlockSpec((1, d), lambda i: (0, 0))],
        out_specs=pl.BlockSpec((tm, d), lambda i: (i, 0)),
        out_shape=jax.ShapeDtypeStruct((t, d), BF16),
        compiler_params=_params(1),
        name="rmsnorm",
    )(x, g.reshape(1, d))


def _conv_kernel(xn_ref, wb_ref, wc_ref, wh_ref, cw_ref, o_ref, wbf_ref, carry_ref):
    i = pl.program_id(1)
    tn = wb_ref.shape[1]
    tm = xn_ref.shape[0]

    @pl.when(i == 0)
    def _():
        wbf_ref[:, 0:tn] = wb_ref[...].astype(BF16)
        wbf_ref[:, tn:2 * tn] = wc_ref[...].astype(BF16)
        wbf_ref[:, 2 * tn:3 * tn] = wh_ref[...].astype(BF16)
        carry_ref[...] = jnp.zeros_like(carry_ref)

    proj = _dot(xn_ref[...], wbf_ref[...])
    b = proj[:, 0:tn]
    p = proj[:, tn:2 * tn] * proj[:, 2 * tn:3 * tn]
    prev = carry_ref[...]
    carry_ref[...] = p[tm - 8:tm, :]
    row = lax.broadcasted_iota(jnp.int32, p.shape, 0)
    p1 = jnp.where(row == 0, prev[7:8, :], pltpu.roll(p, 1, axis=0))
    p2 = jnp.where(row == 0, prev[6:7, :],
                   jnp.where(row == 1, prev[7:8, :], pltpu.roll(p, 2, axis=0)))
    cw = cw_ref[...]
    y = b * (cw[0:1, :] * p2 + cw[1:2, :] * p1 + cw[2:3, :] * p)
    o_ref[...] = y.astype(o_ref.dtype)


def _conv_branch(xn, w_in, conv_w, width, tm=1024, tn=256):
    t, d = xn.shape
    nj = width // tn
    return pl.pallas_call(
        _conv_kernel,
        grid=(nj, t // tm),
        in_specs=[pl.BlockSpec((tm, d), lambda j, i: (i, 0)),
                  pl.BlockSpec((d, tn), lambda j, i: (0, j)),
                  pl.BlockSpec((d, tn), lambda j, i: (0, nj + j)),
                  pl.BlockSpec((d, tn), lambda j, i: (0, 2 * nj + j)),
                  pl.BlockSpec((CONV_K, tn), lambda j, i: (0, j))],
        out_specs=pl.BlockSpec((tm, tn), lambda j, i: (i, j)),
        out_shape=jax.ShapeDtypeStruct((t, width), BF16),
        scratch_shapes=[pltpu.VMEM((d, 3 * tn), BF16),
                        pltpu.VMEM((8, tn), F32)],
        compiler_params=_params(2),
        name="conv_branch",
    )(xn, w_in, w_in, w_in, conv_w)


def _zproj_kernel(xn_ref, w_ref, o_ref, wbf_ref):
    @pl.when(pl.program_id(1) == 0)
    def _():
        wbf_ref[...] = w_ref[...].astype(BF16)

    z = _dot(xn_ref[...], wbf_ref[...])
    o_ref[...] = jax.nn.gelu(z).astype(o_ref.dtype)


def _zproj(xn, w_in, col0, width, tm=1024, tn=512):
    t, d = xn.shape
    c0 = col0 // tn
    return pl.pallas_call(
        _zproj_kernel,
        grid=(width // tn, t // tm),
        in_specs=[pl.BlockSpec((tm, d), lambda j, i: (i, 0)),
                  pl.BlockSpec((d, tn), lambda j, i: (0, c0 + j))],
        out_specs=pl.BlockSpec((tm, tn), lambda j, i: (i, j)),
        out_shape=jax.ShapeDtypeStruct((t, width), BF16),
        scratch_shapes=[pltpu.VMEM((d, tn), BF16)],
        compiler_params=_params(2),
        name="sgu_zproj",
    )(xn, w_in)


def _sgu_kernel(gz_ref, lng_ref, lnb_ref, ws_ref, bsx_ref, o_ref):
    tm = gz_ref.shape[0]
    w = o_ref.shape[1]
    hd = w // SGU_HEADS
    v = gz_ref[:, w:2 * w].astype(F32)
    mu = jnp.mean(v, axis=-1, keepdims=True)
    vc = v - mu
    var = jnp.mean(vc * vc, axis=-1, keepdims=True)
    vn = (vc * lax.rsqrt(var + EPS) * lng_ref[...] + lnb_ref[...]).astype(BF16)
    ii = lax.broadcasted_iota(jnp.int32, (SGU_BLOCK, SGU_BLOCK), 0)
    jj = lax.broadcasted_iota(jnp.int32, (SGU_BLOCK, SGU_BLOCK), 1)
    mask = (jj // CHUNK) <= (ii // CHUNK)
    for h in range(SGU_HEADS):
        wm = jnp.where(mask, ws_ref[h], 0.0).astype(BF16)
        cs = slice(h * hd, (h + 1) * hd)
        for n in range(tm // SGU_BLOCK):
            rs = slice(n * SGU_BLOCK, (n + 1) * SGU_BLOCK)
            vm = _dot(wm, vn[rs, cs]) + bsx_ref[:, cs]
            o_ref[rs, cs] = (gz_ref[rs, cs].astype(F32) * vm).astype(o_ref.dtype)


def _sgu_mix(gz, ln_g, ln_b, w_s, b_s, tm=512):
    t, w2 = gz.shape
    w = w2 // 2
    hd = w // SGU_HEADS
    bsx = jnp.repeat(b_s.T, hd, axis=1)
    return pl.pallas_call(
        _sgu_kernel,
        grid=(t // tm,),
        in_specs=[pl.BlockSpec((tm, w2), lambda i: (i, 0)),
                  pl.BlockSpec((1, w), lambda i: (0, 0)),
                  pl.BlockSpec((1, w), lambda i: (0, 0)),
                  pl.BlockSpec((SGU_HEADS, SGU_BLOCK, SGU_BLOCK), lambda i: (0, 0, 0)),
                  pl.BlockSpec((SGU_BLOCK, w), lambda i: (0, 0))],
        out_specs=pl.BlockSpec((tm, w), lambda i: (i, 0)),
        out_shape=jax.ShapeDtypeStruct((t, w), BF16),
        compiler_params=_params(1),
        name="sgu_mix",
    )(gz, ln_g.reshape(1, w), ln_b.reshape(1, w), w_s, bsx)


def _upgate_kernel(xn_ref, ya_ref, yb_ref, wgc_ref, wgs_ref, wua_ref, wub_ref, eg_ref, eu_ref, ed_ref,
                   o_ref, eg_bf_ref, eu_bf_ref, ed_bf_ref, wgbf_ref, wuabf_ref, wubbf_ref):
    tn = wgc_ref.shape[1]
    eg_bf_ref[...] = eg_ref[...].astype(BF16)
    eu_bf_ref[...] = eu_ref[...].astype(BF16)
    ed_bf_ref[...] = ed_ref[...].astype(BF16)

    @pl.when(pl.program_id(1) == 0)
    def _():
        wgbf_ref[:, 0:tn] = wgc_ref[...].astype(BF16)
        wgbf_ref[:, tn:2 * tn] = wgs_ref[...].astype(BF16)
        wuabf_ref[...] = wua_ref[...].astype(BF16)
        wubbf_ref[...] = wub_ref[...].astype(BF16)

    gl = _dot(xn_ref[...], wgbf_ref[...])
    a = _dot(ya_ref[...], wuabf_ref[...])
    b = _dot(yb_ref[...], wubbf_ref[...])
    m = jax.nn.sigmoid(gl[:, 0:tn]) * a + jax.nn.sigmoid(gl[:, tn:2 * tn]) * b
    o_ref[...] = m.astype(o_ref.dtype)


def _upgate(xn, ya, yb, w_in, gate_col0, w_up_a, w_up_b, w_eg, w_eu, w_ed, tm=512, tn=512):
    t, d = xn.shape
    wa = ya.shape[1]
    wb = yb.shape[1]
    dout = w_up_a.shape[1]
    c0 = gate_col0 // tn
    nj = dout // tn
    ni = t // tm
    n_e, d_e, f_e = w_eg.shape
    up_rows = n_e * d_e // (nj * ni)
    down_rows = n_e * f_e // (nj * ni)
    assert up_rows * nj * ni == n_e * d_e and up_rows % 16 == 0
    assert down_rows * nj * ni == n_e * f_e and down_rows % 16 == 0
    up_spec = pl.BlockSpec((up_rows, f_e), lambda j, i: (j * ni + i, 0))
    down_spec = pl.BlockSpec((down_rows, d_e), lambda j, i: (j * ni + i, 0))
    m, eg_bf, eu_bf, ed_bf = pl.pallas_call(
        _upgate_kernel,
        grid=(nj, ni),
        in_specs=[pl.BlockSpec((tm, d), lambda j, i: (i, 0)),
                  pl.BlockSpec((tm, wa), lambda j, i: (i, 0)),
                  pl.BlockSpec((tm, wb), lambda j, i: (i, 0)),
                  pl.BlockSpec((d, tn), lambda j, i: (0, c0 + j)),
                  pl.BlockSpec((d, tn), lambda j, i: (0, c0 + nj + j)),
                  pl.BlockSpec((wa, tn), lambda j, i: (0, j)),
                  pl.BlockSpec((wb, tn), lambda j, i: (0, j)),
                  up_spec, up_spec, down_spec],
        out_specs=[pl.BlockSpec((tm, tn), lambda j, i: (i, j)), up_spec, up_spec, down_spec],
        out_shape=[jax.ShapeDtypeStruct((t, dout), BF16),
                   jax.ShapeDtypeStruct((n_e * d_e, f_e), BF16),
                   jax.ShapeDtypeStruct((n_e * d_e, f_e), BF16),
                   jax.ShapeDtypeStruct((n_e * f_e, d_e), BF16)],
        scratch_shapes=[pltpu.VMEM((d, 2 * tn), BF16),
                        pltpu.VMEM((wa, tn), BF16),
                        pltpu.VMEM((wb, tn), BF16)],
        compiler_params=_params(2),
        name="upgate",
    )(xn, ya, yb, w_in, w_in, w_up_a, w_up_b,
      w_eg.reshape(n_e * d_e, f_e), w_eu.reshape(n_e * d_e, f_e), w_ed.reshape(n_e * f_e, d_e))
    return (m, eg_bf.reshape(n_e, d_e, f_e), eu_bf.reshape(n_e, d_e, f_e),
            ed_bf.reshape(n_e, f_e, d_e))


def _outproj_kernel(m_ref, w_ref, x_ref, o_ref, wbf_ref):
    @pl.when(pl.program_id(1) == 0)
    def _():
        wbf_ref[...] = w_ref[...].astype(BF16)

    o_ref[...] = x_ref[...] + _dot(m_ref[...], wbf_ref[...])


def _outproj(m, w_out, x, tm=512, tn=1024):
    t, d = m.shape
    dout = w_out.shape[1]
    return pl.pallas_call(
        _outproj_kernel,
        grid=(dout // tn, t // tm),
        in_specs=[pl.BlockSpec((tm, d), lambda j, i: (i, 0)),
                  pl.BlockSpec((d, tn), lambda j, i: (0, j)),
                  pl.BlockSpec((tm, tn), lambda j, i: (i, j))],
        out_specs=pl.BlockSpec((tm, tn), lambda j, i: (i, j)),
        out_shape=jax.ShapeDtypeStruct((t, dout), F32),
        scratch_shapes=[pltpu.VMEM((d, tn), BF16)],
        compiler_params=_params(2),
        name="outproj",
    )(m, w_out, x)


def _argmax_rows(rows):
    best = rows[0]
    idx = jnp.zeros(rows[0].shape, jnp.int32)
    for k in range(1, len(rows)):
        better = rows[k] > best
        best = jnp.where(better, rows[k], best)
        idx = jnp.where(better, k, idx)
    return best, idx


def _softmax_rows(rows):
    mx = functools.reduce(jnp.maximum, rows)
    ex = [jnp.exp(r - mx) for r in rows]
    den = functools.reduce(lambda a, b: a + b, ex)
    return [e / den for e in ex]


def _route_sort_kernel(h_ref, g_ref, wr_ref, br_ref, xs_ref, meta_ref, cnt_ref):
    tm = h_ref.shape[0]
    xn = _rms_scale(h_ref[...], g_ref[...])
    lt = lax.dot_general(wr_ref[...], xn, (((1,), (1,)), ((), ())),
                         precision=lax.Precision.HIGHEST,
                         preferred_element_type=F32) + br_ref[...]
    pgs = _softmax_rows([lt[k:k + 1, :] for k in range(N_GROUPS)])
    pg, gi = _argmax_rows(pgs)
    sel = []
    for k in range(EXPERTS_PER_GROUP):
        r = jnp.zeros_like(pg)
        for g in range(N_GROUPS):
            row = N_GROUPS + g * EXPERTS_PER_GROUP + k
            r = jnp.where(gi == g, lt[row:row + 1, :], r)
        sel.append(r)
    pes = _softmax_rows(sel)
    p1, e1 = _argmax_rows(pes)
    rest = [jnp.where(e1 == k, -1.0, pes[k]) for k in range(EXPERTS_PER_GROUP)]
    p2, e2 = _argmax_rows(rest)
    den = p1 + p2
    w1 = pg * (p1 / den)
    w2 = pg * (p2 / den)
    lo = jnp.minimum(e1, e2)
    hi = jnp.maximum(e1, e2)
    w_lo = jnp.where(e1 < e2, w1, w2)
    w_hi = jnp.where(e1 < e2, w2, w1)
    ea = gi * EXPERTS_PER_GROUP + lo
    eb = gi * EXPERTS_PER_GROUP + hi

    erow = lax.broadcasted_iota(jnp.int32, (N_EXPERTS, tm), 0)
    oh_a = (erow == ea).astype(F32)
    oh_b = (erow == eb).astype(F32)
    a = lax.broadcasted_iota(jnp.int32, (tm, tm), 0)
    b = lax.broadcasted_iota(jnp.int32, (tm, tm), 1)
    before = (a < b).astype(BF16)
    cum = _dot((oh_a + oh_b).astype(BF16), before)
    cnt = jnp.sum(oh_a + oh_b, axis=1, keepdims=True)
    padded = jnp.floor((cnt + (GROUP - 1)) * (1.0 / GROUP)) * GROUP
    pos_a = jnp.sum(oh_a * cum + jnp.where(erow < ea, padded, 0.0), axis=0, keepdims=True)
    pos_b = jnp.sum(oh_b * cum + jnp.where(erow < eb, padded, 0.0), axis=0, keepdims=True)

    q = lax.broadcasted_iota(jnp.int32, (xs_ref.shape[0], tm), 0)
    perm = jnp.where((q == pos_a.astype(jnp.int32)) | (q == pos_b.astype(jnp.int32)), 1.0, 0.0)
    xs_ref[...] = _dot(perm.astype(BF16), xn.astype(BF16)).astype(xs_ref.dtype)

    cnt_ref[...] = jnp.broadcast_to(cnt, cnt_ref.shape)
    meta_ref[0:1, :] = pos_a
    meta_ref[1:2, :] = pos_b
    meta_ref[2:3, :] = w_lo
    meta_ref[3:4, :] = w_hi
    meta_ref[4:8, :] = jnp.zeros((4, tm), F32)


def _route_sort(h1, g, w_rg, b_rg, w_re, b_re):
    t, d = h1.shape
    tm = ROUTE_BLOCK
    nb = t // tm
    n_log = w_rg.shape[1] + w_re.shape[1]
    wr = jnp.concatenate([w_rg, w_re], axis=1).T
    wr = jnp.pad(wr, ((0, ROUTER_ROWS - n_log), (0, 0)))
    br = jnp.pad(jnp.concatenate([b_rg, b_re]), (0, ROUTER_ROWS - n_log)).reshape(ROUTER_ROWS, 1)
    return pl.pallas_call(
        _route_sort_kernel,
        grid=(nb,),
        in_specs=[pl.BlockSpec((tm, d), lambda i: (i, 0)),
                  pl.BlockSpec((1, d), lambda i: (0, 0)),
                  pl.BlockSpec((ROUTER_ROWS, d), lambda i: (0, 0)),
                  pl.BlockSpec((ROUTER_ROWS, 1), lambda i: (0, 0))],
        out_specs=[pl.BlockSpec((LOCAL_ROWS, d), lambda i: (i, 0)),
                   pl.BlockSpec((8, tm), lambda i: (0, i)),
                   pl.BlockSpec((N_EXPERTS, LANES), lambda i: (i, 0))],
        out_shape=[jax.ShapeDtypeStruct((nb * LOCAL_ROWS, d), BF16),
                   jax.ShapeDtypeStruct((8, t), F32),
                   jax.ShapeDtypeStruct((nb * N_EXPERTS, LANES), F32)],
        compiler_params=_params(1),
        name="route_sort",
    )(h1, g.reshape(1, d), wr, br)


def _group_copy(src_ref, s_group, dst_ref, d_group, sem):
    return pltpu.make_async_copy(src_ref.at[pl.ds(pl.multiple_of(s_group * GROUP, GROUP), GROUP), :],
                                 dst_ref.at[pl.ds(pl.multiple_of(d_group * GROUP, GROUP), GROUP), :], sem)


def _expert_kernel(te_ref, nxt_ref, nt_ref, gsrc_ref, gdst_ref, tail_ref,
                   xs_ref, wg_ref, wu_ref, wd_ref, ys_ref,
                   xbuf_ref, ybuf_ref, zero_ref, wg_buf_ref, wu_buf_ref, wd_buf_ref,
                   cur_ref, gsem, ssem, zsem, wsem):
    r = pl.program_id(0)
    nt = nt_ref[0]

    def gather(q, slot):
        for i in range(TILE_GROUPS):
            _group_copy(xs_ref, gsrc_ref[q * TILE_GROUPS + i], xbuf_ref, slot * TILE_GROUPS + i,
                        gsem.at[slot]).start(priority=1)

    def weight_copies(e, wslot):
        return (pltpu.make_async_copy(wg_ref.at[e], wg_buf_ref.at[wslot], wsem.at[wslot]),
                pltpu.make_async_copy(wu_ref.at[e], wu_buf_ref.at[wslot], wsem.at[wslot]),
                pltpu.make_async_copy(wd_ref.at[e], wd_buf_ref.at[wslot], wsem.at[wslot]))

    def tile_wait(src, dst, sem):
        pltpu.make_async_copy(src.at[pl.ds(0, MOE_TILE), :], dst.at[pl.ds(0, MOE_TILE), :], sem).wait()

    @pl.when(r == 0)
    def _():
        cur_ref[0] = -1
        cur_ref[1] = -1
        for cp in weight_copies(te_ref[0], 0):
            cp.start()
        zero_ref[...] = jnp.zeros_like(zero_ref)
        n_blocks = tail_ref.shape[0]

        def fill(make):
            def body(g, c):
                make(g)
                return c
            return body

        for blk in range(n_blocks):
            lax.fori_loop(tail_ref[blk], GROUPS_PER_BLOCK, fill(
                lambda g, blk=blk: _group_copy(zero_ref, 0, ys_ref, blk * GROUPS_PER_BLOCK + g, zsem).start()), 0)
        for blk in range(n_blocks):
            lax.fori_loop(tail_ref[blk], GROUPS_PER_BLOCK, fill(
                lambda g, blk=blk: _group_copy(zero_ref, 0, ys_ref, blk * GROUPS_PER_BLOCK + g, zsem).wait()), 0)
        gather(0, 0)

    @pl.when(r < nt)
    def _():
        e = te_ref[r]
        slot = r % 2

        gather(jnp.minimum(r + 1, nt - 1), 1 - slot)

        @pl.when(cur_ref[0] != e)
        def _():
            wslot = (cur_ref[1] + 1) % 2
            for cp in weight_copies(e, wslot):
                cp.wait()
            cur_ref[0] = e
            cur_ref[1] = cur_ref[1] + 1

            @pl.when(nxt_ref[r] >= 0)
            def _():
                for cp in weight_copies(nxt_ref[r], 1 - wslot):
                    cp.start()

        wslot = cur_ref[1] % 2
        tile_wait(xs_ref, xbuf_ref, gsem.at[slot])
        row0 = pl.multiple_of(slot * MOE_TILE, MOE_TILE)
        x = xbuf_ref[pl.ds(row0, MOE_TILE), :]
        hid = (jax.nn.silu(_dot(x, wg_buf_ref[wslot])) * _dot(x, wu_buf_ref[wslot])).astype(BF16)
        y = _dot(hid, wd_buf_ref[wslot]).astype(ybuf_ref.dtype)

        @pl.when(r >= 2)
        def _():
            tile_wait(ybuf_ref, ys_ref, ssem.at[slot])

        ybuf_ref[pl.ds(row0, MOE_TILE), :] = y
        for i in range(TILE_GROUPS):
            _group_copy(ybuf_ref, slot * TILE_GROUPS + i, ys_ref, gdst_ref[r * TILE_GROUPS + i],
                        ssem.at[slot]).start(priority=1)

        @pl.when(r == nt - 1)
        def _():
            tile_wait(xs_ref, xbuf_ref, gsem.at[1 - slot])
            tile_wait(ybuf_ref, ys_ref, ssem.at[slot])

            @pl.when(r >= 1)
            def _():
                tile_wait(ybuf_ref, ys_ref, ssem.at[1 - slot])


def _experts(plan, xs, w_gate, w_up, w_down):
    d = xs.shape[1]
    f = w_gate.shape[2]
    tile_e, next_e, n_tiles, gsrc, gdst, tail = plan
    r_max = tile_e.shape[0]
    n_blocks = tail.shape[0]
    any_spec = pl.BlockSpec(memory_space=pl.ANY)
    grid_spec = pltpu.PrefetchScalarGridSpec(
        num_scalar_prefetch=6,
        grid=(r_max,),
        in_specs=[any_spec, any_spec, any_spec, any_spec],
        out_specs=any_spec,
        scratch_shapes=[pltpu.VMEM((2 * MOE_TILE, d), BF16),
                        pltpu.VMEM((2 * MOE_TILE, d), BF16),
                        pltpu.VMEM((GROUP, d), BF16),
                        pltpu.VMEM((2, d, f), BF16),
                        pltpu.VMEM((2, d, f), BF16),
                        pltpu.VMEM((2, f, d), BF16),
                        pltpu.SMEM((2,), jnp.int32),
                        pltpu.SemaphoreType.DMA((2,)),
                        pltpu.SemaphoreType.DMA((2,)),
                        pltpu.SemaphoreType.DMA(()),
                        pltpu.SemaphoreType.DMA((2,))],
    )
    return pl.pallas_call(
        _expert_kernel,
        grid_spec=grid_spec,
        out_shape=jax.ShapeDtypeStruct((n_blocks * LOCAL_ROWS, d), BF16),
        compiler_params=_params(1),
        name="experts",
    )(tile_e, next_e, n_tiles, gsrc, gdst, tail, xs, w_gate, w_up, w_down)


def _combine_kernel(ys_ref, h_ref, cm_ref, g_ref, o_ref):
    tm = h_ref.shape[0]
    cm = cm_ref[...]
    q = lax.broadcasted_iota(jnp.int32, (tm, ys_ref.shape[0]), 1)
    ys = ys_ref[...]
    sel_a = jnp.where(q == cm[:, 0:1].astype(jnp.int32), 1.0, 0.0).astype(BF16)
    sel_b = jnp.where(q == cm[:, 1:2].astype(jnp.int32), 1.0, 0.0).astype(BF16)
    h2 = h_ref[...] + cm[:, 2:3] * _dot(sel_a, ys) + cm[:, 3:4] * _dot(sel_b, ys)
    o_ref[...] = _rms_scale(h2, g_ref[...])


def _combine(ys, h1, cmeta, g):
    t, d = h1.shape
    tm = ROUTE_BLOCK
    return pl.pallas_call(
        _combine_kernel,
        grid=(t // tm,),
        in_specs=[pl.BlockSpec((LOCAL_ROWS, d), lambda i: (i, 0)),
                  pl.BlockSpec((tm, d), lambda i: (i, 0)),
                  pl.BlockSpec((tm, LANES), lambda i: (i, 0)),
                  pl.BlockSpec((1, d), lambda i: (0, 0))],
        out_specs=pl.BlockSpec((tm, d), lambda i: (i, 0)),
        out_shape=jax.ShapeDtypeStruct((t, d), F32),
        compiler_params=_params(1),
        name="combine",
    )(ys, h1, cmeta, g.reshape(1, d))


def _expert_plan(counts, n_blocks):
    cnt = counts[:, 0].astype(jnp.int32).reshape(n_blocks, N_EXPERTS)
    groups = (cnt + GROUP - 1) // GROUP
    first = jnp.cumsum(groups, axis=1) - groups
    upto = jnp.cumsum(groups, axis=0)
    per_expert = upto[-1]
    tiles_e = (per_expert + TILE_GROUPS - 1) // TILE_GROUPS
    tile_end = jnp.cumsum(tiles_e)
    n_tiles = tile_end[-1]
    max_groups = 2 * ROUTE_BLOCK * n_blocks // GROUP + n_blocks * N_EXPERTS
    r_max = max_groups // TILE_GROUPS + N_EXPERTS
    tile_ids = jnp.arange(r_max, dtype=jnp.int32)
    tile = jnp.minimum(tile_ids, n_tiles - 1)
    tile_e = jnp.sum((tile_end[None, :] <= tile[:, None]).astype(jnp.int32), axis=1)
    later = (tile_e[None, :] > tile_e[:, None]) & (tile_ids[None, :] < n_tiles)
    next_e = jnp.min(jnp.where(later, tile_e[None, :], N_EXPERTS), axis=1)
    next_e = jnp.where(next_e == N_EXPERTS, -1, next_e)

    slot = jnp.arange(r_max * TILE_GROUPS, dtype=jnp.int32)
    s_tile = slot // TILE_GROUPS
    oh_e = jnp.repeat(tile_e, TILE_GROUPS)[:, None] == jnp.arange(N_EXPERTS, dtype=jnp.int32)[None, :]

    def by_expert(table):
        return jnp.sum(jnp.where(oh_e[:, None, :], table[None], 0), axis=-1)

    k = slot - by_expert(((tile_end - tiles_e) * TILE_GROUPS)[None, :])[:, 0]
    real = (k < by_expert(per_expert[None, :])[:, 0]) & (s_tile < n_tiles)
    upto_e = by_expert(upto)
    blk = jnp.minimum(jnp.sum((upto_e <= k[:, None]).astype(jnp.int32), axis=1), n_blocks - 1)
    oh_b = blk[:, None] == jnp.arange(n_blocks, dtype=jnp.int32)[None, :]

    def by_block(table_se):
        return jnp.sum(jnp.where(oh_b, table_se, 0), axis=1)

    before = by_block(upto_e - by_expert(groups))
    src = blk * GROUPS_PER_BLOCK + by_block(by_expert(first)) + (k - before)
    zero_group = GROUPS_PER_BLOCK - 1
    spare = n_blocks * GROUPS_PER_BLOCK + (s_tile % 2) * TILE_GROUPS + slot % TILE_GROUPS
    gsrc = jnp.where(real, src, zero_group)
    gdst = jnp.where(real, src, spare)
    tail = jnp.concatenate([jnp.sum(groups, axis=1), jnp.zeros((1,), jnp.int32)])
    return tile_e, next_e, n_tiles.reshape(1), gsrc, gdst, tail


def _layer(h, norm_mix_g, w_in, conv_w, sgu_ln_g, sgu_ln_b, sgu_w_s, sgu_b_s, w_up_conv,
           w_up_sgu, w_out, norm_ffn_g, w_rg, b_rg, w_re, b_re, w_eg, w_eu, w_ed, out_g):
    t, d = h.shape
    conv_width = conv_w.shape[1]
    sgu_width = sgu_ln_g.shape[0]
    xn = _rmsnorm(h, norm_mix_g)
    ya = _conv_branch(xn, w_in, conv_w, conv_width)
    gz = _zproj(xn, w_in, 3 * conv_width, 2 * sgu_width)
    yb = _sgu_mix(gz, sgu_ln_g, sgu_ln_b, sgu_w_s, sgu_b_s)
    m, eg_bf, eu_bf, ed_bf = _upgate(xn, ya, yb, w_in, 3 * conv_width + 2 * sgu_width,
                                     w_up_conv, w_up_sgu, w_eg, w_eu, w_ed)
    h1 = _outproj(m, w_out, h)
    xs, meta, counts = _route_sort(h1, norm_ffn_g, w_rg, b_rg, w_re, b_re)
    ys = _experts(_expert_plan(counts, t // ROUTE_BLOCK), xs, eg_bf, eu_bf, ed_bf)
    cmeta = jnp.pad(meta[0:4].T, ((0, 0), (0, LANES - 4)))
    return _combine(ys, h1, cmeta, out_g)


def kernel(x, norm_mix_g, w_in, conv_w, sgu_ln_g, sgu_ln_b, sgu_w_s, sgu_b_s, w_up_conv, w_up_sgu, w_out, norm_ffn_g, w_router_group, b_router_group, w_router_expert, b_router_expert, w_exp_gate, w_exp_up, w_exp_down, norm_final_g):
    bsz, s, d = x.shape
    depth = w_in.shape[0]
    assert bsz == 1 and depth == 1, "causal conv carry and the fused final norm assume one sequence, one layer"
    assert s % ROUTE_BLOCK == 0
    out = _layer(x.reshape(s, d), norm_mix_g[0], w_in[0], conv_w[0], sgu_ln_g[0], sgu_ln_b[0],
                 sgu_w_s[0], sgu_b_s[0], w_up_conv[0], w_up_sgu[0], w_out[0], norm_ffn_g[0],
                 w_router_group[0], b_router_group[0], w_router_expert[0], b_router_expert[0],
                 w_exp_gate[0], w_exp_up[0], w_exp_down[0], norm_final_g)
    return out.reshape(bsz, s, d)
```

```python
import functools

import jax
import jax.numpy as jnp
from jax import lax
from jax.experimental import pallas as pl
from jax.experimental.pallas import tpu as pltpu

F32 = jnp.float32
BF16 = jnp.bfloat16

EPS = 1e-6
CHUNK = 64
CONV_K = 3
SGU_HEADS = 8
SGU_BLOCK = 128
N_GROUPS = 4
EXPERTS_PER_GROUP = 4
N_EXPERTS = N_GROUPS * EXPERTS_PER_GROUP
ROUTER_ROWS = 32
LANES = 128

VMEM_LIMIT_BYTES = 56 * 1024 * 1024

ROUTE_BLOCK = 512
GROUP = 16
LOCAL_ROWS = -(-(2 * ROUTE_BLOCK + N_EXPERTS * (GROUP - 1)) // 256) * 256
GROUPS_PER_BLOCK = LOCAL_ROWS // GROUP
MOE_TILE = 512
TILE_GROUPS = MOE_TILE // GROUP


def _params(n_axes):
    return pltpu.CompilerParams(
        dimension_semantics=("arbitrary",) * n_axes,
        vmem_limit_bytes=VMEM_LIMIT_BYTES)


def _dot(a, b):
    return jnp.dot(a, b, preferred_element_type=F32)


def _rms_scale(x, g):
    ms = jnp.mean(x * x, axis=-1, keepdims=True)
    return x * lax.rsqrt(ms + EPS) * g


def _rmsnorm_kernel(x_ref, g_ref, o_ref):
    o_ref[...] = _rms_scale(x_ref[...], g_ref[...]).astype(o_ref.dtype)


def _rmsnorm(x, g, tm=512):
    t, d = x.shape
    return pl.pallas_call(
        _rmsnorm_kernel,
        grid=(t // tm,),
        in_specs=[pl.BlockSpec((tm, d), lambda i: (i, 0)),
                  pl.BlockSpec((1, d), lambda i: (0, 0))],
        out_specs=pl.BlockSpec((tm, d), lambda i: (i, 0)),
        out_shape=jax.ShapeDtypeStruct((t, d), BF16),
        compiler_params=_params(1),
        name="rmsnorm",
    )(x, g.reshape(1, d))


def _conv_kernel(xn_ref, wb_ref, wc_ref, wh_ref, cw_ref, o_ref, wbf_ref, carry_ref):
    i = pl.program_id(1)
    tn = wb_ref.shape[1]
    tm = xn_ref.shape[0]

    @pl.when(i == 0)
    def _():
        wbf_ref[:, 0:tn] = wb_ref[...].astype(BF16)
        wbf_ref[:, tn:2 * tn] = wc_ref[...].astype(BF16)
        wbf_ref[:, 2 * tn:3 * tn] = wh_ref[...].astype(BF16)
        carry_ref[...] = jnp.zeros_like(carry_ref)

    proj = _dot(xn_ref[...], wbf_ref[...])
    b = proj[:, 0:tn]
    p = proj[:, tn:2 * tn] * proj[:, 2 * tn:3 * tn]
    prev = carry_ref[...]
    carry_ref[...] = p[tm - 8:tm, :]
    row = lax.broadcasted_iota(jnp.int32, p.shape, 0)
    p1 = jnp.where(row == 0, prev[7:8, :], pltpu.roll(p, 1, axis=0))
    p2 = jnp.where(row == 0, prev[6:7, :],
                   jnp.where(row == 1, prev[7:8, :], pltpu.roll(p, 2, axis=0)))
    cw = cw_ref[...]
    y = b * (cw[0:1, :] * p2 + cw[1:2, :] * p1 + cw[2:3, :] * p)
    o_ref[...] = y.astype(o_ref.dtype)


def _conv_branch(xn, w_in, conv_w, width, tm=1024, tn=256):
    t, d = xn.shape
    nj = width // tn
    return pl.pallas_call(
        _conv_kernel,
        grid=(nj, t // tm),
        in_specs=[pl.BlockSpec((tm, d), lambda j, i: (i, 0)),
                  pl.BlockSpec((d, tn), lambda j, i: (0, j)),
                  pl.BlockSpec((d, tn), lambda j, i: (0, nj + j)),
                  pl.BlockSpec((d, tn), lambda j, i: (0, 2 * nj + j)),
                  pl.BlockSpec((CONV_K, tn), lambda j, i: (0, j))],
        out_specs=pl.BlockSpec((tm, tn), lambda j, i: (i, j)),
        out_shape=jax.ShapeDtypeStruct((t, width), BF16),
        scratch_shapes=[pltpu.VMEM((d, 3 * tn), BF16),
                        pltpu.VMEM((8, tn), F32)],
        compiler_params=_params(2),
        name="conv_branch",
    )(xn, w_in, w_in, w_in, conv_w)


def _zproj_kernel(xn_ref, w_ref, o_ref, wbf_ref):
    @pl.when(pl.program_id(1) == 0)
    def _():
        wbf_ref[...] = w_ref[...].astype(BF16)

    z = _dot(xn_ref[...], wbf_ref[...])
    o_ref[...] = jax.nn.gelu(z).astype(o_ref.dtype)


def _zproj(xn, w_in, col0, width, tm=1024, tn=512):
    t, d = xn.shape
    c0 = col0 // tn
    return pl.pallas_call(
        _zproj_kernel,
        grid=(width // tn, t // tm),
        in_specs=[pl.BlockSpec((tm, d), lambda j, i: (i, 0)),
                  pl.BlockSpec((d, tn), lambda j, i: (0, c0 + j))],
        out_specs=pl.BlockSpec((tm, tn), lambda j, i: (i, j)),
        out_shape=jax.ShapeDtypeStruct((t, width), BF16),
        scratch_shapes=[pltpu.VMEM((d, tn), BF16)],
        compiler_params=_params(2),
        name="sgu_zproj",
    )(xn, w_in)


def _sgu_kernel(gz_ref, lng_ref, lnb_ref, ws_ref, bsx_ref, o_ref):
    tm = gz_ref.shape[0]
    w = o_ref.shape[1]
    hd = w // SGU_HEADS
    v = gz_ref[:, w:2 * w].astype(F32)
    mu = jnp.mean(v, axis=-1, keepdims=True)
    vc = v - mu
    var = jnp.mean(vc * vc, axis=-1, keepdims=True)
    vn = (vc * lax.rsqrt(var + EPS) * lng_ref[...] + lnb_ref[...]).astype(BF16)
    ii = lax.broadcasted_iota(jnp.int32, (SGU_BLOCK, SGU_BLOCK), 0)
    jj = lax.broadcasted_iota(jnp.int32, (SGU_BLOCK, SGU_BLOCK), 1)
    mask = (jj // CHUNK) <= (ii // CHUNK)
    for h in range(SGU_HEADS):
        wm = jnp.where(mask, ws_ref[h], 0.0).astype(BF16)
        cs = slice(h * hd, (h + 1) * hd)
        for n in range(tm // SGU_BLOCK):
            rs = slice(n * SGU_BLOCK, (n + 1) * SGU_BLOCK)
            vm = _dot(wm, vn[rs, cs]) + bsx_ref[:, cs]
            o_ref[rs, cs] = (gz_ref[rs, cs].astype(F32) * vm).astype(o_ref.dtype)


def _sgu_mix(gz, ln_g, ln_b, w_s, b_s, tm=512):
    t, w2 = gz.shape
    w = w2 // 2
    hd = w // SGU_HEADS
    bsx = jnp.repeat(b_s.T, hd, axis=1)
    return pl.pallas_call(
        _sgu_kernel,
        grid=(t // tm,),
        in_specs=[pl.BlockSpec((tm, w2), lambda i: (i, 0)),
                  pl.BlockSpec((1, w), lambda i: (0, 0)),
                  pl.BlockSpec((1, w), lambda i: (0, 0)),
                  pl.BlockSpec((SGU_HEADS, SGU_BLOCK, SGU_BLOCK), lambda i: (0, 0, 0)),
                  pl.BlockSpec((SGU_BLOCK, w), lambda i: (0, 0))],
        out_specs=pl.BlockSpec((tm, w), lambda i: (i, 0)),
        out_shape=jax.ShapeDtypeStruct((t, w), BF16),
        compiler_params=_params(1),
        name="sgu_mix",
    )(gz, ln_g.reshape(1, w), ln_b.reshape(1, w), w_s, bsx)


def _upgate_kernel(xn_ref, ya_ref, yb_ref, wgc_ref, wgs_ref, wua_ref, wub_ref, eg_ref, eu_ref, ed_ref,
                   o_ref, eg_bf_ref, eu_bf_ref, ed_bf_ref, wgbf_ref, wuabf_ref, wubbf_ref):
    tn = wgc_ref.shape[1]
    eg_bf_ref[...] = eg_ref[...].astype(BF16)
    eu_bf_ref[...] = eu_ref[...].astype(BF16)
    ed_bf_ref[...] = ed_ref[...].astype(BF16)

    @pl.when(pl.program_id(1) == 0)
    def _():
        wgbf_ref[:, 0:tn] = wgc_ref[...].astype(BF16)
        wgbf_ref[:, tn:2 * tn] = wgs_ref[...].astype(BF16)
        wuabf_ref[...] = wua_ref[...].astype(BF16)
        wubbf_ref[...] = wub_ref[...].astype(BF16)

    gl = _dot(xn_ref[...], wgbf_ref[...])
    a = _dot(ya_ref[...], wuabf_ref[...])
    b = _dot(yb_ref[...], wubbf_ref[...])
    m = jax.nn.sigmoid(gl[:, 0:tn]) * a + jax.nn.sigmoid(gl[:, tn:2 * tn]) * b
    o_ref[...] = m.astype(o_ref.dtype)


def _upgate(xn, ya, yb, w_in, gate_col0, w_up_a, w_up_b, w_eg, w_eu, w_ed, tm=512, tn=512):
    t, d = xn.shape
    wa = ya.shape[1]
    wb = yb.shape[1]
    dout = w_up_a.shape[1]
    c0 = gate_col0 // tn
    nj = dout // tn
    ni = t // tm
    n_e, d_e, f_e = w_eg.shape
    up_rows = n_e * d_e // (nj * ni)
    down_rows = n_e * f_e // (nj * ni)
    assert up_rows * nj * ni == n_e * d_e and up_rows % 16 == 0
    assert down_rows * nj * ni == n_e * f_e and down_rows % 16 == 0
    up_spec = pl.BlockSpec((up_rows, f_e), lambda j, i: (j * ni + i, 0))
    down_spec = pl.BlockSpec((down_rows, d_e), lambda j, i: (j * ni + i, 0))
    m, eg_bf, eu_bf, ed_bf = pl.pallas_call(
        _upgate_kernel,
        grid=(nj, ni),
        in_specs=[pl.BlockSpec((tm, d), lambda j, i: (i, 0)),
                  pl.BlockSpec((tm, wa), lambda j, i: (i, 0)),
                  pl.BlockSpec((tm, wb), lambda j, i: (i, 0)),
                  pl.BlockSpec((d, tn), lambda j, i: (0, c0 + j)),
                  pl.BlockSpec((d, tn), lambda j, i: (0, c0 + nj + j)),
                  pl.BlockSpec((wa, tn), lambda j, i: (0, j)),
                  pl.BlockSpec((wb, tn), lambda j, i: (0, j)),
                  up_spec, up_spec, down_spec],
        out_specs=[pl.BlockSpec((tm, tn), lambda j, i: (i, j)), up_spec, up_spec, down_spec],
        out_shape=[jax.ShapeDtypeStruct((t, dout), BF16),
                   jax.ShapeDtypeStruct((n_e * d_e, f_e), BF16),
                   jax.ShapeDtypeStruct((n_e * d_e, f_e), BF16),
                   jax.ShapeDtypeStruct((n_e * f_e, d_e), BF16)],
        scratch_shapes=[pltpu.VMEM((d, 2 * tn), BF16),
                        pltpu.VMEM((wa, tn), BF16),
                        pltpu.VMEM((wb, tn), BF16)],
        compiler_params=_params(2),
        name="upgate",
    )(xn, ya, yb, w_in, w_in, w_up_a, w_up_b,
      w_eg.reshape(n_e * d_e, f_e), w_eu.reshape(n_e * d_e, f_e), w_ed.reshape(n_e * f_e, d_e))
    return (m, eg_bf.reshape(n_e, d_e, f_e), eu_bf.reshape(n_e, d_e, f_e),
            ed_bf.reshape(n_e, f_e, d_e))


def _outproj_kernel(m_ref, w_ref, x_ref, o_ref, wbf_ref):
    @pl.when(pl.program_id(1) == 0)
    def _():
        wbf_ref[...] = w_ref[...].astype(BF16)

    o_ref[...] = x_ref[...] + _dot(m_ref[...], wbf_ref[...])


def _outproj(m, w_out, x, tm=512, tn=1024):
    t, d = m.shape
    dout = w_out.shape[1]
    return pl.pallas_call(
        _outproj_kernel,
        grid=(dout // tn, t // tm),
        in_specs=[pl.BlockSpec((tm, d), lambda j, i: (i, 0)),
                  pl.BlockSpec((d, tn), lambda j, i: (0, j)),
                  pl.BlockSpec((tm, tn), lambda j, i: (i, j))],
        out_specs=pl.BlockSpec((tm, tn), lambda j, i: (i, j)),
        out_shape=jax.ShapeDtypeStruct((t, dout), F32),
        scratch_shapes=[pltpu.VMEM((d, tn), BF16)],
        compiler_params=_params(2),
        name="outproj",
    )(m, w_out, x)


def _argmax_rows(rows):
    best = rows[0]
    idx = jnp.zeros(rows[0].shape, jnp.int32)
    for k in range(1, len(rows)):
        better = rows[k] > best
        best = jnp.where(better, rows[k], best)
        idx = jnp.where(better, k, idx)
    return best, idx


def _softmax_rows(rows):
    mx = functools.reduce(jnp.maximum, rows)
    ex = [jnp.exp(r - mx) for r in rows]
    den = functools.reduce(lambda a, b: a + b, ex)
    return [e / den for e in ex]


def _route_sort_kernel(h_ref, g_ref, wr_ref, br_ref, xs_ref, meta_ref, cnt_ref):
    tm = h_ref.shape[0]
    xn = _rms_scale(h_ref[...], g_ref[...])
    lt = lax.dot_general(wr_ref[...], xn, (((1,), (1,)), ((), ())),
                         precision=lax.Precision.HIGHEST,
                         preferred_element_type=F32) + br_ref[...]
    pgs = _softmax_rows([lt[k:k + 1, :] for k in range(N_GROUPS)])
    pg, gi = _argmax_rows(pgs)
    sel = []
    for k in range(EXPERTS_PER_GROUP):
        r = jnp.zeros_like(pg)
        for g in range(N_GROUPS):
            row = N_GROUPS + g * EXPERTS_PER_GROUP + k
            r = jnp.where(gi == g, lt[row:row + 1, :], r)
        sel.append(r)
    pes = _softmax_rows(sel)
    p1, e1 = _argmax_rows(pes)
    rest = [jnp.where(e1 == k, -1.0, pes[k]) for k in range(EXPERTS_PER_GROUP)]
    p2, e2 = _argmax_rows(rest)
    den = p1 + p2
    w1 = pg * (p1 / den)
    w2 = pg * (p2 / den)
    lo = jnp.minimum(e1, e2)
    hi = jnp.maximum(e1, e2)
    w_lo = jnp.where(e1 < e2, w1, w2)
    w_hi = jnp.where(e1 < e2, w2, w1)
    ea = gi * EXPERTS_PER_GROUP + lo
    eb = gi * EXPERTS_PER_GROUP + hi

    erow = lax.broadcasted_iota(jnp.int32, (N_EXPERTS, tm), 0)
    oh_a = (erow == ea).astype(F32)
    oh_b = (erow == eb).astype(F32)
    a = lax.broadcasted_iota(jnp.int32, (tm, tm), 0)
    b = lax.broadcasted_iota(jnp.int32, (tm, tm), 1)
    before = (a < b).astype(BF16)
    cum = _dot((oh_a + oh_b).astype(BF16), before)
    cnt = jnp.sum(oh_a + oh_b, axis=1, keepdims=True)
    padded = jnp.floor((cnt + (GROUP - 1)) * (1.0 / GROUP)) * GROUP
    pos_a = jnp.sum(oh_a * cum + jnp.where(erow < ea, padded, 0.0), axis=0, keepdims=True)
    pos_b = jnp.sum(oh_b * cum + jnp.where(erow < eb, padded, 0.0), axis=0, keepdims=True)

    q = lax.broadcasted_iota(jnp.int32, (xs_ref.shape[0], tm), 0)
    perm = jnp.where((q == pos_a.astype(jnp.int32)) | (q == pos_b.astype(jnp.int32)), 1.0, 0.0)
    xs_ref[...] = _dot(perm.astype(BF16), xn.astype(BF16)).astype(xs_ref.dtype)

    cnt_ref[...] = jnp.broadcast_to(cnt, cnt_ref.shape)
    meta_ref[0:1, :] = pos_a
    meta_ref[1:2, :] = pos_b
    meta_ref[2:3, :] = w_lo
    meta_ref[3:4, :] = w_hi
    meta_ref[4:8, :] = jnp.zeros((4, tm), F32)


def _route_sort(h1, g, w_rg, b_rg, w_re, b_re):
    t, d = h1.shape
    tm = ROUTE_BLOCK
    nb = t // tm
    n_log = w_rg.shape[1] + w_re.shape[1]
    wr = jnp.concatenate([w_rg, w_re], axis=1).T
    wr = jnp.pad(wr, ((0, ROUTER_ROWS - n_log), (0, 0)))
    br = jnp.pad(jnp.concatenate([b_rg, b_re]), (0, ROUTER_ROWS - n_log)).reshape(ROUTER_ROWS, 1)
    return pl.pallas_call(
        _route_sort_kernel,
        grid=(nb,),
        in_specs=[pl.BlockSpec((tm, d), lambda i: (i, 0)),
                  pl.BlockSpec((1, d), lambda i: (0, 0)),
                  pl.BlockSpec((ROUTER_ROWS, d), lambda i: (0, 0)),
                  pl.BlockSpec((ROUTER_ROWS, 1), lambda i: (0, 0))],
        out_specs=[pl.BlockSpec((LOCAL_ROWS, d), lambda i: (i, 0)),
                   pl.BlockSpec((8, tm), lambda i: (0, i)),
                   pl.BlockSpec((N_EXPERTS, LANES), lambda i: (i, 0))],
        out_shape=[jax.ShapeDtypeStruct((nb * LOCAL_ROWS, d), BF16),
                   jax.ShapeDtypeStruct((8, t), F32),
                   jax.ShapeDtypeStruct((nb * N_EXPERTS, LANES), F32)],
        compiler_params=_params(1),
        name="route_sort",
    )(h1, g.reshape(1, d), wr, br)


def _group_copy(src_ref, s_group, dst_ref, d_group, sem):
    return pltpu.make_async_copy(src_ref.at[pl.ds(pl.multiple_of(s_group * GROUP, GROUP), GROUP), :],
                                 dst_ref.at[pl.ds(pl.multiple_of(d_group * GROUP, GROUP), GROUP), :], sem)


def _expert_kernel(te_ref, nxt_ref, nt_ref, gsrc_ref, gdst_ref, tail_ref,
                   xs_ref, wg_ref, wu_ref, wd_ref, ys_ref,
                   xbuf_ref, ybuf_ref, zero_ref, wg_buf_ref, wu_buf_ref, wd_buf_ref,
                   cur_ref, gsem, ssem, zsem, wsem):
    r = pl.program_id(0)
    nt = nt_ref[0]

    def gather(q, slot):
        for i in range(TILE_GROUPS):
            _group_copy(xs_ref, gsrc_ref[q * TILE_GROUPS + i], xbuf_ref, slot * TILE_GROUPS + i,
                        gsem.at[slot]).start(priority=1)

    def weight_copies(e, wslot):
        return (pltpu.make_async_copy(wg_ref.at[e], wg_buf_ref.at[wslot], wsem.at[wslot]),
                pltpu.make_async_copy(wu_ref.at[e], wu_buf_ref.at[wslot], wsem.at[wslot]),
                pltpu.make_async_copy(wd_ref.at[e], wd_buf_ref.at[wslot], wsem.at[wslot]))

    def tile_wait(src, dst, sem):
        pltpu.make_async_copy(src.at[pl.ds(0, MOE_TILE), :], dst.at[pl.ds(0, MOE_TILE), :], sem).wait()

    @pl.when(r == 0)
    def _():
        cur_ref[0] = -1
        cur_ref[1] = -1
        for cp in weight_copies(te_ref[0], 0):
            cp.start()
        zero_ref[...] = jnp.zeros_like(zero_ref)
        n_blocks = tail_ref.shape[0]

        def fill(make):
            def body(g, c):
                make(g)
                return c
            return body

        for blk in range(n_blocks):
            lax.fori_loop(tail_ref[blk], GROUPS_PER_BLOCK, fill(
                lambda g, blk=blk: _group_copy(zero_ref, 0, ys_ref, blk * GROUPS_PER_BLOCK + g, zsem).start()), 0)
        for blk in range(n_blocks):
            lax.fori_loop(tail_ref[blk], GROUPS_PER_BLOCK, fill(
                lambda g, blk=blk: _group_copy(zero_ref, 0, ys_ref, blk * GROUPS_PER_BLOCK + g, zsem).wait()), 0)
        gather(0, 0)

    @pl.when(r < nt)
    def _():
        e = te_ref[r]
        slot = r % 2

        gather(jnp.minimum(r + 1, nt - 1), 1 - slot)

        @pl.when(cur_ref[0] != e)
        def _():
            wslot = (cur_ref[1] + 1) % 2
            for cp in weight_copies(e, wslot):
                cp.wait()
            cur_ref[0] = e
            cur_ref[1] = cur_ref[1] + 1

            @pl.when(nxt_ref[r] >= 0)
            def _():
                for cp in weight_copies(nxt_ref[r], 1 - wslot):
                    cp.start()

        wslot = cur_ref[1] % 2
        tile_wait(xs_ref, xbuf_ref, gsem.at[slot])
        row0 = pl.multiple_of(slot * MOE_TILE, MOE_TILE)
        x = xbuf_ref[pl.ds(row0, MOE_TILE), :]
        hid = (jax.nn.silu(_dot(x, wg_buf_ref[wslot])) * _dot(x, wu_buf_ref[wslot])).astype(BF16)
        y = _dot(hid, wd_buf_ref[wslot]).astype(ybuf_ref.dtype)

        @pl.when(r >= 2)
        def _():
            tile_wait(ybuf_ref, ys_ref, ssem.at[slot])

        ybuf_ref[pl.ds(row0, MOE_TILE), :] = y
        for i in range(TILE_GROUPS):
            _group_copy(ybuf_ref, slot * TILE_GROUPS + i, ys_ref, gdst_ref[r * TILE_GROUPS + i],
                        ssem.at[slot]).start(priority=1)

        @pl.when(r == nt - 1)
        def _():
            tile_wait(xs_ref, xbuf_ref, gsem.at[1 - slot])
            tile_wait(ybuf_ref, ys_ref, ssem.at[slot])

            @pl.when(r >= 1)
            def _():
                tile_wait(ybuf_ref, ys_ref, ssem.at[1 - slot])


def _experts(plan, xs, w_gate, w_up, w_down):
    d = xs.shape[1]
    f = w_gate.shape[2]
    tile_e, next_e, n_tiles, gsrc, gdst, tail = plan
    r_max = tile_e.shape[0]
    n_blocks = tail.shape[0]
    any_spec = pl.BlockSpec(memory_space=pl.ANY)
    grid_spec = pltpu.PrefetchScalarGridSpec(
        num_scalar_prefetch=6,
        grid=(r_max,),
        in_specs=[any_spec, any_spec, any_spec, any_spec],
        out_specs=any_spec,
        scratch_shapes=[pltpu.VMEM((2 * MOE_TILE, d), BF16),
                        pltpu.VMEM((2 * MOE_TILE, d), BF16),
                        pltpu.VMEM((GROUP, d), BF16),
                        pltpu.VMEM((2, d, f), BF16),
                        pltpu.VMEM((2, d, f), BF16),
                        pltpu.VMEM((2, f, d), BF16),
                        pltpu.SMEM((2,), jnp.int32),
                        pltpu.SemaphoreType.DMA((2,)),
                        pltpu.SemaphoreType.DMA((2,)),
                        pltpu.SemaphoreType.DMA(()),
                        pltpu.SemaphoreType.DMA((2,))],
    )
    return pl.pallas_call(
        _expert_kernel,
        grid_spec=grid_spec,
        out_shape=jax.ShapeDtypeStruct((n_blocks * LOCAL_ROWS, d), BF16),
        compiler_params=_params(1),
        name="experts",
    )(tile_e, next_e, n_tiles, gsrc, gdst, tail, xs, w_gate, w_up, w_down)


def _combine_kernel(ys_ref, h_ref, cm_ref, g_ref, o_ref):
    tm = h_ref.shape[0]
    cm = cm_ref[...]
    q = lax.broadcasted_iota(jnp.int32, (tm, ys_ref.shape[0]), 1)
    ys = ys_ref[...]
    sel_a = jnp.where(q == cm[:, 0:1].astype(jnp.int32), 1.0, 0.0).astype(BF16)
    sel_b = jnp.where(q == cm[:, 1:2].astype(jnp.int32), 1.0, 0.0).astype(BF16)
    h2 = h_ref[...] + cm[:, 2:3] * _dot(sel_a, ys) + cm[:, 3:4] * _dot(sel_b, ys)
    o_ref[...] = _rms_scale(h2, g_ref[...])


def _combine(ys, h1, cmeta, g):
    t, d = h1.shape
    tm = ROUTE_BLOCK
    return pl.pallas_call(
        _combine_kernel,
        grid=(t // tm,),
        in_specs=[pl.BlockSpec((LOCAL_ROWS, d), lambda i: (i, 0)),
                  pl.BlockSpec((tm, d), lambda i: (i, 0)),
                  pl.BlockSpec((tm, LANES), lambda i: (i, 0)),
                  pl.BlockSpec((1, d), lambda i: (0, 0))],
        out_specs=pl.BlockSpec((tm, d), lambda i: (i, 0)),
        out_shape=jax.ShapeDtypeStruct((t, d), F32),
        compiler_params=_params(1),
        name="combine",
    )(ys, h1, cmeta, g.reshape(1, d))


def _expert_plan(counts, n_blocks):
    cnt = counts[:, 0].astype(jnp.int32).reshape(n_blocks, N_EXPERTS)
    groups = (cnt + GROUP - 1) // GROUP
    first = jnp.cumsum(groups, axis=1) - groups
    upto = jnp.cumsum(groups, axis=0)
    per_expert = upto[-1]
    tiles_e = (per_expert + TILE_GROUPS - 1) // TILE_GROUPS
    tile_end = jnp.cumsum(tiles_e)
    n_tiles = tile_end[-1]
    max_groups = 2 * ROUTE_BLOCK * n_blocks // GROUP + n_blocks * N_EXPERTS
    r_max = max_groups // TILE_GROUPS + N_EXPERTS
    tile_ids = jnp.arange(r_max, dtype=jnp.int32)
    tile = jnp.minimum(tile_ids, n_tiles - 1)
    tile_e = jnp.sum((tile_end[None, :] <= tile[:, None]).astype(jnp.int32), axis=1)
    later = (tile_e[None, :] > tile_e[:, None]) & (tile_ids[None, :] < n_tiles)
    next_e = jnp.min(jnp.where(later, tile_e[None, :], N_EXPERTS), axis=1)
    next_e = jnp.where(next_e == N_EXPERTS, -1, next_e)

    slot = jnp.arange(r_max * TILE_GROUPS, dtype=jnp.int32)
    s_tile = slot // TILE_GROUPS
    oh_e = jnp.repeat(tile_e, TILE_GROUPS)[:, None] == jnp.arange(N_EXPERTS, dtype=jnp.int32)[None, :]

    def by_expert(table):
        return jnp.sum(jnp.where(oh_e[:, None, :], table[None], 0), axis=-1)

    k = slot - by_expert(((tile_end - tiles_e) * TILE_GROUPS)[None, :])[:, 0]
    real = (k < by_expert(per_expert[None, :])[:, 0]) & (s_tile < n_tiles)
    upto_e = by_expert(upto)
    blk = jnp.minimum(jnp.sum((upto_e <= k[:, None]).astype(jnp.int32), axis=1), n_blocks - 1)
    oh_b = blk[:, None] == jnp.arange(n_blocks, dtype=jnp.int32)[None, :]

    def by_block(table_se):
        return jnp.sum(jnp.where(oh_b, table_se, 0), axis=1)

    before = by_block(upto_e - by_expert(groups))
    src = blk * GROUPS_PER_BLOCK + by_block(by_expert(first)) + (k - before)
    zero_group = GROUPS_PER_BLOCK - 1
    spare = n_blocks * GROUPS_PER_BLOCK + (s_tile % 2) * TILE_GROUPS + slot % TILE_GROUPS
    gsrc = jnp.where(real, src, zero_group)
    gdst = jnp.where(real, src, spare)
    tail = jnp.concatenate([jnp.sum(groups, axis=1), jnp.zeros((1,), jnp.int32)])
    return tile_e, next_e, n_tiles.reshape(1), gsrc, gdst, tail


def _layer(h, norm_mix_g, w_in, conv_w, sgu_ln_g, sgu_ln_b, sgu_w_s, sgu_b_s, w_up_conv,
           w_up_sgu, w_out, norm_ffn_g, w_rg, b_rg, w_re, b_re, w_eg, w_eu, w_ed, out_g):
    t, d = h.shape
    conv_width = conv_w.shape[1]
    sgu_width = sgu_ln_g.shape[0]
    xn = _rmsnorm(h, norm_mix_g)
    ya = _conv_branch(xn, w_in, conv_w, conv_width)
    gz = _zproj(xn, w_in, 3 * conv_width, 2 * sgu_width)
    yb = _sgu_mix(gz, sgu_ln_g, sgu_ln_b, sgu_w_s, sgu_b_s)
    m, eg_bf, eu_bf, ed_bf = _upgate(xn, ya, yb, w_in, 3 * conv_width + 2 * sgu_width,
                                     w_up_conv, w_up_sgu, w_eg, w_eu, w_ed)
    h1 = _outproj(m, w_out, h)
    xs, meta, counts = _route_sort(h1, norm_ffn_g, w_rg, b_rg, w_re, b_re)
    ys = _experts(_expert_plan(counts, t // ROUTE_BLOCK), xs, eg_bf, eu_bf, ed_bf)
    cmeta = jnp.pad(meta[0:4].T, ((0, 0), (0, LANES - 4)))
    return _combine(ys, h1, cmeta, out_g)


def kernel(x, norm_mix_g, w_in, conv_w, sgu_ln_g, sgu_ln_b, sgu_w_s, sgu_b_s, w_up_conv, w_up_sgu, w_out, norm_ffn_g, w_router_group, b_router_group, w_router_expert, b_router_expert, w_exp_gate, w_exp_up, w_exp_down, norm_final_g):
    bsz, s, d = x.shape
    depth = w_in.shape[0]
    assert bsz == 1 and depth == 1, "causal conv carry and the fused final norm assume one sequence, one layer"
    assert s % ROUTE_BLOCK == 0
    out = _layer(x.reshape(s, d), norm_mix_g[0], w_in[0], conv_w[0], sgu_ln_g[0], sgu_ln_b[0],
                 sgu_w_s[0], sgu_b_s[0], w_up_conv[0], w_up_sgu[0], w_out[0], norm_ffn_g[0],
                 w_router_group[0], b_router_group[0], w_router_expert[0], b_router_expert[0],
                 w_exp_gate[0], w_exp_up[0], w_exp_down[0], norm_final_g)
    return out.reshape(bsz, s, d)
```

```python
import functools

import jax
import jax.numpy as jnp
from jax import lax
from jax.experimental import pallas as pl
from jax.experimental.pallas import tpu as pltpu

F32 = jnp.float32
BF16 = jnp.bfloat16

EPS = 1e-6
CHUNK = 64
CONV_K = 3
SGU_HEADS = 8
SGU_BLOCK = 128
N_GROUPS = 4
EXPERTS_PER_GROUP = 4
N_EXPERTS = N_GROUPS * EXPERTS_PER_GROUP
ROUTER_ROWS = 32
LANES = 128

VMEM_LIMIT_BYTES = 56 * 1024 * 1024

ROUTE_BLOCK = 512
GROUP = 16
LOCAL_ROWS = -(-(2 * ROUTE_BLOCK + N_EXPERTS * (GROUP - 1)) // 256) * 256
GROUPS_PER_BLOCK = LOCAL_ROWS // GROUP
MOE_TILE = 512
TILE_GROUPS = MOE_TILE // GROUP


def _params(n_axes):
    return pltpu.CompilerParams(
        dimension_semantics=("arbitrary",) * n_axes,
        vmem_limit_bytes=VMEM_LIMIT_BYTES)


def _dot(a, b):
    return jnp.dot(a, b, preferred_element_type=F32)


def _rms_scale(x, g):
    ms = jnp.mean(x * x, axis=-1, keepdims=True)
    return x * lax.rsqrt(ms + EPS) * g


def _rmsnorm_kernel(x_ref, g_ref, o_ref):
    o_ref[...] = _rms_scale(x_ref[...], g_ref[...]).astype(o_ref.dtype)


def _rmsnorm(x, g, tm=512):
    t, d = x.shape
    return pl.pallas_call(
        _rmsnorm_kernel,
        grid=(t // tm,),
        in_specs=[pl.BlockSpec((tm, d), lambda i: (i, 0)),
                  pl.BlockSpec((1, d), lambda i: (0, 0))],
        out_specs=pl.BlockSpec((tm, d), lambda i: (i, 0)),
        out_shape=jax.ShapeDtypeStruct((t, d), BF16),
        compiler_params=_params(1),
        name="rmsnorm",
    )(x, g.reshape(1, d))


def _conv_kernel(xn_ref, wb_ref, wc_ref, wh_ref, cw_ref, o_ref, wbf_ref, carry_ref):
    i = pl.program_id(1)
    tn = wb_ref.shape[1]
    tm = xn_ref.shape[0]

    @pl.when(i == 0)
    def _():
        wbf_ref[:, 0:tn] = wb_ref[...].astype(BF16)
        wbf_ref[:, tn:2 * tn] = wc_ref[...].astype(BF16)
        wbf_ref[:, 2 * tn:3 * tn] = wh_ref[...].astype(BF16)
        carry_ref[...] = jnp.zeros_like(carry_ref)

    proj = _dot(xn_ref[...], wbf_ref[...])
    b = proj[:, 0:tn]
    p = proj[:, tn:2 * tn] * proj[:, 2 * tn:3 * tn]
    prev = carry_ref[...]
    carry_ref[...] = p[tm - 8:tm, :]
    row = lax.broadcasted_iota(jnp.int32, p.shape, 0)
    p1 = jnp.where(row == 0, prev[7:8, :], pltpu.roll(p, 1, axis=0))
    p2 = jnp.where(row == 0, prev[6:7, :],
                   jnp.where(row == 1, prev[7:8, :], pltpu.roll(p, 2, axis=0)))
    cw = cw_ref[...]
    y = b * (cw[0:1, :] * p2 + cw[1:2, :] * p1 + cw[2:3, :] * p)
    o_ref[...] = y.astype(o_ref.dtype)


def _conv_branch(xn, w_in, conv_w, width, tm=1024, tn=256):
    t, d = xn.shape
    nj = width // tn
    return pl.pallas_call(
        _conv_kernel,
        grid=(nj, t // tm),
        in_specs=[pl.BlockSpec((tm, d), lambda j, i: (i, 0)),
                  pl.BlockSpec((d, tn), lambda j, i: (0, j)),
                  pl.BlockSpec((d, tn), lambda j, i: (0, nj + j)),
                  pl.BlockSpec((d, tn), lambda j, i: (0, 2 * nj + j)),
                  pl.BlockSpec((CONV_K, tn), lambda j, i: (0, j))],
        out_specs=pl.BlockSpec((tm, tn), lambda j, i: (i, j)),
        out_shape=jax.ShapeDtypeStruct((t, width), BF16),
        scratch_shapes=[pltpu.VMEM((d, 3 * tn), BF16),
                        pltpu.VMEM((8, tn), F32)],
        compiler_params=_params(2),
        name="conv_branch",
    )(xn, w_in, w_in, w_in, conv_w)


def _zproj_kernel(xn_ref, w_ref, o_ref, wbf_ref):
    @pl.when(pl.program_id(1) == 0)
    def _():
        wbf_ref[...] = w_ref[...].astype(BF16)

    z = _dot(xn_ref[...], wbf_ref[...])
    o_ref[...] = jax.nn.gelu(z).astype(o_ref.dtype)


def _zproj(xn, w_in, col0, width, tm=1024, tn=512):
    t, d = xn.shape
    c0 = col0 // tn
    return pl.pallas_call(
        _zproj_kernel,
        grid=(width // tn, t // tm),
        in_specs=[pl.BlockSpec((tm, d), lambda j, i: (i, 0)),
                  pl.BlockSpec((d, tn), lambda j, i: (0, c0 + j))],
        out_specs=pl.BlockSpec((tm, tn), lambda j, i: (i, j)),
        out_shape=jax.ShapeDtypeStruct((t, width), BF16),
        scratch_shapes=[pltpu.VMEM((d, tn), BF16)],
        compiler_params=_params(2),
        name="sgu_zproj",
    )(xn, w_in)


def _sgu_kernel(gz_ref, lng_ref, lnb_ref, ws_ref, bsx_ref, o_ref):
    tm = gz_ref.shape[0]
    w = o_ref.shape[1]
    hd = w // SGU_HEADS
    v = gz_ref[:, w:2 * w].astype(F32)
    mu = jnp.mean(v, axis=-1, keepdims=True)
    vc = v - mu
    var = jnp.mean(vc * vc, axis=-1, keepdims=True)
    vn = (vc * lax.rsqrt(var + EPS) * lng_ref[...] + lnb_ref[...]).astype(BF16)
    ii = lax.broadcasted_iota(jnp.int32, (SGU_BLOCK, SGU_BLOCK), 0)
    jj = lax.broadcasted_iota(jnp.int32, (SGU_BLOCK, SGU_BLOCK), 1)
    mask = (jj // CHUNK) <= (ii // CHUNK)
    for h in range(SGU_HEADS):
        wm = jnp.where(mask, ws_ref[h], 0.0).astype(BF16)
        cs = slice(h * hd, (h + 1) * hd)
        for n in range(tm // SGU_BLOCK):
            rs = slice(n * SGU_BLOCK, (n + 1) * SGU_BLOCK)
            vm = _dot(wm, vn[rs, cs]) + bsx_ref[:, cs]
            o_ref[rs, cs] = (gz_ref[rs, cs].astype(F32) * vm).astype(o_ref.dtype)


def _sgu_mix(gz, ln_g, ln_b, w_s, b_s, tm=512):
    t, w2 = gz.shape
    w = w2 // 2
    hd = w // SGU_HEADS
    bsx = jnp.repeat(b_s.T, hd, axis=1)
    return pl.pallas_call(
        _sgu_kernel,
        grid=(t // tm,),
        in_specs=[pl.BlockSpec((tm, w2), lambda i: (i, 0)),
                  pl.BlockSpec((1, w), lambda i: (0, 0)),
                  pl.BlockSpec((1, w), lambda i: (0, 0)),
                  pl.BlockSpec((SGU_HEADS, SGU_BLOCK, SGU_BLOCK), lambda i: (0, 0, 0)),
                  pl.BlockSpec((SGU_BLOCK, w), lambda i: (0, 0))],
        out_specs=pl.BlockSpec((tm, w), lambda i: (i, 0)),
        out_shape=jax.ShapeDtypeStruct((t, w), BF16),
        compiler_params=_params(1),
        name="sgu_mix",
    )(gz, ln_g.reshape(1, w), ln_b.reshape(1, w), w_s, bsx)


def _upgate_kernel(xn_ref, ya_ref, yb_ref, wgc_ref, wgs_ref, wua_ref, wub_ref, eg_ref, eu_ref, ed_ref,
                   o_ref, eg_bf_ref, eu_bf_ref, ed_bf_ref, wgbf_ref, wuabf_ref, wubbf_ref):
    tn = wgc_ref.shape[1]
    eg_bf_ref[...] = eg_ref[...].astype(BF16)
    eu_bf_ref[...] = eu_ref[...].astype(BF16)
    ed_bf_ref[...] = ed_ref[...].astype(BF16)

    @pl.when(pl.program_id(1) == 0)
    def _():
        wgbf_ref[:, 0:tn] = wgc_ref[...].astype(BF16)
        wgbf_ref[:, tn:2 * tn] = wgs_ref[...].astype(BF16)
        wuabf_ref[...] = wua_ref[...].astype(BF16)
        wubbf_ref[...] = wub_ref[...].astype(BF16)

    gl = _dot(xn_ref[...], wgbf_ref[...])
    a = _dot(ya_ref[...], wuabf_ref[...])
    b = _dot(yb_ref[...], wubbf_ref[...])
    m = jax.nn.sigmoid(gl[:, 0:tn]) * a + jax.nn.sigmoid(gl[:, tn:2 * tn]) * b
    o_ref[...] = m.astype(o_ref.dtype)


def _upgate(xn, ya, yb, w_in, gate_col0, w_up_a, w_up_b, w_eg, w_eu, w_ed, tm=512, tn=512):
    t, d = xn.shape
    wa = ya.shape[1]
    wb = yb.shape[1]
    dout = w_up_a.shape[1]
    c0 = gate_col0 // tn
    nj = dout // tn
    ni = t // tm
    n_e, d_e, f_e = w_eg.shape
    up_rows = n_e * d_e // (nj * ni)
    down_rows = n_e * f_e // (nj * ni)
    assert up_rows * nj * ni == n_e * d_e and up_rows % 16 == 0
    assert down_rows * nj * ni == n_e * f_e and down_rows % 16 == 0
    up_spec = pl.BlockSpec((up_rows, f_e), lambda j, i: (j * ni + i, 0))
    down_spec = pl.BlockSpec((down_rows, d_e), lambda j, i: (j * ni + i, 0))
    m, eg_bf, eu_bf, ed_bf = pl.pallas_call(
        _upgate_kernel,
        grid=(nj, ni),
        in_specs=[pl.BlockSpec((tm, d), lambda j, i: (i, 0)),
                  pl.BlockSpec((tm, wa), lambda j, i: (i, 0)),
                  pl.BlockSpec((tm, wb), lambda j, i: (i, 0)),
                  pl.BlockSpec((d, tn), lambda j, i: (0, c0 + j)),
                  pl.BlockSpec((d, tn), lambda j, i: (0, c0 + nj + j)),
                  pl.BlockSpec((wa, tn), lambda j, i: (0, j)),
                  pl.BlockSpec((wb, tn), lambda j, i: (0, j)),
                  up_spec, up_spec, down_spec],
        out_specs=[pl.BlockSpec((tm, tn), lambda j, i: (i, j)), up_spec, up_spec, down_spec],
        out_shape=[jax.ShapeDtypeStruct((t, dout), BF16),
                   jax.ShapeDtypeStruct((n_e * d_e, f_e), BF16),
                   jax.ShapeDtypeStruct((n_e * d_e, f_e), BF16),
                   jax.ShapeDtypeStruct((n_e * f_e, d_e), BF16)],
        scratch_shapes=[pltpu.VMEM((d, 2 * tn), BF16),
                        pltpu.VMEM((wa, tn), BF16),
                        pltpu.VMEM((wb, tn), BF16)],
        compiler_params=_params(2),
        name="upgate",
    )(xn, ya, yb, w_in, w_in, w_up_a, w_up_b,
      w_eg.reshape(n_e * d_e, f_e), w_eu.reshape(n_e * d_e, f_e), w_ed.reshape(n_e * f_e, d_e))
    return (m, eg_bf.reshape(n_e, d_e, f_e), eu_bf.reshape(n_e, d_e, f_e),
            ed_bf.reshape(n_e, f_e, d_e))


def _outproj_kernel(m_ref, w_ref, x_ref, o_ref, wbf_ref):
    @pl.when(pl.program_id(1) == 0)
    def _():
        wbf_ref[...] = w_ref[...].astype(BF16)

    o_ref[...] = x_ref[...] + _dot(m_ref[...], wbf_ref[...])


def _outproj(m, w_out, x, tm=512, tn=1024):
    t, d = m.shape
    dout = w_out.shape[1]
    return pl.pallas_call(
        _outproj_kernel,
        grid=(dout // tn, t // tm),
        in_specs=[pl.BlockSpec((tm, d), lambda j, i: (i, 0)),
                  pl.BlockSpec((d, tn), lambda j, i: (0, j)),
                  pl.BlockSpec((tm, tn), lambda j, i: (i, j))],
        out_specs=pl.BlockSpec((tm, tn), lambda j, i: (i, j)),
        out_shape=jax.ShapeDtypeStruct((t, dout), F32),
        scratch_shapes=[pltpu.VMEM((d, tn), BF16)],
        compiler_params=_params(2),
        name="outproj",
    )(m, w_out, x)


def _argmax_rows(rows):
    best = rows[0]
    idx = jnp.zeros(rows[0].shape, jnp.int32)
    for k in range(1, len(rows)):
        better = rows[k] > best
        best = jnp.where(better, rows[k], best)
        idx = jnp.where(better, k, idx)
    return best, idx


def _softmax_rows(rows):
    mx = functools.reduce(jnp.maximum, rows)
    ex = [jnp.exp(r - mx) for r in rows]
    den = functools.reduce(lambda a, b: a + b, ex)
    return [e / den for e in ex]


def _route_sort_kernel(h_ref, g_ref, wr_ref, br_ref, xs_ref, meta_ref, cnt_ref):
    tm = h_ref.shape[0]
    xn = _rms_scale(h_ref[...], g_ref[...])
    lt = lax.dot_general(wr_ref[...], xn, (((1,), (1,)), ((), ())),
                         precision=lax.Precision.HIGHEST,
                         preferred_element_type=F32) + br_ref[...]
    pgs = _softmax_rows([lt[k:k + 1, :] for k in range(N_GROUPS)])
    pg, gi = _argmax_rows(pgs)
    sel = []
    for k in range(EXPERTS_PER_GROUP):
        r = jnp.zeros_like(pg)
        for g in range(N_GROUPS):
            row = N_GROUPS + g * EXPERTS_PER_GROUP + k
            r = jnp.where(gi == g, lt[row:row + 1, :], r)
        sel.append(r)
    pes = _softmax_rows(sel)
    p1, e1 = _argmax_rows(pes)
    rest = [jnp.where(e1 == k, -1.0, pes[k]) for k in range(EXPERTS_PER_GROUP)]
    p2, e2 = _argmax_rows(rest)
    den = p1 + p2
    w1 = pg * (p1 / den)
    w2 = pg * (p2 / den)
    lo = jnp.minimum(e1, e2)
    hi = jnp.maximum(e1, e2)
    w_lo = jnp.where(e1 < e2, w1, w2)
    w_hi = jnp.where(e1 < e2, w2, w1)
    ea = gi * EXPERTS_PER_GROUP + lo
    eb = gi * EXPERTS_PER_GROUP + hi

    erow = lax.broadcasted_iota(jnp.int32, (N_EXPERTS, tm), 0)
    oh_a = (erow == ea).astype(F32)
    oh_b = (erow == eb).astype(F32)
    a = lax.broadcasted_iota(jnp.int32, (tm, tm), 0)
    b = lax.broadcasted_iota(jnp.int32, (tm, tm), 1)
    before = (a < b).astype(BF16)
    cum = _dot((oh_a + oh_b).astype(BF16), before)
    cnt = jnp.sum(oh_a + oh_b, axis=1, keepdims=True)
    padded = jnp.floor((cnt + (GROUP - 1)) * (1.0 / GROUP)) * GROUP
    pos_a = jnp.sum(oh_a * cum + jnp.where(erow < ea, padded, 0.0), axis=0, keepdims=True)
    pos_b = jnp.sum(oh_b * cum + jnp.where(erow < eb, padded, 0.0), axis=0, keepdims=True)

    d = h_ref.shape[1]
    q = lax.broadcasted_iota(jnp.int32, (xs_ref.shape[0], tm), 0)
    perm_a = jnp.where(q == pos_a.astype(jnp.int32), 1.0, 0.0).astype(BF16)
    perm_b = jnp.where(q == pos_b.astype(jnp.int32), 1.0, 0.0).astype(BF16)
    xs_ref[:, 0:d] = _dot(perm_a + perm_b, xn.astype(BF16)).astype(xs_ref.dtype)

    def gate_rows(w):
        hi = w.astype(BF16).astype(F32)
        lo = w - hi
        k = lax.broadcasted_iota(jnp.int32, (LANES, tm), 0)
        return jnp.where(k == 0, hi, jnp.where(k == 1, lo, 0.0)).astype(BF16)

    nt_dims = (((1,), (1,)), ((), ()))
    gates = (lax.dot_general(perm_a, gate_rows(w_lo), nt_dims, preferred_element_type=F32)
             + lax.dot_general(perm_b, gate_rows(w_hi), nt_dims, preferred_element_type=F32))
    xs_ref[:, d:d + LANES] = gates.astype(xs_ref.dtype)

    cnt_ref[...] = jnp.broadcast_to(cnt, cnt_ref.shape)
    meta_ref[0:1, :] = pos_a
    meta_ref[1:2, :] = pos_b
    meta_ref[2:8, :] = jnp.zeros((6, tm), F32)


def _route_sort(h1, g, w_rg, b_rg, w_re, b_re):
    t, d = h1.shape
    tm = ROUTE_BLOCK
    nb = t // tm
    n_log = w_rg.shape[1] + w_re.shape[1]
    wr = jnp.concatenate([w_rg, w_re], axis=1).T
    wr = jnp.pad(wr, ((0, ROUTER_ROWS - n_log), (0, 0)))
    br = jnp.pad(jnp.concatenate([b_rg, b_re]), (0, ROUTER_ROWS - n_log)).reshape(ROUTER_ROWS, 1)
    return pl.pallas_call(
        _route_sort_kernel,
        grid=(nb,),
        in_specs=[pl.BlockSpec((tm, d), lambda i: (i, 0)),
                  pl.BlockSpec((1, d), lambda i: (0, 0)),
                  pl.BlockSpec((ROUTER_ROWS, d), lambda i: (0, 0)),
                  pl.BlockSpec((ROUTER_ROWS, 1), lambda i: (0, 0))],
        out_specs=[pl.BlockSpec((LOCAL_ROWS, d + LANES), lambda i: (i, 0)),
                   pl.BlockSpec((8, tm), lambda i: (0, i)),
                   pl.BlockSpec((N_EXPERTS, LANES), lambda i: (i, 0))],
        out_shape=[jax.ShapeDtypeStruct((nb * LOCAL_ROWS, d + LANES), BF16),
                   jax.ShapeDtypeStruct((8, t), F32),
                   jax.ShapeDtypeStruct((nb * N_EXPERTS, LANES), F32)],
        compiler_params=_params(1),
        name="route_sort",
    )(h1, g.reshape(1, d), wr, br)


def _group_copy(src_ref, s_group, dst_ref, d_group, sem):
    return pltpu.make_async_copy(src_ref.at[pl.ds(pl.multiple_of(s_group * GROUP, GROUP), GROUP), :],
                                 dst_ref.at[pl.ds(pl.multiple_of(d_group * GROUP, GROUP), GROUP), :], sem)


def _expert_kernel(te_ref, nxt_ref, nt_ref, gsrc_ref, gdst_ref, tail_ref,
                   xs_ref, wg_ref, wu_ref, wd_ref, ys_ref,
                   xbuf_ref, ybuf_ref, zero_ref, wg_buf_ref, wu_buf_ref, wd_buf_ref,
                   cur_ref, gsem, ssem, zsem, wsem):
    r = pl.program_id(0)
    nt = nt_ref[0]

    def gather(q, slot):
        for i in range(TILE_GROUPS):
            _group_copy(xs_ref, gsrc_ref[q * TILE_GROUPS + i], xbuf_ref, slot * TILE_GROUPS + i,
                        gsem.at[slot]).start(priority=1)

    def weight_copies(e, wslot):
        return (pltpu.make_async_copy(wg_ref.at[e], wg_buf_ref.at[wslot], wsem.at[wslot]),
                pltpu.make_async_copy(wu_ref.at[e], wu_buf_ref.at[wslot], wsem.at[wslot]),
                pltpu.make_async_copy(wd_ref.at[e], wd_buf_ref.at[wslot], wsem.at[wslot]))

    def tile_wait(src, dst, sem):
        pltpu.make_async_copy(src.at[pl.ds(0, MOE_TILE), :], dst.at[pl.ds(0, MOE_TILE), :], sem).wait()

    @pl.when(r == 0)
    def _():
        cur_ref[0] = -1
        cur_ref[1] = -1
        for cp in weight_copies(te_ref[0], 0):
            cp.start()
        zero_ref[...] = jnp.zeros_like(zero_ref)
        n_blocks = tail_ref.shape[0]

        def fill(make):
            def body(g, c):
                make(g)
                return c
            return body

        for blk in range(n_blocks):
            lax.fori_loop(tail_ref[blk], GROUPS_PER_BLOCK, fill(
                lambda g, blk=blk: _group_copy(zero_ref, 0, ys_ref, blk * GROUPS_PER_BLOCK + g, zsem).start()), 0)
        for blk in range(n_blocks):
            lax.fori_loop(tail_ref[blk], GROUPS_PER_BLOCK, fill(
                lambda g, blk=blk: _group_copy(zero_ref, 0, ys_ref, blk * GROUPS_PER_BLOCK + g, zsem).wait()), 0)
        gather(0, 0)

    @pl.when(r < nt)
    def _():
        e = te_ref[r]
        slot = r % 2

        gather(jnp.minimum(r + 1, nt - 1), 1 - slot)

        @pl.when(cur_ref[0] != e)
        def _():
            wslot = (cur_ref[1] + 1) % 2
            for cp in weight_copies(e, wslot):
                cp.wait()
            cur_ref[0] = e
            cur_ref[1] = cur_ref[1] + 1

            @pl.when(nxt_ref[r] >= 0)
            def _():
                for cp in weight_copies(nxt_ref[r], 1 - wslot):
                    cp.start()

        wslot = cur_ref[1] % 2
        tile_wait(xs_ref, xbuf_ref, gsem.at[slot])
        row0 = pl.multiple_of(slot * MOE_TILE, MOE_TILE)
        d = ybuf_ref.shape[1]
        x = xbuf_ref[pl.ds(row0, MOE_TILE), 0:d]
        gate_parts = xbuf_ref[pl.ds(row0, MOE_TILE), d:d + LANES].astype(F32)
        gate = gate_parts[:, 0:1] + gate_parts[:, 1:2]
        hid = (jax.nn.silu(_dot(x, wg_buf_ref[wslot])) * _dot(x, wu_buf_ref[wslot])).astype(BF16)
        y = (_dot(hid, wd_buf_ref[wslot]) * gate).astype(ybuf_ref.dtype)

        @pl.when(r >= 2)
        def _():
            tile_wait(ybuf_ref, ys_ref, ssem.at[slot])

        ybuf_ref[pl.ds(row0, MOE_TILE), :] = y
        for i in range(TILE_GROUPS):
            _group_copy(ybuf_ref, slot * TILE_GROUPS + i, ys_ref, gdst_ref[r * TILE_GROUPS + i],
                        ssem.at[slot]).start(priority=1)

        @pl.when(r == nt - 1)
        def _():
            tile_wait(xs_ref, xbuf_ref, gsem.at[1 - slot])
            tile_wait(ybuf_ref, ys_ref, ssem.at[slot])

            @pl.when(r >= 1)
            def _():
                tile_wait(ybuf_ref, ys_ref, ssem.at[1 - slot])


def _experts(plan, xs, w_gate, w_up, w_down):
    d = w_gate.shape[1]
    f = w_gate.shape[2]
    tile_e, next_e, n_tiles, gsrc, gdst, tail = plan
    r_max = tile_e.shape[0]
    n_blocks = tail.shape[0]
    any_spec = pl.BlockSpec(memory_space=pl.ANY)
    grid_spec = pltpu.PrefetchScalarGridSpec(
        num_scalar_prefetch=6,
        grid=(r_max,),
        in_specs=[any_spec, any_spec, any_spec, any_spec],
        out_specs=any_spec,
        scratch_shapes=[pltpu.VMEM((2 * MOE_TILE, d + LANES), BF16),
                        pltpu.VMEM((2 * MOE_TILE, d), BF16),
                        pltpu.VMEM((GROUP, d), BF16),
                        pltpu.VMEM((2, d, f), BF16),
                        pltpu.VMEM((2, d, f), BF16),
                        pltpu.VMEM((2, f, d), BF16),
                        pltpu.SMEM((2,), jnp.int32),
                        pltpu.SemaphoreType.DMA((2,)),
                        pltpu.SemaphoreType.DMA((2,)),
                        pltpu.SemaphoreType.DMA(()),
                        pltpu.SemaphoreType.DMA((2,))],
    )
    return pl.pallas_call(
        _expert_kernel,
        grid_spec=grid_spec,
        out_shape=jax.ShapeDtypeStruct((n_blocks * LOCAL_ROWS, d), BF16),
        compiler_params=_params(1),
        name="experts",
    )(tile_e, next_e, n_tiles, gsrc, gdst, tail, xs, w_gate, w_up, w_down)


def _combine_kernel(ys_ref, h_ref, cm_ref, g_ref, o_ref):
    tm = h_ref.shape[0]
    cm = cm_ref[...]
    q = lax.broadcasted_iota(jnp.int32, (tm, ys_ref.shape[0]), 1)
    sel = jnp.where((q == cm[:, 0:1].astype(jnp.int32)) | (q == cm[:, 1:2].astype(jnp.int32)), 1.0, 0.0)
    h2 = h_ref[...] + _dot(sel.astype(BF16), ys_ref[...])
    o_ref[...] = _rms_scale(h2, g_ref[...])


def _combine(ys, h1, cmeta, g):
    t, d = h1.shape
    tm = ROUTE_BLOCK
    return pl.pallas_call(
        _combine_kernel,
        grid=(t // tm,),
        in_specs=[pl.BlockSpec((LOCAL_ROWS, d), lambda i: (i, 0)),
                  pl.BlockSpec((tm, d), lambda i: (i, 0)),
                  pl.BlockSpec((tm, LANES), lambda i: (i, 0)),
                  pl.BlockSpec((1, d), lambda i: (0, 0))],
        out_specs=pl.BlockSpec((tm, d), lambda i: (i, 0)),
        out_shape=jax.ShapeDtypeStruct((t, d), F32),
        compiler_params=_params(1),
        name="combine",
    )(ys, h1, cmeta, g.reshape(1, d))


def _expert_plan(counts, n_blocks):
    cnt = counts[:, 0].astype(jnp.int32).reshape(n_blocks, N_EXPERTS)
    groups = (cnt + GROUP - 1) // GROUP
    first = jnp.cumsum(groups, axis=1) - groups
    upto = jnp.cumsum(groups, axis=0)
    per_expert = upto[-1]
    tiles_e = (per_expert + TILE_GROUPS - 1) // TILE_GROUPS
    tile_end = jnp.cumsum(tiles_e)
    n_tiles = tile_end[-1]
    max_groups = 2 * ROUTE_BLOCK * n_blocks // GROUP + n_blocks * N_EXPERTS
    r_max = max_groups // TILE_GROUPS + N_EXPERTS
    tile_ids = jnp.arange(r_max, dtype=jnp.int32)
    tile = jnp.minimum(tile_ids, n_tiles - 1)
    tile_e = jnp.sum((tile_end[None, :] <= tile[:, None]).astype(jnp.int32), axis=1)
    later = (tile_e[None, :] > tile_e[:, None]) & (tile_ids[None, :] < n_tiles)
    next_e = jnp.min(jnp.where(later, tile_e[None, :], N_EXPERTS), axis=1)
    next_e = jnp.where(next_e == N_EXPERTS, -1, next_e)

    slot = jnp.arange(r_max * TILE_GROUPS, dtype=jnp.int32)
    s_tile = slot // TILE_GROUPS
    oh_e = jnp.repeat(tile_e, TILE_GROUPS)[:, None] == jnp.arange(N_EXPERTS, dtype=jnp.int32)[None, :]

    def by_expert(table):
        return jnp.sum(jnp.where(oh_e[:, None, :], table[None], 0), axis=-1)

    k = slot - by_expert(((tile_end - tiles_e) * TILE_GROUPS)[None, :])[:, 0]
    real = (k < by_expert(per_expert[None, :])[:, 0]) & (s_tile < n_tiles)
    upto_e = by_expert(upto)
    blk = jnp.minimum(jnp.sum((upto_e <= k[:, None]).astype(jnp.int32), axis=1), n_blocks - 1)
    oh_b = blk[:, None] == jnp.arange(n_blocks, dtype=jnp.int32)[None, :]

    def by_block(table_se):
        return jnp.sum(jnp.where(oh_b, table_se, 0), axis=1)

    before = by_block(upto_e - by_expert(groups))
    src = blk * GROUPS_PER_BLOCK + by_block(by_expert(first)) + (k - before)
    zero_group = GROUPS_PER_BLOCK - 1
    spare = n_blocks * GROUPS_PER_BLOCK + (s_tile % 2) * TILE_GROUPS + slot % TILE_GROUPS
    gsrc = jnp.where(real, src, zero_group)
    gdst = jnp.where(real, src, spare)
    tail = jnp.concatenate([jnp.sum(groups, axis=1), jnp.zeros((1,), jnp.int32)])
    return tile_e, next_e, n_tiles.reshape(1), gsrc, gdst, tail


def _layer(h, norm_mix_g, w_in, conv_w, sgu_ln_g, sgu_ln_b, sgu_w_s, sgu_b_s, w_up_conv,
           w_up_sgu, w_out, norm_ffn_g, w_rg, b_rg, w_re, b_re, w_eg, w_eu, w_ed, out_g):
    t, d = h.shape
    conv_width = conv_w.shape[1]
    sgu_width = sgu_ln_g.shape[0]
    xn = _rmsnorm(h, norm_mix_g)
    ya = _conv_branch(xn, w_in, conv_w, conv_width)
    gz = _zproj(xn, w_in, 3 * conv_width, 2 * sgu_width)
    yb = _sgu_mix(gz, sgu_ln_g, sgu_ln_b, sgu_w_s, sgu_b_s)
    m, eg_bf, eu_bf, ed_bf = _upgate(xn, ya, yb, w_in, 3 * conv_width + 2 * sgu_width,
                                     w_up_conv, w_up_sgu, w_eg, w_eu, w_ed)
    h1 = _outproj(m, w_out, h)
    xs, meta, counts = _route_sort(h1, norm_ffn_g, w_rg, b_rg, w_re, b_re)
    ys = _experts(_expert_plan(counts, t // ROUTE_BLOCK), xs, eg_bf, eu_bf, ed_bf)
    cmeta = jnp.pad(meta[0:2].T, ((0, 0), (0, LANES - 2)))
    return _combine(ys, h1, cmeta, out_g)


def kernel(x, norm_mix_g, w_in, conv_w, sgu_ln_g, sgu_ln_b, sgu_w_s, sgu_b_s, w_up_conv, w_up_sgu, w_out, norm_ffn_g, w_router_group, b_router_group, w_router_expert, b_router_expert, w_exp_gate, w_exp_up, w_exp_down, norm_final_g):
    bsz, s, d = x.shape
    depth = w_in.shape[0]
    assert bsz == 1 and depth == 1, "causal conv carry and the fused final norm assume one sequence, one layer"
    assert s % ROUTE_BLOCK == 0
    out = _layer(x.reshape(s, d), norm_mix_g[0], w_in[0], conv_w[0], sgu_ln_g[0], sgu_ln_b[0],
                 sgu_w_s[0], sgu_b_s[0], w_up_conv[0], w_up_sgu[0], w_out[0], norm_ffn_g[0],
                 w_router_group[0], b_router_group[0], w_router_expert[0], b_router_expert[0],
                 w_exp_gate[0], w_exp_up[0], w_exp_down[0], norm_final_g)
    return out.reshape(bsz, s, d)
```

```python
import functools

import jax
import jax.numpy as jnp
from jax import lax
from jax.experimental import pallas as pl
from jax.experimental.pallas import tpu as pltpu

F32 = jnp.float32
BF16 = jnp.bfloat16

EPS = 1e-6
CHUNK = 64
CONV_K = 3
SGU_HEADS = 8
SGU_BLOCK = 128
N_GROUPS = 4
EXPERTS_PER_GROUP = 4
N_EXPERTS = N_GROUPS * EXPERTS_PER_GROUP
ROUTER_ROWS = 32
LANES = 128

VMEM_LIMIT_BYTES = 56 * 1024 * 1024

ROUTE_BLOCK = 512
GROUP = 16
LOCAL_ROWS = -(-(2 * ROUTE_BLOCK + N_EXPERTS * (GROUP - 1)) // 256) * 256
GROUPS_PER_BLOCK = LOCAL_ROWS // GROUP
MOE_TILE = 512
TILE_GROUPS = MOE_TILE // GROUP
MXU_ROWS = 256


def _params(n_axes):
    return pltpu.CompilerParams(
        dimension_semantics=("arbitrary",) * n_axes,
        vmem_limit_bytes=VMEM_LIMIT_BYTES)


def _dot(a, b):
    return jnp.dot(a, b, preferred_element_type=F32)


def _rms_scale(x, g):
    ms = jnp.mean(x * x, axis=-1, keepdims=True)
    return x * lax.rsqrt(ms + EPS) * g


def _rmsnorm_kernel(x_ref, g_ref, o_ref):
    o_ref[...] = _rms_scale(x_ref[...], g_ref[...]).astype(o_ref.dtype)


def _rmsnorm(x, g, tm=512):
    t, d = x.shape
    return pl.pallas_call(
        _rmsnorm_kernel,
        grid=(t // tm,),
        in_specs=[pl.BlockSpec((tm, d), lambda i: (i, 0)),
                  pl.BlockSpec((1, d), lambda i: (0, 0))],
        out_specs=pl.BlockSpec((tm, d), lambda i: (i, 0)),
        out_shape=jax.ShapeDtypeStruct((t, d), BF16),
        compiler_params=_params(1),
        name="rmsnorm",
    )(x, g.reshape(1, d))


def _conv_kernel(xn_ref, wb_ref, wc_ref, wh_ref, cw_ref, o_ref, wbf_ref, carry_ref):
    i = pl.program_id(1)
    tn = wb_ref.shape[1]
    tm = xn_ref.shape[0]

    @pl.when(i == 0)
    def _():
        wbf_ref[:, 0:tn] = wb_ref[...].astype(BF16)
        wbf_ref[:, tn:2 * tn] = wc_ref[...].astype(BF16)
        wbf_ref[:, 2 * tn:3 * tn] = wh_ref[...].astype(BF16)
        carry_ref[...] = jnp.zeros_like(carry_ref)

    proj = _dot(xn_ref[...], wbf_ref[...])
    b = proj[:, 0:tn]
    p = proj[:, tn:2 * tn] * proj[:, 2 * tn:3 * tn]
    prev = carry_ref[...]
    carry_ref[...] = p[tm - 8:tm, :]
    row = lax.broadcasted_iota(jnp.int32, p.shape, 0)
    p1 = jnp.where(row == 0, prev[7:8, :], pltpu.roll(p, 1, axis=0))
    p2 = jnp.where(row == 0, prev[6:7, :],
                   jnp.where(row == 1, prev[7:8, :], pltpu.roll(p, 2, axis=0)))
    cw = cw_ref[...]
    y = b * (cw[0:1, :] * p2 + cw[1:2, :] * p1 + cw[2:3, :] * p)
    o_ref[...] = y.astype(o_ref.dtype)


def _conv_branch(xn, w_in, conv_w, width, tm=1024, tn=256):
    t, d = xn.shape
    nj = width // tn
    return pl.pallas_call(
        _conv_kernel,
        grid=(nj, t // tm),
        in_specs=[pl.BlockSpec((tm, d), lambda j, i: (i, 0)),
                  pl.BlockSpec((d, tn), lambda j, i: (0, j)),
                  pl.BlockSpec((d, tn), lambda j, i: (0, nj + j)),
                  pl.BlockSpec((d, tn), lambda j, i: (0, 2 * nj + j)),
                  pl.BlockSpec((CONV_K, tn), lambda j, i: (0, j))],
        out_specs=pl.BlockSpec((tm, tn), lambda j, i: (i, j)),
        out_shape=jax.ShapeDtypeStruct((t, width), BF16),
        scratch_shapes=[pltpu.VMEM((d, 3 * tn), BF16),
                        pltpu.VMEM((8, tn), F32)],
        compiler_params=_params(2),
        name="conv_branch",
    )(xn, w_in, w_in, w_in, conv_w)


def _zproj_kernel(xn_ref, w_ref, o_ref, wbf_ref):
    @pl.when(pl.program_id(1) == 0)
    def _():
        wbf_ref[...] = w_ref[...].astype(BF16)

    z = _dot(xn_ref[...], wbf_ref[...])
    o_ref[...] = jax.nn.gelu(z).astype(o_ref.dtype)


def _zproj(xn, w_in, col0, width, tm=1024, tn=512):
    t, d = xn.shape
    c0 = col0 // tn
    return pl.pallas_call(
        _zproj_kernel,
        grid=(width // tn, t // tm),
        in_specs=[pl.BlockSpec((tm, d), lambda j, i: (i, 0)),
                  pl.BlockSpec((d, tn), lambda j, i: (0, c0 + j))],
        out_specs=pl.BlockSpec((tm, tn), lambda j, i: (i, j)),
        out_shape=jax.ShapeDtypeStruct((t, width), BF16),
        scratch_shapes=[pltpu.VMEM((d, tn), BF16)],
        compiler_params=_params(2),
        name="sgu_zproj",
    )(xn, w_in)


def _sgu_kernel(gz_ref, lng_ref, lnb_ref, ws_ref, bsx_ref, o_ref):
    tm = gz_ref.shape[0]
    w = o_ref.shape[1]
    hd = w // SGU_HEADS
    v = gz_ref[:, w:2 * w].astype(F32)
    mu = jnp.mean(v, axis=-1, keepdims=True)
    vc = v - mu
    var = jnp.mean(vc * vc, axis=-1, keepdims=True)
    vn = (vc * lax.rsqrt(var + EPS) * lng_ref[...] + lnb_ref[...]).astype(BF16)
    ii = lax.broadcasted_iota(jnp.int32, (SGU_BLOCK, SGU_BLOCK), 0)
    jj = lax.broadcasted_iota(jnp.int32, (SGU_BLOCK, SGU_BLOCK), 1)
    mask = (jj // CHUNK) <= (ii // CHUNK)
    for h in range(SGU_HEADS):
        wm = jnp.where(mask, ws_ref[h], 0.0).astype(BF16)
        cs = slice(h * hd, (h + 1) * hd)
        for n in range(tm // SGU_BLOCK):
            rs = slice(n * SGU_BLOCK, (n + 1) * SGU_BLOCK)
            vm = _dot(wm, vn[rs, cs]) + bsx_ref[:, cs]
            o_ref[rs, cs] = (gz_ref[rs, cs].astype(F32) * vm).astype(o_ref.dtype)


def _sgu_mix(gz, ln_g, ln_b, w_s, b_s, tm=512):
    t, w2 = gz.shape
    w = w2 // 2
    hd = w // SGU_HEADS
    bsx = jnp.repeat(b_s.T, hd, axis=1)
    return pl.pallas_call(
        _sgu_kernel,
        grid=(t // tm,),
        in_specs=[pl.BlockSpec((tm, w2), lambda i: (i, 0)),
                  pl.BlockSpec((1, w), lambda i: (0, 0)),
                  pl.BlockSpec((1, w), lambda i: (0, 0)),
                  pl.BlockSpec((SGU_HEADS, SGU_BLOCK, SGU_BLOCK), lambda i: (0, 0, 0)),
                  pl.BlockSpec((SGU_BLOCK, w), lambda i: (0, 0))],
        out_specs=pl.BlockSpec((tm, w), lambda i: (i, 0)),
        out_shape=jax.ShapeDtypeStruct((t, w), BF16),
        compiler_params=_params(1),
        name="sgu_mix",
    )(gz, ln_g.reshape(1, w), ln_b.reshape(1, w), w_s, bsx)


def _upgate_kernel(xn_ref, ya_ref, yb_ref, wgc_ref, wgs_ref, wua_ref, wub_ref, eg_ref, eu_ref, ed_ref,
                   o_ref, eg_bf_ref, eu_bf_ref, ed_bf_ref, wgbf_ref, wuabf_ref, wubbf_ref):
    tn = wgc_ref.shape[1]
    eg_bf_ref[...] = eg_ref[...].astype(BF16)
    eu_bf_ref[...] = eu_ref[...].astype(BF16)
    ed_bf_ref[...] = ed_ref[...].astype(BF16)

    @pl.when(pl.program_id(1) == 0)
    def _():
        wgbf_ref[:, 0:tn] = wgc_ref[...].astype(BF16)
        wgbf_ref[:, tn:2 * tn] = wgs_ref[...].astype(BF16)
        wuabf_ref[...] = wua_ref[...].astype(BF16)
        wubbf_ref[...] = wub_ref[...].astype(BF16)

    gl = _dot(xn_ref[...], wgbf_ref[...])
    a = _dot(ya_ref[...], wuabf_ref[...])
    b = _dot(yb_ref[...], wubbf_ref[...])
    m = jax.nn.sigmoid(gl[:, 0:tn]) * a + jax.nn.sigmoid(gl[:, tn:2 * tn]) * b
    o_ref[...] = m.astype(o_ref.dtype)


def _upgate(xn, ya, yb, w_in, gate_col0, w_up_a, w_up_b, w_eg, w_eu, w_ed, tm=512, tn=512):
    t, d = xn.shape
    wa = ya.shape[1]
    wb = yb.shape[1]
    dout = w_up_a.shape[1]
    c0 = gate_col0 // tn
    nj = dout // tn
    ni = t // tm
    n_e, d_e, f_e = w_eg.shape
    up_rows = n_e * d_e // (nj * ni)
    down_rows = n_e * f_e // (nj * ni)
    assert up_rows * nj * ni == n_e * d_e and up_rows % 16 == 0
    assert down_rows * nj * ni == n_e * f_e and down_rows % 16 == 0
    up_spec = pl.BlockSpec((up_rows, f_e), lambda j, i: (j * ni + i, 0))
    down_spec = pl.BlockSpec((down_rows, d_e), lambda j, i: (j * ni + i, 0))
    m, eg_bf, eu_bf, ed_bf = pl.pallas_call(
        _upgate_kernel,
        grid=(nj, ni),
        in_specs=[pl.BlockSpec((tm, d), lambda j, i: (i, 0)),
                  pl.BlockSpec((tm, wa), lambda j, i: (i, 0)),
                  pl.BlockSpec((tm, wb), lambda j, i: (i, 0)),
                  pl.BlockSpec((d, tn), lambda j, i: (0, c0 + j)),
                  pl.BlockSpec((d, tn), lambda j, i: (0, c0 + nj + j)),
                  pl.BlockSpec((wa, tn), lambda j, i: (0, j)),
                  pl.BlockSpec((wb, tn), lambda j, i: (0, j)),
                  up_spec, up_spec, down_spec],
        out_specs=[pl.BlockSpec((tm, tn), lambda j, i: (i, j)), up_spec, up_spec, down_spec],
        out_shape=[jax.ShapeDtypeStruct((t, dout), BF16),
                   jax.ShapeDtypeStruct((n_e * d_e, f_e), BF16),
                   jax.ShapeDtypeStruct((n_e * d_e, f_e), BF16),
                   jax.ShapeDtypeStruct((n_e * f_e, d_e), BF16)],
        scratch_shapes=[pltpu.VMEM((d, 2 * tn), BF16),
                        pltpu.VMEM((wa, tn), BF16),
                        pltpu.VMEM((wb, tn), BF16)],
        compiler_params=_params(2),
        name="upgate",
    )(xn, ya, yb, w_in, w_in, w_up_a, w_up_b,
      w_eg.reshape(n_e * d_e, f_e), w_eu.reshape(n_e * d_e, f_e), w_ed.reshape(n_e * f_e, d_e))
    return (m, eg_bf.reshape(n_e, d_e, f_e), eu_bf.reshape(n_e, d_e, f_e),
            ed_bf.reshape(n_e, f_e, d_e))


def _outproj_kernel(m_ref, w_ref, x_ref, o_ref, wbf_ref):
    @pl.when(pl.program_id(1) == 0)
    def _():
        wbf_ref[...] = w_ref[...].astype(BF16)

    o_ref[...] = x_ref[...] + _dot(m_ref[...], wbf_ref[...])


def _outproj(m, w_out, x, tm=512, tn=1024):
    t, d = m.shape
    dout = w_out.shape[1]
    return pl.pallas_call(
        _outproj_kernel,
        grid=(dout // tn, t // tm),
        in_specs=[pl.BlockSpec((tm, d), lambda j, i: (i, 0)),
                  pl.BlockSpec((d, tn), lambda j, i: (0, j)),
                  pl.BlockSpec((tm, tn), lambda j, i: (i, j))],
        out_specs=pl.BlockSpec((tm, tn), lambda j, i: (i, j)),
        out_shape=jax.ShapeDtypeStruct((t, dout), F32),
        scratch_shapes=[pltpu.VMEM((d, tn), BF16)],
        compiler_params=_params(2),
        name="outproj",
    )(m, w_out, x)


def _argmax_rows(rows):
    best = rows[0]
    idx = jnp.zeros(rows[0].shape, jnp.int32)
    for k in range(1, len(rows)):
        better = rows[k] > best
        best = jnp.where(better, rows[k], best)
        idx = jnp.where(better, k, idx)
    return best, idx


def _softmax_rows(rows):
    mx = functools.reduce(jnp.maximum, rows)
    ex = [jnp.exp(r - mx) for r in rows]
    den = functools.reduce(lambda a, b: a + b, ex)
    return [e / den for e in ex]


def _route_sort_kernel(h_ref, g_ref, wr_ref, br_ref, xs_ref, meta_ref, cnt_ref):
    tm = h_ref.shape[0]
    xn = _rms_scale(h_ref[...], g_ref[...])
    lt = lax.dot_general(wr_ref[...], xn, (((1,), (1,)), ((), ())),
                         precision=lax.Precision.HIGHEST,
                         preferred_element_type=F32) + br_ref[...]
    pgs = _softmax_rows([lt[k:k + 1, :] for k in range(N_GROUPS)])
    pg, gi = _argmax_rows(pgs)
    sel = []
    for k in range(EXPERTS_PER_GROUP):
        r = jnp.zeros_like(pg)
        for g in range(N_GROUPS):
            row = N_GROUPS + g * EXPERTS_PER_GROUP + k
            r = jnp.where(gi == g, lt[row:row + 1, :], r)
        sel.append(r)
    pes = _softmax_rows(sel)
    p1, e1 = _argmax_rows(pes)
    rest = [jnp.where(e1 == k, -1.0, pes[k]) for k in range(EXPERTS_PER_GROUP)]
    p2, e2 = _argmax_rows(rest)
    den = p1 + p2
    w1 = pg * (p1 / den)
    w2 = pg * (p2 / den)
    lo = jnp.minimum(e1, e2)
    hi = jnp.maximum(e1, e2)
    w_lo = jnp.where(e1 < e2, w1, w2)
    w_hi = jnp.where(e1 < e2, w2, w1)
    ea = gi * EXPERTS_PER_GROUP + lo
    eb = gi * EXPERTS_PER_GROUP + hi

    erow = lax.broadcasted_iota(jnp.int32, (N_EXPERTS, tm), 0)
    oh_a = (erow == ea).astype(F32)
    oh_b = (erow == eb).astype(F32)
    a = lax.broadcasted_iota(jnp.int32, (tm, tm), 0)
    b = lax.broadcasted_iota(jnp.int32, (tm, tm), 1)
    before = (a < b).astype(BF16)
    cum = _dot((oh_a + oh_b).astype(BF16), before)
    cnt = jnp.sum(oh_a + oh_b, axis=1, keepdims=True)
    padded = jnp.floor((cnt + (GROUP - 1)) * (1.0 / GROUP)) * GROUP
    pos_a = jnp.sum(oh_a * cum + jnp.where(erow < ea, padded, 0.0), axis=0, keepdims=True)
    pos_b = jnp.sum(oh_b * cum + jnp.where(erow < eb, padded, 0.0), axis=0, keepdims=True)

    d = h_ref.shape[1]
    q = lax.broadcasted_iota(jnp.int32, (xs_ref.shape[0], tm), 0)
    perm_a = jnp.where(q == pos_a.astype(jnp.int32), 1.0, 0.0).astype(BF16)
    perm_b = jnp.where(q == pos_b.astype(jnp.int32), 1.0, 0.0).astype(BF16)
    xs_ref[:, 0:d] = _dot(perm_a + perm_b, xn.astype(BF16)).astype(xs_ref.dtype)

    def gate_rows(w):
        hi = w.astype(BF16).astype(F32)
        lo = w - hi
        k = lax.broadcasted_iota(jnp.int32, (LANES, tm), 0)
        return jnp.where(k == 0, hi, jnp.where(k == 1, lo, 0.0)).astype(BF16)

    nt_dims = (((1,), (1,)), ((), ()))
    gates = (lax.dot_general(perm_a, gate_rows(w_lo), nt_dims, preferred_element_type=F32)
             + lax.dot_general(perm_b, gate_rows(w_hi), nt_dims, preferred_element_type=F32))
    xs_ref[:, d:d + LANES] = gates.astype(xs_ref.dtype)

    cnt_ref[...] = jnp.broadcast_to(cnt, cnt_ref.shape)
    meta_ref[0:1, :] = pos_a
    meta_ref[1:2, :] = pos_b
    meta_ref[2:8, :] = jnp.zeros((6, tm), F32)


def _route_sort(h1, g, w_rg, b_rg, w_re, b_re):
    t, d = h1.shape
    tm = ROUTE_BLOCK
    nb = t // tm
    n_log = w_rg.shape[1] + w_re.shape[1]
    wr = jnp.concatenate([w_rg, w_re], axis=1).T
    wr = jnp.pad(wr, ((0, ROUTER_ROWS - n_log), (0, 0)))
    br = jnp.pad(jnp.concatenate([b_rg, b_re]), (0, ROUTER_ROWS - n_log)).reshape(ROUTER_ROWS, 1)
    return pl.pallas_call(
        _route_sort_kernel,
        grid=(nb,),
        in_specs=[pl.BlockSpec((tm, d), lambda i: (i, 0)),
                  pl.BlockSpec((1, d), lambda i: (0, 0)),
                  pl.BlockSpec((ROUTER_ROWS, d), lambda i: (0, 0)),
                  pl.BlockSpec((ROUTER_ROWS, 1), lambda i: (0, 0))],
        out_specs=[pl.BlockSpec((LOCAL_ROWS, d + LANES), lambda i: (i, 0)),
                   pl.BlockSpec((8, tm), lambda i: (0, i)),
                   pl.BlockSpec((N_EXPERTS, LANES), lambda i: (i, 0))],
        out_shape=[jax.ShapeDtypeStruct((nb * LOCAL_ROWS, d + LANES), BF16),
                   jax.ShapeDtypeStruct((8, t), F32),
                   jax.ShapeDtypeStruct((nb * N_EXPERTS, LANES), F32)],
        compiler_params=_params(1),
        name="route_sort",
    )(h1, g.reshape(1, d), wr, br)


def _group_copy(src_ref, s_group, dst_ref, d_group, sem):
    return pltpu.make_async_copy(src_ref.at[pl.ds(pl.multiple_of(s_group * GROUP, GROUP), GROUP), :],
                                 dst_ref.at[pl.ds(pl.multiple_of(d_group * GROUP, GROUP), GROUP), :], sem)


def _expert_kernel(te_ref, nxt_ref, nreal_ref, nt_ref, gsrc_ref, gdst_ref, tail_ref,
                   xs_ref, wg_ref, wu_ref, wd_ref, ys_ref,
                   xbuf_ref, ybuf_ref, zero_ref, wgu_buf_ref, wd_buf_ref,
                   cur_ref, gsem, ssem, zsem, wsem):
    r = pl.program_id(0)
    nt = nt_ref[0]

    def gather(q, slot):
        for i in range(TILE_GROUPS):
            _group_copy(xs_ref, gsrc_ref[q * TILE_GROUPS + i], xbuf_ref, slot * TILE_GROUPS + i,
                        gsem.at[slot]).start(priority=1)

    def weight_copies(e, wslot):
        f = wg_ref.shape[2]
        return (pltpu.make_async_copy(wg_ref.at[e], wgu_buf_ref.at[wslot, :, pl.ds(0, f)], wsem.at[wslot]),
                pltpu.make_async_copy(wu_ref.at[e], wgu_buf_ref.at[wslot, :, pl.ds(f, f)], wsem.at[wslot]),
                pltpu.make_async_copy(wd_ref.at[e], wd_buf_ref.at[wslot], wsem.at[wslot]))

    def tile_wait(src, dst, sem):
        pltpu.make_async_copy(src.at[pl.ds(0, MOE_TILE), :], dst.at[pl.ds(0, MOE_TILE), :], sem).wait()

    @pl.when(r == 0)
    def _():
        cur_ref[0] = -1
        cur_ref[1] = -1
        for cp in weight_copies(te_ref[0], 0):
            cp.start()
        zero_ref[...] = jnp.zeros_like(zero_ref)
        ybuf_ref[...] = jnp.zeros_like(ybuf_ref)
        n_blocks = tail_ref.shape[0]

        def fill(make):
            def body(g, c):
                make(g)
                return c
            return body

        for blk in range(n_blocks):
            lax.fori_loop(tail_ref[blk], GROUPS_PER_BLOCK, fill(
                lambda g, blk=blk: _group_copy(zero_ref, 0, ys_ref, blk * GROUPS_PER_BLOCK + g, zsem).start()), 0)
        for blk in range(n_blocks):
            lax.fori_loop(tail_ref[blk], GROUPS_PER_BLOCK, fill(
                lambda g, blk=blk: _group_copy(zero_ref, 0, ys_ref, blk * GROUPS_PER_BLOCK + g, zsem).wait()), 0)
        gather(0, 0)

    @pl.when(r < nt)
    def _():
        e = te_ref[r]
        slot = r % 2

        gather(jnp.minimum(r + 1, nt - 1), 1 - slot)

        @pl.when(cur_ref[0] != e)
        def _():
            wslot = (cur_ref[1] + 1) % 2
            for cp in weight_copies(e, wslot):
                cp.wait()
            cur_ref[0] = e
            cur_ref[1] = cur_ref[1] + 1

            @pl.when(nxt_ref[r] >= 0)
            def _():
                for cp in weight_copies(nxt_ref[r], 1 - wslot):
                    cp.start()

        wslot = cur_ref[1] % 2
        tile_wait(xs_ref, xbuf_ref, gsem.at[slot])

        @pl.when(r >= 2)
        def _():
            tile_wait(ybuf_ref, ys_ref, ssem.at[slot])

        d = ybuf_ref.shape[1]
        f = wd_buf_ref.shape[1]
        for part in range(MOE_TILE // MXU_ROWS):
            @pl.when(nreal_ref[r] > part * (MXU_ROWS // GROUP))
            def _(part=part):
                rows = pl.ds(pl.multiple_of(slot * MOE_TILE + part * MXU_ROWS, MXU_ROWS), MXU_ROWS)
                x = xbuf_ref[rows, 0:d]
                gate_parts = xbuf_ref[rows, d:d + LANES].astype(F32)
                gate = gate_parts[:, 0:1] + gate_parts[:, 1:2]
                gu = _dot(x, wgu_buf_ref[wslot])
                hid = (jax.nn.silu(gu[:, 0:f]) * gu[:, f:2 * f]).astype(BF16)
                ybuf_ref[rows, :] = (_dot(hid, wd_buf_ref[wslot]) * gate).astype(ybuf_ref.dtype)

        for i in range(TILE_GROUPS):
            _group_copy(ybuf_ref, slot * TILE_GROUPS + i, ys_ref, gdst_ref[r * TILE_GROUPS + i],
                        ssem.at[slot]).start(priority=1)

        @pl.when(r == nt - 1)
        def _():
            tile_wait(xs_ref, xbuf_ref, gsem.at[1 - slot])
            tile_wait(ybuf_ref, ys_ref, ssem.at[slot])

            @pl.when(r >= 1)
            def _():
                tile_wait(ybuf_ref, ys_ref, ssem.at[1 - slot])


def _experts(plan, xs, w_gate, w_up, w_down):
    d = w_gate.shape[1]
    f = w_gate.shape[2]
    tile_e, next_e, n_real, n_tiles, gsrc, gdst, tail = plan
    r_max = tile_e.shape[0]
    n_blocks = tail.shape[0]
    any_spec = pl.BlockSpec(memory_space=pl.ANY)
    grid_spec = pltpu.PrefetchScalarGridSpec(
        num_scalar_prefetch=7,
        grid=(r_max,),
        in_specs=[any_spec, any_spec, any_spec, any_spec],
        out_specs=any_spec,
        scratch_shapes=[pltpu.VMEM((2 * MOE_TILE, d + LANES), BF16),
                        pltpu.VMEM((2 * MOE_TILE, d), BF16),
                        pltpu.VMEM((GROUP, d), BF16),
                        pltpu.VMEM((2, d, 2 * f), BF16),
                        pltpu.VMEM((2, f, d), BF16),
                        pltpu.SMEM((2,), jnp.int32),
                        pltpu.SemaphoreType.DMA((2,)),
                        pltpu.SemaphoreType.DMA((2,)),
                        pltpu.SemaphoreType.DMA(()),
                        pltpu.SemaphoreType.DMA((2,))],
    )
    return pl.pallas_call(
        _expert_kernel,
        grid_spec=grid_spec,
        out_shape=jax.ShapeDtypeStruct((n_blocks * LOCAL_ROWS, d), BF16),
        compiler_params=_params(1),
        name="experts",
    )(tile_e, next_e, n_real, n_tiles, gsrc, gdst, tail, xs, w_gate, w_up, w_down)


def _combine_kernel(ys_ref, h_ref, cm_ref, g_ref, o_ref):
    tm = h_ref.shape[0]
    cm = cm_ref[...]
    q = lax.broadcasted_iota(jnp.int32, (tm, ys_ref.shape[0]), 1)
    sel = jnp.where((q == cm[:, 0:1].astype(jnp.int32)) | (q == cm[:, 1:2].astype(jnp.int32)), 1.0, 0.0)
    h2 = h_ref[...] + _dot(sel.astype(BF16), ys_ref[...])
    o_ref[...] = _rms_scale(h2, g_ref[...])


def _combine(ys, h1, cmeta, g):
    t, d = h1.shape
    tm = ROUTE_BLOCK
    return pl.pallas_call(
        _combine_kernel,
        grid=(t // tm,),
        in_specs=[pl.BlockSpec((LOCAL_ROWS, d), lambda i: (i, 0)),
                  pl.BlockSpec((tm, d), lambda i: (i, 0)),
                  pl.BlockSpec((tm, LANES), lambda i: (i, 0)),
                  pl.BlockSpec((1, d), lambda i: (0, 0))],
        out_specs=pl.BlockSpec((tm, d), lambda i: (i, 0)),
        out_shape=jax.ShapeDtypeStruct((t, d), F32),
        compiler_params=_params(1),
        name="combine",
    )(ys, h1, cmeta, g.reshape(1, d))


def _expert_plan(counts, n_blocks):
    cnt = counts[:, 0].astype(jnp.int32).reshape(n_blocks, N_EXPERTS)
    groups = (cnt + GROUP - 1) // GROUP
    first = jnp.cumsum(groups, axis=1) - groups
    upto = jnp.cumsum(groups, axis=0)
    per_expert = upto[-1]
    tiles_e = (per_expert + TILE_GROUPS - 1) // TILE_GROUPS
    tile_end = jnp.cumsum(tiles_e)
    n_tiles = tile_end[-1]
    max_groups = 2 * ROUTE_BLOCK * n_blocks // GROUP + n_blocks * N_EXPERTS
    r_max = max_groups // TILE_GROUPS + N_EXPERTS
    tile_ids = jnp.arange(r_max, dtype=jnp.int32)
    tile = jnp.minimum(tile_ids, n_tiles - 1)
    tile_e = jnp.sum((tile_end[None, :] <= tile[:, None]).astype(jnp.int32), axis=1)
    later = (tile_e[None, :] > tile_e[:, None]) & (tile_ids[None, :] < n_tiles)
    next_e = jnp.min(jnp.where(later, tile_e[None, :], N_EXPERTS), axis=1)
    next_e = jnp.where(next_e == N_EXPERTS, -1, next_e)

    slot = jnp.arange(r_max * TILE_GROUPS, dtype=jnp.int32)
    s_tile = slot // TILE_GROUPS
    oh_e = jnp.repeat(tile_e, TILE_GROUPS)[:, None] == jnp.arange(N_EXPERTS, dtype=jnp.int32)[None, :]

    def by_expert(table):
        return jnp.sum(jnp.where(oh_e[:, None, :], table[None], 0), axis=-1)

    k = slot - by_expert(((tile_end - tiles_e) * TILE_GROUPS)[None, :])[:, 0]
    real = (k < by_expert(per_expert[None, :])[:, 0]) & (s_tile < n_tiles)
    upto_e = by_expert(upto)
    blk = jnp.minimum(jnp.sum((upto_e <= k[:, None]).astype(jnp.int32), axis=1), n_blocks - 1)
    oh_b = blk[:, None] == jnp.arange(n_blocks, dtype=jnp.int32)[None, :]

    def by_block(table_se):
        return jnp.sum(jnp.where(oh_b, table_se, 0), axis=1)

    before = by_block(upto_e - by_expert(groups))
    src = blk * GROUPS_PER_BLOCK + by_block(by_expert(first)) + (k - before)
    zero_group = GROUPS_PER_BLOCK - 1
    spare = n_blocks * GROUPS_PER_BLOCK + (s_tile % 2) * TILE_GROUPS + slot % TILE_GROUPS
    gsrc = jnp.where(real, src, zero_group)
    gdst = jnp.where(real, src, spare)
    tail = jnp.concatenate([jnp.sum(groups, axis=1), jnp.zeros((1,), jnp.int32)])
    n_real = jnp.sum(real.reshape(r_max, TILE_GROUPS).astype(jnp.int32), axis=1)
    return tile_e, next_e, n_real, n_tiles.reshape(1), gsrc, gdst, tail


def _layer(h, norm_mix_g, w_in, conv_w, sgu_ln_g, sgu_ln_b, sgu_w_s, sgu_b_s, w_up_conv,
           w_up_sgu, w_out, norm_ffn_g, w_rg, b_rg, w_re, b_re, w_eg, w_eu, w_ed, out_g):
    t, d = h.shape
    conv_width = conv_w.shape[1]
    sgu_width = sgu_ln_g.shape[0]
    xn = _rmsnorm(h, norm_mix_g)
    ya = _conv_branch(xn, w_in, conv_w, conv_width)
    gz = _zproj(xn, w_in, 3 * conv_width, 2 * sgu_width)
    yb = _sgu_mix(gz, sgu_ln_g, sgu_ln_b, sgu_w_s, sgu_b_s)
    m, eg_bf, eu_bf, ed_bf = _upgate(xn, ya, yb, w_in, 3 * conv_width + 2 * sgu_width,
                                     w_up_conv, w_up_sgu, w_eg, w_eu, w_ed)
    h1 = _outproj(m, w_out, h)
    xs, meta, counts = _route_sort(h1, norm_ffn_g, w_rg, b_rg, w_re, b_re)
    ys = _experts(_expert_plan(counts, t // ROUTE_BLOCK), xs, eg_bf, eu_bf, ed_bf)
    cmeta = jnp.pad(meta[0:2].T, ((0, 0), (0, LANES - 2)))
    return _combine(ys, h1, cmeta, out_g)


def kernel(x, norm_mix_g, w_in, conv_w, sgu_ln_g, sgu_ln_b, sgu_w_s, sgu_b_s, w_up_conv, w_up_sgu, w_out, norm_ffn_g, w_router_group, b_router_group, w_router_expert, b_router_expert, w_exp_gate, w_exp_up, w_exp_down, norm_final_g):
    bsz, s, d = x.shape
    depth = w_in.shape[0]
    assert bsz == 1 and depth == 1, "causal conv carry and the fused final norm assume one sequence, one layer"
    assert s % ROUTE_BLOCK == 0
    out = _layer(x.reshape(s, d), norm_mix_g[0], w_in[0], conv_w[0], sgu_ln_g[0], sgu_ln_b[0],
                 sgu_w_s[0], sgu_b_s[0], w_up_conv[0], w_up_sgu[0], w_out[0], norm_ffn_g[0],
                 w_router_group[0], b_router_group[0], w_router_expert[0], b_router_expert[0],
                 w_exp_gate[0], w_exp_up[0], w_exp_down[0], norm_final_g)
    return out.reshape(bsz, s, d)
```

```python
import functools

import jax
import jax.numpy as jnp
from jax import lax
from jax.experimental import pallas as pl
from jax.experimental.pallas import tpu as pltpu

F32 = jnp.float32
BF16 = jnp.bfloat16

EPS = 1e-6
CHUNK = 64
CONV_K = 3
SGU_HEADS = 8
SGU_BLOCK = 128
N_GROUPS = 4
EXPERTS_PER_GROUP = 4
N_EXPERTS = N_GROUPS * EXPERTS_PER_GROUP
ROUTER_ROWS = 32
LANES = 128

VMEM_LIMIT_BYTES = 56 * 1024 * 1024

ROUTE_BLOCK = 512
GROUP = 16
LOCAL_ROWS = -(-(2 * ROUTE_BLOCK + N_EXPERTS * (GROUP - 1)) // 256) * 256
GROUPS_PER_BLOCK = LOCAL_ROWS // GROUP
MOE_TILE = 512
TILE_GROUPS = MOE_TILE // GROUP
MXU_ROWS = 256


def _params(n_axes):
    return pltpu.CompilerParams(
        dimension_semantics=("arbitrary",) * n_axes,
        vmem_limit_bytes=VMEM_LIMIT_BYTES)


def _dot(a, b):
    return jnp.dot(a, b, preferred_element_type=F32)


def _rms_scale(x, g):
    ms = jnp.mean(x * x, axis=-1, keepdims=True)
    return x * lax.rsqrt(ms + EPS) * g


def _rmsnorm_kernel(x_ref, g_ref, o_ref):
    o_ref[...] = _rms_scale(x_ref[...], g_ref[...]).astype(o_ref.dtype)


def _rmsnorm(x, g, tm=512):
    t, d = x.shape
    return pl.pallas_call(
        _rmsnorm_kernel,
        grid=(t // tm,),
        in_specs=[pl.BlockSpec((tm, d), lambda i: (i, 0)),
                  pl.BlockSpec((1, d), lambda i: (0, 0))],
        out_specs=pl.BlockSpec((tm, d), lambda i: (i, 0)),
        out_shape=jax.ShapeDtypeStruct((t, d), BF16),
        compiler_params=_params(1),
        name="rmsnorm",
    )(x, g.reshape(1, d))


def _conv_kernel(xn_ref, wb_ref, wc_ref, wh_ref, cw_ref, o_ref, wbf_ref, carry_ref):
    i = pl.program_id(1)
    tn = wb_ref.shape[1]
    tm = xn_ref.shape[0]

    @pl.when(i == 0)
    def _():
        wbf_ref[:, 0:tn] = wb_ref[...].astype(BF16)
        wbf_ref[:, tn:2 * tn] = wc_ref[...].astype(BF16)
        wbf_ref[:, 2 * tn:3 * tn] = wh_ref[...].astype(BF16)
        carry_ref[...] = jnp.zeros_like(carry_ref)

    proj = _dot(xn_ref[...], wbf_ref[...])
    b = proj[:, 0:tn]
    p = proj[:, tn:2 * tn] * proj[:, 2 * tn:3 * tn]
    prev = carry_ref[...]
    carry_ref[...] = p[tm - 8:tm, :]
    row = lax.broadcasted_iota(jnp.int32, p.shape, 0)
    p1 = jnp.where(row == 0, prev[7:8, :], pltpu.roll(p, 1, axis=0))
    p2 = jnp.where(row == 0, prev[6:7, :],
                   jnp.where(row == 1, prev[7:8, :], pltpu.roll(p, 2, axis=0)))
    cw = cw_ref[...]
    y = b * (cw[0:1, :] * p2 + cw[1:2, :] * p1 + cw[2:3, :] * p)
    o_ref[...] = y.astype(o_ref.dtype)


def _conv_branch(xn, w_in, conv_w, width, tm=1024, tn=256):
    t, d = xn.shape
    nj = width // tn
    return pl.pallas_call(
        _conv_kernel,
        grid=(nj, t // tm),
        in_specs=[pl.BlockSpec((tm, d), lambda j, i: (i, 0)),
                  pl.BlockSpec((d, tn), lambda j, i: (0, j)),
                  pl.BlockSpec((d, tn), lambda j, i: (0, nj + j)),
                  pl.BlockSpec((d, tn), lambda j, i: (0, 2 * nj + j)),
                  pl.BlockSpec((CONV_K, tn), lambda j, i: (0, j))],
        out_specs=pl.BlockSpec((tm, tn), lambda j, i: (i, j)),
        out_shape=jax.ShapeDtypeStruct((t, width), BF16),
        scratch_shapes=[pltpu.VMEM((d, 3 * tn), BF16),
                        pltpu.VMEM((8, tn), F32)],
        compiler_params=_params(2),
        name="conv_branch",
    )(xn, w_in, w_in, w_in, conv_w)


def _sgu_kernel(xn_ref, wu_ref, wv_ref, lng_ref, lnb_ref, ws_ref, bsx_ref, o_ref, wbf_ref):
    tm = xn_ref.shape[0]
    w = o_ref.shape[1]
    hd = w // SGU_HEADS

    @pl.when(pl.program_id(0) == 0)
    def _():
        wbf_ref[:, 0:w] = wu_ref[...].astype(BF16)
        wbf_ref[:, w:2 * w] = wv_ref[...].astype(BF16)

    gz = jax.nn.gelu(_dot(xn_ref[...], wbf_ref[...]))
    v = gz[:, w:2 * w]
    mu = jnp.mean(v, axis=-1, keepdims=True)
    vc = v - mu
    var = jnp.mean(vc * vc, axis=-1, keepdims=True)
    vn = (vc * lax.rsqrt(var + EPS) * lng_ref[...] + lnb_ref[...]).astype(BF16)
    ii = lax.broadcasted_iota(jnp.int32, (SGU_BLOCK, SGU_BLOCK), 0)
    jj = lax.broadcasted_iota(jnp.int32, (SGU_BLOCK, SGU_BLOCK), 1)
    mask = (jj // CHUNK) <= (ii // CHUNK)
    for h in range(SGU_HEADS):
        wm = jnp.where(mask, ws_ref[h], 0.0).astype(BF16)
        cs = slice(h * hd, (h + 1) * hd)
        for n in range(tm // SGU_BLOCK):
            rs = slice(n * SGU_BLOCK, (n + 1) * SGU_BLOCK)
            vm = _dot(wm, vn[rs, cs]) + bsx_ref[:, cs]
            o_ref[rs, cs] = (gz[rs, cs] * vm).astype(o_ref.dtype)


def _sgu_branch(xn, w_in, col0, ln_g, ln_b, w_s, b_s, tm=512):
    t, d = xn.shape
    w = ln_g.shape[0]
    hd = w // SGU_HEADS
    assert col0 % w == 0
    c0 = col0 // w
    bsx = jnp.repeat(b_s.T, hd, axis=1)
    once = pl.Buffered(1)
    return pl.pallas_call(
        _sgu_kernel,
        grid=(t // tm,),
        in_specs=[pl.BlockSpec((tm, d), lambda i: (i, 0)),
                  pl.BlockSpec((d, w), lambda i: (0, c0), pipeline_mode=once),
                  pl.BlockSpec((d, w), lambda i: (0, c0 + 1), pipeline_mode=once),
                  pl.BlockSpec((1, w), lambda i: (0, 0)),
                  pl.BlockSpec((1, w), lambda i: (0, 0)),
                  pl.BlockSpec((SGU_HEADS, SGU_BLOCK, SGU_BLOCK), lambda i: (0, 0, 0)),
                  pl.BlockSpec((SGU_BLOCK, w), lambda i: (0, 0))],
        out_specs=pl.BlockSpec((tm, w), lambda i: (i, 0)),
        out_shape=jax.ShapeDtypeStruct((t, w), BF16),
        scratch_shapes=[pltpu.VMEM((d, 2 * w), BF16)],
        compiler_params=_params(1),
        name="sgu_branch",
    )(xn, w_in, w_in, ln_g.reshape(1, w), ln_b.reshape(1, w), w_s, bsx)


def _upgate_kernel(xn_ref, ya_ref, yb_ref, wgc_ref, wgs_ref, wua_ref, wub_ref, eg_ref, eu_ref, ed_ref,
                   o_ref, eg_bf_ref, eu_bf_ref, ed_bf_ref, wgbf_ref, wuabf_ref, wubbf_ref):
    tn = wgc_ref.shape[1]
    eg_bf_ref[...] = eg_ref[...].astype(BF16)
    eu_bf_ref[...] = eu_ref[...].astype(BF16)
    ed_bf_ref[...] = ed_ref[...].astype(BF16)

    @pl.when(pl.program_id(1) == 0)
    def _():
        wgbf_ref[:, 0:tn] = wgc_ref[...].astype(BF16)
        wgbf_ref[:, tn:2 * tn] = wgs_ref[...].astype(BF16)
        wuabf_ref[...] = wua_ref[...].astype(BF16)
        wubbf_ref[...] = wub_ref[...].astype(BF16)

    gl = _dot(xn_ref[...], wgbf_ref[...])
    a = _dot(ya_ref[...], wuabf_ref[...])
    b = _dot(yb_ref[...], wubbf_ref[...])
    m = jax.nn.sigmoid(gl[:, 0:tn]) * a + jax.nn.sigmoid(gl[:, tn:2 * tn]) * b
    o_ref[...] = m.astype(o_ref.dtype)


def _upgate(xn, ya, yb, w_in, gate_col0, w_up_a, w_up_b, w_eg, w_eu, w_ed, tm=512, tn=512):
    t, d = xn.shape
    wa = ya.shape[1]
    wb = yb.shape[1]
    dout = w_up_a.shape[1]
    c0 = gate_col0 // tn
    nj = dout // tn
    ni = t // tm
    n_e, d_e, f_e = w_eg.shape
    up_rows = n_e * d_e // (nj * ni)
    down_rows = n_e * f_e // (nj * ni)
    assert up_rows * nj * ni == n_e * d_e and up_rows % 16 == 0
    assert down_rows * nj * ni == n_e * f_e and down_rows % 16 == 0
    up_spec = pl.BlockSpec((up_rows, f_e), lambda j, i: (j * ni + i, 0))
    down_spec = pl.BlockSpec((down_rows, d_e), lambda j, i: (j * ni + i, 0))
    m, eg_bf, eu_bf, ed_bf = pl.pallas_call(
        _upgate_kernel,
        grid=(nj, ni),
        in_specs=[pl.BlockSpec((tm, d), lambda j, i: (i, 0)),
                  pl.BlockSpec((tm, wa), lambda j, i: (i, 0)),
                  pl.BlockSpec((tm, wb), lambda j, i: (i, 0)),
                  pl.BlockSpec((d, tn), lambda j, i: (0, c0 + j)),
                  pl.BlockSpec((d, tn), lambda j, i: (0, c0 + nj + j)),
                  pl.BlockSpec((wa, tn), lambda j, i: (0, j)),
                  pl.BlockSpec((wb, tn), lambda j, i: (0, j)),
                  up_spec, up_spec, down_spec],
        out_specs=[pl.BlockSpec((tm, tn), lambda j, i: (i, j)), up_spec, up_spec, down_spec],
        out_shape=[jax.ShapeDtypeStruct((t, dout), BF16),
                   jax.ShapeDtypeStruct((n_e * d_e, f_e), BF16),
                   jax.ShapeDtypeStruct((n_e * d_e, f_e), BF16),
                   jax.ShapeDtypeStruct((n_e * f_e, d_e), BF16)],
        scratch_shapes=[pltpu.VMEM((d, 2 * tn), BF16),
                        pltpu.VMEM((wa, tn), BF16),
                        pltpu.VMEM((wb, tn), BF16)],
        compiler_params=_params(2),
        name="upgate",
    )(xn, ya, yb, w_in, w_in, w_up_a, w_up_b,
      w_eg.reshape(n_e * d_e, f_e), w_eu.reshape(n_e * d_e, f_e), w_ed.reshape(n_e * f_e, d_e))
    return (m, eg_bf.reshape(n_e, d_e, f_e), eu_bf.reshape(n_e, d_e, f_e),
            ed_bf.reshape(n_e, f_e, d_e))


def _outproj_kernel(m_ref, w_ref, x_ref, o_ref, wbf_ref):
    @pl.when(pl.program_id(1) == 0)
    def _():
        wbf_ref[...] = w_ref[...].astype(BF16)

    o_ref[...] = x_ref[...] + _dot(m_ref[...], wbf_ref[...])


def _outproj(m, w_out, x, tm=512, tn=1024):
    t, d = m.shape
    dout = w_out.shape[1]
    return pl.pallas_call(
        _outproj_kernel,
        grid=(dout // tn, t // tm),
        in_specs=[pl.BlockSpec((tm, d), lambda j, i: (i, 0)),
                  pl.BlockSpec((d, tn), lambda j, i: (0, j)),
                  pl.BlockSpec((tm, tn), lambda j, i: (i, j))],
        out_specs=pl.BlockSpec((tm, tn), lambda j, i: (i, j)),
        out_shape=jax.ShapeDtypeStruct((t, dout), F32),
        scratch_shapes=[pltpu.VMEM((d, tn), BF16)],
        compiler_params=_params(2),
        name="outproj",
    )(m, w_out, x)


def _argmax_rows(rows):
    best = rows[0]
    idx = jnp.zeros(rows[0].shape, jnp.int32)
    for k in range(1, len(rows)):
        better = rows[k] > best
        best = jnp.where(better, rows[k], best)
        idx = jnp.where(better, k, idx)
    return best, idx


def _softmax_rows(rows):
    mx = functools.reduce(jnp.maximum, rows)
    ex = [jnp.exp(r - mx) for r in rows]
    den = functools.reduce(lambda a, b: a + b, ex)
    return [e / den for e in ex]


def _route_sort_kernel(h_ref, g_ref, wr_ref, br_ref, xs_ref, meta_ref, cnt_ref):
    tm = h_ref.shape[0]
    xn = _rms_scale(h_ref[...], g_ref[...])
    xn_hi = xn.astype(BF16)
    xn_lo = (xn - xn_hi.astype(F32)).astype(BF16)
    wr = wr_ref[...]
    wr_hi = wr.astype(BF16)
    wr_lo = (wr - wr_hi.astype(F32)).astype(BF16)
    nt_dims = (((1,), (1,)), ((), ()))
    lt = (lax.dot_general(wr_hi, xn_hi, nt_dims, preferred_element_type=F32)
          + lax.dot_general(wr_hi, xn_lo, nt_dims, preferred_element_type=F32)
          + lax.dot_general(wr_lo, xn_hi, nt_dims, preferred_element_type=F32)) + br_ref[...]
    pgs = _softmax_rows([lt[k:k + 1, :] for k in range(N_GROUPS)])
    pg, gi = _argmax_rows(pgs)
    sel = []
    for k in range(EXPERTS_PER_GROUP):
        r = jnp.zeros_like(pg)
        for g in range(N_GROUPS):
            row = N_GROUPS + g * EXPERTS_PER_GROUP + k
            r = jnp.where(gi == g, lt[row:row + 1, :], r)
        sel.append(r)
    pes = _softmax_rows(sel)
    p1, e1 = _argmax_rows(pes)
    rest = [jnp.where(e1 == k, -1.0, pes[k]) for k in range(EXPERTS_PER_GROUP)]
    p2, e2 = _argmax_rows(rest)
    den = p1 + p2
    w1 = pg * (p1 / den)
    w2 = pg * (p2 / den)
    lo = jnp.minimum(e1, e2)
    hi = jnp.maximum(e1, e2)
    w_lo = jnp.where(e1 < e2, w1, w2)
    w_hi = jnp.where(e1 < e2, w2, w1)
    ea = gi * EXPERTS_PER_GROUP + lo
    eb = gi * EXPERTS_PER_GROUP + hi

    erow = lax.broadcasted_iota(jnp.int32, (N_EXPERTS, tm), 0)
    oh_a = (erow == ea).astype(F32)
    oh_b = (erow == eb).astype(F32)
    a = lax.broadcasted_iota(jnp.int32, (tm, tm), 0)
    b = lax.broadcasted_iota(jnp.int32, (tm, tm), 1)
    before = (a < b).astype(BF16)
    cum = _dot((oh_a + oh_b).astype(BF16), before)
    cnt = jnp.sum(oh_a + oh_b, axis=1, keepdims=True)
    padded = jnp.floor((cnt + (GROUP - 1)) * (1.0 / GROUP)) * GROUP
    pos_a = jnp.sum(oh_a * cum + jnp.where(erow < ea, padded, 0.0), axis=0, keepdims=True)
    pos_b = jnp.sum(oh_b * cum + jnp.where(erow < eb, padded, 0.0), axis=0, keepdims=True)

    d = h_ref.shape[1]
    q = lax.broadcasted_iota(jnp.int32, (xs_ref.shape[0], tm), 0)
    perm_a = jnp.where(q == pos_a.astype(jnp.int32), 1.0, 0.0).astype(BF16)
    perm_b = jnp.where(q == pos_b.astype(jnp.int32), 1.0, 0.0).astype(BF16)
    xs_ref[:, 0:d] = _dot(perm_a + perm_b, xn_hi).astype(xs_ref.dtype)

    def gate_rows(w):
        hi = w.astype(BF16).astype(F32)
        lo = w - hi
        k = lax.broadcasted_iota(jnp.int32, (LANES, tm), 0)
        return jnp.where(k == 0, hi, jnp.where(k == 1, lo, 0.0)).astype(BF16)

    gates = (lax.dot_general(perm_a, gate_rows(w_lo), nt_dims, preferred_element_type=F32)
             + lax.dot_general(perm_b, gate_rows(w_hi), nt_dims, preferred_element_type=F32))
    xs_ref[:, d:d + LANES] = gates.astype(xs_ref.dtype)

    cnt_ref[...] = jnp.broadcast_to(cnt, cnt_ref.shape)
    meta_ref[0:1, :] = pos_a
    meta_ref[1:2, :] = pos_b
    meta_ref[2:8, :] = jnp.zeros((6, tm), F32)


def _route_sort(h1, g, w_rg, b_rg, w_re, b_re):
    t, d = h1.shape
    tm = ROUTE_BLOCK
    nb = t // tm
    n_log = w_rg.shape[1] + w_re.shape[1]
    wr = jnp.concatenate([w_rg, w_re], axis=1).T
    wr = jnp.pad(wr, ((0, ROUTER_ROWS - n_log), (0, 0)))
    br = jnp.pad(jnp.concatenate([b_rg, b_re]), (0, ROUTER_ROWS - n_log)).reshape(ROUTER_ROWS, 1)
    return pl.pallas_call(
        _route_sort_kernel,
        grid=(nb,),
        in_specs=[pl.BlockSpec((tm, d), lambda i: (i, 0)),
                  pl.BlockSpec((1, d), lambda i: (0, 0)),
                  pl.BlockSpec((ROUTER_ROWS, d), lambda i: (0, 0)),
                  pl.BlockSpec((ROUTER_ROWS, 1), lambda i: (0, 0))],
        out_specs=[pl.BlockSpec((LOCAL_ROWS, d + LANES), lambda i: (i, 0)),
                   pl.BlockSpec((8, tm), lambda i: (0, i)),
                   pl.BlockSpec((N_EXPERTS, LANES), lambda i: (i, 0))],
        out_shape=[jax.ShapeDtypeStruct((nb * LOCAL_ROWS, d + LANES), BF16),
                   jax.ShapeDtypeStruct((8, t), F32),
                   jax.ShapeDtypeStruct((nb * N_EXPERTS, LANES), F32)],
        compiler_params=_params(1),
        name="route_sort",
    )(h1, g.reshape(1, d), wr, br)


def _group_copy(src_ref, s_group, dst_ref, d_group, sem):
    return pltpu.make_async_copy(src_ref.at[pl.ds(pl.multiple_of(s_group * GROUP, GROUP), GROUP), :],
                                 dst_ref.at[pl.ds(pl.multiple_of(d_group * GROUP, GROUP), GROUP), :], sem)


def _expert_kernel(te_ref, nxt_ref, nreal_ref, nt_ref, gsrc_ref, gdst_ref, tail_ref,
                   xs_ref, wg_ref, wu_ref, wd_ref, ys_ref,
                   xbuf_ref, ybuf_ref, zero_ref, wgu_buf_ref, wd_buf_ref,
                   cur_ref, gsem, ssem, zsem, wsem):
    r = pl.program_id(0)
    nt = nt_ref[0]

    def gather(q, slot):
        for i in range(TILE_GROUPS):
            _group_copy(xs_ref, gsrc_ref[q * TILE_GROUPS + i], xbuf_ref, slot * TILE_GROUPS + i,
                        gsem.at[slot]).start(priority=1)

    def weight_copies(e, wslot):
        f = wg_ref.shape[2]
        return (pltpu.make_async_copy(wg_ref.at[e], wgu_buf_ref.at[wslot, :, pl.ds(0, f)], wsem.at[wslot]),
                pltpu.make_async_copy(wu_ref.at[e], wgu_buf_ref.at[wslot, :, pl.ds(f, f)], wsem.at[wslot]),
                pltpu.make_async_copy(wd_ref.at[e], wd_buf_ref.at[wslot], wsem.at[wslot]))

    def tile_wait(src, dst, sem):
        pltpu.make_async_copy(src.at[pl.ds(0, MOE_TILE), :], dst.at[pl.ds(0, MOE_TILE), :], sem).wait()

    @pl.when(r == 0)
    def _():
        cur_ref[0] = -1
        cur_ref[1] = -1
        for cp in weight_copies(te_ref[0], 0):
            cp.start()
        zero_ref[...] = jnp.zeros_like(zero_ref)
        ybuf_ref[...] = jnp.zeros_like(ybuf_ref)
        n_blocks = tail_ref.shape[0]

        def fill(make):
            def body(g, c):
                make(g)
                return c
            return body

        for blk in range(n_blocks):
            lax.fori_loop(tail_ref[blk], GROUPS_PER_BLOCK, fill(
                lambda g, blk=blk: _group_copy(zero_ref, 0, ys_ref, blk * GROUPS_PER_BLOCK + g, zsem).start()), 0)
        for blk in range(n_blocks):
            lax.fori_loop(tail_ref[blk], GROUPS_PER_BLOCK, fill(
                lambda g, blk=blk: _group_copy(zero_ref, 0, ys_ref, blk * GROUPS_PER_BLOCK + g, zsem).wait()), 0)
        gather(0, 0)

    @pl.when(r < nt)
    def _():
        e = te_ref[r]
        slot = r % 2

        gather(jnp.minimum(r + 1, nt - 1), 1 - slot)

        @pl.when(cur_ref[0] != e)
        def _():
            wslot = (cur_ref[1] + 1) % 2
            for cp in weight_copies(e, wslot):
                cp.wait()
            cur_ref[0] = e
            cur_ref[1] = cur_ref[1] + 1

            @pl.when(nxt_ref[r] >= 0)
            def _():
                for cp in weight_copies(nxt_ref[r], 1 - wslot):
                    cp.start()

        wslot = cur_ref[1] % 2
        tile_wait(xs_ref, xbuf_ref, gsem.at[slot])

        @pl.when(r >= 2)
        def _():
            tile_wait(ybuf_ref, ys_ref, ssem.at[slot])

        d = ybuf_ref.shape[1]
        f = wd_buf_ref.shape[1]
        for part in range(MOE_TILE // MXU_ROWS):
            @pl.when(nreal_ref[r] > part * (MXU_ROWS // GROUP))
            def _(part=part):
                rows = pl.ds(pl.multiple_of(slot * MOE_TILE + part * MXU_ROWS, MXU_ROWS), MXU_ROWS)
                x = xbuf_ref[rows, 0:d]
                gate_parts = xbuf_ref[rows, d:d + LANES].astype(F32)
                gate = gate_parts[:, 0:1] + gate_parts[:, 1:2]
                gu = _dot(x, wgu_buf_ref[wslot])
                hid = (jax.nn.silu(gu[:, 0:f]) * gu[:, f:2 * f]).astype(BF16)
                ybuf_ref[rows, :] = (_dot(hid, wd_buf_ref[wslot]) * gate).astype(ybuf_ref.dtype)

        for i in range(TILE_GROUPS):
            _group_copy(ybuf_ref, slot * TILE_GROUPS + i, ys_ref, gdst_ref[r * TILE_GROUPS + i],
                        ssem.at[slot]).start(priority=1)

        @pl.when(r == nt - 1)
        def _():
            tile_wait(xs_ref, xbuf_ref, gsem.at[1 - slot])
            tile_wait(ybuf_ref, ys_ref, ssem.at[slot])

            @pl.when(r >= 1)
            def _():
                tile_wait(ybuf_ref, ys_ref, ssem.at[1 - slot])


def _experts(plan, xs, w_gate, w_up, w_down):
    d = w_gate.shape[1]
    f = w_gate.shape[2]
    tile_e, next_e, n_real, n_tiles, gsrc, gdst, tail = plan
    r_max = tile_e.shape[0]
    n_blocks = tail.shape[0]
    any_spec = pl.BlockSpec(memory_space=pl.ANY)
    grid_spec = pltpu.PrefetchScalarGridSpec(
        num_scalar_prefetch=7,
        grid=(r_max,),
        in_specs=[any_spec, any_spec, any_spec, any_spec],
        out_specs=any_spec,
        scratch_shapes=[pltpu.VMEM((2 * MOE_TILE, d + LANES), BF16),
                        pltpu.VMEM((2 * MOE_TILE, d), BF16),
                        pltpu.VMEM((GROUP, d), BF16),
                        pltpu.VMEM((2, d, 2 * f), BF16),
                        pltpu.VMEM((2, f, d), BF16),
                        pltpu.SMEM((2,), jnp.int32),
                        pltpu.SemaphoreType.DMA((2,)),
                        pltpu.SemaphoreType.DMA((2,)),
                        pltpu.SemaphoreType.DMA(()),
                        pltpu.SemaphoreType.DMA((2,))],
    )
    return pl.pallas_call(
        _expert_kernel,
        grid_spec=grid_spec,
        out_shape=jax.ShapeDtypeStruct((n_blocks * LOCAL_ROWS, d), BF16),
        compiler_params=_params(1),
        name="experts",
    )(tile_e, next_e, n_real, n_tiles, gsrc, gdst, tail, xs, w_gate, w_up, w_down)


def _combine_kernel(ys_ref, h_ref, cm_ref, g_ref, o_ref):
    tm = h_ref.shape[0]
    cm = cm_ref[...]
    q = lax.broadcasted_iota(jnp.int32, (tm, ys_ref.shape[0]), 1)
    sel = jnp.where((q == cm[:, 0:1].astype(jnp.int32)) | (q == cm[:, 1:2].astype(jnp.int32)), 1.0, 0.0)
    h2 = h_ref[...] + _dot(sel.astype(BF16), ys_ref[...])
    o_ref[...] = _rms_scale(h2, g_ref[...])


def _combine(ys, h1, cmeta, g):
    t, d = h1.shape
    tm = ROUTE_BLOCK
    return pl.pallas_call(
        _combine_kernel,
        grid=(t // tm,),
        in_specs=[pl.BlockSpec((LOCAL_ROWS, d), lambda i: (i, 0)),
                  pl.BlockSpec((tm, d), lambda i: (i, 0)),
                  pl.BlockSpec((tm, LANES), lambda i: (i, 0)),
                  pl.BlockSpec((1, d), lambda i: (0, 0))],
        out_specs=pl.BlockSpec((tm, d), lambda i: (i, 0)),
        out_shape=jax.ShapeDtypeStruct((t, d), F32),
        compiler_params=_params(1),
        name="combine",
    )(ys, h1, cmeta, g.reshape(1, d))


def _expert_plan(counts, n_blocks):
    cnt = counts[:, 0].astype(jnp.int32).reshape(n_blocks, N_EXPERTS)
    groups = (cnt + GROUP - 1) // GROUP
    first = jnp.cumsum(groups, axis=1) - groups
    upto = jnp.cumsum(groups, axis=0)
    per_expert = upto[-1]
    tiles_e = (per_expert + TILE_GROUPS - 1) // TILE_GROUPS
    tile_end = jnp.cumsum(tiles_e)
    n_tiles = tile_end[-1]
    max_groups = 2 * ROUTE_BLOCK * n_blocks // GROUP + n_blocks * N_EXPERTS
    r_max = max_groups // TILE_GROUPS + N_EXPERTS
    tile_ids = jnp.arange(r_max, dtype=jnp.int32)
    tile = jnp.minimum(tile_ids, n_tiles - 1)
    tile_e = jnp.sum((tile_end[None, :] <= tile[:, None]).astype(jnp.int32), axis=1)
    later = (tile_e[None, :] > tile_e[:, None]) & (tile_ids[None, :] < n_tiles)
    next_e = jnp.min(jnp.where(later, tile_e[None, :], N_EXPERTS), axis=1)
    next_e = jnp.where(next_e == N_EXPERTS, -1, next_e)

    slot = jnp.arange(r_max * TILE_GROUPS, dtype=jnp.int32)
    s_tile = slot // TILE_GROUPS
    oh_e = jnp.repeat(tile_e, TILE_GROUPS)[:, None] == jnp.arange(N_EXPERTS, dtype=jnp.int32)[None, :]

    def by_expert(table):
        return jnp.sum(jnp.where(oh_e[:, None, :], table[None], 0), axis=-1)

    k = slot - by_expert(((tile_end - tiles_e) * TILE_GROUPS)[None, :])[:, 0]
    real = (k < by_expert(per_expert[None, :])[:, 0]) & (s_tile < n_tiles)
    upto_e = by_expert(upto)
    blk = jnp.minimum(jnp.sum((upto_e <= k[:, None]).astype(jnp.int32), axis=1), n_blocks - 1)
    oh_b = blk[:, None] == jnp.arange(n_blocks, dtype=jnp.int32)[None, :]

    def by_block(table_se):
        return jnp.sum(jnp.where(oh_b, table_se, 0), axis=1)

    before = by_block(upto_e - by_expert(groups))
    src = blk * GROUPS_PER_BLOCK + by_block(by_expert(first)) + (k - before)
    zero_group = GROUPS_PER_BLOCK - 1
    spare = n_blocks * GROUPS_PER_BLOCK + (s_tile % 2) * TILE_GROUPS + slot % TILE_GROUPS
    gsrc = jnp.where(real, src, zero_group)
    gdst = jnp.where(real, src, spare)
    tail = jnp.concatenate([jnp.sum(groups, axis=1), jnp.zeros((1,), jnp.int32)])
    n_real = jnp.sum(real.reshape(r_max, TILE_GROUPS).astype(jnp.int32), axis=1)
    return tile_e, next_e, n_real, n_tiles.reshape(1), gsrc, gdst, tail


def _layer(h, norm_mix_g, w_in, conv_w, sgu_ln_g, sgu_ln_b, sgu_w_s, sgu_b_s, w_up_conv,
           w_up_sgu, w_out, norm_ffn_g, w_rg, b_rg, w_re, b_re, w_eg, w_eu, w_ed, out_g):
    t, d = h.shape
    conv_width = conv_w.shape[1]
    sgu_width = sgu_ln_g.shape[0]
    xn = _rmsnorm(h, norm_mix_g)
    ya = _conv_branch(xn, w_in, conv_w, conv_width)
    yb = _sgu_branch(xn, w_in, 3 * conv_width, sgu_ln_g, sgu_ln_b, sgu_w_s, sgu_b_s)
    m, eg_bf, eu_bf, ed_bf = _upgate(xn, ya, yb, w_in, 3 * conv_width + 2 * sgu_width,
                                     w_up_conv, w_up_sgu, w_eg, w_eu, w_ed)
    h1 = _outproj(m, w_out, h)
    xs, meta, counts = _route_sort(h1, norm_ffn_g, w_rg, b_rg, w_re, b_re)
    ys = _experts(_expert_plan(counts, t // ROUTE_BLOCK), xs, eg_bf, eu_bf, ed_bf)
    cmeta = jnp.pad(meta[0:2].T, ((0, 0), (0, LANES - 2)))
    return _combine(ys, h1, cmeta, out_g)


def kernel(x, norm_mix_g, w_in, conv_w, sgu_ln_g, sgu_ln_b, sgu_w_s, sgu_b_s, w_up_conv, w_up_sgu, w_out, norm_ffn_g, w_router_group, b_router_group, w_router_expert, b_router_expert, w_exp_gate, w_exp_up, w_exp_down, norm_final_g):
    bsz, s, d = x.shape
    depth = w_in.shape[0]
    assert bsz == 1 and depth == 1, "causal conv carry and the fused final norm assume one sequence, one layer"
    assert s % ROUTE_BLOCK == 0
    out = _layer(x.reshape(s, d), norm_mix_g[0], w_in[0], conv_w[0], sgu_ln_g[0], sgu_ln_b[0],
                 sgu_w_s[0], sgu_b_s[0], w_up_conv[0], w_up_sgu[0], w_out[0], norm_ffn_g[0],
                 w_router_group[0], b_router_group[0], w_router_expert[0], b_router_expert[0],
                 w_exp_gate[0], w_exp_up[0], w_exp_down[0], norm_final_g)
    return out.reshape(bsz, s, d)
```

```python
import functools

import jax
import jax.numpy as jnp
from jax import lax
from jax.experimental import pallas as pl
from jax.experimental.pallas import tpu as pltpu

F32 = jnp.float32
BF16 = jnp.bfloat16

EPS = 1e-6
CHUNK = 64
CONV_K = 3
SGU_HEADS = 8
SGU_BLOCK = 128
N_GROUPS = 4
EXPERTS_PER_GROUP = 4
N_EXPERTS = N_GROUPS * EXPERTS_PER_GROUP
ROUTER_ROWS = 32
LANES = 128

VMEM_LIMIT_BYTES = 56 * 1024 * 1024

ROUTE_BLOCK = 512
GROUP = 16
LOCAL_ROWS = -(-(2 * ROUTE_BLOCK + N_EXPERTS * (GROUP - 1)) // 256) * 256
GROUPS_PER_BLOCK = LOCAL_ROWS // GROUP
MOE_TILE = 512
TILE_GROUPS = MOE_TILE // GROUP
MXU_ROWS = 256


def _params(n_axes):
    return pltpu.CompilerParams(
        dimension_semantics=("arbitrary",) * n_axes,
        vmem_limit_bytes=VMEM_LIMIT_BYTES)


def _dot(a, b):
    return jnp.dot(a, b, preferred_element_type=F32)


def _rms_scale(x, g):
    ms = jnp.mean(x * x, axis=-1, keepdims=True)
    return x * lax.rsqrt(ms + EPS) * g


def _conv_kernel(xn_ref, wb_ref, wc_ref, wh_ref, cw_ref, o_ref, wbf_ref, carry_ref):
    i = pl.program_id(1)
    tn = wb_ref.shape[1]
    tm = xn_ref.shape[0]

    @pl.when(i == 0)
    def _():
        wbf_ref[:, 0:tn] = wb_ref[...].astype(BF16)
        wbf_ref[:, tn:2 * tn] = wc_ref[...].astype(BF16)
        wbf_ref[:, 2 * tn:3 * tn] = wh_ref[...].astype(BF16)
        carry_ref[...] = jnp.zeros_like(carry_ref)

    proj = _dot(xn_ref[...], wbf_ref[...])
    b = proj[:, 0:tn]
    p = proj[:, tn:2 * tn] * proj[:, 2 * tn:3 * tn]
    prev = carry_ref[...]
    carry_ref[...] = p[tm - 8:tm, :]
    row = lax.broadcasted_iota(jnp.int32, p.shape, 0)
    p1 = jnp.where(row == 0, prev[7:8, :], pltpu.roll(p, 1, axis=0))
    p2 = jnp.where(row == 0, prev[6:7, :],
                   jnp.where(row == 1, prev[7:8, :], pltpu.roll(p, 2, axis=0)))
    cw = cw_ref[...]
    y = b * (cw[0:1, :] * p2 + cw[1:2, :] * p1 + cw[2:3, :] * p)
    o_ref[...] = y.astype(o_ref.dtype)


def _conv_branch(xn, w_in, conv_w, width, tm=1024, tn=256):
    t, d = xn.shape
    nj = width // tn
    return pl.pallas_call(
        _conv_kernel,
        grid=(nj, t // tm),
        in_specs=[pl.BlockSpec((tm, d), lambda j, i: (i, 0)),
                  pl.BlockSpec((d, tn), lambda j, i: (0, j)),
                  pl.BlockSpec((d, tn), lambda j, i: (0, nj + j)),
                  pl.BlockSpec((d, tn), lambda j, i: (0, 2 * nj + j)),
                  pl.BlockSpec((CONV_K, tn), lambda j, i: (0, j))],
        out_specs=pl.BlockSpec((tm, tn), lambda j, i: (i, j)),
        out_shape=jax.ShapeDtypeStruct((t, width), BF16),
        scratch_shapes=[pltpu.VMEM((d, 3 * tn), BF16),
                        pltpu.VMEM((8, tn), F32)],
        compiler_params=_params(2),
        name="conv_branch",
    )(xn, w_in, w_in, w_in, conv_w)


def _sgu_kernel(x_ref, g_ref, wu_ref, wv_ref, lng_ref, lnb_ref, ws_ref, bsx_ref, xn_ref, o_ref, wbf_ref):
    tm = x_ref.shape[0]
    w = o_ref.shape[1]
    hd = w // SGU_HEADS

    @pl.when(pl.program_id(0) == 0)
    def _():
        wbf_ref[:, 0:w] = wu_ref[...].astype(BF16)
        wbf_ref[:, w:2 * w] = wv_ref[...].astype(BF16)

    xn = _rms_scale(x_ref[...], g_ref[...]).astype(BF16)
    xn_ref[...] = xn
    gz = jax.nn.gelu(_dot(xn, wbf_ref[...]))
    v = gz[:, w:2 * w]
    mu = jnp.mean(v, axis=-1, keepdims=True)
    vc = v - mu
    var = jnp.mean(vc * vc, axis=-1, keepdims=True)
    vn = (vc * lax.rsqrt(var + EPS) * lng_ref[...] + lnb_ref[...]).astype(BF16)
    ii = lax.broadcasted_iota(jnp.int32, (SGU_BLOCK, SGU_BLOCK), 0)
    jj = lax.broadcasted_iota(jnp.int32, (SGU_BLOCK, SGU_BLOCK), 1)
    mask = (jj // CHUNK) <= (ii // CHUNK)
    for h in range(SGU_HEADS):
        wm = jnp.where(mask, ws_ref[h], 0.0).astype(BF16)
        cs = slice(h * hd, (h + 1) * hd)
        for n in range(tm // SGU_BLOCK):
            rs = slice(n * SGU_BLOCK, (n + 1) * SGU_BLOCK)
            vm = _dot(wm, vn[rs, cs]) + bsx_ref[:, cs]
            o_ref[rs, cs] = (gz[rs, cs] * vm).astype(o_ref.dtype)


def _sgu_branch(x, g, w_in, col0, ln_g, ln_b, w_s, b_s, tm=512):
    t, d = x.shape
    w = ln_g.shape[0]
    hd = w // SGU_HEADS
    assert col0 % w == 0
    c0 = col0 // w
    bsx = jnp.repeat(b_s.T, hd, axis=1)
    once = pl.Buffered(1)
    return pl.pallas_call(
        _sgu_kernel,
        grid=(t // tm,),
        in_specs=[pl.BlockSpec((tm, d), lambda i: (i, 0)),
                  pl.BlockSpec((1, d), lambda i: (0, 0)),
                  pl.BlockSpec((d, w), lambda i: (0, c0), pipeline_mode=once),
                  pl.BlockSpec((d, w), lambda i: (0, c0 + 1), pipeline_mode=once),
                  pl.BlockSpec((1, w), lambda i: (0, 0)),
                  pl.BlockSpec((1, w), lambda i: (0, 0)),
                  pl.BlockSpec((SGU_HEADS, SGU_BLOCK, SGU_BLOCK), lambda i: (0, 0, 0)),
                  pl.BlockSpec((SGU_BLOCK, w), lambda i: (0, 0))],
        out_specs=[pl.BlockSpec((tm, d), lambda i: (i, 0)),
                   pl.BlockSpec((tm, w), lambda i: (i, 0))],
        out_shape=[jax.ShapeDtypeStruct((t, d), BF16),
                   jax.ShapeDtypeStruct((t, w), BF16)],
        scratch_shapes=[pltpu.VMEM((d, 2 * w), BF16)],
        compiler_params=_params(1),
        name="sgu_branch",
    )(x, g.reshape(1, d), w_in, w_in, ln_g.reshape(1, w), ln_b.reshape(1, w), w_s, bsx)


def _upgate_kernel(xn_ref, ya_ref, yb_ref, wgc_ref, wgs_ref, wua_ref, wub_ref, eg_ref, eu_ref, ed_ref,
                   o_ref, eg_bf_ref, eu_bf_ref, ed_bf_ref, wgbf_ref, wuabf_ref, wubbf_ref):
    tn = wgc_ref.shape[1]
    eg_bf_ref[...] = eg_ref[...].astype(BF16)
    eu_bf_ref[...] = eu_ref[...].astype(BF16)
    ed_bf_ref[...] = ed_ref[...].astype(BF16)

    @pl.when(pl.program_id(1) == 0)
    def _():
        wgbf_ref[:, 0:tn] = wgc_ref[...].astype(BF16)
        wgbf_ref[:, tn:2 * tn] = wgs_ref[...].astype(BF16)
        wuabf_ref[...] = wua_ref[...].astype(BF16)
        wubbf_ref[...] = wub_ref[...].astype(BF16)

    gl = _dot(xn_ref[...], wgbf_ref[...])
    a = _dot(ya_ref[...], wuabf_ref[...])
    b = _dot(yb_ref[...], wubbf_ref[...])
    m = jax.nn.sigmoid(gl[:, 0:tn]) * a + jax.nn.sigmoid(gl[:, tn:2 * tn]) * b
    o_ref[...] = m.astype(o_ref.dtype)


def _upgate(xn, ya, yb, w_in, gate_col0, w_up_a, w_up_b, w_eg, w_eu, w_ed, tm=512, tn=512):
    t, d = xn.shape
    wa = ya.shape[1]
    wb = yb.shape[1]
    dout = w_up_a.shape[1]
    c0 = gate_col0 // tn
    nj = dout // tn
    ni = t // tm
    n_e, d_e, f_e = w_eg.shape
    up_rows = n_e * d_e // (nj * ni)
    down_rows = n_e * f_e // (nj * ni)
    assert up_rows * nj * ni == n_e * d_e and up_rows % 16 == 0
    assert down_rows * nj * ni == n_e * f_e and down_rows % 16 == 0
    up_spec = pl.BlockSpec((up_rows, f_e), lambda j, i: (j * ni + i, 0))
    down_spec = pl.BlockSpec((down_rows, d_e), lambda j, i: (j * ni + i, 0))
    m, eg_bf, eu_bf, ed_bf = pl.pallas_call(
        _upgate_kernel,
        grid=(nj, ni),
        in_specs=[pl.BlockSpec((tm, d), lambda j, i: (i, 0)),
                  pl.BlockSpec((tm, wa), lambda j, i: (i, 0)),
                  pl.BlockSpec((tm, wb), lambda j, i: (i, 0)),
                  pl.BlockSpec((d, tn), lambda j, i: (0, c0 + j)),
                  pl.BlockSpec((d, tn), lambda j, i: (0, c0 + nj + j)),
                  pl.BlockSpec((wa, tn), lambda j, i: (0, j)),
                  pl.BlockSpec((wb, tn), lambda j, i: (0, j)),
                  up_spec, up_spec, down_spec],
        out_specs=[pl.BlockSpec((tm, tn), lambda j, i: (i, j)), up_spec, up_spec, down_spec],
        out_shape=[jax.ShapeDtypeStruct((t, dout), BF16),
                   jax.ShapeDtypeStruct((n_e * d_e, f_e), BF16),
                   jax.ShapeDtypeStruct((n_e * d_e, f_e), BF16),
                   jax.ShapeDtypeStruct((n_e * f_e, d_e), BF16)],
        scratch_shapes=[pltpu.VMEM((d, 2 * tn), BF16),
                        pltpu.VMEM((wa, tn), BF16),
                        pltpu.VMEM((wb, tn), BF16)],
        compiler_params=_params(2),
        name="upgate",
    )(xn, ya, yb, w_in, w_in, w_up_a, w_up_b,
      w_eg.reshape(n_e * d_e, f_e), w_eu.reshape(n_e * d_e, f_e), w_ed.reshape(n_e * f_e, d_e))
    return (m, eg_bf.reshape(n_e, d_e, f_e), eu_bf.reshape(n_e, d_e, f_e),
            ed_bf.reshape(n_e, f_e, d_e))


def _outproj_kernel(m_ref, w_ref, x_ref, o_ref, wbf_ref):
    @pl.when(pl.program_id(1) == 0)
    def _():
        wbf_ref[...] = w_ref[...].astype(BF16)

    o_ref[...] = x_ref[...] + _dot(m_ref[...], wbf_ref[...])


def _outproj(m, w_out, x, tm=512, tn=1024):
    t, d = m.shape
    dout = w_out.shape[1]
    return pl.pallas_call(
        _outproj_kernel,
        grid=(dout // tn, t // tm),
        in_specs=[pl.BlockSpec((tm, d), lambda j, i: (i, 0)),
                  pl.BlockSpec((d, tn), lambda j, i: (0, j)),
                  pl.BlockSpec((tm, tn), lambda j, i: (i, j))],
        out_specs=pl.BlockSpec((tm, tn), lambda j, i: (i, j)),
        out_shape=jax.ShapeDtypeStruct((t, dout), F32),
        scratch_shapes=[pltpu.VMEM((d, tn), BF16)],
        compiler_params=_params(2),
        name="outproj",
    )(m, w_out, x)


def _argmax_rows(rows):
    best = rows[0]
    idx = jnp.zeros(rows[0].shape, jnp.int32)
    for k in range(1, len(rows)):
        better = rows[k] > best
        best = jnp.where(better, rows[k], best)
        idx = jnp.where(better, k, idx)
    return best, idx


def _softmax_rows(rows):
    mx = functools.reduce(jnp.maximum, rows)
    ex = [jnp.exp(r - mx) for r in rows]
    den = functools.reduce(lambda a, b: a + b, ex)
    return [e / den for e in ex]


def _route_sort_kernel(h_ref, g_ref, wr_ref, br_ref, xs_ref, meta_ref, cnt_ref):
    tm = h_ref.shape[0]
    xn = _rms_scale(h_ref[...], g_ref[...])
    xn_hi = xn.astype(BF16)
    xn_lo = (xn - xn_hi.astype(F32)).astype(BF16)
    wr = wr_ref[...]
    wr_hi = wr.astype(BF16)
    wr_lo = (wr - wr_hi.astype(F32)).astype(BF16)
    nt_dims = (((1,), (1,)), ((), ()))
    lt = (lax.dot_general(wr_hi, xn_hi, nt_dims, preferred_element_type=F32)
          + lax.dot_general(wr_hi, xn_lo, nt_dims, preferred_element_type=F32)
          + lax.dot_general(wr_lo, xn_hi, nt_dims, preferred_element_type=F32)) + br_ref[...]
    pgs = _softmax_rows([lt[k:k + 1, :] for k in range(N_GROUPS)])
    pg, gi = _argmax_rows(pgs)
    sel = []
    for k in range(EXPERTS_PER_GROUP):
        r = jnp.zeros_like(pg)
        for g in range(N_GROUPS):
            row = N_GROUPS + g * EXPERTS_PER_GROUP + k
            r = jnp.where(gi == g, lt[row:row + 1, :], r)
        sel.append(r)
    pes = _softmax_rows(sel)
    p1, e1 = _argmax_rows(pes)
    rest = [jnp.where(e1 == k, -1.0, pes[k]) for k in range(EXPERTS_PER_GROUP)]
    p2, e2 = _argmax_rows(rest)
    den = p1 + p2
    w1 = pg * (p1 / den)
    w2 = pg * (p2 / den)
    lo = jnp.minimum(e1, e2)
    hi = jnp.maximum(e1, e2)
    w_lo = jnp.where(e1 < e2, w1, w2)
    w_hi = jnp.where(e1 < e2, w2, w1)
    ea = gi * EXPERTS_PER_GROUP + lo
    eb = gi * EXPERTS_PER_GROUP + hi

    erow = lax.broadcasted_iota(jnp.int32, (N_EXPERTS, tm), 0)
    oh_a = (erow == ea).astype(F32)
    oh_b = (erow == eb).astype(F32)
    a = lax.broadcasted_iota(jnp.int32, (tm, tm), 0)
    b = lax.broadcasted_iota(jnp.int32, (tm, tm), 1)
    before = (a < b).astype(BF16)
    cum = _dot((oh_a + oh_b).astype(BF16), before)
    cnt = jnp.sum(oh_a + oh_b, axis=1, keepdims=True)
    padded = jnp.floor((cnt + (GROUP - 1)) * (1.0 / GROUP)) * GROUP
    pos_a = jnp.sum(oh_a * cum + jnp.where(erow < ea, padded, 0.0), axis=0, keepdims=True)
    pos_b = jnp.sum(oh_b * cum + jnp.where(erow < eb, padded, 0.0), axis=0, keepdims=True)

    d = h_ref.shape[1]
    q = lax.broadcasted_iota(jnp.int32, (xs_ref.shape[0], tm), 0)
    perm_a = jnp.where(q == pos_a.astype(jnp.int32), 1.0, 0.0).astype(BF16)
    perm_b = jnp.where(q == pos_b.astype(jnp.int32), 1.0, 0.0).astype(BF16)
    xs_ref[:, 0:d] = _dot(perm_a + perm_b, xn_hi).astype(xs_ref.dtype)

    def gate_rows(w):
        hi = w.astype(BF16).astype(F32)
        lo = w - hi
        k = lax.broadcasted_iota(jnp.int32, (LANES, tm), 0)
        return jnp.where(k == 0, hi, jnp.where(k == 1, lo, 0.0)).astype(BF16)

    gates = (lax.dot_general(perm_a, gate_rows(w_lo), nt_dims, preferred_element_type=F32)
             + lax.dot_general(perm_b, gate_rows(w_hi), nt_dims, preferred_element_type=F32))
    xs_ref[:, d:d + LANES] = gates.astype(xs_ref.dtype)

    cnt_ref[...] = jnp.broadcast_to(cnt, cnt_ref.shape)
    meta_ref[0:1, :] = pos_a
    meta_ref[1:2, :] = pos_b
    meta_ref[2:8, :] = jnp.zeros((6, tm), F32)


def _route_sort(h1, g, w_rg, b_rg, w_re, b_re):
    t, d = h1.shape
    tm = ROUTE_BLOCK
    nb = t // tm
    n_log = w_rg.shape[1] + w_re.shape[1]
    wr = jnp.concatenate([w_rg, w_re], axis=1).T
    wr = jnp.pad(wr, ((0, ROUTER_ROWS - n_log), (0, 0)))
    br = jnp.pad(jnp.concatenate([b_rg, b_re]), (0, ROUTER_ROWS - n_log)).reshape(ROUTER_ROWS, 1)
    return pl.pallas_call(
        _route_sort_kernel,
        grid=(nb,),
        in_specs=[pl.BlockSpec((tm, d), lambda i: (i, 0)),
                  pl.BlockSpec((1, d), lambda i: (0, 0)),
                  pl.BlockSpec((ROUTER_ROWS, d), lambda i: (0, 0)),
                  pl.BlockSpec((ROUTER_ROWS, 1), lambda i: (0, 0))],
        out_specs=[pl.BlockSpec((LOCAL_ROWS, d + LANES), lambda i: (i, 0)),
                   pl.BlockSpec((8, tm), lambda i: (0, i)),
                   pl.BlockSpec((N_EXPERTS, LANES), lambda i: (i, 0))],
        out_shape=[jax.ShapeDtypeStruct((nb * LOCAL_ROWS, d + LANES), BF16),
                   jax.ShapeDtypeStruct((8, t), F32),
                   jax.ShapeDtypeStruct((nb * N_EXPERTS, LANES), F32)],
        compiler_params=_params(1),
        name="route_sort",
    )(h1, g.reshape(1, d), wr, br)


def _group_copy(src_ref, s_group, dst_ref, d_group, sem):
    return pltpu.make_async_copy(src_ref.at[pl.ds(pl.multiple_of(s_group * GROUP, GROUP), GROUP), :],
                                 dst_ref.at[pl.ds(pl.multiple_of(d_group * GROUP, GROUP), GROUP), :], sem)


def _expert_kernel(te_ref, nxt_ref, nreal_ref, nt_ref, gsrc_ref, gdst_ref, tail_ref,
                   xs_ref, wg_ref, wu_ref, wd_ref, ys_ref,
                   xbuf_ref, ybuf_ref, zero_ref, wgu_buf_ref, wd_buf_ref,
                   cur_ref, gsem, ssem, zsem, wsem):
    r = pl.program_id(0)
    nt = nt_ref[0]

    def gather(q, slot):
        for i in range(TILE_GROUPS):
            _group_copy(xs_ref, gsrc_ref[q * TILE_GROUPS + i], xbuf_ref, slot * TILE_GROUPS + i,
                        gsem.at[slot]).start(priority=1)

    def weight_copies(e, wslot):
        f = wg_ref.shape[2]
        return (pltpu.make_async_copy(wg_ref.at[e], wgu_buf_ref.at[wslot, :, pl.ds(0, f)], wsem.at[wslot]),
                pltpu.make_async_copy(wu_ref.at[e], wgu_buf_ref.at[wslot, :, pl.ds(f, f)], wsem.at[wslot]),
                pltpu.make_async_copy(wd_ref.at[e], wd_buf_ref.at[wslot], wsem.at[wslot]))

    def tile_wait(src, dst, sem):
        pltpu.make_async_copy(src.at[pl.ds(0, MOE_TILE), :], dst.at[pl.ds(0, MOE_TILE), :], sem).wait()

    @pl.when(r == 0)
    def _():
        cur_ref[0] = -1
        cur_ref[1] = -1
        for cp in weight_copies(te_ref[0], 0):
            cp.start()
        zero_ref[...] = jnp.zeros_like(zero_ref)
        ybuf_ref[...] = jnp.zeros_like(ybuf_ref)
        n_blocks = tail_ref.shape[0]

        def fill(make):
            def body(g, c):
                make(g)
                return c
            return body

        for blk in range(n_blocks):
            lax.fori_loop(tail_ref[blk], GROUPS_PER_BLOCK, fill(
                lambda g, blk=blk: _group_copy(zero_ref, 0, ys_ref, blk * GROUPS_PER_BLOCK + g, zsem).start()), 0)
        for blk in range(n_blocks):
            lax.fori_loop(tail_ref[blk], GROUPS_PER_BLOCK, fill(
                lambda g, blk=blk: _group_copy(zero_ref, 0, ys_ref, blk * GROUPS_PER_BLOCK + g, zsem).wait()), 0)
        gather(0, 0)

    @pl.when(r < nt)
    def _():
        e = te_ref[r]
        slot = r % 2

        gather(jnp.minimum(r + 1, nt - 1), 1 - slot)

        @pl.when(cur_ref[0] != e)
        def _():
            wslot = (cur_ref[1] + 1) % 2
            for cp in weight_copies(e, wslot):
                cp.wait()
            cur_ref[0] = e
            cur_ref[1] = cur_ref[1] + 1

            @pl.when(nxt_ref[r] >= 0)
            def _():
                for cp in weight_copies(nxt_ref[r], 1 - wslot):
                    cp.start()

        wslot = cur_ref[1] % 2
        tile_wait(xs_ref, xbuf_ref, gsem.at[slot])

        @pl.when(r >= 2)
        def _():
            tile_wait(ybuf_ref, ys_ref, ssem.at[slot])

        d = ybuf_ref.shape[1]
        f = wd_buf_ref.shape[1]
        for part in range(MOE_TILE // MXU_ROWS):
            @pl.when(nreal_ref[r] > part * (MXU_ROWS // GROUP))
            def _(part=part):
                rows = pl.ds(pl.multiple_of(slot * MOE_TILE + part * MXU_ROWS, MXU_ROWS), MXU_ROWS)
                x = xbuf_ref[rows, 0:d]
                gate_parts = xbuf_ref[rows, d:d + LANES].astype(F32)
                gate = gate_parts[:, 0:1] + gate_parts[:, 1:2]
                gu = _dot(x, wgu_buf_ref[wslot])
                hid = (jax.nn.silu(gu[:, 0:f]) * gu[:, f:2 * f]).astype(BF16)
                ybuf_ref[rows, :] = (_dot(hid, wd_buf_ref[wslot]) * gate).astype(ybuf_ref.dtype)

        for i in range(TILE_GROUPS):
            _group_copy(ybuf_ref, slot * TILE_GROUPS + i, ys_ref, gdst_ref[r * TILE_GROUPS + i],
                        ssem.at[slot]).start(priority=1)

        @pl.when(r == nt - 1)
        def _():
            tile_wait(xs_ref, xbuf_ref, gsem.at[1 - slot])
            tile_wait(ybuf_ref, ys_ref, ssem.at[slot])

            @pl.when(r >= 1)
            def _():
                tile_wait(ybuf_ref, ys_ref, ssem.at[1 - slot])


def _experts(plan, xs, w_gate, w_up, w_down):
    d = w_gate.shape[1]
    f = w_gate.shape[2]
    tile_e, next_e, n_real, n_tiles, gsrc, gdst, tail = plan
    r_max = tile_e.shape[0]
    n_blocks = tail.shape[0]
    any_spec = pl.BlockSpec(memory_space=pl.ANY)
    grid_spec = pltpu.PrefetchScalarGridSpec(
        num_scalar_prefetch=7,
        grid=(r_max,),
        in_specs=[any_spec, any_spec, any_spec, any_spec],
        out_specs=any_spec,
        scratch_shapes=[pltpu.VMEM((2 * MOE_TILE, d + LANES), BF16),
                        pltpu.VMEM((2 * MOE_TILE, d), BF16),
                        pltpu.VMEM((GROUP, d), BF16),
                        pltpu.VMEM((2, d, 2 * f), BF16),
                        pltpu.VMEM((2, f, d), BF16),
                        pltpu.SMEM((2,), jnp.int32),
                        pltpu.SemaphoreType.DMA((2,)),
                        pltpu.SemaphoreType.DMA((2,)),
                        pltpu.SemaphoreType.DMA(()),
                        pltpu.SemaphoreType.DMA((2,))],
    )
    return pl.pallas_call(
        _expert_kernel,
        grid_spec=grid_spec,
        out_shape=jax.ShapeDtypeStruct((n_blocks * LOCAL_ROWS, d), BF16),
        compiler_params=_params(1),
        name="experts",
    )(tile_e, next_e, n_real, n_tiles, gsrc, gdst, tail, xs, w_gate, w_up, w_down)


def _combine_kernel(ys_ref, h_ref, cm_ref, g_ref, o_ref):
    tm = h_ref.shape[0]
    cm = cm_ref[...]
    q = lax.broadcasted_iota(jnp.int32, (tm, ys_ref.shape[0]), 1)
    sel = jnp.where((q == cm[:, 0:1].astype(jnp.int32)) | (q == cm[:, 1:2].astype(jnp.int32)), 1.0, 0.0)
    h2 = h_ref[...] + _dot(sel.astype(BF16), ys_ref[...])
    o_ref[...] = _rms_scale(h2, g_ref[...])


def _combine(ys, h1, cmeta, g):
    t, d = h1.shape
    tm = ROUTE_BLOCK
    return pl.pallas_call(
        _combine_kernel,
        grid=(t // tm,),
        in_specs=[pl.BlockSpec((LOCAL_ROWS, d), lambda i: (i, 0)),
                  pl.BlockSpec((tm, d), lambda i: (i, 0)),
                  pl.BlockSpec((tm, LANES), lambda i: (i, 0)),
                  pl.BlockSpec((1, d), lambda i: (0, 0))],
        out_specs=pl.BlockSpec((tm, d), lambda i: (i, 0)),
        out_shape=jax.ShapeDtypeStruct((t, d), F32),
        compiler_params=_params(1),
        name="combine",
    )(ys, h1, cmeta, g.reshape(1, d))


def _expert_plan(counts, n_blocks):
    cnt = counts[:, 0].astype(jnp.int32).reshape(n_blocks, N_EXPERTS)
    groups = (cnt + GROUP - 1) // GROUP
    first = jnp.cumsum(groups, axis=1) - groups
    upto = jnp.cumsum(groups, axis=0)
    per_expert = upto[-1]
    tiles_e = (per_expert + TILE_GROUPS - 1) // TILE_GROUPS
    tile_end = jnp.cumsum(tiles_e)
    n_tiles = tile_end[-1]
    max_groups = 2 * ROUTE_BLOCK * n_blocks // GROUP + n_blocks * N_EXPERTS
    r_max = max_groups // TILE_GROUPS + N_EXPERTS
    tile_ids = jnp.arange(r_max, dtype=jnp.int32)
    tile = jnp.minimum(tile_ids, n_tiles - 1)
    tile_e = jnp.sum((tile_end[None, :] <= tile[:, None]).astype(jnp.int32), axis=1)
    later = (tile_e[None, :] > tile_e[:, None]) & (tile_ids[None, :] < n_tiles)
    next_e = jnp.min(jnp.where(later, tile_e[None, :], N_EXPERTS), axis=1)
    next_e = jnp.where(next_e == N_EXPERTS, -1, next_e)

    slot = jnp.arange(r_max * TILE_GROUPS, dtype=jnp.int32)
    s_tile = slot // TILE_GROUPS
    oh_e = jnp.repeat(tile_e, TILE_GROUPS)[:, None] == jnp.arange(N_EXPERTS, dtype=jnp.int32)[None, :]

    def by_expert(table):
        return jnp.sum(jnp.where(oh_e[:, None, :], table[None], 0), axis=-1)

    k = slot - by_expert(((tile_end - tiles_e) * TILE_GROUPS)[None, :])[:, 0]
    real = (k < by_expert(per_expert[None, :])[:, 0]) & (s_tile < n_tiles)
    upto_e = by_expert(upto)
    blk = jnp.minimum(jnp.sum((upto_e <= k[:, None]).astype(jnp.int32), axis=1), n_blocks - 1)
    oh_b = blk[:, None] == jnp.arange(n_blocks, dtype=jnp.int32)[None, :]

    def by_block(table_se):
        return jnp.sum(jnp.where(oh_b, table_se, 0), axis=1)

    before = by_block(upto_e - by_expert(groups))
    src = blk * GROUPS_PER_BLOCK + by_block(by_expert(first)) + (k - before)
    zero_group = GROUPS_PER_BLOCK - 1
    spare = n_blocks * GROUPS_PER_BLOCK + (s_tile % 2) * TILE_GROUPS + slot % TILE_GROUPS
    gsrc = jnp.where(real, src, zero_group)
    gdst = jnp.where(real, src, spare)
    tail = jnp.concatenate([jnp.sum(groups, axis=1), jnp.zeros((1,), jnp.int32)])
    n_real = jnp.sum(real.reshape(r_max, TILE_GROUPS).astype(jnp.int32), axis=1)
    return tile_e, next_e, n_real, n_tiles.reshape(1), gsrc, gdst, tail


def _layer(h, norm_mix_g, w_in, conv_w, sgu_ln_g, sgu_ln_b, sgu_w_s, sgu_b_s, w_up_conv,
           w_up_sgu, w_out, norm_ffn_g, w_rg, b_rg, w_re, b_re, w_eg, w_eu, w_ed, out_g):
    t, d = h.shape
    conv_width = conv_w.shape[1]
    sgu_width = sgu_ln_g.shape[0]
    xn, yb = _sgu_branch(h, norm_mix_g, w_in, 3 * conv_width, sgu_ln_g, sgu_ln_b, sgu_w_s, sgu_b_s)
    ya = _conv_branch(xn, w_in, conv_w, conv_width)
    m, eg_bf, eu_bf, ed_bf = _upgate(xn, ya, yb, w_in, 3 * conv_width + 2 * sgu_width,
                                     w_up_conv, w_up_sgu, w_eg, w_eu, w_ed)
    h1 = _outproj(m, w_out, h)
    xs, meta, counts = _route_sort(h1, norm_ffn_g, w_rg, b_rg, w_re, b_re)
    ys = _experts(_expert_plan(counts, t // ROUTE_BLOCK), xs, eg_bf, eu_bf, ed_bf)
    cmeta = jnp.pad(meta[0:2].T, ((0, 0), (0, LANES - 2)))
    return _combine(ys, h1, cmeta, out_g)


def kernel(x, norm_mix_g, w_in, conv_w, sgu_ln_g, sgu_ln_b, sgu_w_s, sgu_b_s, w_up_conv, w_up_sgu, w_out, norm_ffn_g, w_router_group, b_router_group, w_router_expert, b_router_expert, w_exp_gate, w_exp_up, w_exp_down, norm_final_g):
    bsz, s, d = x.shape
    depth = w_in.shape[0]
    assert bsz == 1 and depth == 1, "causal conv carry and the fused final norm assume one sequence, one layer"
    assert s % ROUTE_BLOCK == 0
    out = _layer(x.reshape(s, d), norm_mix_g[0], w_in[0], conv_w[0], sgu_ln_g[0], sgu_ln_b[0],
                 sgu_w_s[0], sgu_b_s[0], w_up_conv[0], w_up_sgu[0], w_out[0], norm_ffn_g[0],
                 w_router_group[0], b_router_group[0], w_router_expert[0], b_router_expert[0],
                 w_exp_gate[0], w_exp_up[0], w_exp_down[0], norm_final_g)
    return out.reshape(bsz, s, d)
```

```python
import functools

import jax
import jax.numpy as jnp
from jax import lax
from jax.experimental import pallas as pl
from jax.experimental.pallas import tpu as pltpu

F32 = jnp.float32
BF16 = jnp.bfloat16

EPS = 1e-6
CHUNK = 64
CONV_K = 3
SGU_HEADS = 8
SGU_BLOCK = 128
N_GROUPS = 4
EXPERTS_PER_GROUP = 4
N_EXPERTS = N_GROUPS * EXPERTS_PER_GROUP
ROUTER_ROWS = 32
LANES = 128

VMEM_LIMIT_BYTES = 56 * 1024 * 1024

ROUTE_BLOCK = 512
GROUP = 16
LOCAL_ROWS = -(-(2 * ROUTE_BLOCK + N_EXPERTS * (GROUP - 1)) // 256) * 256
GROUPS_PER_BLOCK = LOCAL_ROWS // GROUP
MOE_TILE = 512
TILE_GROUPS = MOE_TILE // GROUP
MXU_ROWS = 256


def _params(n_axes):
    return pltpu.CompilerParams(
        dimension_semantics=("arbitrary",) * n_axes,
        vmem_limit_bytes=VMEM_LIMIT_BYTES)


def _dot(a, b):
    return jnp.dot(a, b, preferred_element_type=F32)


def _rms_scale(x, g):
    ms = jnp.mean(x * x, axis=-1, keepdims=True)
    return x * lax.rsqrt(ms + EPS) * g


def _conv_kernel(cast_chunks, xn_ref, wb_ref, wc_ref, wh_ref, cw_ref, *rest):
    n_in = sum(cast_chunks)
    cast_in, o_ref = rest[:n_in], rest[n_in]
    cast_out = rest[n_in + 1:n_in + 1 + len(cast_chunks)]
    wbf_ref, carry_ref = rest[n_in + 1 + len(cast_chunks):]
    i = pl.program_id(1)
    tn = wb_ref.shape[1]
    tm = xn_ref.shape[0]
    src = iter(cast_in)
    for dst, n_chunks in zip(cast_out, cast_chunks):
        wc = dst.shape[1] // n_chunks
        for k in range(n_chunks):
            dst[:, k * wc:(k + 1) * wc] = next(src)[...].astype(BF16)

    @pl.when(i == 0)
    def _():
        wbf_ref[:, 0:tn] = wb_ref[...].astype(BF16)
        wbf_ref[:, tn:2 * tn] = wc_ref[...].astype(BF16)
        wbf_ref[:, 2 * tn:3 * tn] = wh_ref[...].astype(BF16)
        carry_ref[...] = jnp.zeros_like(carry_ref)

    proj = _dot(xn_ref[...], wbf_ref[...])
    b = proj[:, 0:tn]
    p = proj[:, tn:2 * tn] * proj[:, 2 * tn:3 * tn]
    prev = carry_ref[...]
    carry_ref[...] = p[tm - 8:tm, :]
    row = lax.broadcasted_iota(jnp.int32, p.shape, 0)
    p1 = jnp.where(row == 0, prev[7:8, :], pltpu.roll(p, 1, axis=0))
    p2 = jnp.where(row == 0, prev[6:7, :],
                   jnp.where(row == 1, prev[7:8, :], pltpu.roll(p, 2, axis=0)))
    cw = cw_ref[...]
    y = b * (cw[0:1, :] * p2 + cw[1:2, :] * p1 + cw[2:3, :] * p)
    o_ref[...] = y.astype(o_ref.dtype)


def _conv_branch(xn, w_in, conv_w, width, to_cast, tm=1024, tn=256):
    t, d = xn.shape
    nj = width // tn
    ni = t // tm
    steps = nj * ni
    cast_args, cast_in_specs, cast_out_specs, cast_shapes, cast_chunks = [], [], [], [], []
    for arr, col0, n_cols, chunk in to_cast:
        rows = arr.shape[0] // steps
        assert rows * steps == arr.shape[0] and rows % 16 == 0
        assert col0 % chunk == 0 and n_cols % chunk == 0
        for k in range(n_cols // chunk):
            cast_args.append(arr)
            cast_in_specs.append(
                pl.BlockSpec((rows, chunk), lambda j, i, c=col0 // chunk + k: (j * ni + i, c)))
        cast_chunks.append(n_cols // chunk)
        cast_out_specs.append(pl.BlockSpec((rows, n_cols), lambda j, i: (j * ni + i, 0)))
        cast_shapes.append(jax.ShapeDtypeStruct((arr.shape[0], n_cols), BF16))
    outs = pl.pallas_call(
        functools.partial(_conv_kernel, tuple(cast_chunks)),
        grid=(nj, ni),
        in_specs=[pl.BlockSpec((tm, d), lambda j, i: (i, 0)),
                  pl.BlockSpec((d, tn), lambda j, i: (0, j)),
                  pl.BlockSpec((d, tn), lambda j, i: (0, nj + j)),
                  pl.BlockSpec((d, tn), lambda j, i: (0, 2 * nj + j)),
                  pl.BlockSpec((CONV_K, tn), lambda j, i: (0, j))] + cast_in_specs,
        out_specs=[pl.BlockSpec((tm, tn), lambda j, i: (i, j))] + cast_out_specs,
        out_shape=[jax.ShapeDtypeStruct((t, width), BF16)] + cast_shapes,
        scratch_shapes=[pltpu.VMEM((d, 3 * tn), BF16),
                        pltpu.VMEM((8, tn), F32)],
        compiler_params=_params(2),
        name="conv_branch",
    )(xn, w_in, w_in, w_in, conv_w, *cast_args)
    return outs[0], outs[1:]


def _sgu_kernel(x_ref, g_ref, wu_ref, wv_ref, lng_ref, lnb_ref, ws_ref, bsx_ref, xn_ref, o_ref, wbf_ref):
    tm = x_ref.shape[0]
    w = o_ref.shape[1]
    hd = w // SGU_HEADS

    @pl.when(pl.program_id(0) == 0)
    def _():
        wbf_ref[:, 0:w] = wu_ref[...].astype(BF16)
        wbf_ref[:, w:2 * w] = wv_ref[...].astype(BF16)

    xn = _rms_scale(x_ref[...], g_ref[...]).astype(BF16)
    xn_ref[...] = xn
    gz = jax.nn.gelu(_dot(xn, wbf_ref[...]))
    v = gz[:, w:2 * w]
    mu = jnp.mean(v, axis=-1, keepdims=True)
    vc = v - mu
    var = jnp.mean(vc * vc, axis=-1, keepdims=True)
    vn = (vc * lax.rsqrt(var + EPS) * lng_ref[...] + lnb_ref[...]).astype(BF16)
    ii = lax.broadcasted_iota(jnp.int32, (SGU_BLOCK, SGU_BLOCK), 0)
    jj = lax.broadcasted_iota(jnp.int32, (SGU_BLOCK, SGU_BLOCK), 1)
    mask = (jj // CHUNK) <= (ii // CHUNK)
    for h in range(SGU_HEADS):
        wm = jnp.where(mask, ws_ref[h], 0.0).astype(BF16)
        cs = slice(h * hd, (h + 1) * hd)
        for n in range(tm // SGU_BLOCK):
            rs = slice(n * SGU_BLOCK, (n + 1) * SGU_BLOCK)
            vm = _dot(wm, vn[rs, cs]) + bsx_ref[:, cs]
            o_ref[rs, cs] = (gz[rs, cs] * vm).astype(o_ref.dtype)


def _sgu_branch(x, g, w_in, col0, ln_g, ln_b, w_s, b_s, tm=512):
    t, d = x.shape
    w = ln_g.shape[0]
    hd = w // SGU_HEADS
    assert col0 % w == 0
    c0 = col0 // w
    bsx = jnp.repeat(b_s.T, hd, axis=1)
    once = pl.Buffered(1)
    return pl.pallas_call(
        _sgu_kernel,
        grid=(t // tm,),
        in_specs=[pl.BlockSpec((tm, d), lambda i: (i, 0)),
                  pl.BlockSpec((1, d), lambda i: (0, 0)),
                  pl.BlockSpec((d, w), lambda i: (0, c0), pipeline_mode=once),
                  pl.BlockSpec((d, w), lambda i: (0, c0 + 1), pipeline_mode=once),
                  pl.BlockSpec((1, w), lambda i: (0, 0)),
                  pl.BlockSpec((1, w), lambda i: (0, 0)),
                  pl.BlockSpec((SGU_HEADS, SGU_BLOCK, SGU_BLOCK), lambda i: (0, 0, 0)),
                  pl.BlockSpec((SGU_BLOCK, w), lambda i: (0, 0))],
        out_specs=[pl.BlockSpec((tm, d), lambda i: (i, 0)),
                   pl.BlockSpec((tm, w), lambda i: (i, 0))],
        out_shape=[jax.ShapeDtypeStruct((t, d), BF16),
                   jax.ShapeDtypeStruct((t, w), BF16)],
        scratch_shapes=[pltpu.VMEM((d, 2 * w), BF16)],
        compiler_params=_params(1),
        name="sgu_branch",
    )(x, g.reshape(1, d), w_in, w_in, ln_g.reshape(1, w), ln_b.reshape(1, w), w_s, bsx)


def _upgate_kernel(xn_ref, ya_ref, yb_ref, wgc_ref, wgs_ref, wua_ref, wub_ref, eg_ref, eu_ref, ed_ref,
                   o_ref, eg_bf_ref, eu_bf_ref, ed_bf_ref):
    eg_bf_ref[...] = eg_ref[...].astype(BF16)
    eu_bf_ref[...] = eu_ref[...].astype(BF16)
    ed_bf_ref[...] = ed_ref[...].astype(BF16)

    xn = xn_ref[...]
    m = (jax.nn.sigmoid(_dot(xn, wgc_ref[...])) * _dot(ya_ref[...], wua_ref[...])
         + jax.nn.sigmoid(_dot(xn, wgs_ref[...])) * _dot(yb_ref[...], wub_ref[...]))
    o_ref[...] = m.astype(o_ref.dtype)


def _upgate(xn, ya, yb, w_gates, w_up_a, w_up_b, w_eg, w_eu, w_ed, tm=1024, tn=512):
    t, d = xn.shape
    wa = ya.shape[1]
    wb = yb.shape[1]
    dout = w_up_a.shape[1]
    c0 = 0
    nj = dout // tn
    ni = t // tm
    n_e, d_e, f_e = w_eg.shape
    up_rows = n_e * d_e // (nj * ni)
    down_rows = n_e * f_e // (nj * ni)
    assert up_rows * nj * ni == n_e * d_e and up_rows % 16 == 0
    assert down_rows * nj * ni == n_e * f_e and down_rows % 16 == 0
    up_spec = pl.BlockSpec((up_rows, f_e), lambda j, i: (j * ni + i, 0))
    down_spec = pl.BlockSpec((down_rows, d_e), lambda j, i: (j * ni + i, 0))
    m, eg_bf, eu_bf, ed_bf = pl.pallas_call(
        _upgate_kernel,
        grid=(nj, ni),
        in_specs=[pl.BlockSpec((tm, d), lambda j, i: (i, 0)),
                  pl.BlockSpec((tm, wa), lambda j, i: (i, 0)),
                  pl.BlockSpec((tm, wb), lambda j, i: (i, 0)),
                  pl.BlockSpec((d, tn), lambda j, i: (0, c0 + j)),
                  pl.BlockSpec((d, tn), lambda j, i: (0, c0 + nj + j)),
                  pl.BlockSpec((wa, tn), lambda j, i: (0, j)),
                  pl.BlockSpec((wb, tn), lambda j, i: (0, j)),
                  up_spec, up_spec, down_spec],
        out_specs=[pl.BlockSpec((tm, tn), lambda j, i: (i, j)), up_spec, up_spec, down_spec],
        out_shape=[jax.ShapeDtypeStruct((t, dout), BF16),
                   jax.ShapeDtypeStruct((n_e * d_e, f_e), BF16),
                   jax.ShapeDtypeStruct((n_e * d_e, f_e), BF16),
                   jax.ShapeDtypeStruct((n_e * f_e, d_e), BF16)],
        compiler_params=_params(2),
        name="upgate",
    )(xn, ya, yb, w_gates, w_gates, w_up_a, w_up_b,
      w_eg.reshape(n_e * d_e, f_e), w_eu.reshape(n_e * d_e, f_e), w_ed.reshape(n_e * f_e, d_e))
    return (m, eg_bf.reshape(n_e, d_e, f_e), eu_bf.reshape(n_e, d_e, f_e),
            ed_bf.reshape(n_e, f_e, d_e))


def _outproj_kernel(m_ref, w_ref, x_ref, o_ref):
    o_ref[...] = x_ref[...] + _dot(m_ref[...], w_ref[...])


def _outproj(m, w_out_bf, x, tm=512):
    t, d = m.shape
    dout = w_out_bf.shape[1]
    return pl.pallas_call(
        _outproj_kernel,
        grid=(t // tm,),
        in_specs=[pl.BlockSpec((tm, d), lambda i: (i, 0)),
                  pl.BlockSpec((d, dout), lambda i: (0, 0), pipeline_mode=pl.Buffered(1)),
                  pl.BlockSpec((tm, dout), lambda i: (i, 0))],
        out_specs=pl.BlockSpec((tm, dout), lambda i: (i, 0)),
        out_shape=jax.ShapeDtypeStruct((t, dout), F32),
        compiler_params=_params(1),
        name="outproj",
    )(m, w_out_bf, x)


def _argmax_rows(rows):
    best = rows[0]
    idx = jnp.zeros(rows[0].shape, jnp.int32)
    for k in range(1, len(rows)):
        better = rows[k] > best
        best = jnp.where(better, rows[k], best)
        idx = jnp.where(better, k, idx)
    return best, idx


def _softmax_rows(rows):
    mx = functools.reduce(jnp.maximum, rows)
    ex = [jnp.exp(r - mx) for r in rows]
    den = functools.reduce(lambda a, b: a + b, ex)
    return [e / den for e in ex]


def _route_sort_kernel(h_ref, g_ref, wr_ref, br_ref, xs_ref, meta_ref, cnt_ref):
    tm = h_ref.shape[0]
    xn = _rms_scale(h_ref[...], g_ref[...])
    xn_hi = xn.astype(BF16)
    xn_lo = (xn - xn_hi.astype(F32)).astype(BF16)
    wr = wr_ref[...]
    wr_hi = wr.astype(BF16)
    wr_lo = (wr - wr_hi.astype(F32)).astype(BF16)
    nt_dims = (((1,), (1,)), ((), ()))
    lt = (lax.dot_general(wr_hi, xn_hi, nt_dims, preferred_element_type=F32)
          + lax.dot_general(wr_hi, xn_lo, nt_dims, preferred_element_type=F32)
          + lax.dot_general(wr_lo, xn_hi, nt_dims, preferred_element_type=F32)) + br_ref[...]
    pgs = _softmax_rows([lt[k:k + 1, :] for k in range(N_GROUPS)])
    pg, gi = _argmax_rows(pgs)
    sel = []
    for k in range(EXPERTS_PER_GROUP):
        r = jnp.zeros_like(pg)
        for g in range(N_GROUPS):
            row = N_GROUPS + g * EXPERTS_PER_GROUP + k
            r = jnp.where(gi == g, lt[row:row + 1, :], r)
        sel.append(r)
    pes = _softmax_rows(sel)
    p1, e1 = _argmax_rows(pes)
    rest = [jnp.where(e1 == k, -1.0, pes[k]) for k in range(EXPERTS_PER_GROUP)]
    p2, e2 = _argmax_rows(rest)
    den = p1 + p2
    w1 = pg * (p1 / den)
    w2 = pg * (p2 / den)
    lo = jnp.minimum(e1, e2)
    hi = jnp.maximum(e1, e2)
    w_lo = jnp.where(e1 < e2, w1, w2)
    w_hi = jnp.where(e1 < e2, w2, w1)
    ea = gi * EXPERTS_PER_GROUP + lo
    eb = gi * EXPERTS_PER_GROUP + hi

    erow = lax.broadcasted_iota(jnp.int32, (N_EXPERTS, tm), 0)
    oh_a = (erow == ea).astype(F32)
    oh_b = (erow == eb).astype(F32)
    a = lax.broadcasted_iota(jnp.int32, (tm, tm), 0)
    b = lax.broadcasted_iota(jnp.int32, (tm, tm), 1)
    before = (a < b).astype(BF16)
    cum = _dot((oh_a + oh_b).astype(BF16), before)
    cnt = jnp.sum(oh_a + oh_b, axis=1, keepdims=True)
    padded = jnp.floor((cnt + (GROUP - 1)) * (1.0 / GROUP)) * GROUP
    pos_a = jnp.sum(oh_a * cum + jnp.where(erow < ea, padded, 0.0), axis=0, keepdims=True)
    pos_b = jnp.sum(oh_b * cum + jnp.where(erow < eb, padded, 0.0), axis=0, keepdims=True)

    d = h_ref.shape[1]
    q = lax.broadcasted_iota(jnp.int32, (xs_ref.shape[0], tm), 0)
    perm_a = jnp.where(q == pos_a.astype(jnp.int32), 1.0, 0.0).astype(BF16)
    perm_b = jnp.where(q == pos_b.astype(jnp.int32), 1.0, 0.0).astype(BF16)
    xs_ref[:, 0:d] = _dot(perm_a + perm_b, xn_hi).astype(xs_ref.dtype)

    def gate_rows(w):
        hi = w.astype(BF16).astype(F32)
        lo = w - hi
        k = lax.broadcasted_iota(jnp.int32, (LANES, tm), 0)
        return jnp.where(k == 0, hi, jnp.where(k == 1, lo, 0.0)).astype(BF16)

    gates = (lax.dot_general(perm_a, gate_rows(w_lo), nt_dims, preferred_element_type=F32)
             + lax.dot_general(perm_b, gate_rows(w_hi), nt_dims, preferred_element_type=F32))
    xs_ref[:, d:d + LANES] = gates.astype(xs_ref.dtype)

    cnt_ref[...] = jnp.broadcast_to(cnt, cnt_ref.shape)
    meta_ref[0:1, :] = pos_a
    meta_ref[1:2, :] = pos_b
    meta_ref[2:8, :] = jnp.zeros((6, tm), F32)


def _route_sort(h1, g, w_rg, b_rg, w_re, b_re):
    t, d = h1.shape
    tm = ROUTE_BLOCK
    nb = t // tm
    n_log = w_rg.shape[1] + w_re.shape[1]
    wr = jnp.concatenate([w_rg, w_re], axis=1).T
    wr = jnp.pad(wr, ((0, ROUTER_ROWS - n_log), (0, 0)))
    br = jnp.pad(jnp.concatenate([b_rg, b_re]), (0, ROUTER_ROWS - n_log)).reshape(ROUTER_ROWS, 1)
    return pl.pallas_call(
        _route_sort_kernel,
        grid=(nb,),
        in_specs=[pl.BlockSpec((tm, d), lambda i: (i, 0)),
                  pl.BlockSpec((1, d), lambda i: (0, 0)),
                  pl.BlockSpec((ROUTER_ROWS, d), lambda i: (0, 0)),
                  pl.BlockSpec((ROUTER_ROWS, 1), lambda i: (0, 0))],
        out_specs=[pl.BlockSpec((LOCAL_ROWS, d + LANES), lambda i: (i, 0)),
                   pl.BlockSpec((8, tm), lambda i: (0, i)),
                   pl.BlockSpec((N_EXPERTS, LANES), lambda i: (i, 0))],
        out_shape=[jax.ShapeDtypeStruct((nb * LOCAL_ROWS, d + LANES), BF16),
                   jax.ShapeDtypeStruct((8, t), F32),
                   jax.ShapeDtypeStruct((nb * N_EXPERTS, LANES), F32)],
        compiler_params=_params(1),
        name="route_sort",
    )(h1, g.reshape(1, d), wr, br)


def _group_copy(src_ref, s_group, dst_ref, d_group, sem):
    return pltpu.make_async_copy(src_ref.at[pl.ds(pl.multiple_of(s_group * GROUP, GROUP), GROUP), :],
                                 dst_ref.at[pl.ds(pl.multiple_of(d_group * GROUP, GROUP), GROUP), :], sem)


def _expert_kernel(te_ref, nxt_ref, nreal_ref, nt_ref, gsrc_ref, gdst_ref, tail_ref,
                   xs_ref, wg_ref, wu_ref, wd_ref, ys_ref,
                   xbuf_ref, ybuf_ref, zero_ref, wgu_buf_ref, wd_buf_ref,
                   cur_ref, gsem, ssem, zsem, wsem):
    r = pl.program_id(0)
    nt = nt_ref[0]

    def gather(q, slot):
        for i in range(TILE_GROUPS):
            _group_copy(xs_ref, gsrc_ref[q * TILE_GROUPS + i], xbuf_ref, slot * TILE_GROUPS + i,
                        gsem.at[slot]).start(priority=1)

    def weight_copies(e, wslot):
        f = wg_ref.shape[2]
        return (pltpu.make_async_copy(wg_ref.at[e], wgu_buf_ref.at[wslot, :, pl.ds(0, f)], wsem.at[wslot]),
                pltpu.make_async_copy(wu_ref.at[e], wgu_buf_ref.at[wslot, :, pl.ds(f, f)], wsem.at[wslot]),
                pltpu.make_async_copy(wd_ref.at[e], wd_buf_ref.at[wslot], wsem.at[wslot]))

    def tile_wait(src, dst, sem):
        pltpu.make_async_copy(src.at[pl.ds(0, MOE_TILE), :], dst.at[pl.ds(0, MOE_TILE), :], sem).wait()

    @pl.when(r == 0)
    def _():
        cur_ref[0] = -1
        cur_ref[1] = -1
        for cp in weight_copies(te_ref[0], 0):
            cp.start()
        zero_ref[...] = jnp.zeros_like(zero_ref)
        ybuf_ref[...] = jnp.zeros_like(ybuf_ref)
        n_blocks = tail_ref.shape[0]

        def fill(make):
            def body(g, c):
                make(g)
                return c
            return body

        for blk in range(n_blocks):
            lax.fori_loop(tail_ref[blk], GROUPS_PER_BLOCK, fill(
                lambda g, blk=blk: _group_copy(zero_ref, 0, ys_ref, blk * GROUPS_PER_BLOCK + g, zsem).start()), 0)
        for blk in range(n_blocks):
            lax.fori_loop(tail_ref[blk], GROUPS_PER_BLOCK, fill(
                lambda g, blk=blk: _group_copy(zero_ref, 0, ys_ref, blk * GROUPS_PER_BLOCK + g, zsem).wait()), 0)
        gather(0, 0)

    @pl.when(r < nt)
    def _():
        e = te_ref[r]
        slot = r % 2

        gather(jnp.minimum(r + 1, nt - 1), 1 - slot)

        @pl.when(cur_ref[0] != e)
        def _():
            wslot = (cur_ref[1] + 1) % 2
            for cp in weight_copies(e, wslot):
                cp.wait()
            cur_ref[0] = e
            cur_ref[1] = cur_ref[1] + 1

            @pl.when(nxt_ref[r] >= 0)
            def _():
                for cp in weight_copies(nxt_ref[r], 1 - wslot):
                    cp.start()

        wslot = cur_ref[1] % 2
        tile_wait(xs_ref, xbuf_ref, gsem.at[slot])

        @pl.when(r >= 2)
        def _():
            tile_wait(ybuf_ref, ys_ref, ssem.at[slot])

        d = ybuf_ref.shape[1]
        f = wd_buf_ref.shape[1]
        for part in range(MOE_TILE // MXU_ROWS):
            @pl.when(nreal_ref[r] > part * (MXU_ROWS // GROUP))
            def _(part=part):
                rows = pl.ds(pl.multiple_of(slot * MOE_TILE + part * MXU_ROWS, MXU_ROWS), MXU_ROWS)
                x = xbuf_ref[rows, 0:d]
                gate_parts = xbuf_ref[rows, d:d + LANES].astype(F32)
                gate = gate_parts[:, 0:1] + gate_parts[:, 1:2]
                gu = _dot(x, wgu_buf_ref[wslot])
                hid = (jax.nn.silu(gu[:, 0:f]) * gu[:, f:2 * f]).astype(BF16)
                ybuf_ref[rows, :] = (_dot(hid, wd_buf_ref[wslot]) * gate).astype(ybuf_ref.dtype)

        for i in range(TILE_GROUPS):
            _group_copy(ybuf_ref, slot * TILE_GROUPS + i, ys_ref, gdst_ref[r * TILE_GROUPS + i],
                        ssem.at[slot]).start(priority=1)

        @pl.when(r == nt - 1)
        def _():
            tile_wait(xs_ref, xbuf_ref, gsem.at[1 - slot])
            tile_wait(ybuf_ref, ys_ref, ssem.at[slot])

            @pl.when(r >= 1)
            def _():
                tile_wait(ybuf_ref, ys_ref, ssem.at[1 - slot])


def _experts(plan, xs, w_gate, w_up, w_down):
    d = w_gate.shape[1]
    f = w_gate.shape[2]
    tile_e, next_e, n_real, n_tiles, gsrc, gdst, tail = plan
    r_max = tile_e.shape[0]
    n_blocks = tail.shape[0]
    any_spec = pl.BlockSpec(memory_space=pl.ANY)
    grid_spec = pltpu.PrefetchScalarGridSpec(
        num_scalar_prefetch=7,
        grid=(r_max,),
        in_specs=[any_spec, any_spec, any_spec, any_spec],
        out_specs=any_spec,
        scratch_shapes=[pltpu.VMEM((2 * MOE_TILE, d + LANES), BF16),
                        pltpu.VMEM((2 * MOE_TILE, d), BF16),
                        pltpu.VMEM((GROUP, d), BF16),
                        pltpu.VMEM((2, d, 2 * f), BF16),
                        pltpu.VMEM((2, f, d), BF16),
                        pltpu.SMEM((2,), jnp.int32),
                        pltpu.SemaphoreType.DMA((2,)),
                        pltpu.SemaphoreType.DMA((2,)),
                        pltpu.SemaphoreType.DMA(()),
                        pltpu.SemaphoreType.DMA((2,))],
    )
    return pl.pallas_call(
        _expert_kernel,
        grid_spec=grid_spec,
        out_shape=jax.ShapeDtypeStruct((n_blocks * LOCAL_ROWS, d), BF16),
        compiler_params=_params(1),
        name="experts",
    )(tile_e, next_e, n_real, n_tiles, gsrc, gdst, tail, xs, w_gate, w_up, w_down)


def _combine_kernel(ys_ref, h_ref, cm_ref, g_ref, o_ref):
    tm = h_ref.shape[0]
    cm = cm_ref[...]
    q = lax.broadcasted_iota(jnp.int32, (tm, ys_ref.shape[0]), 1)
    sel = jnp.where((q == cm[:, 0:1].astype(jnp.int32)) | (q == cm[:, 1:2].astype(jnp.int32)), 1.0, 0.0)
    h2 = h_ref[...] + _dot(sel.astype(BF16), ys_ref[...])
    o_ref[...] = _rms_scale(h2, g_ref[...])


def _combine(ys, h1, cmeta, g):
    t, d = h1.shape
    tm = ROUTE_BLOCK
    return pl.pallas_call(
        _combine_kernel,
        grid=(t // tm,),
        in_specs=[pl.BlockSpec((LOCAL_ROWS, d), lambda i: (i, 0)),
                  pl.BlockSpec((tm, d), lambda i: (i, 0)),
                  pl.BlockSpec((tm, LANES), lambda i: (i, 0)),
                  pl.BlockSpec((1, d), lambda i: (0, 0))],
        out_specs=pl.BlockSpec((tm, d), lambda i: (i, 0)),
        out_shape=jax.ShapeDtypeStruct((t, d), F32),
        compiler_params=_params(1),
        name="combine",
    )(ys, h1, cmeta, g.reshape(1, d))


def _expert_plan(counts, n_blocks):
    cnt = counts[:, 0].astype(jnp.int32).reshape(n_blocks, N_EXPERTS)
    groups = (cnt + GROUP - 1) // GROUP
    first = jnp.cumsum(groups, axis=1) - groups
    upto = jnp.cumsum(groups, axis=0)
    per_expert = upto[-1]
    tiles_e = (per_expert + TILE_GROUPS - 1) // TILE_GROUPS
    tile_end = jnp.cumsum(tiles_e)
    n_tiles = tile_end[-1]
    max_groups = 2 * ROUTE_BLOCK * n_blocks // GROUP + n_blocks * N_EXPERTS
    r_max = max_groups // TILE_GROUPS + N_EXPERTS
    tile_ids = jnp.arange(r_max, dtype=jnp.int32)
    tile = jnp.minimum(tile_ids, n_tiles - 1)
    tile_e = jnp.sum((tile_end[None, :] <= tile[:, None]).astype(jnp.int32), axis=1)
    later = (tile_e[None, :] > tile_e[:, None]) & (tile_ids[None, :] < n_tiles)
    next_e = jnp.min(jnp.where(later, tile_e[None, :], N_EXPERTS), axis=1)
    next_e = jnp.where(next_e == N_EXPERTS, -1, next_e)

    slot = jnp.arange(r_max * TILE_GROUPS, dtype=jnp.int32)
    s_tile = slot // TILE_GROUPS
    oh_e = jnp.repeat(tile_e, TILE_GROUPS)[:, None] == jnp.arange(N_EXPERTS, dtype=jnp.int32)[None, :]

    def by_expert(table):
        return jnp.sum(jnp.where(oh_e[:, None, :], table[None], 0), axis=-1)

    k = slot - by_expert(((tile_end - tiles_e) * TILE_GROUPS)[None, :])[:, 0]
    real = (k < by_expert(per_expert[None, :])[:, 0]) & (s_tile < n_tiles)
    upto_e = by_expert(upto)
    blk = jnp.minimum(jnp.sum((upto_e <= k[:, None]).astype(jnp.int32), axis=1), n_blocks - 1)
    oh_b = blk[:, None] == jnp.arange(n_blocks, dtype=jnp.int32)[None, :]

    def by_block(table_se):
        return jnp.sum(jnp.where(oh_b, table_se, 0), axis=1)

    before = by_block(upto_e - by_expert(groups))
    src = blk * GROUPS_PER_BLOCK + by_block(by_expert(first)) + (k - before)
    zero_group = GROUPS_PER_BLOCK - 1
    spare = n_blocks * GROUPS_PER_BLOCK + (s_tile % 2) * TILE_GROUPS + slot % TILE_GROUPS
    gsrc = jnp.where(real, src, zero_group)
    gdst = jnp.where(real, src, spare)
    tail = jnp.concatenate([jnp.sum(groups, axis=1), jnp.zeros((1,), jnp.int32)])
    n_real = jnp.sum(real.reshape(r_max, TILE_GROUPS).astype(jnp.int32), axis=1)
    return tile_e, next_e, n_real, n_tiles.reshape(1), gsrc, gdst, tail


def _layer(h, norm_mix_g, w_in, conv_w, sgu_ln_g, sgu_ln_b, sgu_w_s, sgu_b_s, w_up_conv,
           w_up_sgu, w_out, norm_ffn_g, w_rg, b_rg, w_re, b_re, w_eg, w_eu, w_ed, out_g):
    t, d = h.shape
    conv_width = conv_w.shape[1]
    sgu_width = sgu_ln_g.shape[0]
    xn, yb = _sgu_branch(h, norm_mix_g, w_in, 3 * conv_width, sgu_ln_g, sgu_ln_b, sgu_w_s, sgu_b_s)
    gate_col0 = 3 * conv_width + 2 * sgu_width
    ya, (wg_bf, wua_bf, wub_bf, wout_bf) = _conv_branch(
        xn, w_in, conv_w, conv_width,
        to_cast=[(w_in, gate_col0, w_in.shape[1] - gate_col0, 1024),
                 (w_up_conv, 0, d, d), (w_up_sgu, 0, d, d), (w_out, 0, d, d)])
    m, eg_bf, eu_bf, ed_bf = _upgate(xn, ya, yb, wg_bf, wua_bf, wub_bf, w_eg, w_eu, w_ed)
    h1 = _outproj(m, wout_bf, h)
    xs, meta, counts = _route_sort(h1, norm_ffn_g, w_rg, b_rg, w_re, b_re)
    ys = _experts(_expert_plan(counts, t // ROUTE_BLOCK), xs, eg_bf, eu_bf, ed_bf)
    cmeta = jnp.pad(meta[0:2].T, ((0, 0), (0, LANES - 2)))
    return _combine(ys, h1, cmeta, out_g)


def kernel(x, norm_mix_g, w_in, conv_w, sgu_ln_g, sgu_ln_b, sgu_w_s, sgu_b_s, w_up_conv, w_up_sgu, w_out, norm_ffn_g, w_router_group, b_router_group, w_router_expert, b_router_expert, w_exp_gate, w_exp_up, w_exp_down, norm_final_g):
    bsz, s, d = x.shape
    depth = w_in.shape[0]
    assert bsz == 1 and depth == 1, "causal conv carry and the fused final norm assume one sequence, one layer"
    assert s % ROUTE_BLOCK == 0
    out = _layer(x.reshape(s, d), norm_mix_g[0], w_in[0], conv_w[0], sgu_ln_g[0], sgu_ln_b[0],
                 sgu_w_s[0], sgu_b_s[0], w_up_conv[0], w_up_sgu[0], w_out[0], norm_ffn_g[0],
                 w_router_group[0], b_router_group[0], w_router_expert[0], b_router_expert[0],
                 w_exp_gate[0], w_exp_up[0], w_exp_down[0], norm_final_g)
    return out.reshape(bsz, s, d)
```

```python
import functools

import jax
import jax.numpy as jnp
from jax import lax
from jax.experimental import pallas as pl
from jax.experimental.pallas import tpu as pltpu

F32 = jnp.float32
BF16 = jnp.bfloat16

EPS = 1e-6
CHUNK = 64
CONV_K = 3
SGU_HEADS = 8
SGU_BLOCK = 128
N_GROUPS = 4
EXPERTS_PER_GROUP = 4
N_EXPERTS = N_GROUPS * EXPERTS_PER_GROUP
ROUTER_ROWS = 32
LANES = 128

VMEM_LIMIT_BYTES = 56 * 1024 * 1024

ROUTE_BLOCK = 512
GROUP = 16
LOCAL_ROWS = -(-(2 * ROUTE_BLOCK + N_EXPERTS * (GROUP - 1)) // 256) * 256
GROUPS_PER_BLOCK = LOCAL_ROWS // GROUP
MOE_TILE = 512
TILE_GROUPS = MOE_TILE // GROUP
MXU_ROWS = 256


def _params(n_axes):
    return pltpu.CompilerParams(
        dimension_semantics=("arbitrary",) * n_axes,
        vmem_limit_bytes=VMEM_LIMIT_BYTES)


def _dot(a, b):
    return jnp.dot(a, b, preferred_element_type=F32)


def _rms_scale(x, g):
    ms = jnp.mean(x * x, axis=-1, keepdims=True)
    return x * lax.rsqrt(ms + EPS) * g


def _conv_kernel(cast_chunks, xn_ref, wb_ref, wc_ref, wh_ref, cw_ref, *rest):
    n_in = sum(cast_chunks)
    cast_in, o_ref = rest[:n_in], rest[n_in]
    cast_out = rest[n_in + 1:n_in + 1 + len(cast_chunks)]
    wbf_ref, carry_ref = rest[n_in + 1 + len(cast_chunks):]
    i = pl.program_id(1)
    tn = wb_ref.shape[1]
    tm = xn_ref.shape[0]
    src = iter(cast_in)
    for dst, n_chunks in zip(cast_out, cast_chunks):
        wc = dst.shape[1] // n_chunks
        for k in range(n_chunks):
            dst[:, k * wc:(k + 1) * wc] = next(src)[...].astype(BF16)

    @pl.when(i == 0)
    def _():
        wbf_ref[:, 0:tn] = wb_ref[...].astype(BF16)
        wbf_ref[:, tn:2 * tn] = wc_ref[...].astype(BF16)
        wbf_ref[:, 2 * tn:3 * tn] = wh_ref[...].astype(BF16)
        carry_ref[...] = jnp.zeros_like(carry_ref)

    proj = _dot(xn_ref[...], wbf_ref[...])
    b = proj[:, 0:tn]
    p = proj[:, tn:2 * tn] * proj[:, 2 * tn:3 * tn]
    prev = carry_ref[...]
    carry_ref[...] = p[tm - 8:tm, :]
    row = lax.broadcasted_iota(jnp.int32, p.shape, 0)
    p1 = jnp.where(row == 0, prev[7:8, :], pltpu.roll(p, 1, axis=0))
    p2 = jnp.where(row == 0, prev[6:7, :],
                   jnp.where(row == 1, prev[7:8, :], pltpu.roll(p, 2, axis=0)))
    cw = cw_ref[...]
    y = b * (cw[0:1, :] * p2 + cw[1:2, :] * p1 + cw[2:3, :] * p)
    o_ref[...] = y.astype(o_ref.dtype)


def _conv_branch(xn, w_in, conv_w, width, to_cast, tm=1024, tn=256):
    t, d = xn.shape
    nj = width // tn
    ni = t // tm
    steps = nj * ni
    cast_args, cast_in_specs, cast_out_specs, cast_shapes, cast_chunks = [], [], [], [], []
    for arr, col0, n_cols, chunk in to_cast:
        rows = arr.shape[0] // steps
        assert rows * steps == arr.shape[0] and rows % 16 == 0
        assert col0 % chunk == 0 and n_cols % chunk == 0
        for k in range(n_cols // chunk):
            cast_args.append(arr)
            cast_in_specs.append(
                pl.BlockSpec((rows, chunk), lambda j, i, c=col0 // chunk + k: (j * ni + i, c)))
        cast_chunks.append(n_cols // chunk)
        cast_out_specs.append(pl.BlockSpec((rows, n_cols), lambda j, i: (j * ni + i, 0)))
        cast_shapes.append(jax.ShapeDtypeStruct((arr.shape[0], n_cols), BF16))
    outs = pl.pallas_call(
        functools.partial(_conv_kernel, tuple(cast_chunks)),
        grid=(nj, ni),
        in_specs=[pl.BlockSpec((tm, d), lambda j, i: (i, 0)),
                  pl.BlockSpec((d, tn), lambda j, i: (0, j)),
                  pl.BlockSpec((d, tn), lambda j, i: (0, nj + j)),
                  pl.BlockSpec((d, tn), lambda j, i: (0, 2 * nj + j)),
                  pl.BlockSpec((CONV_K, tn), lambda j, i: (0, j))] + cast_in_specs,
        out_specs=[pl.BlockSpec((tm, tn), lambda j, i: (i, j))] + cast_out_specs,
        out_shape=[jax.ShapeDtypeStruct((t, width), BF16)] + cast_shapes,
        scratch_shapes=[pltpu.VMEM((d, 3 * tn), BF16),
                        pltpu.VMEM((8, tn), F32)],
        compiler_params=_params(2),
        name="conv_branch",
    )(xn, w_in, w_in, w_in, conv_w, *cast_args)
    return outs[0], outs[1:]


def _sgu_kernel(x_ref, g_ref, wu_ref, wv_ref, lng_ref, lnb_ref, ws_ref, bsx_ref, xn_ref, o_ref, wbf_ref):
    tm = x_ref.shape[0]
    w = o_ref.shape[1]
    hd = w // SGU_HEADS

    @pl.when(pl.program_id(0) == 0)
    def _():
        wbf_ref[:, 0:w] = wu_ref[...].astype(BF16)
        wbf_ref[:, w:2 * w] = wv_ref[...].astype(BF16)

    xn = _rms_scale(x_ref[...], g_ref[...]).astype(BF16)
    xn_ref[...] = xn
    gz = jax.nn.gelu(_dot(xn, wbf_ref[...]))
    v = gz[:, w:2 * w]
    mu = jnp.mean(v, axis=-1, keepdims=True)
    vc = v - mu
    var = jnp.mean(vc * vc, axis=-1, keepdims=True)
    vn = (vc * lax.rsqrt(var + EPS) * lng_ref[...] + lnb_ref[...]).astype(BF16)
    ii = lax.broadcasted_iota(jnp.int32, (SGU_BLOCK, SGU_BLOCK), 0)
    jj = lax.broadcasted_iota(jnp.int32, (SGU_BLOCK, SGU_BLOCK), 1)
    mask = (jj // CHUNK) <= (ii // CHUNK)
    for h in range(SGU_HEADS):
        wm = jnp.where(mask, ws_ref[h], 0.0).astype(BF16)
        cs = slice(h * hd, (h + 1) * hd)
        for n in range(tm // SGU_BLOCK):
            rs = slice(n * SGU_BLOCK, (n + 1) * SGU_BLOCK)
            vm = _dot(wm, vn[rs, cs]) + bsx_ref[:, cs]
            o_ref[rs, cs] = (gz[rs, cs] * vm).astype(o_ref.dtype)


def _sgu_branch(x, g, w_in, col0, ln_g, ln_b, w_s, b_s, tm=512):
    t, d = x.shape
    w = ln_g.shape[0]
    hd = w // SGU_HEADS
    assert col0 % w == 0
    c0 = col0 // w
    bsx = jnp.repeat(b_s.T, hd, axis=1)
    once = pl.Buffered(1)
    return pl.pallas_call(
        _sgu_kernel,
        grid=(t // tm,),
        in_specs=[pl.BlockSpec((tm, d), lambda i: (i, 0)),
                  pl.BlockSpec((1, d), lambda i: (0, 0)),
                  pl.BlockSpec((d, w), lambda i: (0, c0), pipeline_mode=once),
                  pl.BlockSpec((d, w), lambda i: (0, c0 + 1), pipeline_mode=once),
                  pl.BlockSpec((1, w), lambda i: (0, 0)),
                  pl.BlockSpec((1, w), lambda i: (0, 0)),
                  pl.BlockSpec((SGU_HEADS, SGU_BLOCK, SGU_BLOCK), lambda i: (0, 0, 0)),
                  pl.BlockSpec((SGU_BLOCK, w), lambda i: (0, 0))],
        out_specs=[pl.BlockSpec((tm, d), lambda i: (i, 0)),
                   pl.BlockSpec((tm, w), lambda i: (i, 0))],
        out_shape=[jax.ShapeDtypeStruct((t, d), BF16),
                   jax.ShapeDtypeStruct((t, w), BF16)],
        scratch_shapes=[pltpu.VMEM((d, 2 * w), BF16)],
        compiler_params=_params(1),
        name="sgu_branch",
    )(x, g.reshape(1, d), w_in, w_in, ln_g.reshape(1, w), ln_b.reshape(1, w), w_s, bsx)


def _upgate_kernel(xn_ref, ya_ref, yb_ref, wgc_ref, wgs_ref, wua_ref, wub_ref, eg_ref, eu_ref, ed_ref,
                   o_ref, eg_bf_ref, eu_bf_ref, ed_bf_ref):
    eg_bf_ref[...] = eg_ref[...].astype(BF16)
    eu_bf_ref[...] = eu_ref[...].astype(BF16)
    ed_bf_ref[...] = ed_ref[...].astype(BF16)

    xn = xn_ref[...]
    m = (jax.nn.sigmoid(_dot(xn, wgc_ref[...])) * _dot(ya_ref[...], wua_ref[...])
         + jax.nn.sigmoid(_dot(xn, wgs_ref[...])) * _dot(yb_ref[...], wub_ref[...]))
    o_ref[...] = m.astype(o_ref.dtype)


def _upgate(xn, ya, yb, w_gates, w_up_a, w_up_b, w_eg, w_eu, w_ed, tm=1024, tn=512):
    t, d = xn.shape
    wa = ya.shape[1]
    wb = yb.shape[1]
    dout = w_up_a.shape[1]
    c0 = 0
    nj = dout // tn
    ni = t // tm
    n_e, d_e, f_e = w_eg.shape
    up_rows = n_e * d_e // (nj * ni)
    down_rows = n_e * f_e // (nj * ni)
    assert up_rows * nj * ni == n_e * d_e and up_rows % 16 == 0
    assert down_rows * nj * ni == n_e * f_e and down_rows % 16 == 0
    up_spec = pl.BlockSpec((up_rows, f_e), lambda j, i: (j * ni + i, 0))
    down_spec = pl.BlockSpec((down_rows, d_e), lambda j, i: (j * ni + i, 0))
    m, eg_bf, eu_bf, ed_bf = pl.pallas_call(
        _upgate_kernel,
        grid=(nj, ni),
        in_specs=[pl.BlockSpec((tm, d), lambda j, i: (i, 0)),
                  pl.BlockSpec((tm, wa), lambda j, i: (i, 0)),
                  pl.BlockSpec((tm, wb), lambda j, i: (i, 0)),
                  pl.BlockSpec((d, tn), lambda j, i: (0, c0 + j)),
                  pl.BlockSpec((d, tn), lambda j, i: (0, c0 + nj + j)),
                  pl.BlockSpec((wa, tn), lambda j, i: (0, j)),
                  pl.BlockSpec((wb, tn), lambda j, i: (0, j)),
                  up_spec, up_spec, down_spec],
        out_specs=[pl.BlockSpec((tm, tn), lambda j, i: (i, j)), up_spec, up_spec, down_spec],
        out_shape=[jax.ShapeDtypeStruct((t, dout), BF16),
                   jax.ShapeDtypeStruct((n_e * d_e, f_e), BF16),
                   jax.ShapeDtypeStruct((n_e * d_e, f_e), BF16),
                   jax.ShapeDtypeStruct((n_e * f_e, d_e), BF16)],
        compiler_params=_params(2),
        name="upgate",
    )(xn, ya, yb, w_gates, w_gates, w_up_a, w_up_b,
      w_eg.reshape(n_e * d_e, f_e), w_eu.reshape(n_e * d_e, f_e), w_ed.reshape(n_e * f_e, d_e))
    return (m, eg_bf.reshape(n_e, d_e, f_e), eu_bf.reshape(n_e, d_e, f_e),
            ed_bf.reshape(n_e, f_e, d_e))


def _outproj_kernel(m_ref, w_ref, x_ref, o_ref):
    o_ref[...] = x_ref[...] + _dot(m_ref[...], w_ref[...])


def _outproj(m, w_out_bf, x, tm=512):
    t, d = m.shape
    dout = w_out_bf.shape[1]
    return pl.pallas_call(
        _outproj_kernel,
        grid=(t // tm,),
        in_specs=[pl.BlockSpec((tm, d), lambda i: (i, 0)),
                  pl.BlockSpec((d, dout), lambda i: (0, 0), pipeline_mode=pl.Buffered(1)),
                  pl.BlockSpec((tm, dout), lambda i: (i, 0))],
        out_specs=pl.BlockSpec((tm, dout), lambda i: (i, 0)),
        out_shape=jax.ShapeDtypeStruct((t, dout), F32),
        compiler_params=_params(1),
        name="outproj",
    )(m, w_out_bf, x)


def _argmax_rows(rows):
    best = rows[0]
    idx = jnp.zeros(rows[0].shape, jnp.int32)
    for k in range(1, len(rows)):
        better = rows[k] > best
        best = jnp.where(better, rows[k], best)
        idx = jnp.where(better, k, idx)
    return best, idx


def _softmax_rows(rows):
    mx = functools.reduce(jnp.maximum, rows)
    ex = [jnp.exp(r - mx) for r in rows]
    den = functools.reduce(lambda a, b: a + b, ex)
    return [e / den for e in ex]


def _route_sort_kernel(h_ref, g_ref, wr_ref, br_ref, xs_ref, meta_ref, cnt_ref, before_ref):
    tm = h_ref.shape[0]
    xn = _rms_scale(h_ref[...], g_ref[...])
    xn_hi = xn.astype(BF16)
    xn_lo = (xn - xn_hi.astype(F32)).astype(BF16)
    wr = wr_ref[...]
    wr_hi = wr.astype(BF16)
    wr_lo = (wr - wr_hi.astype(F32)).astype(BF16)
    nt_dims = (((1,), (1,)), ((), ()))
    lt = (lax.dot_general(wr_hi, xn_hi, nt_dims, preferred_element_type=F32)
          + lax.dot_general(wr_hi, xn_lo, nt_dims, preferred_element_type=F32)
          + lax.dot_general(wr_lo, xn_hi, nt_dims, preferred_element_type=F32)) + br_ref[...]
    pgs = _softmax_rows([lt[k:k + 1, :] for k in range(N_GROUPS)])
    pg, gi = _argmax_rows(pgs)
    sel = []
    for k in range(EXPERTS_PER_GROUP):
        r = jnp.zeros_like(pg)
        for g in range(N_GROUPS):
            row = N_GROUPS + g * EXPERTS_PER_GROUP + k
            r = jnp.where(gi == g, lt[row:row + 1, :], r)
        sel.append(r)
    pes = _softmax_rows(sel)
    p1, e1 = _argmax_rows(pes)
    rest = [jnp.where(e1 == k, -1.0, pes[k]) for k in range(EXPERTS_PER_GROUP)]
    p2, e2 = _argmax_rows(rest)
    den = p1 + p2
    w1 = pg * (p1 / den)
    w2 = pg * (p2 / den)
    lo = jnp.minimum(e1, e2)
    hi = jnp.maximum(e1, e2)
    w_lo = jnp.where(e1 < e2, w1, w2)
    w_hi = jnp.where(e1 < e2, w2, w1)
    ea = gi * EXPERTS_PER_GROUP + lo
    eb = gi * EXPERTS_PER_GROUP + hi

    erow = lax.broadcasted_iota(jnp.int32, (N_EXPERTS, tm), 0)
    oh_a = (erow == ea).astype(F32)
    oh_b = (erow == eb).astype(F32)

    @pl.when(pl.program_id(0) == 0)
    def _():
        a = lax.broadcasted_iota(jnp.int32, (tm, tm), 0)
        b = lax.broadcasted_iota(jnp.int32, (tm, tm), 1)
        before_ref[...] = (a < b).astype(BF16)

    cum = _dot((oh_a + oh_b).astype(BF16), before_ref[...])
    cnt = jnp.sum(oh_a + oh_b, axis=1, keepdims=True)
    padded = jnp.floor((cnt + (GROUP - 1)) * (1.0 / GROUP)) * GROUP
    pos_a = jnp.sum(oh_a * cum + jnp.where(erow < ea, padded, 0.0), axis=0, keepdims=True)
    pos_b = jnp.sum(oh_b * cum + jnp.where(erow < eb, padded, 0.0), axis=0, keepdims=True)

    d = h_ref.shape[1]
    q = lax.broadcasted_iota(jnp.int32, (xs_ref.shape[0], tm), 0)
    perm_a = jnp.where(q == pos_a.astype(jnp.int32), 1.0, 0.0).astype(BF16)
    perm_b = jnp.where(q == pos_b.astype(jnp.int32), 1.0, 0.0).astype(BF16)
    xs_ref[:, 0:d] = _dot(perm_a + perm_b, xn_hi).astype(xs_ref.dtype)

    def gate_rows(w):
        hi = w.astype(BF16).astype(F32)
        lo = w - hi
        k = lax.broadcasted_iota(jnp.int32, (LANES, tm), 0)
        return jnp.where(k == 0, hi, jnp.where(k == 1, lo, 0.0)).astype(BF16)

    gates = (lax.dot_general(perm_a, gate_rows(w_lo), nt_dims, preferred_element_type=F32)
             + lax.dot_general(perm_b, gate_rows(w_hi), nt_dims, preferred_element_type=F32))
    xs_ref[:, d:d + LANES] = gates.astype(xs_ref.dtype)

    cnt_ref[...] = jnp.broadcast_to(cnt, cnt_ref.shape)
    meta_ref[0:1, :] = pos_a
    meta_ref[1:2, :] = pos_b
    meta_ref[2:8, :] = jnp.zeros((6, tm), F32)


def _route_sort(h1, g, w_rg, b_rg, w_re, b_re):
    t, d = h1.shape
    tm = ROUTE_BLOCK
    nb = t // tm
    n_log = w_rg.shape[1] + w_re.shape[1]
    wr = jnp.concatenate([w_rg, w_re], axis=1).T
    wr = jnp.pad(wr, ((0, ROUTER_ROWS - n_log), (0, 0)))
    br = jnp.pad(jnp.concatenate([b_rg, b_re]), (0, ROUTER_ROWS - n_log)).reshape(ROUTER_ROWS, 1)
    return pl.pallas_call(
        _route_sort_kernel,
        grid=(nb,),
        in_specs=[pl.BlockSpec((tm, d), lambda i: (i, 0)),
                  pl.BlockSpec((1, d), lambda i: (0, 0)),
                  pl.BlockSpec((ROUTER_ROWS, d), lambda i: (0, 0)),
                  pl.BlockSpec((ROUTER_ROWS, 1), lambda i: (0, 0))],
        out_specs=[pl.BlockSpec((LOCAL_ROWS, d + LANES), lambda i: (i, 0)),
                   pl.BlockSpec((8, tm), lambda i: (0, i)),
                   pl.BlockSpec((N_EXPERTS, LANES), lambda i: (i, 0))],
        out_shape=[jax.ShapeDtypeStruct((nb * LOCAL_ROWS, d + LANES), BF16),
                   jax.ShapeDtypeStruct((8, t), F32),
                   jax.ShapeDtypeStruct((nb * N_EXPERTS, LANES), F32)],
        scratch_shapes=[pltpu.VMEM((tm, tm), BF16)],
        compiler_params=_params(1),
        name="route_sort",
    )(h1, g.reshape(1, d), wr, br)


def _group_copy(src_ref, s_group, dst_ref, d_group, sem):
    return pltpu.make_async_copy(src_ref.at[pl.ds(pl.multiple_of(s_group * GROUP, GROUP), GROUP), :],
                                 dst_ref.at[pl.ds(pl.multiple_of(d_group * GROUP, GROUP), GROUP), :], sem)


def _expert_kernel(te_ref, nxt_ref, nreal_ref, nt_ref, gsrc_ref, gdst_ref, tail_ref,
                   xs_ref, wg_ref, wu_ref, wd_ref, ys_ref,
                   xbuf_ref, ybuf_ref, zero_ref, wgu_buf_ref, wd_buf_ref,
                   cur_ref, gsem, ssem, zsem, wsem):
    r = pl.program_id(0)
    nt = nt_ref[0]

    def gather(q, slot):
        for i in range(TILE_GROUPS):
            _group_copy(xs_ref, gsrc_ref[q * TILE_GROUPS + i], xbuf_ref, slot * TILE_GROUPS + i,
                        gsem.at[slot]).start(priority=1)

    def weight_copies(e, wslot):
        f = wg_ref.shape[2]
        return (pltpu.make_async_copy(wg_ref.at[e], wgu_buf_ref.at[wslot, :, pl.ds(0, f)], wsem.at[wslot]),
                pltpu.make_async_copy(wu_ref.at[e], wgu_buf_ref.at[wslot, :, pl.ds(f, f)], wsem.at[wslot]),
                pltpu.make_async_copy(wd_ref.at[e], wd_buf_ref.at[wslot], wsem.at[wslot]))

    def tile_wait(src, dst, sem):
        pltpu.make_async_copy(src.at[pl.ds(0, MOE_TILE), :], dst.at[pl.ds(0, MOE_TILE), :], sem).wait()

    @pl.when(r == 0)
    def _():
        cur_ref[0] = -1
        cur_ref[1] = -1
        for cp in weight_copies(te_ref[0], 0):
            cp.start()
        gather(0, 0)
        zero_ref[...] = jnp.zeros_like(zero_ref)
        ybuf_ref[...] = jnp.zeros_like(ybuf_ref)
        n_blocks = tail_ref.shape[0]

        def fill(make):
            def body(g, c):
                make(g)
                return c
            return body

        for blk in range(n_blocks):
            lax.fori_loop(tail_ref[blk], GROUPS_PER_BLOCK, fill(
                lambda g, blk=blk: _group_copy(zero_ref, 0, ys_ref, blk * GROUPS_PER_BLOCK + g, zsem).start()), 0)
        for blk in range(n_blocks):
            lax.fori_loop(tail_ref[blk], GROUPS_PER_BLOCK, fill(
                lambda g, blk=blk: _group_copy(zero_ref, 0, ys_ref, blk * GROUPS_PER_BLOCK + g, zsem).wait()), 0)

    @pl.when(r < nt)
    def _():
        e = te_ref[r]
        slot = r % 2

        gather(jnp.minimum(r + 1, nt - 1), 1 - slot)

        @pl.when(cur_ref[0] != e)
        def _():
            wslot = (cur_ref[1] + 1) % 2
            for cp in weight_copies(e, wslot):
                cp.wait()
            cur_ref[0] = e
            cur_ref[1] = cur_ref[1] + 1

            @pl.when(nxt_ref[r] >= 0)
            def _():
                for cp in weight_copies(nxt_ref[r], 1 - wslot):
                    cp.start()

        wslot = cur_ref[1] % 2
        tile_wait(xs_ref, xbuf_ref, gsem.at[slot])

        @pl.when(r >= 2)
        def _():
            tile_wait(ybuf_ref, ys_ref, ssem.at[slot])

        d = ybuf_ref.shape[1]
        f = wd_buf_ref.shape[1]
        for part in range(MOE_TILE // MXU_ROWS):
            @pl.when(nreal_ref[r] > part * (MXU_ROWS // GROUP))
            def _(part=part):
                rows = pl.ds(pl.multiple_of(slot * MOE_TILE + part * MXU_ROWS, MXU_ROWS), MXU_ROWS)
                x = xbuf_ref[rows, 0:d]
                gate_parts = xbuf_ref[rows, d:d + LANES].astype(F32)
                gate = gate_parts[:, 0:1] + gate_parts[:, 1:2]
                gu = _dot(x, wgu_buf_ref[wslot])
                hid = (jax.nn.silu(gu[:, 0:f]) * gu[:, f:2 * f]).astype(BF16)
                ybuf_ref[rows, :] = (_dot(hid, wd_buf_ref[wslot]) * gate).astype(ybuf_ref.dtype)

        for i in range(TILE_GROUPS):
            _group_copy(ybuf_ref, slot * TILE_GROUPS + i, ys_ref, gdst_ref[r * TILE_GROUPS + i],
                        ssem.at[slot]).start(priority=1)

        @pl.when(r == nt - 1)
        def _():
            tile_wait(xs_ref, xbuf_ref, gsem.at[1 - slot])
            tile_wait(ybuf_ref, ys_ref, ssem.at[slot])

            @pl.when(r >= 1)
            def _():
                tile_wait(ybuf_ref, ys_ref, ssem.at[1 - slot])


def _experts(plan, xs, w_gate, w_up, w_down):
    d = w_gate.shape[1]
    f = w_gate.shape[2]
    tile_e, next_e, n_real, n_tiles, gsrc, gdst, tail = plan
    r_max = tile_e.shape[0]
    n_blocks = tail.shape[0]
    any_spec = pl.BlockSpec(memory_space=pl.ANY)
    grid_spec = pltpu.PrefetchScalarGridSpec(
        num_scalar_prefetch=7,
        grid=(r_max,),
        in_specs=[any_spec, any_spec, any_spec, any_spec],
        out_specs=any_spec,
        scratch_shapes=[pltpu.VMEM((2 * MOE_TILE, d + LANES), BF16),
                        pltpu.VMEM((2 * MOE_TILE, d), BF16),
                        pltpu.VMEM((GROUP, d), BF16),
                        pltpu.VMEM((2, d, 2 * f), BF16),
                        pltpu.VMEM((2, f, d), BF16),
                        pltpu.SMEM((2,), jnp.int32),
                        pltpu.SemaphoreType.DMA((2,)),
                        pltpu.SemaphoreType.DMA((2,)),
                        pltpu.SemaphoreType.DMA(()),
                        pltpu.SemaphoreType.DMA((2,))],
    )
    return pl.pallas_call(
        _expert_kernel,
        grid_spec=grid_spec,
        out_shape=jax.ShapeDtypeStruct((n_blocks * LOCAL_ROWS, d), BF16),
        compiler_params=_params(1),
        name="experts",
    )(tile_e, next_e, n_real, n_tiles, gsrc, gdst, tail, xs, w_gate, w_up, w_down)


def _combine_kernel(ys_ref, h_ref, cm_ref, g_ref, o_ref):
    tm = h_ref.shape[0]
    cm = cm_ref[...]
    q = lax.broadcasted_iota(jnp.int32, (tm, ys_ref.shape[0]), 1)
    sel = jnp.where((q == cm[:, 0:1].astype(jnp.int32)) | (q == cm[:, 1:2].astype(jnp.int32)), 1.0, 0.0)
    h2 = h_ref[...] + _dot(sel.astype(BF16), ys_ref[...])
    o_ref[...] = _rms_scale(h2, g_ref[...])


def _combine(ys, h1, cmeta, g):
    t, d = h1.shape
    tm = ROUTE_BLOCK
    return pl.pallas_call(
        _combine_kernel,
        grid=(t // tm,),
        in_specs=[pl.BlockSpec((LOCAL_ROWS, d), lambda i: (i, 0)),
                  pl.BlockSpec((tm, d), lambda i: (i, 0)),
                  pl.BlockSpec((tm, LANES), lambda i: (i, 0)),
                  pl.BlockSpec((1, d), lambda i: (0, 0))],
        out_specs=pl.BlockSpec((tm, d), lambda i: (i, 0)),
        out_shape=jax.ShapeDtypeStruct((t, d), F32),
        compiler_params=_params(1),
        name="combine",
    )(ys, h1, cmeta, g.reshape(1, d))


def _expert_plan(counts, n_blocks):
    cnt = counts[:, 0].astype(jnp.int32).reshape(n_blocks, N_EXPERTS)
    groups = (cnt + GROUP - 1) // GROUP
    first = jnp.cumsum(groups, axis=1) - groups
    upto = jnp.cumsum(groups, axis=0)
    per_expert = upto[-1]
    tiles_e = (per_expert + TILE_GROUPS - 1) // TILE_GROUPS
    tile_end = jnp.cumsum(tiles_e)
    n_tiles = tile_end[-1]
    max_groups = 2 * ROUTE_BLOCK * n_blocks // GROUP + n_blocks * N_EXPERTS
    r_max = max_groups // TILE_GROUPS + N_EXPERTS
    tile_ids = jnp.arange(r_max, dtype=jnp.int32)
    tile = jnp.minimum(tile_ids, n_tiles - 1)
    tile_e = jnp.sum((tile_end[None, :] <= tile[:, None]).astype(jnp.int32), axis=1)
    later = (tile_e[None, :] > tile_e[:, None]) & (tile_ids[None, :] < n_tiles)
    next_e = jnp.min(jnp.where(later, tile_e[None, :], N_EXPERTS), axis=1)
    next_e = jnp.where(next_e == N_EXPERTS, -1, next_e)

    slot = jnp.arange(r_max * TILE_GROUPS, dtype=jnp.int32)
    s_tile = slot // TILE_GROUPS
    oh_e = jnp.repeat(tile_e, TILE_GROUPS)[:, None] == jnp.arange(N_EXPERTS, dtype=jnp.int32)[None, :]

    def by_expert(table):
        return jnp.sum(jnp.where(oh_e[:, None, :], table[None], 0), axis=-1)

    k = slot - by_expert(((tile_end - tiles_e) * TILE_GROUPS)[None, :])[:, 0]
    real = (k < by_expert(per_expert[None, :])[:, 0]) & (s_tile < n_tiles)
    upto_e = by_expert(upto)
    blk = jnp.minimum(jnp.sum((upto_e <= k[:, None]).astype(jnp.int32), axis=1), n_blocks - 1)
    oh_b = blk[:, None] == jnp.arange(n_blocks, dtype=jnp.int32)[None, :]

    def by_block(table_se):
        return jnp.sum(jnp.where(oh_b, table_se, 0), axis=1)

    before = by_block(upto_e - by_expert(groups))
    src = blk * GROUPS_PER_BLOCK + by_block(by_expert(first)) + (k - before)
    zero_group = GROUPS_PER_BLOCK - 1
    spare = n_blocks * GROUPS_PER_BLOCK + (s_tile % 2) * TILE_GROUPS + slot % TILE_GROUPS
    gsrc = jnp.where(real, src, zero_group)
    gdst = jnp.where(real, src, spare)
    tail = jnp.concatenate([jnp.sum(groups, axis=1), jnp.zeros((1,), jnp.int32)])
    n_real = jnp.sum(real.reshape(r_max, TILE_GROUPS).astype(jnp.int32), axis=1)
    return tile_e, next_e, n_real, n_tiles.reshape(1), gsrc, gdst, tail


def _layer(h, norm_mix_g, w_in, conv_w, sgu_ln_g, sgu_ln_b, sgu_w_s, sgu_b_s, w_up_conv,
           w_up_sgu, w_out, norm_ffn_g, w_rg, b_rg, w_re, b_re, w_eg, w_eu, w_ed, out_g):
    t, d = h.shape
    conv_width = conv_w.shape[1]
    sgu_width = sgu_ln_g.shape[0]
    xn, yb = _sgu_branch(h, norm_mix_g, w_in, 3 * conv_width, sgu_ln_g, sgu_ln_b, sgu_w_s, sgu_b_s)
    gate_col0 = 3 * conv_width + 2 * sgu_width
    ya, (wg_bf, wua_bf, wub_bf, wout_bf) = _conv_branch(
        xn, w_in, conv_w, conv_width,
        to_cast=[(w_in, gate_col0, w_in.shape[1] - gate_col0, 1024),
                 (w_up_conv, 0, d, d), (w_up_sgu, 0, d, d), (w_out, 0, d, d)])
    m, eg_bf, eu_bf, ed_bf = _upgate(xn, ya, yb, wg_bf, wua_bf, wub_bf, w_eg, w_eu, w_ed)
    h1 = _outproj(m, wout_bf, h)
    xs, meta, counts = _route_sort(h1, norm_ffn_g, w_rg, b_rg, w_re, b_re)
    ys = _experts(_expert_plan(counts, t // ROUTE_BLOCK), xs, eg_bf, eu_bf, ed_bf)
    cmeta = jnp.pad(meta[0:2].T, ((0, 0), (0, LANES - 2)))
    return _combine(ys, h1, cmeta, out_g)


def kernel(x, norm_mix_g, w_in, conv_w, sgu_ln_g, sgu_ln_b, sgu_w_s, sgu_b_s, w_up_conv, w_up_sgu, w_out, norm_ffn_g, w_router_group, b_router_group, w_router_expert, b_router_expert, w_exp_gate, w_exp_up, w_exp_down, norm_final_g):
    bsz, s, d = x.shape
    depth = w_in.shape[0]
    assert bsz == 1 and depth == 1, "causal conv carry and the fused final norm assume one sequence, one layer"
    assert s % ROUTE_BLOCK == 0
    out = _layer(x.reshape(s, d), norm_mix_g[0], w_in[0], conv_w[0], sgu_ln_g[0], sgu_ln_b[0],
                 sgu_w_s[0], sgu_b_s[0], w_up_conv[0], w_up_sgu[0], w_out[0], norm_ffn_g[0],
                 w_router_group[0], b_router_group[0], w_router_expert[0], b_router_expert[0],
                 w_exp_gate[0], w_exp_up[0], w_exp_down[0], norm_final_g)
    return out.reshape(bsz, s, d)
```

```python
import functools

import jax
import jax.numpy as jnp
from jax import lax
from jax.experimental import pallas as pl
from jax.experimental.pallas import tpu as pltpu

F32 = jnp.float32
BF16 = jnp.bfloat16

EPS = 1e-6
CHUNK = 64
CONV_K = 3
SGU_HEADS = 8
SGU_BLOCK = 128
N_GROUPS = 4
EXPERTS_PER_GROUP = 4
N_EXPERTS = N_GROUPS * EXPERTS_PER_GROUP
ROUTER_ROWS = 32
LANES = 128

VMEM_LIMIT_BYTES = 56 * 1024 * 1024

ROUTE_BLOCK = 512
GROUP = 16
LOCAL_ROWS = -(-(2 * ROUTE_BLOCK + N_EXPERTS * (GROUP - 1)) // 256) * 256
GROUPS_PER_BLOCK = LOCAL_ROWS // GROUP
MOE_TILE = 1024
TILE_GROUPS = MOE_TILE // GROUP
MXU_ROWS = 256


def _params(n_axes):
    return pltpu.CompilerParams(
        dimension_semantics=("arbitrary",) * n_axes,
        vmem_limit_bytes=VMEM_LIMIT_BYTES)


def _dot(a, b):
    return jnp.dot(a, b, preferred_element_type=F32)


def _rms_scale(x, g):
    ms = jnp.mean(x * x, axis=-1, keepdims=True)
    return x * lax.rsqrt(ms + EPS) * g


def _conv_kernel(cast_chunks, xn_ref, wb_ref, wc_ref, wh_ref, cw_ref, *rest):
    n_in = sum(cast_chunks)
    cast_in, o_ref = rest[:n_in], rest[n_in]
    cast_out = rest[n_in + 1:n_in + 1 + len(cast_chunks)]
    wbf_ref, carry_ref = rest[n_in + 1 + len(cast_chunks):]
    i = pl.program_id(1)
    tn = wb_ref.shape[1]
    tm = xn_ref.shape[0]
    src = iter(cast_in)
    for dst, n_chunks in zip(cast_out, cast_chunks):
        wc = dst.shape[1] // n_chunks
        for k in range(n_chunks):
            dst[:, k * wc:(k + 1) * wc] = next(src)[...].astype(BF16)

    @pl.when(i == 0)
    def _():
        wbf_ref[:, 0:tn] = wb_ref[...].astype(BF16)
        wbf_ref[:, tn:2 * tn] = wc_ref[...].astype(BF16)
        wbf_ref[:, 2 * tn:3 * tn] = wh_ref[...].astype(BF16)
        carry_ref[...] = jnp.zeros_like(carry_ref)

    proj = _dot(xn_ref[...], wbf_ref[...])
    b = proj[:, 0:tn]
    p = proj[:, tn:2 * tn] * proj[:, 2 * tn:3 * tn]
    prev = carry_ref[...]
    carry_ref[...] = p[tm - 8:tm, :]
    row = lax.broadcasted_iota(jnp.int32, p.shape, 0)
    p1 = jnp.where(row == 0, prev[7:8, :], pltpu.roll(p, 1, axis=0))
    p2 = jnp.where(row == 0, prev[6:7, :],
                   jnp.where(row == 1, prev[7:8, :], pltpu.roll(p, 2, axis=0)))
    cw = cw_ref[...]
    y = b * (cw[0:1, :] * p2 + cw[1:2, :] * p1 + cw[2:3, :] * p)
    o_ref[...] = y.astype(o_ref.dtype)


def _conv_branch(xn, w_in, conv_w, width, to_cast, tm=1024, tn=256):
    t, d = xn.shape
    nj = width // tn
    ni = t // tm
    steps = nj * ni
    cast_args, cast_in_specs, cast_out_specs, cast_shapes, cast_chunks = [], [], [], [], []
    for arr, col0, n_cols, chunk in to_cast:
        rows = arr.shape[0] // steps
        assert rows * steps == arr.shape[0] and rows % 16 == 0
        assert col0 % chunk == 0 and n_cols % chunk == 0
        for k in range(n_cols // chunk):
            cast_args.append(arr)
            cast_in_specs.append(
                pl.BlockSpec((rows, chunk), lambda j, i, c=col0 // chunk + k: (j * ni + i, c)))
        cast_chunks.append(n_cols // chunk)
        cast_out_specs.append(pl.BlockSpec((rows, n_cols), lambda j, i: (j * ni + i, 0)))
        cast_shapes.append(jax.ShapeDtypeStruct((arr.shape[0], n_cols), BF16))
    outs = pl.pallas_call(
        functools.partial(_conv_kernel, tuple(cast_chunks)),
        grid=(nj, ni),
        in_specs=[pl.BlockSpec((tm, d), lambda j, i: (i, 0)),
                  pl.BlockSpec((d, tn), lambda j, i: (0, j)),
                  pl.BlockSpec((d, tn), lambda j, i: (0, nj + j)),
                  pl.BlockSpec((d, tn), lambda j, i: (0, 2 * nj + j)),
                  pl.BlockSpec((CONV_K, tn), lambda j, i: (0, j))] + cast_in_specs,
        out_specs=[pl.BlockSpec((tm, tn), lambda j, i: (i, j))] + cast_out_specs,
        out_shape=[jax.ShapeDtypeStruct((t, width), BF16)] + cast_shapes,
        scratch_shapes=[pltpu.VMEM((d, 3 * tn), BF16),
                        pltpu.VMEM((8, tn), F32)],
        compiler_params=_params(2),
        name="conv_branch",
    )(xn, w_in, w_in, w_in, conv_w, *cast_args)
    return outs[0], outs[1:]


def _sgu_kernel(x_ref, g_ref, wu_ref, wv_ref, lng_ref, lnb_ref, ws_ref, bsx_ref, xn_ref, o_ref, wbf_ref):
    tm = x_ref.shape[0]
    w = o_ref.shape[1]
    hd = w // SGU_HEADS

    @pl.when(pl.program_id(0) == 0)
    def _():
        wbf_ref[:, 0:w] = wu_ref[...].astype(BF16)
        wbf_ref[:, w:2 * w] = wv_ref[...].astype(BF16)

    xn = _rms_scale(x_ref[...], g_ref[...]).astype(BF16)
    xn_ref[...] = xn
    gz = jax.nn.gelu(_dot(xn, wbf_ref[...]))
    v = gz[:, w:2 * w]
    mu = jnp.mean(v, axis=-1, keepdims=True)
    vc = v - mu
    var = jnp.mean(vc * vc, axis=-1, keepdims=True)
    vn = (vc * lax.rsqrt(var + EPS) * lng_ref[...] + lnb_ref[...]).astype(BF16)
    ii = lax.broadcasted_iota(jnp.int32, (SGU_BLOCK, SGU_BLOCK), 0)
    jj = lax.broadcasted_iota(jnp.int32, (SGU_BLOCK, SGU_BLOCK), 1)
    mask = (jj // CHUNK) <= (ii // CHUNK)
    for h in range(SGU_HEADS):
        wm = jnp.where(mask, ws_ref[h], 0.0).astype(BF16)
        cs = slice(h * hd, (h + 1) * hd)
        for n in range(tm // SGU_BLOCK):
            rs = slice(n * SGU_BLOCK, (n + 1) * SGU_BLOCK)
            vm = _dot(wm, vn[rs, cs]) + bsx_ref[:, cs]
            o_ref[rs, cs] = (gz[rs, cs] * vm).astype(o_ref.dtype)


def _sgu_branch(x, g, w_in, col0, ln_g, ln_b, w_s, b_s, tm=512):
    t, d = x.shape
    w = ln_g.shape[0]
    hd = w // SGU_HEADS
    assert col0 % w == 0
    c0 = col0 // w
    bsx = jnp.repeat(b_s.T, hd, axis=1)
    once = pl.Buffered(1)
    return pl.pallas_call(
        _sgu_kernel,
        grid=(t // tm,),
        in_specs=[pl.BlockSpec((tm, d), lambda i: (i, 0)),
                  pl.BlockSpec((1, d), lambda i: (0, 0)),
                  pl.BlockSpec((d, w), lambda i: (0, c0), pipeline_mode=once),
                  pl.BlockSpec((d, w), lambda i: (0, c0 + 1), pipeline_mode=once),
                  pl.BlockSpec((1, w), lambda i: (0, 0)),
                  pl.BlockSpec((1, w), lambda i: (0, 0)),
                  pl.BlockSpec((SGU_HEADS, SGU_BLOCK, SGU_BLOCK), lambda i: (0, 0, 0)),
                  pl.BlockSpec((SGU_BLOCK, w), lambda i: (0, 0))],
        out_specs=[pl.BlockSpec((tm, d), lambda i: (i, 0)),
                   pl.BlockSpec((tm, w), lambda i: (i, 0))],
        out_shape=[jax.ShapeDtypeStruct((t, d), BF16),
                   jax.ShapeDtypeStruct((t, w), BF16)],
        scratch_shapes=[pltpu.VMEM((d, 2 * w), BF16)],
        compiler_params=_params(1),
        name="sgu_branch",
    )(x, g.reshape(1, d), w_in, w_in, ln_g.reshape(1, w), ln_b.reshape(1, w), w_s, bsx)


def _upgate_kernel(xn_ref, ya_ref, yb_ref, wgc_ref, wgs_ref, wua_ref, wub_ref, eg_ref, eu_ref, ed_ref,
                   o_ref, eg_bf_ref, eu_bf_ref, ed_bf_ref):
    eg_bf_ref[...] = eg_ref[...].astype(BF16)
    eu_bf_ref[...] = eu_ref[...].astype(BF16)
    ed_bf_ref[...] = ed_ref[...].astype(BF16)

    xn = xn_ref[...]
    m = (jax.nn.sigmoid(_dot(xn, wgc_ref[...])) * _dot(ya_ref[...], wua_ref[...])
         + jax.nn.sigmoid(_dot(xn, wgs_ref[...])) * _dot(yb_ref[...], wub_ref[...]))
    o_ref[...] = m.astype(o_ref.dtype)


def _upgate(xn, ya, yb, w_gates, w_up_a, w_up_b, w_eg, w_eu, w_ed, tm=1024, tn=512):
    t, d = xn.shape
    wa = ya.shape[1]
    wb = yb.shape[1]
    dout = w_up_a.shape[1]
    c0 = 0
    nj = dout // tn
    ni = t // tm
    n_e, d_e, f_e = w_eg.shape
    up_rows = n_e * d_e // (nj * ni)
    down_rows = n_e * f_e // (nj * ni)
    assert up_rows * nj * ni == n_e * d_e and up_rows % 16 == 0
    assert down_rows * nj * ni == n_e * f_e and down_rows % 16 == 0
    up_spec = pl.BlockSpec((up_rows, f_e), lambda j, i: (j * ni + i, 0))
    down_spec = pl.BlockSpec((down_rows, d_e), lambda j, i: (j * ni + i, 0))
    m, eg_bf, eu_bf, ed_bf = pl.pallas_call(
        _upgate_kernel,
        grid=(nj, ni),
        in_specs=[pl.BlockSpec((tm, d), lambda j, i: (i, 0)),
                  pl.BlockSpec((tm, wa), lambda j, i: (i, 0)),
                  pl.BlockSpec((tm, wb), lambda j, i: (i, 0)),
                  pl.BlockSpec((d, tn), lambda j, i: (0, c0 + j)),
                  pl.BlockSpec((d, tn), lambda j, i: (0, c0 + nj + j)),
                  pl.BlockSpec((wa, tn), lambda j, i: (0, j)),
                  pl.BlockSpec((wb, tn), lambda j, i: (0, j)),
                  up_spec, up_spec, down_spec],
        out_specs=[pl.BlockSpec((tm, tn), lambda j, i: (i, j)), up_spec, up_spec, down_spec],
        out_shape=[jax.ShapeDtypeStruct((t, dout), BF16),
                   jax.ShapeDtypeStruct((n_e * d_e, f_e), BF16),
                   jax.ShapeDtypeStruct((n_e * d_e, f_e), BF16),
                   jax.ShapeDtypeStruct((n_e * f_e, d_e), BF16)],
        compiler_params=_params(2),
        name="upgate",
    )(xn, ya, yb, w_gates, w_gates, w_up_a, w_up_b,
      w_eg.reshape(n_e * d_e, f_e), w_eu.reshape(n_e * d_e, f_e), w_ed.reshape(n_e * f_e, d_e))
    return (m, eg_bf.reshape(n_e, d_e, f_e), eu_bf.reshape(n_e, d_e, f_e),
            ed_bf.reshape(n_e, f_e, d_e))


def _outproj_kernel(m_ref, w_ref, x_ref, o_ref):
    o_ref[...] = x_ref[...] + _dot(m_ref[...], w_ref[...])


def _outproj(m, w_out_bf, x, tm=512):
    t, d = m.shape
    dout = w_out_bf.shape[1]
    return pl.pallas_call(
        _outproj_kernel,
        grid=(t // tm,),
        in_specs=[pl.BlockSpec((tm, d), lambda i: (i, 0)),
                  pl.BlockSpec((d, dout), lambda i: (0, 0), pipeline_mode=pl.Buffered(1)),
                  pl.BlockSpec((tm, dout), lambda i: (i, 0))],
        out_specs=pl.BlockSpec((tm, dout), lambda i: (i, 0)),
        out_shape=jax.ShapeDtypeStruct((t, dout), F32),
        compiler_params=_params(1),
        name="outproj",
    )(m, w_out_bf, x)


def _argmax_rows(rows):
    best = rows[0]
    idx = jnp.zeros(rows[0].shape, jnp.int32)
    for k in range(1, len(rows)):
        better = rows[k] > best
        best = jnp.where(better, rows[k], best)
        idx = jnp.where(better, k, idx)
    return best, idx


def _softmax_rows(rows):
    mx = functools.reduce(jnp.maximum, rows)
    ex = [jnp.exp(r - mx) for r in rows]
    den = functools.reduce(lambda a, b: a + b, ex)
    return [e / den for e in ex]


def _route_sort_kernel(h_ref, g_ref, wr_ref, br_ref, xs_ref, meta_ref, cnt_ref, before_ref):
    tm = h_ref.shape[0]
    xn = _rms_scale(h_ref[...], g_ref[...])
    xn_hi = xn.astype(BF16)
    xn_lo = (xn - xn_hi.astype(F32)).astype(BF16)
    wr = wr_ref[...]
    wr_hi = wr.astype(BF16)
    wr_lo = (wr - wr_hi.astype(F32)).astype(BF16)
    nt_dims = (((1,), (1,)), ((), ()))
    lt = (lax.dot_general(wr_hi, xn_hi, nt_dims, preferred_element_type=F32)
          + lax.dot_general(wr_hi, xn_lo, nt_dims, preferred_element_type=F32)
          + lax.dot_general(wr_lo, xn_hi, nt_dims, preferred_element_type=F32)) + br_ref[...]
    pgs = _softmax_rows([lt[k:k + 1, :] for k in range(N_GROUPS)])
    pg, gi = _argmax_rows(pgs)
    sel = []
    for k in range(EXPERTS_PER_GROUP):
        r = jnp.zeros_like(pg)
        for g in range(N_GROUPS):
            row = N_GROUPS + g * EXPERTS_PER_GROUP + k
            r = jnp.where(gi == g, lt[row:row + 1, :], r)
        sel.append(r)
    pes = _softmax_rows(sel)
    p1, e1 = _argmax_rows(pes)
    rest = [jnp.where(e1 == k, -1.0, pes[k]) for k in range(EXPERTS_PER_GROUP)]
    p2, e2 = _argmax_rows(rest)
    den = p1 + p2
    w1 = pg * (p1 / den)
    w2 = pg * (p2 / den)
    lo = jnp.minimum(e1, e2)
    hi = jnp.maximum(e1, e2)
    w_lo = jnp.where(e1 < e2, w1, w2)
    w_hi = jnp.where(e1 < e2, w2, w1)
    ea = gi * EXPERTS_PER_GROUP + lo
    eb = gi * EXPERTS_PER_GROUP + hi

    erow = lax.broadcasted_iota(jnp.int32, (N_EXPERTS, tm), 0)
    oh_a = (erow == ea).astype(F32)
    oh_b = (erow == eb).astype(F32)

    @pl.when(pl.program_id(0) == 0)
    def _():
        a = lax.broadcasted_iota(jnp.int32, (tm, tm), 0)
        b = lax.broadcasted_iota(jnp.int32, (tm, tm), 1)
        before_ref[...] = (a < b).astype(BF16)

    cum = _dot((oh_a + oh_b).astype(BF16), before_ref[...])
    cnt = jnp.sum(oh_a + oh_b, axis=1, keepdims=True)
    padded = jnp.floor((cnt + (GROUP - 1)) * (1.0 / GROUP)) * GROUP
    pos_a = jnp.sum(oh_a * cum + jnp.where(erow < ea, padded, 0.0), axis=0, keepdims=True)
    pos_b = jnp.sum(oh_b * cum + jnp.where(erow < eb, padded, 0.0), axis=0, keepdims=True)

    d = h_ref.shape[1]
    q = lax.broadcasted_iota(jnp.int32, (xs_ref.shape[0], tm), 0)
    perm_a = jnp.where(q == pos_a.astype(jnp.int32), 1.0, 0.0).astype(BF16)
    perm_b = jnp.where(q == pos_b.astype(jnp.int32), 1.0, 0.0).astype(BF16)
    xs_ref[:, 0:d] = _dot(perm_a + perm_b, xn_hi).astype(xs_ref.dtype)

    def gate_rows(w):
        hi = w.astype(BF16).astype(F32)
        lo = w - hi
        k = lax.broadcasted_iota(jnp.int32, (LANES, tm), 0)
        return jnp.where(k == 0, hi, jnp.where(k == 1, lo, 0.0)).astype(BF16)

    gates = (lax.dot_general(perm_a, gate_rows(w_lo), nt_dims, preferred_element_type=F32)
             + lax.dot_general(perm_b, gate_rows(w_hi), nt_dims, preferred_element_type=F32))
    xs_ref[:, d:d + LANES] = gates.astype(xs_ref.dtype)

    cnt_ref[...] = jnp.broadcast_to(cnt, cnt_ref.shape)
    meta_ref[0:1, :] = pos_a
    meta_ref[1:2, :] = pos_b
    meta_ref[2:8, :] = jnp.zeros((6, tm), F32)


def _route_sort(h1, g, w_rg, b_rg, w_re, b_re):
    t, d = h1.shape
    tm = ROUTE_BLOCK
    nb = t // tm
    n_log = w_rg.shape[1] + w_re.shape[1]
    wr = jnp.concatenate([w_rg, w_re], axis=1).T
    wr = jnp.pad(wr, ((0, ROUTER_ROWS - n_log), (0, 0)))
    br = jnp.pad(jnp.concatenate([b_rg, b_re]), (0, ROUTER_ROWS - n_log)).reshape(ROUTER_ROWS, 1)
    return pl.pallas_call(
        _route_sort_kernel,
        grid=(nb,),
        in_specs=[pl.BlockSpec((tm, d), lambda i: (i, 0)),
                  pl.BlockSpec((1, d), lambda i: (0, 0)),
                  pl.BlockSpec((ROUTER_ROWS, d), lambda i: (0, 0)),
                  pl.BlockSpec((ROUTER_ROWS, 1), lambda i: (0, 0))],
        out_specs=[pl.BlockSpec((LOCAL_ROWS, d + LANES), lambda i: (i, 0)),
                   pl.BlockSpec((8, tm), lambda i: (0, i)),
                   pl.BlockSpec((N_EXPERTS, LANES), lambda i: (i, 0))],
        out_shape=[jax.ShapeDtypeStruct((nb * LOCAL_ROWS, d + LANES), BF16),
                   jax.ShapeDtypeStruct((8, t), F32),
                   jax.ShapeDtypeStruct((nb * N_EXPERTS, LANES), F32)],
        scratch_shapes=[pltpu.VMEM((tm, tm), BF16)],
        compiler_params=_params(1),
        name="route_sort",
    )(h1, g.reshape(1, d), wr, br)


def _group_copy(src_ref, s_group, dst_ref, d_group, sem):
    return pltpu.make_async_copy(src_ref.at[pl.ds(pl.multiple_of(s_group * GROUP, GROUP), GROUP), :],
                                 dst_ref.at[pl.ds(pl.multiple_of(d_group * GROUP, GROUP), GROUP), :], sem)


def _expert_kernel(te_ref, nxt_ref, nreal_ref, nt_ref, gsrc_ref, gdst_ref, tail_ref,
                   xs_ref, wg_ref, wu_ref, wd_ref, ys_ref,
                   xbuf_ref, ybuf_ref, zero_ref, wgu_buf_ref, wd_buf_ref,
                   cur_ref, gsem, ssem, zsem, wsem):
    r = pl.program_id(0)
    nt = nt_ref[0]

    def gather(q, slot):
        for i in range(TILE_GROUPS):
            _group_copy(xs_ref, gsrc_ref[q * TILE_GROUPS + i], xbuf_ref, slot * TILE_GROUPS + i,
                        gsem.at[slot]).start(priority=1)

    def weight_copies(e, wslot):
        f = wg_ref.shape[2]
        return (pltpu.make_async_copy(wg_ref.at[e], wgu_buf_ref.at[wslot, :, pl.ds(0, f)], wsem.at[wslot]),
                pltpu.make_async_copy(wu_ref.at[e], wgu_buf_ref.at[wslot, :, pl.ds(f, f)], wsem.at[wslot]),
                pltpu.make_async_copy(wd_ref.at[e], wd_buf_ref.at[wslot], wsem.at[wslot]))

    def tile_wait(src, dst, sem):
        pltpu.make_async_copy(src.at[pl.ds(0, MOE_TILE), :], dst.at[pl.ds(0, MOE_TILE), :], sem).wait()

    @pl.when(r == 0)
    def _():
        cur_ref[0] = -1
        cur_ref[1] = -1
        for cp in weight_copies(te_ref[0], 0):
            cp.start()
        gather(0, 0)
        zero_ref[...] = jnp.zeros_like(zero_ref)
        ybuf_ref[...] = jnp.zeros_like(ybuf_ref)
        n_blocks = tail_ref.shape[0]

        def fill(make):
            def body(g, c):
                make(g)
                return c
            return body

        for blk in range(n_blocks):
            lax.fori_loop(tail_ref[blk], GROUPS_PER_BLOCK, fill(
                lambda g, blk=blk: _group_copy(zero_ref, 0, ys_ref, blk * GROUPS_PER_BLOCK + g, zsem).start()), 0)
        for blk in range(n_blocks):
            lax.fori_loop(tail_ref[blk], GROUPS_PER_BLOCK, fill(
                lambda g, blk=blk: _group_copy(zero_ref, 0, ys_ref, blk * GROUPS_PER_BLOCK + g, zsem).wait()), 0)

    @pl.when(r < nt)
    def _():
        e = te_ref[r]
        slot = r % 2

        gather(jnp.minimum(r + 1, nt - 1), 1 - slot)

        @pl.when(cur_ref[0] != e)
        def _():
            wslot = (cur_ref[1] + 1) % 2
            for cp in weight_copies(e, wslot):
                cp.wait()
            cur_ref[0] = e
            cur_ref[1] = cur_ref[1] + 1

            @pl.when(nxt_ref[r] >= 0)
            def _():
                for cp in weight_copies(nxt_ref[r], 1 - wslot):
                    cp.start()

        wslot = cur_ref[1] % 2
        tile_wait(xs_ref, xbuf_ref, gsem.at[slot])

        @pl.when(r >= 2)
        def _():
            tile_wait(ybuf_ref, ys_ref, ssem.at[slot])

        d = ybuf_ref.shape[1]
        f = wd_buf_ref.shape[1]
        for part in range(MOE_TILE // MXU_ROWS):
            @pl.when(nreal_ref[r] > part * (MXU_ROWS // GROUP))
            def _(part=part):
                rows = pl.ds(pl.multiple_of(slot * MOE_TILE + part * MXU_ROWS, MXU_ROWS), MXU_ROWS)
                x = xbuf_ref[rows, 0:d]
                gate_parts = xbuf_ref[rows, d:d + LANES].astype(F32)
                gate = gate_parts[:, 0:1] + gate_parts[:, 1:2]
                gu = _dot(x, wgu_buf_ref[wslot])
                hid = (jax.nn.silu(gu[:, 0:f]) * gu[:, f:2 * f]).astype(BF16)
                ybuf_ref[rows, :] = (_dot(hid, wd_buf_ref[wslot]) * gate).astype(ybuf_ref.dtype)

        for i in range(TILE_GROUPS):
            _group_copy(ybuf_ref, slot * TILE_GROUPS + i, ys_ref, gdst_ref[r * TILE_GROUPS + i],
                        ssem.at[slot]).start(priority=1)

        @pl.when(r == nt - 1)
        def _():
            tile_wait(xs_ref, xbuf_ref, gsem.at[1 - slot])
            tile_wait(ybuf_ref, ys_ref, ssem.at[slot])

            @pl.when(r >= 1)
            def _():
                tile_wait(ybuf_ref, ys_ref, ssem.at[1 - slot])


def _experts(plan, xs, w_gate, w_up, w_down):
    d = w_gate.shape[1]
    f = w_gate.shape[2]
    tile_e, next_e, n_real, n_tiles, gsrc, gdst, tail = plan
    r_max = tile_e.shape[0]
    n_blocks = tail.shape[0]
    any_spec = pl.BlockSpec(memory_space=pl.ANY)
    grid_spec = pltpu.PrefetchScalarGridSpec(
        num_scalar_prefetch=7,
        grid=(r_max,),
        in_specs=[any_spec, any_spec, any_spec, any_spec],
        out_specs=any_spec,
        scratch_shapes=[pltpu.VMEM((2 * MOE_TILE, d + LANES), BF16),
                        pltpu.VMEM((2 * MOE_TILE, d), BF16),
                        pltpu.VMEM((GROUP, d), BF16),
                        pltpu.VMEM((2, d, 2 * f), BF16),
                        pltpu.VMEM((2, f, d), BF16),
                        pltpu.SMEM((2,), jnp.int32),
                        pltpu.SemaphoreType.DMA((2,)),
                        pltpu.SemaphoreType.DMA((2,)),
                        pltpu.SemaphoreType.DMA(()),
                        pltpu.SemaphoreType.DMA((2,))],
    )
    return pl.pallas_call(
        _expert_kernel,
        grid_spec=grid_spec,
        out_shape=jax.ShapeDtypeStruct((n_blocks * LOCAL_ROWS, d), BF16),
        compiler_params=_params(1),
        name="experts",
    )(tile_e, next_e, n_real, n_tiles, gsrc, gdst, tail, xs, w_gate, w_up, w_down)


def _combine_kernel(ys_ref, h_ref, cm_ref, g_ref, o_ref):
    tm = h_ref.shape[0]
    cm = cm_ref[...]
    q = lax.broadcasted_iota(jnp.int32, (tm, ys_ref.shape[0]), 1)
    sel = jnp.where((q == cm[:, 0:1].astype(jnp.int32)) | (q == cm[:, 1:2].astype(jnp.int32)), 1.0, 0.0)
    h2 = h_ref[...] + _dot(sel.astype(BF16), ys_ref[...])
    o_ref[...] = _rms_scale(h2, g_ref[...])


def _combine(ys, h1, cmeta, g):
    t, d = h1.shape
    tm = ROUTE_BLOCK
    return pl.pallas_call(
        _combine_kernel,
        grid=(t // tm,),
        in_specs=[pl.BlockSpec((LOCAL_ROWS, d), lambda i: (i, 0)),
                  pl.BlockSpec((tm, d), lambda i: (i, 0)),
                  pl.BlockSpec((tm, LANES), lambda i: (i, 0)),
                  pl.BlockSpec((1, d), lambda i: (0, 0))],
        out_specs=pl.BlockSpec((tm, d), lambda i: (i, 0)),
        out_shape=jax.ShapeDtypeStruct((t, d), F32),
        compiler_params=_params(1),
        name="combine",
    )(ys, h1, cmeta, g.reshape(1, d))


def _expert_plan(counts, n_blocks):
    cnt = counts[:, 0].astype(jnp.int32).reshape(n_blocks, N_EXPERTS)
    groups = (cnt + GROUP - 1) // GROUP
    first = jnp.cumsum(groups, axis=1) - groups
    upto = jnp.cumsum(groups, axis=0)
    per_expert = upto[-1]
    tiles_e = (per_expert + TILE_GROUPS - 1) // TILE_GROUPS
    tile_end = jnp.cumsum(tiles_e)
    n_tiles = tile_end[-1]
    max_groups = 2 * ROUTE_BLOCK * n_blocks // GROUP + n_blocks * N_EXPERTS
    r_max = max_groups // TILE_GROUPS + N_EXPERTS
    tile_ids = jnp.arange(r_max, dtype=jnp.int32)
    tile = jnp.minimum(tile_ids, n_tiles - 1)
    tile_e = jnp.sum((tile_end[None, :] <= tile[:, None]).astype(jnp.int32), axis=1)
    later = (tile_e[None, :] > tile_e[:, None]) & (tile_ids[None, :] < n_tiles)
    next_e = jnp.min(jnp.where(later, tile_e[None, :], N_EXPERTS), axis=1)
    next_e = jnp.where(next_e == N_EXPERTS, -1, next_e)

    slot = jnp.arange(r_max * TILE_GROUPS, dtype=jnp.int32)
    s_tile = slot // TILE_GROUPS
    oh_e = jnp.repeat(tile_e, TILE_GROUPS)[:, None] == jnp.arange(N_EXPERTS, dtype=jnp.int32)[None, :]

    def by_expert(table):
        return jnp.sum(jnp.where(oh_e[:, None, :], table[None], 0), axis=-1)

    k = slot - by_expert(((tile_end - tiles_e) * TILE_GROUPS)[None, :])[:, 0]
    real = (k < by_expert(per_expert[None, :])[:, 0]) & (s_tile < n_tiles)
    upto_e = by_expert(upto)
    blk = jnp.minimum(jnp.sum((upto_e <= k[:, None]).astype(jnp.int32), axis=1), n_blocks - 1)
    oh_b = blk[:, None] == jnp.arange(n_blocks, dtype=jnp.int32)[None, :]

    def by_block(table_se):
        return jnp.sum(jnp.where(oh_b, table_se, 0), axis=1)

    before = by_block(upto_e - by_expert(groups))
    src = blk * GROUPS_PER_BLOCK + by_block(by_expert(first)) + (k - before)
    zero_group = GROUPS_PER_BLOCK - 1
    spare = n_blocks * GROUPS_PER_BLOCK + (s_tile % 2) * TILE_GROUPS + slot % TILE_GROUPS
    gsrc = jnp.where(real, src, zero_group)
    gdst = jnp.where(real, src, spare)
    n_spare_blocks = -(-2 * TILE_GROUPS // GROUPS_PER_BLOCK)
    tail = jnp.concatenate([jnp.sum(groups, axis=1), jnp.zeros((n_spare_blocks,), jnp.int32)])
    n_real = jnp.sum(real.reshape(r_max, TILE_GROUPS).astype(jnp.int32), axis=1)
    return tile_e, next_e, n_real, n_tiles.reshape(1), gsrc, gdst, tail


def _layer(h, norm_mix_g, w_in, conv_w, sgu_ln_g, sgu_ln_b, sgu_w_s, sgu_b_s, w_up_conv,
           w_up_sgu, w_out, norm_ffn_g, w_rg, b_rg, w_re, b_re, w_eg, w_eu, w_ed, out_g):
    t, d = h.shape
    conv_width = conv_w.shape[1]
    sgu_width = sgu_ln_g.shape[0]
    xn, yb = _sgu_branch(h, norm_mix_g, w_in, 3 * conv_width, sgu_ln_g, sgu_ln_b, sgu_w_s, sgu_b_s)
    gate_col0 = 3 * conv_width + 2 * sgu_width
    ya, (wg_bf, wua_bf, wub_bf, wout_bf) = _conv_branch(
        xn, w_in, conv_w, conv_width,
        to_cast=[(w_in, gate_col0, w_in.shape[1] - gate_col0, 1024),
                 (w_up_conv, 0, d, d), (w_up_sgu, 0, d, d), (w_out, 0, d, d)])
    m, eg_bf, eu_bf, ed_bf = _upgate(xn, ya, yb, wg_bf, wua_bf, wub_bf, w_eg, w_eu, w_ed)
    h1 = _outproj(m, wout_bf, h)
    xs, meta, counts = _route_sort(h1, norm_ffn_g, w_rg, b_rg, w_re, b_re)
    ys = _experts(_expert_plan(counts, t // ROUTE_BLOCK), xs, eg_bf, eu_bf, ed_bf)
    cmeta = jnp.pad(meta[0:2].T, ((0, 0), (0, LANES - 2)))
    return _combine(ys, h1, cmeta, out_g)


def kernel(x, norm_mix_g, w_in, conv_w, sgu_ln_g, sgu_ln_b, sgu_w_s, sgu_b_s, w_up_conv, w_up_sgu, w_out, norm_ffn_g, w_router_group, b_router_group, w_router_expert, b_router_expert, w_exp_gate, w_exp_up, w_exp_down, norm_final_g):
    bsz, s, d = x.shape
    depth = w_in.shape[0]
    assert bsz == 1 and depth == 1, "causal conv carry and the fused final norm assume one sequence, one layer"
    assert s % ROUTE_BLOCK == 0
    out = _layer(x.reshape(s, d), norm_mix_g[0], w_in[0], conv_w[0], sgu_ln_g[0], sgu_ln_b[0],
                 sgu_w_s[0], sgu_b_s[0], w_up_conv[0], w_up_sgu[0], w_out[0], norm_ffn_g[0],
                 w_router_group[0], b_router_group[0], w_router_expert[0], b_router_expert[0],
                 w_exp_gate[0], w_exp_up[0], w_exp_down[0], norm_final_g)
    return out.reshape(bsz, s, d)
```

```python
import functools

import jax
import jax.numpy as jnp
from jax import lax
from jax.experimental import pallas as pl
from jax.experimental.pallas import tpu as pltpu

F32 = jnp.float32
BF16 = jnp.bfloat16

EPS = 1e-6
CHUNK = 64
CONV_K = 3
SGU_HEADS = 8
SGU_BLOCK = 128
N_GROUPS = 4
EXPERTS_PER_GROUP = 4
N_EXPERTS = N_GROUPS * EXPERTS_PER_GROUP
ROUTER_ROWS = 32
LANES = 128

VMEM_LIMIT_BYTES = 56 * 1024 * 1024

ROUTE_BLOCK = 512
GROUP = 16
LOCAL_ROWS = -(-(2 * ROUTE_BLOCK + N_EXPERTS * (GROUP - 1)) // 256) * 256
GROUPS_PER_BLOCK = LOCAL_ROWS // GROUP
MOE_TILE = 512
TILE_GROUPS = MOE_TILE // GROUP
MXU_ROWS = 256


def _params(n_axes):
    return pltpu.CompilerParams(
        dimension_semantics=("arbitrary",) * n_axes,
        vmem_limit_bytes=VMEM_LIMIT_BYTES)


def _dot(a, b):
    return jnp.dot(a, b, preferred_element_type=F32)


def _rms_scale(x, g):
    ms = jnp.mean(x * x, axis=-1, keepdims=True)
    return x * lax.rsqrt(ms + EPS) * g


def _conv_kernel(cast_chunks, xn_ref, wb_ref, wc_ref, wh_ref, cw_ref, *rest):
    n_in = sum(cast_chunks)
    cast_in, o_ref = rest[:n_in], rest[n_in]
    cast_out = rest[n_in + 1:n_in + 1 + len(cast_chunks)]
    wbf_ref, carry_ref = rest[n_in + 1 + len(cast_chunks):]
    i = pl.program_id(1)
    tn = wb_ref.shape[1]
    tm = xn_ref.shape[0]
    src = iter(cast_in)
    for dst, n_chunks in zip(cast_out, cast_chunks):
        wc = dst.shape[1] // n_chunks
        for k in range(n_chunks):
            dst[:, k * wc:(k + 1) * wc] = next(src)[...].astype(BF16)

    @pl.when(i == 0)
    def _():
        wbf_ref[:, 0:tn] = wb_ref[...].astype(BF16)
        wbf_ref[:, tn:2 * tn] = wc_ref[...].astype(BF16)
        wbf_ref[:, 2 * tn:3 * tn] = wh_ref[...].astype(BF16)
        carry_ref[...] = jnp.zeros_like(carry_ref)

    proj = _dot(xn_ref[...], wbf_ref[...])
    b = proj[:, 0:tn]
    p = proj[:, tn:2 * tn] * proj[:, 2 * tn:3 * tn]
    prev = carry_ref[...]
    carry_ref[...] = p[tm - 8:tm, :]
    row = lax.broadcasted_iota(jnp.int32, p.shape, 0)
    p1 = jnp.where(row == 0, prev[7:8, :], pltpu.roll(p, 1, axis=0))
    p2 = jnp.where(row == 0, prev[6:7, :],
                   jnp.where(row == 1, prev[7:8, :], pltpu.roll(p, 2, axis=0)))
    cw = cw_ref[...]
    y = b * (cw[0:1, :] * p2 + cw[1:2, :] * p1 + cw[2:3, :] * p)
    o_ref[...] = y.astype(o_ref.dtype)


def _conv_branch(xn, w_in, conv_w, width, to_cast, tm=1024, tn=256):
    t, d = xn.shape
    nj = width // tn
    ni = t // tm
    steps = nj * ni
    cast_args, cast_in_specs, cast_out_specs, cast_shapes, cast_chunks = [], [], [], [], []
    for arr, col0, n_cols, chunk in to_cast:
        rows = arr.shape[0] // steps
        assert rows * steps == arr.shape[0] and rows % 16 == 0
        assert col0 % chunk == 0 and n_cols % chunk == 0
        for k in range(n_cols // chunk):
            cast_args.append(arr)
            cast_in_specs.append(
                pl.BlockSpec((rows, chunk), lambda j, i, c=col0 // chunk + k: (j * ni + i, c)))
        cast_chunks.append(n_cols // chunk)
        cast_out_specs.append(pl.BlockSpec((rows, n_cols), lambda j, i: (j * ni + i, 0)))
        cast_shapes.append(jax.ShapeDtypeStruct((arr.shape[0], n_cols), BF16))
    outs = pl.pallas_call(
        functools.partial(_conv_kernel, tuple(cast_chunks)),
        grid=(nj, ni),
        in_specs=[pl.BlockSpec((tm, d), lambda j, i: (i, 0)),
                  pl.BlockSpec((d, tn), lambda j, i: (0, j)),
                  pl.BlockSpec((d, tn), lambda j, i: (0, nj + j)),
                  pl.BlockSpec((d, tn), lambda j, i: (0, 2 * nj + j)),
                  pl.BlockSpec((CONV_K, tn), lambda j, i: (0, j))] + cast_in_specs,
        out_specs=[pl.BlockSpec((tm, tn), lambda j, i: (i, j))] + cast_out_specs,
        out_shape=[jax.ShapeDtypeStruct((t, width), BF16)] + cast_shapes,
        scratch_shapes=[pltpu.VMEM((d, 3 * tn), BF16),
                        pltpu.VMEM((8, tn), F32)],
        compiler_params=_params(2),
        name="conv_branch",
    )(xn, w_in, w_in, w_in, conv_w, *cast_args)
    return outs[0], outs[1:]


def _sgu_kernel(x_ref, g_ref, wu_ref, wv_ref, lng_ref, lnb_ref, ws_ref, bsx_ref, xn_ref, o_ref, wbf_ref):
    tm = x_ref.shape[0]
    w = o_ref.shape[1]
    hd = w // SGU_HEADS

    @pl.when(pl.program_id(0) == 0)
    def _():
        wbf_ref[:, 0:w] = wu_ref[...].astype(BF16)
        wbf_ref[:, w:2 * w] = wv_ref[...].astype(BF16)

    xn = _rms_scale(x_ref[...], g_ref[...]).astype(BF16)
    xn_ref[...] = xn
    gz = jax.nn.gelu(_dot(xn, wbf_ref[...]))
    v = gz[:, w:2 * w]
    mu = jnp.mean(v, axis=-1, keepdims=True)
    vc = v - mu
    var = jnp.mean(vc * vc, axis=-1, keepdims=True)
    vn = (vc * lax.rsqrt(var + EPS) * lng_ref[...] + lnb_ref[...]).astype(BF16)
    ii = lax.broadcasted_iota(jnp.int32, (SGU_BLOCK, SGU_BLOCK), 0)
    jj = lax.broadcasted_iota(jnp.int32, (SGU_BLOCK, SGU_BLOCK), 1)
    mask = (jj // CHUNK) <= (ii // CHUNK)
    for h in range(SGU_HEADS):
        wm = jnp.where(mask, ws_ref[h], 0.0).astype(BF16)
        cs = slice(h * hd, (h + 1) * hd)
        for n in range(tm // SGU_BLOCK):
            rs = slice(n * SGU_BLOCK, (n + 1) * SGU_BLOCK)
            vm = _dot(wm, vn[rs, cs]) + bsx_ref[:, cs]
            o_ref[rs, cs] = (gz[rs, cs] * vm).astype(o_ref.dtype)


def _sgu_branch(x, g, w_in, col0, ln_g, ln_b, w_s, b_s, tm=512):
    t, d = x.shape
    w = ln_g.shape[0]
    hd = w // SGU_HEADS
    assert col0 % w == 0
    c0 = col0 // w
    bsx = jnp.repeat(b_s.T, hd, axis=1)
    once = pl.Buffered(1)
    return pl.pallas_call(
        _sgu_kernel,
        grid=(t // tm,),
        in_specs=[pl.BlockSpec((tm, d), lambda i: (i, 0)),
                  pl.BlockSpec((1, d), lambda i: (0, 0)),
                  pl.BlockSpec((d, w), lambda i: (0, c0), pipeline_mode=once),
                  pl.BlockSpec((d, w), lambda i: (0, c0 + 1), pipeline_mode=once),
                  pl.BlockSpec((1, w), lambda i: (0, 0)),
                  pl.BlockSpec((1, w), lambda i: (0, 0)),
                  pl.BlockSpec((SGU_HEADS, SGU_BLOCK, SGU_BLOCK), lambda i: (0, 0, 0)),
                  pl.BlockSpec((SGU_BLOCK, w), lambda i: (0, 0))],
        out_specs=[pl.BlockSpec((tm, d), lambda i: (i, 0)),
                   pl.BlockSpec((tm, w), lambda i: (i, 0))],
        out_shape=[jax.ShapeDtypeStruct((t, d), BF16),
                   jax.ShapeDtypeStruct((t, w), BF16)],
        scratch_shapes=[pltpu.VMEM((d, 2 * w), BF16)],
        compiler_params=_params(1),
        name="sgu_branch",
    )(x, g.reshape(1, d), w_in, w_in, ln_g.reshape(1, w), ln_b.reshape(1, w), w_s, bsx)


def _upgate_kernel(xn_ref, ya_ref, yb_ref, wgc_ref, wgs_ref, wua_ref, wub_ref, eg_ref, eu_ref, ed_ref,
                   o_ref, eg_bf_ref, eu_bf_ref, ed_bf_ref):
    eg_bf_ref[...] = eg_ref[...].astype(BF16)
    eu_bf_ref[...] = eu_ref[...].astype(BF16)
    ed_bf_ref[...] = ed_ref[...].astype(BF16)

    xn = xn_ref[...]
    m = (jax.nn.sigmoid(_dot(xn, wgc_ref[...])) * _dot(ya_ref[...], wua_ref[...])
         + jax.nn.sigmoid(_dot(xn, wgs_ref[...])) * _dot(yb_ref[...], wub_ref[...]))
    o_ref[...] = m.astype(o_ref.dtype)


def _upgate(xn, ya, yb, w_gates, w_up_a, w_up_b, w_eg, w_eu, w_ed, tm=1024, tn=512):
    t, d = xn.shape
    wa = ya.shape[1]
    wb = yb.shape[1]
    dout = w_up_a.shape[1]
    c0 = 0
    nj = dout // tn
    ni = t // tm
    n_e, d_e, f_e = w_eg.shape
    up_rows = n_e * d_e // (nj * ni)
    down_rows = n_e * f_e // (nj * ni)
    assert up_rows * nj * ni == n_e * d_e and up_rows % 16 == 0
    assert down_rows * nj * ni == n_e * f_e and down_rows % 16 == 0
    up_spec = pl.BlockSpec((up_rows, f_e), lambda j, i: (j * ni + i, 0))
    down_spec = pl.BlockSpec((down_rows, d_e), lambda j, i: (j * ni + i, 0))
    m, eg_bf, eu_bf, ed_bf = pl.pallas_call(
        _upgate_kernel,
        grid=(nj, ni),
        in_specs=[pl.BlockSpec((tm, d), lambda j, i: (i, 0)),
                  pl.BlockSpec((tm, wa), lambda j, i: (i, 0)),
                  pl.BlockSpec((tm, wb), lambda j, i: (i, 0)),
                  pl.BlockSpec((d, tn), lambda j, i: (0, c0 + j)),
                  pl.BlockSpec((d, tn), lambda j, i: (0, c0 + nj + j)),
                  pl.BlockSpec((wa, tn), lambda j, i: (0, j)),
                  pl.BlockSpec((wb, tn), lambda j, i: (0, j)),
                  up_spec, up_spec, down_spec],
        out_specs=[pl.BlockSpec((tm, tn), lambda j, i: (i, j)), up_spec, up_spec, down_spec],
        out_shape=[jax.ShapeDtypeStruct((t, dout), BF16),
                   jax.ShapeDtypeStruct((n_e * d_e, f_e), BF16),
                   jax.ShapeDtypeStruct((n_e * d_e, f_e), BF16),
                   jax.ShapeDtypeStruct((n_e * f_e, d_e), BF16)],
        compiler_params=_params(2),
        name="upgate",
    )(xn, ya, yb, w_gates, w_gates, w_up_a, w_up_b,
      w_eg.reshape(n_e * d_e, f_e), w_eu.reshape(n_e * d_e, f_e), w_ed.reshape(n_e * f_e, d_e))
    return (m, eg_bf.reshape(n_e, d_e, f_e), eu_bf.reshape(n_e, d_e, f_e),
            ed_bf.reshape(n_e, f_e, d_e))


def _outproj_kernel(m_ref, w_ref, x_ref, g_ref, o_ref, hi_ref, lo_ref):
    h1 = x_ref[...] + _dot(m_ref[...], w_ref[...])
    o_ref[...] = h1
    xn = _rms_scale(h1, g_ref[...])
    hi = xn.astype(BF16)
    hi_ref[...] = hi
    lo_ref[...] = (xn - hi.astype(F32)).astype(BF16)


def _outproj(m, w_out_bf, x, g, tm=512):
    t, d = m.shape
    dout = w_out_bf.shape[1]
    row_spec = pl.BlockSpec((tm, dout), lambda i: (i, 0))
    return pl.pallas_call(
        _outproj_kernel,
        grid=(t // tm,),
        in_specs=[pl.BlockSpec((tm, d), lambda i: (i, 0)),
                  pl.BlockSpec((d, dout), lambda i: (0, 0), pipeline_mode=pl.Buffered(1)),
                  row_spec,
                  pl.BlockSpec((1, dout), lambda i: (0, 0))],
        out_specs=[row_spec, row_spec, row_spec],
        out_shape=[jax.ShapeDtypeStruct((t, dout), F32),
                   jax.ShapeDtypeStruct((t, dout), BF16),
                   jax.ShapeDtypeStruct((t, dout), BF16)],
        compiler_params=_params(1),
        name="outproj",
    )(m, w_out_bf, x, g.reshape(1, dout))


def _argmax_rows(rows):
    best = rows[0]
    idx = jnp.zeros(rows[0].shape, jnp.int32)
    for k in range(1, len(rows)):
        better = rows[k] > best
        best = jnp.where(better, rows[k], best)
        idx = jnp.where(better, k, idx)
    return best, idx


def _softmax_rows(rows):
    mx = functools.reduce(jnp.maximum, rows)
    ex = [jnp.exp(r - mx) for r in rows]
    den = functools.reduce(lambda a, b: a + b, ex)
    return [e / den for e in ex]


def _route_sort_kernel(hi_ref, lo_ref, wr_ref, br_ref, xs_ref, meta_ref, cnt_ref, before_ref):
    tm = hi_ref.shape[0]
    xn_hi = hi_ref[...]
    xn_lo = lo_ref[...]
    wr = wr_ref[...]
    wr_hi = wr.astype(BF16)
    wr_lo = (wr - wr_hi.astype(F32)).astype(BF16)
    nt_dims = (((1,), (1,)), ((), ()))
    lt = (lax.dot_general(wr_hi, xn_hi, nt_dims, preferred_element_type=F32)
          + lax.dot_general(wr_hi, xn_lo, nt_dims, preferred_element_type=F32)
          + lax.dot_general(wr_lo, xn_hi, nt_dims, preferred_element_type=F32)) + br_ref[...]
    pgs = _softmax_rows([lt[k:k + 1, :] for k in range(N_GROUPS)])
    pg, gi = _argmax_rows(pgs)
    sel = []
    for k in range(EXPERTS_PER_GROUP):
        r = jnp.zeros_like(pg)
        for g in range(N_GROUPS):
            row = N_GROUPS + g * EXPERTS_PER_GROUP + k
            r = jnp.where(gi == g, lt[row:row + 1, :], r)
        sel.append(r)
    pes = _softmax_rows(sel)
    p1, e1 = _argmax_rows(pes)
    rest = [jnp.where(e1 == k, -1.0, pes[k]) for k in range(EXPERTS_PER_GROUP)]
    p2, e2 = _argmax_rows(rest)
    den = p1 + p2
    w1 = pg * (p1 / den)
    w2 = pg * (p2 / den)
    lo = jnp.minimum(e1, e2)
    hi = jnp.maximum(e1, e2)
    w_lo = jnp.where(e1 < e2, w1, w2)
    w_hi = jnp.where(e1 < e2, w2, w1)
    ea = gi * EXPERTS_PER_GROUP + lo
    eb = gi * EXPERTS_PER_GROUP + hi

    erow = lax.broadcasted_iota(jnp.int32, (N_EXPERTS, tm), 0)
    oh_a = (erow == ea).astype(F32)
    oh_b = (erow == eb).astype(F32)

    @pl.when(pl.program_id(0) == 0)
    def _():
        a = lax.broadcasted_iota(jnp.int32, (tm, tm), 0)
        b = lax.broadcasted_iota(jnp.int32, (tm, tm), 1)
        before_ref[...] = (a < b).astype(BF16)

    cum = _dot((oh_a + oh_b).astype(BF16), before_ref[...])
    cnt = jnp.sum(oh_a + oh_b, axis=1, keepdims=True)
    padded = jnp.floor((cnt + (GROUP - 1)) * (1.0 / GROUP)) * GROUP
    pos_a = jnp.sum(oh_a * cum + jnp.where(erow < ea, padded, 0.0), axis=0, keepdims=True)
    pos_b = jnp.sum(oh_b * cum + jnp.where(erow < eb, padded, 0.0), axis=0, keepdims=True)

    d = hi_ref.shape[1]
    q = lax.broadcasted_iota(jnp.int32, (xs_ref.shape[0], tm), 0)
    perm_a = jnp.where(q == pos_a.astype(jnp.int32), 1.0, 0.0).astype(BF16)
    perm_b = jnp.where(q == pos_b.astype(jnp.int32), 1.0, 0.0).astype(BF16)
    xs_ref[:, 0:d] = _dot(perm_a + perm_b, xn_hi).astype(xs_ref.dtype)

    def gate_rows(w):
        hi = w.astype(BF16).astype(F32)
        lo = w - hi
        k = lax.broadcasted_iota(jnp.int32, (LANES, tm), 0)
        return jnp.where(k == 0, hi, jnp.where(k == 1, lo, 0.0)).astype(BF16)

    gates = (lax.dot_general(perm_a, gate_rows(w_lo), nt_dims, preferred_element_type=F32)
             + lax.dot_general(perm_b, gate_rows(w_hi), nt_dims, preferred_element_type=F32))
    xs_ref[:, d:d + LANES] = gates.astype(xs_ref.dtype)

    cnt_ref[...] = jnp.broadcast_to(cnt, cnt_ref.shape)
    meta_ref[0:1, :] = pos_a
    meta_ref[1:2, :] = pos_b
    meta_ref[2:8, :] = jnp.zeros((6, tm), F32)


def _route_sort(xn_hi, xn_lo, w_rg, b_rg, w_re, b_re):
    t, d = xn_hi.shape
    tm = ROUTE_BLOCK
    nb = t // tm
    n_log = w_rg.shape[1] + w_re.shape[1]
    wr = jnp.concatenate([w_rg, w_re], axis=1).T
    wr = jnp.pad(wr, ((0, ROUTER_ROWS - n_log), (0, 0)))
    br = jnp.pad(jnp.concatenate([b_rg, b_re]), (0, ROUTER_ROWS - n_log)).reshape(ROUTER_ROWS, 1)
    return pl.pallas_call(
        _route_sort_kernel,
        grid=(nb,),
        in_specs=[pl.BlockSpec((tm, d), lambda i: (i, 0)),
                  pl.BlockSpec((tm, d), lambda i: (i, 0)),
                  pl.BlockSpec((ROUTER_ROWS, d), lambda i: (0, 0)),
                  pl.BlockSpec((ROUTER_ROWS, 1), lambda i: (0, 0))],
        out_specs=[pl.BlockSpec((LOCAL_ROWS, d + LANES), lambda i: (i, 0)),
                   pl.BlockSpec((8, tm), lambda i: (0, i)),
                   pl.BlockSpec((N_EXPERTS, LANES), lambda i: (i, 0))],
        out_shape=[jax.ShapeDtypeStruct((nb * LOCAL_ROWS, d + LANES), BF16),
                   jax.ShapeDtypeStruct((8, t), F32),
                   jax.ShapeDtypeStruct((nb * N_EXPERTS, LANES), F32)],
        scratch_shapes=[pltpu.VMEM((tm, tm), BF16)],
        compiler_params=_params(1),
        name="route_sort",
    )(xn_hi, xn_lo, wr, br)


def _group_copy(src_ref, s_group, dst_ref, d_group, sem):
    return pltpu.make_async_copy(src_ref.at[pl.ds(pl.multiple_of(s_group * GROUP, GROUP), GROUP), :],
                                 dst_ref.at[pl.ds(pl.multiple_of(d_group * GROUP, GROUP), GROUP), :], sem)


def _expert_kernel(te_ref, nxt_ref, nreal_ref, nt_ref, gsrc_ref, gdst_ref, tail_ref,
                   xs_ref, wg_ref, wu_ref, wd_ref, ys_ref,
                   xbuf_ref, ybuf_ref, zero_ref, wgu_buf_ref, wd_buf_ref,
                   cur_ref, gsem, ssem, zsem, wsem):
    r = pl.program_id(0)
    nt = nt_ref[0]

    def gather(q, slot):
        for i in range(TILE_GROUPS):
            _group_copy(xs_ref, gsrc_ref[q * TILE_GROUPS + i], xbuf_ref, slot * TILE_GROUPS + i,
                        gsem.at[slot]).start(priority=1)

    def weight_copies(e, wslot):
        f = wg_ref.shape[2]
        return (pltpu.make_async_copy(wg_ref.at[e], wgu_buf_ref.at[wslot, :, pl.ds(0, f)], wsem.at[wslot]),
                pltpu.make_async_copy(wu_ref.at[e], wgu_buf_ref.at[wslot, :, pl.ds(f, f)], wsem.at[wslot]),
                pltpu.make_async_copy(wd_ref.at[e], wd_buf_ref.at[wslot], wsem.at[wslot]))

    def tile_wait(src, dst, sem):
        pltpu.make_async_copy(src.at[pl.ds(0, MOE_TILE), :], dst.at[pl.ds(0, MOE_TILE), :], sem).wait()

    @pl.when(r == 0)
    def _():
        cur_ref[0] = -1
        cur_ref[1] = -1
        for cp in weight_copies(te_ref[0], 0):
            cp.start()
        gather(0, 0)
        zero_ref[...] = jnp.zeros_like(zero_ref)
        ybuf_ref[...] = jnp.zeros_like(ybuf_ref)
        n_blocks = tail_ref.shape[0]

        def fill(make):
            def body(g, c):
                make(g)
                return c
            return body

        for blk in range(n_blocks):
            lax.fori_loop(tail_ref[blk], GROUPS_PER_BLOCK, fill(
                lambda g, blk=blk: _group_copy(zero_ref, 0, ys_ref, blk * GROUPS_PER_BLOCK + g, zsem).start()), 0)
        for blk in range(n_blocks):
            lax.fori_loop(tail_ref[blk], GROUPS_PER_BLOCK, fill(
                lambda g, blk=blk: _group_copy(zero_ref, 0, ys_ref, blk * GROUPS_PER_BLOCK + g, zsem).wait()), 0)

    @pl.when(r < nt)
    def _():
        e = te_ref[r]
        slot = r % 2

        gather(jnp.minimum(r + 1, nt - 1), 1 - slot)

        @pl.when(cur_ref[0] != e)
        def _():
            wslot = (cur_ref[1] + 1) % 2
            for cp in weight_copies(e, wslot):
                cp.wait()
            cur_ref[0] = e
            cur_ref[1] = cur_ref[1] + 1

            @pl.when(nxt_ref[r] >= 0)
            def _():
                for cp in weight_copies(nxt_ref[r], 1 - wslot):
                    cp.start()

        wslot = cur_ref[1] % 2
        tile_wait(xs_ref, xbuf_ref, gsem.at[slot])

        @pl.when(r >= 2)
        def _():
            tile_wait(ybuf_ref, ys_ref, ssem.at[slot])

        d = ybuf_ref.shape[1]
        f = wd_buf_ref.shape[1]
        def chain(part):
            rows = pl.ds(pl.multiple_of(slot * MOE_TILE + part * MXU_ROWS, MXU_ROWS), MXU_ROWS)
            x = xbuf_ref[rows, 0:d]
            gate_parts = xbuf_ref[rows, d:d + LANES].astype(F32)
            gate = gate_parts[:, 0:1] + gate_parts[:, 1:2]
            gu = _dot(x, wgu_buf_ref[wslot])
            hid = (jax.nn.silu(gu[:, 0:f]) * gu[:, f:2 * f]).astype(BF16)
            ybuf_ref[rows, :] = (_dot(hid, wd_buf_ref[wslot]) * gate).astype(ybuf_ref.dtype)

        chunk_groups = MXU_ROWS // GROUP
        n_chunks = (nreal_ref[r] + chunk_groups - 1) // chunk_groups
        for count in range(1, MOE_TILE // MXU_ROWS + 1):
            @pl.when(n_chunks == count)
            def _(count=count):
                for part in range(count):
                    chain(part)

        for i in range(TILE_GROUPS):
            _group_copy(ybuf_ref, slot * TILE_GROUPS + i, ys_ref, gdst_ref[r * TILE_GROUPS + i],
                        ssem.at[slot]).start(priority=1)

        @pl.when(r == nt - 1)
        def _():
            tile_wait(xs_ref, xbuf_ref, gsem.at[1 - slot])
            tile_wait(ybuf_ref, ys_ref, ssem.at[slot])

            @pl.when(r >= 1)
            def _():
                tile_wait(ybuf_ref, ys_ref, ssem.at[1 - slot])


def _experts(plan, xs, w_gate, w_up, w_down):
    d = w_gate.shape[1]
    f = w_gate.shape[2]
    tile_e, next_e, n_real, n_tiles, gsrc, gdst, tail = plan
    r_max = tile_e.shape[0]
    n_blocks = tail.shape[0]
    any_spec = pl.BlockSpec(memory_space=pl.ANY)
    grid_spec = pltpu.PrefetchScalarGridSpec(
        num_scalar_prefetch=7,
        grid=(r_max,),
        in_specs=[any_spec, any_spec, any_spec, any_spec],
        out_specs=any_spec,
        scratch_shapes=[pltpu.VMEM((2 * MOE_TILE, d + LANES), BF16),
                        pltpu.VMEM((2 * MOE_TILE, d), BF16),
                        pltpu.VMEM((GROUP, d), BF16),
                        pltpu.VMEM((2, d, 2 * f), BF16),
                        pltpu.VMEM((2, f, d), BF16),
                        pltpu.SMEM((2,), jnp.int32),
                        pltpu.SemaphoreType.DMA((2,)),
                        pltpu.SemaphoreType.DMA((2,)),
                        pltpu.SemaphoreType.DMA(()),
                        pltpu.SemaphoreType.DMA((2,))],
    )
    return pl.pallas_call(
        _expert_kernel,
        grid_spec=grid_spec,
        out_shape=jax.ShapeDtypeStruct((n_blocks * LOCAL_ROWS, d), BF16),
        compiler_params=_params(1),
        name="experts",
    )(tile_e, next_e, n_real, n_tiles, gsrc, gdst, tail, xs, w_gate, w_up, w_down)


def _combine_kernel(ys_ref, h_ref, cm_ref, g_ref, o_ref):
    tm = h_ref.shape[0]
    cm = cm_ref[...]
    q = lax.broadcasted_iota(jnp.int32, (tm, ys_ref.shape[0]), 1)
    sel = jnp.where((q == cm[:, 0:1].astype(jnp.int32)) | (q == cm[:, 1:2].astype(jnp.int32)), 1.0, 0.0)
    h2 = h_ref[...] + _dot(sel.astype(BF16), ys_ref[...])
    o_ref[...] = _rms_scale(h2, g_ref[...])


def _combine(ys, h1, cmeta, g):
    t, d = h1.shape
    tm = ROUTE_BLOCK
    return pl.pallas_call(
        _combine_kernel,
        grid=(t // tm,),
        in_specs=[pl.BlockSpec((LOCAL_ROWS, d), lambda i: (i, 0)),
                  pl.BlockSpec((tm, d), lambda i: (i, 0)),
                  pl.BlockSpec((tm, LANES), lambda i: (i, 0)),
                  pl.BlockSpec((1, d), lambda i: (0, 0))],
        out_specs=pl.BlockSpec((tm, d), lambda i: (i, 0)),
        out_shape=jax.ShapeDtypeStruct((t, d), F32),
        compiler_params=_params(1),
        name="combine",
    )(ys, h1, cmeta, g.reshape(1, d))


def _expert_plan(counts, n_blocks):
    cnt = counts[:, 0].astype(jnp.int32).reshape(n_blocks, N_EXPERTS)
    groups = (cnt + GROUP - 1) // GROUP
    first = jnp.cumsum(groups, axis=1) - groups
    upto = jnp.cumsum(groups, axis=0)
    per_expert = upto[-1]
    tiles_e = (per_expert + TILE_GROUPS - 1) // TILE_GROUPS
    tile_end = jnp.cumsum(tiles_e)
    n_tiles = tile_end[-1]
    max_groups = 2 * ROUTE_BLOCK * n_blocks // GROUP + n_blocks * N_EXPERTS
    r_max = max_groups // TILE_GROUPS + N_EXPERTS
    tile_ids = jnp.arange(r_max, dtype=jnp.int32)
    tile = jnp.minimum(tile_ids, n_tiles - 1)
    tile_e = jnp.sum((tile_end[None, :] <= tile[:, None]).astype(jnp.int32), axis=1)
    later = (tile_e[None, :] > tile_e[:, None]) & (tile_ids[None, :] < n_tiles)
    next_e = jnp.min(jnp.where(later, tile_e[None, :], N_EXPERTS), axis=1)
    next_e = jnp.where(next_e == N_EXPERTS, -1, next_e)

    slot = jnp.arange(r_max * TILE_GROUPS, dtype=jnp.int32)
    s_tile = slot // TILE_GROUPS
    oh_e = jnp.repeat(tile_e, TILE_GROUPS)[:, None] == jnp.arange(N_EXPERTS, dtype=jnp.int32)[None, :]

    def by_expert(table):
        return jnp.sum(jnp.where(oh_e[:, None, :], table[None], 0), axis=-1)

    k = slot - by_expert(((tile_end - tiles_e) * TILE_GROUPS)[None, :])[:, 0]
    real = (k < by_expert(per_expert[None, :])[:, 0]) & (s_tile < n_tiles)
    upto_e = by_expert(upto)
    blk = jnp.minimum(jnp.sum((upto_e <= k[:, None]).astype(jnp.int32), axis=1), n_blocks - 1)
    oh_b = blk[:, None] == jnp.arange(n_blocks, dtype=jnp.int32)[None, :]

    def by_block(table_se):
        return jnp.sum(jnp.where(oh_b, table_se, 0), axis=1)

    before = by_block(upto_e - by_expert(groups))
    src = blk * GROUPS_PER_BLOCK + by_block(by_expert(first)) + (k - before)
    zero_group = GROUPS_PER_BLOCK - 1
    spare = n_blocks * GROUPS_PER_BLOCK + (s_tile % 2) * TILE_GROUPS + slot % TILE_GROUPS
    gsrc = jnp.where(real, src, zero_group)
    gdst = jnp.where(real, src, spare)
    n_spare_blocks = -(-2 * TILE_GROUPS // GROUPS_PER_BLOCK)
    tail = jnp.concatenate([jnp.sum(groups, axis=1), jnp.zeros((n_spare_blocks,), jnp.int32)])
    n_real = jnp.sum(real.reshape(r_max, TILE_GROUPS).astype(jnp.int32), axis=1)
    return tile_e, next_e, n_real, n_tiles.reshape(1), gsrc, gdst, tail


def _layer(h, norm_mix_g, w_in, conv_w, sgu_ln_g, sgu_ln_b, sgu_w_s, sgu_b_s, w_up_conv,
           w_up_sgu, w_out, norm_ffn_g, w_rg, b_rg, w_re, b_re, w_eg, w_eu, w_ed, out_g):
    t, d = h.shape
    conv_width = conv_w.shape[1]
    sgu_width = sgu_ln_g.shape[0]
    xn, yb = _sgu_branch(h, norm_mix_g, w_in, 3 * conv_width, sgu_ln_g, sgu_ln_b, sgu_w_s, sgu_b_s)
    gate_col0 = 3 * conv_width + 2 * sgu_width
    ya, (wg_bf, wua_bf, wub_bf, wout_bf) = _conv_branch(
        xn, w_in, conv_w, conv_width,
        to_cast=[(w_in, gate_col0, w_in.shape[1] - gate_col0, 1024),
                 (w_up_conv, 0, d, d), (w_up_sgu, 0, d, d), (w_out, 0, d, d)])
    m, eg_bf, eu_bf, ed_bf = _upgate(xn, ya, yb, wg_bf, wua_bf, wub_bf, w_eg, w_eu, w_ed)
    h1, xn2_hi, xn2_lo = _outproj(m, wout_bf, h, norm_ffn_g)
    xs, meta, counts = _route_sort(xn2_hi, xn2_lo, w_rg, b_rg, w_re, b_re)
    ys = _experts(_expert_plan(counts, t // ROUTE_BLOCK), xs, eg_bf, eu_bf, ed_bf)
    cmeta = jnp.pad(meta[0:2].T, ((0, 0), (0, LANES - 2)))
    return _combine(ys, h1, cmeta, out_g)


def kernel(x, norm_mix_g, w_in, conv_w, sgu_ln_g, sgu_ln_b, sgu_w_s, sgu_b_s, w_up_conv, w_up_sgu, w_out, norm_ffn_g, w_router_group, b_router_group, w_router_expert, b_router_expert, w_exp_gate, w_exp_up, w_exp_down, norm_final_g):
    bsz, s, d = x.shape
    depth = w_in.shape[0]
    assert bsz == 1 and depth == 1, "causal conv carry and the fused final norm assume one sequence, one layer"
    assert s % ROUTE_BLOCK == 0
    out = _layer(x.reshape(s, d), norm_mix_g[0], w_in[0], conv_w[0], sgu_ln_g[0], sgu_ln_b[0],
                 sgu_w_s[0], sgu_b_s[0], w_up_conv[0], w_up_sgu[0], w_out[0], norm_ffn_g[0],
                 w_router_group[0], b_router_group[0], w_router_expert[0], b_router_expert[0],
                 w_exp_gate[0], w_exp_up[0], w_exp_down[0], norm_final_g)
    return out.reshape(bsz, s, d)
```

```python
import functools

import jax
import jax.numpy as jnp
from jax import lax
from jax.experimental import pallas as pl
from jax.experimental.pallas import tpu as pltpu

F32 = jnp.float32
BF16 = jnp.bfloat16

EPS = 1e-6
CHUNK = 64
CONV_K = 3
SGU_HEADS = 8
SGU_BLOCK = 128
N_GROUPS = 4
EXPERTS_PER_GROUP = 4
N_EXPERTS = N_GROUPS * EXPERTS_PER_GROUP
ROUTER_ROWS = 32
LANES = 128

VMEM_LIMIT_BYTES = 56 * 1024 * 1024

ROUTE_BLOCK = 512
GROUP = 16
LOCAL_ROWS = -(-(2 * ROUTE_BLOCK + N_EXPERTS * (GROUP - 1)) // 256) * 256
GROUPS_PER_BLOCK = LOCAL_ROWS // GROUP
MOE_TILE = 512
TILE_GROUPS = MOE_TILE // GROUP
MXU_ROWS = 256


def _params(n_axes):
    return pltpu.CompilerParams(
        dimension_semantics=("arbitrary",) * n_axes,
        vmem_limit_bytes=VMEM_LIMIT_BYTES)


def _dot(a, b):
    return jnp.dot(a, b, preferred_element_type=F32)


def _rms_scale(x, g):
    ms = jnp.mean(x * x, axis=-1, keepdims=True)
    return x * lax.rsqrt(ms + EPS) * g


def _conv_kernel(cast_chunks, xn_ref, wb_ref, wc_ref, wh_ref, cw_ref, *rest):
    n_in = sum(cast_chunks)
    cast_in, o_ref = rest[:n_in], rest[n_in]
    cast_out = rest[n_in + 1:n_in + 1 + len(cast_chunks)]
    wbf_ref, carry_ref = rest[n_in + 1 + len(cast_chunks):]
    i = pl.program_id(1)
    tn = wb_ref.shape[1]
    tm = xn_ref.shape[0]
    src = iter(cast_in)
    for dst, n_chunks in zip(cast_out, cast_chunks):
        wc = dst.shape[1] // n_chunks
        for k in range(n_chunks):
            dst[:, k * wc:(k + 1) * wc] = next(src)[...].astype(BF16)

    @pl.when(i == 0)
    def _():
        wbf_ref[:, 0:tn] = wb_ref[...].astype(BF16)
        wbf_ref[:, tn:2 * tn] = wc_ref[...].astype(BF16)
        wbf_ref[:, 2 * tn:3 * tn] = wh_ref[...].astype(BF16)
        carry_ref[...] = jnp.zeros_like(carry_ref)

    proj = _dot(xn_ref[...], wbf_ref[...])
    b = proj[:, 0:tn]
    p = proj[:, tn:2 * tn] * proj[:, 2 * tn:3 * tn]
    prev = carry_ref[...]
    carry_ref[...] = p[tm - 8:tm, :]
    row = lax.broadcasted_iota(jnp.int32, p.shape, 0)
    p1 = jnp.where(row == 0, prev[7:8, :], pltpu.roll(p, 1, axis=0))
    p2 = jnp.where(row == 0, prev[6:7, :],
                   jnp.where(row == 1, prev[7:8, :], pltpu.roll(p, 2, axis=0)))
    cw = cw_ref[...]
    y = b * (cw[0:1, :] * p2 + cw[1:2, :] * p1 + cw[2:3, :] * p)
    o_ref[...] = y.astype(o_ref.dtype)


def _conv_branch(xn, w_in, conv_w, width, to_cast, tm=1024, tn=256):
    t, d = xn.shape
    nj = width // tn
    ni = t // tm
    steps = nj * ni
    cast_args, cast_in_specs, cast_out_specs, cast_shapes, cast_chunks = [], [], [], [], []
    for arr, col0, n_cols, chunk in to_cast:
        rows = arr.shape[0] // steps
        assert rows * steps == arr.shape[0] and rows % 16 == 0
        assert col0 % chunk == 0 and n_cols % chunk == 0
        for k in range(n_cols // chunk):
            cast_args.append(arr)
            cast_in_specs.append(
                pl.BlockSpec((rows, chunk), lambda j, i, c=col0 // chunk + k: (j * ni + i, c)))
        cast_chunks.append(n_cols // chunk)
        cast_out_specs.append(pl.BlockSpec((rows, n_cols), lambda j, i: (j * ni + i, 0)))
        cast_shapes.append(jax.ShapeDtypeStruct((arr.shape[0], n_cols), BF16))
    outs = pl.pallas_call(
        functools.partial(_conv_kernel, tuple(cast_chunks)),
        grid=(nj, ni),
        in_specs=[pl.BlockSpec((tm, d), lambda j, i: (i, 0)),
                  pl.BlockSpec((d, tn), lambda j, i: (0, j)),
                  pl.BlockSpec((d, tn), lambda j, i: (0, nj + j)),
                  pl.BlockSpec((d, tn), lambda j, i: (0, 2 * nj + j)),
                  pl.BlockSpec((CONV_K, tn), lambda j, i: (0, j))] + cast_in_specs,
        out_specs=[pl.BlockSpec((tm, tn), lambda j, i: (i, j))] + cast_out_specs,
        out_shape=[jax.ShapeDtypeStruct((t, width), BF16)] + cast_shapes,
        scratch_shapes=[pltpu.VMEM((d, 3 * tn), BF16),
                        pltpu.VMEM((8, tn), F32)],
        compiler_params=_params(2),
        name="conv_branch",
    )(xn, w_in, w_in, w_in, conv_w, *cast_args)
    return outs[0], outs[1:]


def _sgu_kernel(x_ref, g_ref, wu_ref, wv_ref, lng_ref, lnb_ref, ws_ref, bsx_ref, xn_ref, o_ref, wbf_ref):
    tm = x_ref.shape[0]
    w = o_ref.shape[1]
    hd = w // SGU_HEADS

    @pl.when(pl.program_id(0) == 0)
    def _():
        wbf_ref[:, 0:w] = wu_ref[...].astype(BF16)
        wbf_ref[:, w:2 * w] = wv_ref[...].astype(BF16)

    xn = _rms_scale(x_ref[...], g_ref[...]).astype(BF16)
    xn_ref[...] = xn
    gz = jax.nn.gelu(_dot(xn, wbf_ref[...]))
    v = gz[:, w:2 * w]
    mu = jnp.mean(v, axis=-1, keepdims=True)
    vc = v - mu
    var = jnp.mean(vc * vc, axis=-1, keepdims=True)
    vn = (vc * lax.rsqrt(var + EPS) * lng_ref[...] + lnb_ref[...]).astype(BF16)
    ii = lax.broadcasted_iota(jnp.int32, (SGU_BLOCK, SGU_BLOCK), 0)
    jj = lax.broadcasted_iota(jnp.int32, (SGU_BLOCK, SGU_BLOCK), 1)
    mask = (jj // CHUNK) <= (ii // CHUNK)
    for h in range(SGU_HEADS):
        wm = jnp.where(mask, ws_ref[h], 0.0).astype(BF16)
        cs = slice(h * hd, (h + 1) * hd)
        for n in range(tm // SGU_BLOCK):
            rs = slice(n * SGU_BLOCK, (n + 1) * SGU_BLOCK)
            vm = _dot(wm, vn[rs, cs]) + bsx_ref[:, cs]
            o_ref[rs, cs] = (gz[rs, cs] * vm).astype(o_ref.dtype)


def _sgu_branch(x, g, w_in, col0, ln_g, ln_b, w_s, b_s, tm=512):
    t, d = x.shape
    w = ln_g.shape[0]
    hd = w // SGU_HEADS
    assert col0 % w == 0
    c0 = col0 // w
    bsx = jnp.repeat(b_s.T, hd, axis=1)
    once = pl.Buffered(1)
    return pl.pallas_call(
        _sgu_kernel,
        grid=(t // tm,),
        in_specs=[pl.BlockSpec((tm, d), lambda i: (i, 0)),
                  pl.BlockSpec((1, d), lambda i: (0, 0)),
                  pl.BlockSpec((d, w), lambda i: (0, c0), pipeline_mode=once),
                  pl.BlockSpec((d, w), lambda i: (0, c0 + 1), pipeline_mode=once),
                  pl.BlockSpec((1, w), lambda i: (0, 0)),
                  pl.BlockSpec((1, w), lambda i: (0, 0)),
                  pl.BlockSpec((SGU_HEADS, SGU_BLOCK, SGU_BLOCK), lambda i: (0, 0, 0)),
                  pl.BlockSpec((SGU_BLOCK, w), lambda i: (0, 0))],
        out_specs=[pl.BlockSpec((tm, d), lambda i: (i, 0)),
                   pl.BlockSpec((tm, w), lambda i: (i, 0))],
        out_shape=[jax.ShapeDtypeStruct((t, d), BF16),
                   jax.ShapeDtypeStruct((t, w), BF16)],
        scratch_shapes=[pltpu.VMEM((d, 2 * w), BF16)],
        compiler_params=_params(1),
        name="sgu_branch",
    )(x, g.reshape(1, d), w_in, w_in, ln_g.reshape(1, w), ln_b.reshape(1, w), w_s, bsx)


def _upgate_kernel(xn_ref, ya_ref, yb_ref, wgc_ref, wgs_ref, wua_ref, wub_ref, eg_ref, eu_ref, ed_ref,
                   o_ref, eg_bf_ref, eu_bf_ref, ed_bf_ref):
    eg_bf_ref[...] = eg_ref[...].astype(BF16)
    eu_bf_ref[...] = eu_ref[...].astype(BF16)
    ed_bf_ref[...] = ed_ref[...].astype(BF16)

    xn = xn_ref[...]
    m = (jax.nn.sigmoid(_dot(xn, wgc_ref[...])) * _dot(ya_ref[...], wua_ref[...])
         + jax.nn.sigmoid(_dot(xn, wgs_ref[...])) * _dot(yb_ref[...], wub_ref[...]))
    o_ref[...] = m.astype(o_ref.dtype)


def _upgate(xn, ya, yb, w_gates, w_up_a, w_up_b, w_eg, w_eu, w_ed, tm=1024, tn=512):
    t, d = xn.shape
    wa = ya.shape[1]
    wb = yb.shape[1]
    dout = w_up_a.shape[1]
    c0 = 0
    nj = dout // tn
    ni = t // tm
    n_e, d_e, f_e = w_eg.shape
    up_rows = n_e * d_e // (nj * ni)
    down_rows = n_e * f_e // (nj * ni)
    assert up_rows * nj * ni == n_e * d_e and up_rows % 16 == 0
    assert down_rows * nj * ni == n_e * f_e and down_rows % 16 == 0
    up_spec = pl.BlockSpec((up_rows, f_e), lambda j, i: (j * ni + i, 0))
    down_spec = pl.BlockSpec((down_rows, d_e), lambda j, i: (j * ni + i, 0))
    m, eg_bf, eu_bf, ed_bf = pl.pallas_call(
        _upgate_kernel,
        grid=(nj, ni),
        in_specs=[pl.BlockSpec((tm, d), lambda j, i: (i, 0)),
                  pl.BlockSpec((tm, wa), lambda j, i: (i, 0)),
                  pl.BlockSpec((tm, wb), lambda j, i: (i, 0)),
                  pl.BlockSpec((d, tn), lambda j, i: (0, c0 + j)),
                  pl.BlockSpec((d, tn), lambda j, i: (0, c0 + nj + j)),
                  pl.BlockSpec((wa, tn), lambda j, i: (0, j)),
                  pl.BlockSpec((wb, tn), lambda j, i: (0, j)),
                  up_spec, up_spec, down_spec],
        out_specs=[pl.BlockSpec((tm, tn), lambda j, i: (i, j)), up_spec, up_spec, down_spec],
        out_shape=[jax.ShapeDtypeStruct((t, dout), BF16),
                   jax.ShapeDtypeStruct((n_e * d_e, f_e), BF16),
                   jax.ShapeDtypeStruct((n_e * d_e, f_e), BF16),
                   jax.ShapeDtypeStruct((n_e * f_e, d_e), BF16)],
        compiler_params=_params(2),
        name="upgate",
    )(xn, ya, yb, w_gates, w_gates, w_up_a, w_up_b,
      w_eg.reshape(n_e * d_e, f_e), w_eu.reshape(n_e * d_e, f_e), w_ed.reshape(n_e * f_e, d_e))
    return (m, eg_bf.reshape(n_e, d_e, f_e), eu_bf.reshape(n_e, d_e, f_e),
            ed_bf.reshape(n_e, f_e, d_e))


def _outproj_kernel(m_ref, w_ref, x_ref, o_ref):
    o_ref[...] = x_ref[...] + _dot(m_ref[...], w_ref[...])


def _outproj(m, w_out_bf, x, tm=512):
    t, d = m.shape
    dout = w_out_bf.shape[1]
    return pl.pallas_call(
        _outproj_kernel,
        grid=(t // tm,),
        in_specs=[pl.BlockSpec((tm, d), lambda i: (i, 0)),
                  pl.BlockSpec((d, dout), lambda i: (0, 0), pipeline_mode=pl.Buffered(1)),
                  pl.BlockSpec((tm, dout), lambda i: (i, 0))],
        out_specs=pl.BlockSpec((tm, dout), lambda i: (i, 0)),
        out_shape=jax.ShapeDtypeStruct((t, dout), F32),
        compiler_params=_params(1),
        name="outproj",
    )(m, w_out_bf, x)


def _argmax_rows(rows):
    best = rows[0]
    idx = jnp.zeros(rows[0].shape, jnp.int32)
    for k in range(1, len(rows)):
        better = rows[k] > best
        best = jnp.where(better, rows[k], best)
        idx = jnp.where(better, k, idx)
    return best, idx


def _softmax_rows(rows):
    mx = functools.reduce(jnp.maximum, rows)
    ex = [jnp.exp(r - mx) for r in rows]
    den = functools.reduce(lambda a, b: a + b, ex)
    return [e / den for e in ex]


def _route_sort_kernel(h_ref, g_ref, wr_ref, br_ref, xs_ref, meta_ref, cnt_ref, before_ref):
    tm = h_ref.shape[0]
    xn = _rms_scale(h_ref[...], g_ref[...])
    xn_hi = xn.astype(BF16)
    xn_lo = (xn - xn_hi.astype(F32)).astype(BF16)
    wr = wr_ref[...]
    wr_hi = wr.astype(BF16)
    wr_lo = (wr - wr_hi.astype(F32)).astype(BF16)
    nt_dims = (((1,), (1,)), ((), ()))
    lt = (lax.dot_general(wr_hi, xn_hi, nt_dims, preferred_element_type=F32)
          + lax.dot_general(wr_hi, xn_lo, nt_dims, preferred_element_type=F32)
          + lax.dot_general(wr_lo, xn_hi, nt_dims, preferred_element_type=F32)) + br_ref[...]
    pgs = _softmax_rows([lt[k:k + 1, :] for k in range(N_GROUPS)])
    pg, gi = _argmax_rows(pgs)
    sel = []
    for k in range(EXPERTS_PER_GROUP):
        r = jnp.zeros_like(pg)
        for g in range(N_GROUPS):
            row = N_GROUPS + g * EXPERTS_PER_GROUP + k
            r = jnp.where(gi == g, lt[row:row + 1, :], r)
        sel.append(r)
    pes = _softmax_rows(sel)
    p1, e1 = _argmax_rows(pes)
    rest = [jnp.where(e1 == k, -1.0, pes[k]) for k in range(EXPERTS_PER_GROUP)]
    p2, e2 = _argmax_rows(rest)
    den = p1 + p2
    w1 = pg * (p1 / den)
    w2 = pg * (p2 / den)
    lo = jnp.minimum(e1, e2)
    hi = jnp.maximum(e1, e2)
    w_lo = jnp.where(e1 < e2, w1, w2)
    w_hi = jnp.where(e1 < e2, w2, w1)
    ea = gi * EXPERTS_PER_GROUP + lo
    eb = gi * EXPERTS_PER_GROUP + hi

    erow = lax.broadcasted_iota(jnp.int32, (N_EXPERTS, tm), 0)
    oh_a = (erow == ea).astype(F32)
    oh_b = (erow == eb).astype(F32)

    @pl.when(pl.program_id(0) == 0)
    def _():
        a = lax.broadcasted_iota(jnp.int32, (tm, tm), 0)
        b = lax.broadcasted_iota(jnp.int32, (tm, tm), 1)
        before_ref[...] = (a < b).astype(BF16)

    cum = _dot((oh_a + oh_b).astype(BF16), before_ref[...])
    cnt = jnp.sum(oh_a + oh_b, axis=1, keepdims=True)
    padded = jnp.floor((cnt + (GROUP - 1)) * (1.0 / GROUP)) * GROUP
    pos_a = jnp.sum(oh_a * cum + jnp.where(erow < ea, padded, 0.0), axis=0, keepdims=True)
    pos_b = jnp.sum(oh_b * cum + jnp.where(erow < eb, padded, 0.0), axis=0, keepdims=True)

    d = h_ref.shape[1]
    q = lax.broadcasted_iota(jnp.int32, (xs_ref.shape[0], tm), 0)
    perm_a = jnp.where(q == pos_a.astype(jnp.int32), 1.0, 0.0).astype(BF16)
    perm_b = jnp.where(q == pos_b.astype(jnp.int32), 1.0, 0.0).astype(BF16)
    xs_ref[:, 0:d] = _dot(perm_a + perm_b, xn_hi).astype(xs_ref.dtype)

    def gate_rows(w):
        hi = w.astype(BF16).astype(F32)
        lo = w - hi
        k = lax.broadcasted_iota(jnp.int32, (LANES, tm), 0)
        return jnp.where(k == 0, hi, jnp.where(k == 1, lo, 0.0)).astype(BF16)

    gates = (lax.dot_general(perm_a, gate_rows(w_lo), nt_dims, preferred_element_type=F32)
             + lax.dot_general(perm_b, gate_rows(w_hi), nt_dims, preferred_element_type=F32))
    xs_ref[:, d:d + LANES] = gates.astype(xs_ref.dtype)

    cnt_ref[...] = jnp.broadcast_to(cnt, cnt_ref.shape)
    meta_ref[0:1, :] = pos_a
    meta_ref[1:2, :] = pos_b
    meta_ref[2:8, :] = jnp.zeros((6, tm), F32)


def _route_sort(h1, g, w_rg, b_rg, w_re, b_re):
    t, d = h1.shape
    tm = ROUTE_BLOCK
    nb = t // tm
    n_log = w_rg.shape[1] + w_re.shape[1]
    wr = jnp.concatenate([w_rg, w_re], axis=1).T
    wr = jnp.pad(wr, ((0, ROUTER_ROWS - n_log), (0, 0)))
    br = jnp.pad(jnp.concatenate([b_rg, b_re]), (0, ROUTER_ROWS - n_log)).reshape(ROUTER_ROWS, 1)
    return pl.pallas_call(
        _route_sort_kernel,
        grid=(nb,),
        in_specs=[pl.BlockSpec((tm, d), lambda i: (i, 0)),
                  pl.BlockSpec((1, d), lambda i: (0, 0)),
                  pl.BlockSpec((ROUTER_ROWS, d), lambda i: (0, 0)),
                  pl.BlockSpec((ROUTER_ROWS, 1), lambda i: (0, 0))],
        out_specs=[pl.BlockSpec((LOCAL_ROWS, d + LANES), lambda i: (i, 0)),
                   pl.BlockSpec((8, tm), lambda i: (0, i)),
                   pl.BlockSpec((N_EXPERTS, LANES), lambda i: (i, 0))],
        out_shape=[jax.ShapeDtypeStruct((nb * LOCAL_ROWS, d + LANES), BF16),
                   jax.ShapeDtypeStruct((8, t), F32),
                   jax.ShapeDtypeStruct((nb * N_EXPERTS, LANES), F32)],
        scratch_shapes=[pltpu.VMEM((tm, tm), BF16)],
        compiler_params=_params(1),
        name="route_sort",
    )(h1, g.reshape(1, d), wr, br)


def _group_copy(src_ref, s_group, dst_ref, d_group, sem):
    return pltpu.make_async_copy(src_ref.at[pl.ds(pl.multiple_of(s_group * GROUP, GROUP), GROUP), :],
                                 dst_ref.at[pl.ds(pl.multiple_of(d_group * GROUP, GROUP), GROUP), :], sem)


def _expert_kernel(te_ref, nxt_ref, nreal_ref, nt_ref, gsrc_ref, gdst_ref, tail_ref,
                   xs_ref, wg_ref, wu_ref, wd_ref, ys_ref,
                   xbuf_ref, ybuf_ref, zero_ref, wgu_buf_ref, wd_buf_ref,
                   cur_ref, gsem, ssem, zsem, wsem):
    r = pl.program_id(0)
    nt = nt_ref[0]

    def gather(q, slot):
        for i in range(TILE_GROUPS):
            _group_copy(xs_ref, gsrc_ref[q * TILE_GROUPS + i], xbuf_ref, slot * TILE_GROUPS + i,
                        gsem.at[slot]).start(priority=1)

    def weight_copies(e, wslot):
        f = wg_ref.shape[2]
        return (pltpu.make_async_copy(wg_ref.at[e], wgu_buf_ref.at[wslot, :, pl.ds(0, f)], wsem.at[wslot]),
                pltpu.make_async_copy(wu_ref.at[e], wgu_buf_ref.at[wslot, :, pl.ds(f, f)], wsem.at[wslot]),
                pltpu.make_async_copy(wd_ref.at[e], wd_buf_ref.at[wslot], wsem.at[wslot]))

    def tile_wait(src, dst, sem):
        pltpu.make_async_copy(src.at[pl.ds(0, MOE_TILE), :], dst.at[pl.ds(0, MOE_TILE), :], sem).wait()

    @pl.when(r == 0)
    def _():
        cur_ref[0] = -1
        cur_ref[1] = -1
        for cp in weight_copies(te_ref[0], 0):
            cp.start()
        gather(0, 0)
        zero_ref[...] = jnp.zeros_like(zero_ref)
        ybuf_ref[...] = jnp.zeros_like(ybuf_ref)
        n_blocks = tail_ref.shape[0]

        def fill(make):
            def body(g, c):
                make(g)
                return c
            return body

        for blk in range(n_blocks):
            lax.fori_loop(tail_ref[blk], GROUPS_PER_BLOCK, fill(
                lambda g, blk=blk: _group_copy(zero_ref, 0, ys_ref, blk * GROUPS_PER_BLOCK + g, zsem).start()), 0)
        for blk in range(n_blocks):
            lax.fori_loop(tail_ref[blk], GROUPS_PER_BLOCK, fill(
                lambda g, blk=blk: _group_copy(zero_ref, 0, ys_ref, blk * GROUPS_PER_BLOCK + g, zsem).wait()), 0)

    @pl.when(r < nt)
    def _():
        e = te_ref[r]
        slot = r % 2

        gather(jnp.minimum(r + 1, nt - 1), 1 - slot)

        @pl.when(cur_ref[0] != e)
        def _():
            wslot = (cur_ref[1] + 1) % 2
            for cp in weight_copies(e, wslot):
                cp.wait()
            cur_ref[0] = e
            cur_ref[1] = cur_ref[1] + 1

            @pl.when(nxt_ref[r] >= 0)
            def _():
                for cp in weight_copies(nxt_ref[r], 1 - wslot):
                    cp.start()

        wslot = cur_ref[1] % 2
        tile_wait(xs_ref, xbuf_ref, gsem.at[slot])

        @pl.when(r >= 2)
        def _():
            tile_wait(ybuf_ref, ys_ref, ssem.at[slot])

        d = ybuf_ref.shape[1]
        f = wd_buf_ref.shape[1]
        def chain(part):
            rows = pl.ds(pl.multiple_of(slot * MOE_TILE + part * MXU_ROWS, MXU_ROWS), MXU_ROWS)
            x = xbuf_ref[rows, 0:d]
            gate_parts = xbuf_ref[rows, d:d + LANES].astype(F32)
            gate = gate_parts[:, 0:1] + gate_parts[:, 1:2]
            gu = _dot(x, wgu_buf_ref[wslot])
            hid = (jax.nn.silu(gu[:, 0:f]) * gu[:, f:2 * f]).astype(BF16)
            ybuf_ref[rows, :] = (_dot(hid, wd_buf_ref[wslot]) * gate).astype(ybuf_ref.dtype)

        chunk_groups = MXU_ROWS // GROUP
        n_chunks = (nreal_ref[r] + chunk_groups - 1) // chunk_groups
        for count in range(1, MOE_TILE // MXU_ROWS + 1):
            @pl.when(n_chunks == count)
            def _(count=count):
                for part in range(count):
                    chain(part)

        for i in range(TILE_GROUPS):
            _group_copy(ybuf_ref, slot * TILE_GROUPS + i, ys_ref, gdst_ref[r * TILE_GROUPS + i],
                        ssem.at[slot]).start(priority=1)

        @pl.when(r == nt - 1)
        def _():
            tile_wait(xs_ref, xbuf_ref, gsem.at[1 - slot])
            tile_wait(ybuf_ref, ys_ref, ssem.at[slot])

            @pl.when(r >= 1)
            def _():
                tile_wait(ybuf_ref, ys_ref, ssem.at[1 - slot])


def _experts(plan, xs, w_gate, w_up, w_down):
    d = w_gate.shape[1]
    f = w_gate.shape[2]
    tile_e, next_e, n_real, n_tiles, gsrc, gdst, tail = plan
    r_max = tile_e.shape[0]
    n_blocks = tail.shape[0]
    any_spec = pl.BlockSpec(memory_space=pl.ANY)
    grid_spec = pltpu.PrefetchScalarGridSpec(
        num_scalar_prefetch=7,
        grid=(r_max,),
        in_specs=[any_spec, any_spec, any_spec, any_spec],
        out_specs=any_spec,
        scratch_shapes=[pltpu.VMEM((2 * MOE_TILE, d + LANES), BF16),
                        pltpu.VMEM((2 * MOE_TILE, d), BF16),
                        pltpu.VMEM((GROUP, d), BF16),
                        pltpu.VMEM((2, d, 2 * f), BF16),
                        pltpu.VMEM((2, f, d), BF16),
                        pltpu.SMEM((2,), jnp.int32),
                        pltpu.SemaphoreType.DMA((2,)),
                        pltpu.SemaphoreType.DMA((2,)),
                        pltpu.SemaphoreType.DMA(()),
                        pltpu.SemaphoreType.DMA((2,))],
    )
    return pl.pallas_call(
        _expert_kernel,
        grid_spec=grid_spec,
        out_shape=jax.ShapeDtypeStruct((n_blocks * LOCAL_ROWS, d), BF16),
        compiler_params=_params(1),
        name="experts",
    )(tile_e, next_e, n_real, n_tiles, gsrc, gdst, tail, xs, w_gate, w_up, w_down)


def _combine_kernel(ys_ref, h_ref, cm_ref, g_ref, o_ref):
    tm = h_ref.shape[0]
    cm = cm_ref[...]
    q = lax.broadcasted_iota(jnp.int32, (tm, ys_ref.shape[0]), 1)
    sel = jnp.where((q == cm[:, 0:1].astype(jnp.int32)) | (q == cm[:, 1:2].astype(jnp.int32)), 1.0, 0.0)
    h2 = h_ref[...] + _dot(sel.astype(BF16), ys_ref[...])
    o_ref[...] = _rms_scale(h2, g_ref[...])


def _combine(ys, h1, cmeta, g):
    t, d = h1.shape
    tm = ROUTE_BLOCK
    return pl.pallas_call(
        _combine_kernel,
        grid=(t // tm,),
        in_specs=[pl.BlockSpec((LOCAL_ROWS, d), lambda i: (i, 0)),
                  pl.BlockSpec((tm, d), lambda i: (i, 0)),
                  pl.BlockSpec((tm, LANES), lambda i: (i, 0)),
                  pl.BlockSpec((1, d), lambda i: (0, 0))],
        out_specs=pl.BlockSpec((tm, d), lambda i: (i, 0)),
        out_shape=jax.ShapeDtypeStruct((t, d), F32),
        compiler_params=_params(1),
        name="combine",
    )(ys, h1, cmeta, g.reshape(1, d))


def _expert_plan(counts, n_blocks):
    cnt = counts[:, 0].astype(jnp.int32).reshape(n_blocks, N_EXPERTS)
    groups = (cnt + GROUP - 1) // GROUP
    first = jnp.cumsum(groups, axis=1) - groups
    upto = jnp.cumsum(groups, axis=0)
    per_expert = upto[-1]
    tiles_e = (per_expert + TILE_GROUPS - 1) // TILE_GROUPS
    tile_end = jnp.cumsum(tiles_e)
    n_tiles = tile_end[-1]
    max_groups = 2 * ROUTE_BLOCK * n_blocks // GROUP + n_blocks * N_EXPERTS
    r_max = max_groups // TILE_GROUPS + N_EXPERTS
    tile_ids = jnp.arange(r_max, dtype=jnp.int32)
    tile = jnp.minimum(tile_ids, n_tiles - 1)
    tile_e = jnp.sum((tile_end[None, :] <= tile[:, None]).astype(jnp.int32), axis=1)
    later = (tile_e[None, :] > tile_e[:, None]) & (tile_ids[None, :] < n_tiles)
    next_e = jnp.min(jnp.where(later, tile_e[None, :], N_EXPERTS), axis=1)
    next_e = jnp.where(next_e == N_EXPERTS, -1, next_e)

    slot = jnp.arange(r_max * TILE_GROUPS, dtype=jnp.int32)
    s_tile = slot // TILE_GROUPS
    oh_e = jnp.repeat(tile_e, TILE_GROUPS)[:, None] == jnp.arange(N_EXPERTS, dtype=jnp.int32)[None, :]

    def by_expert(table):
        return jnp.sum(jnp.where(oh_e[:, None, :], table[None], 0), axis=-1)

    k = slot - by_expert(((tile_end - tiles_e) * TILE_GROUPS)[None, :])[:, 0]
    real = (k < by_expert(per_expert[None, :])[:, 0]) & (s_tile < n_tiles)
    upto_e = by_expert(upto)
    blk = jnp.minimum(jnp.sum((upto_e <= k[:, None]).astype(jnp.int32), axis=1), n_blocks - 1)
    oh_b = blk[:, None] == jnp.arange(n_blocks, dtype=jnp.int32)[None, :]

    def by_block(table_se):
        return jnp.sum(jnp.where(oh_b, table_se, 0), axis=1)

    before = by_block(upto_e - by_expert(groups))
    src = blk * GROUPS_PER_BLOCK + by_block(by_expert(first)) + (k - before)
    zero_group = GROUPS_PER_BLOCK - 1
    spare = n_blocks * GROUPS_PER_BLOCK + (s_tile % 2) * TILE_GROUPS + slot % TILE_GROUPS
    gsrc = jnp.where(real, src, zero_group)
    gdst = jnp.where(real, src, spare)
    n_spare_blocks = -(-2 * TILE_GROUPS // GROUPS_PER_BLOCK)
    tail = jnp.concatenate([jnp.sum(groups, axis=1), jnp.zeros((n_spare_blocks,), jnp.int32)])
    n_real = jnp.sum(real.reshape(r_max, TILE_GROUPS).astype(jnp.int32), axis=1)
    return tile_e, next_e, n_real, n_tiles.reshape(1), gsrc, gdst, tail


def _layer(h, norm_mix_g, w_in, conv_w, sgu_ln_g, sgu_ln_b, sgu_w_s, sgu_b_s, w_up_conv,
           w_up_sgu, w_out, norm_ffn_g, w_rg, b_rg, w_re, b_re, w_eg, w_eu, w_ed, out_g):
    t, d = h.shape
    conv_width = conv_w.shape[1]
    sgu_width = sgu_ln_g.shape[0]
    xn, yb = _sgu_branch(h, norm_mix_g, w_in, 3 * conv_width, sgu_ln_g, sgu_ln_b, sgu_w_s, sgu_b_s)
    gate_col0 = 3 * conv_width + 2 * sgu_width
    ya, (wg_bf, wua_bf, wub_bf, wout_bf) = _conv_branch(
        xn, w_in, conv_w, conv_width,
        to_cast=[(w_in, gate_col0, w_in.shape[1] - gate_col0, 1024),
                 (w_up_conv, 0, d, d), (w_up_sgu, 0, d, d), (w_out, 0, d, d)])
    m, eg_bf, eu_bf, ed_bf = _upgate(xn, ya, yb, wg_bf, wua_bf, wub_bf, w_eg, w_eu, w_ed)
    h1 = _outproj(m, wout_bf, h)
    xs, meta, counts = _route_sort(h1, norm_ffn_g, w_rg, b_rg, w_re, b_re)
    ys = _experts(_expert_plan(counts, t // ROUTE_BLOCK), xs, eg_bf, eu_bf, ed_bf)
    cmeta = jnp.pad(meta[0:2].T, ((0, 0), (0, LANES - 2)))
    return _combine(ys, h1, cmeta, out_g)


def kernel(x, norm_mix_g, w_in, conv_w, sgu_ln_g, sgu_ln_b, sgu_w_s, sgu_b_s, w_up_conv, w_up_sgu, w_out, norm_ffn_g, w_router_group, b_router_group, w_router_expert, b_router_expert, w_exp_gate, w_exp_up, w_exp_down, norm_final_g):
    bsz, s, d = x.shape
    depth = w_in.shape[0]
    assert bsz == 1 and depth == 1, "causal conv carry and the fused final norm assume one sequence, one layer"
    assert s % ROUTE_BLOCK == 0
    out = _layer(x.reshape(s, d), norm_mix_g[0], w_in[0], conv_w[0], sgu_ln_g[0], sgu_ln_b[0],
                 sgu_w_s[0], sgu_b_s[0], w_up_conv[0], w_up_sgu[0], w_out[0], norm_ffn_g[0],
                 w_router_group[0], b_router_group[0], w_router_expert[0], b_router_expert[0],
                 w_exp_gate[0], w_exp_up[0], w_exp_down[0], norm_final_g)
    return out.reshape(bsz, s, d)
```

```python
import functools

import jax
import jax.numpy as jnp
from jax import lax
from jax.experimental import pallas as pl
from jax.experimental.pallas import tpu as pltpu

F32 = jnp.float32
BF16 = jnp.bfloat16

EPS = 1e-6
CHUNK = 64
CONV_K = 3
SGU_HEADS = 8
SGU_BLOCK = 128
N_GROUPS = 4
EXPERTS_PER_GROUP = 4
N_EXPERTS = N_GROUPS * EXPERTS_PER_GROUP
ROUTER_ROWS = 32
LANES = 128

VMEM_LIMIT_BYTES = 56 * 1024 * 1024

ROUTE_BLOCK = 512
GROUP = 16
LOCAL_ROWS = -(-(2 * ROUTE_BLOCK + N_EXPERTS * (GROUP - 1)) // 256) * 256
GROUPS_PER_BLOCK = LOCAL_ROWS // GROUP
MOE_TILE = 512
TILE_GROUPS = MOE_TILE // GROUP
MXU_ROWS = 256


def _params(n_axes):
    return pltpu.CompilerParams(
        dimension_semantics=("arbitrary",) * n_axes,
        vmem_limit_bytes=VMEM_LIMIT_BYTES)


def _dot(a, b):
    return jnp.dot(a, b, preferred_element_type=F32)


def _rms_scale(x, g):
    ms = jnp.mean(x * x, axis=-1, keepdims=True)
    return x * lax.rsqrt(ms + EPS) * g


def _conv_kernel(cast_chunks, xn_ref, wb_ref, wc_ref, wh_ref, cw_ref, *rest):
    n_in = sum(cast_chunks)
    cast_in, o_ref = rest[:n_in], rest[n_in]
    cast_out = rest[n_in + 1:n_in + 1 + len(cast_chunks)]
    wbf_ref, carry_ref = rest[n_in + 1 + len(cast_chunks):]
    i = pl.program_id(1)
    tn = wb_ref.shape[1]
    tm = xn_ref.shape[0]
    src = iter(cast_in)
    for dst, n_chunks in zip(cast_out, cast_chunks):
        wc = dst.shape[1] // n_chunks
        for k in range(n_chunks):
            dst[:, k * wc:(k + 1) * wc] = next(src)[...].astype(BF16)

    @pl.when(i == 0)
    def _():
        wbf_ref[:, 0:tn] = wb_ref[...].astype(BF16)
        wbf_ref[:, tn:2 * tn] = wc_ref[...].astype(BF16)
        wbf_ref[:, 2 * tn:3 * tn] = wh_ref[...].astype(BF16)
        carry_ref[...] = jnp.zeros_like(carry_ref)

    proj = _dot(xn_ref[...], wbf_ref[...])
    b = proj[:, 0:tn]
    p = proj[:, tn:2 * tn] * proj[:, 2 * tn:3 * tn]
    prev = carry_ref[...]
    carry_ref[...] = p[tm - 8:tm, :]
    row = lax.broadcasted_iota(jnp.int32, p.shape, 0)
    p1 = jnp.where(row == 0, prev[7:8, :], pltpu.roll(p, 1, axis=0))
    p2 = jnp.where(row == 0, prev[6:7, :],
                   jnp.where(row == 1, prev[7:8, :], pltpu.roll(p, 2, axis=0)))
    cw = cw_ref[...]
    y = b * (cw[0:1, :] * p2 + cw[1:2, :] * p1 + cw[2:3, :] * p)
    o_ref[...] = y.astype(o_ref.dtype)


def _conv_branch(xn, w_in, conv_w, width, to_cast, tm=1024, tn=256):
    t, d = xn.shape
    nj = width // tn
    ni = t // tm
    steps = nj * ni
    cast_args, cast_in_specs, cast_out_specs, cast_shapes, cast_chunks = [], [], [], [], []
    for arr, col0, n_cols, chunk in to_cast:
        rows = arr.shape[0] // steps
        assert rows * steps == arr.shape[0] and rows % 16 == 0
        assert col0 % chunk == 0 and n_cols % chunk == 0
        for k in range(n_cols // chunk):
            cast_args.append(arr)
            cast_in_specs.append(
                pl.BlockSpec((rows, chunk), lambda j, i, c=col0 // chunk + k: (j * ni + i, c)))
        cast_chunks.append(n_cols // chunk)
        cast_out_specs.append(pl.BlockSpec((rows, n_cols), lambda j, i: (j * ni + i, 0)))
        cast_shapes.append(jax.ShapeDtypeStruct((arr.shape[0], n_cols), BF16))
    outs = pl.pallas_call(
        functools.partial(_conv_kernel, tuple(cast_chunks)),
        grid=(nj, ni),
        in_specs=[pl.BlockSpec((tm, d), lambda j, i: (i, 0)),
                  pl.BlockSpec((d, tn), lambda j, i: (0, j)),
                  pl.BlockSpec((d, tn), lambda j, i: (0, nj + j)),
                  pl.BlockSpec((d, tn), lambda j, i: (0, 2 * nj + j)),
                  pl.BlockSpec((CONV_K, tn), lambda j, i: (0, j))] + cast_in_specs,
        out_specs=[pl.BlockSpec((tm, tn), lambda j, i: (i, j))] + cast_out_specs,
        out_shape=[jax.ShapeDtypeStruct((t, width), BF16)] + cast_shapes,
        scratch_shapes=[pltpu.VMEM((d, 3 * tn), BF16),
                        pltpu.VMEM((8, tn), F32)],
        compiler_params=_params(2),
        name="conv_branch",
    )(xn, w_in, w_in, w_in, conv_w, *cast_args)
    return outs[0], outs[1:]


def _sgu_kernel(x_ref, g_ref, wu_ref, wv_ref, lng_ref, lnb_ref, ws_ref, bsx_ref, xn_ref, o_ref, wbf_ref):
    tm = x_ref.shape[0]
    w = o_ref.shape[1]
    hd = w // SGU_HEADS

    @pl.when(pl.program_id(0) == 0)
    def _():
        wbf_ref[:, 0:w] = wu_ref[...].astype(BF16)
        wbf_ref[:, w:2 * w] = wv_ref[...].astype(BF16)

    xn = _rms_scale(x_ref[...], g_ref[...]).astype(BF16)
    xn_ref[...] = xn
    gz = jax.nn.gelu(_dot(xn, wbf_ref[...]))
    v = gz[:, w:2 * w]
    mu = jnp.mean(v, axis=-1, keepdims=True)
    vc = v - mu
    var = jnp.mean(vc * vc, axis=-1, keepdims=True)
    vn = (vc * lax.rsqrt(var + EPS) * lng_ref[...] + lnb_ref[...]).astype(BF16)
    ii = lax.broadcasted_iota(jnp.int32, (SGU_BLOCK, SGU_BLOCK), 0)
    jj = lax.broadcasted_iota(jnp.int32, (SGU_BLOCK, SGU_BLOCK), 1)
    mask = (jj // CHUNK) <= (ii // CHUNK)
    for h in range(SGU_HEADS):
        wm = jnp.where(mask, ws_ref[h], 0.0).astype(BF16)
        cs = slice(h * hd, (h + 1) * hd)
        for n in range(tm // SGU_BLOCK):
            rs = slice(n * SGU_BLOCK, (n + 1) * SGU_BLOCK)
            vm = _dot(wm, vn[rs, cs]) + bsx_ref[:, cs]
            o_ref[rs, cs] = (gz[rs, cs] * vm).astype(o_ref.dtype)


def _sgu_branch(x, g, w_in, col0, ln_g, ln_b, w_s, b_s, tm=512):
    t, d = x.shape
    w = ln_g.shape[0]
    hd = w // SGU_HEADS
    assert col0 % w == 0
    c0 = col0 // w
    bsx = jnp.repeat(b_s.T, hd, axis=1)
    once = pl.Buffered(1)
    return pl.pallas_call(
        _sgu_kernel,
        grid=(t // tm,),
        in_specs=[pl.BlockSpec((tm, d), lambda i: (i, 0)),
                  pl.BlockSpec((1, d), lambda i: (0, 0)),
                  pl.BlockSpec((d, w), lambda i: (0, c0), pipeline_mode=once),
                  pl.BlockSpec((d, w), lambda i: (0, c0 + 1), pipeline_mode=once),
                  pl.BlockSpec((1, w), lambda i: (0, 0)),
                  pl.BlockSpec((1, w), lambda i: (0, 0)),
                  pl.BlockSpec((SGU_HEADS, SGU_BLOCK, SGU_BLOCK), lambda i: (0, 0, 0)),
                  pl.BlockSpec((SGU_BLOCK, w), lambda i: (0, 0))],
        out_specs=[pl.BlockSpec((tm, d), lambda i: (i, 0)),
                   pl.BlockSpec((tm, w), lambda i: (i, 0))],
        out_shape=[jax.ShapeDtypeStruct((t, d), BF16),
                   jax.ShapeDtypeStruct((t, w), BF16)],
        scratch_shapes=[pltpu.VMEM((d, 2 * w), BF16)],
        compiler_params=_params(1),
        name="sgu_branch",
    )(x, g.reshape(1, d), w_in, w_in, ln_g.reshape(1, w), ln_b.reshape(1, w), w_s, bsx)


def _upgate_kernel(xn_ref, ya_ref, yb_ref, wgc_ref, wgs_ref, wua_ref, wub_ref, eg_ref, eu_ref, ed_ref,
                   o_ref, eg_bf_ref, eu_bf_ref, ed_bf_ref):
    eg_bf_ref[...] = eg_ref[...].astype(BF16)
    eu_bf_ref[...] = eu_ref[...].astype(BF16)
    ed_bf_ref[...] = ed_ref[...].astype(BF16)

    xn = xn_ref[...]
    m = (jax.nn.sigmoid(_dot(xn, wgc_ref[...])) * _dot(ya_ref[...], wua_ref[...])
         + jax.nn.sigmoid(_dot(xn, wgs_ref[...])) * _dot(yb_ref[...], wub_ref[...]))
    o_ref[...] = m.astype(o_ref.dtype)


def _upgate(xn, ya, yb, w_gates, w_up_a, w_up_b, w_eg, w_eu, w_ed, tm=1024, tn=512):
    t, d = xn.shape
    wa = ya.shape[1]
    wb = yb.shape[1]
    dout = w_up_a.shape[1]
    c0 = 0
    nj = dout // tn
    ni = t // tm
    n_e, d_e, f_e = w_eg.shape
    up_rows = n_e * d_e // (nj * ni)
    down_rows = n_e * f_e // (nj * ni)
    assert up_rows * nj * ni == n_e * d_e and up_rows % 16 == 0
    assert down_rows * nj * ni == n_e * f_e and down_rows % 16 == 0
    up_spec = pl.BlockSpec((up_rows, f_e), lambda j, i: (j * ni + i, 0))
    down_spec = pl.BlockSpec((down_rows, d_e), lambda j, i: (j * ni + i, 0))
    m, eg_bf, eu_bf, ed_bf = pl.pallas_call(
        _upgate_kernel,
        grid=(nj, ni),
        in_specs=[pl.BlockSpec((tm, d), lambda j, i: (i, 0)),
                  pl.BlockSpec((tm, wa), lambda j, i: (i, 0)),
                  pl.BlockSpec((tm, wb), lambda j, i: (i, 0)),
                  pl.BlockSpec((d, tn), lambda j, i: (0, c0 + j)),
                  pl.BlockSpec((d, tn), lambda j, i: (0, c0 + nj + j)),
                  pl.BlockSpec((wa, tn), lambda j, i: (0, j)),
                  pl.BlockSpec((wb, tn), lambda j, i: (0, j)),
                  up_spec, up_spec, down_spec],
        out_specs=[pl.BlockSpec((tm, tn), lambda j, i: (i, j)), up_spec, up_spec, down_spec],
        out_shape=[jax.ShapeDtypeStruct((t, dout), BF16),
                   jax.ShapeDtypeStruct((n_e * d_e, f_e), BF16),
                   jax.ShapeDtypeStruct((n_e * d_e, f_e), BF16),
                   jax.ShapeDtypeStruct((n_e * f_e, d_e), BF16)],
        compiler_params=_params(2),
        name="upgate",
    )(xn, ya, yb, w_gates, w_gates, w_up_a, w_up_b,
      w_eg.reshape(n_e * d_e, f_e), w_eu.reshape(n_e * d_e, f_e), w_ed.reshape(n_e * f_e, d_e))
    return (m, eg_bf.reshape(n_e, d_e, f_e), eu_bf.reshape(n_e, d_e, f_e),
            ed_bf.reshape(n_e, f_e, d_e))


def _outproj_kernel(m_ref, w_ref, x_ref, o_ref):
    o_ref[...] = x_ref[...] + _dot(m_ref[...], w_ref[...])


def _outproj(m, w_out_bf, x, tm=1024):
    t, d = m.shape
    dout = w_out_bf.shape[1]
    return pl.pallas_call(
        _outproj_kernel,
        grid=(t // tm,),
        in_specs=[pl.BlockSpec((tm, d), lambda i: (i, 0)),
                  pl.BlockSpec((d, dout), lambda i: (0, 0), pipeline_mode=pl.Buffered(1)),
                  pl.BlockSpec((tm, dout), lambda i: (i, 0))],
        out_specs=pl.BlockSpec((tm, dout), lambda i: (i, 0)),
        out_shape=jax.ShapeDtypeStruct((t, dout), F32),
        compiler_params=_params(1),
        name="outproj",
    )(m, w_out_bf, x)


def _argmax_rows(rows):
    best = rows[0]
    idx = jnp.zeros(rows[0].shape, jnp.int32)
    for k in range(1, len(rows)):
        better = rows[k] > best
        best = jnp.where(better, rows[k], best)
        idx = jnp.where(better, k, idx)
    return best, idx


def _softmax_rows(rows):
    mx = functools.reduce(jnp.maximum, rows)
    ex = [jnp.exp(r - mx) for r in rows]
    den = functools.reduce(lambda a, b: a + b, ex)
    return [e / den for e in ex]


def _route_sort_kernel(h_ref, g_ref, wr_ref, br_ref, xs_ref, meta_ref, cnt_ref, before_ref):
    tm = h_ref.shape[0]
    xn = _rms_scale(h_ref[...], g_ref[...])
    xn_hi = xn.astype(BF16)
    xn_lo = (xn - xn_hi.astype(F32)).astype(BF16)
    wr = wr_ref[...]
    wr_hi = wr.astype(BF16)
    wr_lo = (wr - wr_hi.astype(F32)).astype(BF16)
    nt_dims = (((1,), (1,)), ((), ()))
    lt = (lax.dot_general(wr_hi, xn_hi, nt_dims, preferred_element_type=F32)
          + lax.dot_general(wr_hi, xn_lo, nt_dims, preferred_element_type=F32)
          + lax.dot_general(wr_lo, xn_hi, nt_dims, preferred_element_type=F32)) + br_ref[...]
    pgs = _softmax_rows([lt[k:k + 1, :] for k in range(N_GROUPS)])
    pg, gi = _argmax_rows(pgs)
    sel = []
    for k in range(EXPERTS_PER_GROUP):
        r = jnp.zeros_like(pg)
        for g in range(N_GROUPS):
            row = N_GROUPS + g * EXPERTS_PER_GROUP + k
            r = jnp.where(gi == g, lt[row:row + 1, :], r)
        sel.append(r)
    pes = _softmax_rows(sel)
    p1, e1 = _argmax_rows(pes)
    rest = [jnp.where(e1 == k, -1.0, pes[k]) for k in range(EXPERTS_PER_GROUP)]
    p2, e2 = _argmax_rows(rest)
    den = p1 + p2
    w1 = pg * (p1 / den)
    w2 = pg * (p2 / den)
    lo = jnp.minimum(e1, e2)
    hi = jnp.maximum(e1, e2)
    w_lo = jnp.where(e1 < e2, w1, w2)
    w_hi = jnp.where(e1 < e2, w2, w1)
    ea = gi * EXPERTS_PER_GROUP + lo
    eb = gi * EXPERTS_PER_GROUP + hi

    erow = lax.broadcasted_iota(jnp.int32, (N_EXPERTS, tm), 0)
    oh_a = (erow == ea).astype(F32)
    oh_b = (erow == eb).astype(F32)

    @pl.when(pl.program_id(0) == 0)
    def _():
        a = lax.broadcasted_iota(jnp.int32, (tm, tm), 0)
        b = lax.broadcasted_iota(jnp.int32, (tm, tm), 1)
        before_ref[...] = (a < b).astype(BF16)

    cum = _dot((oh_a + oh_b).astype(BF16), before_ref[...])
    cnt = jnp.sum(oh_a + oh_b, axis=1, keepdims=True)
    padded = jnp.floor((cnt + (GROUP - 1)) * (1.0 / GROUP)) * GROUP
    pos_a = jnp.sum(oh_a * cum + jnp.where(erow < ea, padded, 0.0), axis=0, keepdims=True)
    pos_b = jnp.sum(oh_b * cum + jnp.where(erow < eb, padded, 0.0), axis=0, keepdims=True)

    d = h_ref.shape[1]
    q = lax.broadcasted_iota(jnp.int32, (xs_ref.shape[0], tm), 0)
    perm_a = jnp.where(q == pos_a.astype(jnp.int32), 1.0, 0.0).astype(BF16)
    perm_b = jnp.where(q == pos_b.astype(jnp.int32), 1.0, 0.0).astype(BF16)
    xs_ref[:, 0:d] = _dot(perm_a + perm_b, xn_hi).astype(xs_ref.dtype)

    def gate_rows(w):
        hi = w.astype(BF16).astype(F32)
        lo = w - hi
        k = lax.broadcasted_iota(jnp.int32, (LANES, tm), 0)
        return jnp.where(k == 0, hi, jnp.where(k == 1, lo, 0.0)).astype(BF16)

    gates = (lax.dot_general(perm_a, gate_rows(w_lo), nt_dims, preferred_element_type=F32)
             + lax.dot_general(perm_b, gate_rows(w_hi), nt_dims, preferred_element_type=F32))
    xs_ref[:, d:d + LANES] = gates.astype(xs_ref.dtype)

    cnt_ref[...] = jnp.broadcast_to(cnt, cnt_ref.shape)
    meta_ref[0:1, :] = pos_a
    meta_ref[1:2, :] = pos_b
    meta_ref[2:8, :] = jnp.zeros((6, tm), F32)


def _route_sort(h1, g, w_rg, b_rg, w_re, b_re):
    t, d = h1.shape
    tm = ROUTE_BLOCK
    nb = t // tm
    n_log = w_rg.shape[1] + w_re.shape[1]
    wr = jnp.concatenate([w_rg, w_re], axis=1).T
    wr = jnp.pad(wr, ((0, ROUTER_ROWS - n_log), (0, 0)))
    br = jnp.pad(jnp.concatenate([b_rg, b_re]), (0, ROUTER_ROWS - n_log)).reshape(ROUTER_ROWS, 1)
    return pl.pallas_call(
        _route_sort_kernel,
        grid=(nb,),
        in_specs=[pl.BlockSpec((tm, d), lambda i: (i, 0)),
                  pl.BlockSpec((1, d), lambda i: (0, 0)),
                  pl.BlockSpec((ROUTER_ROWS, d), lambda i: (0, 0)),
                  pl.BlockSpec((ROUTER_ROWS, 1), lambda i: (0, 0))],
        out_specs=[pl.BlockSpec((LOCAL_ROWS, d + LANES), lambda i: (i, 0)),
                   pl.BlockSpec((8, tm), lambda i: (0, i)),
                   pl.BlockSpec((N_EXPERTS, LANES), lambda i: (i, 0))],
        out_shape=[jax.ShapeDtypeStruct((nb * LOCAL_ROWS, d + LANES), BF16),
                   jax.ShapeDtypeStruct((8, t), F32),
                   jax.ShapeDtypeStruct((nb * N_EXPERTS, LANES), F32)],
        scratch_shapes=[pltpu.VMEM((tm, tm), BF16)],
        compiler_params=_params(1),
        name="route_sort",
    )(h1, g.reshape(1, d), wr, br)


def _group_copy(src_ref, s_group, dst_ref, d_group, sem):
    return pltpu.make_async_copy(src_ref.at[pl.ds(pl.multiple_of(s_group * GROUP, GROUP), GROUP), :],
                                 dst_ref.at[pl.ds(pl.multiple_of(d_group * GROUP, GROUP), GROUP), :], sem)


def _expert_kernel(te_ref, nxt_ref, nreal_ref, nt_ref, gsrc_ref, gdst_ref, tail_ref,
                   xs_ref, wg_ref, wu_ref, wd_ref, ys_ref,
                   xbuf_ref, ybuf_ref, zero_ref, wgu_buf_ref, wd_buf_ref,
                   cur_ref, gsem, ssem, zsem, wsem):
    r = pl.program_id(0)
    nt = nt_ref[0]

    def gather(q, slot):
        for i in range(TILE_GROUPS):
            _group_copy(xs_ref, gsrc_ref[q * TILE_GROUPS + i], xbuf_ref, slot * TILE_GROUPS + i,
                        gsem.at[slot]).start(priority=1)

    def weight_copies(e, wslot):
        f = wg_ref.shape[2]
        return (pltpu.make_async_copy(wg_ref.at[e], wgu_buf_ref.at[wslot, :, pl.ds(0, f)], wsem.at[wslot]),
                pltpu.make_async_copy(wu_ref.at[e], wgu_buf_ref.at[wslot, :, pl.ds(f, f)], wsem.at[wslot]),
                pltpu.make_async_copy(wd_ref.at[e], wd_buf_ref.at[wslot], wsem.at[wslot]))

    def tile_wait(src, dst, sem):
        pltpu.make_async_copy(src.at[pl.ds(0, MOE_TILE), :], dst.at[pl.ds(0, MOE_TILE), :], sem).wait()

    @pl.when(r == 0)
    def _():
        cur_ref[0] = -1
        cur_ref[1] = -1
        for cp in weight_copies(te_ref[0], 0):
            cp.start()
        gather(0, 0)
        zero_ref[...] = jnp.zeros_like(zero_ref)
        ybuf_ref[...] = jnp.zeros_like(ybuf_ref)
        n_blocks = tail_ref.shape[0]

        def fill(make):
            def body(g, c):
                make(g)
                return c
            return body

        for blk in range(n_blocks):
            lax.fori_loop(tail_ref[blk], GROUPS_PER_BLOCK, fill(
                lambda g, blk=blk: _group_copy(zero_ref, 0, ys_ref, blk * GROUPS_PER_BLOCK + g, zsem).start()), 0)
        for blk in range(n_blocks):
            lax.fori_loop(tail_ref[blk], GROUPS_PER_BLOCK, fill(
                lambda g, blk=blk: _group_copy(zero_ref, 0, ys_ref, blk * GROUPS_PER_BLOCK + g, zsem).wait()), 0)

    @pl.when(r < nt)
    def _():
        e = te_ref[r]
        slot = r % 2

        gather(jnp.minimum(r + 1, nt - 1), 1 - slot)

        @pl.when(cur_ref[0] != e)
        def _():
            wslot = (cur_ref[1] + 1) % 2
            for cp in weight_copies(e, wslot):
                cp.wait()
            cur_ref[0] = e
            cur_ref[1] = cur_ref[1] + 1

            @pl.when(nxt_ref[r] >= 0)
            def _():
                for cp in weight_copies(nxt_ref[r], 1 - wslot):
                    cp.start()

        wslot = cur_ref[1] % 2
        tile_wait(xs_ref, xbuf_ref, gsem.at[slot])

        @pl.when(r >= 2)
        def _():
            tile_wait(ybuf_ref, ys_ref, ssem.at[slot])

        d = ybuf_ref.shape[1]
        f = wd_buf_ref.shape[1]
        def chain(part):
            rows = pl.ds(pl.multiple_of(slot * MOE_TILE + part * MXU_ROWS, MXU_ROWS), MXU_ROWS)
            x = xbuf_ref[rows, 0:d]
            gate_parts = xbuf_ref[rows, d:d + LANES].astype(F32)
            gate = gate_parts[:, 0:1] + gate_parts[:, 1:2]
            gu = _dot(x, wgu_buf_ref[wslot])
            hid = (jax.nn.silu(gu[:, 0:f]) * gu[:, f:2 * f]).astype(BF16)
            ybuf_ref[rows, :] = (_dot(hid, wd_buf_ref[wslot]) * gate).astype(ybuf_ref.dtype)

        chunk_groups = MXU_ROWS // GROUP
        n_chunks = (nreal_ref[r] + chunk_groups - 1) // chunk_groups
        for count in range(1, MOE_TILE // MXU_ROWS + 1):
            @pl.when(n_chunks == count)
            def _(count=count):
                for part in range(count):
                    chain(part)

        for i in range(TILE_GROUPS):
            _group_copy(ybuf_ref, slot * TILE_GROUPS + i, ys_ref, gdst_ref[r * TILE_GROUPS + i],
                        ssem.at[slot]).start(priority=1)

        @pl.when(r == nt - 1)
        def _():
            tile_wait(xs_ref, xbuf_ref, gsem.at[1 - slot])
            tile_wait(ybuf_ref, ys_ref, ssem.at[slot])

            @pl.when(r >= 1)
            def _():
                tile_wait(ybuf_ref, ys_ref, ssem.at[1 - slot])


def _experts(plan, xs, w_gate, w_up, w_down):
    d = w_gate.shape[1]
    f = w_gate.shape[2]
    tile_e, next_e, n_real, n_tiles, gsrc, gdst, tail = plan
    r_max = tile_e.shape[0]
    n_blocks = tail.shape[0]
    any_spec = pl.BlockSpec(memory_space=pl.ANY)
    grid_spec = pltpu.PrefetchScalarGridSpec(
        num_scalar_prefetch=7,
        grid=(r_max,),
        in_specs=[any_spec, any_spec, any_spec, any_spec],
        out_specs=any_spec,
        scratch_shapes=[pltpu.VMEM((2 * MOE_TILE, d + LANES), BF16),
                        pltpu.VMEM((2 * MOE_TILE, d), BF16),
                        pltpu.VMEM((GROUP, d), BF16),
                        pltpu.VMEM((2, d, 2 * f), BF16),
                        pltpu.VMEM((2, f, d), BF16),
                        pltpu.SMEM((2,), jnp.int32),
                        pltpu.SemaphoreType.DMA((2,)),
                        pltpu.SemaphoreType.DMA((2,)),
                        pltpu.SemaphoreType.DMA(()),
                        pltpu.SemaphoreType.DMA((2,))],
    )
    return pl.pallas_call(
        _expert_kernel,
        grid_spec=grid_spec,
        out_shape=jax.ShapeDtypeStruct((n_blocks * LOCAL_ROWS, d), BF16),
        compiler_params=_params(1),
        name="experts",
    )(tile_e, next_e, n_real, n_tiles, gsrc, gdst, tail, xs, w_gate, w_up, w_down)


def _combine_kernel(ys_ref, h_ref, cm_ref, g_ref, o_ref):
    tm = h_ref.shape[0]
    cm = cm_ref[...]
    q = lax.broadcasted_iota(jnp.int32, (tm, ys_ref.shape[0]), 1)
    sel = jnp.where((q == cm[:, 0:1].astype(jnp.int32)) | (q == cm[:, 1:2].astype(jnp.int32)), 1.0, 0.0)
    h2 = h_ref[...] + _dot(sel.astype(BF16), ys_ref[...])
    o_ref[...] = _rms_scale(h2, g_ref[...])


def _combine(ys, h1, cmeta, g):
    t, d = h1.shape
    tm = ROUTE_BLOCK
    return pl.pallas_call(
        _combine_kernel,
        grid=(t // tm,),
        in_specs=[pl.BlockSpec((LOCAL_ROWS, d), lambda i: (i, 0)),
                  pl.BlockSpec((tm, d), lambda i: (i, 0)),
                  pl.BlockSpec((tm, LANES), lambda i: (i, 0)),
                  pl.BlockSpec((1, d), lambda i: (0, 0))],
        out_specs=pl.BlockSpec((tm, d), lambda i: (i, 0)),
        out_shape=jax.ShapeDtypeStruct((t, d), F32),
        compiler_params=_params(1),
        name="combine",
    )(ys, h1, cmeta, g.reshape(1, d))


def _expert_plan(counts, n_blocks):
    cnt = counts[:, 0].astype(jnp.int32).reshape(n_blocks, N_EXPERTS)
    groups = (cnt + GROUP - 1) // GROUP
    first = jnp.cumsum(groups, axis=1) - groups
    upto = jnp.cumsum(groups, axis=0)
    per_expert = upto[-1]
    tiles_e = (per_expert + TILE_GROUPS - 1) // TILE_GROUPS
    tile_end = jnp.cumsum(tiles_e)
    n_tiles = tile_end[-1]
    max_groups = 2 * ROUTE_BLOCK * n_blocks // GROUP + n_blocks * N_EXPERTS
    r_max = max_groups // TILE_GROUPS + N_EXPERTS
    tile_ids = jnp.arange(r_max, dtype=jnp.int32)
    tile = jnp.minimum(tile_ids, n_tiles - 1)
    tile_e = jnp.sum((tile_end[None, :] <= tile[:, None]).astype(jnp.int32), axis=1)
    later = (tile_e[None, :] > tile_e[:, None]) & (tile_ids[None, :] < n_tiles)
    next_e = jnp.min(jnp.where(later, tile_e[None, :], N_EXPERTS), axis=1)
    next_e = jnp.where(next_e == N_EXPERTS, -1, next_e)

    slot = jnp.arange(r_max * TILE_GROUPS, dtype=jnp.int32)
    s_tile = slot // TILE_GROUPS
    oh_e = jnp.repeat(tile_e, TILE_GROUPS)[:, None] == jnp.arange(N_EXPERTS, dtype=jnp.int32)[None, :]

    def by_expert(table):
        return jnp.sum(jnp.where(oh_e[:, None, :], table[None], 0), axis=-1)

    k = slot - by_expert(((tile_end - tiles_e) * TILE_GROUPS)[None, :])[:, 0]
    real = (k < by_expert(per_expert[None, :])[:, 0]) & (s_tile < n_tiles)
    upto_e = by_expert(upto)
    blk = jnp.minimum(jnp.sum((upto_e <= k[:, None]).astype(jnp.int32), axis=1), n_blocks - 1)
    oh_b = blk[:, None] == jnp.arange(n_blocks, dtype=jnp.int32)[None, :]

    def by_block(table_se):
        return jnp.sum(jnp.where(oh_b, table_se, 0), axis=1)

    before = by_block(upto_e - by_expert(groups))
    src = blk * GROUPS_PER_BLOCK + by_block(by_expert(first)) + (k - before)
    zero_group = GROUPS_PER_BLOCK - 1
    spare = n_blocks * GROUPS_PER_BLOCK + (s_tile % 2) * TILE_GROUPS + slot % TILE_GROUPS
    gsrc = jnp.where(real, src, zero_group)
    gdst = jnp.where(real, src, spare)
    n_spare_blocks = -(-2 * TILE_GROUPS // GROUPS_PER_BLOCK)
    tail = jnp.concatenate([jnp.sum(groups, axis=1), jnp.zeros((n_spare_blocks,), jnp.int32)])
    n_real = jnp.sum(real.reshape(r_max, TILE_GROUPS).astype(jnp.int32), axis=1)
    return tile_e, next_e, n_real, n_tiles.reshape(1), gsrc, gdst, tail


def _layer(h, norm_mix_g, w_in, conv_w, sgu_ln_g, sgu_ln_b, sgu_w_s, sgu_b_s, w_up_conv,
           w_up_sgu, w_out, norm_ffn_g, w_rg, b_rg, w_re, b_re, w_eg, w_eu, w_ed, out_g):
    t, d = h.shape
    conv_width = conv_w.shape[1]
    sgu_width = sgu_ln_g.shape[0]
    xn, yb = _sgu_branch(h, norm_mix_g, w_in, 3 * conv_width, sgu_ln_g, sgu_ln_b, sgu_w_s, sgu_b_s)
    gate_col0 = 3 * conv_width + 2 * sgu_width
    ya, (wg_bf, wua_bf, wub_bf, wout_bf) = _conv_branch(
        xn, w_in, conv_w, conv_width,
        to_cast=[(w_in, gate_col0, w_in.shape[1] - gate_col0, 1024),
                 (w_up_conv, 0, d, d), (w_up_sgu, 0, d, d), (w_out, 0, d, d)])
    m, eg_bf, eu_bf, ed_bf = _upgate(xn, ya, yb, wg_bf, wua_bf, wub_bf, w_eg, w_eu, w_ed)
    h1 = _outproj(m, wout_bf, h)
    xs, meta, counts = _route_sort(h1, norm_ffn_g, w_rg, b_rg, w_re, b_re)
    ys = _experts(_expert_plan(counts, t // ROUTE_BLOCK), xs, eg_bf, eu_bf, ed_bf)
    cmeta = jnp.pad(meta[0:2].T, ((0, 0), (0, LANES - 2)))
    return _combine(ys, h1, cmeta, out_g)


def kernel(x, norm_mix_g, w_in, conv_w, sgu_ln_g, sgu_ln_b, sgu_w_s, sgu_b_s, w_up_conv, w_up_sgu, w_out, norm_ffn_g, w_router_group, b_router_group, w_router_expert, b_router_expert, w_exp_gate, w_exp_up, w_exp_down, norm_final_g):
    bsz, s, d = x.shape
    depth = w_in.shape[0]
    assert bsz == 1 and depth == 1, "causal conv carry and the fused final norm assume one sequence, one layer"
    assert s % ROUTE_BLOCK == 0
    out = _layer(x.reshape(s, d), norm_mix_g[0], w_in[0], conv_w[0], sgu_ln_g[0], sgu_ln_b[0],
                 sgu_w_s[0], sgu_b_s[0], w_up_conv[0], w_up_sgu[0], w_out[0], norm_ffn_g[0],
                 w_router_group[0], b_router_group[0], w_router_expert[0], b_router_expert[0],
                 w_exp_gate[0], w_exp_up[0], w_exp_down[0], norm_final_g)
    return out.reshape(bsz, s, d)
```

```python
import functools
import math

import jax
import jax.numpy as jnp
from jax import lax
from jax.experimental import pallas as pl
from jax.experimental.pallas import tpu as pltpu

F32 = jnp.float32
BF16 = jnp.bfloat16

EPS = 1e-6
CHUNK = 64
CONV_K = 3
SGU_HEADS = 8
SGU_BLOCK = 128
N_GROUPS = 4
EXPERTS_PER_GROUP = 4
N_EXPERTS = N_GROUPS * EXPERTS_PER_GROUP
ROUTER_ROWS = 32
LANES = 128

VMEM_LIMIT_BYTES = 56 * 1024 * 1024

ROUTE_BLOCK = 512
GROUP = 16
MXU_ROWS = 256
LOCAL_ROWS = -(-(2 * ROUTE_BLOCK + N_EXPERTS * (GROUP - 1)) // MXU_ROWS) * MXU_ROWS
GROUPS_PER_BLOCK = LOCAL_ROWS // GROUP
MOE_TILE = 512
TILE_GROUPS = MOE_TILE // GROUP


def _params(n_axes, fuse_inputs=None):
    return pltpu.CompilerParams(
        dimension_semantics=("arbitrary",) * n_axes,
        vmem_limit_bytes=VMEM_LIMIT_BYTES,
        allow_input_fusion=fuse_inputs)


def _dot(a, b):
    return jnp.dot(a, b, preferred_element_type=F32)


def _rms_scale(x, g):
    ms = jnp.mean(x * x, axis=-1, keepdims=True)
    return x * lax.rsqrt(ms + EPS) * g


def _conv_kernel(cast_chunks, xn_ref, wb_ref, wc_ref, wh_ref, cw_ref, *rest):
    n_in = sum(cast_chunks)
    cast_in, o_ref = rest[:n_in], rest[n_in]
    cast_out = rest[n_in + 1:n_in + 1 + len(cast_chunks)]
    wbf_ref, carry_ref = rest[n_in + 1 + len(cast_chunks):]
    i = pl.program_id(1)
    tn = wb_ref.shape[1]
    tm = xn_ref.shape[0]
    src = iter(cast_in)
    for dst, n_chunks in zip(cast_out, cast_chunks):
        wc = dst.shape[1] // n_chunks
        for k in range(n_chunks):
            dst[:, k * wc:(k + 1) * wc] = next(src)[...].astype(BF16)

    @pl.when(i == 0)
    def _():
        wbf_ref[:, 0:tn] = wb_ref[...].astype(BF16)
        wbf_ref[:, tn:2 * tn] = wc_ref[...].astype(BF16)
        wbf_ref[:, 2 * tn:3 * tn] = wh_ref[...].astype(BF16)
        carry_ref[...] = jnp.zeros_like(carry_ref)

    proj = _dot(xn_ref[...], wbf_ref[...])
    b = proj[:, 0:tn]
    p = proj[:, tn:2 * tn] * proj[:, 2 * tn:3 * tn]
    prev = carry_ref[...]
    carry_ref[...] = p[tm - 8:tm, :]
    row = lax.broadcasted_iota(jnp.int32, p.shape, 0)
    p1 = jnp.where(row == 0, prev[7:8, :], pltpu.roll(p, 1, axis=0))
    p2 = jnp.where(row == 0, prev[6:7, :],
                   jnp.where(row == 1, prev[7:8, :], pltpu.roll(p, 2, axis=0)))
    cw = cw_ref[...]
    y = b * (cw[0:1, :] * p2 + cw[1:2, :] * p1 + cw[2:3, :] * p)
    o_ref[...] = y.astype(o_ref.dtype)


def _conv_branch(xn, w_in, conv_w, width, to_cast, tm=1024, tn=256):
    t, d = xn.shape
    nj = width // tn
    ni = t // tm
    steps = nj * ni
    cast_args, cast_in_specs, cast_out_specs, cast_shapes, cast_chunks = [], [], [], [], []
    for arr, col0, n_cols, chunk in to_cast:
        rows = arr.shape[0] // steps
        assert rows * steps == arr.shape[0] and rows % 16 == 0
        assert col0 % chunk == 0 and n_cols % chunk == 0
        for k in range(n_cols // chunk):
            cast_args.append(arr)
            cast_in_specs.append(
                pl.BlockSpec((rows, chunk), lambda j, i, c=col0 // chunk + k: (j * ni + i, c)))
        cast_chunks.append(n_cols // chunk)
        cast_out_specs.append(pl.BlockSpec((rows, n_cols), lambda j, i: (j * ni + i, 0)))
        cast_shapes.append(jax.ShapeDtypeStruct((arr.shape[0], n_cols), BF16))
    outs = pl.pallas_call(
        functools.partial(_conv_kernel, tuple(cast_chunks)),
        grid=(nj, ni),
        in_specs=[pl.BlockSpec((tm, d), lambda j, i: (i, 0)),
                  pl.BlockSpec((d, tn), lambda j, i: (0, j)),
                  pl.BlockSpec((d, tn), lambda j, i: (0, nj + j)),
                  pl.BlockSpec((d, tn), lambda j, i: (0, 2 * nj + j)),
                  pl.BlockSpec((CONV_K, tn), lambda j, i: (0, j))] + cast_in_specs,
        out_specs=[pl.BlockSpec((tm, tn), lambda j, i: (i, j))] + cast_out_specs,
        out_shape=[jax.ShapeDtypeStruct((t, width), BF16)] + cast_shapes,
        scratch_shapes=[pltpu.VMEM((d, 3 * tn), BF16),
                        pltpu.VMEM((8, tn), F32)],
        compiler_params=_params(2),
        name="conv_branch",
    )(xn, w_in, w_in, w_in, conv_w, *cast_args)
    return outs[0], outs[1:]


def _sgu_kernel(x_ref, g_ref, wu_ref, wv_ref, lng_ref, lnb_ref, ws_ref, bsx_ref, xn_ref, o_ref, wbf_ref):
    tm = x_ref.shape[0]
    w = o_ref.shape[1]
    hd = w // SGU_HEADS

    @pl.when(pl.program_id(0) == 0)
    def _():
        wbf_ref[:, 0:w] = wu_ref[...].astype(BF16)
        wbf_ref[:, w:2 * w] = wv_ref[...].astype(BF16)

    xn = _rms_scale(x_ref[...], g_ref[...]).astype(BF16)
    xn_ref[...] = xn
    gz = jax.nn.gelu(_dot(xn, wbf_ref[...]))
    v = gz[:, w:2 * w]
    mu = jnp.mean(v, axis=-1, keepdims=True)
    vc = v - mu
    var = jnp.mean(vc * vc, axis=-1, keepdims=True)
    vn = (vc * lax.rsqrt(var + EPS) * lng_ref[...] + lnb_ref[...]).astype(BF16)
    ii = lax.broadcasted_iota(jnp.int32, (SGU_BLOCK, SGU_BLOCK), 0)
    jj = lax.broadcasted_iota(jnp.int32, (SGU_BLOCK, SGU_BLOCK), 1)
    mask = (jj // CHUNK) <= (ii // CHUNK)
    for h in range(SGU_HEADS):
        wm = jnp.where(mask, ws_ref[h], 0.0).astype(BF16)
        cs = slice(h * hd, (h + 1) * hd)
        for n in range(tm // SGU_BLOCK):
            rs = slice(n * SGU_BLOCK, (n + 1) * SGU_BLOCK)
            vm = _dot(wm, vn[rs, cs]) + bsx_ref[:, cs]
            o_ref[rs, cs] = (gz[rs, cs] * vm).astype(o_ref.dtype)


def _sgu_branch(x, g, w_in, col0, ln_g, ln_b, w_s, b_s, tm=512):
    t, d = x.shape
    w = ln_g.shape[0]
    hd = w // SGU_HEADS
    assert col0 % w == 0
    c0 = col0 // w
    bsx = jnp.repeat(b_s.T, hd, axis=1)
    once = pl.Buffered(1)
    return pl.pallas_call(
        _sgu_kernel,
        grid=(t // tm,),
        in_specs=[pl.BlockSpec((tm, d), lambda i: (i, 0)),
                  pl.BlockSpec((1, d), lambda i: (0, 0)),
                  pl.BlockSpec((d, w), lambda i: (0, c0), pipeline_mode=once),
                  pl.BlockSpec((d, w), lambda i: (0, c0 + 1), pipeline_mode=once),
                  pl.BlockSpec((1, w), lambda i: (0, 0)),
                  pl.BlockSpec((1, w), lambda i: (0, 0)),
                  pl.BlockSpec((SGU_HEADS, SGU_BLOCK, SGU_BLOCK), lambda i: (0, 0, 0)),
                  pl.BlockSpec((SGU_BLOCK, w), lambda i: (0, 0))],
        out_specs=[pl.BlockSpec((tm, d), lambda i: (i, 0)),
                   pl.BlockSpec((tm, w), lambda i: (i, 0))],
        out_shape=[jax.ShapeDtypeStruct((t, d), BF16),
                   jax.ShapeDtypeStruct((t, w), BF16)],
        scratch_shapes=[pltpu.VMEM((d, 2 * w), BF16)],
        compiler_params=_params(1, fuse_inputs=[False, True, False, False, True, True, False, True]),
        name="sgu_branch",
    )(x, g.reshape(1, d), w_in, w_in, ln_g.reshape(1, w), ln_b.reshape(1, w), w_s, bsx)


def _upgate_kernel(xn_ref, ya_ref, yb_ref, wgc_ref, wgs_ref, wua_ref, wub_ref, eg_ref, eu_ref, ed_ref,
                   o_ref, eg_bf_ref, eu_bf_ref, ed_bf_ref):
    eg_bf_ref[...] = eg_ref[...].astype(BF16)
    eu_bf_ref[...] = eu_ref[...].astype(BF16)
    ed_bf_ref[...] = ed_ref[...].astype(BF16)

    xn = xn_ref[...]
    m = (jax.nn.sigmoid(_dot(xn, wgc_ref[...])) * _dot(ya_ref[...], wua_ref[...])
         + jax.nn.sigmoid(_dot(xn, wgs_ref[...])) * _dot(yb_ref[...], wub_ref[...]))
    o_ref[...] = m.astype(o_ref.dtype)


def _upgate(xn, ya, yb, w_gates, w_up_a, w_up_b, w_eg, w_eu, w_ed, tm=1024, tn=512):
    t, d = xn.shape
    wa = ya.shape[1]
    wb = yb.shape[1]
    dout = w_up_a.shape[1]
    c0 = 0
    nj = dout // tn
    ni = t // tm
    n_e, d_e, f_e = w_eg.shape
    up_rows = n_e * d_e // (nj * ni)
    down_rows = n_e * f_e // (nj * ni)
    assert up_rows * nj * ni == n_e * d_e and up_rows % 16 == 0
    assert down_rows * nj * ni == n_e * f_e and down_rows % 16 == 0
    up_spec = pl.BlockSpec((up_rows, f_e), lambda j, i: (j * ni + i, 0))
    down_spec = pl.BlockSpec((down_rows, d_e), lambda j, i: (j * ni + i, 0))
    m, eg_bf, eu_bf, ed_bf = pl.pallas_call(
        _upgate_kernel,
        grid=(nj, ni),
        in_specs=[pl.BlockSpec((tm, d), lambda j, i: (i, 0)),
                  pl.BlockSpec((tm, wa), lambda j, i: (i, 0)),
                  pl.BlockSpec((tm, wb), lambda j, i: (i, 0)),
                  pl.BlockSpec((d, tn), lambda j, i: (0, c0 + j)),
                  pl.BlockSpec((d, tn), lambda j, i: (0, c0 + nj + j)),
                  pl.BlockSpec((wa, tn), lambda j, i: (0, j)),
                  pl.BlockSpec((wb, tn), lambda j, i: (0, j)),
                  up_spec, up_spec, down_spec],
        out_specs=[pl.BlockSpec((tm, tn), lambda j, i: (i, j)), up_spec, up_spec, down_spec],
        out_shape=[jax.ShapeDtypeStruct((t, dout), BF16),
                   jax.ShapeDtypeStruct((n_e * d_e, f_e), BF16),
                   jax.ShapeDtypeStruct((n_e * d_e, f_e), BF16),
                   jax.ShapeDtypeStruct((n_e * f_e, d_e), BF16)],
        compiler_params=_params(2),
        name="upgate",
    )(xn, ya, yb, w_gates, w_gates, w_up_a, w_up_b,
      w_eg.reshape(n_e * d_e, f_e), w_eu.reshape(n_e * d_e, f_e), w_ed.reshape(n_e * f_e, d_e))
    return (m, eg_bf.reshape(n_e, d_e, f_e), eu_bf.reshape(n_e, d_e, f_e),
            ed_bf.reshape(n_e, f_e, d_e))


def _outproj_kernel(m_ref, w_ref, x_ref, o_ref):
    o_ref[...] = x_ref[...] + _dot(m_ref[...], w_ref[...])


def _outproj(m, w_out_bf, x, tm=512):
    t, d = m.shape
    dout = w_out_bf.shape[1]
    return pl.pallas_call(
        _outproj_kernel,
        grid=(t // tm,),
        in_specs=[pl.BlockSpec((tm, d), lambda i: (i, 0)),
                  pl.BlockSpec((d, dout), lambda i: (0, 0), pipeline_mode=pl.Buffered(1)),
                  pl.BlockSpec((tm, dout), lambda i: (i, 0))],
        out_specs=pl.BlockSpec((tm, dout), lambda i: (i, 0)),
        out_shape=jax.ShapeDtypeStruct((t, dout), F32),
        compiler_params=_params(1),
        name="outproj",
    )(m, w_out_bf, x)


def _argmax_rows(rows):
    best = rows[0]
    idx = jnp.zeros(rows[0].shape, jnp.int32)
    for k in range(1, len(rows)):
        better = rows[k] > best
        best = jnp.where(better, rows[k], best)
        idx = jnp.where(better, k, idx)
    return best, idx


def _softmax_rows(rows):
    mx = functools.reduce(jnp.maximum, rows)
    ex = [jnp.exp(r - mx) for r in rows]
    den = functools.reduce(lambda a, b: a + b, ex)
    return [e / den for e in ex]


def _route_sort_kernel(h_ref, g_ref, wr_ref, br_ref, xs_ref, meta_ref, cnt_ref, before_ref):
    tm = h_ref.shape[0]
    xn = _rms_scale(h_ref[...], g_ref[...])
    xn_hi = xn.astype(BF16)
    xn_lo = (xn - xn_hi.astype(F32)).astype(BF16)
    wr = wr_ref[...]
    wr_hi = wr.astype(BF16)
    wr_lo = (wr - wr_hi.astype(F32)).astype(BF16)
    nt_dims = (((1,), (1,)), ((), ()))
    lt = (lax.dot_general(wr_hi, xn_hi, nt_dims, preferred_element_type=F32)
          + lax.dot_general(wr_hi, xn_lo, nt_dims, preferred_element_type=F32)
          + lax.dot_general(wr_lo, xn_hi, nt_dims, preferred_element_type=F32)) + br_ref[...]
    pgs = _softmax_rows([lt[k:k + 1, :] for k in range(N_GROUPS)])
    pg, gi = _argmax_rows(pgs)
    sel = []
    for k in range(EXPERTS_PER_GROUP):
        r = jnp.zeros_like(pg)
        for g in range(N_GROUPS):
            row = N_GROUPS + g * EXPERTS_PER_GROUP + k
            r = jnp.where(gi == g, lt[row:row + 1, :], r)
        sel.append(r)
    pes = _softmax_rows(sel)
    p1, e1 = _argmax_rows(pes)
    rest = [jnp.where(e1 == k, -1.0, pes[k]) for k in range(EXPERTS_PER_GROUP)]
    p2, e2 = _argmax_rows(rest)
    den = p1 + p2
    w1 = pg * (p1 / den)
    w2 = pg * (p2 / den)
    lo = jnp.minimum(e1, e2)
    hi = jnp.maximum(e1, e2)
    w_lo = jnp.where(e1 < e2, w1, w2)
    w_hi = jnp.where(e1 < e2, w2, w1)
    ea = gi * EXPERTS_PER_GROUP + lo
    eb = gi * EXPERTS_PER_GROUP + hi

    erow = lax.broadcasted_iota(jnp.int32, (N_EXPERTS, tm), 0)
    oh_a = (erow == ea).astype(F32)
    oh_b = (erow == eb).astype(F32)

    @pl.when(pl.program_id(0) == 0)
    def _():
        a = lax.broadcasted_iota(jnp.int32, (tm, tm), 0)
        b = lax.broadcasted_iota(jnp.int32, (tm, tm), 1)
        before_ref[...] = (a < b).astype(BF16)

    cum = _dot((oh_a + oh_b).astype(BF16), before_ref[...])
    cnt = jnp.sum(oh_a + oh_b, axis=1, keepdims=True)
    padded = jnp.floor((cnt + (GROUP - 1)) * (1.0 / GROUP)) * GROUP
    pos_a = jnp.sum(oh_a * cum + jnp.where(erow < ea, padded, 0.0), axis=0, keepdims=True)
    pos_b = jnp.sum(oh_b * cum + jnp.where(erow < eb, padded, 0.0), axis=0, keepdims=True)

    d = h_ref.shape[1]
    q = lax.broadcasted_iota(jnp.int32, (xs_ref.shape[0], tm), 0)
    perm_a = jnp.where(q == pos_a.astype(jnp.int32), 1.0, 0.0).astype(BF16)
    perm_b = jnp.where(q == pos_b.astype(jnp.int32), 1.0, 0.0).astype(BF16)
    xs_ref[:, 0:d] = _dot(perm_a + perm_b, xn_hi).astype(xs_ref.dtype)

    def gate_rows(w):
        hi = w.astype(BF16).astype(F32)
        lo = w - hi
        k = lax.broadcasted_iota(jnp.int32, (LANES, tm), 0)
        return jnp.where(k == 0, hi, jnp.where(k == 1, lo, 0.0)).astype(BF16)

    gates = (lax.dot_general(perm_a, gate_rows(w_lo), nt_dims, preferred_element_type=F32)
             + lax.dot_general(perm_b, gate_rows(w_hi), nt_dims, preferred_element_type=F32))
    xs_ref[:, d:d + LANES] = gates.astype(xs_ref.dtype)

    cnt_ref[...] = jnp.broadcast_to(cnt, cnt_ref.shape)
    meta_ref[0:1, :] = pos_a
    meta_ref[1:2, :] = pos_b
    meta_ref[2:8, :] = jnp.zeros((6, tm), F32)


def _route_sort(h1, g, w_rg, b_rg, w_re, b_re):
    t, d = h1.shape
    tm = ROUTE_BLOCK
    nb = t // tm
    n_log = w_rg.shape[1] + w_re.shape[1]
    wr = jnp.concatenate([w_rg, w_re], axis=1).T
    wr = jnp.pad(wr, ((0, ROUTER_ROWS - n_log), (0, 0)))
    br = jnp.pad(jnp.concatenate([b_rg, b_re]), (0, ROUTER_ROWS - n_log)).reshape(ROUTER_ROWS, 1)
    return pl.pallas_call(
        _route_sort_kernel,
        grid=(nb,),
        in_specs=[pl.BlockSpec((tm, d), lambda i: (i, 0)),
                  pl.BlockSpec((1, d), lambda i: (0, 0)),
                  pl.BlockSpec((ROUTER_ROWS, d), lambda i: (0, 0)),
                  pl.BlockSpec((ROUTER_ROWS, 1), lambda i: (0, 0))],
        out_specs=[pl.BlockSpec((LOCAL_ROWS, d + LANES), lambda i: (i, 0)),
                   pl.BlockSpec((8, tm), lambda i: (0, i)),
                   pl.BlockSpec((N_EXPERTS, LANES), lambda i: (i, 0))],
        out_shape=[jax.ShapeDtypeStruct((nb * LOCAL_ROWS, d + LANES), BF16),
                   jax.ShapeDtypeStruct((8, t), F32),
                   jax.ShapeDtypeStruct((nb * N_EXPERTS, LANES), F32)],
        scratch_shapes=[pltpu.VMEM((tm, tm), BF16)],
        compiler_params=_params(1, fuse_inputs=[False, True, True, True]),
        name="route_sort",
    )(h1, g.reshape(1, d), wr, br)


def _group_copy(src_ref, s_group, dst_ref, d_group, sem):
    return pltpu.make_async_copy(src_ref.at[pl.ds(pl.multiple_of(s_group * GROUP, GROUP), GROUP), :],
                                 dst_ref.at[pl.ds(pl.multiple_of(d_group * GROUP, GROUP), GROUP), :], sem)


def _expert_kernel(te_ref, nxt_ref, nreal_ref, nt_ref, gsrc_ref, gdst_ref, tail_ref,
                   xs_ref, wg_ref, wu_ref, wd_ref, ys_ref,
                   xbuf_ref, ybuf_ref, zero_ref, wgu_buf_ref, wd_buf_ref,
                   cur_ref, gsem, ssem, zsem, wsem):
    r = pl.program_id(0)
    nt = nt_ref[0]

    def gather(q, slot):
        for i in range(TILE_GROUPS):
            _group_copy(xs_ref, gsrc_ref[q * TILE_GROUPS + i], xbuf_ref, slot * TILE_GROUPS + i,
                        gsem.at[slot]).start(priority=1)

    def weight_copies(e, wslot):
        f = wg_ref.shape[2]
        return (pltpu.make_async_copy(wg_ref.at[e], wgu_buf_ref.at[wslot, :, pl.ds(0, f)], wsem.at[wslot]),
                pltpu.make_async_copy(wu_ref.at[e], wgu_buf_ref.at[wslot, :, pl.ds(f, f)], wsem.at[wslot]),
                pltpu.make_async_copy(wd_ref.at[e], wd_buf_ref.at[wslot], wsem.at[wslot]))

    def tile_wait(src, dst, sem):
        pltpu.make_async_copy(src.at[pl.ds(0, MOE_TILE), :], dst.at[pl.ds(0, MOE_TILE), :], sem).wait()

    @pl.when(r == 0)
    def _():
        cur_ref[0] = -1
        cur_ref[1] = -1
        for cp in weight_copies(te_ref[0], 0):
            cp.start()
        gather(0, 0)
        zero_ref[...] = jnp.zeros_like(zero_ref)
        ybuf_ref[...] = jnp.zeros_like(ybuf_ref)
        n_blocks = tail_ref.shape[0]

        def fill(make):
            def body(g, c):
                make(g)
                return c
            return body

        for blk in range(n_blocks):
            lax.fori_loop(tail_ref[blk], GROUPS_PER_BLOCK, fill(
                lambda g, blk=blk: _group_copy(zero_ref, 0, ys_ref, blk * GROUPS_PER_BLOCK + g, zsem).start()), 0)
        for blk in range(n_blocks):
            lax.fori_loop(tail_ref[blk], GROUPS_PER_BLOCK, fill(
                lambda g, blk=blk: _group_copy(zero_ref, 0, ys_ref, blk * GROUPS_PER_BLOCK + g, zsem).wait()), 0)

    @pl.when(r < nt)
    def _():
        e = te_ref[r]
        slot = r % 2

        gather(jnp.minimum(r + 1, nt - 1), 1 - slot)

        @pl.when(cur_ref[0] != e)
        def _():
            wslot = (cur_ref[1] + 1) % 2
            for cp in weight_copies(e, wslot):
                cp.wait()
            cur_ref[0] = e
            cur_ref[1] = cur_ref[1] + 1

            @pl.when(nxt_ref[r] >= 0)
            def _():
                for cp in weight_copies(nxt_ref[r], 1 - wslot):
                    cp.start()

        wslot = cur_ref[1] % 2
        tile_wait(xs_ref, xbuf_ref, gsem.at[slot])

        @pl.when(r >= 2)
        def _():
            tile_wait(ybuf_ref, ys_ref, ssem.at[slot])

        d = ybuf_ref.shape[1]
        f = wd_buf_ref.shape[1]
        def chain(part):
            rows = pl.ds(pl.multiple_of(slot * MOE_TILE + part * MXU_ROWS, MXU_ROWS), MXU_ROWS)
            x = xbuf_ref[rows, 0:d]
            gate_parts = xbuf_ref[rows, d:d + LANES].astype(F32)
            gate = gate_parts[:, 0:1] + gate_parts[:, 1:2]
            gu = _dot(x, wgu_buf_ref[wslot])
            hid = (jax.nn.silu(gu[:, 0:f]) * gu[:, f:2 * f]).astype(BF16)
            ybuf_ref[rows, :] = (_dot(hid, wd_buf_ref[wslot]) * gate).astype(ybuf_ref.dtype)

        chunk_groups = MXU_ROWS // GROUP
        n_chunks = (nreal_ref[r] + chunk_groups - 1) // chunk_groups
        for count in range(1, MOE_TILE // MXU_ROWS + 1):
            @pl.when(n_chunks == count)
            def _(count=count):
                for part in range(count):
                    chain(part)

        for i in range(TILE_GROUPS):
            _group_copy(ybuf_ref, slot * TILE_GROUPS + i, ys_ref, gdst_ref[r * TILE_GROUPS + i],
                        ssem.at[slot]).start(priority=1)

        @pl.when(r == nt - 1)
        def _():
            tile_wait(xs_ref, xbuf_ref, gsem.at[1 - slot])
            tile_wait(ybuf_ref, ys_ref, ssem.at[slot])

            @pl.when(r >= 1)
            def _():
                tile_wait(ybuf_ref, ys_ref, ssem.at[1 - slot])


def _experts(plan, xs, w_gate, w_up, w_down):
    d = w_gate.shape[1]
    f = w_gate.shape[2]
    tile_e, next_e, n_real, n_tiles, gsrc, gdst, tail = plan
    r_max = tile_e.shape[0]
    n_blocks = tail.shape[0]
    any_spec = pl.BlockSpec(memory_space=pl.ANY)
    grid_spec = pltpu.PrefetchScalarGridSpec(
        num_scalar_prefetch=7,
        grid=(r_max,),
        in_specs=[any_spec, any_spec, any_spec, any_spec],
        out_specs=any_spec,
        scratch_shapes=[pltpu.VMEM((2 * MOE_TILE, d + LANES), BF16),
                        pltpu.VMEM((2 * MOE_TILE, d), BF16),
                        pltpu.VMEM((GROUP, d), BF16),
                        pltpu.VMEM((2, d, 2 * f), BF16),
                        pltpu.VMEM((2, f, d), BF16),
                        pltpu.SMEM((2,), jnp.int32),
                        pltpu.SemaphoreType.DMA((2,)),
                        pltpu.SemaphoreType.DMA((2,)),
                        pltpu.SemaphoreType.DMA(()),
                        pltpu.SemaphoreType.DMA((2,))],
    )
    return pl.pallas_call(
        _expert_kernel,
        grid_spec=grid_spec,
        out_shape=jax.ShapeDtypeStruct((n_blocks * LOCAL_ROWS, d), BF16),
        compiler_params=_params(1),
        name="experts",
    )(tile_e, next_e, n_real, n_tiles, gsrc, gdst, tail, xs, w_gate, w_up, w_down)


def _combine_kernel(ys_ref, h_ref, cm_ref, g_ref, o_ref):
    tm = h_ref.shape[0]
    cm = cm_ref[...]
    q = lax.broadcasted_iota(jnp.int32, (tm, ys_ref.shape[0]), 1)
    sel = jnp.where((q == cm[:, 0:1].astype(jnp.int32)) | (q == cm[:, 1:2].astype(jnp.int32)), 1.0, 0.0)
    h2 = h_ref[...] + _dot(sel.astype(BF16), ys_ref[...])
    o_ref[...] = _rms_scale(h2, g_ref[...])


def _combine(ys, h1, cmeta, g):
    t, d = h1.shape
    tm = ROUTE_BLOCK
    return pl.pallas_call(
        _combine_kernel,
        grid=(t // tm,),
        in_specs=[pl.BlockSpec((LOCAL_ROWS, d), lambda i: (i, 0)),
                  pl.BlockSpec((tm, d), lambda i: (i, 0)),
                  pl.BlockSpec((tm, LANES), lambda i: (i, 0)),
                  pl.BlockSpec((1, d), lambda i: (0, 0))],
        out_specs=pl.BlockSpec((tm, d), lambda i: (i, 0)),
        out_shape=jax.ShapeDtypeStruct((t, d), F32),
        compiler_params=_params(1, fuse_inputs=[False, False, True, True]),
        name="combine",
    )(ys, h1, cmeta, g.reshape(1, d))


def _expert_plan(counts, n_blocks):
    cnt = counts[:, 0].astype(jnp.int32).reshape(n_blocks, N_EXPERTS)
    groups = (cnt + GROUP - 1) // GROUP
    first = jnp.cumsum(groups, axis=1) - groups
    upto = jnp.cumsum(groups, axis=0)
    per_expert = upto[-1]
    tiles_e = (per_expert + TILE_GROUPS - 1) // TILE_GROUPS
    tile_end = jnp.cumsum(tiles_e)
    n_tiles = tile_end[-1]
    max_groups = 2 * ROUTE_BLOCK * n_blocks // GROUP + n_blocks * N_EXPERTS
    r_max = max_groups // TILE_GROUPS + N_EXPERTS
    tile_ids = jnp.arange(r_max, dtype=jnp.int32)
    tile = jnp.minimum(tile_ids, n_tiles - 1)
    tile_e = jnp.sum((tile_end[None, :] <= tile[:, None]).astype(jnp.int32), axis=1)
    later = (tile_e[None, :] > tile_e[:, None]) & (tile_ids[None, :] < n_tiles)
    next_e = jnp.min(jnp.where(later, tile_e[None, :], N_EXPERTS), axis=1)
    next_e = jnp.where(next_e == N_EXPERTS, -1, next_e)

    slot = jnp.arange(r_max * TILE_GROUPS, dtype=jnp.int32)
    s_tile = slot // TILE_GROUPS
    oh_e = jnp.repeat(tile_e, TILE_GROUPS)[:, None] == jnp.arange(N_EXPERTS, dtype=jnp.int32)[None, :]

    def by_expert(table):
        return jnp.sum(jnp.where(oh_e[:, None, :], table[None], 0), axis=-1)

    k = slot - by_expert(((tile_end - tiles_e) * TILE_GROUPS)[None, :])[:, 0]
    real = (k < by_expert(per_expert[None, :])[:, 0]) & (s_tile < n_tiles)
    upto_e = by_expert(upto)
    blk = jnp.minimum(jnp.sum((upto_e <= k[:, None]).astype(jnp.int32), axis=1), n_blocks - 1)
    oh_b = blk[:, None] == jnp.arange(n_blocks, dtype=jnp.int32)[None, :]

    def by_block(table_se):
        return jnp.sum(jnp.where(oh_b, table_se, 0), axis=1)

    before = by_block(upto_e - by_expert(groups))
    src = blk * GROUPS_PER_BLOCK + by_block(by_expert(first)) + (k - before)
    zero_group = GROUPS_PER_BLOCK - 1
    spare = n_blocks * GROUPS_PER_BLOCK + (s_tile % 2) * TILE_GROUPS + slot % TILE_GROUPS
    gsrc = jnp.where(real, src, zero_group)
    gdst = jnp.where(real, src, spare)
    n_spare_blocks = -(-2 * TILE_GROUPS // GROUPS_PER_BLOCK)
    tail = jnp.concatenate([jnp.sum(groups, axis=1), jnp.zeros((n_spare_blocks,), jnp.int32)])
    n_real = jnp.sum(real.reshape(r_max, TILE_GROUPS).astype(jnp.int32), axis=1)
    return tile_e, next_e, n_real, n_tiles.reshape(1), gsrc, gdst, tail


def _layer(h, norm_mix_g, w_in, conv_w, sgu_ln_g, sgu_ln_b, sgu_w_s, sgu_b_s, w_up_conv,
           w_up_sgu, w_out, norm_ffn_g, w_rg, b_rg, w_re, b_re, w_eg, w_eu, w_ed, out_g):
    t, d = h.shape
    conv_width = conv_w.shape[1]
    sgu_width = sgu_ln_g.shape[0]
    xn, yb = _sgu_branch(h, norm_mix_g, w_in, 3 * conv_width, sgu_ln_g, sgu_ln_b, sgu_w_s, sgu_b_s)
    gate_col0 = 3 * conv_width + 2 * sgu_width
    ya, (wg_bf, wua_bf, wub_bf, wout_bf) = _conv_branch(
        xn, w_in, conv_w, conv_width,
        to_cast=[(w_in, gate_col0, w_in.shape[1] - gate_col0, math.gcd(gate_col0, w_in.shape[1])),
                 (w_up_conv, 0, d, d), (w_up_sgu, 0, d, d), (w_out, 0, d, d)])
    m, eg_bf, eu_bf, ed_bf = _upgate(xn, ya, yb, wg_bf, wua_bf, wub_bf, w_eg, w_eu, w_ed)
    h1 = _outproj(m, wout_bf, h)
    xs, meta, counts = _route_sort(h1, norm_ffn_g, w_rg, b_rg, w_re, b_re)
    ys = _experts(_expert_plan(counts, t // ROUTE_BLOCK), xs, eg_bf, eu_bf, ed_bf)
    cmeta = jnp.pad(meta[0:2].T, ((0, 0), (0, LANES - 2)))
    return _combine(ys, h1, cmeta, out_g)


def kernel(x, norm_mix_g, w_in, conv_w, sgu_ln_g, sgu_ln_b, sgu_w_s, sgu_b_s, w_up_conv, w_up_sgu, w_out, norm_ffn_g, w_router_group, b_router_group, w_router_expert, b_router_expert, w_exp_gate, w_exp_up, w_exp_down, norm_final_g):
    bsz, s, d = x.shape
    depth = w_in.shape[0]
    assert bsz == 1 and depth == 1, "causal conv carry and the fused final norm assume one sequence, one layer"
    assert s % ROUTE_BLOCK == 0
    out = _layer(x.reshape(s, d), norm_mix_g[0], w_in[0], conv_w[0], sgu_ln_g[0], sgu_ln_b[0],
                 sgu_w_s[0], sgu_b_s[0], w_up_conv[0], w_up_sgu[0], w_out[0], norm_ffn_g[0],
                 w_router_group[0], b_router_group[0], w_router_expert[0], b_router_expert[0],
                 w_exp_gate[0], w_exp_up[0], w_exp_down[0], norm_final_g)
    return out.reshape(bsz, s, d)
```

```python
import functools
import math

import jax
import jax.numpy as jnp
from jax import lax
from jax.experimental import pallas as pl
from jax.experimental.pallas import tpu as pltpu

F32 = jnp.float32
BF16 = jnp.bfloat16

EPS = 1e-6
CHUNK = 64
CONV_K = 3
SGU_HEADS = 8
SGU_BLOCK = 128
N_GROUPS = 4
EXPERTS_PER_GROUP = 4
N_EXPERTS = N_GROUPS * EXPERTS_PER_GROUP
ROUTER_ROWS = 32
LANES = 128

VMEM_LIMIT_BYTES = 56 * 1024 * 1024

ROUTE_BLOCK = 512
GROUP = 16
MXU_ROWS = 256
LOCAL_ROWS = -(-(2 * ROUTE_BLOCK + N_EXPERTS * (GROUP - 1)) // MXU_ROWS) * MXU_ROWS
GROUPS_PER_BLOCK = LOCAL_ROWS // GROUP
assert LOCAL_ROWS - (2 * ROUTE_BLOCK + N_EXPERTS * (GROUP - 1)) >= GROUP
MOE_TILE = 512
TILE_GROUPS = MOE_TILE // GROUP


def _params(n_axes):
    return pltpu.CompilerParams(
        dimension_semantics=("arbitrary",) * n_axes,
        vmem_limit_bytes=VMEM_LIMIT_BYTES)


def _dot(a, b):
    return jnp.dot(a, b, preferred_element_type=F32)


def _rms_scale(x, g):
    ms = jnp.mean(x * x, axis=-1, keepdims=True)
    return x * lax.rsqrt(ms + EPS) * g


def _conv_kernel(cast_chunks, xn_ref, wb_ref, wc_ref, wh_ref, cw_ref, *rest):
    n_in = sum(cast_chunks)
    cast_in, o_ref = rest[:n_in], rest[n_in]
    cast_out = rest[n_in + 1:n_in + 1 + len(cast_chunks)]
    wbf_ref, carry_ref = rest[n_in + 1 + len(cast_chunks):]
    i = pl.program_id(1)
    tn = wb_ref.shape[1]
    tm = xn_ref.shape[0]
    src = iter(cast_in)
    for dst, n_chunks in zip(cast_out, cast_chunks):
        wc = dst.shape[1] // n_chunks
        for k in range(n_chunks):
            dst[:, k * wc:(k + 1) * wc] = next(src)[...].astype(BF16)

    @pl.when(i == 0)
    def _():
        wbf_ref[:, 0:tn] = wb_ref[...].astype(BF16)
        wbf_ref[:, tn:2 * tn] = wc_ref[...].astype(BF16)
        wbf_ref[:, 2 * tn:3 * tn] = wh_ref[...].astype(BF16)
        carry_ref[...] = jnp.zeros_like(carry_ref)

    proj = _dot(xn_ref[...], wbf_ref[...])
    b = proj[:, 0:tn]
    p = proj[:, tn:2 * tn] * proj[:, 2 * tn:3 * tn]
    prev = carry_ref[...]
    carry_ref[...] = p[tm - 8:tm, :]
    row = lax.broadcasted_iota(jnp.int32, p.shape, 0)
    p1 = jnp.where(row == 0, prev[7:8, :], pltpu.roll(p, 1, axis=0))
    p2 = jnp.where(row == 0, prev[6:7, :],
                   jnp.where(row == 1, prev[7:8, :], pltpu.roll(p, 2, axis=0)))
    cw = cw_ref[...]
    y = b * (cw[0:1, :] * p2 + cw[1:2, :] * p1 + cw[2:3, :] * p)
    o_ref[...] = y.astype(o_ref.dtype)


def _conv_branch(xn, w_in, conv_w, width, to_cast, tm=1024, tn=256):
    t, d = xn.shape
    nj = width // tn
    ni = t // tm
    steps = nj * ni
    cast_args, cast_in_specs, cast_out_specs, cast_shapes, cast_chunks = [], [], [], [], []
    for arr, col0, n_cols, chunk in to_cast:
        rows = arr.shape[0] // steps
        assert rows * steps == arr.shape[0] and rows % 16 == 0
        assert col0 % chunk == 0 and n_cols % chunk == 0
        for k in range(n_cols // chunk):
            cast_args.append(arr)
            cast_in_specs.append(
                pl.BlockSpec((rows, chunk), lambda j, i, c=col0 // chunk + k: (j * ni + i, c)))
        cast_chunks.append(n_cols // chunk)
        cast_out_specs.append(pl.BlockSpec((rows, n_cols), lambda j, i: (j * ni + i, 0)))
        cast_shapes.append(jax.ShapeDtypeStruct((arr.shape[0], n_cols), BF16))
    outs = pl.pallas_call(
        functools.partial(_conv_kernel, tuple(cast_chunks)),
        grid=(nj, ni),
        in_specs=[pl.BlockSpec((tm, d), lambda j, i: (i, 0)),
                  pl.BlockSpec((d, tn), lambda j, i: (0, j)),
                  pl.BlockSpec((d, tn), lambda j, i: (0, nj + j)),
                  pl.BlockSpec((d, tn), lambda j, i: (0, 2 * nj + j)),
                  pl.BlockSpec((CONV_K, tn), lambda j, i: (0, j))] + cast_in_specs,
        out_specs=[pl.BlockSpec((tm, tn), lambda j, i: (i, j))] + cast_out_specs,
        out_shape=[jax.ShapeDtypeStruct((t, width), BF16)] + cast_shapes,
        scratch_shapes=[pltpu.VMEM((d, 3 * tn), BF16),
                        pltpu.VMEM((8, tn), F32)],
        compiler_params=_params(2),
        name="conv_branch",
    )(xn, w_in, w_in, w_in, conv_w, *cast_args)
    return outs[0], outs[1:]


def _sgu_kernel(x_ref, g_ref, wu_ref, wv_ref, lng_ref, lnb_ref, ws_ref, bsx_ref, xn_ref, o_ref, wbf_ref):
    tm = x_ref.shape[0]
    w = o_ref.shape[1]
    hd = w // SGU_HEADS

    @pl.when(pl.program_id(0) == 0)
    def _():
        wbf_ref[:, 0:w] = wu_ref[...].astype(BF16)
        wbf_ref[:, w:2 * w] = wv_ref[...].astype(BF16)

    xn = _rms_scale(x_ref[...], g_ref[...]).astype(BF16)
    xn_ref[...] = xn
    gz = jax.nn.gelu(_dot(xn, wbf_ref[...]))
    v = gz[:, w:2 * w]
    mu = jnp.mean(v, axis=-1, keepdims=True)
    vc = v - mu
    var = jnp.mean(vc * vc, axis=-1, keepdims=True)
    vn = (vc * lax.rsqrt(var + EPS) * lng_ref[...] + lnb_ref[...]).astype(BF16)
    ii = lax.broadcasted_iota(jnp.int32, (SGU_BLOCK, SGU_BLOCK), 0)
    jj = lax.broadcasted_iota(jnp.int32, (SGU_BLOCK, SGU_BLOCK), 1)
    mask = (jj // CHUNK) <= (ii // CHUNK)
    for h in range(SGU_HEADS):
        wm = jnp.where(mask, ws_ref[h], 0.0).astype(BF16)
        cs = slice(h * hd, (h + 1) * hd)
        for n in range(tm // SGU_BLOCK):
            rs = slice(n * SGU_BLOCK, (n + 1) * SGU_BLOCK)
            vm = _dot(wm, vn[rs, cs]) + bsx_ref[:, cs]
            o_ref[rs, cs] = (gz[rs, cs] * vm).astype(o_ref.dtype)


def _sgu_branch(x, g, w_in, col0, ln_g, ln_b, w_s, b_s, tm=512):
    t, d = x.shape
    w = ln_g.shape[0]
    hd = w // SGU_HEADS
    assert col0 % w == 0
    c0 = col0 // w
    bsx = jnp.repeat(b_s.T, hd, axis=1)
    once = pl.Buffered(1)
    return pl.pallas_call(
        _sgu_kernel,
        grid=(t // tm,),
        in_specs=[pl.BlockSpec((tm, d), lambda i: (i, 0)),
                  pl.BlockSpec((1, d), lambda i: (0, 0)),
                  pl.BlockSpec((d, w), lambda i: (0, c0), pipeline_mode=once),
                  pl.BlockSpec((d, w), lambda i: (0, c0 + 1), pipeline_mode=once),
                  pl.BlockSpec((1, w), lambda i: (0, 0)),
                  pl.BlockSpec((1, w), lambda i: (0, 0)),
                  pl.BlockSpec((SGU_HEADS, SGU_BLOCK, SGU_BLOCK), lambda i: (0, 0, 0)),
                  pl.BlockSpec((SGU_BLOCK, w), lambda i: (0, 0))],
        out_specs=[pl.BlockSpec((tm, d), lambda i: (i, 0)),
                   pl.BlockSpec((tm, w), lambda i: (i, 0))],
        out_shape=[jax.ShapeDtypeStruct((t, d), BF16),
                   jax.ShapeDtypeStruct((t, w), BF16)],
        scratch_shapes=[pltpu.VMEM((d, 2 * w), BF16)],
        compiler_params=_params(1),
        name="sgu_branch",
    )(x, g.reshape(1, d), w_in, w_in, ln_g.reshape(1, w), ln_b.reshape(1, w), w_s, bsx)


def _upgate_kernel(xn_ref, ya_ref, yb_ref, wgc_ref, wgs_ref, wua_ref, wub_ref, eg_ref, eu_ref, ed_ref,
                   o_ref, eg_bf_ref, eu_bf_ref, ed_bf_ref):
    eg_bf_ref[...] = eg_ref[...].astype(BF16)
    eu_bf_ref[...] = eu_ref[...].astype(BF16)
    ed_bf_ref[...] = ed_ref[...].astype(BF16)

    xn = xn_ref[...]
    m = (jax.nn.sigmoid(_dot(xn, wgc_ref[...])) * _dot(ya_ref[...], wua_ref[...])
         + jax.nn.sigmoid(_dot(xn, wgs_ref[...])) * _dot(yb_ref[...], wub_ref[...]))
    o_ref[...] = m.astype(o_ref.dtype)


def _upgate(xn, ya, yb, w_gates, w_up_a, w_up_b, w_eg, w_eu, w_ed, tm=1024, tn=512):
    t, d = xn.shape
    wa = ya.shape[1]
    wb = yb.shape[1]
    dout = w_up_a.shape[1]
    c0 = 0
    nj = dout // tn
    ni = t // tm
    n_e, d_e, f_e = w_eg.shape
    up_rows = n_e * d_e // (nj * ni)
    down_rows = n_e * f_e // (nj * ni)
    assert up_rows * nj * ni == n_e * d_e and up_rows % 16 == 0
    assert down_rows * nj * ni == n_e * f_e and down_rows % 16 == 0
    up_spec = pl.BlockSpec((up_rows, f_e), lambda j, i: (j * ni + i, 0))
    down_spec = pl.BlockSpec((down_rows, d_e), lambda j, i: (j * ni + i, 0))
    m, eg_bf, eu_bf, ed_bf = pl.pallas_call(
        _upgate_kernel,
        grid=(nj, ni),
        in_specs=[pl.BlockSpec((tm, d), lambda j, i: (i, 0)),
                  pl.BlockSpec((tm, wa), lambda j, i: (i, 0)),
                  pl.BlockSpec((tm, wb), lambda j, i: (i, 0)),
                  pl.BlockSpec((d, tn), lambda j, i: (0, c0 + j)),
                  pl.BlockSpec((d, tn), lambda j, i: (0, c0 + nj + j)),
                  pl.BlockSpec((wa, tn), lambda j, i: (0, j)),
                  pl.BlockSpec((wb, tn), lambda j, i: (0, j)),
                  up_spec, up_spec, down_spec],
        out_specs=[pl.BlockSpec((tm, tn), lambda j, i: (i, j)), up_spec, up_spec, down_spec],
        out_shape=[jax.ShapeDtypeStruct((t, dout), BF16),
                   jax.ShapeDtypeStruct((n_e * d_e, f_e), BF16),
                   jax.ShapeDtypeStruct((n_e * d_e, f_e), BF16),
                   jax.ShapeDtypeStruct((n_e * f_e, d_e), BF16)],
        compiler_params=_params(2),
        name="upgate",
    )(xn, ya, yb, w_gates, w_gates, w_up_a, w_up_b,
      w_eg.reshape(n_e * d_e, f_e), w_eu.reshape(n_e * d_e, f_e), w_ed.reshape(n_e * f_e, d_e))
    return (m, eg_bf.reshape(n_e, d_e, f_e), eu_bf.reshape(n_e, d_e, f_e),
            ed_bf.reshape(n_e, f_e, d_e))


def _outproj_kernel(m_ref, w_ref, x_ref, o_ref):
    o_ref[...] = x_ref[...] + _dot(m_ref[...], w_ref[...])


def _outproj(m, w_out_bf, x, tm=512):
    t, d = m.shape
    dout = w_out_bf.shape[1]
    return pl.pallas_call(
        _outproj_kernel,
        grid=(t // tm,),
        in_specs=[pl.BlockSpec((tm, d), lambda i: (i, 0)),
                  pl.BlockSpec((d, dout), lambda i: (0, 0), pipeline_mode=pl.Buffered(1)),
                  pl.BlockSpec((tm, dout), lambda i: (i, 0))],
        out_specs=pl.BlockSpec((tm, dout), lambda i: (i, 0)),
        out_shape=jax.ShapeDtypeStruct((t, dout), F32),
        compiler_params=_params(1),
        name="outproj",
    )(m, w_out_bf, x)


def _argmax_rows(rows):
    best = rows[0]
    idx = jnp.zeros(rows[0].shape, jnp.int32)
    for k in range(1, len(rows)):
        better = rows[k] > best
        best = jnp.where(better, rows[k], best)
        idx = jnp.where(better, k, idx)
    return best, idx


def _softmax_rows(rows):
    mx = functools.reduce(jnp.maximum, rows)
    ex = [jnp.exp(r - mx) for r in rows]
    den = functools.reduce(lambda a, b: a + b, ex)
    return [e / den for e in ex]


def _route_sort_kernel(h_ref, g_ref, wr_ref, br_ref, xs_ref, meta_ref, cnt_ref, before_ref):
    tm = h_ref.shape[0]
    xn = _rms_scale(h_ref[...], g_ref[...])
    xn_hi = xn.astype(BF16)
    xn_lo = (xn - xn_hi.astype(F32)).astype(BF16)
    wr = wr_ref[...]
    wr_hi = wr.astype(BF16)
    wr_lo = (wr - wr_hi.astype(F32)).astype(BF16)
    nt_dims = (((1,), (1,)), ((), ()))
    lt = (lax.dot_general(wr_hi, xn_hi, nt_dims, preferred_element_type=F32)
          + lax.dot_general(wr_hi, xn_lo, nt_dims, preferred_element_type=F32)
          + lax.dot_general(wr_lo, xn_hi, nt_dims, preferred_element_type=F32)) + br_ref[...]
    pgs = _softmax_rows([lt[k:k + 1, :] for k in range(N_GROUPS)])
    pg, gi = _argmax_rows(pgs)
    sel = []
    for k in range(EXPERTS_PER_GROUP):
        r = jnp.zeros_like(pg)
        for g in range(N_GROUPS):
            row = N_GROUPS + g * EXPERTS_PER_GROUP + k
            r = jnp.where(gi == g, lt[row:row + 1, :], r)
        sel.append(r)
    pes = _softmax_rows(sel)
    p1, e1 = _argmax_rows(pes)
    rest = [jnp.where(e1 == k, -1.0, pes[k]) for k in range(EXPERTS_PER_GROUP)]
    p2, e2 = _argmax_rows(rest)
    den = p1 + p2
    w1 = pg * (p1 / den)
    w2 = pg * (p2 / den)
    lo = jnp.minimum(e1, e2)
    hi = jnp.maximum(e1, e2)
    w_lo = jnp.where(e1 < e2, w1, w2)
    w_hi = jnp.where(e1 < e2, w2, w1)
    ea = gi * EXPERTS_PER_GROUP + lo
    eb = gi * EXPERTS_PER_GROUP + hi

    erow = lax.broadcasted_iota(jnp.int32, (N_EXPERTS, tm), 0)
    oh_a = (erow == ea).astype(F32)
    oh_b = (erow == eb).astype(F32)

    @pl.when(pl.program_id(0) == 0)
    def _():
        a = lax.broadcasted_iota(jnp.int32, (tm, tm), 0)
        b = lax.broadcasted_iota(jnp.int32, (tm, tm), 1)
        before_ref[...] = (a < b).astype(BF16)

    cum = _dot((oh_a + oh_b).astype(BF16), before_ref[...])
    cnt = jnp.sum(oh_a + oh_b, axis=1, keepdims=True)
    padded = jnp.floor((cnt + (GROUP - 1)) * (1.0 / GROUP)) * GROUP
    pos_a = jnp.sum(oh_a * cum + jnp.where(erow < ea, padded, 0.0), axis=0, keepdims=True)
    pos_b = jnp.sum(oh_b * cum + jnp.where(erow < eb, padded, 0.0), axis=0, keepdims=True)

    d = h_ref.shape[1]
    q = lax.broadcasted_iota(jnp.int32, (xs_ref.shape[0], tm), 0)
    perm_a = jnp.where(q == pos_a.astype(jnp.int32), 1.0, 0.0).astype(BF16)
    perm_b = jnp.where(q == pos_b.astype(jnp.int32), 1.0, 0.0).astype(BF16)
    xs_ref[:, 0:d] = _dot(perm_a + perm_b, xn_hi).astype(xs_ref.dtype)

    def gate_rows(w):
        hi = w.astype(BF16).astype(F32)
        lo = w - hi
        k = lax.broadcasted_iota(jnp.int32, (LANES, tm), 0)
        return jnp.where(k == 0, hi, jnp.where(k == 1, lo, 0.0)).astype(BF16)

    gates = (lax.dot_general(perm_a, gate_rows(w_lo), nt_dims, preferred_element_type=F32)
             + lax.dot_general(perm_b, gate_rows(w_hi), nt_dims, preferred_element_type=F32))
    xs_ref[:, d:d + LANES] = gates.astype(xs_ref.dtype)

    cnt_ref[...] = jnp.broadcast_to(cnt, cnt_ref.shape)
    meta_ref[0:1, :] = pos_a
    meta_ref[1:2, :] = pos_b
    meta_ref[2:8, :] = jnp.zeros((6, tm), F32)


def _route_sort(h1, g, w_rg, b_rg, w_re, b_re):
    t, d = h1.shape
    tm = ROUTE_BLOCK
    nb = t // tm
    n_log = w_rg.shape[1] + w_re.shape[1]
    wr = jnp.concatenate([w_rg, w_re], axis=1).T
    wr = jnp.pad(wr, ((0, ROUTER_ROWS - n_log), (0, 0)))
    br = jnp.pad(jnp.concatenate([b_rg, b_re]), (0, ROUTER_ROWS - n_log)).reshape(ROUTER_ROWS, 1)
    return pl.pallas_call(
        _route_sort_kernel,
        grid=(nb,),
        in_specs=[pl.BlockSpec((tm, d), lambda i: (i, 0)),
                  pl.BlockSpec((1, d), lambda i: (0, 0)),
                  pl.BlockSpec((ROUTER_ROWS, d), lambda i: (0, 0)),
                  pl.BlockSpec((ROUTER_ROWS, 1), lambda i: (0, 0))],
        out_specs=[pl.BlockSpec((LOCAL_ROWS, d + LANES), lambda i: (i, 0)),
                   pl.BlockSpec((8, tm), lambda i: (0, i)),
                   pl.BlockSpec((N_EXPERTS, LANES), lambda i: (i, 0))],
        out_shape=[jax.ShapeDtypeStruct((nb * LOCAL_ROWS, d + LANES), BF16),
                   jax.ShapeDtypeStruct((8, t), F32),
                   jax.ShapeDtypeStruct((nb * N_EXPERTS, LANES), F32)],
        scratch_shapes=[pltpu.VMEM((tm, tm), BF16)],
        compiler_params=_params(1),
        name="route_sort",
    )(h1, g.reshape(1, d), wr, br)


def _group_copy(src_ref, s_group, dst_ref, d_group, sem):
    return pltpu.make_async_copy(src_ref.at[pl.ds(pl.multiple_of(s_group * GROUP, GROUP), GROUP), :],
                                 dst_ref.at[pl.ds(pl.multiple_of(d_group * GROUP, GROUP), GROUP), :], sem)


def _expert_kernel(te_ref, nxt_ref, nreal_ref, nt_ref, gsrc_ref, gdst_ref, tail_ref,
                   xs_ref, wg_ref, wu_ref, wd_ref, ys_ref,
                   xbuf_ref, ybuf_ref, zero_ref, wgu_buf_ref, wd_buf_ref,
                   cur_ref, gsem, ssem, zsem, wsem):
    r = pl.program_id(0)
    nt = nt_ref[0]

    def gather(q, slot):
        for i in range(TILE_GROUPS):
            _group_copy(xs_ref, gsrc_ref[q * TILE_GROUPS + i], xbuf_ref, slot * TILE_GROUPS + i,
                        gsem.at[slot]).start(priority=1)

    def weight_copies(e, wslot):
        f = wg_ref.shape[2]
        return (pltpu.make_async_copy(wg_ref.at[e], wgu_buf_ref.at[wslot, :, pl.ds(0, f)], wsem.at[wslot]),
                pltpu.make_async_copy(wu_ref.at[e], wgu_buf_ref.at[wslot, :, pl.ds(f, f)], wsem.at[wslot]),
                pltpu.make_async_copy(wd_ref.at[e], wd_buf_ref.at[wslot], wsem.at[wslot]))

    def tile_wait(src, dst, sem):
        pltpu.make_async_copy(src.at[pl.ds(0, MOE_TILE), :], dst.at[pl.ds(0, MOE_TILE), :], sem).wait()

    @pl.when(r == 0)
    def _():
        cur_ref[0] = -1
        cur_ref[1] = -1
        for cp in weight_copies(te_ref[0], 0):
            cp.start()
        gather(0, 0)
        zero_ref[...] = jnp.zeros_like(zero_ref)
        ybuf_ref[...] = jnp.zeros_like(ybuf_ref)
        n_blocks = tail_ref.shape[0]

        def fill(make):
            def body(g, c):
                make(g)
                return c
            return body

        for blk in range(n_blocks):
            lax.fori_loop(tail_ref[blk], GROUPS_PER_BLOCK, fill(
                lambda g, blk=blk: _group_copy(zero_ref, 0, ys_ref, blk * GROUPS_PER_BLOCK + g, zsem).start()), 0)
        for blk in range(n_blocks):
            lax.fori_loop(tail_ref[blk], GROUPS_PER_BLOCK, fill(
                lambda g, blk=blk: _group_copy(zero_ref, 0, ys_ref, blk * GROUPS_PER_BLOCK + g, zsem).wait()), 0)

    @pl.when(r < nt)
    def _():
        e = te_ref[r]
        slot = r % 2

        gather(jnp.minimum(r + 1, nt - 1), 1 - slot)

        @pl.when(cur_ref[0] != e)
        def _():
            wslot = (cur_ref[1] + 1) % 2
            for cp in weight_copies(e, wslot):
                cp.wait()
            cur_ref[0] = e
            cur_ref[1] = cur_ref[1] + 1

            @pl.when(nxt_ref[r] >= 0)
            def _():
                for cp in weight_copies(nxt_ref[r], 1 - wslot):
                    cp.start()

        wslot = cur_ref[1] % 2
        tile_wait(xs_ref, xbuf_ref, gsem.at[slot])

        @pl.when(r >= 2)
        def _():
            tile_wait(ybuf_ref, ys_ref, ssem.at[slot])

        d = ybuf_ref.shape[1]
        f = wd_buf_ref.shape[1]
        def chain(part):
            rows = pl.ds(pl.multiple_of(slot * MOE_TILE + part * MXU_ROWS, MXU_ROWS), MXU_ROWS)
            x = xbuf_ref[rows, 0:d]
            gate_parts = xbuf_ref[rows, d:d + LANES].astype(F32)
            gate = gate_parts[:, 0:1] + gate_parts[:, 1:2]
            gu = _dot(x, wgu_buf_ref[wslot])
            hid = (jax.nn.silu(gu[:, 0:f]) * gu[:, f:2 * f]).astype(BF16)
            ybuf_ref[rows, :] = (_dot(hid, wd_buf_ref[wslot]) * gate).astype(ybuf_ref.dtype)

        chunk_groups = MXU_ROWS // GROUP
        n_chunks = (nreal_ref[r] + chunk_groups - 1) // chunk_groups
        for count in range(1, MOE_TILE // MXU_ROWS + 1):
            @pl.when(n_chunks == count)
            def _(count=count):
                for part in range(count):
                    chain(part)

        for i in range(TILE_GROUPS):
            _group_copy(ybuf_ref, slot * TILE_GROUPS + i, ys_ref, gdst_ref[r * TILE_GROUPS + i],
                        ssem.at[slot]).start(priority=1)

        @pl.when(r == nt - 1)
        def _():
            tile_wait(xs_ref, xbuf_ref, gsem.at[1 - slot])
            tile_wait(ybuf_ref, ys_ref, ssem.at[slot])

            @pl.when(r >= 1)
            def _():
                tile_wait(ybuf_ref, ys_ref, ssem.at[1 - slot])


def _experts(plan, xs, w_gate, w_up, w_down):
    d = w_gate.shape[1]
    f = w_gate.shape[2]
    tile_e, next_e, n_real, n_tiles, gsrc, gdst, tail = plan
    r_max = tile_e.shape[0]
    n_blocks = tail.shape[0]
    any_spec = pl.BlockSpec(memory_space=pl.ANY)
    grid_spec = pltpu.PrefetchScalarGridSpec(
        num_scalar_prefetch=7,
        grid=(r_max,),
        in_specs=[any_spec, any_spec, any_spec, any_spec],
        out_specs=any_spec,
        scratch_shapes=[pltpu.VMEM((2 * MOE_TILE, d + LANES), BF16),
                        pltpu.VMEM((2 * MOE_TILE, d), BF16),
                        pltpu.VMEM((GROUP, d), BF16),
                        pltpu.VMEM((2, d, 2 * f), BF16),
                        pltpu.VMEM((2, f, d), BF16),
                        pltpu.SMEM((2,), jnp.int32),
                        pltpu.SemaphoreType.DMA((2,)),
                        pltpu.SemaphoreType.DMA((2,)),
                        pltpu.SemaphoreType.DMA(()),
                        pltpu.SemaphoreType.DMA((2,))],
    )
    return pl.pallas_call(
        _expert_kernel,
        grid_spec=grid_spec,
        out_shape=jax.ShapeDtypeStruct((n_blocks * LOCAL_ROWS, d), BF16),
        compiler_params=_params(1),
        name="experts",
    )(tile_e, next_e, n_real, n_tiles, gsrc, gdst, tail, xs, w_gate, w_up, w_down)


def _combine_kernel(ys_ref, h_ref, cm_ref, g_ref, o_ref):
    tm = h_ref.shape[0]
    cm = cm_ref[...]
    q = lax.broadcasted_iota(jnp.int32, (tm, ys_ref.shape[0]), 1)
    sel = jnp.where((q == cm[:, 0:1].astype(jnp.int32)) | (q == cm[:, 1:2].astype(jnp.int32)), 1.0, 0.0)
    h2 = h_ref[...] + _dot(sel.astype(BF16), ys_ref[...])
    o_ref[...] = _rms_scale(h2, g_ref[...])


def _combine(ys, h1, cmeta, g):
    t, d = h1.shape
    tm = ROUTE_BLOCK
    return pl.pallas_call(
        _combine_kernel,
        grid=(t // tm,),
        in_specs=[pl.BlockSpec((LOCAL_ROWS, d), lambda i: (i, 0)),
                  pl.BlockSpec((tm, d), lambda i: (i, 0)),
                  pl.BlockSpec((tm, LANES), lambda i: (i, 0)),
                  pl.BlockSpec((1, d), lambda i: (0, 0))],
        out_specs=pl.BlockSpec((tm, d), lambda i: (i, 0)),
        out_shape=jax.ShapeDtypeStruct((t, d), F32),
        compiler_params=_params(1),
        name="combine",
    )(ys, h1, cmeta, g.reshape(1, d))


def _expert_plan(counts, n_blocks):
    cnt = counts[:, 0].astype(jnp.int32).reshape(n_blocks, N_EXPERTS)
    groups = (cnt + GROUP - 1) // GROUP
    first = jnp.cumsum(groups, axis=1) - groups
    upto = jnp.cumsum(groups, axis=0)
    per_expert = upto[-1]
    tiles_e = (per_expert + TILE_GROUPS - 1) // TILE_GROUPS
    tile_end = jnp.cumsum(tiles_e)
    n_tiles = tile_end[-1]
    max_groups = 2 * ROUTE_BLOCK * n_blocks // GROUP + n_blocks * N_EXPERTS
    r_max = max_groups // TILE_GROUPS + N_EXPERTS
    tile_ids = jnp.arange(r_max, dtype=jnp.int32)
    tile = jnp.minimum(tile_ids, n_tiles - 1)
    tile_e = jnp.sum((tile_end[None, :] <= tile[:, None]).astype(jnp.int32), axis=1)
    later = (tile_e[None, :] > tile_e[:, None]) & (tile_ids[None, :] < n_tiles)
    next_e = jnp.min(jnp.where(later, tile_e[None, :], N_EXPERTS), axis=1)
    next_e = jnp.where(next_e == N_EXPERTS, -1, next_e)

    slot = jnp.arange(r_max * TILE_GROUPS, dtype=jnp.int32)
    s_tile = slot // TILE_GROUPS
    oh_e = jnp.repeat(tile_e, TILE_GROUPS)[:, None] == jnp.arange(N_EXPERTS, dtype=jnp.int32)[None, :]

    def by_expert(table):
        return jnp.sum(jnp.where(oh_e[:, None, :], table[None], 0), axis=-1)

    k = slot - by_expert(((tile_end - tiles_e) * TILE_GROUPS)[None, :])[:, 0]
    real = (k < by_expert(per_expert[None, :])[:, 0]) & (s_tile < n_tiles)
    upto_e = by_expert(upto)
    blk = jnp.minimum(jnp.sum((upto_e <= k[:, None]).astype(jnp.int32), axis=1), n_blocks - 1)
    oh_b = blk[:, None] == jnp.arange(n_blocks, dtype=jnp.int32)[None, :]

    def by_block(table_se):
        return jnp.sum(jnp.where(oh_b, table_se, 0), axis=1)

    before = by_block(upto_e - by_expert(groups))
    src = blk * GROUPS_PER_BLOCK + by_block(by_expert(first)) + (k - before)
    zero_group = GROUPS_PER_BLOCK - 1
    spare = n_blocks * GROUPS_PER_BLOCK + (s_tile % 2) * TILE_GROUPS + slot % TILE_GROUPS
    gsrc = jnp.where(real, src, zero_group)
    gdst = jnp.where(real, src, spare)
    n_spare_blocks = -(-2 * TILE_GROUPS // GROUPS_PER_BLOCK)
    tail = jnp.concatenate([jnp.sum(groups, axis=1), jnp.zeros((n_spare_blocks,), jnp.int32)])
    n_real = jnp.sum(real.reshape(r_max, TILE_GROUPS).astype(jnp.int32), axis=1)
    return tile_e, next_e, n_real, n_tiles.reshape(1), gsrc, gdst, tail


def _layer(h, norm_mix_g, w_in, conv_w, sgu_ln_g, sgu_ln_b, sgu_w_s, sgu_b_s, w_up_conv,
           w_up_sgu, w_out, norm_ffn_g, w_rg, b_rg, w_re, b_re, w_eg, w_eu, w_ed, out_g):
    t, d = h.shape
    conv_width = conv_w.shape[1]
    sgu_width = sgu_ln_g.shape[0]
    xn, yb = _sgu_branch(h, norm_mix_g, w_in, 3 * conv_width, sgu_ln_g, sgu_ln_b, sgu_w_s, sgu_b_s)
    gate_col0 = 3 * conv_width + 2 * sgu_width
    ya, (wg_bf, wua_bf, wub_bf, wout_bf) = _conv_branch(
        xn, w_in, conv_w, conv_width,
        to_cast=[(w_in, gate_col0, w_in.shape[1] - gate_col0, math.gcd(gate_col0, w_in.shape[1])),
                 (w_up_conv, 0, d, d), (w_up_sgu, 0, d, d), (w_out, 0, d, d)])
    m, eg_bf, eu_bf, ed_bf = _upgate(xn, ya, yb, wg_bf, wua_bf, wub_bf, w_eg, w_eu, w_ed)
    h1 = _outproj(m, wout_bf, h)
    xs, meta, counts = _route_sort(h1, norm_ffn_g, w_rg, b_rg, w_re, b_re)
    ys = _experts(_expert_plan(counts, t // ROUTE_BLOCK), xs, eg_bf, eu_bf, ed_bf)
    cmeta = jnp.pad(meta[0:2].T, ((0, 0), (0, LANES - 2)))
    return _combine(ys, h1, cmeta, out_g)


def kernel(x, norm_mix_g, w_in, conv_w, sgu_ln_g, sgu_ln_b, sgu_w_s, sgu_b_s, w_up_conv, w_up_sgu, w_out, norm_ffn_g, w_router_group, b_router_group, w_router_expert, b_router_expert, w_exp_gate, w_exp_up, w_exp_down, norm_final_g):
    bsz, s, d = x.shape
    depth = w_in.shape[0]
    assert bsz == 1 and depth == 1, "causal conv carry and the fused final norm assume one sequence, one layer"
    assert s % ROUTE_BLOCK == 0
    out = _layer(x.reshape(s, d), norm_mix_g[0], w_in[0], conv_w[0], sgu_ln_g[0], sgu_ln_b[0],
                 sgu_w_s[0], sgu_b_s[0], w_up_conv[0], w_up_sgu[0], w_out[0], norm_ffn_g[0],
                 w_router_group[0], b_router_group[0], w_router_expert[0], b_router_expert[0],
                 w_exp_gate[0], w_exp_up[0], w_exp_down[0], norm_final_g)
    return out.reshape(bsz, s, d)
```

```python
import functools
import math

import jax
import jax.numpy as jnp
from jax import lax
from jax.experimental import pallas as pl
from jax.experimental.pallas import tpu as pltpu

F32 = jnp.float32
BF16 = jnp.bfloat16

EPS = 1e-6
CHUNK = 64
CONV_K = 3
SGU_HEADS = 8
SGU_BLOCK = 128
N_GROUPS = 4
EXPERTS_PER_GROUP = 4
N_EXPERTS = N_GROUPS * EXPERTS_PER_GROUP
ROUTER_ROWS = 32
LANES = 128

VMEM_LIMIT_BYTES = 56 * 1024 * 1024

ROUTE_BLOCK = 512
GROUP = 16
MXU_ROWS = 256
HALF_ROWS = MXU_ROWS // 2
LOCAL_ROWS = -(-(2 * ROUTE_BLOCK + N_EXPERTS * (GROUP - 1)) // MXU_ROWS) * MXU_ROWS
GROUPS_PER_BLOCK = LOCAL_ROWS // GROUP
assert LOCAL_ROWS - (2 * ROUTE_BLOCK + N_EXPERTS * (GROUP - 1)) >= GROUP
MOE_TILE = 512
TILE_GROUPS = MOE_TILE // GROUP


def _params(n_axes):
    return pltpu.CompilerParams(
        dimension_semantics=("arbitrary",) * n_axes,
        vmem_limit_bytes=VMEM_LIMIT_BYTES)


def _dot(a, b):
    return jnp.dot(a, b, preferred_element_type=F32)


def _rms_scale(x, g):
    ms = jnp.mean(x * x, axis=-1, keepdims=True)
    return x * lax.rsqrt(ms + EPS) * g


def _conv_kernel(cast_chunks, xn_ref, wb_ref, wc_ref, wh_ref, cw_ref, *rest):
    n_in = sum(cast_chunks)
    cast_in, o_ref = rest[:n_in], rest[n_in]
    cast_out = rest[n_in + 1:n_in + 1 + len(cast_chunks)]
    wbf_ref, carry_ref = rest[n_in + 1 + len(cast_chunks):]
    i = pl.program_id(1)
    tn = wb_ref.shape[1]
    tm = xn_ref.shape[0]
    src = iter(cast_in)
    for dst, n_chunks in zip(cast_out, cast_chunks):
        wc = dst.shape[1] // n_chunks
        for k in range(n_chunks):
            dst[:, k * wc:(k + 1) * wc] = next(src)[...].astype(BF16)

    @pl.when(i == 0)
    def _():
        wbf_ref[:, 0:tn] = wb_ref[...].astype(BF16)
        wbf_ref[:, tn:2 * tn] = wc_ref[...].astype(BF16)
        wbf_ref[:, 2 * tn:3 * tn] = wh_ref[...].astype(BF16)
        carry_ref[...] = jnp.zeros_like(carry_ref)

    proj = _dot(xn_ref[...], wbf_ref[...])
    b = proj[:, 0:tn]
    p = proj[:, tn:2 * tn] * proj[:, 2 * tn:3 * tn]
    prev = carry_ref[...]
    carry_ref[...] = p[tm - 8:tm, :]
    row = lax.broadcasted_iota(jnp.int32, p.shape, 0)
    p1 = jnp.where(row == 0, prev[7:8, :], pltpu.roll(p, 1, axis=0))
    p2 = jnp.where(row == 0, prev[6:7, :],
                   jnp.where(row == 1, prev[7:8, :], pltpu.roll(p, 2, axis=0)))
    cw = cw_ref[...]
    y = b * (cw[0:1, :] * p2 + cw[1:2, :] * p1 + cw[2:3, :] * p)
    o_ref[...] = y.astype(o_ref.dtype)


def _conv_branch(xn, w_in, conv_w, width, to_cast, tm=1024, tn=256):
    t, d = xn.shape
    nj = width // tn
    ni = t // tm
    steps = nj * ni
    cast_args, cast_in_specs, cast_out_specs, cast_shapes, cast_chunks = [], [], [], [], []
    for arr, col0, n_cols, chunk in to_cast:
        rows = arr.shape[0] // steps
        assert rows * steps == arr.shape[0] and rows % 16 == 0
        assert col0 % chunk == 0 and n_cols % chunk == 0
        for k in range(n_cols // chunk):
            cast_args.append(arr)
            cast_in_specs.append(
                pl.BlockSpec((rows, chunk), lambda j, i, c=col0 // chunk + k: (j * ni + i, c)))
        cast_chunks.append(n_cols // chunk)
        cast_out_specs.append(pl.BlockSpec((rows, n_cols), lambda j, i: (j * ni + i, 0)))
        cast_shapes.append(jax.ShapeDtypeStruct((arr.shape[0], n_cols), BF16))
    outs = pl.pallas_call(
        functools.partial(_conv_kernel, tuple(cast_chunks)),
        grid=(nj, ni),
        in_specs=[pl.BlockSpec((tm, d), lambda j, i: (i, 0)),
                  pl.BlockSpec((d, tn), lambda j, i: (0, j)),
                  pl.BlockSpec((d, tn), lambda j, i: (0, nj + j)),
                  pl.BlockSpec((d, tn), lambda j, i: (0, 2 * nj + j)),
                  pl.BlockSpec((CONV_K, tn), lambda j, i: (0, j))] + cast_in_specs,
        out_specs=[pl.BlockSpec((tm, tn), lambda j, i: (i, j))] + cast_out_specs,
        out_shape=[jax.ShapeDtypeStruct((t, width), BF16)] + cast_shapes,
        scratch_shapes=[pltpu.VMEM((d, 3 * tn), BF16),
                        pltpu.VMEM((8, tn), F32)],
        compiler_params=_params(2),
        name="conv_branch",
    )(xn, w_in, w_in, w_in, conv_w, *cast_args)
    return outs[0], outs[1:]


def _sgu_kernel(x_ref, g_ref, wu_ref, wv_ref, lng_ref, lnb_ref, ws_ref, bsx_ref, xn_ref, o_ref, wbf_ref):
    tm = x_ref.shape[0]
    w = o_ref.shape[1]
    hd = w // SGU_HEADS

    @pl.when(pl.program_id(0) == 0)
    def _():
        wbf_ref[:, 0:w] = wu_ref[...].astype(BF16)
        wbf_ref[:, w:2 * w] = wv_ref[...].astype(BF16)

    xn = _rms_scale(x_ref[...], g_ref[...]).astype(BF16)
    xn_ref[...] = xn
    gz = jax.nn.gelu(_dot(xn, wbf_ref[...]))
    v = gz[:, w:2 * w]
    mu = jnp.mean(v, axis=-1, keepdims=True)
    vc = v - mu
    var = jnp.mean(vc * vc, axis=-1, keepdims=True)
    vn = (vc * lax.rsqrt(var + EPS) * lng_ref[...] + lnb_ref[...]).astype(BF16)
    ii = lax.broadcasted_iota(jnp.int32, (SGU_BLOCK, SGU_BLOCK), 0)
    jj = lax.broadcasted_iota(jnp.int32, (SGU_BLOCK, SGU_BLOCK), 1)
    mask = (jj // CHUNK) <= (ii // CHUNK)
    for h in range(SGU_HEADS):
        wm = jnp.where(mask, ws_ref[h], 0.0).astype(BF16)
        cs = slice(h * hd, (h + 1) * hd)
        for n in range(tm // SGU_BLOCK):
            rs = slice(n * SGU_BLOCK, (n + 1) * SGU_BLOCK)
            vm = _dot(wm, vn[rs, cs]) + bsx_ref[:, cs]
            o_ref[rs, cs] = (gz[rs, cs] * vm).astype(o_ref.dtype)


def _sgu_branch(x, g, w_in, col0, ln_g, ln_b, w_s, b_s, tm=512):
    t, d = x.shape
    w = ln_g.shape[0]
    hd = w // SGU_HEADS
    assert col0 % w == 0
    c0 = col0 // w
    bsx = jnp.repeat(b_s.T, hd, axis=1)
    once = pl.Buffered(1)
    return pl.pallas_call(
        _sgu_kernel,
        grid=(t // tm,),
        in_specs=[pl.BlockSpec((tm, d), lambda i: (i, 0)),
                  pl.BlockSpec((1, d), lambda i: (0, 0)),
                  pl.BlockSpec((d, w), lambda i: (0, c0), pipeline_mode=once),
                  pl.BlockSpec((d, w), lambda i: (0, c0 + 1), pipeline_mode=once),
                  pl.BlockSpec((1, w), lambda i: (0, 0)),
                  pl.BlockSpec((1, w), lambda i: (0, 0)),
                  pl.BlockSpec((SGU_HEADS, SGU_BLOCK, SGU_BLOCK), lambda i: (0, 0, 0)),
                  pl.BlockSpec((SGU_BLOCK, w), lambda i: (0, 0))],
        out_specs=[pl.BlockSpec((tm, d), lambda i: (i, 0)),
                   pl.BlockSpec((tm, w), lambda i: (i, 0))],
        out_shape=[jax.ShapeDtypeStruct((t, d), BF16),
                   jax.ShapeDtypeStruct((t, w), BF16)],
        scratch_shapes=[pltpu.VMEM((d, 2 * w), BF16)],
        compiler_params=_params(1),
        name="sgu_branch",
    )(x, g.reshape(1, d), w_in, w_in, ln_g.reshape(1, w), ln_b.reshape(1, w), w_s, bsx)


def _upgate_kernel(xn_ref, ya_ref, yb_ref, wgc_ref, wgs_ref, wua_ref, wub_ref, eg_ref, eu_ref, ed_ref,
                   o_ref, eg_bf_ref, eu_bf_ref, ed_bf_ref):
    eg_bf_ref[...] = eg_ref[...].astype(BF16)
    eu_bf_ref[...] = eu_ref[...].astype(BF16)
    ed_bf_ref[...] = ed_ref[...].astype(BF16)

    xn = xn_ref[...]
    m = (jax.nn.sigmoid(_dot(xn, wgc_ref[...])) * _dot(ya_ref[...], wua_ref[...])
         + jax.nn.sigmoid(_dot(xn, wgs_ref[...])) * _dot(yb_ref[...], wub_ref[...]))
    o_ref[...] = m.astype(o_ref.dtype)


def _upgate(xn, ya, yb, w_gates, w_up_a, w_up_b, w_eg, w_eu, w_ed, tm=1024, tn=512):
    t, d = xn.shape
    wa = ya.shape[1]
    wb = yb.shape[1]
    dout = w_up_a.shape[1]
    c0 = 0
    nj = dout // tn
    ni = t // tm
    n_e, d_e, f_e = w_eg.shape
    up_rows = n_e * d_e // (nj * ni)
    down_rows = n_e * f_e // (nj * ni)
    assert up_rows * nj * ni == n_e * d_e and up_rows % 16 == 0
    assert down_rows * nj * ni == n_e * f_e and down_rows % 16 == 0
    up_spec = pl.BlockSpec((up_rows, f_e), lambda j, i: (j * ni + i, 0))
    down_spec = pl.BlockSpec((down_rows, d_e), lambda j, i: (j * ni + i, 0))
    m, eg_bf, eu_bf, ed_bf = pl.pallas_call(
        _upgate_kernel,
        grid=(nj, ni),
        in_specs=[pl.BlockSpec((tm, d), lambda j, i: (i, 0)),
                  pl.BlockSpec((tm, wa), lambda j, i: (i, 0)),
                  pl.BlockSpec((tm, wb), lambda j, i: (i, 0)),
                  pl.BlockSpec((d, tn), lambda j, i: (0, c0 + j)),
                  pl.BlockSpec((d, tn), lambda j, i: (0, c0 + nj + j)),
                  pl.BlockSpec((wa, tn), lambda j, i: (0, j)),
                  pl.BlockSpec((wb, tn), lambda j, i: (0, j)),
                  up_spec, up_spec, down_spec],
        out_specs=[pl.BlockSpec((tm, tn), lambda j, i: (i, j)), up_spec, up_spec, down_spec],
        out_shape=[jax.ShapeDtypeStruct((t, dout), BF16),
                   jax.ShapeDtypeStruct((n_e * d_e, f_e), BF16),
                   jax.ShapeDtypeStruct((n_e * d_e, f_e), BF16),
                   jax.ShapeDtypeStruct((n_e * f_e, d_e), BF16)],
        compiler_params=_params(2),
        name="upgate",
    )(xn, ya, yb, w_gates, w_gates, w_up_a, w_up_b,
      w_eg.reshape(n_e * d_e, f_e), w_eu.reshape(n_e * d_e, f_e), w_ed.reshape(n_e * f_e, d_e))
    return (m, eg_bf.reshape(n_e, d_e, f_e), eu_bf.reshape(n_e, d_e, f_e),
            ed_bf.reshape(n_e, f_e, d_e))


def _outproj_kernel(m_ref, w_ref, x_ref, o_ref):
    o_ref[...] = x_ref[...] + _dot(m_ref[...], w_ref[...])


def _outproj(m, w_out_bf, x, tm=512):
    t, d = m.shape
    dout = w_out_bf.shape[1]
    return pl.pallas_call(
        _outproj_kernel,
        grid=(t // tm,),
        in_specs=[pl.BlockSpec((tm, d), lambda i: (i, 0)),
                  pl.BlockSpec((d, dout), lambda i: (0, 0), pipeline_mode=pl.Buffered(1)),
                  pl.BlockSpec((tm, dout), lambda i: (i, 0))],
        out_specs=pl.BlockSpec((tm, dout), lambda i: (i, 0)),
        out_shape=jax.ShapeDtypeStruct((t, dout), F32),
        compiler_params=_params(1),
        name="outproj",
    )(m, w_out_bf, x)


def _argmax_rows(rows):
    best = rows[0]
    idx = jnp.zeros(rows[0].shape, jnp.int32)
    for k in range(1, len(rows)):
        better = rows[k] > best
        best = jnp.where(better, rows[k], best)
        idx = jnp.where(better, k, idx)
    return best, idx


def _softmax_rows(rows):
    mx = functools.reduce(jnp.maximum, rows)
    ex = [jnp.exp(r - mx) for r in rows]
    den = functools.reduce(lambda a, b: a + b, ex)
    return [e / den for e in ex]


def _route_sort_kernel(h_ref, g_ref, wr_ref, br_ref, xs_ref, meta_ref, cnt_ref, before_ref):
    tm = h_ref.shape[0]
    xn = _rms_scale(h_ref[...], g_ref[...])
    xn_hi = xn.astype(BF16)
    xn_lo = (xn - xn_hi.astype(F32)).astype(BF16)
    wr = wr_ref[...]
    wr_hi = wr.astype(BF16)
    wr_lo = (wr - wr_hi.astype(F32)).astype(BF16)
    nt_dims = (((1,), (1,)), ((), ()))
    lt = (lax.dot_general(wr_hi, xn_hi, nt_dims, preferred_element_type=F32)
          + lax.dot_general(wr_hi, xn_lo, nt_dims, preferred_element_type=F32)
          + lax.dot_general(wr_lo, xn_hi, nt_dims, preferred_element_type=F32)) + br_ref[...]
    pgs = _softmax_rows([lt[k:k + 1, :] for k in range(N_GROUPS)])
    pg, gi = _argmax_rows(pgs)
    sel = []
    for k in range(EXPERTS_PER_GROUP):
        r = jnp.zeros_like(pg)
        for g in range(N_GROUPS):
            row = N_GROUPS + g * EXPERTS_PER_GROUP + k
            r = jnp.where(gi == g, lt[row:row + 1, :], r)
        sel.append(r)
    pes = _softmax_rows(sel)
    p1, e1 = _argmax_rows(pes)
    rest = [jnp.where(e1 == k, -1.0, pes[k]) for k in range(EXPERTS_PER_GROUP)]
    p2, e2 = _argmax_rows(rest)
    den = p1 + p2
    w1 = pg * (p1 / den)
    w2 = pg * (p2 / den)
    lo = jnp.minimum(e1, e2)
    hi = jnp.maximum(e1, e2)
    w_lo = jnp.where(e1 < e2, w1, w2)
    w_hi = jnp.where(e1 < e2, w2, w1)
    ea = gi * EXPERTS_PER_GROUP + lo
    eb = gi * EXPERTS_PER_GROUP + hi

    erow = lax.broadcasted_iota(jnp.int32, (N_EXPERTS, tm), 0)
    oh_a = (erow == ea).astype(F32)
    oh_b = (erow == eb).astype(F32)

    @pl.when(pl.program_id(0) == 0)
    def _():
        a = lax.broadcasted_iota(jnp.int32, (tm, tm), 0)
        b = lax.broadcasted_iota(jnp.int32, (tm, tm), 1)
        before_ref[...] = (a < b).astype(BF16)

    cum = _dot((oh_a + oh_b).astype(BF16), before_ref[...])
    cnt = jnp.sum(oh_a + oh_b, axis=1, keepdims=True)
    padded = jnp.floor((cnt + (GROUP - 1)) * (1.0 / GROUP)) * GROUP
    pos_a = jnp.sum(oh_a * cum + jnp.where(erow < ea, padded, 0.0), axis=0, keepdims=True)
    pos_b = jnp.sum(oh_b * cum + jnp.where(erow < eb, padded, 0.0), axis=0, keepdims=True)

    d = h_ref.shape[1]
    q = lax.broadcasted_iota(jnp.int32, (xs_ref.shape[0], tm), 0)
    perm_a = jnp.where(q == pos_a.astype(jnp.int32), 1.0, 0.0).astype(BF16)
    perm_b = jnp.where(q == pos_b.astype(jnp.int32), 1.0, 0.0).astype(BF16)
    xs_ref[:, 0:d] = _dot(perm_a + perm_b, xn_hi).astype(xs_ref.dtype)

    def gate_rows(w):
        hi = w.astype(BF16).astype(F32)
        lo = w - hi
        k = lax.broadcasted_iota(jnp.int32, (LANES, tm), 0)
        return jnp.where(k == 0, hi, jnp.where(k == 1, lo, 0.0)).astype(BF16)

    gates = (lax.dot_general(perm_a, gate_rows(w_lo), nt_dims, preferred_element_type=F32)
             + lax.dot_general(perm_b, gate_rows(w_hi), nt_dims, preferred_element_type=F32))
    xs_ref[:, d:d + LANES] = gates.astype(xs_ref.dtype)

    cnt_ref[...] = jnp.broadcast_to(cnt, cnt_ref.shape)
    meta_ref[0:1, :] = pos_a
    meta_ref[1:2, :] = pos_b
    meta_ref[2:8, :] = jnp.zeros((6, tm), F32)


def _route_sort(h1, g, w_rg, b_rg, w_re, b_re):
    t, d = h1.shape
    tm = ROUTE_BLOCK
    nb = t // tm
    n_log = w_rg.shape[1] + w_re.shape[1]
    wr = jnp.concatenate([w_rg, w_re], axis=1).T
    wr = jnp.pad(wr, ((0, ROUTER_ROWS - n_log), (0, 0)))
    br = jnp.pad(jnp.concatenate([b_rg, b_re]), (0, ROUTER_ROWS - n_log)).reshape(ROUTER_ROWS, 1)
    return pl.pallas_call(
        _route_sort_kernel,
        grid=(nb,),
        in_specs=[pl.BlockSpec((tm, d), lambda i: (i, 0)),
                  pl.BlockSpec((1, d), lambda i: (0, 0)),
                  pl.BlockSpec((ROUTER_ROWS, d), lambda i: (0, 0)),
                  pl.BlockSpec((ROUTER_ROWS, 1), lambda i: (0, 0))],
        out_specs=[pl.BlockSpec((LOCAL_ROWS, d + LANES), lambda i: (i, 0)),
                   pl.BlockSpec((8, tm), lambda i: (0, i)),
                   pl.BlockSpec((N_EXPERTS, LANES), lambda i: (i, 0))],
        out_shape=[jax.ShapeDtypeStruct((nb * LOCAL_ROWS, d + LANES), BF16),
                   jax.ShapeDtypeStruct((8, t), F32),
                   jax.ShapeDtypeStruct((nb * N_EXPERTS, LANES), F32)],
        scratch_shapes=[pltpu.VMEM((tm, tm), BF16)],
        compiler_params=_params(1),
        name="route_sort",
    )(h1, g.reshape(1, d), wr, br)


def _group_copy(src_ref, s_group, dst_ref, d_group, sem):
    return pltpu.make_async_copy(src_ref.at[pl.ds(pl.multiple_of(s_group * GROUP, GROUP), GROUP), :],
                                 dst_ref.at[pl.ds(pl.multiple_of(d_group * GROUP, GROUP), GROUP), :], sem)


def _expert_kernel(te_ref, nxt_ref, nreal_ref, nt_ref, gsrc_ref, gdst_ref, tail_ref,
                   xs_ref, wg_ref, wu_ref, wd_ref, ys_ref,
                   xbuf_ref, ybuf_ref, zero_ref, wgu_buf_ref, wd_buf_ref,
                   cur_ref, gsem, ssem, zsem, wsem):
    r = pl.program_id(0)
    nt = nt_ref[0]

    def gather(q, slot):
        for i in range(TILE_GROUPS):
            _group_copy(xs_ref, gsrc_ref[q * TILE_GROUPS + i], xbuf_ref, slot * TILE_GROUPS + i,
                        gsem.at[slot]).start(priority=1)

    def weight_copies(e, wslot):
        f = wg_ref.shape[2]
        return (pltpu.make_async_copy(wg_ref.at[e], wgu_buf_ref.at[wslot, :, pl.ds(0, f)], wsem.at[wslot]),
                pltpu.make_async_copy(wu_ref.at[e], wgu_buf_ref.at[wslot, :, pl.ds(f, f)], wsem.at[wslot]),
                pltpu.make_async_copy(wd_ref.at[e], wd_buf_ref.at[wslot], wsem.at[wslot]))

    def tile_wait(src, dst, sem):
        pltpu.make_async_copy(src.at[pl.ds(0, MOE_TILE), :], dst.at[pl.ds(0, MOE_TILE), :], sem).wait()

    @pl.when(r == 0)
    def _():
        cur_ref[0] = -1
        cur_ref[1] = -1
        for cp in weight_copies(te_ref[0], 0):
            cp.start()
        gather(0, 0)
        zero_ref[...] = jnp.zeros_like(zero_ref)
        ybuf_ref[...] = jnp.zeros_like(ybuf_ref)
        n_blocks = tail_ref.shape[0]

        def fill(make):
            def body(g, c):
                make(g)
                return c
            return body

        for blk in range(n_blocks):
            lax.fori_loop(tail_ref[blk], GROUPS_PER_BLOCK, fill(
                lambda g, blk=blk: _group_copy(zero_ref, 0, ys_ref, blk * GROUPS_PER_BLOCK + g, zsem).start()), 0)
        for blk in range(n_blocks):
            lax.fori_loop(tail_ref[blk], GROUPS_PER_BLOCK, fill(
                lambda g, blk=blk: _group_copy(zero_ref, 0, ys_ref, blk * GROUPS_PER_BLOCK + g, zsem).wait()), 0)

    @pl.when(r < nt)
    def _():
        e = te_ref[r]
        slot = r % 2

        gather(jnp.minimum(r + 1, nt - 1), 1 - slot)

        @pl.when(cur_ref[0] != e)
        def _():
            wslot = (cur_ref[1] + 1) % 2
            for cp in weight_copies(e, wslot):
                cp.wait()
            cur_ref[0] = e
            cur_ref[1] = cur_ref[1] + 1

            @pl.when(nxt_ref[r] >= 0)
            def _():
                for cp in weight_copies(nxt_ref[r], 1 - wslot):
                    cp.start()

        wslot = cur_ref[1] % 2
        tile_wait(xs_ref, xbuf_ref, gsem.at[slot])

        @pl.when(r >= 2)
        def _():
            tile_wait(ybuf_ref, ys_ref, ssem.at[slot])

        d = ybuf_ref.shape[1]
        f = wd_buf_ref.shape[1]
        def chain(row0, n_rows):
            rows = pl.ds(pl.multiple_of(slot * MOE_TILE + row0, HALF_ROWS), n_rows)
            x = xbuf_ref[rows, 0:d]
            gate_parts = xbuf_ref[rows, d:d + LANES].astype(F32)
            gate = gate_parts[:, 0:1] + gate_parts[:, 1:2]
            gu = _dot(x, wgu_buf_ref[wslot])
            hid = (jax.nn.silu(gu[:, 0:f]) * gu[:, f:2 * f]).astype(BF16)
            ybuf_ref[rows, :] = (_dot(hid, wd_buf_ref[wslot]) * gate).astype(ybuf_ref.dtype)

        half_groups = HALF_ROWS // GROUP
        n_halves = (nreal_ref[r] + half_groups - 1) // half_groups
        for count in range(1, MOE_TILE // HALF_ROWS + 1):
            @pl.when(n_halves == count)
            def _(count=count):
                for part in range(count // 2):
                    chain(part * MXU_ROWS, MXU_ROWS)
                if count % 2:
                    chain((count // 2) * MXU_ROWS, HALF_ROWS)

        for i in range(TILE_GROUPS):
            _group_copy(ybuf_ref, slot * TILE_GROUPS + i, ys_ref, gdst_ref[r * TILE_GROUPS + i],
                        ssem.at[slot]).start(priority=1)

        @pl.when(r == nt - 1)
        def _():
            tile_wait(xs_ref, xbuf_ref, gsem.at[1 - slot])
            tile_wait(ybuf_ref, ys_ref, ssem.at[slot])

            @pl.when(r >= 1)
            def _():
                tile_wait(ybuf_ref, ys_ref, ssem.at[1 - slot])


def _experts(plan, xs, w_gate, w_up, w_down):
    d = w_gate.shape[1]
    f = w_gate.shape[2]
    tile_e, next_e, n_real, n_tiles, gsrc, gdst, tail = plan
    r_max = tile_e.shape[0]
    n_blocks = tail.shape[0]
    any_spec = pl.BlockSpec(memory_space=pl.ANY)
    grid_spec = pltpu.PrefetchScalarGridSpec(
        num_scalar_prefetch=7,
        grid=(r_max,),
        in_specs=[any_spec, any_spec, any_spec, any_spec],
        out_specs=any_spec,
        scratch_shapes=[pltpu.VMEM((2 * MOE_TILE, d + LANES), BF16),
                        pltpu.VMEM((2 * MOE_TILE, d), BF16),
                        pltpu.VMEM((GROUP, d), BF16),
                        pltpu.VMEM((2, d, 2 * f), BF16),
                        pltpu.VMEM((2, f, d), BF16),
                        pltpu.SMEM((2,), jnp.int32),
                        pltpu.SemaphoreType.DMA((2,)),
                        pltpu.SemaphoreType.DMA((2,)),
                        pltpu.SemaphoreType.DMA(()),
                        pltpu.SemaphoreType.DMA((2,))],
    )
    return pl.pallas_call(
        _expert_kernel,
        grid_spec=grid_spec,
        out_shape=jax.ShapeDtypeStruct((n_blocks * LOCAL_ROWS, d), BF16),
        compiler_params=_params(1),
        name="experts",
    )(tile_e, next_e, n_real, n_tiles, gsrc, gdst, tail, xs, w_gate, w_up, w_down)


def _combine_kernel(ys_ref, h_ref, cm_ref, g_ref, o_ref):
    tm = h_ref.shape[0]
    cm = cm_ref[...]
    q = lax.broadcasted_iota(jnp.int32, (tm, ys_ref.shape[0]), 1)
    sel = jnp.where((q == cm[:, 0:1].astype(jnp.int32)) | (q == cm[:, 1:2].astype(jnp.int32)), 1.0, 0.0)
    h2 = h_ref[...] + _dot(sel.astype(BF16), ys_ref[...])
    o_ref[...] = _rms_scale(h2, g_ref[...])


def _combine(ys, h1, cmeta, g):
    t, d = h1.shape
    tm = ROUTE_BLOCK
    return pl.pallas_call(
        _combine_kernel,
        grid=(t // tm,),
        in_specs=[pl.BlockSpec((LOCAL_ROWS, d), lambda i: (i, 0)),
                  pl.BlockSpec((tm, d), lambda i: (i, 0)),
                  pl.BlockSpec((tm, LANES), lambda i: (i, 0)),
                  pl.BlockSpec((1, d), lambda i: (0, 0))],
        out_specs=pl.BlockSpec((tm, d), lambda i: (i, 0)),
        out_shape=jax.ShapeDtypeStruct((t, d), F32),
        compiler_params=_params(1),
        name="combine",
    )(ys, h1, cmeta, g.reshape(1, d))


def _expert_plan(counts, n_blocks):
    cnt = counts[:, 0].astype(jnp.int32).reshape(n_blocks, N_EXPERTS)
    groups = (cnt + GROUP - 1) // GROUP
    first = jnp.cumsum(groups, axis=1) - groups
    upto = jnp.cumsum(groups, axis=0)
    per_expert = upto[-1]
    tiles_e = (per_expert + TILE_GROUPS - 1) // TILE_GROUPS
    tile_end = jnp.cumsum(tiles_e)
    n_tiles = tile_end[-1]
    max_groups = 2 * ROUTE_BLOCK * n_blocks // GROUP + n_blocks * N_EXPERTS
    r_max = max_groups // TILE_GROUPS + N_EXPERTS
    tile_ids = jnp.arange(r_max, dtype=jnp.int32)
    tile = jnp.minimum(tile_ids, n_tiles - 1)
    tile_e = jnp.sum((tile_end[None, :] <= tile[:, None]).astype(jnp.int32), axis=1)
    later = (tile_e[None, :] > tile_e[:, None]) & (tile_ids[None, :] < n_tiles)
    next_e = jnp.min(jnp.where(later, tile_e[None, :], N_EXPERTS), axis=1)
    next_e = jnp.where(next_e == N_EXPERTS, -1, next_e)

    slot = jnp.arange(r_max * TILE_GROUPS, dtype=jnp.int32)
    s_tile = slot // TILE_GROUPS
    oh_e = jnp.repeat(tile_e, TILE_GROUPS)[:, None] == jnp.arange(N_EXPERTS, dtype=jnp.int32)[None, :]

    def by_expert(table):
        return jnp.sum(jnp.where(oh_e[:, None, :], table[None], 0), axis=-1)

    k = slot - by_expert(((tile_end - tiles_e) * TILE_GROUPS)[None, :])[:, 0]
    real = (k < by_expert(per_expert[None, :])[:, 0]) & (s_tile < n_tiles)
    upto_e = by_expert(upto)
    blk = jnp.minimum(jnp.sum((upto_e <= k[:, None]).astype(jnp.int32), axis=1), n_blocks - 1)
    oh_b = blk[:, None] == jnp.arange(n_blocks, dtype=jnp.int32)[None, :]

    def by_block(table_se):
        return jnp.sum(jnp.where(oh_b, table_se, 0), axis=1)

    before = by_block(upto_e - by_expert(groups))
    src = blk * GROUPS_PER_BLOCK + by_block(by_expert(first)) + (k - before)
    zero_group = GROUPS_PER_BLOCK - 1
    spare = n_blocks * GROUPS_PER_BLOCK + (s_tile % 2) * TILE_GROUPS + slot % TILE_GROUPS
    gsrc = jnp.where(real, src, zero_group)
    gdst = jnp.where(real, src, spare)
    n_spare_blocks = -(-2 * TILE_GROUPS // GROUPS_PER_BLOCK)
    tail = jnp.concatenate([jnp.sum(groups, axis=1), jnp.zeros((n_spare_blocks,), jnp.int32)])
    n_real = jnp.sum(real.reshape(r_max, TILE_GROUPS).astype(jnp.int32), axis=1)
    return tile_e, next_e, n_real, n_tiles.reshape(1), gsrc, gdst, tail


def _layer(h, norm_mix_g, w_in, conv_w, sgu_ln_g, sgu_ln_b, sgu_w_s, sgu_b_s, w_up_conv,
           w_up_sgu, w_out, norm_ffn_g, w_rg, b_rg, w_re, b_re, w_eg, w_eu, w_ed, out_g):
    t, d = h.shape
    conv_width = conv_w.shape[1]
    sgu_width = sgu_ln_g.shape[0]
    xn, yb = _sgu_branch(h, norm_mix_g, w_in, 3 * conv_width, sgu_ln_g, sgu_ln_b, sgu_w_s, sgu_b_s)
    gate_col0 = 3 * conv_width + 2 * sgu_width
    ya, (wg_bf, wua_bf, wub_bf, wout_bf) = _conv_branch(
        xn, w_in, conv_w, conv_width,
        to_cast=[(w_in, gate_col0, w_in.shape[1] - gate_col0, math.gcd(gate_col0, w_in.shape[1])),
                 (w_up_conv, 0, d, d), (w_up_sgu, 0, d, d), (w_out, 0, d, d)])
    m, eg_bf, eu_bf, ed_bf = _upgate(xn, ya, yb, wg_bf, wua_bf, wub_bf, w_eg, w_eu, w_ed)
    h1 = _outproj(m, wout_bf, h)
    xs, meta, counts = _route_sort(h1, norm_ffn_g, w_rg, b_rg, w_re, b_re)
    ys = _experts(_expert_plan(counts, t // ROUTE_BLOCK), xs, eg_bf, eu_bf, ed_bf)
    cmeta = jnp.pad(meta[0:2].T, ((0, 0), (0, LANES - 2)))
    return _combine(ys, h1, cmeta, out_g)


def kernel(x, norm_mix_g, w_in, conv_w, sgu_ln_g, sgu_ln_b, sgu_w_s, sgu_b_s, w_up_conv, w_up_sgu, w_out, norm_ffn_g, w_router_group, b_router_group, w_router_expert, b_router_expert, w_exp_gate, w_exp_up, w_exp_down, norm_final_g):
    bsz, s, d = x.shape
    depth = w_in.shape[0]
    assert bsz == 1 and depth == 1, "causal conv carry and the fused final norm assume one sequence, one layer"
    assert s % ROUTE_BLOCK == 0
    out = _layer(x.reshape(s, d), norm_mix_g[0], w_in[0], conv_w[0], sgu_ln_g[0], sgu_ln_b[0],
                 sgu_w_s[0], sgu_b_s[0], w_up_conv[0], w_up_sgu[0], w_out[0], norm_ffn_g[0],
                 w_router_group[0], b_router_group[0], w_router_expert[0], b_router_expert[0],
                 w_exp_gate[0], w_exp_up[0], w_exp_down[0], norm_final_g)
    return out.reshape(bsz, s, d)
```

```python
import functools
import math

import jax
import jax.numpy as jnp
from jax import lax
from jax.experimental import pallas as pl
from jax.experimental.pallas import tpu as pltpu

F32 = jnp.float32
BF16 = jnp.bfloat16

EPS = 1e-6
CHUNK = 64
CONV_K = 3
SGU_HEADS = 8
SGU_BLOCK = 128
N_GROUPS = 4
EXPERTS_PER_GROUP = 4
N_EXPERTS = N_GROUPS * EXPERTS_PER_GROUP
ROUTER_ROWS = 32
LANES = 128

VMEM_LIMIT_BYTES = 56 * 1024 * 1024

ROUTE_BLOCK = 512
GROUP = 16
MXU_ROWS = 256
HALF_ROWS = MXU_ROWS // 2
LOCAL_ROWS = -(-(2 * ROUTE_BLOCK + N_EXPERTS * (GROUP - 1)) // MXU_ROWS) * MXU_ROWS
GROUPS_PER_BLOCK = LOCAL_ROWS // GROUP
assert LOCAL_ROWS - (2 * ROUTE_BLOCK + N_EXPERTS * (GROUP - 1)) >= GROUP
MOE_TILE = 512
TILE_GROUPS = MOE_TILE // GROUP


def _params(n_axes):
    return pltpu.CompilerParams(
        dimension_semantics=("arbitrary",) * n_axes,
        vmem_limit_bytes=VMEM_LIMIT_BYTES)


def _dot(a, b):
    return jnp.dot(a, b, preferred_element_type=F32)


def _rms_scale(x, g):
    ms = jnp.mean(x * x, axis=-1, keepdims=True)
    return x * lax.rsqrt(ms + EPS) * g


def _conv_kernel(cast_chunks, xn_ref, wb_ref, wc_ref, wh_ref, cw_ref, *rest):
    n_in = sum(cast_chunks)
    cast_in, o_ref = rest[:n_in], rest[n_in]
    cast_out = rest[n_in + 1:n_in + 1 + len(cast_chunks)]
    wbf_ref, carry_ref = rest[n_in + 1 + len(cast_chunks):]
    i = pl.program_id(1)
    tn = wb_ref.shape[1]
    tm = xn_ref.shape[0]
    src = iter(cast_in)
    for dst, n_chunks in zip(cast_out, cast_chunks):
        wc = dst.shape[1] // n_chunks
        for k in range(n_chunks):
            dst[:, k * wc:(k + 1) * wc] = next(src)[...].astype(BF16)

    @pl.when(i == 0)
    def _():
        wbf_ref[:, 0:tn] = wb_ref[...].astype(BF16)
        wbf_ref[:, tn:2 * tn] = wc_ref[...].astype(BF16)
        wbf_ref[:, 2 * tn:3 * tn] = wh_ref[...].astype(BF16)
        carry_ref[...] = jnp.zeros_like(carry_ref)

    proj = _dot(xn_ref[...], wbf_ref[...])
    b = proj[:, 0:tn]
    p = proj[:, tn:2 * tn] * proj[:, 2 * tn:3 * tn]
    prev = carry_ref[...]
    carry_ref[...] = p[tm - 8:tm, :]
    row = lax.broadcasted_iota(jnp.int32, p.shape, 0)
    p1 = jnp.where(row == 0, prev[7:8, :], pltpu.roll(p, 1, axis=0))
    p2 = jnp.where(row == 0, prev[6:7, :],
                   jnp.where(row == 1, prev[7:8, :], pltpu.roll(p, 2, axis=0)))
    cw = cw_ref[...]
    y = b * (cw[0:1, :] * p2 + cw[1:2, :] * p1 + cw[2:3, :] * p)
    o_ref[...] = y.astype(o_ref.dtype)


def _conv_branch(xn, w_in, conv_w, width, to_cast, tm=1024, tn=256):
    t, d = xn.shape
    nj = width // tn
    ni = t // tm
    steps = nj * ni
    cast_args, cast_in_specs, cast_out_specs, cast_shapes, cast_chunks = [], [], [], [], []
    for arr, col0, n_cols, chunk in to_cast:
        rows = arr.shape[0] // steps
        assert rows * steps == arr.shape[0] and rows % 16 == 0
        assert col0 % chunk == 0 and n_cols % chunk == 0
        for k in range(n_cols // chunk):
            cast_args.append(arr)
            cast_in_specs.append(
                pl.BlockSpec((rows, chunk), lambda j, i, c=col0 // chunk + k: (j * ni + i, c)))
        cast_chunks.append(n_cols // chunk)
        cast_out_specs.append(pl.BlockSpec((rows, n_cols), lambda j, i: (j * ni + i, 0)))
        cast_shapes.append(jax.ShapeDtypeStruct((arr.shape[0], n_cols), BF16))
    outs = pl.pallas_call(
        functools.partial(_conv_kernel, tuple(cast_chunks)),
        grid=(nj, ni),
        in_specs=[pl.BlockSpec((tm, d), lambda j, i: (i, 0)),
                  pl.BlockSpec((d, tn), lambda j, i: (0, j)),
                  pl.BlockSpec((d, tn), lambda j, i: (0, nj + j)),
                  pl.BlockSpec((d, tn), lambda j, i: (0, 2 * nj + j)),
                  pl.BlockSpec((CONV_K, tn), lambda j, i: (0, j))] + cast_in_specs,
        out_specs=[pl.BlockSpec((tm, tn), lambda j, i: (i, j))] + cast_out_specs,
        out_shape=[jax.ShapeDtypeStruct((t, width), BF16)] + cast_shapes,
        scratch_shapes=[pltpu.VMEM((d, 3 * tn), BF16),
                        pltpu.VMEM((8, tn), F32)],
        compiler_params=_params(2),
        name="conv_branch",
    )(xn, w_in, w_in, w_in, conv_w, *cast_args)
    return outs[0], outs[1:]


def _sgu_kernel(x_ref, g_ref, wu_ref, wv_ref, lng_ref, lnb_ref, ws_ref, bsx_ref, xn_ref, o_ref, wbf_ref):
    tm = x_ref.shape[0]
    w = o_ref.shape[1]
    hd = w // SGU_HEADS

    @pl.when(pl.program_id(0) == 0)
    def _():
        wbf_ref[:, 0:w] = wu_ref[...].astype(BF16)
        wbf_ref[:, w:2 * w] = wv_ref[...].astype(BF16)

    xn = _rms_scale(x_ref[...], g_ref[...]).astype(BF16)
    xn_ref[...] = xn
    gz = jax.nn.gelu(_dot(xn, wbf_ref[...]))
    v = gz[:, w:2 * w]
    mu = jnp.mean(v, axis=-1, keepdims=True)
    vc = v - mu
    var = jnp.mean(vc * vc, axis=-1, keepdims=True)
    vn = (vc * lax.rsqrt(var + EPS) * lng_ref[...] + lnb_ref[...]).astype(BF16)
    ii = lax.broadcasted_iota(jnp.int32, (SGU_BLOCK, SGU_BLOCK), 0)
    jj = lax.broadcasted_iota(jnp.int32, (SGU_BLOCK, SGU_BLOCK), 1)
    mask = (jj // CHUNK) <= (ii // CHUNK)
    for h in range(SGU_HEADS):
        wm = jnp.where(mask, ws_ref[h], 0.0).astype(BF16)
        cs = slice(h * hd, (h + 1) * hd)
        for n in range(tm // SGU_BLOCK):
            rs = slice(n * SGU_BLOCK, (n + 1) * SGU_BLOCK)
            vm = _dot(wm, vn[rs, cs]) + bsx_ref[:, cs]
            o_ref[rs, cs] = (gz[rs, cs] * vm).astype(o_ref.dtype)


def _sgu_branch(x, g, w_in, col0, ln_g, ln_b, w_s, b_s, tm=512):
    t, d = x.shape
    w = ln_g.shape[0]
    hd = w // SGU_HEADS
    assert col0 % w == 0
    c0 = col0 // w
    bsx = jnp.repeat(b_s.T, hd, axis=1)
    once = pl.Buffered(1)
    return pl.pallas_call(
        _sgu_kernel,
        grid=(t // tm,),
        in_specs=[pl.BlockSpec((tm, d), lambda i: (i, 0)),
                  pl.BlockSpec((1, d), lambda i: (0, 0)),
                  pl.BlockSpec((d, w), lambda i: (0, c0), pipeline_mode=once),
                  pl.BlockSpec((d, w), lambda i: (0, c0 + 1), pipeline_mode=once),
                  pl.BlockSpec((1, w), lambda i: (0, 0)),
                  pl.BlockSpec((1, w), lambda i: (0, 0)),
                  pl.BlockSpec((SGU_HEADS, SGU_BLOCK, SGU_BLOCK), lambda i: (0, 0, 0)),
                  pl.BlockSpec((SGU_BLOCK, w), lambda i: (0, 0))],
        out_specs=[pl.BlockSpec((tm, d), lambda i: (i, 0)),
                   pl.BlockSpec((tm, w), lambda i: (i, 0))],
        out_shape=[jax.ShapeDtypeStruct((t, d), BF16),
                   jax.ShapeDtypeStruct((t, w), BF16)],
        scratch_shapes=[pltpu.VMEM((d, 2 * w), BF16)],
        compiler_params=_params(1),
        name="sgu_branch",
    )(x, g.reshape(1, d), w_in, w_in, ln_g.reshape(1, w), ln_b.reshape(1, w), w_s, bsx)


def _upgate_kernel(xn_ref, ya_ref, yb_ref, wgc_ref, wgs_ref, wua_ref, wub_ref, eg_ref, eu_ref, ed_ref,
                   o_ref, eg_bf_ref, eu_bf_ref, ed_bf_ref):
    eg_bf_ref[...] = eg_ref[...].astype(BF16)
    eu_bf_ref[...] = eu_ref[...].astype(BF16)
    ed_bf_ref[...] = ed_ref[...].astype(BF16)

    xn = xn_ref[...]
    m = (jax.nn.sigmoid(_dot(xn, wgc_ref[...])) * _dot(ya_ref[...], wua_ref[...])
         + jax.nn.sigmoid(_dot(xn, wgs_ref[...])) * _dot(yb_ref[...], wub_ref[...]))
    o_ref[...] = m.astype(o_ref.dtype)


def _upgate(xn, ya, yb, w_gates, w_up_a, w_up_b, w_eg, w_eu, w_ed, tm=1024, tn=512):
    t, d = xn.shape
    wa = ya.shape[1]
    wb = yb.shape[1]
    dout = w_up_a.shape[1]
    c0 = 0
    nj = dout // tn
    ni = t // tm
    n_e, d_e, f_e = w_eg.shape
    up_rows = n_e * d_e // (nj * ni)
    down_rows = n_e * f_e // (nj * ni)
    assert up_rows * nj * ni == n_e * d_e and up_rows % 16 == 0
    assert down_rows * nj * ni == n_e * f_e and down_rows % 16 == 0
    up_spec = pl.BlockSpec((up_rows, f_e), lambda j, i: (j * ni + i, 0))
    down_spec = pl.BlockSpec((down_rows, d_e), lambda j, i: (j * ni + i, 0))
    m, eg_bf, eu_bf, ed_bf = pl.pallas_call(
        _upgate_kernel,
        grid=(nj, ni),
        in_specs=[pl.BlockSpec((tm, d), lambda j, i: (i, 0)),
                  pl.BlockSpec((tm, wa), lambda j, i: (i, 0)),
                  pl.BlockSpec((tm, wb), lambda j, i: (i, 0)),
                  pl.BlockSpec((d, tn), lambda j, i: (0, c0 + j)),
                  pl.BlockSpec((d, tn), lambda j, i: (0, c0 + nj + j)),
                  pl.BlockSpec((wa, tn), lambda j, i: (0, j)),
                  pl.BlockSpec((wb, tn), lambda j, i: (0, j)),
                  up_spec, up_spec, down_spec],
        out_specs=[pl.BlockSpec((tm, tn), lambda j, i: (i, j)), up_spec, up_spec, down_spec],
        out_shape=[jax.ShapeDtypeStruct((t, dout), BF16),
                   jax.ShapeDtypeStruct((n_e * d_e, f_e), BF16),
                   jax.ShapeDtypeStruct((n_e * d_e, f_e), BF16),
                   jax.ShapeDtypeStruct((n_e * f_e, d_e), BF16)],
        compiler_params=_params(2),
        name="upgate",
    )(xn, ya, yb, w_gates, w_gates, w_up_a, w_up_b,
      w_eg.reshape(n_e * d_e, f_e), w_eu.reshape(n_e * d_e, f_e), w_ed.reshape(n_e * f_e, d_e))
    return (m, eg_bf.reshape(n_e, d_e, f_e), eu_bf.reshape(n_e, d_e, f_e),
            ed_bf.reshape(n_e, f_e, d_e))


def _outproj_kernel(m_ref, w_ref, x_ref, o_ref):
    o_ref[...] = x_ref[...] + _dot(m_ref[...], w_ref[...])


def _outproj(m, w_out_bf, x, tm=512):
    t, d = m.shape
    dout = w_out_bf.shape[1]
    return pl.pallas_call(
        _outproj_kernel,
        grid=(t // tm,),
        in_specs=[pl.BlockSpec((tm, d), lambda i: (i, 0)),
                  pl.BlockSpec((d, dout), lambda i: (0, 0), pipeline_mode=pl.Buffered(1)),
                  pl.BlockSpec((tm, dout), lambda i: (i, 0))],
        out_specs=pl.BlockSpec((tm, dout), lambda i: (i, 0)),
        out_shape=jax.ShapeDtypeStruct((t, dout), F32),
        compiler_params=_params(1),
        name="outproj",
    )(m, w_out_bf, x)


def _argmax_rows(rows):
    best = rows[0]
    idx = jnp.zeros(rows[0].shape, jnp.int32)
    for k in range(1, len(rows)):
        better = rows[k] > best
        best = jnp.where(better, rows[k], best)
        idx = jnp.where(better, k, idx)
    return best, idx


def _softmax_rows(rows):
    mx = functools.reduce(jnp.maximum, rows)
    ex = [jnp.exp(r - mx) for r in rows]
    den = functools.reduce(lambda a, b: a + b, ex)
    return [e / den for e in ex]


def _route_sort_kernel(h_ref, g_ref, wr_ref, br_ref, xs_ref, meta_ref, cnt_ref, before_ref):
    tm = h_ref.shape[0]
    xn = _rms_scale(h_ref[...], g_ref[...])
    xn_hi = xn.astype(BF16)
    xn_lo = (xn - xn_hi.astype(F32)).astype(BF16)
    wr = wr_ref[...]
    wr_hi = wr.astype(BF16)
    wr_lo = (wr - wr_hi.astype(F32)).astype(BF16)
    nt_dims = (((1,), (1,)), ((), ()))
    lt = (lax.dot_general(wr_hi, xn_hi, nt_dims, preferred_element_type=F32)
          + lax.dot_general(wr_hi, xn_lo, nt_dims, preferred_element_type=F32)
          + lax.dot_general(wr_lo, xn_hi, nt_dims, preferred_element_type=F32)) + br_ref[...]
    pgs = _softmax_rows([lt[k:k + 1, :] for k in range(N_GROUPS)])
    pg, gi = _argmax_rows(pgs)
    sel = []
    for k in range(EXPERTS_PER_GROUP):
        r = jnp.zeros_like(pg)
        for g in range(N_GROUPS):
            row = N_GROUPS + g * EXPERTS_PER_GROUP + k
            r = jnp.where(gi == g, lt[row:row + 1, :], r)
        sel.append(r)
    pes = _softmax_rows(sel)
    p1, e1 = _argmax_rows(pes)
    rest = [jnp.where(e1 == k, -1.0, pes[k]) for k in range(EXPERTS_PER_GROUP)]
    p2, e2 = _argmax_rows(rest)
    den = p1 + p2
    w1 = pg * (p1 / den)
    w2 = pg * (p2 / den)
    lo = jnp.minimum(e1, e2)
    hi = jnp.maximum(e1, e2)
    w_lo = jnp.where(e1 < e2, w1, w2)
    w_hi = jnp.where(e1 < e2, w2, w1)
    ea = gi * EXPERTS_PER_GROUP + lo
    eb = gi * EXPERTS_PER_GROUP + hi

    erow = lax.broadcasted_iota(jnp.int32, (N_EXPERTS, tm), 0)
    oh_a = (erow == ea).astype(F32)
    oh_b = (erow == eb).astype(F32)

    @pl.when(pl.program_id(0) == 0)
    def _():
        a = lax.broadcasted_iota(jnp.int32, (tm, tm), 0)
        b = lax.broadcasted_iota(jnp.int32, (tm, tm), 1)
        before_ref[...] = (a < b).astype(BF16)

    cum = _dot((oh_a + oh_b).astype(BF16), before_ref[...])
    cnt = jnp.sum(oh_a + oh_b, axis=1, keepdims=True)
    padded = jnp.floor((cnt + (GROUP - 1)) * (1.0 / GROUP)) * GROUP
    pos_a = jnp.sum(oh_a * cum + jnp.where(erow < ea, padded, 0.0), axis=0, keepdims=True)
    pos_b = jnp.sum(oh_b * cum + jnp.where(erow < eb, padded, 0.0), axis=0, keepdims=True)

    d = h_ref.shape[1]

    def gate_rows(w):
        hi = w.astype(BF16).astype(F32)
        lo = w - hi
        k = lax.broadcasted_iota(jnp.int32, (LANES, tm), 0)
        return jnp.where(k == 0, hi, jnp.where(k == 1, lo, 0.0)).astype(BF16)

    gate_a = gate_rows(w_lo)
    gate_b = gate_rows(w_hi)

    def sort_rows(r0, n):
        q = r0 + lax.broadcasted_iota(jnp.int32, (n, tm), 0)
        perm_a = jnp.where(q == pos_a.astype(jnp.int32), 1.0, 0.0).astype(BF16)
        perm_b = jnp.where(q == pos_b.astype(jnp.int32), 1.0, 0.0).astype(BF16)
        xs_ref[r0:r0 + n, 0:d] = _dot(perm_a + perm_b, xn_hi).astype(xs_ref.dtype)
        gates = (lax.dot_general(perm_a, gate_a, nt_dims, preferred_element_type=F32)
                 + lax.dot_general(perm_b, gate_b, nt_dims, preferred_element_type=F32))
        xs_ref[r0:r0 + n, d:d + LANES] = gates.astype(xs_ref.dtype)

    n_rows = xs_ref.shape[0]
    head = n_rows - HALF_ROWS
    used = jnp.sum(padded)
    sort_rows(0, head)

    @pl.when(used > head)
    def _():
        sort_rows(head, HALF_ROWS)

    @pl.when(used <= head)
    def _():
        xs_ref[head:n_rows, :] = jnp.zeros((HALF_ROWS, xs_ref.shape[1]), xs_ref.dtype)

    cnt_ref[...] = jnp.broadcast_to(cnt, cnt_ref.shape)
    meta_ref[0:1, :] = pos_a
    meta_ref[1:2, :] = pos_b
    meta_ref[2:8, :] = jnp.zeros((6, tm), F32)


def _route_sort(h1, g, w_rg, b_rg, w_re, b_re):
    t, d = h1.shape
    tm = ROUTE_BLOCK
    nb = t // tm
    n_log = w_rg.shape[1] + w_re.shape[1]
    wr = jnp.concatenate([w_rg, w_re], axis=1).T
    wr = jnp.pad(wr, ((0, ROUTER_ROWS - n_log), (0, 0)))
    br = jnp.pad(jnp.concatenate([b_rg, b_re]), (0, ROUTER_ROWS - n_log)).reshape(ROUTER_ROWS, 1)
    return pl.pallas_call(
        _route_sort_kernel,
        grid=(nb,),
        in_specs=[pl.BlockSpec((tm, d), lambda i: (i, 0)),
                  pl.BlockSpec((1, d), lambda i: (0, 0)),
                  pl.BlockSpec((ROUTER_ROWS, d), lambda i: (0, 0)),
                  pl.BlockSpec((ROUTER_ROWS, 1), lambda i: (0, 0))],
        out_specs=[pl.BlockSpec((LOCAL_ROWS, d + LANES), lambda i: (i, 0)),
                   pl.BlockSpec((8, tm), lambda i: (0, i)),
                   pl.BlockSpec((N_EXPERTS, LANES), lambda i: (i, 0))],
        out_shape=[jax.ShapeDtypeStruct((nb * LOCAL_ROWS, d + LANES), BF16),
                   jax.ShapeDtypeStruct((8, t), F32),
                   jax.ShapeDtypeStruct((nb * N_EXPERTS, LANES), F32)],
        scratch_shapes=[pltpu.VMEM((tm, tm), BF16)],
        compiler_params=_params(1),
        name="route_sort",
    )(h1, g.reshape(1, d), wr, br)


def _group_copy(src_ref, s_group, dst_ref, d_group, sem):
    return pltpu.make_async_copy(src_ref.at[pl.ds(pl.multiple_of(s_group * GROUP, GROUP), GROUP), :],
                                 dst_ref.at[pl.ds(pl.multiple_of(d_group * GROUP, GROUP), GROUP), :], sem)


def _expert_kernel(te_ref, nxt_ref, nreal_ref, nt_ref, gsrc_ref, gdst_ref, tail_ref,
                   xs_ref, wg_ref, wu_ref, wd_ref, ys_ref,
                   xbuf_ref, ybuf_ref, zero_ref, wgu_buf_ref, wd_buf_ref,
                   cur_ref, gsem, ssem, zsem, wsem):
    r = pl.program_id(0)
    nt = nt_ref[0]

    def gather(q, slot):
        for i in range(TILE_GROUPS):
            _group_copy(xs_ref, gsrc_ref[q * TILE_GROUPS + i], xbuf_ref, slot * TILE_GROUPS + i,
                        gsem.at[slot]).start(priority=1)

    def weight_copies(e, wslot):
        f = wg_ref.shape[2]
        return (pltpu.make_async_copy(wg_ref.at[e], wgu_buf_ref.at[wslot, :, pl.ds(0, f)], wsem.at[wslot]),
                pltpu.make_async_copy(wu_ref.at[e], wgu_buf_ref.at[wslot, :, pl.ds(f, f)], wsem.at[wslot]),
                pltpu.make_async_copy(wd_ref.at[e], wd_buf_ref.at[wslot], wsem.at[wslot]))

    def tile_wait(src, dst, sem):
        pltpu.make_async_copy(src.at[pl.ds(0, MOE_TILE), :], dst.at[pl.ds(0, MOE_TILE), :], sem).wait()

    @pl.when(r == 0)
    def _():
        cur_ref[0] = -1
        cur_ref[1] = -1
        for cp in weight_copies(te_ref[0], 0):
            cp.start()
        gather(0, 0)
        zero_ref[...] = jnp.zeros_like(zero_ref)
        ybuf_ref[...] = jnp.zeros_like(ybuf_ref)
        n_blocks = tail_ref.shape[0]

        def fill(make):
            def body(g, c):
                make(g)
                return c
            return body

        for blk in range(n_blocks):
            lax.fori_loop(tail_ref[blk], GROUPS_PER_BLOCK, fill(
                lambda g, blk=blk: _group_copy(zero_ref, 0, ys_ref, blk * GROUPS_PER_BLOCK + g, zsem).start()), 0)
        for blk in range(n_blocks):
            lax.fori_loop(tail_ref[blk], GROUPS_PER_BLOCK, fill(
                lambda g, blk=blk: _group_copy(zero_ref, 0, ys_ref, blk * GROUPS_PER_BLOCK + g, zsem).wait()), 0)

    @pl.when(r < nt)
    def _():
        e = te_ref[r]
        slot = r % 2

        gather(jnp.minimum(r + 1, nt - 1), 1 - slot)

        @pl.when(cur_ref[0] != e)
        def _():
            wslot = (cur_ref[1] + 1) % 2
            for cp in weight_copies(e, wslot):
                cp.wait()
            cur_ref[0] = e
            cur_ref[1] = cur_ref[1] + 1

            @pl.when(nxt_ref[r] >= 0)
            def _():
                for cp in weight_copies(nxt_ref[r], 1 - wslot):
                    cp.start()

        wslot = cur_ref[1] % 2
        tile_wait(xs_ref, xbuf_ref, gsem.at[slot])

        @pl.when(r >= 2)
        def _():
            tile_wait(ybuf_ref, ys_ref, ssem.at[slot])

        d = ybuf_ref.shape[1]
        f = wd_buf_ref.shape[1]
        def chain(row0, n_rows):
            rows = pl.ds(pl.multiple_of(slot * MOE_TILE + row0, HALF_ROWS), n_rows)
            x = xbuf_ref[rows, 0:d]
            gate_parts = xbuf_ref[rows, d:d + LANES].astype(F32)
            gate = gate_parts[:, 0:1] + gate_parts[:, 1:2]
            gu = _dot(x, wgu_buf_ref[wslot])
            hid = (jax.nn.silu(gu[:, 0:f]) * gu[:, f:2 * f]).astype(BF16)
            ybuf_ref[rows, :] = (_dot(hid, wd_buf_ref[wslot]) * gate).astype(ybuf_ref.dtype)

        half_groups = HALF_ROWS // GROUP
        n_halves = (nreal_ref[r] + half_groups - 1) // half_groups
        for count in range(1, MOE_TILE // HALF_ROWS + 1):
            @pl.when(n_halves == count)
            def _(count=count):
                for part in range(count // 2):
                    chain(part * MXU_ROWS, MXU_ROWS)
                if count % 2:
                    chain((count // 2) * MXU_ROWS, HALF_ROWS)

        for i in range(TILE_GROUPS):
            _group_copy(ybuf_ref, slot * TILE_GROUPS + i, ys_ref, gdst_ref[r * TILE_GROUPS + i],
                        ssem.at[slot]).start(priority=1)

        @pl.when(r == nt - 1)
        def _():
            tile_wait(xs_ref, xbuf_ref, gsem.at[1 - slot])
            tile_wait(ybuf_ref, ys_ref, ssem.at[slot])

            @pl.when(r >= 1)
            def _():
                tile_wait(ybuf_ref, ys_ref, ssem.at[1 - slot])


def _experts(plan, xs, w_gate, w_up, w_down):
    d = w_gate.shape[1]
    f = w_gate.shape[2]
    tile_e, next_e, n_real, n_tiles, gsrc, gdst, tail = plan
    r_max = tile_e.shape[0]
    n_blocks = tail.shape[0]
    any_spec = pl.BlockSpec(memory_space=pl.ANY)
    grid_spec = pltpu.PrefetchScalarGridSpec(
        num_scalar_prefetch=7,
        grid=(r_max,),
        in_specs=[any_spec, any_spec, any_spec, any_spec],
        out_specs=any_spec,
        scratch_shapes=[pltpu.VMEM((2 * MOE_TILE, d + LANES), BF16),
                        pltpu.VMEM((2 * MOE_TILE, d), BF16),
                        pltpu.VMEM((GROUP, d), BF16),
                        pltpu.VMEM((2, d, 2 * f), BF16),
                        pltpu.VMEM((2, f, d), BF16),
                        pltpu.SMEM((2,), jnp.int32),
                        pltpu.SemaphoreType.DMA((2,)),
                        pltpu.SemaphoreType.DMA((2,)),
                        pltpu.SemaphoreType.DMA(()),
                        pltpu.SemaphoreType.DMA((2,))],
    )
    return pl.pallas_call(
        _expert_kernel,
        grid_spec=grid_spec,
        out_shape=jax.ShapeDtypeStruct((n_blocks * LOCAL_ROWS, d), BF16),
        compiler_params=_params(1),
        name="experts",
    )(tile_e, next_e, n_real, n_tiles, gsrc, gdst, tail, xs, w_gate, w_up, w_down)


def _combine_kernel(ys_ref, h_ref, cm_ref, g_ref, o_ref):
    tm = h_ref.shape[0]
    cm = cm_ref[...]
    q = lax.broadcasted_iota(jnp.int32, (tm, ys_ref.shape[0]), 1)
    sel = jnp.where((q == cm[:, 0:1].astype(jnp.int32)) | (q == cm[:, 1:2].astype(jnp.int32)), 1.0, 0.0)
    h2 = h_ref[...] + _dot(sel.astype(BF16), ys_ref[...])
    o_ref[...] = _rms_scale(h2, g_ref[...])


def _combine(ys, h1, cmeta, g):
    t, d = h1.shape
    tm = ROUTE_BLOCK
    return pl.pallas_call(
        _combine_kernel,
        grid=(t // tm,),
        in_specs=[pl.BlockSpec((LOCAL_ROWS, d), lambda i: (i, 0)),
                  pl.BlockSpec((tm, d), lambda i: (i, 0)),
                  pl.BlockSpec((tm, LANES), lambda i: (i, 0)),
                  pl.BlockSpec((1, d), lambda i: (0, 0))],
        out_specs=pl.BlockSpec((tm, d), lambda i: (i, 0)),
        out_shape=jax.ShapeDtypeStruct((t, d), F32),
        compiler_params=_params(1),
        name="combine",
    )(ys, h1, cmeta, g.reshape(1, d))


def _expert_plan(counts, n_blocks):
    cnt = counts[:, 0].astype(jnp.int32).reshape(n_blocks, N_EXPERTS)
    groups = (cnt + GROUP - 1) // GROUP
    first = jnp.cumsum(groups, axis=1) - groups
    upto = jnp.cumsum(groups, axis=0)
    per_expert = upto[-1]
    tiles_e = (per_expert + TILE_GROUPS - 1) // TILE_GROUPS
    tile_end = jnp.cumsum(tiles_e)
    n_tiles = tile_end[-1]
    max_groups = 2 * ROUTE_BLOCK * n_blocks // GROUP + n_blocks * N_EXPERTS
    r_max = max_groups // TILE_GROUPS + N_EXPERTS
    tile_ids = jnp.arange(r_max, dtype=jnp.int32)
    tile = jnp.minimum(tile_ids, n_tiles - 1)
    tile_e = jnp.sum((tile_end[None, :] <= tile[:, None]).astype(jnp.int32), axis=1)
    later = (tile_e[None, :] > tile_e[:, None]) & (tile_ids[None, :] < n_tiles)
    next_e = jnp.min(jnp.where(later, tile_e[None, :], N_EXPERTS), axis=1)
    next_e = jnp.where(next_e == N_EXPERTS, -1, next_e)

    slot = jnp.arange(r_max * TILE_GROUPS, dtype=jnp.int32)
    s_tile = slot // TILE_GROUPS
    oh_e = jnp.repeat(tile_e, TILE_GROUPS)[:, None] == jnp.arange(N_EXPERTS, dtype=jnp.int32)[None, :]

    def by_expert(table):
        return jnp.sum(jnp.where(oh_e[:, None, :], table[None], 0), axis=-1)

    k = slot - by_expert(((tile_end - tiles_e) * TILE_GROUPS)[None, :])[:, 0]
    real = (k < by_expert(per_expert[None, :])[:, 0]) & (s_tile < n_tiles)
    upto_e = by_expert(upto)
    blk = jnp.minimum(jnp.sum((upto_e <= k[:, None]).astype(jnp.int32), axis=1), n_blocks - 1)
    oh_b = blk[:, None] == jnp.arange(n_blocks, dtype=jnp.int32)[None, :]

    def by_block(table_se):
        return jnp.sum(jnp.where(oh_b, table_se, 0), axis=1)

    before = by_block(upto_e - by_expert(groups))
    src = blk * GROUPS_PER_BLOCK + by_block(by_expert(first)) + (k - before)
    zero_group = GROUPS_PER_BLOCK - 1
    spare = n_blocks * GROUPS_PER_BLOCK + (s_tile % 2) * TILE_GROUPS + slot % TILE_GROUPS
    gsrc = jnp.where(real, src, zero_group)
    gdst = jnp.where(real, src, spare)
    n_spare_blocks = -(-2 * TILE_GROUPS // GROUPS_PER_BLOCK)
    tail = jnp.concatenate([jnp.sum(groups, axis=1), jnp.zeros((n_spare_blocks,), jnp.int32)])
    n_real = jnp.sum(real.reshape(r_max, TILE_GROUPS).astype(jnp.int32), axis=1)
    return tile_e, next_e, n_real, n_tiles.reshape(1), gsrc, gdst, tail


def _layer(h, norm_mix_g, w_in, conv_w, sgu_ln_g, sgu_ln_b, sgu_w_s, sgu_b_s, w_up_conv,
           w_up_sgu, w_out, norm_ffn_g, w_rg, b_rg, w_re, b_re, w_eg, w_eu, w_ed, out_g):
    t, d = h.shape
    conv_width = conv_w.shape[1]
    sgu_width = sgu_ln_g.shape[0]
    xn, yb = _sgu_branch(h, norm_mix_g, w_in, 3 * conv_width, sgu_ln_g, sgu_ln_b, sgu_w_s, sgu_b_s)
    gate_col0 = 3 * conv_width + 2 * sgu_width
    ya, (wg_bf, wua_bf, wub_bf, wout_bf) = _conv_branch(
        xn, w_in, conv_w, conv_width,
        to_cast=[(w_in, gate_col0, w_in.shape[1] - gate_col0, math.gcd(gate_col0, w_in.shape[1])),
                 (w_up_conv, 0, d, d), (w_up_sgu, 0, d, d), (w_out, 0, d, d)])
    m, eg_bf, eu_bf, ed_bf = _upgate(xn, ya, yb, wg_bf, wua_bf, wub_bf, w_eg, w_eu, w_ed)
    h1 = _outproj(m, wout_bf, h)
    xs, meta, counts = _route_sort(h1, norm_ffn_g, w_rg, b_rg, w_re, b_re)
    ys = _experts(_expert_plan(counts, t // ROUTE_BLOCK), xs, eg_bf, eu_bf, ed_bf)
    cmeta = jnp.pad(meta[0:2].T, ((0, 0), (0, LANES - 2)))
    return _combine(ys, h1, cmeta, out_g)


def kernel(x, norm_mix_g, w_in, conv_w, sgu_ln_g, sgu_ln_b, sgu_w_s, sgu_b_s, w_up_conv, w_up_sgu, w_out, norm_ffn_g, w_router_group, b_router_group, w_router_expert, b_router_expert, w_exp_gate, w_exp_up, w_exp_down, norm_final_g):
    bsz, s, d = x.shape
    depth = w_in.shape[0]
    assert bsz == 1 and depth == 1, "causal conv carry and the fused final norm assume one sequence, one layer"
    assert s % ROUTE_BLOCK == 0
    out = _layer(x.reshape(s, d), norm_mix_g[0], w_in[0], conv_w[0], sgu_ln_g[0], sgu_ln_b[0],
                 sgu_w_s[0], sgu_b_s[0], w_up_conv[0], w_up_sgu[0], w_out[0], norm_ffn_g[0],
                 w_router_group[0], b_router_group[0], w_router_expert[0], b_router_expert[0],
                 w_exp_gate[0], w_exp_up[0], w_exp_down[0], norm_final_g)
    return out.reshape(bsz, s, d)
```

```python
import functools
import math

import jax
import jax.numpy as jnp
from jax import lax
from jax.experimental import pallas as pl
from jax.experimental.pallas import tpu as pltpu

F32 = jnp.float32
BF16 = jnp.bfloat16

EPS = 1e-6
CHUNK = 64
CONV_K = 3
SGU_HEADS = 8
SGU_BLOCK = 128
N_GROUPS = 4
EXPERTS_PER_GROUP = 4
N_EXPERTS = N_GROUPS * EXPERTS_PER_GROUP
ROUTER_ROWS = 32
LANES = 128

VMEM_LIMIT_BYTES = 56 * 1024 * 1024

ROUTE_BLOCK = 512
GROUP = 16
MXU_ROWS = 256
HALF_ROWS = MXU_ROWS // 2
LOCAL_ROWS = -(-(2 * ROUTE_BLOCK + N_EXPERTS * (GROUP - 1)) // MXU_ROWS) * MXU_ROWS
GROUPS_PER_BLOCK = LOCAL_ROWS // GROUP
assert LOCAL_ROWS - (2 * ROUTE_BLOCK + N_EXPERTS * (GROUP - 1)) >= GROUP
MOE_TILE = 512
TILE_GROUPS = MOE_TILE // GROUP


def _params(n_axes):
    return pltpu.CompilerParams(
        dimension_semantics=("arbitrary",) * n_axes,
        vmem_limit_bytes=VMEM_LIMIT_BYTES)


def _dot(a, b):
    return jnp.dot(a, b, preferred_element_type=F32)


def _rms_scale(x, g):
    ms = jnp.mean(x * x, axis=-1, keepdims=True)
    return x * lax.rsqrt(ms + EPS) * g


def _conv_kernel(cast_chunks, xn_ref, wb_ref, wc_ref, wh_ref, cw_ref, *rest):
    n_in = sum(cast_chunks)
    cast_in, o_ref = rest[:n_in], rest[n_in]
    cast_out = rest[n_in + 1:n_in + 1 + len(cast_chunks)]
    wbf_ref, carry_ref = rest[n_in + 1 + len(cast_chunks):]
    i = pl.program_id(1)
    tn = wb_ref.shape[1]
    tm = xn_ref.shape[0]
    src = iter(cast_in)
    for dst, n_chunks in zip(cast_out, cast_chunks):
        wc = dst.shape[1] // n_chunks
        for k in range(n_chunks):
            dst[:, k * wc:(k + 1) * wc] = next(src)[...].astype(BF16)

    @pl.when(i == 0)
    def _():
        wbf_ref[:, 0:tn] = wb_ref[...].astype(BF16)
        wbf_ref[:, tn:2 * tn] = wc_ref[...].astype(BF16)
        wbf_ref[:, 2 * tn:3 * tn] = wh_ref[...].astype(BF16)
        carry_ref[...] = jnp.zeros_like(carry_ref)

    proj = _dot(xn_ref[...], wbf_ref[...])
    b = proj[:, 0:tn]
    p = proj[:, tn:2 * tn] * proj[:, 2 * tn:3 * tn]
    prev = carry_ref[...]
    carry_ref[...] = p[tm - 8:tm, :]
    row = lax.broadcasted_iota(jnp.int32, p.shape, 0)
    p1 = jnp.where(row == 0, prev[7:8, :], pltpu.roll(p, 1, axis=0))
    p2 = jnp.where(row == 0, prev[6:7, :],
                   jnp.where(row == 1, prev[7:8, :], pltpu.roll(p, 2, axis=0)))
    cw = cw_ref[...]
    y = b * (cw[0:1, :] * p2 + cw[1:2, :] * p1 + cw[2:3, :] * p)
    o_ref[...] = y.astype(o_ref.dtype)


def _conv_branch(xn, w_in, conv_w, width, to_cast, tm=1024, tn=256):
    t, d = xn.shape
    nj = width // tn
    ni = t // tm
    steps = nj * ni
    cast_args, cast_in_specs, cast_out_specs, cast_shapes, cast_chunks = [], [], [], [], []
    for arr, col0, n_cols, chunk in to_cast:
        rows = arr.shape[0] // steps
        assert rows * steps == arr.shape[0] and rows % 16 == 0
        assert col0 % chunk == 0 and n_cols % chunk == 0
        for k in range(n_cols // chunk):
            cast_args.append(arr)
            cast_in_specs.append(
                pl.BlockSpec((rows, chunk), lambda j, i, c=col0 // chunk + k: (j * ni + i, c)))
        cast_chunks.append(n_cols // chunk)
        cast_out_specs.append(pl.BlockSpec((rows, n_cols), lambda j, i: (j * ni + i, 0)))
        cast_shapes.append(jax.ShapeDtypeStruct((arr.shape[0], n_cols), BF16))
    outs = pl.pallas_call(
        functools.partial(_conv_kernel, tuple(cast_chunks)),
        grid=(nj, ni),
        in_specs=[pl.BlockSpec((tm, d), lambda j, i: (i, 0)),
                  pl.BlockSpec((d, tn), lambda j, i: (0, j)),
                  pl.BlockSpec((d, tn), lambda j, i: (0, nj + j)),
                  pl.BlockSpec((d, tn), lambda j, i: (0, 2 * nj + j)),
                  pl.BlockSpec((CONV_K, tn), lambda j, i: (0, j))] + cast_in_specs,
        out_specs=[pl.BlockSpec((tm, tn), lambda j, i: (i, j))] + cast_out_specs,
        out_shape=[jax.ShapeDtypeStruct((t, width), BF16)] + cast_shapes,
        scratch_shapes=[pltpu.VMEM((d, 3 * tn), BF16),
                        pltpu.VMEM((8, tn), F32)],
        compiler_params=_params(2),
        name="conv_branch",
    )(xn, w_in, w_in, w_in, conv_w, *cast_args)
    return outs[0], outs[1:]


def _sgu_kernel(x_ref, g_ref, wu_ref, wv_ref, lng_ref, lnb_ref, ws_ref, bsx_ref, xn_ref, o_ref, wbf_ref):
    tm = x_ref.shape[0]
    w = o_ref.shape[1]
    hd = w // SGU_HEADS

    @pl.when(pl.program_id(0) == 0)
    def _():
        wbf_ref[:, 0:w] = wu_ref[...].astype(BF16)
        wbf_ref[:, w:2 * w] = wv_ref[...].astype(BF16)

    xn = _rms_scale(x_ref[...], g_ref[...]).astype(BF16)
    xn_ref[...] = xn
    gz = jax.nn.gelu(_dot(xn, wbf_ref[...]))
    v = gz[:, w:2 * w]
    mu = jnp.mean(v, axis=-1, keepdims=True)
    vc = v - mu
    var = jnp.mean(vc * vc, axis=-1, keepdims=True)
    vn = (vc * lax.rsqrt(var + EPS) * lng_ref[...] + lnb_ref[...]).astype(BF16)
    ii = lax.broadcasted_iota(jnp.int32, (SGU_BLOCK, SGU_BLOCK), 0)
    jj = lax.broadcasted_iota(jnp.int32, (SGU_BLOCK, SGU_BLOCK), 1)
    mask = (jj // CHUNK) <= (ii // CHUNK)
    for h in range(SGU_HEADS):
        wm = jnp.where(mask, ws_ref[h], 0.0).astype(BF16)
        cs = slice(h * hd, (h + 1) * hd)
        for n in range(tm // SGU_BLOCK):
            rs = slice(n * SGU_BLOCK, (n + 1) * SGU_BLOCK)
            vm = _dot(wm, vn[rs, cs]) + bsx_ref[:, cs]
            o_ref[rs, cs] = (gz[rs, cs] * vm).astype(o_ref.dtype)


def _sgu_branch(x, g, w_in, col0, ln_g, ln_b, w_s, b_s, tm=512):
    t, d = x.shape
    w = ln_g.shape[0]
    hd = w // SGU_HEADS
    assert col0 % w == 0
    c0 = col0 // w
    bsx = jnp.repeat(b_s.T, hd, axis=1)
    once = pl.Buffered(1)
    return pl.pallas_call(
        _sgu_kernel,
        grid=(t // tm,),
        in_specs=[pl.BlockSpec((tm, d), lambda i: (i, 0)),
                  pl.BlockSpec((1, d), lambda i: (0, 0)),
                  pl.BlockSpec((d, w), lambda i: (0, c0), pipeline_mode=once),
                  pl.BlockSpec((d, w), lambda i: (0, c0 + 1), pipeline_mode=once),
                  pl.BlockSpec((1, w), lambda i: (0, 0)),
                  pl.BlockSpec((1, w), lambda i: (0, 0)),
                  pl.BlockSpec((SGU_HEADS, SGU_BLOCK, SGU_BLOCK), lambda i: (0, 0, 0)),
                  pl.BlockSpec((SGU_BLOCK, w), lambda i: (0, 0))],
        out_specs=[pl.BlockSpec((tm, d), lambda i: (i, 0)),
                   pl.BlockSpec((tm, w), lambda i: (i, 0))],
        out_shape=[jax.ShapeDtypeStruct((t, d), BF16),
                   jax.ShapeDtypeStruct((t, w), BF16)],
        scratch_shapes=[pltpu.VMEM((d, 2 * w), BF16)],
        compiler_params=_params(1),
        name="sgu_branch",
    )(x, g.reshape(1, d), w_in, w_in, ln_g.reshape(1, w), ln_b.reshape(1, w), w_s, bsx)


def _upgate_kernel(xn_ref, ya_ref, yb_ref, wgc_ref, wgs_ref, wua_ref, wub_ref, eg_ref, eu_ref, ed_ref,
                   o_ref, eg_bf_ref, eu_bf_ref, ed_bf_ref):
    eg_bf_ref[...] = eg_ref[...].astype(BF16)
    eu_bf_ref[...] = eu_ref[...].astype(BF16)
    ed_bf_ref[...] = ed_ref[...].astype(BF16)

    xn = xn_ref[...]
    m = (jax.nn.sigmoid(_dot(xn, wgc_ref[...])) * _dot(ya_ref[...], wua_ref[...])
         + jax.nn.sigmoid(_dot(xn, wgs_ref[...])) * _dot(yb_ref[...], wub_ref[...]))
    o_ref[...] = m.astype(o_ref.dtype)


def _upgate(xn, ya, yb, w_gates, w_up_a, w_up_b, w_eg, w_eu, w_ed, tm=1024, tn=512):
    t, d = xn.shape
    wa = ya.shape[1]
    wb = yb.shape[1]
    dout = w_up_a.shape[1]
    c0 = 0
    nj = dout // tn
    ni = t // tm
    n_e, d_e, f_e = w_eg.shape
    up_rows = n_e * d_e // (nj * ni)
    down_rows = n_e * f_e // (nj * ni)
    assert up_rows * nj * ni == n_e * d_e and up_rows % 16 == 0
    assert down_rows * nj * ni == n_e * f_e and down_rows % 16 == 0
    up_spec = pl.BlockSpec((up_rows, f_e), lambda j, i: (j * ni + i, 0))
    down_spec = pl.BlockSpec((down_rows, d_e), lambda j, i: (j * ni + i, 0))
    m, eg_bf, eu_bf, ed_bf = pl.pallas_call(
        _upgate_kernel,
        grid=(nj, ni),
        in_specs=[pl.BlockSpec((tm, d), lambda j, i: (i, 0)),
                  pl.BlockSpec((tm, wa), lambda j, i: (i, 0)),
                  pl.BlockSpec((tm, wb), lambda j, i: (i, 0)),
                  pl.BlockSpec((d, tn), lambda j, i: (0, c0 + j)),
                  pl.BlockSpec((d, tn), lambda j, i: (0, c0 + nj + j)),
                  pl.BlockSpec((wa, tn), lambda j, i: (0, j)),
                  pl.BlockSpec((wb, tn), lambda j, i: (0, j)),
                  up_spec, up_spec, down_spec],
        out_specs=[pl.BlockSpec((tm, tn), lambda j, i: (i, j)), up_spec, up_spec, down_spec],
        out_shape=[jax.ShapeDtypeStruct((t, dout), BF16),
                   jax.ShapeDtypeStruct((n_e * d_e, f_e), BF16),
                   jax.ShapeDtypeStruct((n_e * d_e, f_e), BF16),
                   jax.ShapeDtypeStruct((n_e * f_e, d_e), BF16)],
        compiler_params=_params(2),
        name="upgate",
    )(xn, ya, yb, w_gates, w_gates, w_up_a, w_up_b,
      w_eg.reshape(n_e * d_e, f_e), w_eu.reshape(n_e * d_e, f_e), w_ed.reshape(n_e * f_e, d_e))
    return (m, eg_bf.reshape(n_e, d_e, f_e), eu_bf.reshape(n_e, d_e, f_e),
            ed_bf.reshape(n_e, f_e, d_e))


def _outproj_kernel(m_ref, w_ref, x_ref, o_ref):
    o_ref[...] = x_ref[...] + _dot(m_ref[...], w_ref[...])


def _outproj(m, w_out_bf, x, tm=512):
    t, d = m.shape
    dout = w_out_bf.shape[1]
    return pl.pallas_call(
        _outproj_kernel,
        grid=(t // tm,),
        in_specs=[pl.BlockSpec((tm, d), lambda i: (i, 0)),
                  pl.BlockSpec((d, dout), lambda i: (0, 0), pipeline_mode=pl.Buffered(1)),
                  pl.BlockSpec((tm, dout), lambda i: (i, 0))],
        out_specs=pl.BlockSpec((tm, dout), lambda i: (i, 0)),
        out_shape=jax.ShapeDtypeStruct((t, dout), F32),
        compiler_params=_params(1),
        name="outproj",
    )(m, w_out_bf, x)


def _argmax_rows(rows):
    best = rows[0]
    idx = jnp.zeros(rows[0].shape, jnp.int32)
    for k in range(1, len(rows)):
        better = rows[k] > best
        best = jnp.where(better, rows[k], best)
        idx = jnp.where(better, k, idx)
    return best, idx


def _softmax_rows(rows):
    mx = functools.reduce(jnp.maximum, rows)
    ex = [jnp.exp(r - mx) for r in rows]
    den = functools.reduce(lambda a, b: a + b, ex)
    return [e / den for e in ex]


def _route_sort_kernel(h_ref, g_ref, wr_ref, br_ref, xs_ref, meta_ref, cnt_ref, before_ref):
    tm = h_ref.shape[0]
    xn = _rms_scale(h_ref[...], g_ref[...])
    xn_hi = xn.astype(BF16)
    xn_lo = (xn - xn_hi.astype(F32)).astype(BF16)
    wr = wr_ref[...]
    wr_hi = wr.astype(BF16)
    wr_lo = (wr - wr_hi.astype(F32)).astype(BF16)
    nt_dims = (((1,), (1,)), ((), ()))
    lt = (lax.dot_general(wr_hi, xn_hi, nt_dims, preferred_element_type=F32)
          + lax.dot_general(wr_hi, xn_lo, nt_dims, preferred_element_type=F32)
          + lax.dot_general(wr_lo, xn_hi, nt_dims, preferred_element_type=F32)) + br_ref[...]
    pgs = _softmax_rows([lt[k:k + 1, :] for k in range(N_GROUPS)])
    pg, gi = _argmax_rows(pgs)
    sel = []
    for k in range(EXPERTS_PER_GROUP):
        r = jnp.zeros_like(pg)
        for g in range(N_GROUPS):
            row = N_GROUPS + g * EXPERTS_PER_GROUP + k
            r = jnp.where(gi == g, lt[row:row + 1, :], r)
        sel.append(r)
    pes = _softmax_rows(sel)
    p1, e1 = _argmax_rows(pes)
    rest = [jnp.where(e1 == k, -1.0, pes[k]) for k in range(EXPERTS_PER_GROUP)]
    p2, e2 = _argmax_rows(rest)
    den = p1 + p2
    w1 = pg * (p1 / den)
    w2 = pg * (p2 / den)
    lo = jnp.minimum(e1, e2)
    hi = jnp.maximum(e1, e2)
    w_lo = jnp.where(e1 < e2, w1, w2)
    w_hi = jnp.where(e1 < e2, w2, w1)
    ea = gi * EXPERTS_PER_GROUP + lo
    eb = gi * EXPERTS_PER_GROUP + hi

    erow = lax.broadcasted_iota(jnp.int32, (N_EXPERTS, tm), 0)
    oh_a = (erow == ea).astype(F32)
    oh_b = (erow == eb).astype(F32)

    @pl.when(pl.program_id(0) == 0)
    def _():
        a = lax.broadcasted_iota(jnp.int32, (tm, tm), 0)
        b = lax.broadcasted_iota(jnp.int32, (tm, tm), 1)
        before_ref[...] = (a < b).astype(BF16)

    cum = _dot((oh_a + oh_b).astype(BF16), before_ref[...])
    cnt = jnp.sum(oh_a + oh_b, axis=1, keepdims=True)
    padded = jnp.floor((cnt + (GROUP - 1)) * (1.0 / GROUP)) * GROUP
    pos_a = jnp.sum(oh_a * cum + jnp.where(erow < ea, padded, 0.0), axis=0, keepdims=True)
    pos_b = jnp.sum(oh_b * cum + jnp.where(erow < eb, padded, 0.0), axis=0, keepdims=True)

    d = h_ref.shape[1]

    def gate_rows(w):
        hi = w.astype(BF16).astype(F32)
        lo = w - hi
        k = lax.broadcasted_iota(jnp.int32, (LANES, tm), 0)
        return jnp.where(k == 0, hi, jnp.where(k == 1, lo, 0.0)).astype(BF16)

    gate_a = gate_rows(w_lo)
    gate_b = gate_rows(w_hi)

    def sort_rows(r0, n):
        q = r0 + lax.broadcasted_iota(jnp.int32, (n, tm), 0)
        perm_a = jnp.where(q == pos_a.astype(jnp.int32), 1.0, 0.0).astype(BF16)
        perm_b = jnp.where(q == pos_b.astype(jnp.int32), 1.0, 0.0).astype(BF16)
        xs_ref[r0:r0 + n, 0:d] = _dot(perm_a + perm_b, xn_hi).astype(xs_ref.dtype)
        gates = (lax.dot_general(perm_a, gate_a, nt_dims, preferred_element_type=F32)
                 + lax.dot_general(perm_b, gate_b, nt_dims, preferred_element_type=F32))
        xs_ref[r0:r0 + n, d:d + LANES] = gates.astype(xs_ref.dtype)

    n_rows = xs_ref.shape[0]
    head = n_rows - HALF_ROWS
    used = jnp.sum(padded)
    sort_rows(0, head)

    @pl.when(used > head)
    def _():
        sort_rows(head, HALF_ROWS)

    @pl.when(used <= head)
    def _():
        xs_ref[head:n_rows, :] = jnp.zeros((HALF_ROWS, xs_ref.shape[1]), xs_ref.dtype)

    cnt_ref[...] = jnp.broadcast_to(cnt, cnt_ref.shape)
    meta_ref[0:1, :] = pos_a
    meta_ref[1:2, :] = pos_b
    meta_ref[2:8, :] = jnp.zeros((6, tm), F32)


def _route_sort(h1, g, w_rg, b_rg, w_re, b_re):
    t, d = h1.shape
    tm = ROUTE_BLOCK
    nb = t // tm
    n_log = w_rg.shape[1] + w_re.shape[1]
    wr = jnp.concatenate([w_rg, w_re], axis=1).T
    wr = jnp.pad(wr, ((0, ROUTER_ROWS - n_log), (0, 0)))
    br = jnp.pad(jnp.concatenate([b_rg, b_re]), (0, ROUTER_ROWS - n_log)).reshape(ROUTER_ROWS, 1)
    return pl.pallas_call(
        _route_sort_kernel,
        grid=(nb,),
        in_specs=[pl.BlockSpec((tm, d), lambda i: (i, 0)),
                  pl.BlockSpec((1, d), lambda i: (0, 0)),
                  pl.BlockSpec((ROUTER_ROWS, d), lambda i: (0, 0)),
                  pl.BlockSpec((ROUTER_ROWS, 1), lambda i: (0, 0))],
        out_specs=[pl.BlockSpec((LOCAL_ROWS, d + LANES), lambda i: (i, 0)),
                   pl.BlockSpec((8, tm), lambda i: (0, i)),
                   pl.BlockSpec((N_EXPERTS, LANES), lambda i: (i, 0))],
        out_shape=[jax.ShapeDtypeStruct((nb * LOCAL_ROWS, d + LANES), BF16),
                   jax.ShapeDtypeStruct((8, t), F32),
                   jax.ShapeDtypeStruct((nb * N_EXPERTS, LANES), F32)],
        scratch_shapes=[pltpu.VMEM((tm, tm), BF16)],
        compiler_params=_params(1),
        name="route_sort",
    )(h1, g.reshape(1, d), wr, br)


def _group_copy(src_ref, s_group, dst_ref, d_group, sem):
    return pltpu.make_async_copy(src_ref.at[pl.ds(pl.multiple_of(s_group * GROUP, GROUP), GROUP), :],
                                 dst_ref.at[pl.ds(pl.multiple_of(d_group * GROUP, GROUP), GROUP), :], sem)


def _expert_kernel(te_ref, nxt_ref, nreal_ref, nt_ref, gsrc_ref, gdst_ref, tail_ref,
                   xs_ref, wg_ref, wu_ref, wd_ref, ys_ref,
                   xbuf_ref, ybuf_ref, zero_ref, wgu_buf_ref, wd_buf_ref,
                   cur_ref, gsem, ssem, zsem, wsem):
    r = pl.program_id(0)
    nt = nt_ref[0]

    def gather(q, slot):
        for i in range(TILE_GROUPS):
            _group_copy(xs_ref, gsrc_ref[q * TILE_GROUPS + i], xbuf_ref, slot * TILE_GROUPS + i,
                        gsem.at[slot]).start(priority=1)

    def weight_copies(e, wslot):
        f = wg_ref.shape[2]
        return (pltpu.make_async_copy(wg_ref.at[e], wgu_buf_ref.at[wslot, :, pl.ds(0, f)], wsem.at[wslot]),
                pltpu.make_async_copy(wu_ref.at[e], wgu_buf_ref.at[wslot, :, pl.ds(f, f)], wsem.at[wslot]),
                pltpu.make_async_copy(wd_ref.at[e], wd_buf_ref.at[wslot], wsem.at[wslot]))

    def tile_wait(src, dst, sem):
        pltpu.make_async_copy(src.at[pl.ds(0, MOE_TILE), :], dst.at[pl.ds(0, MOE_TILE), :], sem).wait()

    @pl.when(r == 0)
    def _():
        cur_ref[0] = -1
        cur_ref[1] = -1
        for cp in weight_copies(te_ref[0], 0):
            cp.start()
        gather(0, 0)
        zero_ref[...] = jnp.zeros_like(zero_ref)
        ybuf_ref[...] = jnp.zeros_like(ybuf_ref)
        n_blocks = tail_ref.shape[0]

        def fill(make):
            def body(g, c):
                make(g)
                return c
            return body

        for blk in range(n_blocks):
            lax.fori_loop(tail_ref[blk], GROUPS_PER_BLOCK, fill(
                lambda g, blk=blk: _group_copy(zero_ref, 0, ys_ref, blk * GROUPS_PER_BLOCK + g, zsem).start()), 0)
        for blk in range(n_blocks):
            lax.fori_loop(tail_ref[blk], GROUPS_PER_BLOCK, fill(
                lambda g, blk=blk: _group_copy(zero_ref, 0, ys_ref, blk * GROUPS_PER_BLOCK + g, zsem).wait()), 0)

    @pl.when(r < nt)
    def _():
        e = te_ref[r]
        slot = r % 2

        gather(jnp.minimum(r + 1, nt - 1), 1 - slot)

        @pl.when(cur_ref[0] != e)
        def _():
            wslot = (cur_ref[1] + 1) % 2
            for cp in weight_copies(e, wslot):
                cp.wait()
            cur_ref[0] = e
            cur_ref[1] = cur_ref[1] + 1

            @pl.when(nxt_ref[r] >= 0)
            def _():
                for cp in weight_copies(nxt_ref[r], 1 - wslot):
                    cp.start()

        wslot = cur_ref[1] % 2
        tile_wait(xs_ref, xbuf_ref, gsem.at[slot])

        @pl.when(r >= 2)
        def _():
            tile_wait(ybuf_ref, ys_ref, ssem.at[slot])

        d = ybuf_ref.shape[1]
        f = wd_buf_ref.shape[1]
        def chain(row0, n_rows):
            rows = pl.ds(pl.multiple_of(slot * MOE_TILE + row0, HALF_ROWS), n_rows)
            x = xbuf_ref[rows, 0:d]
            gate_parts = xbuf_ref[rows, d:d + LANES].astype(F32)
            gate = gate_parts[:, 0:1] + gate_parts[:, 1:2]
            gu = _dot(x, wgu_buf_ref[wslot])
            hid = (jax.nn.silu(gu[:, 0:f]) * gu[:, f:2 * f]).astype(BF16)
            ybuf_ref[rows, :] = (_dot(hid, wd_buf_ref[wslot]) * gate).astype(ybuf_ref.dtype)

        half_groups = HALF_ROWS // GROUP
        n_halves = (nreal_ref[r] + half_groups - 1) // half_groups
        for count in range(1, MOE_TILE // HALF_ROWS + 1):
            @pl.when(n_halves == count)
            def _(count=count):
                for part in range(count // 2):
                    chain(part * MXU_ROWS, MXU_ROWS)
                if count % 2:
                    chain((count // 2) * MXU_ROWS, HALF_ROWS)

        for i in range(TILE_GROUPS):
            _group_copy(ybuf_ref, slot * TILE_GROUPS + i, ys_ref, gdst_ref[r * TILE_GROUPS + i],
                        ssem.at[slot]).start(priority=1)

        @pl.when(r == nt - 1)
        def _():
            tile_wait(xs_ref, xbuf_ref, gsem.at[1 - slot])
            tile_wait(ybuf_ref, ys_ref, ssem.at[slot])

            @pl.when(r >= 1)
            def _():
                tile_wait(ybuf_ref, ys_ref, ssem.at[1 - slot])


def _experts(plan, xs, w_gate, w_up, w_down):
    d = w_gate.shape[1]
    f = w_gate.shape[2]
    tile_e, next_e, n_real, n_tiles, gsrc, gdst, tail = plan
    r_max = tile_e.shape[0]
    n_blocks = tail.shape[0]
    any_spec = pl.BlockSpec(memory_space=pl.ANY)
    grid_spec = pltpu.PrefetchScalarGridSpec(
        num_scalar_prefetch=7,
        grid=(r_max,),
        in_specs=[any_spec, any_spec, any_spec, any_spec],
        out_specs=any_spec,
        scratch_shapes=[pltpu.VMEM((2 * MOE_TILE, d + LANES), BF16),
                        pltpu.VMEM((2 * MOE_TILE, d), BF16),
                        pltpu.VMEM((GROUP, d), BF16),
                        pltpu.VMEM((2, d, 2 * f), BF16),
                        pltpu.VMEM((2, f, d), BF16),
                        pltpu.SMEM((2,), jnp.int32),
                        pltpu.SemaphoreType.DMA((2,)),
                        pltpu.SemaphoreType.DMA((2,)),
                        pltpu.SemaphoreType.DMA(()),
                        pltpu.SemaphoreType.DMA((2,))],
    )
    return pl.pallas_call(
        _expert_kernel,
        grid_spec=grid_spec,
        out_shape=jax.ShapeDtypeStruct((n_blocks * LOCAL_ROWS, d), BF16),
        compiler_params=_params(1),
        name="experts",
    )(tile_e, next_e, n_real, n_tiles, gsrc, gdst, tail, xs, w_gate, w_up, w_down)


COMBINE_BUFS = 3


def _combine_kernel(ys_ref, h_ref, cm_ref, g_ref, o_ref, ybuf_ref, hbuf_ref, sem):
    i = pl.program_id(0)
    n = pl.num_programs(0)
    tm = o_ref.shape[0]
    rows = ybuf_ref.shape[1]

    def copies(blk, slot):
        return (pltpu.make_async_copy(ys_ref.at[pl.ds(pl.multiple_of(blk * rows, rows), rows), :],
                                      ybuf_ref.at[slot], sem.at[slot]),
                pltpu.make_async_copy(h_ref.at[pl.ds(pl.multiple_of(blk * tm, tm), tm), :],
                                      hbuf_ref.at[slot], sem.at[slot]))

    @pl.when(i == 0)
    def _():
        for blk in range(COMBINE_BUFS - 1):
            for cp in copies(blk, blk):
                cp.start()

    ahead = i + COMBINE_BUFS - 1

    @pl.when(ahead < n)
    def _():
        for cp in copies(ahead, ahead % COMBINE_BUFS):
            cp.start()

    slot = i % COMBINE_BUFS
    for cp in copies(i, slot):
        cp.wait()
    cm = cm_ref[...]
    q = lax.broadcasted_iota(jnp.int32, (tm, rows), 1)
    sel = jnp.where((q == cm[:, 0:1].astype(jnp.int32)) | (q == cm[:, 1:2].astype(jnp.int32)), 1.0, 0.0)
    h2 = hbuf_ref[slot] + _dot(sel.astype(BF16), ybuf_ref[slot])
    o_ref[...] = _rms_scale(h2, g_ref[...])


def _combine(ys, h1, cmeta, g):
    t, d = h1.shape
    tm = ROUTE_BLOCK
    assert t // tm >= COMBINE_BUFS - 1
    any_spec = pl.BlockSpec(memory_space=pl.ANY)
    return pl.pallas_call(
        _combine_kernel,
        grid=(t // tm,),
        in_specs=[any_spec, any_spec,
                  pl.BlockSpec((tm, LANES), lambda i: (i, 0)),
                  pl.BlockSpec((1, d), lambda i: (0, 0))],
        out_specs=pl.BlockSpec((tm, d), lambda i: (i, 0)),
        out_shape=jax.ShapeDtypeStruct((t, d), F32),
        scratch_shapes=[pltpu.VMEM((COMBINE_BUFS, LOCAL_ROWS, d), BF16),
                        pltpu.VMEM((COMBINE_BUFS, tm, d), F32),
                        pltpu.SemaphoreType.DMA((COMBINE_BUFS,))],
        compiler_params=_params(1),
        name="combine",
    )(ys, h1, cmeta, g.reshape(1, d))


def _expert_plan(counts, n_blocks):
    cnt = counts[:, 0].astype(jnp.int32).reshape(n_blocks, N_EXPERTS)
    groups = (cnt + GROUP - 1) // GROUP
    first = jnp.cumsum(groups, axis=1) - groups
    upto = jnp.cumsum(groups, axis=0)
    per_expert = upto[-1]
    tiles_e = (per_expert + TILE_GROUPS - 1) // TILE_GROUPS
    tile_end = jnp.cumsum(tiles_e)
    n_tiles = tile_end[-1]
    max_groups = 2 * ROUTE_BLOCK * n_blocks // GROUP + n_blocks * N_EXPERTS
    r_max = max_groups // TILE_GROUPS + N_EXPERTS
    tile_ids = jnp.arange(r_max, dtype=jnp.int32)
    tile = jnp.minimum(tile_ids, n_tiles - 1)
    tile_e = jnp.sum((tile_end[None, :] <= tile[:, None]).astype(jnp.int32), axis=1)
    later = (tile_e[None, :] > tile_e[:, None]) & (tile_ids[None, :] < n_tiles)
    next_e = jnp.min(jnp.where(later, tile_e[None, :], N_EXPERTS), axis=1)
    next_e = jnp.where(next_e == N_EXPERTS, -1, next_e)

    slot = jnp.arange(r_max * TILE_GROUPS, dtype=jnp.int32)
    s_tile = slot // TILE_GROUPS
    oh_e = jnp.repeat(tile_e, TILE_GROUPS)[:, None] == jnp.arange(N_EXPERTS, dtype=jnp.int32)[None, :]

    def by_expert(table):
        return jnp.sum(jnp.where(oh_e[:, None, :], table[None], 0), axis=-1)

    k = slot - by_expert(((tile_end - tiles_e) * TILE_GROUPS)[None, :])[:, 0]
    real = (k < by_expert(per_expert[None, :])[:, 0]) & (s_tile < n_tiles)
    upto_e = by_expert(upto)
    blk = jnp.minimum(jnp.sum((upto_e <= k[:, None]).astype(jnp.int32), axis=1), n_blocks - 1)
    oh_b = blk[:, None] == jnp.arange(n_blocks, dtype=jnp.int32)[None, :]

    def by_block(table_se):
        return jnp.sum(jnp.where(oh_b, table_se, 0), axis=1)

    before = by_block(upto_e - by_expert(groups))
    src = blk * GROUPS_PER_BLOCK + by_block(by_expert(first)) + (k - before)
    zero_group = GROUPS_PER_BLOCK - 1
    spare = n_blocks * GROUPS_PER_BLOCK + (s_tile % 2) * TILE_GROUPS + slot % TILE_GROUPS
    gsrc = jnp.where(real, src, zero_group)
    gdst = jnp.where(real, src, spare)
    n_spare_blocks = -(-2 * TILE_GROUPS // GROUPS_PER_BLOCK)
    tail = jnp.concatenate([jnp.sum(groups, axis=1), jnp.zeros((n_spare_blocks,), jnp.int32)])
    n_real = jnp.sum(real.reshape(r_max, TILE_GROUPS).astype(jnp.int32), axis=1)
    return tile_e, next_e, n_real, n_tiles.reshape(1), gsrc, gdst, tail


def _layer(h, norm_mix_g, w_in, conv_w, sgu_ln_g, sgu_ln_b, sgu_w_s, sgu_b_s, w_up_conv,
           w_up_sgu, w_out, norm_ffn_g, w_rg, b_rg, w_re, b_re, w_eg, w_eu, w_ed, out_g):
    t, d = h.shape
    conv_width = conv_w.shape[1]
    sgu_width = sgu_ln_g.shape[0]
    xn, yb = _sgu_branch(h, norm_mix_g, w_in, 3 * conv_width, sgu_ln_g, sgu_ln_b, sgu_w_s, sgu_b_s)
    gate_col0 = 3 * conv_width + 2 * sgu_width
    ya, (wg_bf, wua_bf, wub_bf, wout_bf) = _conv_branch(
        xn, w_in, conv_w, conv_width,
        to_cast=[(w_in, gate_col0, w_in.shape[1] - gate_col0, math.gcd(gate_col0, w_in.shape[1])),
                 (w_up_conv, 0, d, d), (w_up_sgu, 0, d, d), (w_out, 0, d, d)])
    m, eg_bf, eu_bf, ed_bf = _upgate(xn, ya, yb, wg_bf, wua_bf, wub_bf, w_eg, w_eu, w_ed)
    h1 = _outproj(m, wout_bf, h)
    xs, meta, counts = _route_sort(h1, norm_ffn_g, w_rg, b_rg, w_re, b_re)
    ys = _experts(_expert_plan(counts, t // ROUTE_BLOCK), xs, eg_bf, eu_bf, ed_bf)
    cmeta = jnp.pad(meta[0:2].T, ((0, 0), (0, LANES - 2)))
    return _combine(ys, h1, cmeta, out_g)


def kernel(x, norm_mix_g, w_in, conv_w, sgu_ln_g, sgu_ln_b, sgu_w_s, sgu_b_s, w_up_conv, w_up_sgu, w_out, norm_ffn_g, w_router_group, b_router_group, w_router_expert, b_router_expert, w_exp_gate, w_exp_up, w_exp_down, norm_final_g):
    bsz, s, d = x.shape
    depth = w_in.shape[0]
    assert bsz == 1 and depth == 1, "causal conv carry and the fused final norm assume one sequence, one layer"
    assert s % ROUTE_BLOCK == 0
    out = _layer(x.reshape(s, d), norm_mix_g[0], w_in[0], conv_w[0], sgu_ln_g[0], sgu_ln_b[0],
                 sgu_w_s[0], sgu_b_s[0], w_up_conv[0], w_up_sgu[0], w_out[0], norm_ffn_g[0],
                 w_router_group[0], b_router_group[0], w_router_expert[0], b_router_expert[0],
                 w_exp_gate[0], w_exp_up[0], w_exp_down[0], norm_final_g)
    return out.reshape(bsz, s, d)
```

```python
import functools
import math

import jax
import jax.numpy as jnp
from jax import lax
from jax.experimental import pallas as pl
from jax.experimental.pallas import tpu as pltpu

F32 = jnp.float32
BF16 = jnp.bfloat16

EPS = 1e-6
CHUNK = 64
CONV_K = 3
SGU_HEADS = 8
SGU_BLOCK = 128
N_GROUPS = 4
EXPERTS_PER_GROUP = 4
N_EXPERTS = N_GROUPS * EXPERTS_PER_GROUP
ROUTER_ROWS = 32
LANES = 128

VMEM_LIMIT_BYTES = 56 * 1024 * 1024

ROUTE_BLOCK = 512
GROUP = 16
MXU_ROWS = 256
HALF_ROWS = MXU_ROWS // 2
LOCAL_ROWS = -(-(2 * ROUTE_BLOCK + N_EXPERTS * (GROUP - 1)) // MXU_ROWS) * MXU_ROWS
GROUPS_PER_BLOCK = LOCAL_ROWS // GROUP
assert LOCAL_ROWS - (2 * ROUTE_BLOCK + N_EXPERTS * (GROUP - 1)) >= GROUP
MOE_TILE = 512
TILE_GROUPS = MOE_TILE // GROUP


def _params(n_axes):
    return pltpu.CompilerParams(
        dimension_semantics=("arbitrary",) * n_axes,
        vmem_limit_bytes=VMEM_LIMIT_BYTES)


def _dot(a, b):
    return jnp.dot(a, b, preferred_element_type=F32)


def _rms_scale(x, g):
    ms = jnp.mean(x * x, axis=-1, keepdims=True)
    return x * lax.rsqrt(ms + EPS) * g


RING = 3


def _ring_fetch(srcs, bufs, sem, step, n_steps, n_blocks):
    def copies(s):
        blk = s % n_blocks
        slot = s % RING
        return [pltpu.make_async_copy(
            src.at[pl.ds(pl.multiple_of(blk * buf.shape[1], buf.shape[1]), buf.shape[1]), :],
            buf.at[slot], sem.at[slot]) for src, buf in zip(srcs, bufs)]

    @pl.when(step == 0)
    def _():
        for s in range(RING - 1):
            for cp in copies(s):
                cp.start()

    @pl.when(step + RING - 1 < n_steps)
    def _():
        for cp in copies(step + RING - 1):
            cp.start()

    for cp in copies(step):
        cp.wait()
    return step % RING


def _conv_kernel(cast_chunks, xn_ref, wb_ref, wc_ref, wh_ref, cw_ref, *rest):
    n_in = sum(cast_chunks)
    cast_in, o_ref = rest[:n_in], rest[n_in]
    cast_out = rest[n_in + 1:n_in + 1 + len(cast_chunks)]
    wbf_ref, carry_ref = rest[n_in + 1 + len(cast_chunks):]
    i = pl.program_id(1)
    tn = wb_ref.shape[1]
    tm = xn_ref.shape[0]
    src = iter(cast_in)
    for dst, n_chunks in zip(cast_out, cast_chunks):
        wc = dst.shape[1] // n_chunks
        for k in range(n_chunks):
            dst[:, k * wc:(k + 1) * wc] = next(src)[...].astype(BF16)

    @pl.when(i == 0)
    def _():
        wbf_ref[:, 0:tn] = wb_ref[...].astype(BF16)
        wbf_ref[:, tn:2 * tn] = wc_ref[...].astype(BF16)
        wbf_ref[:, 2 * tn:3 * tn] = wh_ref[...].astype(BF16)
        carry_ref[...] = jnp.zeros_like(carry_ref)

    proj = _dot(xn_ref[...], wbf_ref[...])
    b = proj[:, 0:tn]
    p = proj[:, tn:2 * tn] * proj[:, 2 * tn:3 * tn]
    prev = carry_ref[...]
    carry_ref[...] = p[tm - 8:tm, :]
    row = lax.broadcasted_iota(jnp.int32, p.shape, 0)
    p1 = jnp.where(row == 0, prev[7:8, :], pltpu.roll(p, 1, axis=0))
    p2 = jnp.where(row == 0, prev[6:7, :],
                   jnp.where(row == 1, prev[7:8, :], pltpu.roll(p, 2, axis=0)))
    cw = cw_ref[...]
    y = b * (cw[0:1, :] * p2 + cw[1:2, :] * p1 + cw[2:3, :] * p)
    o_ref[...] = y.astype(o_ref.dtype)


def _conv_branch(xn, w_in, conv_w, width, to_cast, tm=1024, tn=256):
    t, d = xn.shape
    nj = width // tn
    ni = t // tm
    steps = nj * ni
    cast_args, cast_in_specs, cast_out_specs, cast_shapes, cast_chunks = [], [], [], [], []
    for arr, col0, n_cols, chunk in to_cast:
        rows = arr.shape[0] // steps
        assert rows * steps == arr.shape[0] and rows % 16 == 0
        assert col0 % chunk == 0 and n_cols % chunk == 0
        for k in range(n_cols // chunk):
            cast_args.append(arr)
            cast_in_specs.append(
                pl.BlockSpec((rows, chunk), lambda j, i, c=col0 // chunk + k: (j * ni + i, c)))
        cast_chunks.append(n_cols // chunk)
        cast_out_specs.append(pl.BlockSpec((rows, n_cols), lambda j, i: (j * ni + i, 0)))
        cast_shapes.append(jax.ShapeDtypeStruct((arr.shape[0], n_cols), BF16))
    outs = pl.pallas_call(
        functools.partial(_conv_kernel, tuple(cast_chunks)),
        grid=(nj, ni),
        in_specs=[pl.BlockSpec((tm, d), lambda j, i: (i, 0)),
                  pl.BlockSpec((d, tn), lambda j, i: (0, j)),
                  pl.BlockSpec((d, tn), lambda j, i: (0, nj + j)),
                  pl.BlockSpec((d, tn), lambda j, i: (0, 2 * nj + j)),
                  pl.BlockSpec((CONV_K, tn), lambda j, i: (0, j))] + cast_in_specs,
        out_specs=[pl.BlockSpec((tm, tn), lambda j, i: (i, j))] + cast_out_specs,
        out_shape=[jax.ShapeDtypeStruct((t, width), BF16)] + cast_shapes,
        scratch_shapes=[pltpu.VMEM((d, 3 * tn), BF16),
                        pltpu.VMEM((8, tn), F32)],
        compiler_params=_params(2),
        name="conv_branch",
    )(xn, w_in, w_in, w_in, conv_w, *cast_args)
    return outs[0], outs[1:]


def _sgu_kernel(x_ref, g_ref, wu_ref, wv_ref, lng_ref, lnb_ref, ws_ref, bsx_ref, xn_ref, o_ref, wbf_ref):
    tm = x_ref.shape[0]
    w = o_ref.shape[1]
    hd = w // SGU_HEADS

    @pl.when(pl.program_id(0) == 0)
    def _():
        wbf_ref[:, 0:w] = wu_ref[...].astype(BF16)
        wbf_ref[:, w:2 * w] = wv_ref[...].astype(BF16)

    xn = _rms_scale(x_ref[...], g_ref[...]).astype(BF16)
    xn_ref[...] = xn
    gz = jax.nn.gelu(_dot(xn, wbf_ref[...]))
    v = gz[:, w:2 * w]
    mu = jnp.mean(v, axis=-1, keepdims=True)
    vc = v - mu
    var = jnp.mean(vc * vc, axis=-1, keepdims=True)
    vn = (vc * lax.rsqrt(var + EPS) * lng_ref[...] + lnb_ref[...]).astype(BF16)
    ii = lax.broadcasted_iota(jnp.int32, (SGU_BLOCK, SGU_BLOCK), 0)
    jj = lax.broadcasted_iota(jnp.int32, (SGU_BLOCK, SGU_BLOCK), 1)
    mask = (jj // CHUNK) <= (ii // CHUNK)
    for h in range(SGU_HEADS):
        wm = jnp.where(mask, ws_ref[h], 0.0).astype(BF16)
        cs = slice(h * hd, (h + 1) * hd)
        for n in range(tm // SGU_BLOCK):
            rs = slice(n * SGU_BLOCK, (n + 1) * SGU_BLOCK)
            vm = _dot(wm, vn[rs, cs]) + bsx_ref[:, cs]
            o_ref[rs, cs] = (gz[rs, cs] * vm).astype(o_ref.dtype)


def _sgu_branch(x, g, w_in, col0, ln_g, ln_b, w_s, b_s, tm=512):
    t, d = x.shape
    w = ln_g.shape[0]
    hd = w // SGU_HEADS
    assert col0 % w == 0
    c0 = col0 // w
    bsx = jnp.repeat(b_s.T, hd, axis=1)
    once = pl.Buffered(1)
    return pl.pallas_call(
        _sgu_kernel,
        grid=(t // tm,),
        in_specs=[pl.BlockSpec((tm, d), lambda i: (i, 0)),
                  pl.BlockSpec((1, d), lambda i: (0, 0)),
                  pl.BlockSpec((d, w), lambda i: (0, c0), pipeline_mode=once),
                  pl.BlockSpec((d, w), lambda i: (0, c0 + 1), pipeline_mode=once),
                  pl.BlockSpec((1, w), lambda i: (0, 0)),
                  pl.BlockSpec((1, w), lambda i: (0, 0)),
                  pl.BlockSpec((SGU_HEADS, SGU_BLOCK, SGU_BLOCK), lambda i: (0, 0, 0)),
                  pl.BlockSpec((SGU_BLOCK, w), lambda i: (0, 0))],
        out_specs=[pl.BlockSpec((tm, d), lambda i: (i, 0)),
                   pl.BlockSpec((tm, w), lambda i: (i, 0))],
        out_shape=[jax.ShapeDtypeStruct((t, d), BF16),
                   jax.ShapeDtypeStruct((t, w), BF16)],
        scratch_shapes=[pltpu.VMEM((d, 2 * w), BF16)],
        compiler_params=_params(1),
        name="sgu_branch",
    )(x, g.reshape(1, d), w_in, w_in, ln_g.reshape(1, w), ln_b.reshape(1, w), w_s, bsx)


def _upgate_kernel(xn_ref, ya_ref, yb_ref, wgc_ref, wgs_ref, wua_ref, wub_ref, eg_ref, eu_ref, ed_ref,
                   o_ref, eg_bf_ref, eu_bf_ref, ed_bf_ref):
    eg_bf_ref[...] = eg_ref[...].astype(BF16)
    eu_bf_ref[...] = eu_ref[...].astype(BF16)
    ed_bf_ref[...] = ed_ref[...].astype(BF16)

    xn = xn_ref[...]
    m = (jax.nn.sigmoid(_dot(xn, wgc_ref[...])) * _dot(ya_ref[...], wua_ref[...])
         + jax.nn.sigmoid(_dot(xn, wgs_ref[...])) * _dot(yb_ref[...], wub_ref[...]))
    o_ref[...] = m.astype(o_ref.dtype)


def _upgate(xn, ya, yb, w_gates, w_up_a, w_up_b, w_eg, w_eu, w_ed, tm=1024, tn=512):
    t, d = xn.shape
    wa = ya.shape[1]
    wb = yb.shape[1]
    dout = w_up_a.shape[1]
    c0 = 0
    nj = dout // tn
    ni = t // tm
    n_e, d_e, f_e = w_eg.shape
    up_rows = n_e * d_e // (nj * ni)
    down_rows = n_e * f_e // (nj * ni)
    assert up_rows * nj * ni == n_e * d_e and up_rows % 16 == 0
    assert down_rows * nj * ni == n_e * f_e and down_rows % 16 == 0
    up_spec = pl.BlockSpec((up_rows, f_e), lambda j, i: (j * ni + i, 0))
    down_spec = pl.BlockSpec((down_rows, d_e), lambda j, i: (j * ni + i, 0))
    m, eg_bf, eu_bf, ed_bf = pl.pallas_call(
        _upgate_kernel,
        grid=(nj, ni),
        in_specs=[pl.BlockSpec((tm, d), lambda j, i: (i, 0)),
                  pl.BlockSpec((tm, wa), lambda j, i: (i, 0)),
                  pl.BlockSpec((tm, wb), lambda j, i: (i, 0)),
                  pl.BlockSpec((d, tn), lambda j, i: (0, c0 + j)),
                  pl.BlockSpec((d, tn), lambda j, i: (0, c0 + nj + j)),
                  pl.BlockSpec((wa, tn), lambda j, i: (0, j)),
                  pl.BlockSpec((wb, tn), lambda j, i: (0, j)),
                  up_spec, up_spec, down_spec],
        out_specs=[pl.BlockSpec((tm, tn), lambda j, i: (i, j)), up_spec, up_spec, down_spec],
        out_shape=[jax.ShapeDtypeStruct((t, dout), BF16),
                   jax.ShapeDtypeStruct((n_e * d_e, f_e), BF16),
                   jax.ShapeDtypeStruct((n_e * d_e, f_e), BF16),
                   jax.ShapeDtypeStruct((n_e * f_e, d_e), BF16)],
        compiler_params=_params(2),
        name="upgate",
    )(xn, ya, yb, w_gates, w_gates, w_up_a, w_up_b,
      w_eg.reshape(n_e * d_e, f_e), w_eu.reshape(n_e * d_e, f_e), w_ed.reshape(n_e * f_e, d_e))
    return (m, eg_bf.reshape(n_e, d_e, f_e), eu_bf.reshape(n_e, d_e, f_e),
            ed_bf.reshape(n_e, f_e, d_e))


def _outproj_kernel(m_ref, w_ref, x_ref, o_ref, mbuf_ref, xbuf_ref, sem):
    n = pl.num_programs(0)
    slot = _ring_fetch((m_ref, x_ref), (mbuf_ref, xbuf_ref), sem, pl.program_id(0), n, n)
    o_ref[...] = xbuf_ref[slot] + _dot(mbuf_ref[slot], w_ref[...])


def _outproj(m, w_out_bf, x, tm=512):
    t, d = m.shape
    dout = w_out_bf.shape[1]
    assert t // tm >= RING - 1
    any_spec = pl.BlockSpec(memory_space=pl.ANY)
    return pl.pallas_call(
        _outproj_kernel,
        grid=(t // tm,),
        in_specs=[any_spec,
                  pl.BlockSpec((d, dout), lambda i: (0, 0), pipeline_mode=pl.Buffered(1)),
                  any_spec],
        out_specs=pl.BlockSpec((tm, dout), lambda i: (i, 0)),
        out_shape=jax.ShapeDtypeStruct((t, dout), F32),
        scratch_shapes=[pltpu.VMEM((RING, tm, d), BF16),
                        pltpu.VMEM((RING, tm, dout), F32),
                        pltpu.SemaphoreType.DMA((RING,))],
        compiler_params=_params(1),
        name="outproj",
    )(m, w_out_bf, x)


def _argmax_rows(rows):
    best = rows[0]
    idx = jnp.zeros(rows[0].shape, jnp.int32)
    for k in range(1, len(rows)):
        better = rows[k] > best
        best = jnp.where(better, rows[k], best)
        idx = jnp.where(better, k, idx)
    return best, idx


def _softmax_rows(rows):
    mx = functools.reduce(jnp.maximum, rows)
    ex = [jnp.exp(r - mx) for r in rows]
    den = functools.reduce(lambda a, b: a + b, ex)
    return [e / den for e in ex]


def _route_sort_kernel(h_ref, g_ref, wr_ref, br_ref, xs_ref, meta_ref, cnt_ref, before_ref):
    tm = h_ref.shape[0]
    xn = _rms_scale(h_ref[...], g_ref[...])
    xn_hi = xn.astype(BF16)
    xn_lo = (xn - xn_hi.astype(F32)).astype(BF16)
    wr = wr_ref[...]
    wr_hi = wr.astype(BF16)
    wr_lo = (wr - wr_hi.astype(F32)).astype(BF16)
    nt_dims = (((1,), (1,)), ((), ()))
    lt = (lax.dot_general(wr_hi, xn_hi, nt_dims, preferred_element_type=F32)
          + lax.dot_general(wr_hi, xn_lo, nt_dims, preferred_element_type=F32)
          + lax.dot_general(wr_lo, xn_hi, nt_dims, preferred_element_type=F32)) + br_ref[...]
    pgs = _softmax_rows([lt[k:k + 1, :] for k in range(N_GROUPS)])
    pg, gi = _argmax_rows(pgs)
    sel = []
    for k in range(EXPERTS_PER_GROUP):
        r = jnp.zeros_like(pg)
        for g in range(N_GROUPS):
            row = N_GROUPS + g * EXPERTS_PER_GROUP + k
            r = jnp.where(gi == g, lt[row:row + 1, :], r)
        sel.append(r)
    pes = _softmax_rows(sel)
    p1, e1 = _argmax_rows(pes)
    rest = [jnp.where(e1 == k, -1.0, pes[k]) for k in range(EXPERTS_PER_GROUP)]
    p2, e2 = _argmax_rows(rest)
    den = p1 + p2
    w1 = pg * (p1 / den)
    w2 = pg * (p2 / den)
    lo = jnp.minimum(e1, e2)
    hi = jnp.maximum(e1, e2)
    w_lo = jnp.where(e1 < e2, w1, w2)
    w_hi = jnp.where(e1 < e2, w2, w1)
    ea = gi * EXPERTS_PER_GROUP + lo
    eb = gi * EXPERTS_PER_GROUP + hi

    erow = lax.broadcasted_iota(jnp.int32, (N_EXPERTS, tm), 0)
    oh_a = (erow == ea).astype(F32)
    oh_b = (erow == eb).astype(F32)

    @pl.when(pl.program_id(0) == 0)
    def _():
        a = lax.broadcasted_iota(jnp.int32, (tm, tm), 0)
        b = lax.broadcasted_iota(jnp.int32, (tm, tm), 1)
        before_ref[...] = (a < b).astype(BF16)

    cum = _dot((oh_a + oh_b).astype(BF16), before_ref[...])
    cnt = jnp.sum(oh_a + oh_b, axis=1, keepdims=True)
    padded = jnp.floor((cnt + (GROUP - 1)) * (1.0 / GROUP)) * GROUP
    pos_a = jnp.sum(oh_a * cum + jnp.where(erow < ea, padded, 0.0), axis=0, keepdims=True)
    pos_b = jnp.sum(oh_b * cum + jnp.where(erow < eb, padded, 0.0), axis=0, keepdims=True)

    d = h_ref.shape[1]

    def gate_rows(w):
        hi = w.astype(BF16).astype(F32)
        lo = w - hi
        k = lax.broadcasted_iota(jnp.int32, (LANES, tm), 0)
        return jnp.where(k == 0, hi, jnp.where(k == 1, lo, 0.0)).astype(BF16)

    gate_a = gate_rows(w_lo)
    gate_b = gate_rows(w_hi)

    def sort_rows(r0, n):
        q = r0 + lax.broadcasted_iota(jnp.int32, (n, tm), 0)
        perm_a = jnp.where(q == pos_a.astype(jnp.int32), 1.0, 0.0).astype(BF16)
        perm_b = jnp.where(q == pos_b.astype(jnp.int32), 1.0, 0.0).astype(BF16)
        xs_ref[r0:r0 + n, 0:d] = _dot(perm_a + perm_b, xn_hi).astype(xs_ref.dtype)
        gates = (lax.dot_general(perm_a, gate_a, nt_dims, preferred_element_type=F32)
                 + lax.dot_general(perm_b, gate_b, nt_dims, preferred_element_type=F32))
        xs_ref[r0:r0 + n, d:d + LANES] = gates.astype(xs_ref.dtype)

    n_rows = xs_ref.shape[0]
    head = n_rows - HALF_ROWS
    used = jnp.sum(padded)
    sort_rows(0, head)

    @pl.when(used > head)
    def _():
        sort_rows(head, HALF_ROWS)

    @pl.when(used <= head)
    def _():
        xs_ref[head:n_rows, :] = jnp.zeros((HALF_ROWS, xs_ref.shape[1]), xs_ref.dtype)

    cnt_ref[...] = jnp.broadcast_to(cnt, cnt_ref.shape)
    meta_ref[0:1, :] = pos_a
    meta_ref[1:2, :] = pos_b
    meta_ref[2:8, :] = jnp.zeros((6, tm), F32)


def _route_sort(h1, g, w_rg, b_rg, w_re, b_re):
    t, d = h1.shape
    tm = ROUTE_BLOCK
    nb = t // tm
    n_log = w_rg.shape[1] + w_re.shape[1]
    wr = jnp.concatenate([w_rg, w_re], axis=1).T
    wr = jnp.pad(wr, ((0, ROUTER_ROWS - n_log), (0, 0)))
    br = jnp.pad(jnp.concatenate([b_rg, b_re]), (0, ROUTER_ROWS - n_log)).reshape(ROUTER_ROWS, 1)
    return pl.pallas_call(
        _route_sort_kernel,
        grid=(nb,),
        in_specs=[pl.BlockSpec((tm, d), lambda i: (i, 0)),
                  pl.BlockSpec((1, d), lambda i: (0, 0)),
                  pl.BlockSpec((ROUTER_ROWS, d), lambda i: (0, 0)),
                  pl.BlockSpec((ROUTER_ROWS, 1), lambda i: (0, 0))],
        out_specs=[pl.BlockSpec((LOCAL_ROWS, d + LANES), lambda i: (i, 0)),
                   pl.BlockSpec((8, tm), lambda i: (0, i)),
                   pl.BlockSpec((N_EXPERTS, LANES), lambda i: (i, 0))],
        out_shape=[jax.ShapeDtypeStruct((nb * LOCAL_ROWS, d + LANES), BF16),
                   jax.ShapeDtypeStruct((8, t), F32),
                   jax.ShapeDtypeStruct((nb * N_EXPERTS, LANES), F32)],
        scratch_shapes=[pltpu.VMEM((tm, tm), BF16)],
        compiler_params=_params(1),
        name="route_sort",
    )(h1, g.reshape(1, d), wr, br)


def _group_copy(src_ref, s_group, dst_ref, d_group, sem):
    return pltpu.make_async_copy(src_ref.at[pl.ds(pl.multiple_of(s_group * GROUP, GROUP), GROUP), :],
                                 dst_ref.at[pl.ds(pl.multiple_of(d_group * GROUP, GROUP), GROUP), :], sem)


def _expert_kernel(te_ref, nxt_ref, nreal_ref, nt_ref, gsrc_ref, gdst_ref, tail_ref,
                   xs_ref, wg_ref, wu_ref, wd_ref, ys_ref,
                   xbuf_ref, ybuf_ref, zero_ref, wgu_buf_ref, wd_buf_ref,
                   cur_ref, gsem, ssem, zsem, wsem):
    r = pl.program_id(0)
    nt = nt_ref[0]

    def gather(q, slot):
        for i in range(TILE_GROUPS):
            _group_copy(xs_ref, gsrc_ref[q * TILE_GROUPS + i], xbuf_ref, slot * TILE_GROUPS + i,
                        gsem.at[slot]).start(priority=1)

    def weight_copies(e, wslot):
        f = wg_ref.shape[2]
        return (pltpu.make_async_copy(wg_ref.at[e], wgu_buf_ref.at[wslot, :, pl.ds(0, f)], wsem.at[wslot]),
                pltpu.make_async_copy(wu_ref.at[e], wgu_buf_ref.at[wslot, :, pl.ds(f, f)], wsem.at[wslot]),
                pltpu.make_async_copy(wd_ref.at[e], wd_buf_ref.at[wslot], wsem.at[wslot]))

    def tile_wait(src, dst, sem):
        pltpu.make_async_copy(src.at[pl.ds(0, MOE_TILE), :], dst.at[pl.ds(0, MOE_TILE), :], sem).wait()

    @pl.when(r == 0)
    def _():
        cur_ref[0] = -1
        cur_ref[1] = -1
        for cp in weight_copies(te_ref[0], 0):
            cp.start()
        gather(0, 0)
        zero_ref[...] = jnp.zeros_like(zero_ref)
        ybuf_ref[...] = jnp.zeros_like(ybuf_ref)
        n_blocks = tail_ref.shape[0]

        def fill(make):
            def body(g, c):
                make(g)
                return c
            return body

        for blk in range(n_blocks):
            lax.fori_loop(tail_ref[blk], GROUPS_PER_BLOCK, fill(
                lambda g, blk=blk: _group_copy(zero_ref, 0, ys_ref, blk * GROUPS_PER_BLOCK + g, zsem).start()), 0)
        for blk in range(n_blocks):
            lax.fori_loop(tail_ref[blk], GROUPS_PER_BLOCK, fill(
                lambda g, blk=blk: _group_copy(zero_ref, 0, ys_ref, blk * GROUPS_PER_BLOCK + g, zsem).wait()), 0)

    @pl.when(r < nt)
    def _():
        e = te_ref[r]
        slot = r % 2

        gather(jnp.minimum(r + 1, nt - 1), 1 - slot)

        @pl.when(cur_ref[0] != e)
        def _():
            wslot = (cur_ref[1] + 1) % 2
            for cp in weight_copies(e, wslot):
                cp.wait()
            cur_ref[0] = e
            cur_ref[1] = cur_ref[1] + 1

            @pl.when(nxt_ref[r] >= 0)
            def _():
                for cp in weight_copies(nxt_ref[r], 1 - wslot):
                    cp.start()

        wslot = cur_ref[1] % 2
        tile_wait(xs_ref, xbuf_ref, gsem.at[slot])

        @pl.when(r >= 2)
        def _():
            tile_wait(ybuf_ref, ys_ref, ssem.at[slot])

        d = ybuf_ref.shape[1]
        f = wd_buf_ref.shape[1]
        def chain(row0, n_rows):
            rows = pl.ds(pl.multiple_of(slot * MOE_TILE + row0, HALF_ROWS), n_rows)
            x = xbuf_ref[rows, 0:d]
            gate_parts = xbuf_ref[rows, d:d + LANES].astype(F32)
            gate = gate_parts[:, 0:1] + gate_parts[:, 1:2]
            gu = _dot(x, wgu_buf_ref[wslot])
            hid = (jax.nn.silu(gu[:, 0:f]) * gu[:, f:2 * f]).astype(BF16)
            ybuf_ref[rows, :] = (_dot(hid, wd_buf_ref[wslot]) * gate).astype(ybuf_ref.dtype)

        half_groups = HALF_ROWS // GROUP
        n_halves = (nreal_ref[r] + half_groups - 1) // half_groups
        for count in range(1, MOE_TILE // HALF_ROWS + 1):
            @pl.when(n_halves == count)
            def _(count=count):
                for part in range(count // 2):
                    chain(part * MXU_ROWS, MXU_ROWS)
                if count % 2:
                    chain((count // 2) * MXU_ROWS, HALF_ROWS)

        for i in range(TILE_GROUPS):
            _group_copy(ybuf_ref, slot * TILE_GROUPS + i, ys_ref, gdst_ref[r * TILE_GROUPS + i],
                        ssem.at[slot]).start(priority=1)

        @pl.when(r == nt - 1)
        def _():
            tile_wait(xs_ref, xbuf_ref, gsem.at[1 - slot])
            tile_wait(ybuf_ref, ys_ref, ssem.at[slot])

            @pl.when(r >= 1)
            def _():
                tile_wait(ybuf_ref, ys_ref, ssem.at[1 - slot])


def _experts(plan, xs, w_gate, w_up, w_down):
    d = w_gate.shape[1]
    f = w_gate.shape[2]
    tile_e, next_e, n_real, n_tiles, gsrc, gdst, tail = plan
    r_max = tile_e.shape[0]
    n_blocks = tail.shape[0]
    any_spec = pl.BlockSpec(memory_space=pl.ANY)
    grid_spec = pltpu.PrefetchScalarGridSpec(
        num_scalar_prefetch=7,
        grid=(r_max,),
        in_specs=[any_spec, any_spec, any_spec, any_spec],
        out_specs=any_spec,
        scratch_shapes=[pltpu.VMEM((2 * MOE_TILE, d + LANES), BF16),
                        pltpu.VMEM((2 * MOE_TILE, d), BF16),
                        pltpu.VMEM((GROUP, d), BF16),
                        pltpu.VMEM((2, d, 2 * f), BF16),
                        pltpu.VMEM((2, f, d), BF16),
                        pltpu.SMEM((2,), jnp.int32),
                        pltpu.SemaphoreType.DMA((2,)),
                        pltpu.SemaphoreType.DMA((2,)),
                        pltpu.SemaphoreType.DMA(()),
                        pltpu.SemaphoreType.DMA((2,))],
    )
    return pl.pallas_call(
        _expert_kernel,
        grid_spec=grid_spec,
        out_shape=jax.ShapeDtypeStruct((n_blocks * LOCAL_ROWS, d), BF16),
        compiler_params=_params(1),
        name="experts",
    )(tile_e, next_e, n_real, n_tiles, gsrc, gdst, tail, xs, w_gate, w_up, w_down)


def _combine_kernel(ys_ref, h_ref, cm_ref, g_ref, o_ref, ybuf_ref, hbuf_ref, sem):
    n = pl.num_programs(0)
    tm = o_ref.shape[0]
    rows = ybuf_ref.shape[1]
    slot = _ring_fetch((ys_ref, h_ref), (ybuf_ref, hbuf_ref), sem, pl.program_id(0), n, n)
    cm = cm_ref[...]
    q = lax.broadcasted_iota(jnp.int32, (tm, rows), 1)
    sel = jnp.where((q == cm[:, 0:1].astype(jnp.int32)) | (q == cm[:, 1:2].astype(jnp.int32)), 1.0, 0.0)
    h2 = hbuf_ref[slot] + _dot(sel.astype(BF16), ybuf_ref[slot])
    o_ref[...] = _rms_scale(h2, g_ref[...])


def _combine(ys, h1, cmeta, g):
    t, d = h1.shape
    tm = ROUTE_BLOCK
    assert t // tm >= RING - 1
    any_spec = pl.BlockSpec(memory_space=pl.ANY)
    return pl.pallas_call(
        _combine_kernel,
        grid=(t // tm,),
        in_specs=[any_spec, any_spec,
                  pl.BlockSpec((tm, LANES), lambda i: (i, 0)),
                  pl.BlockSpec((1, d), lambda i: (0, 0))],
        out_specs=pl.BlockSpec((tm, d), lambda i: (i, 0)),
        out_shape=jax.ShapeDtypeStruct((t, d), F32),
        scratch_shapes=[pltpu.VMEM((RING, LOCAL_ROWS, d), BF16),
                        pltpu.VMEM((RING, tm, d), F32),
                        pltpu.SemaphoreType.DMA((RING,))],
        compiler_params=_params(1),
        name="combine",
    )(ys, h1, cmeta, g.reshape(1, d))


def _expert_plan(counts, n_blocks):
    cnt = counts[:, 0].astype(jnp.int32).reshape(n_blocks, N_EXPERTS)
    groups = (cnt + GROUP - 1) // GROUP
    first = jnp.cumsum(groups, axis=1) - groups
    upto = jnp.cumsum(groups, axis=0)
    per_expert = upto[-1]
    tiles_e = (per_expert + TILE_GROUPS - 1) // TILE_GROUPS
    tile_end = jnp.cumsum(tiles_e)
    n_tiles = tile_end[-1]
    max_groups = 2 * ROUTE_BLOCK * n_blocks // GROUP + n_blocks * N_EXPERTS
    r_max = max_groups // TILE_GROUPS + N_EXPERTS
    tile_ids = jnp.arange(r_max, dtype=jnp.int32)
    tile = jnp.minimum(tile_ids, n_tiles - 1)
    tile_e = jnp.sum((tile_end[None, :] <= tile[:, None]).astype(jnp.int32), axis=1)
    later = (tile_e[None, :] > tile_e[:, None]) & (tile_ids[None, :] < n_tiles)
    next_e = jnp.min(jnp.where(later, tile_e[None, :], N_EXPERTS), axis=1)
    next_e = jnp.where(next_e == N_EXPERTS, -1, next_e)

    slot = jnp.arange(r_max * TILE_GROUPS, dtype=jnp.int32)
    s_tile = slot // TILE_GROUPS
    oh_e = jnp.repeat(tile_e, TILE_GROUPS)[:, None] == jnp.arange(N_EXPERTS, dtype=jnp.int32)[None, :]

    def by_expert(table):
        return jnp.sum(jnp.where(oh_e[:, None, :], table[None], 0), axis=-1)

    k = slot - by_expert(((tile_end - tiles_e) * TILE_GROUPS)[None, :])[:, 0]
    real = (k < by_expert(per_expert[None, :])[:, 0]) & (s_tile < n_tiles)
    upto_e = by_expert(upto)
    blk = jnp.minimum(jnp.sum((upto_e <= k[:, None]).astype(jnp.int32), axis=1), n_blocks - 1)
    oh_b = blk[:, None] == jnp.arange(n_blocks, dtype=jnp.int32)[None, :]

    def by_block(table_se):
        return jnp.sum(jnp.where(oh_b, table_se, 0), axis=1)

    before = by_block(upto_e - by_expert(groups))
    src = blk * GROUPS_PER_BLOCK + by_block(by_expert(first)) + (k - before)
    zero_group = GROUPS_PER_BLOCK - 1
    spare = n_blocks * GROUPS_PER_BLOCK + (s_tile % 2) * TILE_GROUPS + slot % TILE_GROUPS
    gsrc = jnp.where(real, src, zero_group)
    gdst = jnp.where(real, src, spare)
    n_spare_blocks = -(-2 * TILE_GROUPS // GROUPS_PER_BLOCK)
    tail = jnp.concatenate([jnp.sum(groups, axis=1), jnp.zeros((n_spare_blocks,), jnp.int32)])
    n_real = jnp.sum(real.reshape(r_max, TILE_GROUPS).astype(jnp.int32), axis=1)
    return tile_e, next_e, n_real, n_tiles.reshape(1), gsrc, gdst, tail


def _layer(h, norm_mix_g, w_in, conv_w, sgu_ln_g, sgu_ln_b, sgu_w_s, sgu_b_s, w_up_conv,
           w_up_sgu, w_out, norm_ffn_g, w_rg, b_rg, w_re, b_re, w_eg, w_eu, w_ed, out_g):
    t, d = h.shape
    conv_width = conv_w.shape[1]
    sgu_width = sgu_ln_g.shape[0]
    xn, yb = _sgu_branch(h, norm_mix_g, w_in, 3 * conv_width, sgu_ln_g, sgu_ln_b, sgu_w_s, sgu_b_s)
    gate_col0 = 3 * conv_width + 2 * sgu_width
    ya, (wg_bf, wua_bf, wub_bf, wout_bf) = _conv_branch(
        xn, w_in, conv_w, conv_width,
        to_cast=[(w_in, gate_col0, w_in.shape[1] - gate_col0, math.gcd(gate_col0, w_in.shape[1])),
                 (w_up_conv, 0, d, d), (w_up_sgu, 0, d, d), (w_out, 0, d, d)])
    m, eg_bf, eu_bf, ed_bf = _upgate(xn, ya, yb, wg_bf, wua_bf, wub_bf, w_eg, w_eu, w_ed)
    h1 = _outproj(m, wout_bf, h)
    xs, meta, counts = _route_sort(h1, norm_ffn_g, w_rg, b_rg, w_re, b_re)
    ys = _experts(_expert_plan(counts, t // ROUTE_BLOCK), xs, eg_bf, eu_bf, ed_bf)
    cmeta = jnp.pad(meta[0:2].T, ((0, 0), (0, LANES - 2)))
    return _combine(ys, h1, cmeta, out_g)


def kernel(x, norm_mix_g, w_in, conv_w, sgu_ln_g, sgu_ln_b, sgu_w_s, sgu_b_s, w_up_conv, w_up_sgu, w_out, norm_ffn_g, w_router_group, b_router_group, w_router_expert, b_router_expert, w_exp_gate, w_exp_up, w_exp_down, norm_final_g):
    bsz, s, d = x.shape
    depth = w_in.shape[0]
    assert bsz == 1 and depth == 1, "causal conv carry and the fused final norm assume one sequence, one layer"
    assert s % ROUTE_BLOCK == 0
    out = _layer(x.reshape(s, d), norm_mix_g[0], w_in[0], conv_w[0], sgu_ln_g[0], sgu_ln_b[0],
                 sgu_w_s[0], sgu_b_s[0], w_up_conv[0], w_up_sgu[0], w_out[0], norm_ffn_g[0],
                 w_router_group[0], b_router_group[0], w_router_expert[0], b_router_expert[0],
                 w_exp_gate[0], w_exp_up[0], w_exp_down[0], norm_final_g)
    return out.reshape(bsz, s, d)
```

```python
import functools
import math

import jax
import jax.numpy as jnp
from jax import lax
from jax.experimental import pallas as pl
from jax.experimental.pallas import tpu as pltpu

F32 = jnp.float32
BF16 = jnp.bfloat16

EPS = 1e-6
CHUNK = 64
CONV_K = 3
SGU_HEADS = 8
SGU_BLOCK = 128
N_GROUPS = 4
EXPERTS_PER_GROUP = 4
N_EXPERTS = N_GROUPS * EXPERTS_PER_GROUP
ROUTER_ROWS = 32
LANES = 128

VMEM_LIMIT_BYTES = 56 * 1024 * 1024

ROUTE_BLOCK = 512
GROUP = 16
MXU_ROWS = 256
HALF_ROWS = MXU_ROWS // 2
LOCAL_ROWS = -(-(2 * ROUTE_BLOCK + N_EXPERTS * (GROUP - 1)) // MXU_ROWS) * MXU_ROWS
GROUPS_PER_BLOCK = LOCAL_ROWS // GROUP
assert LOCAL_ROWS - (2 * ROUTE_BLOCK + N_EXPERTS * (GROUP - 1)) >= GROUP
MOE_TILE = 512
TILE_GROUPS = MOE_TILE // GROUP


def _params(n_axes):
    return pltpu.CompilerParams(
        dimension_semantics=("arbitrary",) * n_axes,
        vmem_limit_bytes=VMEM_LIMIT_BYTES)


def _dot(a, b):
    return jnp.dot(a, b, preferred_element_type=F32)


def _rms_scale(x, g):
    ms = jnp.mean(x * x, axis=-1, keepdims=True)
    return x * lax.rsqrt(ms + EPS) * g


RING = 3


def _ring_fetch(srcs, bufs, sem, step, n_steps, n_blocks):
    def copies(s):
        blk = s % n_blocks
        slot = s % RING
        return [pltpu.make_async_copy(
            src.at[pl.ds(pl.multiple_of(blk * buf.shape[1], buf.shape[1]), buf.shape[1]), :],
            buf.at[slot], sem.at[slot]) for src, buf in zip(srcs, bufs)]

    @pl.when(step == 0)
    def _():
        for s in range(RING - 1):
            for cp in copies(s):
                cp.start()

    @pl.when(step + RING - 1 < n_steps)
    def _():
        for cp in copies(step + RING - 1):
            cp.start()

    for cp in copies(step):
        cp.wait()
    return step % RING


def _conv_kernel(cast_chunks, xn_ref, wb_ref, wc_ref, wh_ref, cw_ref, *rest):
    n_in = sum(cast_chunks)
    cast_in, o_ref = rest[:n_in], rest[n_in]
    cast_out = rest[n_in + 1:n_in + 1 + len(cast_chunks)]
    wbf_ref, carry_ref, xnbuf_ref, xsem = rest[n_in + 1 + len(cast_chunks):]
    i = pl.program_id(1)
    ni = pl.num_programs(1)
    tn = wb_ref.shape[1]
    tm = xnbuf_ref.shape[1]
    slot = _ring_fetch((xn_ref,), (xnbuf_ref,), xsem, pl.program_id(0) * ni + i,
                       pl.num_programs(0) * ni, ni)
    src = iter(cast_in)
    for dst, n_chunks in zip(cast_out, cast_chunks):
        wc = dst.shape[1] // n_chunks
        for k in range(n_chunks):
            dst[:, k * wc:(k + 1) * wc] = next(src)[...].astype(BF16)

    @pl.when(i == 0)
    def _():
        wbf_ref[:, 0:tn] = wb_ref[...].astype(BF16)
        wbf_ref[:, tn:2 * tn] = wc_ref[...].astype(BF16)
        wbf_ref[:, 2 * tn:3 * tn] = wh_ref[...].astype(BF16)
        carry_ref[...] = jnp.zeros_like(carry_ref)

    proj = _dot(xnbuf_ref[slot], wbf_ref[...])
    b = proj[:, 0:tn]
    p = proj[:, tn:2 * tn] * proj[:, 2 * tn:3 * tn]
    prev = carry_ref[...]
    carry_ref[...] = p[tm - 8:tm, :]
    row = lax.broadcasted_iota(jnp.int32, p.shape, 0)
    p1 = jnp.where(row == 0, prev[7:8, :], pltpu.roll(p, 1, axis=0))
    p2 = jnp.where(row == 0, prev[6:7, :],
                   jnp.where(row == 1, prev[7:8, :], pltpu.roll(p, 2, axis=0)))
    cw = cw_ref[...]
    y = b * (cw[0:1, :] * p2 + cw[1:2, :] * p1 + cw[2:3, :] * p)
    o_ref[...] = y.astype(o_ref.dtype)


def _conv_branch(xn, w_in, conv_w, width, to_cast, tm=1024, tn=256):
    t, d = xn.shape
    nj = width // tn
    ni = t // tm
    steps = nj * ni
    cast_args, cast_in_specs, cast_out_specs, cast_shapes, cast_chunks = [], [], [], [], []
    for arr, col0, n_cols, chunk in to_cast:
        rows = arr.shape[0] // steps
        assert rows * steps == arr.shape[0] and rows % 16 == 0
        assert col0 % chunk == 0 and n_cols % chunk == 0
        for k in range(n_cols // chunk):
            cast_args.append(arr)
            cast_in_specs.append(
                pl.BlockSpec((rows, chunk), lambda j, i, c=col0 // chunk + k: (j * ni + i, c)))
        cast_chunks.append(n_cols // chunk)
        cast_out_specs.append(pl.BlockSpec((rows, n_cols), lambda j, i: (j * ni + i, 0)))
        cast_shapes.append(jax.ShapeDtypeStruct((arr.shape[0], n_cols), BF16))
    outs = pl.pallas_call(
        functools.partial(_conv_kernel, tuple(cast_chunks)),
        grid=(nj, ni),
        in_specs=[pl.BlockSpec(memory_space=pl.ANY),
                  pl.BlockSpec((d, tn), lambda j, i: (0, j)),
                  pl.BlockSpec((d, tn), lambda j, i: (0, nj + j)),
                  pl.BlockSpec((d, tn), lambda j, i: (0, 2 * nj + j)),
                  pl.BlockSpec((CONV_K, tn), lambda j, i: (0, j))] + cast_in_specs,
        out_specs=[pl.BlockSpec((tm, tn), lambda j, i: (i, j))] + cast_out_specs,
        out_shape=[jax.ShapeDtypeStruct((t, width), BF16)] + cast_shapes,
        scratch_shapes=[pltpu.VMEM((d, 3 * tn), BF16),
                        pltpu.VMEM((8, tn), F32),
                        pltpu.VMEM((RING, tm, d), BF16),
                        pltpu.SemaphoreType.DMA((RING,))],
        compiler_params=_params(2),
        name="conv_branch",
    )(xn, w_in, w_in, w_in, conv_w, *cast_args)
    return outs[0], outs[1:]


def _sgu_kernel(x_ref, g_ref, wu_ref, wv_ref, lng_ref, lnb_ref, ws_ref, bsx_ref, xn_ref, o_ref, wbf_ref):
    tm = x_ref.shape[0]
    w = o_ref.shape[1]
    hd = w // SGU_HEADS

    @pl.when(pl.program_id(0) == 0)
    def _():
        wbf_ref[:, 0:w] = wu_ref[...].astype(BF16)
        wbf_ref[:, w:2 * w] = wv_ref[...].astype(BF16)

    xn = _rms_scale(x_ref[...], g_ref[...]).astype(BF16)
    xn_ref[...] = xn
    gz = jax.nn.gelu(_dot(xn, wbf_ref[...]))
    v = gz[:, w:2 * w]
    mu = jnp.mean(v, axis=-1, keepdims=True)
    vc = v - mu
    var = jnp.mean(vc * vc, axis=-1, keepdims=True)
    vn = (vc * lax.rsqrt(var + EPS) * lng_ref[...] + lnb_ref[...]).astype(BF16)
    ii = lax.broadcasted_iota(jnp.int32, (SGU_BLOCK, SGU_BLOCK), 0)
    jj = lax.broadcasted_iota(jnp.int32, (SGU_BLOCK, SGU_BLOCK), 1)
    mask = (jj // CHUNK) <= (ii // CHUNK)
    for h in range(SGU_HEADS):
        wm = jnp.where(mask, ws_ref[h], 0.0).astype(BF16)
        cs = slice(h * hd, (h + 1) * hd)
        for n in range(tm // SGU_BLOCK):
            rs = slice(n * SGU_BLOCK, (n + 1) * SGU_BLOCK)
            vm = _dot(wm, vn[rs, cs]) + bsx_ref[:, cs]
            o_ref[rs, cs] = (gz[rs, cs] * vm).astype(o_ref.dtype)


def _sgu_branch(x, g, w_in, col0, ln_g, ln_b, w_s, b_s, tm=512):
    t, d = x.shape
    w = ln_g.shape[0]
    hd = w // SGU_HEADS
    assert col0 % w == 0
    c0 = col0 // w
    bsx = jnp.repeat(b_s.T, hd, axis=1)
    once = pl.Buffered(1)
    return pl.pallas_call(
        _sgu_kernel,
        grid=(t // tm,),
        in_specs=[pl.BlockSpec((tm, d), lambda i: (i, 0)),
                  pl.BlockSpec((1, d), lambda i: (0, 0)),
                  pl.BlockSpec((d, w), lambda i: (0, c0), pipeline_mode=once),
                  pl.BlockSpec((d, w), lambda i: (0, c0 + 1), pipeline_mode=once),
                  pl.BlockSpec((1, w), lambda i: (0, 0)),
                  pl.BlockSpec((1, w), lambda i: (0, 0)),
                  pl.BlockSpec((SGU_HEADS, SGU_BLOCK, SGU_BLOCK), lambda i: (0, 0, 0)),
                  pl.BlockSpec((SGU_BLOCK, w), lambda i: (0, 0))],
        out_specs=[pl.BlockSpec((tm, d), lambda i: (i, 0)),
                   pl.BlockSpec((tm, w), lambda i: (i, 0))],
        out_shape=[jax.ShapeDtypeStruct((t, d), BF16),
                   jax.ShapeDtypeStruct((t, w), BF16)],
        scratch_shapes=[pltpu.VMEM((d, 2 * w), BF16)],
        compiler_params=_params(1),
        name="sgu_branch",
    )(x, g.reshape(1, d), w_in, w_in, ln_g.reshape(1, w), ln_b.reshape(1, w), w_s, bsx)


def _upgate_kernel(xn_ref, ya_ref, yb_ref, wgc_ref, wgs_ref, wua_ref, wub_ref, eg_ref, eu_ref, ed_ref,
                   o_ref, eg_bf_ref, eu_bf_ref, ed_bf_ref):
    eg_bf_ref[...] = eg_ref[...].astype(BF16)
    eu_bf_ref[...] = eu_ref[...].astype(BF16)
    ed_bf_ref[...] = ed_ref[...].astype(BF16)

    xn = xn_ref[...]
    m = (jax.nn.sigmoid(_dot(xn, wgc_ref[...])) * _dot(ya_ref[...], wua_ref[...])
         + jax.nn.sigmoid(_dot(xn, wgs_ref[...])) * _dot(yb_ref[...], wub_ref[...]))
    o_ref[...] = m.astype(o_ref.dtype)


def _upgate(xn, ya, yb, w_gates, w_up_a, w_up_b, w_eg, w_eu, w_ed, tm=1024, tn=512):
    t, d = xn.shape
    wa = ya.shape[1]
    wb = yb.shape[1]
    dout = w_up_a.shape[1]
    c0 = 0
    nj = dout // tn
    ni = t // tm
    n_e, d_e, f_e = w_eg.shape
    up_rows = n_e * d_e // (nj * ni)
    down_rows = n_e * f_e // (nj * ni)
    assert up_rows * nj * ni == n_e * d_e and up_rows % 16 == 0
    assert down_rows * nj * ni == n_e * f_e and down_rows % 16 == 0
    up_spec = pl.BlockSpec((up_rows, f_e), lambda j, i: (j * ni + i, 0))
    down_spec = pl.BlockSpec((down_rows, d_e), lambda j, i: (j * ni + i, 0))
    m, eg_bf, eu_bf, ed_bf = pl.pallas_call(
        _upgate_kernel,
        grid=(nj, ni),
        in_specs=[pl.BlockSpec((tm, d), lambda j, i: (i, 0)),
                  pl.BlockSpec((tm, wa), lambda j, i: (i, 0)),
                  pl.BlockSpec((tm, wb), lambda j, i: (i, 0)),
                  pl.BlockSpec((d, tn), lambda j, i: (0, c0 + j)),
                  pl.BlockSpec((d, tn), lambda j, i: (0, c0 + nj + j)),
                  pl.BlockSpec((wa, tn), lambda j, i: (0, j)),
                  pl.BlockSpec((wb, tn), lambda j, i: (0, j)),
                  up_spec, up_spec, down_spec],
        out_specs=[pl.BlockSpec((tm, tn), lambda j, i: (i, j)), up_spec, up_spec, down_spec],
        out_shape=[jax.ShapeDtypeStruct((t, dout), BF16),
                   jax.ShapeDtypeStruct((n_e * d_e, f_e), BF16),
                   jax.ShapeDtypeStruct((n_e * d_e, f_e), BF16),
                   jax.ShapeDtypeStruct((n_e * f_e, d_e), BF16)],
        compiler_params=_params(2),
        name="upgate",
    )(xn, ya, yb, w_gates, w_gates, w_up_a, w_up_b,
      w_eg.reshape(n_e * d_e, f_e), w_eu.reshape(n_e * d_e, f_e), w_ed.reshape(n_e * f_e, d_e))
    return (m, eg_bf.reshape(n_e, d_e, f_e), eu_bf.reshape(n_e, d_e, f_e),
            ed_bf.reshape(n_e, f_e, d_e))


def _outproj_kernel(m_ref, w_ref, x_ref, o_ref):
    o_ref[...] = x_ref[...] + _dot(m_ref[...], w_ref[...])


def _outproj(m, w_out_bf, x, tm=512):
    t, d = m.shape
    dout = w_out_bf.shape[1]
    return pl.pallas_call(
        _outproj_kernel,
        grid=(t // tm,),
        in_specs=[pl.BlockSpec((tm, d), lambda i: (i, 0)),
                  pl.BlockSpec((d, dout), lambda i: (0, 0), pipeline_mode=pl.Buffered(1)),
                  pl.BlockSpec((tm, dout), lambda i: (i, 0))],
        out_specs=pl.BlockSpec((tm, dout), lambda i: (i, 0)),
        out_shape=jax.ShapeDtypeStruct((t, dout), F32),
        compiler_params=_params(1),
        name="outproj",
    )(m, w_out_bf, x)


def _argmax_rows(rows):
    best = rows[0]
    idx = jnp.zeros(rows[0].shape, jnp.int32)
    for k in range(1, len(rows)):
        better = rows[k] > best
        best = jnp.where(better, rows[k], best)
        idx = jnp.where(better, k, idx)
    return best, idx


def _softmax_rows(rows):
    mx = functools.reduce(jnp.maximum, rows)
    ex = [jnp.exp(r - mx) for r in rows]
    den = functools.reduce(lambda a, b: a + b, ex)
    return [e / den for e in ex]


def _route_sort_kernel(h_ref, g_ref, wr_ref, br_ref, xs_ref, meta_ref, cnt_ref, before_ref):
    tm = h_ref.shape[0]
    xn = _rms_scale(h_ref[...], g_ref[...])
    xn_hi = xn.astype(BF16)
    xn_lo = (xn - xn_hi.astype(F32)).astype(BF16)
    wr = wr_ref[...]
    wr_hi = wr.astype(BF16)
    wr_lo = (wr - wr_hi.astype(F32)).astype(BF16)
    nt_dims = (((1,), (1,)), ((), ()))
    lt = (lax.dot_general(wr_hi, xn_hi, nt_dims, preferred_element_type=F32)
          + lax.dot_general(wr_hi, xn_lo, nt_dims, preferred_element_type=F32)
          + lax.dot_general(wr_lo, xn_hi, nt_dims, preferred_element_type=F32)) + br_ref[...]
    pgs = _softmax_rows([lt[k:k + 1, :] for k in range(N_GROUPS)])
    pg, gi = _argmax_rows(pgs)
    sel = []
    for k in range(EXPERTS_PER_GROUP):
        r = jnp.zeros_like(pg)
        for g in range(N_GROUPS):
            row = N_GROUPS + g * EXPERTS_PER_GROUP + k
            r = jnp.where(gi == g, lt[row:row + 1, :], r)
        sel.append(r)
    pes = _softmax_rows(sel)
    p1, e1 = _argmax_rows(pes)
    rest = [jnp.where(e1 == k, -1.0, pes[k]) for k in range(EXPERTS_PER_GROUP)]
    p2, e2 = _argmax_rows(rest)
    den = p1 + p2
    w1 = pg * (p1 / den)
    w2 = pg * (p2 / den)
    lo = jnp.minimum(e1, e2)
    hi = jnp.maximum(e1, e2)
    w_lo = jnp.where(e1 < e2, w1, w2)
    w_hi = jnp.where(e1 < e2, w2, w1)
    ea = gi * EXPERTS_PER_GROUP + lo
    eb = gi * EXPERTS_PER_GROUP + hi

    erow = lax.broadcasted_iota(jnp.int32, (N_EXPERTS, tm), 0)
    oh_a = (erow == ea).astype(F32)
    oh_b = (erow == eb).astype(F32)

    @pl.when(pl.program_id(0) == 0)
    def _():
        a = lax.broadcasted_iota(jnp.int32, (tm, tm), 0)
        b = lax.broadcasted_iota(jnp.int32, (tm, tm), 1)
        before_ref[...] = (a < b).astype(BF16)

    cum = _dot((oh_a + oh_b).astype(BF16), before_ref[...])
    cnt = jnp.sum(oh_a + oh_b, axis=1, keepdims=True)
    padded = jnp.floor((cnt + (GROUP - 1)) * (1.0 / GROUP)) * GROUP
    pos_a = jnp.sum(oh_a * cum + jnp.where(erow < ea, padded, 0.0), axis=0, keepdims=True)
    pos_b = jnp.sum(oh_b * cum + jnp.where(erow < eb, padded, 0.0), axis=0, keepdims=True)

    d = h_ref.shape[1]

    def gate_rows(w):
        hi = w.astype(BF16).astype(F32)
        lo = w - hi
        k = lax.broadcasted_iota(jnp.int32, (LANES, tm), 0)
        return jnp.where(k == 0, hi, jnp.where(k == 1, lo, 0.0)).astype(BF16)

    gate_a = gate_rows(w_lo)
    gate_b = gate_rows(w_hi)

    def sort_rows(r0, n):
        q = r0 + lax.broadcasted_iota(jnp.int32, (n, tm), 0)
        perm_a = jnp.where(q == pos_a.astype(jnp.int32), 1.0, 0.0).astype(BF16)
        perm_b = jnp.where(q == pos_b.astype(jnp.int32), 1.0, 0.0).astype(BF16)
        xs_ref[r0:r0 + n, 0:d] = _dot(perm_a + perm_b, xn_hi).astype(xs_ref.dtype)
        gates = (lax.dot_general(perm_a, gate_a, nt_dims, preferred_element_type=F32)
                 + lax.dot_general(perm_b, gate_b, nt_dims, preferred_element_type=F32))
        xs_ref[r0:r0 + n, d:d + LANES] = gates.astype(xs_ref.dtype)

    n_rows = xs_ref.shape[0]
    head = n_rows - HALF_ROWS
    used = jnp.sum(padded)
    sort_rows(0, head)

    @pl.when(used > head)
    def _():
        sort_rows(head, HALF_ROWS)

    @pl.when(used <= head)
    def _():
        xs_ref[head:n_rows, :] = jnp.zeros((HALF_ROWS, xs_ref.shape[1]), xs_ref.dtype)

    cnt_ref[...] = jnp.broadcast_to(cnt, cnt_ref.shape)
    meta_ref[0:1, :] = pos_a
    meta_ref[1:2, :] = pos_b
    meta_ref[2:8, :] = jnp.zeros((6, tm), F32)


def _route_sort(h1, g, w_rg, b_rg, w_re, b_re):
    t, d = h1.shape
    tm = ROUTE_BLOCK
    nb = t // tm
    n_log = w_rg.shape[1] + w_re.shape[1]
    wr = jnp.concatenate([w_rg, w_re], axis=1).T
    wr = jnp.pad(wr, ((0, ROUTER_ROWS - n_log), (0, 0)))
    br = jnp.pad(jnp.concatenate([b_rg, b_re]), (0, ROUTER_ROWS - n_log)).reshape(ROUTER_ROWS, 1)
    return pl.pallas_call(
        _route_sort_kernel,
        grid=(nb,),
        in_specs=[pl.BlockSpec((tm, d), lambda i: (i, 0)),
                  pl.BlockSpec((1, d), lambda i: (0, 0)),
                  pl.BlockSpec((ROUTER_ROWS, d), lambda i: (0, 0)),
                  pl.BlockSpec((ROUTER_ROWS, 1), lambda i: (0, 0))],
        out_specs=[pl.BlockSpec((LOCAL_ROWS, d + LANES), lambda i: (i, 0)),
                   pl.BlockSpec((8, tm), lambda i: (0, i)),
                   pl.BlockSpec((N_EXPERTS, LANES), lambda i: (i, 0))],
        out_shape=[jax.ShapeDtypeStruct((nb * LOCAL_ROWS, d + LANES), BF16),
                   jax.ShapeDtypeStruct((8, t), F32),
                   jax.ShapeDtypeStruct((nb * N_EXPERTS, LANES), F32)],
        scratch_shapes=[pltpu.VMEM((tm, tm), BF16)],
        compiler_params=_params(1),
        name="route_sort",
    )(h1, g.reshape(1, d), wr, br)


def _group_copy(src_ref, s_group, dst_ref, d_group, sem):
    return pltpu.make_async_copy(src_ref.at[pl.ds(pl.multiple_of(s_group * GROUP, GROUP), GROUP), :],
                                 dst_ref.at[pl.ds(pl.multiple_of(d_group * GROUP, GROUP), GROUP), :], sem)


def _expert_kernel(te_ref, nxt_ref, nreal_ref, nt_ref, gsrc_ref, gdst_ref, tail_ref,
                   xs_ref, wg_ref, wu_ref, wd_ref, ys_ref,
                   xbuf_ref, ybuf_ref, zero_ref, wgu_buf_ref, wd_buf_ref,
                   cur_ref, gsem, ssem, zsem, wsem):
    r = pl.program_id(0)
    nt = nt_ref[0]

    def gather(q, slot):
        for i in range(TILE_GROUPS):
            _group_copy(xs_ref, gsrc_ref[q * TILE_GROUPS + i], xbuf_ref, slot * TILE_GROUPS + i,
                        gsem.at[slot]).start(priority=1)

    def weight_copies(e, wslot):
        f = wg_ref.shape[2]
        return (pltpu.make_async_copy(wg_ref.at[e], wgu_buf_ref.at[wslot, :, pl.ds(0, f)], wsem.at[wslot]),
                pltpu.make_async_copy(wu_ref.at[e], wgu_buf_ref.at[wslot, :, pl.ds(f, f)], wsem.at[wslot]),
                pltpu.make_async_copy(wd_ref.at[e], wd_buf_ref.at[wslot], wsem.at[wslot]))

    def tile_wait(src, dst, sem):
        pltpu.make_async_copy(src.at[pl.ds(0, MOE_TILE), :], dst.at[pl.ds(0, MOE_TILE), :], sem).wait()

    @pl.when(r == 0)
    def _():
        cur_ref[0] = -1
        cur_ref[1] = -1
        for cp in weight_copies(te_ref[0], 0):
            cp.start()
        gather(0, 0)
        zero_ref[...] = jnp.zeros_like(zero_ref)
        ybuf_ref[...] = jnp.zeros_like(ybuf_ref)
        n_blocks = tail_ref.shape[0]

        def fill(make):
            def body(g, c):
                make(g)
                return c
            return body

        for blk in range(n_blocks):
            lax.fori_loop(tail_ref[blk], GROUPS_PER_BLOCK, fill(
                lambda g, blk=blk: _group_copy(zero_ref, 0, ys_ref, blk * GROUPS_PER_BLOCK + g, zsem).start()), 0)
        for blk in range(n_blocks):
            lax.fori_loop(tail_ref[blk], GROUPS_PER_BLOCK, fill(
                lambda g, blk=blk: _group_copy(zero_ref, 0, ys_ref, blk * GROUPS_PER_BLOCK + g, zsem).wait()), 0)

    @pl.when(r < nt)
    def _():
        e = te_ref[r]
        slot = r % 2

        gather(jnp.minimum(r + 1, nt - 1), 1 - slot)

        @pl.when(cur_ref[0] != e)
        def _():
            wslot = (cur_ref[1] + 1) % 2
            for cp in weight_copies(e, wslot):
                cp.wait()
            cur_ref[0] = e
            cur_ref[1] = cur_ref[1] + 1

            @pl.when(nxt_ref[r] >= 0)
            def _():
                for cp in weight_copies(nxt_ref[r], 1 - wslot):
                    cp.start()

        wslot = cur_ref[1] % 2
        tile_wait(xs_ref, xbuf_ref, gsem.at[slot])

        @pl.when(r >= 2)
        def _():
            tile_wait(ybuf_ref, ys_ref, ssem.at[slot])

        d = ybuf_ref.shape[1]
        f = wd_buf_ref.shape[1]
        def chain(row0, n_rows):
            rows = pl.ds(pl.multiple_of(slot * MOE_TILE + row0, HALF_ROWS), n_rows)
            x = xbuf_ref[rows, 0:d]
            gate_parts = xbuf_ref[rows, d:d + LANES].astype(F32)
            gate = gate_parts[:, 0:1] + gate_parts[:, 1:2]
            gu = _dot(x, wgu_buf_ref[wslot])
            hid = (jax.nn.silu(gu[:, 0:f]) * gu[:, f:2 * f]).astype(BF16)
            ybuf_ref[rows, :] = (_dot(hid, wd_buf_ref[wslot]) * gate).astype(ybuf_ref.dtype)

        half_groups = HALF_ROWS // GROUP
        n_halves = (nreal_ref[r] + half_groups - 1) // half_groups
        for count in range(1, MOE_TILE // HALF_ROWS + 1):
            @pl.when(n_halves == count)
            def _(count=count):
                for part in range(count // 2):
                    chain(part * MXU_ROWS, MXU_ROWS)
                if count % 2:
                    chain((count // 2) * MXU_ROWS, HALF_ROWS)

        for i in range(TILE_GROUPS):
            _group_copy(ybuf_ref, slot * TILE_GROUPS + i, ys_ref, gdst_ref[r * TILE_GROUPS + i],
                        ssem.at[slot]).start(priority=1)

        @pl.when(r == nt - 1)
        def _():
            tile_wait(xs_ref, xbuf_ref, gsem.at[1 - slot])
            tile_wait(ybuf_ref, ys_ref, ssem.at[slot])

            @pl.when(r >= 1)
            def _():
                tile_wait(ybuf_ref, ys_ref, ssem.at[1 - slot])


def _experts(plan, xs, w_gate, w_up, w_down):
    d = w_gate.shape[1]
    f = w_gate.shape[2]
    tile_e, next_e, n_real, n_tiles, gsrc, gdst, tail = plan
    r_max = tile_e.shape[0]
    n_blocks = tail.shape[0]
    any_spec = pl.BlockSpec(memory_space=pl.ANY)
    grid_spec = pltpu.PrefetchScalarGridSpec(
        num_scalar_prefetch=7,
        grid=(r_max,),
        in_specs=[any_spec, any_spec, any_spec, any_spec],
        out_specs=any_spec,
        scratch_shapes=[pltpu.VMEM((2 * MOE_TILE, d + LANES), BF16),
                        pltpu.VMEM((2 * MOE_TILE, d), BF16),
                        pltpu.VMEM((GROUP, d), BF16),
                        pltpu.VMEM((2, d, 2 * f), BF16),
                        pltpu.VMEM((2, f, d), BF16),
                        pltpu.SMEM((2,), jnp.int32),
                        pltpu.SemaphoreType.DMA((2,)),
                        pltpu.SemaphoreType.DMA((2,)),
                        pltpu.SemaphoreType.DMA(()),
                        pltpu.SemaphoreType.DMA((2,))],
    )
    return pl.pallas_call(
        _expert_kernel,
        grid_spec=grid_spec,
        out_shape=jax.ShapeDtypeStruct((n_blocks * LOCAL_ROWS, d), BF16),
        compiler_params=_params(1),
        name="experts",
    )(tile_e, next_e, n_real, n_tiles, gsrc, gdst, tail, xs, w_gate, w_up, w_down)


def _combine_kernel(ys_ref, h_ref, cm_ref, g_ref, o_ref, ybuf_ref, hbuf_ref, sem):
    n = pl.num_programs(0)
    tm = o_ref.shape[0]
    rows = ybuf_ref.shape[1]
    slot = _ring_fetch((ys_ref, h_ref), (ybuf_ref, hbuf_ref), sem, pl.program_id(0), n, n)
    cm = cm_ref[...]
    q = lax.broadcasted_iota(jnp.int32, (tm, rows), 1)
    sel = jnp.where((q == cm[:, 0:1].astype(jnp.int32)) | (q == cm[:, 1:2].astype(jnp.int32)), 1.0, 0.0)
    h2 = hbuf_ref[slot] + _dot(sel.astype(BF16), ybuf_ref[slot])
    o_ref[...] = _rms_scale(h2, g_ref[...])


def _combine(ys, h1, cmeta, g):
    t, d = h1.shape
    tm = ROUTE_BLOCK
    assert t // tm >= RING - 1
    any_spec = pl.BlockSpec(memory_space=pl.ANY)
    return pl.pallas_call(
        _combine_kernel,
        grid=(t // tm,),
        in_specs=[any_spec, any_spec,
                  pl.BlockSpec((tm, LANES), lambda i: (i, 0)),
                  pl.BlockSpec((1, d), lambda i: (0, 0))],
        out_specs=pl.BlockSpec((tm, d), lambda i: (i, 0)),
        out_shape=jax.ShapeDtypeStruct((t, d), F32),
        scratch_shapes=[pltpu.VMEM((RING, LOCAL_ROWS, d), BF16),
                        pltpu.VMEM((RING, tm, d), F32),
                        pltpu.SemaphoreType.DMA((RING,))],
        compiler_params=_params(1),
        name="combine",
    )(ys, h1, cmeta, g.reshape(1, d))


def _expert_plan(counts, n_blocks):
    cnt = counts[:, 0].astype(jnp.int32).reshape(n_blocks, N_EXPERTS)
    groups = (cnt + GROUP - 1) // GROUP
    first = jnp.cumsum(groups, axis=1) - groups
    upto = jnp.cumsum(groups, axis=0)
    per_expert = upto[-1]
    tiles_e = (per_expert + TILE_GROUPS - 1) // TILE_GROUPS
    tile_end = jnp.cumsum(tiles_e)
    n_tiles = tile_end[-1]
    max_groups = 2 * ROUTE_BLOCK * n_blocks // GROUP + n_blocks * N_EXPERTS
    r_max = max_groups // TILE_GROUPS + N_EXPERTS
    tile_ids = jnp.arange(r_max, dtype=jnp.int32)
    tile = jnp.minimum(tile_ids, n_tiles - 1)
    tile_e = jnp.sum((tile_end[None, :] <= tile[:, None]).astype(jnp.int32), axis=1)
    later = (tile_e[None, :] > tile_e[:, None]) & (tile_ids[None, :] < n_tiles)
    next_e = jnp.min(jnp.where(later, tile_e[None, :], N_EXPERTS), axis=1)
    next_e = jnp.where(next_e == N_EXPERTS, -1, next_e)

    slot = jnp.arange(r_max * TILE_GROUPS, dtype=jnp.int32)
    s_tile = slot // TILE_GROUPS
    oh_e = jnp.repeat(tile_e, TILE_GROUPS)[:, None] == jnp.arange(N_EXPERTS, dtype=jnp.int32)[None, :]

    def by_expert(table):
        return jnp.sum(jnp.where(oh_e[:, None, :], table[None], 0), axis=-1)

    k = slot - by_expert(((tile_end - tiles_e) * TILE_GROUPS)[None, :])[:, 0]
    real = (k < by_expert(per_expert[None, :])[:, 0]) & (s_tile < n_tiles)
    upto_e = by_expert(upto)
    blk = jnp.minimum(jnp.sum((upto_e <= k[:, None]).astype(jnp.int32), axis=1), n_blocks - 1)
    oh_b = blk[:, None] == jnp.arange(n_blocks, dtype=jnp.int32)[None, :]

    def by_block(table_se):
        return jnp.sum(jnp.where(oh_b, table_se, 0), axis=1)

    before = by_block(upto_e - by_expert(groups))
    src = blk * GROUPS_PER_BLOCK + by_block(by_expert(first)) + (k - before)
    zero_group = GROUPS_PER_BLOCK - 1
    spare = n_blocks * GROUPS_PER_BLOCK + (s_tile % 2) * TILE_GROUPS + slot % TILE_GROUPS
    gsrc = jnp.where(real, src, zero_group)
    gdst = jnp.where(real, src, spare)
    n_spare_blocks = -(-2 * TILE_GROUPS // GROUPS_PER_BLOCK)
    tail = jnp.concatenate([jnp.sum(groups, axis=1), jnp.zeros((n_spare_blocks,), jnp.int32)])
    n_real = jnp.sum(real.reshape(r_max, TILE_GROUPS).astype(jnp.int32), axis=1)
    return tile_e, next_e, n_real, n_tiles.reshape(1), gsrc, gdst, tail


def _layer(h, norm_mix_g, w_in, conv_w, sgu_ln_g, sgu_ln_b, sgu_w_s, sgu_b_s, w_up_conv,
           w_up_sgu, w_out, norm_ffn_g, w_rg, b_rg, w_re, b_re, w_eg, w_eu, w_ed, out_g):
    t, d = h.shape
    conv_width = conv_w.shape[1]
    sgu_width = sgu_ln_g.shape[0]
    xn, yb = _sgu_branch(h, norm_mix_g, w_in, 3 * conv_width, sgu_ln_g, sgu_ln_b, sgu_w_s, sgu_b_s)
    gate_col0 = 3 * conv_width + 2 * sgu_width
    ya, (wg_bf, wua_bf, wub_bf, wout_bf) = _conv_branch(
        xn, w_in, conv_w, conv_width,
        to_cast=[(w_in, gate_col0, w_in.shape[1] - gate_col0, math.gcd(gate_col0, w_in.shape[1])),
                 (w_up_conv, 0, d, d), (w_up_sgu, 0, d, d), (w_out, 0, d, d)])
    m, eg_bf, eu_bf, ed_bf = _upgate(xn, ya, yb, wg_bf, wua_bf, wub_bf, w_eg, w_eu, w_ed)
    h1 = _outproj(m, wout_bf, h)
    xs, meta, counts = _route_sort(h1, norm_ffn_g, w_rg, b_rg, w_re, b_re)
    ys = _experts(_expert_plan(counts, t // ROUTE_BLOCK), xs, eg_bf, eu_bf, ed_bf)
    cmeta = jnp.pad(meta[0:2].T, ((0, 0), (0, LANES - 2)))
    return _combine(ys, h1, cmeta, out_g)


def kernel(x, norm_mix_g, w_in, conv_w, sgu_ln_g, sgu_ln_b, sgu_w_s, sgu_b_s, w_up_conv, w_up_sgu, w_out, norm_ffn_g, w_router_group, b_router_group, w_router_expert, b_router_expert, w_exp_gate, w_exp_up, w_exp_down, norm_final_g):
    bsz, s, d = x.shape
    depth = w_in.shape[0]
    assert bsz == 1 and depth == 1, "causal conv carry and the fused final norm assume one sequence, one layer"
    assert s % ROUTE_BLOCK == 0
    out = _layer(x.reshape(s, d), norm_mix_g[0], w_in[0], conv_w[0], sgu_ln_g[0], sgu_ln_b[0],
                 sgu_w_s[0], sgu_b_s[0], w_up_conv[0], w_up_sgu[0], w_out[0], norm_ffn_g[0],
                 w_router_group[0], b_router_group[0], w_router_expert[0], b_router_expert[0],
                 w_exp_gate[0], w_exp_up[0], w_exp_down[0], norm_final_g)
    return out.reshape(bsz, s, d)
```

```python
import functools
import math

import jax
import jax.numpy as jnp
from jax import lax
from jax.experimental import pallas as pl
from jax.experimental.pallas import tpu as pltpu

F32 = jnp.float32
BF16 = jnp.bfloat16

EPS = 1e-6
CHUNK = 64
CONV_K = 3
SGU_HEADS = 8
SGU_BLOCK = 128
N_GROUPS = 4
EXPERTS_PER_GROUP = 4
N_EXPERTS = N_GROUPS * EXPERTS_PER_GROUP
ROUTER_ROWS = 32
LANES = 128

VMEM_LIMIT_BYTES = 56 * 1024 * 1024

ROUTE_BLOCK = 512
GROUP = 16
MXU_ROWS = 256
HALF_ROWS = MXU_ROWS // 2
LOCAL_ROWS = -(-(2 * ROUTE_BLOCK + N_EXPERTS * (GROUP - 1)) // MXU_ROWS) * MXU_ROWS
GROUPS_PER_BLOCK = LOCAL_ROWS // GROUP
assert LOCAL_ROWS - (2 * ROUTE_BLOCK + N_EXPERTS * (GROUP - 1)) >= GROUP
MOE_TILE = 512
TILE_GROUPS = MOE_TILE // GROUP


def _params(n_axes):
    return pltpu.CompilerParams(
        dimension_semantics=("arbitrary",) * n_axes,
        vmem_limit_bytes=VMEM_LIMIT_BYTES)


def _dot(a, b):
    return jnp.dot(a, b, preferred_element_type=F32)


def _rms_scale(x, g):
    ms = jnp.mean(x * x, axis=-1, keepdims=True)
    return x * lax.rsqrt(ms + EPS) * g


RING = 3


def _ring_fetch(srcs, bufs, sem, step, n_steps, n_blocks):
    def copies(s):
        blk = s % n_blocks
        slot = s % RING
        return [pltpu.make_async_copy(
            src.at[pl.ds(pl.multiple_of(blk * buf.shape[1], buf.shape[1]), buf.shape[1]), :],
            buf.at[slot], sem.at[slot]) for src, buf in zip(srcs, bufs)]

    @pl.when(step == 0)
    def _():
        for s in range(RING - 1):
            for cp in copies(s):
                cp.start()

    @pl.when(step + RING - 1 < n_steps)
    def _():
        for cp in copies(step + RING - 1):
            cp.start()

    for cp in copies(step):
        cp.wait()
    return step % RING


def _conv_kernel(cast_chunks, xn_ref, wb_ref, wc_ref, wh_ref, cw_ref, *rest):
    n_in = sum(cast_chunks)
    cast_in, o_ref = rest[:n_in], rest[n_in]
    cast_out = rest[n_in + 1:n_in + 1 + len(cast_chunks)]
    wbf_ref, carry_ref, xnbuf_ref, xsem = rest[n_in + 1 + len(cast_chunks):]
    i = pl.program_id(1)
    ni = pl.num_programs(1)
    tn = wb_ref.shape[1]
    tm = xnbuf_ref.shape[1]
    slot = _ring_fetch((xn_ref,), (xnbuf_ref,), xsem, pl.program_id(0) * ni + i,
                       pl.num_programs(0) * ni, ni)
    src = iter(cast_in)
    for dst, n_chunks in zip(cast_out, cast_chunks):
        wc = dst.shape[1] // n_chunks
        for k in range(n_chunks):
            dst[:, k * wc:(k + 1) * wc] = next(src)[...].astype(BF16)

    @pl.when(i == 0)
    def _():
        wbf_ref[:, 0:tn] = wb_ref[...].astype(BF16)
        wbf_ref[:, tn:2 * tn] = wc_ref[...].astype(BF16)
        wbf_ref[:, 2 * tn:3 * tn] = wh_ref[...].astype(BF16)
        carry_ref[...] = jnp.zeros_like(carry_ref)

    proj = _dot(xnbuf_ref[slot], wbf_ref[...])
    b = proj[:, 0:tn]
    p = proj[:, tn:2 * tn] * proj[:, 2 * tn:3 * tn]
    prev = carry_ref[...]
    carry_ref[...] = p[tm - 8:tm, :]
    row = lax.broadcasted_iota(jnp.int32, p.shape, 0)
    p1 = jnp.where(row == 0, prev[7:8, :], pltpu.roll(p, 1, axis=0))
    p2 = jnp.where(row == 0, prev[6:7, :],
                   jnp.where(row == 1, prev[7:8, :], pltpu.roll(p, 2, axis=0)))
    cw = cw_ref[...]
    y = b * (cw[0:1, :] * p2 + cw[1:2, :] * p1 + cw[2:3, :] * p)
    o_ref[...] = y.astype(o_ref.dtype)


def _conv_branch(xn, w_in, conv_w, width, to_cast, tm=1024, tn=256):
    t, d = xn.shape
    nj = width // tn
    ni = t // tm
    steps = nj * ni
    assert steps >= RING - 1
    cast_args, cast_in_specs, cast_out_specs, cast_shapes, cast_chunks = [], [], [], [], []
    for arr, col0, n_cols, chunk in to_cast:
        rows = arr.shape[0] // steps
        assert rows * steps == arr.shape[0] and rows % 16 == 0
        assert col0 % chunk == 0 and n_cols % chunk == 0
        for k in range(n_cols // chunk):
            cast_args.append(arr)
            cast_in_specs.append(
                pl.BlockSpec((rows, chunk), lambda j, i, c=col0 // chunk + k: (j * ni + i, c)))
        cast_chunks.append(n_cols // chunk)
        cast_out_specs.append(pl.BlockSpec((rows, n_cols), lambda j, i: (j * ni + i, 0)))
        cast_shapes.append(jax.ShapeDtypeStruct((arr.shape[0], n_cols), BF16))
    outs = pl.pallas_call(
        functools.partial(_conv_kernel, tuple(cast_chunks)),
        grid=(nj, ni),
        in_specs=[pl.BlockSpec(memory_space=pl.ANY),
                  pl.BlockSpec((d, tn), lambda j, i: (0, j)),
                  pl.BlockSpec((d, tn), lambda j, i: (0, nj + j)),
                  pl.BlockSpec((d, tn), lambda j, i: (0, 2 * nj + j)),
                  pl.BlockSpec((CONV_K, tn), lambda j, i: (0, j))] + cast_in_specs,
        out_specs=[pl.BlockSpec((tm, tn), lambda j, i: (i, j))] + cast_out_specs,
        out_shape=[jax.ShapeDtypeStruct((t, width), BF16)] + cast_shapes,
        scratch_shapes=[pltpu.VMEM((d, 3 * tn), BF16),
                        pltpu.VMEM((8, tn), F32),
                        pltpu.VMEM((RING, tm, d), BF16),
                        pltpu.SemaphoreType.DMA((RING,))],
        compiler_params=_params(2),
        name="conv_branch",
    )(xn, w_in, w_in, w_in, conv_w, *cast_args)
    return outs[0], outs[1:]


def _sgu_kernel(x_ref, g_ref, wu_ref, wv_ref, lng_ref, lnb_ref, ws_ref, bsx_ref, xn_ref, o_ref, wbf_ref):
    tm = x_ref.shape[0]
    w = o_ref.shape[1]
    hd = w // SGU_HEADS

    @pl.when(pl.program_id(0) == 0)
    def _():
        wbf_ref[:, 0:w] = wu_ref[...].astype(BF16)
        wbf_ref[:, w:2 * w] = wv_ref[...].astype(BF16)

    xn = _rms_scale(x_ref[...], g_ref[...]).astype(BF16)
    xn_ref[...] = xn
    gz = jax.nn.gelu(_dot(xn, wbf_ref[...]))
    v = gz[:, w:2 * w]
    mu = jnp.mean(v, axis=-1, keepdims=True)
    vc = v - mu
    var = jnp.mean(vc * vc, axis=-1, keepdims=True)
    vn = (vc * lax.rsqrt(var + EPS) * lng_ref[...] + lnb_ref[...]).astype(BF16)
    ii = lax.broadcasted_iota(jnp.int32, (SGU_BLOCK, SGU_BLOCK), 0)
    jj = lax.broadcasted_iota(jnp.int32, (SGU_BLOCK, SGU_BLOCK), 1)
    mask = (jj // CHUNK) <= (ii // CHUNK)
    for h in range(SGU_HEADS):
        wm = jnp.where(mask, ws_ref[h], 0.0).astype(BF16)
        cs = slice(h * hd, (h + 1) * hd)
        for n in range(tm // SGU_BLOCK):
            rs = slice(n * SGU_BLOCK, (n + 1) * SGU_BLOCK)
            vm = _dot(wm, vn[rs, cs]) + bsx_ref[:, cs]
            o_ref[rs, cs] = (gz[rs, cs] * vm).astype(o_ref.dtype)


def _sgu_branch(x, g, w_in, col0, ln_g, ln_b, w_s, b_s, tm=512):
    t, d = x.shape
    w = ln_g.shape[0]
    hd = w // SGU_HEADS
    assert col0 % w == 0
    c0 = col0 // w
    bsx = jnp.repeat(b_s.T, hd, axis=1)
    once = pl.Buffered(1)
    return pl.pallas_call(
        _sgu_kernel,
        grid=(t // tm,),
        in_specs=[pl.BlockSpec((tm, d), lambda i: (i, 0)),
                  pl.BlockSpec((1, d), lambda i: (0, 0)),
                  pl.BlockSpec((d, w), lambda i: (0, c0), pipeline_mode=once),
                  pl.BlockSpec((d, w), lambda i: (0, c0 + 1), pipeline_mode=once),
                  pl.BlockSpec((1, w), lambda i: (0, 0)),
                  pl.BlockSpec((1, w), lambda i: (0, 0)),
                  pl.BlockSpec((SGU_HEADS, SGU_BLOCK, SGU_BLOCK), lambda i: (0, 0, 0)),
                  pl.BlockSpec((SGU_BLOCK, w), lambda i: (0, 0))],
        out_specs=[pl.BlockSpec((tm, d), lambda i: (i, 0)),
                   pl.BlockSpec((tm, w), lambda i: (i, 0))],
        out_shape=[jax.ShapeDtypeStruct((t, d), BF16),
                   jax.ShapeDtypeStruct((t, w), BF16)],
        scratch_shapes=[pltpu.VMEM((d, 2 * w), BF16)],
        compiler_params=_params(1),
        name="sgu_branch",
    )(x, g.reshape(1, d), w_in, w_in, ln_g.reshape(1, w), ln_b.reshape(1, w), w_s, bsx)


def _upgate_kernel(xn_ref, ya_ref, yb_ref, wgc_ref, wgs_ref, wua_ref, wub_ref, eg_ref, eu_ref, ed_ref,
                   o_ref, eg_bf_ref, eu_bf_ref, ed_bf_ref):
    eg_bf_ref[...] = eg_ref[...].astype(BF16)
    eu_bf_ref[...] = eu_ref[...].astype(BF16)
    ed_bf_ref[...] = ed_ref[...].astype(BF16)

    xn = xn_ref[...]
    m = (jax.nn.sigmoid(_dot(xn, wgc_ref[...])) * _dot(ya_ref[...], wua_ref[...])
         + jax.nn.sigmoid(_dot(xn, wgs_ref[...])) * _dot(yb_ref[...], wub_ref[...]))
    o_ref[...] = m.astype(o_ref.dtype)


def _upgate(xn, ya, yb, w_gates, w_up_a, w_up_b, w_eg, w_eu, w_ed, tm=1024, tn=512):
    t, d = xn.shape
    wa = ya.shape[1]
    wb = yb.shape[1]
    dout = w_up_a.shape[1]
    c0 = 0
    nj = dout // tn
    ni = t // tm
    n_e, d_e, f_e = w_eg.shape
    up_rows = n_e * d_e // (nj * ni)
    down_rows = n_e * f_e // (nj * ni)
    assert up_rows * nj * ni == n_e * d_e and up_rows % 16 == 0
    assert down_rows * nj * ni == n_e * f_e and down_rows % 16 == 0
    up_spec = pl.BlockSpec((up_rows, f_e), lambda j, i: (j * ni + i, 0))
    down_spec = pl.BlockSpec((down_rows, d_e), lambda j, i: (j * ni + i, 0))
    m, eg_bf, eu_bf, ed_bf = pl.pallas_call(
        _upgate_kernel,
        grid=(nj, ni),
        in_specs=[pl.BlockSpec((tm, d), lambda j, i: (i, 0)),
                  pl.BlockSpec((tm, wa), lambda j, i: (i, 0)),
                  pl.BlockSpec((tm, wb), lambda j, i: (i, 0)),
                  pl.BlockSpec((d, tn), lambda j, i: (0, c0 + j)),
                  pl.BlockSpec((d, tn), lambda j, i: (0, c0 + nj + j)),
                  pl.BlockSpec((wa, tn), lambda j, i: (0, j)),
                  pl.BlockSpec((wb, tn), lambda j, i: (0, j)),
                  up_spec, up_spec, down_spec],
        out_specs=[pl.BlockSpec((tm, tn), lambda j, i: (i, j)), up_spec, up_spec, down_spec],
        out_shape=[jax.ShapeDtypeStruct((t, dout), BF16),
                   jax.ShapeDtypeStruct((n_e * d_e, f_e), BF16),
                   jax.ShapeDtypeStruct((n_e * d_e, f_e), BF16),
                   jax.ShapeDtypeStruct((n_e * f_e, d_e), BF16)],
        compiler_params=_params(2),
        name="upgate",
    )(xn, ya, yb, w_gates, w_gates, w_up_a, w_up_b,
      w_eg.reshape(n_e * d_e, f_e), w_eu.reshape(n_e * d_e, f_e), w_ed.reshape(n_e * f_e, d_e))
    return (m, eg_bf.reshape(n_e, d_e, f_e), eu_bf.reshape(n_e, d_e, f_e),
            ed_bf.reshape(n_e, f_e, d_e))


def _outproj_kernel(m_ref, w_ref, x_ref, o_ref):
    o_ref[...] = x_ref[...] + _dot(m_ref[...], w_ref[...])


def _outproj(m, w_out_bf, x, tm=512):
    t, d = m.shape
    dout = w_out_bf.shape[1]
    return pl.pallas_call(
        _outproj_kernel,
        grid=(t // tm,),
        in_specs=[pl.BlockSpec((tm, d), lambda i: (i, 0)),
                  pl.BlockSpec((d, dout), lambda i: (0, 0), pipeline_mode=pl.Buffered(1)),
                  pl.BlockSpec((tm, dout), lambda i: (i, 0))],
        out_specs=pl.BlockSpec((tm, dout), lambda i: (i, 0)),
        out_shape=jax.ShapeDtypeStruct((t, dout), F32),
        compiler_params=_params(1),
        name="outproj",
    )(m, w_out_bf, x)


def _argmax_rows(rows):
    best = rows[0]
    idx = jnp.zeros(rows[0].shape, jnp.int32)
    for k in range(1, len(rows)):
        better = rows[k] > best
        best = jnp.where(better, rows[k], best)
        idx = jnp.where(better, k, idx)
    return best, idx


def _softmax_rows(rows):
    mx = functools.reduce(jnp.maximum, rows)
    ex = [jnp.exp(r - mx) for r in rows]
    den = functools.reduce(lambda a, b: a + b, ex)
    return [e / den for e in ex]


def _route_sort_kernel(h_ref, g_ref, wr_ref, br_ref, xs_ref, meta_ref, cnt_ref, before_ref):
    tm = h_ref.shape[0]
    xn = _rms_scale(h_ref[...], g_ref[...])
    xn_hi = xn.astype(BF16)
    xn_lo = (xn - xn_hi.astype(F32)).astype(BF16)
    wr = wr_ref[...]
    wr_hi = wr.astype(BF16)
    wr_lo = (wr - wr_hi.astype(F32)).astype(BF16)
    nt_dims = (((1,), (1,)), ((), ()))
    lt = (lax.dot_general(wr_hi, xn_hi, nt_dims, preferred_element_type=F32)
          + lax.dot_general(wr_hi, xn_lo, nt_dims, preferred_element_type=F32)
          + lax.dot_general(wr_lo, xn_hi, nt_dims, preferred_element_type=F32)) + br_ref[...]
    pgs = _softmax_rows([lt[k:k + 1, :] for k in range(N_GROUPS)])
    pg, gi = _argmax_rows(pgs)
    sel = []
    for k in range(EXPERTS_PER_GROUP):
        r = jnp.zeros_like(pg)
        for g in range(N_GROUPS):
            row = N_GROUPS + g * EXPERTS_PER_GROUP + k
            r = jnp.where(gi == g, lt[row:row + 1, :], r)
        sel.append(r)
    pes = _softmax_rows(sel)
    p1, e1 = _argmax_rows(pes)
    rest = [jnp.where(e1 == k, -1.0, pes[k]) for k in range(EXPERTS_PER_GROUP)]
    p2, e2 = _argmax_rows(rest)
    den = p1 + p2
    w1 = pg * (p1 / den)
    w2 = pg * (p2 / den)
    lo = jnp.minimum(e1, e2)
    hi = jnp.maximum(e1, e2)
    w_lo = jnp.where(e1 < e2, w1, w2)
    w_hi = jnp.where(e1 < e2, w2, w1)
    ea = gi * EXPERTS_PER_GROUP + lo
    eb = gi * EXPERTS_PER_GROUP + hi

    erow = lax.broadcasted_iota(jnp.int32, (N_EXPERTS, tm), 0)
    oh_a = (erow == ea).astype(F32)
    oh_b = (erow == eb).astype(F32)

    @pl.when(pl.program_id(0) == 0)
    def _():
        a = lax.broadcasted_iota(jnp.int32, (tm, tm), 0)
        b = lax.broadcasted_iota(jnp.int32, (tm, tm), 1)
        before_ref[...] = (a < b).astype(BF16)

    cum = _dot((oh_a + oh_b).astype(BF16), before_ref[...])
    cnt = jnp.sum(oh_a + oh_b, axis=1, keepdims=True)
    padded = jnp.floor((cnt + (GROUP - 1)) * (1.0 / GROUP)) * GROUP
    pos_a = jnp.sum(oh_a * cum + jnp.where(erow < ea, padded, 0.0), axis=0, keepdims=True)
    pos_b = jnp.sum(oh_b * cum + jnp.where(erow < eb, padded, 0.0), axis=0, keepdims=True)

    d = h_ref.shape[1]

    def gate_rows(w):
        hi = w.astype(BF16).astype(F32)
        lo = w - hi
        k = lax.broadcasted_iota(jnp.int32, (LANES, tm), 0)
        return jnp.where(k == 0, hi, jnp.where(k == 1, lo, 0.0)).astype(BF16)

    gate_a = gate_rows(w_lo)
    gate_b = gate_rows(w_hi)

    def sort_rows(r0, n):
        q = r0 + lax.broadcasted_iota(jnp.int32, (n, tm), 0)
        perm_a = jnp.where(q == pos_a.astype(jnp.int32), 1.0, 0.0).astype(BF16)
        perm_b = jnp.where(q == pos_b.astype(jnp.int32), 1.0, 0.0).astype(BF16)
        xs_ref[r0:r0 + n, 0:d] = _dot(perm_a + perm_b, xn_hi).astype(xs_ref.dtype)
        gates = (lax.dot_general(perm_a, gate_a, nt_dims, preferred_element_type=F32)
                 + lax.dot_general(perm_b, gate_b, nt_dims, preferred_element_type=F32))
        xs_ref[r0:r0 + n, d:d + LANES] = gates.astype(xs_ref.dtype)

    n_rows = xs_ref.shape[0]
    head = n_rows - HALF_ROWS
    used = jnp.sum(padded)
    sort_rows(0, head)

    @pl.when(used > head)
    def _():
        sort_rows(head, HALF_ROWS)

    @pl.when(used <= head)
    def _():
        xs_ref[head:n_rows, :] = jnp.zeros((HALF_ROWS, xs_ref.shape[1]), xs_ref.dtype)

    cnt_ref[...] = jnp.broadcast_to(cnt, cnt_ref.shape)
    meta_ref[0:1, :] = pos_a
    meta_ref[1:2, :] = pos_b
    meta_ref[2:8, :] = jnp.zeros((6, tm), F32)


def _route_sort(h1, g, w_rg, b_rg, w_re, b_re):
    t, d = h1.shape
    tm = ROUTE_BLOCK
    nb = t // tm
    n_log = w_rg.shape[1] + w_re.shape[1]
    wr = jnp.concatenate([w_rg, w_re], axis=1).T
    wr = jnp.pad(wr, ((0, ROUTER_ROWS - n_log), (0, 0)))
    br = jnp.pad(jnp.concatenate([b_rg, b_re]), (0, ROUTER_ROWS - n_log)).reshape(ROUTER_ROWS, 1)
    return pl.pallas_call(
        _route_sort_kernel,
        grid=(nb,),
        in_specs=[pl.BlockSpec((tm, d), lambda i: (i, 0)),
                  pl.BlockSpec((1, d), lambda i: (0, 0)),
                  pl.BlockSpec((ROUTER_ROWS, d), lambda i: (0, 0)),
                  pl.BlockSpec((ROUTER_ROWS, 1), lambda i: (0, 0))],
        out_specs=[pl.BlockSpec((LOCAL_ROWS, d + LANES), lambda i: (i, 0)),
                   pl.BlockSpec((8, tm), lambda i: (0, i)),
                   pl.BlockSpec((N_EXPERTS, LANES), lambda i: (i, 0))],
        out_shape=[jax.ShapeDtypeStruct((nb * LOCAL_ROWS, d + LANES), BF16),
                   jax.ShapeDtypeStruct((8, t), F32),
                   jax.ShapeDtypeStruct((nb * N_EXPERTS, LANES), F32)],
        scratch_shapes=[pltpu.VMEM((tm, tm), BF16)],
        compiler_params=_params(1),
        name="route_sort",
    )(h1, g.reshape(1, d), wr, br)


def _group_copy(src_ref, s_group, dst_ref, d_group, sem):
    return pltpu.make_async_copy(src_ref.at[pl.ds(pl.multiple_of(s_group * GROUP, GROUP), GROUP), :],
                                 dst_ref.at[pl.ds(pl.multiple_of(d_group * GROUP, GROUP), GROUP), :], sem)


def _expert_kernel(te_ref, nxt_ref, nreal_ref, nt_ref, gsrc_ref, gdst_ref, tail_ref,
                   xs_ref, wg_ref, wu_ref, wd_ref, ys_ref,
                   xbuf_ref, ybuf_ref, zero_ref, wgu_buf_ref, wd_buf_ref,
                   cur_ref, gsem, ssem, zsem, wsem):
    r = pl.program_id(0)
    nt = nt_ref[0]

    def gather(q, slot):
        for i in range(TILE_GROUPS):
            _group_copy(xs_ref, gsrc_ref[q * TILE_GROUPS + i], xbuf_ref, slot * TILE_GROUPS + i,
                        gsem.at[slot]).start(priority=1)

    def weight_copies(e, wslot):
        f = wg_ref.shape[2]
        return (pltpu.make_async_copy(wg_ref.at[e], wgu_buf_ref.at[wslot, :, pl.ds(0, f)], wsem.at[wslot]),
                pltpu.make_async_copy(wu_ref.at[e], wgu_buf_ref.at[wslot, :, pl.ds(f, f)], wsem.at[wslot]),
                pltpu.make_async_copy(wd_ref.at[e], wd_buf_ref.at[wslot], wsem.at[wslot]))

    def tile_wait(src, dst, sem):
        pltpu.make_async_copy(src.at[pl.ds(0, MOE_TILE), :], dst.at[pl.ds(0, MOE_TILE), :], sem).wait()

    @pl.when(r == 0)
    def _():
        cur_ref[0] = -1
        cur_ref[1] = -1
        for cp in weight_copies(te_ref[0], 0):
            cp.start()
        gather(0, 0)
        zero_ref[...] = jnp.zeros_like(zero_ref)
        ybuf_ref[...] = jnp.zeros_like(ybuf_ref)
        n_blocks = tail_ref.shape[0]

        def fill(make):
            def body(g, c):
                make(g)
                return c
            return body

        for blk in range(n_blocks):
            lax.fori_loop(tail_ref[blk], GROUPS_PER_BLOCK, fill(
                lambda g, blk=blk: _group_copy(zero_ref, 0, ys_ref, blk * GROUPS_PER_BLOCK + g, zsem).start()), 0)
        for blk in range(n_blocks):
            lax.fori_loop(tail_ref[blk], GROUPS_PER_BLOCK, fill(
                lambda g, blk=blk: _group_copy(zero_ref, 0, ys_ref, blk * GROUPS_PER_BLOCK + g, zsem).wait()), 0)

    @pl.when(r < nt)
    def _():
        e = te_ref[r]
        slot = r % 2

        gather(jnp.minimum(r + 1, nt - 1), 1 - slot)

        @pl.when(cur_ref[0] != e)
        def _():
            wslot = (cur_ref[1] + 1) % 2
            for cp in weight_copies(e, wslot):
                cp.wait()
            cur_ref[0] = e
            cur_ref[1] = cur_ref[1] + 1

            @pl.when(nxt_ref[r] >= 0)
            def _():
                for cp in weight_copies(nxt_ref[r], 1 - wslot):
                    cp.start()

        wslot = cur_ref[1] % 2
        tile_wait(xs_ref, xbuf_ref, gsem.at[slot])

        @pl.when(r >= 2)
        def _():
            tile_wait(ybuf_ref, ys_ref, ssem.at[slot])

        d = ybuf_ref.shape[1]
        f = wd_buf_ref.shape[1]
        def chain(row0, n_rows):
            rows = pl.ds(pl.multiple_of(slot * MOE_TILE + row0, HALF_ROWS), n_rows)
            x = xbuf_ref[rows, 0:d]
            gate_parts = xbuf_ref[rows, d:d + LANES].astype(F32)
            gate = gate_parts[:, 0:1] + gate_parts[:, 1:2]
            gu = _dot(x, wgu_buf_ref[wslot])
            hid = (jax.nn.silu(gu[:, 0:f]) * gu[:, f:2 * f]).astype(BF16)
            ybuf_ref[rows, :] = (_dot(hid, wd_buf_ref[wslot]) * gate).astype(ybuf_ref.dtype)

        half_groups = HALF_ROWS // GROUP
        n_halves = (nreal_ref[r] + half_groups - 1) // half_groups
        for count in range(1, MOE_TILE // HALF_ROWS + 1):
            @pl.when(n_halves == count)
            def _(count=count):
                for part in range(count // 2):
                    chain(part * MXU_ROWS, MXU_ROWS)
                if count % 2:
                    chain((count // 2) * MXU_ROWS, HALF_ROWS)

        for i in range(TILE_GROUPS):
            _group_copy(ybuf_ref, slot * TILE_GROUPS + i, ys_ref, gdst_ref[r * TILE_GROUPS + i],
                        ssem.at[slot]).start(priority=1)

        @pl.when(r == nt - 1)
        def _():
            tile_wait(xs_ref, xbuf_ref, gsem.at[1 - slot])
            tile_wait(ybuf_ref, ys_ref, ssem.at[slot])

            @pl.when(r >= 1)
            def _():
                tile_wait(ybuf_ref, ys_ref, ssem.at[1 - slot])


def _experts(plan, xs, w_gate, w_up, w_down):
    d = w_gate.shape[1]
    f = w_gate.shape[2]
    tile_e, next_e, n_real, n_tiles, gsrc, gdst, tail = plan
    r_max = tile_e.shape[0]
    n_blocks = tail.shape[0]
    any_spec = pl.BlockSpec(memory_space=pl.ANY)
    grid_spec = pltpu.PrefetchScalarGridSpec(
        num_scalar_prefetch=7,
        grid=(r_max,),
        in_specs=[any_spec, any_spec, any_spec, any_spec],
        out_specs=any_spec,
        scratch_shapes=[pltpu.VMEM((2 * MOE_TILE, d + LANES), BF16),
                        pltpu.VMEM((2 * MOE_TILE, d), BF16),
                        pltpu.VMEM((GROUP, d), BF16),
                        pltpu.VMEM((2, d, 2 * f), BF16),
                        pltpu.VMEM((2, f, d), BF16),
                        pltpu.SMEM((2,), jnp.int32),
                        pltpu.SemaphoreType.DMA((2,)),
                        pltpu.SemaphoreType.DMA((2,)),
                        pltpu.SemaphoreType.DMA(()),
                        pltpu.SemaphoreType.DMA((2,))],
    )
    return pl.pallas_call(
        _expert_kernel,
        grid_spec=grid_spec,
        out_shape=jax.ShapeDtypeStruct((n_blocks * LOCAL_ROWS, d), BF16),
        compiler_params=_params(1),
        name="experts",
    )(tile_e, next_e, n_real, n_tiles, gsrc, gdst, tail, xs, w_gate, w_up, w_down)


def _combine_kernel(ys_ref, h_ref, cm_ref, g_ref, o_ref, ybuf_ref, hbuf_ref, sem):
    n = pl.num_programs(0)
    tm = o_ref.shape[0]
    rows = ybuf_ref.shape[1]
    slot = _ring_fetch((ys_ref, h_ref), (ybuf_ref, hbuf_ref), sem, pl.program_id(0), n, n)
    cm = cm_ref[...].T
    q = lax.broadcasted_iota(jnp.int32, (tm, rows), 1)
    sel = jnp.where((q == cm[:, 0:1].astype(jnp.int32)) | (q == cm[:, 1:2].astype(jnp.int32)), 1.0, 0.0)
    h2 = hbuf_ref[slot] + _dot(sel.astype(BF16), ybuf_ref[slot])
    o_ref[...] = _rms_scale(h2, g_ref[...])


def _combine(ys, h1, meta, g):
    t, d = h1.shape
    tm = ROUTE_BLOCK
    assert t // tm >= RING - 1
    any_spec = pl.BlockSpec(memory_space=pl.ANY)
    return pl.pallas_call(
        _combine_kernel,
        grid=(t // tm,),
        in_specs=[any_spec, any_spec,
                  pl.BlockSpec((8, tm), lambda i: (0, i)),
                  pl.BlockSpec((1, d), lambda i: (0, 0))],
        out_specs=pl.BlockSpec((tm, d), lambda i: (i, 0)),
        out_shape=jax.ShapeDtypeStruct((t, d), F32),
        scratch_shapes=[pltpu.VMEM((RING, LOCAL_ROWS, d), BF16),
                        pltpu.VMEM((RING, tm, d), F32),
                        pltpu.SemaphoreType.DMA((RING,))],
        compiler_params=_params(1),
        name="combine",
    )(ys, h1, meta, g.reshape(1, d))


def _expert_plan(counts, n_blocks):
    cnt = counts[:, 0].astype(jnp.int32).reshape(n_blocks, N_EXPERTS)
    groups = (cnt + GROUP - 1) // GROUP
    first = jnp.cumsum(groups, axis=1) - groups
    upto = jnp.cumsum(groups, axis=0)
    per_expert = upto[-1]
    tiles_e = (per_expert + TILE_GROUPS - 1) // TILE_GROUPS
    tile_end = jnp.cumsum(tiles_e)
    n_tiles = tile_end[-1]
    max_groups = 2 * ROUTE_BLOCK * n_blocks // GROUP + n_blocks * N_EXPERTS
    r_max = max_groups // TILE_GROUPS + N_EXPERTS
    tile_ids = jnp.arange(r_max, dtype=jnp.int32)
    tile = jnp.minimum(tile_ids, n_tiles - 1)
    tile_e = jnp.sum((tile_end[None, :] <= tile[:, None]).astype(jnp.int32), axis=1)
    later = (tile_e[None, :] > tile_e[:, None]) & (tile_ids[None, :] < n_tiles)
    next_e = jnp.min(jnp.where(later, tile_e[None, :], N_EXPERTS), axis=1)
    next_e = jnp.where(next_e == N_EXPERTS, -1, next_e)

    slot = jnp.arange(r_max * TILE_GROUPS, dtype=jnp.int32)
    s_tile = slot // TILE_GROUPS
    oh_e = jnp.repeat(tile_e, TILE_GROUPS)[:, None] == jnp.arange(N_EXPERTS, dtype=jnp.int32)[None, :]

    def by_expert(table):
        return jnp.sum(jnp.where(oh_e[:, None, :], table[None], 0), axis=-1)

    k = slot - by_expert(((tile_end - tiles_e) * TILE_GROUPS)[None, :])[:, 0]
    real = (k < by_expert(per_expert[None, :])[:, 0]) & (s_tile < n_tiles)
    upto_e = by_expert(upto)
    blk = jnp.minimum(jnp.sum((upto_e <= k[:, None]).astype(jnp.int32), axis=1), n_blocks - 1)
    oh_b = blk[:, None] == jnp.arange(n_blocks, dtype=jnp.int32)[None, :]

    def by_block(table_se):
        return jnp.sum(jnp.where(oh_b, table_se, 0), axis=1)

    before = by_block(upto_e - by_expert(groups))
    src = blk * GROUPS_PER_BLOCK + by_block(by_expert(first)) + (k - before)
    zero_group = GROUPS_PER_BLOCK - 1
    spare = n_blocks * GROUPS_PER_BLOCK + (s_tile % 2) * TILE_GROUPS + slot % TILE_GROUPS
    gsrc = jnp.where(real, src, zero_group)
    gdst = jnp.where(real, src, spare)
    n_spare_blocks = -(-2 * TILE_GROUPS // GROUPS_PER_BLOCK)
    tail = jnp.concatenate([jnp.sum(groups, axis=1), jnp.zeros((n_spare_blocks,), jnp.int32)])
    n_real = jnp.sum(real.reshape(r_max, TILE_GROUPS).astype(jnp.int32), axis=1)
    return tile_e, next_e, n_real, n_tiles.reshape(1), gsrc, gdst, tail


def _layer(h, norm_mix_g, w_in, conv_w, sgu_ln_g, sgu_ln_b, sgu_w_s, sgu_b_s, w_up_conv,
           w_up_sgu, w_out, norm_ffn_g, w_rg, b_rg, w_re, b_re, w_eg, w_eu, w_ed, out_g):
    t, d = h.shape
    conv_width = conv_w.shape[1]
    sgu_width = sgu_ln_g.shape[0]
    xn, yb = _sgu_branch(h, norm_mix_g, w_in, 3 * conv_width, sgu_ln_g, sgu_ln_b, sgu_w_s, sgu_b_s)
    gate_col0 = 3 * conv_width + 2 * sgu_width
    ya, (wg_bf, wua_bf, wub_bf, wout_bf) = _conv_branch(
        xn, w_in, conv_w, conv_width,
        to_cast=[(w_in, gate_col0, w_in.shape[1] - gate_col0, math.gcd(gate_col0, w_in.shape[1])),
                 (w_up_conv, 0, d, d), (w_up_sgu, 0, d, d), (w_out, 0, d, d)])
    m, eg_bf, eu_bf, ed_bf = _upgate(xn, ya, yb, wg_bf, wua_bf, wub_bf, w_eg, w_eu, w_ed)
    h1 = _outproj(m, wout_bf, h)
    xs, meta, counts = _route_sort(h1, norm_ffn_g, w_rg, b_rg, w_re, b_re)
    ys = _experts(_expert_plan(counts, t // ROUTE_BLOCK), xs, eg_bf, eu_bf, ed_bf)
    return _combine(ys, h1, meta, out_g)


def kernel(x, norm_mix_g, w_in, conv_w, sgu_ln_g, sgu_ln_b, sgu_w_s, sgu_b_s, w_up_conv, w_up_sgu, w_out, norm_ffn_g, w_router_group, b_router_group, w_router_expert, b_router_expert, w_exp_gate, w_exp_up, w_exp_down, norm_final_g):
    bsz, s, d = x.shape
    depth = w_in.shape[0]
    assert bsz == 1 and depth == 1, "causal conv carry and the fused final norm assume one sequence, one layer"
    assert s % ROUTE_BLOCK == 0
    out = _layer(x.reshape(s, d), norm_mix_g[0], w_in[0], conv_w[0], sgu_ln_g[0], sgu_ln_b[0],
                 sgu_w_s[0], sgu_b_s[0], w_up_conv[0], w_up_sgu[0], w_out[0], norm_ffn_g[0],
                 w_router_group[0], b_router_group[0], w_router_expert[0], b_router_expert[0],
                 w_exp_gate[0], w_exp_up[0], w_exp_down[0], norm_final_g)
    return out.reshape(bsz, s, d)
```

```python
import functools
import math

import jax
import jax.numpy as jnp
from jax import lax
from jax.experimental import pallas as pl
from jax.experimental.pallas import tpu as pltpu

F32 = jnp.float32
BF16 = jnp.bfloat16

EPS = 1e-6
CHUNK = 64
CONV_K = 3
SGU_HEADS = 8
SGU_BLOCK = 128
N_GROUPS = 4
EXPERTS_PER_GROUP = 4
N_EXPERTS = N_GROUPS * EXPERTS_PER_GROUP
ROUTER_ROWS = 32
LANES = 128

VMEM_LIMIT_BYTES = 56 * 1024 * 1024

ROUTE_BLOCK = 512
GROUP = 16
MXU_ROWS = 256
HALF_ROWS = MXU_ROWS // 2
LOCAL_ROWS = -(-(2 * ROUTE_BLOCK + N_EXPERTS * (GROUP - 1)) // MXU_ROWS) * MXU_ROWS
GROUPS_PER_BLOCK = LOCAL_ROWS // GROUP
assert LOCAL_ROWS - (2 * ROUTE_BLOCK + N_EXPERTS * (GROUP - 1)) >= GROUP
MOE_TILE = 512
TILE_GROUPS = MOE_TILE // GROUP


def _params(n_axes):
    return pltpu.CompilerParams(
        dimension_semantics=("arbitrary",) * n_axes,
        vmem_limit_bytes=VMEM_LIMIT_BYTES)


def _dot(a, b):
    return jnp.dot(a, b, preferred_element_type=F32)


def _rms_scale(x, g):
    ms = jnp.mean(x * x, axis=-1, keepdims=True)
    return x * lax.rsqrt(ms + EPS) * g


RING = 3


def _ring_fetch(srcs, bufs, sem, step, n_steps, n_blocks):
    def copies(s):
        blk = s % n_blocks
        slot = s % RING
        return [pltpu.make_async_copy(
            src.at[pl.ds(pl.multiple_of(blk * buf.shape[1], buf.shape[1]), buf.shape[1]), :],
            buf.at[slot], sem.at[slot]) for src, buf in zip(srcs, bufs)]

    @pl.when(step == 0)
    def _():
        for s in range(RING - 1):
            for cp in copies(s):
                cp.start()

    @pl.when(step + RING - 1 < n_steps)
    def _():
        for cp in copies(step + RING - 1):
            cp.start()

    for cp in copies(step):
        cp.wait()
    return step % RING


def _conv_kernel(cast_chunks, xn_ref, wb_ref, wc_ref, wh_ref, cw_ref, *rest):
    n_in = sum(cast_chunks)
    cast_in, o_ref = rest[:n_in], rest[n_in]
    cast_out = rest[n_in + 1:n_in + 1 + len(cast_chunks)]
    wbf_ref, carry_ref, xnbuf_ref, xsem = rest[n_in + 1 + len(cast_chunks):]
    i = pl.program_id(1)
    ni = pl.num_programs(1)
    tn = wb_ref.shape[1]
    tm = xnbuf_ref.shape[1]
    slot = _ring_fetch((xn_ref,), (xnbuf_ref,), xsem, pl.program_id(0) * ni + i,
                       pl.num_programs(0) * ni, ni)
    src = iter(cast_in)
    for dst, n_chunks in zip(cast_out, cast_chunks):
        wc = dst.shape[1] // n_chunks
        for k in range(n_chunks):
            dst[:, k * wc:(k + 1) * wc] = next(src)[...].astype(BF16)

    @pl.when(i == 0)
    def _():
        wbf_ref[:, 0:tn] = wb_ref[...].astype(BF16)
        wbf_ref[:, tn:2 * tn] = wc_ref[...].astype(BF16)
        wbf_ref[:, 2 * tn:3 * tn] = wh_ref[...].astype(BF16)
        carry_ref[...] = jnp.zeros_like(carry_ref)

    proj = _dot(xnbuf_ref[slot], wbf_ref[...])
    b = proj[:, 0:tn]
    p = proj[:, tn:2 * tn] * proj[:, 2 * tn:3 * tn]
    prev = carry_ref[...]
    carry_ref[...] = p[tm - 8:tm, :]
    row = lax.broadcasted_iota(jnp.int32, p.shape, 0)
    p1 = jnp.where(row == 0, prev[7:8, :], pltpu.roll(p, 1, axis=0))
    p2 = jnp.where(row == 0, prev[6:7, :],
                   jnp.where(row == 1, prev[7:8, :], pltpu.roll(p, 2, axis=0)))
    cw = cw_ref[...]
    y = b * (cw[0:1, :] * p2 + cw[1:2, :] * p1 + cw[2:3, :] * p)
    o_ref[...] = y.astype(o_ref.dtype)


def _conv_branch(xn, w_in, conv_w, width, to_cast, tm=1024, tn=256):
    t, d = xn.shape
    nj = width // tn
    ni = t // tm
    steps = nj * ni
    assert steps >= RING - 1
    cast_args, cast_in_specs, cast_out_specs, cast_shapes, cast_chunks = [], [], [], [], []
    for arr, col0, n_cols, chunk in to_cast:
        rows = arr.shape[0] // steps
        assert rows * steps == arr.shape[0] and rows % 16 == 0
        assert col0 % chunk == 0 and n_cols % chunk == 0
        for k in range(n_cols // chunk):
            cast_args.append(arr)
            cast_in_specs.append(
                pl.BlockSpec((rows, chunk), lambda j, i, c=col0 // chunk + k: (j * ni + i, c)))
        cast_chunks.append(n_cols // chunk)
        cast_out_specs.append(pl.BlockSpec((rows, n_cols), lambda j, i: (j * ni + i, 0)))
        cast_shapes.append(jax.ShapeDtypeStruct((arr.shape[0], n_cols), BF16))
    outs = pl.pallas_call(
        functools.partial(_conv_kernel, tuple(cast_chunks)),
        grid=(nj, ni),
        in_specs=[pl.BlockSpec(memory_space=pl.ANY),
                  pl.BlockSpec((d, tn), lambda j, i: (0, j)),
                  pl.BlockSpec((d, tn), lambda j, i: (0, nj + j)),
                  pl.BlockSpec((d, tn), lambda j, i: (0, 2 * nj + j)),
                  pl.BlockSpec((CONV_K, tn), lambda j, i: (0, j))] + cast_in_specs,
        out_specs=[pl.BlockSpec((tm, tn), lambda j, i: (i, j))] + cast_out_specs,
        out_shape=[jax.ShapeDtypeStruct((t, width), BF16)] + cast_shapes,
        scratch_shapes=[pltpu.VMEM((d, 3 * tn), BF16),
                        pltpu.VMEM((8, tn), F32),
                        pltpu.VMEM((RING, tm, d), BF16),
                        pltpu.SemaphoreType.DMA((RING,))],
        compiler_params=_params(2),
        name="conv_branch",
    )(xn, w_in, w_in, w_in, conv_w, *cast_args)
    return outs[0], outs[1:]


def _sgu_kernel(x_ref, g_ref, wu_ref, wv_ref, lng_ref, lnb_ref, ws_ref, bs_ref, xn_ref, o_ref, wbf_ref):
    tm = x_ref.shape[0]
    w = o_ref.shape[1]
    hd = w // SGU_HEADS

    @pl.when(pl.program_id(0) == 0)
    def _():
        wbf_ref[:, 0:w] = wu_ref[...].astype(BF16)
        wbf_ref[:, w:2 * w] = wv_ref[...].astype(BF16)

    xn = _rms_scale(x_ref[...], g_ref[...]).astype(BF16)
    xn_ref[...] = xn
    gz = jax.nn.gelu(_dot(xn, wbf_ref[...]))
    v = gz[:, w:2 * w]
    mu = jnp.mean(v, axis=-1, keepdims=True)
    vc = v - mu
    var = jnp.mean(vc * vc, axis=-1, keepdims=True)
    vn = (vc * lax.rsqrt(var + EPS) * lng_ref[...] + lnb_ref[...]).astype(BF16)
    ii = lax.broadcasted_iota(jnp.int32, (SGU_BLOCK, SGU_BLOCK), 0)
    jj = lax.broadcasted_iota(jnp.int32, (SGU_BLOCK, SGU_BLOCK), 1)
    mask = (jj // CHUNK) <= (ii // CHUNK)
    bs_t = bs_ref[...].T
    for h in range(SGU_HEADS):
        wm = jnp.where(mask, ws_ref[h], 0.0).astype(BF16)
        cs = slice(h * hd, (h + 1) * hd)
        bias = jnp.broadcast_to(bs_t[:, h:h + 1], (SGU_BLOCK, hd))
        for n in range(tm // SGU_BLOCK):
            rs = slice(n * SGU_BLOCK, (n + 1) * SGU_BLOCK)
            vm = _dot(wm, vn[rs, cs]) + bias
            o_ref[rs, cs] = (gz[rs, cs] * vm).astype(o_ref.dtype)


def _sgu_branch(x, g, w_in, col0, ln_g, ln_b, w_s, b_s, tm=512):
    t, d = x.shape
    w = ln_g.shape[0]
    assert col0 % w == 0
    c0 = col0 // w
    once = pl.Buffered(1)
    return pl.pallas_call(
        _sgu_kernel,
        grid=(t // tm,),
        in_specs=[pl.BlockSpec((tm, d), lambda i: (i, 0)),
                  pl.BlockSpec((1, d), lambda i: (0, 0)),
                  pl.BlockSpec((d, w), lambda i: (0, c0), pipeline_mode=once),
                  pl.BlockSpec((d, w), lambda i: (0, c0 + 1), pipeline_mode=once),
                  pl.BlockSpec((1, w), lambda i: (0, 0)),
                  pl.BlockSpec((1, w), lambda i: (0, 0)),
                  pl.BlockSpec((SGU_HEADS, SGU_BLOCK, SGU_BLOCK), lambda i: (0, 0, 0)),
                  pl.BlockSpec((SGU_HEADS, SGU_BLOCK), lambda i: (0, 0))],
        out_specs=[pl.BlockSpec((tm, d), lambda i: (i, 0)),
                   pl.BlockSpec((tm, w), lambda i: (i, 0))],
        out_shape=[jax.ShapeDtypeStruct((t, d), BF16),
                   jax.ShapeDtypeStruct((t, w), BF16)],
        scratch_shapes=[pltpu.VMEM((d, 2 * w), BF16)],
        compiler_params=_params(1),
        name="sgu_branch",
    )(x, g.reshape(1, d), w_in, w_in, ln_g.reshape(1, w), ln_b.reshape(1, w), w_s, b_s)


def _upgate_kernel(xn_ref, ya_ref, yb_ref, wgc_ref, wgs_ref, wua_ref, wub_ref, eg_ref, eu_ref, ed_ref,
                   o_ref, eg_bf_ref, eu_bf_ref, ed_bf_ref):
    eg_bf_ref[...] = eg_ref[...].astype(BF16)
    eu_bf_ref[...] = eu_ref[...].astype(BF16)
    ed_bf_ref[...] = ed_ref[...].astype(BF16)

    xn = xn_ref[...]
    m = (jax.nn.sigmoid(_dot(xn, wgc_ref[...])) * _dot(ya_ref[...], wua_ref[...])
         + jax.nn.sigmoid(_dot(xn, wgs_ref[...])) * _dot(yb_ref[...], wub_ref[...]))
    o_ref[...] = m.astype(o_ref.dtype)


def _upgate(xn, ya, yb, w_gates, w_up_a, w_up_b, w_eg, w_eu, w_ed, tm=1024, tn=512):
    t, d = xn.shape
    wa = ya.shape[1]
    wb = yb.shape[1]
    dout = w_up_a.shape[1]
    c0 = 0
    nj = dout // tn
    ni = t // tm
    n_e, d_e, f_e = w_eg.shape
    up_rows = n_e * d_e // (nj * ni)
    down_rows = n_e * f_e // (nj * ni)
    assert up_rows * nj * ni == n_e * d_e and up_rows % 16 == 0
    assert down_rows * nj * ni == n_e * f_e and down_rows % 16 == 0
    up_spec = pl.BlockSpec((up_rows, f_e), lambda j, i: (j * ni + i, 0))
    down_spec = pl.BlockSpec((down_rows, d_e), lambda j, i: (j * ni + i, 0))
    m, eg_bf, eu_bf, ed_bf = pl.pallas_call(
        _upgate_kernel,
        grid=(nj, ni),
        in_specs=[pl.BlockSpec((tm, d), lambda j, i: (i, 0)),
                  pl.BlockSpec((tm, wa), lambda j, i: (i, 0)),
                  pl.BlockSpec((tm, wb), lambda j, i: (i, 0)),
                  pl.BlockSpec((d, tn), lambda j, i: (0, c0 + j)),
                  pl.BlockSpec((d, tn), lambda j, i: (0, c0 + nj + j)),
                  pl.BlockSpec((wa, tn), lambda j, i: (0, j)),
                  pl.BlockSpec((wb, tn), lambda j, i: (0, j)),
                  up_spec, up_spec, down_spec],
        out_specs=[pl.BlockSpec((tm, tn), lambda j, i: (i, j)), up_spec, up_spec, down_spec],
        out_shape=[jax.ShapeDtypeStruct((t, dout), BF16),
                   jax.ShapeDtypeStruct((n_e * d_e, f_e), BF16),
                   jax.ShapeDtypeStruct((n_e * d_e, f_e), BF16),
                   jax.ShapeDtypeStruct((n_e * f_e, d_e), BF16)],
        compiler_params=_params(2),
        name="upgate",
    )(xn, ya, yb, w_gates, w_gates, w_up_a, w_up_b,
      w_eg.reshape(n_e * d_e, f_e), w_eu.reshape(n_e * d_e, f_e), w_ed.reshape(n_e * f_e, d_e))
    return (m, eg_bf.reshape(n_e, d_e, f_e), eu_bf.reshape(n_e, d_e, f_e),
            ed_bf.reshape(n_e, f_e, d_e))


def _outproj_kernel(m_ref, w_ref, x_ref, o_ref):
    o_ref[...] = x_ref[...] + _dot(m_ref[...], w_ref[...])


def _outproj(m, w_out_bf, x, tm=512):
    t, d = m.shape
    dout = w_out_bf.shape[1]
    return pl.pallas_call(
        _outproj_kernel,
        grid=(t // tm,),
        in_specs=[pl.BlockSpec((tm, d), lambda i: (i, 0)),
                  pl.BlockSpec((d, dout), lambda i: (0, 0), pipeline_mode=pl.Buffered(1)),
                  pl.BlockSpec((tm, dout), lambda i: (i, 0))],
        out_specs=pl.BlockSpec((tm, dout), lambda i: (i, 0)),
        out_shape=jax.ShapeDtypeStruct((t, dout), F32),
        compiler_params=_params(1),
        name="outproj",
    )(m, w_out_bf, x)


def _argmax_rows(rows):
    best = rows[0]
    idx = jnp.zeros(rows[0].shape, jnp.int32)
    for k in range(1, len(rows)):
        better = rows[k] > best
        best = jnp.where(better, rows[k], best)
        idx = jnp.where(better, k, idx)
    return best, idx


def _softmax_rows(rows):
    mx = functools.reduce(jnp.maximum, rows)
    ex = [jnp.exp(r - mx) for r in rows]
    den = functools.reduce(lambda a, b: a + b, ex)
    return [e / den for e in ex]


def _route_sort_kernel(h_ref, g_ref, wr_ref, br_ref, xs_ref, meta_ref, cnt_ref, before_ref):
    tm = h_ref.shape[0]
    xn = _rms_scale(h_ref[...], g_ref[...])
    xn_hi = xn.astype(BF16)
    xn_lo = (xn - xn_hi.astype(F32)).astype(BF16)
    wr = wr_ref[...]
    wr_hi = wr.astype(BF16)
    wr_lo = (wr - wr_hi.astype(F32)).astype(BF16)
    nt_dims = (((1,), (1,)), ((), ()))
    lt = (lax.dot_general(wr_hi, xn_hi, nt_dims, preferred_element_type=F32)
          + lax.dot_general(wr_hi, xn_lo, nt_dims, preferred_element_type=F32)
          + lax.dot_general(wr_lo, xn_hi, nt_dims, preferred_element_type=F32)) + br_ref[...]
    pgs = _softmax_rows([lt[k:k + 1, :] for k in range(N_GROUPS)])
    pg, gi = _argmax_rows(pgs)
    sel = []
    for k in range(EXPERTS_PER_GROUP):
        r = jnp.zeros_like(pg)
        for g in range(N_GROUPS):
            row = N_GROUPS + g * EXPERTS_PER_GROUP + k
            r = jnp.where(gi == g, lt[row:row + 1, :], r)
        sel.append(r)
    pes = _softmax_rows(sel)
    p1, e1 = _argmax_rows(pes)
    rest = [jnp.where(e1 == k, -1.0, pes[k]) for k in range(EXPERTS_PER_GROUP)]
    p2, e2 = _argmax_rows(rest)
    den = p1 + p2
    w1 = pg * (p1 / den)
    w2 = pg * (p2 / den)
    lo = jnp.minimum(e1, e2)
    hi = jnp.maximum(e1, e2)
    w_lo = jnp.where(e1 < e2, w1, w2)
    w_hi = jnp.where(e1 < e2, w2, w1)
    ea = gi * EXPERTS_PER_GROUP + lo
    eb = gi * EXPERTS_PER_GROUP + hi

    erow = lax.broadcasted_iota(jnp.int32, (N_EXPERTS, tm), 0)
    oh_a = (erow == ea).astype(F32)
    oh_b = (erow == eb).astype(F32)

    @pl.when(pl.program_id(0) == 0)
    def _():
        a = lax.broadcasted_iota(jnp.int32, (tm, tm), 0)
        b = lax.broadcasted_iota(jnp.int32, (tm, tm), 1)
        before_ref[...] = (a < b).astype(BF16)

    cum = _dot((oh_a + oh_b).astype(BF16), before_ref[...])
    cnt = jnp.sum(oh_a + oh_b, axis=1, keepdims=True)
    padded = jnp.floor((cnt + (GROUP - 1)) * (1.0 / GROUP)) * GROUP
    pos_a = jnp.sum(oh_a * cum + jnp.where(erow < ea, padded, 0.0), axis=0, keepdims=True)
    pos_b = jnp.sum(oh_b * cum + jnp.where(erow < eb, padded, 0.0), axis=0, keepdims=True)

    d = h_ref.shape[1]

    def gate_rows(w):
        hi = w.astype(BF16).astype(F32)
        lo = w - hi
        k = lax.broadcasted_iota(jnp.int32, (LANES, tm), 0)
        return jnp.where(k == 0, hi, jnp.where(k == 1, lo, 0.0)).astype(BF16)

    gate_a = gate_rows(w_lo)
    gate_b = gate_rows(w_hi)

    def sort_rows(r0, n):
        q = r0 + lax.broadcasted_iota(jnp.int32, (n, tm), 0)
        perm_a = jnp.where(q == pos_a.astype(jnp.int32), 1.0, 0.0).astype(BF16)
        perm_b = jnp.where(q == pos_b.astype(jnp.int32), 1.0, 0.0).astype(BF16)
        xs_ref[r0:r0 + n, 0:d] = _dot(perm_a + perm_b, xn_hi).astype(xs_ref.dtype)
        gates = (lax.dot_general(perm_a, gate_a, nt_dims, preferred_element_type=F32)
                 + lax.dot_general(perm_b, gate_b, nt_dims, preferred_element_type=F32))
        xs_ref[r0:r0 + n, d:d + LANES] = gates.astype(xs_ref.dtype)

    n_rows = xs_ref.shape[0]
    head = n_rows - HALF_ROWS
    used = jnp.sum(padded)
    sort_rows(0, head)

    @pl.when(used > head)
    def _():
        sort_rows(head, HALF_ROWS)

    @pl.when(used <= head)
    def _():
        xs_ref[head:n_rows, :] = jnp.zeros((HALF_ROWS, xs_ref.shape[1]), xs_ref.dtype)

    cnt_ref[...] = jnp.broadcast_to(cnt, cnt_ref.shape)
    meta_ref[0:1, :] = pos_a
    meta_ref[1:2, :] = pos_b
    meta_ref[2:8, :] = jnp.zeros((6, tm), F32)


def _route_sort(h1, g, w_rg, b_rg, w_re, b_re):
    t, d = h1.shape
    tm = ROUTE_BLOCK
    nb = t // tm
    n_log = w_rg.shape[1] + w_re.shape[1]
    wr = jnp.concatenate([w_rg, w_re], axis=1).T
    wr = jnp.pad(wr, ((0, ROUTER_ROWS - n_log), (0, 0)))
    br = jnp.pad(jnp.concatenate([b_rg, b_re]), (0, ROUTER_ROWS - n_log)).reshape(ROUTER_ROWS, 1)
    return pl.pallas_call(
        _route_sort_kernel,
        grid=(nb,),
        in_specs=[pl.BlockSpec((tm, d), lambda i: (i, 0)),
                  pl.BlockSpec((1, d), lambda i: (0, 0)),
                  pl.BlockSpec((ROUTER_ROWS, d), lambda i: (0, 0)),
                  pl.BlockSpec((ROUTER_ROWS, 1), lambda i: (0, 0))],
        out_specs=[pl.BlockSpec((LOCAL_ROWS, d + LANES), lambda i: (i, 0)),
                   pl.BlockSpec((8, tm), lambda i: (0, i)),
                   pl.BlockSpec((N_EXPERTS, LANES), lambda i: (i, 0))],
        out_shape=[jax.ShapeDtypeStruct((nb * LOCAL_ROWS, d + LANES), BF16),
                   jax.ShapeDtypeStruct((8, t), F32),
                   jax.ShapeDtypeStruct((nb * N_EXPERTS, LANES), F32)],
        scratch_shapes=[pltpu.VMEM((tm, tm), BF16)],
        compiler_params=_params(1),
        name="route_sort",
    )(h1, g.reshape(1, d), wr, br)


def _group_copy(src_ref, s_group, dst_ref, d_group, sem):
    return pltpu.make_async_copy(src_ref.at[pl.ds(pl.multiple_of(s_group * GROUP, GROUP), GROUP), :],
                                 dst_ref.at[pl.ds(pl.multiple_of(d_group * GROUP, GROUP), GROUP), :], sem)


def _expert_kernel(te_ref, nxt_ref, nreal_ref, nt_ref, gsrc_ref, gdst_ref, tail_ref,
                   xs_ref, wg_ref, wu_ref, wd_ref, ys_ref,
                   xbuf_ref, ybuf_ref, zero_ref, wgu_buf_ref, wd_buf_ref,
                   cur_ref, gsem, ssem, zsem, wsem):
    r = pl.program_id(0)
    nt = nt_ref[0]

    def gather(q, slot):
        for i in range(TILE_GROUPS):
            _group_copy(xs_ref, gsrc_ref[q * TILE_GROUPS + i], xbuf_ref, slot * TILE_GROUPS + i,
                        gsem.at[slot]).start(priority=1)

    def weight_copies(e, wslot):
        f = wg_ref.shape[2]
        return (pltpu.make_async_copy(wg_ref.at[e], wgu_buf_ref.at[wslot, :, pl.ds(0, f)], wsem.at[wslot]),
                pltpu.make_async_copy(wu_ref.at[e], wgu_buf_ref.at[wslot, :, pl.ds(f, f)], wsem.at[wslot]),
                pltpu.make_async_copy(wd_ref.at[e], wd_buf_ref.at[wslot], wsem.at[wslot]))

    def tile_wait(src, dst, sem):
        pltpu.make_async_copy(src.at[pl.ds(0, MOE_TILE), :], dst.at[pl.ds(0, MOE_TILE), :], sem).wait()

    @pl.when(r == 0)
    def _():
        cur_ref[0] = -1
        cur_ref[1] = -1
        for cp in weight_copies(te_ref[0], 0):
            cp.start()
        gather(0, 0)
        zero_ref[...] = jnp.zeros_like(zero_ref)
        ybuf_ref[...] = jnp.zeros_like(ybuf_ref)
        n_blocks = tail_ref.shape[0]

        def fill(make):
            def body(g, c):
                make(g)
                return c
            return body

        for blk in range(n_blocks):
            lax.fori_loop(tail_ref[blk], GROUPS_PER_BLOCK, fill(
                lambda g, blk=blk: _group_copy(zero_ref, 0, ys_ref, blk * GROUPS_PER_BLOCK + g, zsem).start()), 0)
        for blk in range(n_blocks):
            lax.fori_loop(tail_ref[blk], GROUPS_PER_BLOCK, fill(
                lambda g, blk=blk: _group_copy(zero_ref, 0, ys_ref, blk * GROUPS_PER_BLOCK + g, zsem).wait()), 0)

    @pl.when(r < nt)
    def _():
        e = te_ref[r]
        slot = r % 2

        gather(jnp.minimum(r + 1, nt - 1), 1 - slot)

        @pl.when(cur_ref[0] != e)
        def _():
            wslot = (cur_ref[1] + 1) % 2
            for cp in weight_copies(e, wslot):
                cp.wait()
            cur_ref[0] = e
            cur_ref[1] = cur_ref[1] + 1

            @pl.when(nxt_ref[r] >= 0)
            def _():
                for cp in weight_copies(nxt_ref[r], 1 - wslot):
                    cp.start()

        wslot = cur_ref[1] % 2
        tile_wait(xs_ref, xbuf_ref, gsem.at[slot])

        @pl.when(r >= 2)
        def _():
            tile_wait(ybuf_ref, ys_ref, ssem.at[slot])

        d = ybuf_ref.shape[1]
        f = wd_buf_ref.shape[1]
        def chain(row0, n_rows):
            rows = pl.ds(pl.multiple_of(slot * MOE_TILE + row0, HALF_ROWS), n_rows)
            x = xbuf_ref[rows, 0:d]
            gate_parts = xbuf_ref[rows, d:d + LANES].astype(F32)
            gate = gate_parts[:, 0:1] + gate_parts[:, 1:2]
            gu = _dot(x, wgu_buf_ref[wslot])
            hid = (jax.nn.silu(gu[:, 0:f]) * gu[:, f:2 * f]).astype(BF16)
            ybuf_ref[rows, :] = (_dot(hid, wd_buf_ref[wslot]) * gate).astype(ybuf_ref.dtype)

        half_groups = HALF_ROWS // GROUP
        n_halves = (nreal_ref[r] + half_groups - 1) // half_groups
        for count in range(1, MOE_TILE // HALF_ROWS + 1):
            @pl.when(n_halves == count)
            def _(count=count):
                for part in range(count // 2):
                    chain(part * MXU_ROWS, MXU_ROWS)
                if count % 2:
                    chain((count // 2) * MXU_ROWS, HALF_ROWS)

        for i in range(TILE_GROUPS):
            _group_copy(ybuf_ref, slot * TILE_GROUPS + i, ys_ref, gdst_ref[r * TILE_GROUPS + i],
                        ssem.at[slot]).start(priority=1)

        @pl.when(r == nt - 1)
        def _():
            tile_wait(xs_ref, xbuf_ref, gsem.at[1 - slot])
            tile_wait(ybuf_ref, ys_ref, ssem.at[slot])

            @pl.when(r >= 1)
            def _():
                tile_wait(ybuf_ref, ys_ref, ssem.at[1 - slot])


def _experts(plan, xs, w_gate, w_up, w_down):
    d = w_gate.shape[1]
    f = w_gate.shape[2]
    tile_e, next_e, n_real, n_tiles, gsrc, gdst, tail = plan
    r_max = tile_e.shape[0]
    n_blocks = tail.shape[0]
    any_spec = pl.BlockSpec(memory_space=pl.ANY)
    grid_spec = pltpu.PrefetchScalarGridSpec(
        num_scalar_prefetch=7,
        grid=(r_max,),
        in_specs=[any_spec, any_spec, any_spec, any_spec],
        out_specs=any_spec,
        scratch_shapes=[pltpu.VMEM((2 * MOE_TILE, d + LANES), BF16),
                        pltpu.VMEM((2 * MOE_TILE, d), BF16),
                        pltpu.VMEM((GROUP, d), BF16),
                        pltpu.VMEM((2, d, 2 * f), BF16),
                        pltpu.VMEM((2, f, d), BF16),
                        pltpu.SMEM((2,), jnp.int32),
                        pltpu.SemaphoreType.DMA((2,)),
                        pltpu.SemaphoreType.DMA((2,)),
                        pltpu.SemaphoreType.DMA(()),
                        pltpu.SemaphoreType.DMA((2,))],
    )
    return pl.pallas_call(
        _expert_kernel,
        grid_spec=grid_spec,
        out_shape=jax.ShapeDtypeStruct((n_blocks * LOCAL_ROWS, d), BF16),
        compiler_params=_params(1),
        name="experts",
    )(tile_e, next_e, n_real, n_tiles, gsrc, gdst, tail, xs, w_gate, w_up, w_down)


def _combine_kernel(ys_ref, h_ref, cm_ref, g_ref, o_ref, ybuf_ref, hbuf_ref, sem):
    n = pl.num_programs(0)
    tm = o_ref.shape[0]
    rows = ybuf_ref.shape[1]
    slot = _ring_fetch((ys_ref, h_ref), (ybuf_ref, hbuf_ref), sem, pl.program_id(0), n, n)
    cm = cm_ref[...].T
    q = lax.broadcasted_iota(jnp.int32, (tm, rows), 1)
    sel = jnp.where((q == cm[:, 0:1].astype(jnp.int32)) | (q == cm[:, 1:2].astype(jnp.int32)), 1.0, 0.0)
    h2 = hbuf_ref[slot] + _dot(sel.astype(BF16), ybuf_ref[slot])
    o_ref[...] = _rms_scale(h2, g_ref[...])


def _combine(ys, h1, meta, g):
    t, d = h1.shape
    tm = ROUTE_BLOCK
    assert t // tm >= RING - 1
    any_spec = pl.BlockSpec(memory_space=pl.ANY)
    return pl.pallas_call(
        _combine_kernel,
        grid=(t // tm,),
        in_specs=[any_spec, any_spec,
                  pl.BlockSpec((8, tm), lambda i: (0, i)),
                  pl.BlockSpec((1, d), lambda i: (0, 0))],
        out_specs=pl.BlockSpec((tm, d), lambda i: (i, 0)),
        out_shape=jax.ShapeDtypeStruct((t, d), F32),
        scratch_shapes=[pltpu.VMEM((RING, LOCAL_ROWS, d), BF16),
                        pltpu.VMEM((RING, tm, d), F32),
                        pltpu.SemaphoreType.DMA((RING,))],
        compiler_params=_params(1),
        name="combine",
    )(ys, h1, meta, g.reshape(1, d))


def _expert_plan(counts, n_blocks):
    cnt = counts[:, 0].astype(jnp.int32).reshape(n_blocks, N_EXPERTS)
    groups = (cnt + GROUP - 1) // GROUP
    first = jnp.cumsum(groups, axis=1) - groups
    upto = jnp.cumsum(groups, axis=0)
    per_expert = upto[-1]
    tiles_e = (per_expert + TILE_GROUPS - 1) // TILE_GROUPS
    tile_end = jnp.cumsum(tiles_e)
    n_tiles = tile_end[-1]
    max_groups = 2 * ROUTE_BLOCK * n_blocks // GROUP + n_blocks * N_EXPERTS
    r_max = max_groups // TILE_GROUPS + N_EXPERTS
    tile_ids = jnp.arange(r_max, dtype=jnp.int32)
    tile = jnp.minimum(tile_ids, n_tiles - 1)
    tile_e = jnp.sum((tile_end[None, :] <= tile[:, None]).astype(jnp.int32), axis=1)
    later = (tile_e[None, :] > tile_e[:, None]) & (tile_ids[None, :] < n_tiles)
    next_e = jnp.min(jnp.where(later, tile_e[None, :], N_EXPERTS), axis=1)
    next_e = jnp.where(next_e == N_EXPERTS, -1, next_e)

    slot = jnp.arange(r_max * TILE_GROUPS, dtype=jnp.int32)
    s_tile = slot // TILE_GROUPS
    oh_e = jnp.repeat(tile_e, TILE_GROUPS)[:, None] == jnp.arange(N_EXPERTS, dtype=jnp.int32)[None, :]

    def by_expert(table):
        return jnp.sum(jnp.where(oh_e[:, None, :], table[None], 0), axis=-1)

    k = slot - by_expert(((tile_end - tiles_e) * TILE_GROUPS)[None, :])[:, 0]
    real = (k < by_expert(per_expert[None, :])[:, 0]) & (s_tile < n_tiles)
    upto_e = by_expert(upto)
    blk = jnp.minimum(jnp.sum((upto_e <= k[:, None]).astype(jnp.int32), axis=1), n_blocks - 1)
    oh_b = blk[:, None] == jnp.arange(n_blocks, dtype=jnp.int32)[None, :]

    def by_block(table_se):
        return jnp.sum(jnp.where(oh_b, table_se, 0), axis=1)

    before = by_block(upto_e - by_expert(groups))
    src = blk * GROUPS_PER_BLOCK + by_block(by_expert(first)) + (k - before)
    zero_group = GROUPS_PER_BLOCK - 1
    spare = n_blocks * GROUPS_PER_BLOCK + (s_tile % 2) * TILE_GROUPS + slot % TILE_GROUPS
    gsrc = jnp.where(real, src, zero_group)
    gdst = jnp.where(real, src, spare)
    n_spare_blocks = -(-2 * TILE_GROUPS // GROUPS_PER_BLOCK)
    tail = jnp.concatenate([jnp.sum(groups, axis=1), jnp.zeros((n_spare_blocks,), jnp.int32)])
    n_real = jnp.sum(real.reshape(r_max, TILE_GROUPS).astype(jnp.int32), axis=1)
    return tile_e, next_e, n_real, n_tiles.reshape(1), gsrc, gdst, tail


def _layer(h, norm_mix_g, w_in, conv_w, sgu_ln_g, sgu_ln_b, sgu_w_s, sgu_b_s, w_up_conv,
           w_up_sgu, w_out, norm_ffn_g, w_rg, b_rg, w_re, b_re, w_eg, w_eu, w_ed, out_g):
    t, d = h.shape
    conv_width = conv_w.shape[1]
    sgu_width = sgu_ln_g.shape[0]
    xn, yb = _sgu_branch(h, norm_mix_g, w_in, 3 * conv_width, sgu_ln_g, sgu_ln_b, sgu_w_s, sgu_b_s)
    gate_col0 = 3 * conv_width + 2 * sgu_width
    ya, (wg_bf, wua_bf, wub_bf, wout_bf) = _conv_branch(
        xn, w_in, conv_w, conv_width,
        to_cast=[(w_in, gate_col0, w_in.shape[1] - gate_col0, math.gcd(gate_col0, w_in.shape[1])),
                 (w_up_conv, 0, d, d), (w_up_sgu, 0, d, d), (w_out, 0, d, d)])
    m, eg_bf, eu_bf, ed_bf = _upgate(xn, ya, yb, wg_bf, wua_bf, wub_bf, w_eg, w_eu, w_ed)
    h1 = _outproj(m, wout_bf, h)
    xs, meta, counts = _route_sort(h1, norm_ffn_g, w_rg, b_rg, w_re, b_re)
    ys = _experts(_expert_plan(counts, t // ROUTE_BLOCK), xs, eg_bf, eu_bf, ed_bf)
    return _combine(ys, h1, meta, out_g)


def kernel(x, norm_mix_g, w_in, conv_w, sgu_ln_g, sgu_ln_b, sgu_w_s, sgu_b_s, w_up_conv, w_up_sgu, w_out, norm_ffn_g, w_router_group, b_router_group, w_router_expert, b_router_expert, w_exp_gate, w_exp_up, w_exp_down, norm_final_g):
    bsz, s, d = x.shape
    depth = w_in.shape[0]
    assert bsz == 1 and depth == 1, "causal conv carry and the fused final norm assume one sequence, one layer"
    assert s % ROUTE_BLOCK == 0
    out = _layer(x.reshape(s, d), norm_mix_g[0], w_in[0], conv_w[0], sgu_ln_g[0], sgu_ln_b[0],
                 sgu_w_s[0], sgu_b_s[0], w_up_conv[0], w_up_sgu[0], w_out[0], norm_ffn_g[0],
                 w_router_group[0], b_router_group[0], w_router_expert[0], b_router_expert[0],
                 w_exp_gate[0], w_exp_up[0], w_exp_down[0], norm_final_g)
    return out.reshape(bsz, s, d)
```

```python
import functools
import math

import jax
import jax.numpy as jnp
from jax import lax
from jax.experimental import pallas as pl
from jax.experimental.pallas import tpu as pltpu

F32 = jnp.float32
BF16 = jnp.bfloat16

EPS = 1e-6
CHUNK = 64
CONV_K = 3
SGU_HEADS = 8
SGU_BLOCK = 128
N_GROUPS = 4
EXPERTS_PER_GROUP = 4
N_EXPERTS = N_GROUPS * EXPERTS_PER_GROUP
ROUTER_ROWS = 32
LANES = 128

VMEM_LIMIT_BYTES = 56 * 1024 * 1024

ROUTE_BLOCK = 512
GROUP = 16
MXU_ROWS = 256
HALF_ROWS = MXU_ROWS // 2
LOCAL_ROWS = -(-(2 * ROUTE_BLOCK + N_EXPERTS * (GROUP - 1)) // MXU_ROWS) * MXU_ROWS
GROUPS_PER_BLOCK = LOCAL_ROWS // GROUP
assert LOCAL_ROWS - (2 * ROUTE_BLOCK + N_EXPERTS * (GROUP - 1)) >= GROUP
MOE_TILE = 512
TILE_GROUPS = MOE_TILE // GROUP


def _params(n_axes):
    return pltpu.CompilerParams(
        dimension_semantics=("arbitrary",) * n_axes,
        vmem_limit_bytes=VMEM_LIMIT_BYTES)


def _dot(a, b):
    return jnp.dot(a, b, preferred_element_type=F32)


def _rms_scale(x, g):
    ms = jnp.mean(x * x, axis=-1, keepdims=True)
    return x * lax.rsqrt(ms + EPS) * g


RING = 3


def _ring_fetch(srcs, bufs, sem, step, n_steps, n_blocks):
    def copies(s):
        blk = s % n_blocks
        slot = s % RING
        return [pltpu.make_async_copy(
            src.at[pl.ds(pl.multiple_of(blk * buf.shape[1], buf.shape[1]), buf.shape[1]), :],
            buf.at[slot], sem.at[slot]) for src, buf in zip(srcs, bufs)]

    @pl.when(step == 0)
    def _():
        for s in range(RING - 1):
            for cp in copies(s):
                cp.start()

    @pl.when(step + RING - 1 < n_steps)
    def _():
        for cp in copies(step + RING - 1):
            cp.start()

    for cp in copies(step):
        cp.wait()
    return step % RING


def _conv_kernel(cast_chunks, xn_ref, wb_ref, wc_ref, wh_ref, cw_ref, *rest):
    n_in = sum(cast_chunks)
    cast_in, o_ref = rest[:n_in], rest[n_in]
    cast_out = rest[n_in + 1:n_in + 1 + len(cast_chunks)]
    wbf_ref, carry_ref, xnbuf_ref, xsem = rest[n_in + 1 + len(cast_chunks):]
    i = pl.program_id(1)
    ni = pl.num_programs(1)
    tn = wb_ref.shape[1]
    tm = xnbuf_ref.shape[1]
    slot = _ring_fetch((xn_ref,), (xnbuf_ref,), xsem, pl.program_id(0) * ni + i,
                       pl.num_programs(0) * ni, ni)
    src = iter(cast_in)
    for dst, n_chunks in zip(cast_out, cast_chunks):
        wc = dst.shape[1] // n_chunks
        for k in range(n_chunks):
            dst[:, k * wc:(k + 1) * wc] = next(src)[...].astype(BF16)

    @pl.when(i == 0)
    def _():
        wbf_ref[:, 0:tn] = wb_ref[...].astype(BF16)
        wbf_ref[:, tn:2 * tn] = wc_ref[...].astype(BF16)
        wbf_ref[:, 2 * tn:3 * tn] = wh_ref[...].astype(BF16)
        carry_ref[...] = jnp.zeros_like(carry_ref)

    proj = _dot(xnbuf_ref[slot], wbf_ref[...])
    b = proj[:, 0:tn]
    p = proj[:, tn:2 * tn] * proj[:, 2 * tn:3 * tn]
    prev = carry_ref[...]
    carry_ref[...] = p[tm - 8:tm, :]
    row = lax.broadcasted_iota(jnp.int32, p.shape, 0)
    p1 = jnp.where(row == 0, prev[7:8, :], pltpu.roll(p, 1, axis=0))
    p2 = jnp.where(row == 0, prev[6:7, :],
                   jnp.where(row == 1, prev[7:8, :], pltpu.roll(p, 2, axis=0)))
    cw = cw_ref[...]
    y = b * (cw[0:1, :] * p2 + cw[1:2, :] * p1 + cw[2:3, :] * p)
    o_ref[...] = y.astype(o_ref.dtype)


def _conv_branch(xn, w_in, conv_w, width, to_cast, tm=1024, tn=256):
    t, d = xn.shape
    nj = width // tn
    ni = t // tm
    steps = nj * ni
    assert steps >= RING - 1
    cast_args, cast_in_specs, cast_out_specs, cast_shapes, cast_chunks = [], [], [], [], []
    for arr, col0, n_cols, chunk in to_cast:
        rows = arr.shape[0] // steps
        assert rows * steps == arr.shape[0] and rows % 16 == 0
        assert col0 % chunk == 0 and n_cols % chunk == 0
        for k in range(n_cols // chunk):
            cast_args.append(arr)
            cast_in_specs.append(
                pl.BlockSpec((rows, chunk), lambda j, i, c=col0 // chunk + k: (j * ni + i, c)))
        cast_chunks.append(n_cols // chunk)
        cast_out_specs.append(pl.BlockSpec((rows, n_cols), lambda j, i: (j * ni + i, 0)))
        cast_shapes.append(jax.ShapeDtypeStruct((arr.shape[0], n_cols), BF16))
    outs = pl.pallas_call(
        functools.partial(_conv_kernel, tuple(cast_chunks)),
        grid=(nj, ni),
        in_specs=[pl.BlockSpec(memory_space=pl.ANY),
                  pl.BlockSpec((d, tn), lambda j, i: (0, j)),
                  pl.BlockSpec((d, tn), lambda j, i: (0, nj + j)),
                  pl.BlockSpec((d, tn), lambda j, i: (0, 2 * nj + j)),
                  pl.BlockSpec((CONV_K, tn), lambda j, i: (0, j))] + cast_in_specs,
        out_specs=[pl.BlockSpec((tm, tn), lambda j, i: (i, j))] + cast_out_specs,
        out_shape=[jax.ShapeDtypeStruct((t, width), BF16)] + cast_shapes,
        scratch_shapes=[pltpu.VMEM((d, 3 * tn), BF16),
                        pltpu.VMEM((8, tn), F32),
                        pltpu.VMEM((RING, tm, d), BF16),
                        pltpu.SemaphoreType.DMA((RING,))],
        compiler_params=_params(2),
        name="conv_branch",
    )(xn, w_in, w_in, w_in, conv_w, *cast_args)
    return outs[0], outs[1:]


def _sgu_kernel(x_ref, g_ref, wu_ref, wv_ref, lng_ref, lnb_ref, ws_ref, bs_ref, xn_ref, o_ref, wbf_ref):
    tm = x_ref.shape[0]
    w = o_ref.shape[1]
    hd = w // SGU_HEADS

    @pl.when(pl.program_id(0) == 0)
    def _():
        wbf_ref[:, 0:w] = wu_ref[...].astype(BF16)
        wbf_ref[:, w:2 * w] = wv_ref[...].astype(BF16)

    xn = _rms_scale(x_ref[...], g_ref[...]).astype(BF16)
    xn_ref[...] = xn
    gz = jax.nn.gelu(_dot(xn, wbf_ref[...]))
    v = gz[:, w:2 * w]
    mu = jnp.mean(v, axis=-1, keepdims=True)
    vc = v - mu
    var = jnp.mean(vc * vc, axis=-1, keepdims=True)
    vn = (vc * lax.rsqrt(var + EPS) * lng_ref[...] + lnb_ref[...]).astype(BF16)
    ii = lax.broadcasted_iota(jnp.int32, (SGU_BLOCK, SGU_BLOCK), 0)
    jj = lax.broadcasted_iota(jnp.int32, (SGU_BLOCK, SGU_BLOCK), 1)
    mask = (jj // CHUNK) <= (ii // CHUNK)
    bs_t = bs_ref[...].T
    for h in range(SGU_HEADS):
        wm = jnp.where(mask, ws_ref[h], 0.0).astype(BF16)
        cs = slice(h * hd, (h + 1) * hd)
        bias = jnp.broadcast_to(bs_t[:, h:h + 1], (SGU_BLOCK, hd))
        for n in range(tm // SGU_BLOCK):
            rs = slice(n * SGU_BLOCK, (n + 1) * SGU_BLOCK)
            vm = _dot(wm, vn[rs, cs]) + bias
            o_ref[rs, cs] = (gz[rs, cs] * vm).astype(o_ref.dtype)


def _sgu_branch(x, g, w_in, col0, ln_g, ln_b, w_s, b_s, tm=512):
    t, d = x.shape
    w = ln_g.shape[0]
    assert col0 % w == 0
    c0 = col0 // w
    once = pl.Buffered(1)
    return pl.pallas_call(
        _sgu_kernel,
        grid=(t // tm,),
        in_specs=[pl.BlockSpec((tm, d), lambda i: (i, 0)),
                  pl.BlockSpec((1, d), lambda i: (0, 0)),
                  pl.BlockSpec((d, w), lambda i: (0, c0), pipeline_mode=once),
                  pl.BlockSpec((d, w), lambda i: (0, c0 + 1), pipeline_mode=once),
                  pl.BlockSpec((1, w), lambda i: (0, 0)),
                  pl.BlockSpec((1, w), lambda i: (0, 0)),
                  pl.BlockSpec((SGU_HEADS, SGU_BLOCK, SGU_BLOCK), lambda i: (0, 0, 0)),
                  pl.BlockSpec((SGU_HEADS, SGU_BLOCK), lambda i: (0, 0))],
        out_specs=[pl.BlockSpec((tm, d), lambda i: (i, 0)),
                   pl.BlockSpec((tm, w), lambda i: (i, 0))],
        out_shape=[jax.ShapeDtypeStruct((t, d), BF16),
                   jax.ShapeDtypeStruct((t, w), BF16)],
        scratch_shapes=[pltpu.VMEM((d, 2 * w), BF16)],
        compiler_params=_params(1),
        name="sgu_branch",
    )(x, g.reshape(1, d), w_in, w_in, ln_g.reshape(1, w), ln_b.reshape(1, w), w_s, b_s)


def _upgate_kernel(xn_ref, ya_ref, yb_ref, wgc_ref, wgs_ref, wua_ref, wub_ref, eg_ref, eu_ref, ed_ref,
                   o_ref, eg_bf_ref, eu_bf_ref, ed_bf_ref):
    eg_bf_ref[...] = eg_ref[...].astype(BF16)
    eu_bf_ref[...] = eu_ref[...].astype(BF16)
    ed_bf_ref[...] = ed_ref[...].astype(BF16)

    xn = xn_ref[...]
    m = (jax.nn.sigmoid(_dot(xn, wgc_ref[...])) * _dot(ya_ref[...], wua_ref[...])
         + jax.nn.sigmoid(_dot(xn, wgs_ref[...])) * _dot(yb_ref[...], wub_ref[...]))
    o_ref[...] = m.astype(o_ref.dtype)


def _upgate(xn, ya, yb, w_gates, w_up_a, w_up_b, w_eg, w_eu, w_ed, tm=1024, tn=512):
    t, d = xn.shape
    wa = ya.shape[1]
    wb = yb.shape[1]
    dout = w_up_a.shape[1]
    c0 = 0
    nj = dout // tn
    ni = t // tm
    n_e, d_e, f_e = w_eg.shape
    up_rows = n_e * d_e // (nj * ni)
    down_rows = n_e * f_e // (nj * ni)
    assert up_rows * nj * ni == n_e * d_e and up_rows % 16 == 0
    assert down_rows * nj * ni == n_e * f_e and down_rows % 16 == 0
    up_spec = pl.BlockSpec((up_rows, f_e), lambda j, i: (j * ni + i, 0))
    down_spec = pl.BlockSpec((down_rows, d_e), lambda j, i: (j * ni + i, 0))
    m, eg_bf, eu_bf, ed_bf = pl.pallas_call(
        _upgate_kernel,
        grid=(nj, ni),
        in_specs=[pl.BlockSpec((tm, d), lambda j, i: (i, 0)),
                  pl.BlockSpec((tm, wa), lambda j, i: (i, 0)),
                  pl.BlockSpec((tm, wb), lambda j, i: (i, 0)),
                  pl.BlockSpec((d, tn), lambda j, i: (0, c0 + j)),
                  pl.BlockSpec((d, tn), lambda j, i: (0, c0 + nj + j)),
                  pl.BlockSpec((wa, tn), lambda j, i: (0, j)),
                  pl.BlockSpec((wb, tn), lambda j, i: (0, j)),
                  up_spec, up_spec, down_spec],
        out_specs=[pl.BlockSpec((tm, tn), lambda j, i: (i, j)), up_spec, up_spec, down_spec],
        out_shape=[jax.ShapeDtypeStruct((t, dout), BF16),
                   jax.ShapeDtypeStruct((n_e * d_e, f_e), BF16),
                   jax.ShapeDtypeStruct((n_e * d_e, f_e), BF16),
                   jax.ShapeDtypeStruct((n_e * f_e, d_e), BF16)],
        compiler_params=_params(2),
        name="upgate",
    )(xn, ya, yb, w_gates, w_gates, w_up_a, w_up_b,
      w_eg.reshape(n_e * d_e, f_e), w_eu.reshape(n_e * d_e, f_e), w_ed.reshape(n_e * f_e, d_e))
    return (m, eg_bf.reshape(n_e, d_e, f_e), eu_bf.reshape(n_e, d_e, f_e),
            ed_bf.reshape(n_e, f_e, d_e))


def _outproj_kernel(m_ref, w_ref, x_ref, o_ref):
    o_ref[...] = x_ref[...] + _dot(m_ref[...], w_ref[...])


def _outproj(m, w_out_bf, x, tm=512):
    t, d = m.shape
    dout = w_out_bf.shape[1]
    return pl.pallas_call(
        _outproj_kernel,
        grid=(t // tm,),
        in_specs=[pl.BlockSpec((tm, d), lambda i: (i, 0)),
                  pl.BlockSpec((d, dout), lambda i: (0, 0), pipeline_mode=pl.Buffered(1)),
                  pl.BlockSpec((tm, dout), lambda i: (i, 0))],
        out_specs=pl.BlockSpec((tm, dout), lambda i: (i, 0)),
        out_shape=jax.ShapeDtypeStruct((t, dout), F32),
        compiler_params=_params(1),
        name="outproj",
    )(m, w_out_bf, x)


def _argmax_rows(rows):
    best = rows[0]
    idx = jnp.zeros(rows[0].shape, jnp.int32)
    for k in range(1, len(rows)):
        better = rows[k] > best
        best = jnp.where(better, rows[k], best)
        idx = jnp.where(better, k, idx)
    return best, idx


def _softmax_rows(rows):
    mx = functools.reduce(jnp.maximum, rows)
    ex = [jnp.exp(r - mx) for r in rows]
    den = functools.reduce(lambda a, b: a + b, ex)
    return [e / den for e in ex]


def _route_sort_kernel(h_ref, g_ref, wrg_ref, wre_ref, brg_ref, bre_ref, xs_ref, meta_ref, cnt_ref,
                       before_ref, wsrc_ref, wt_ref):
    tm = h_ref.shape[0]
    xn = _rms_scale(h_ref[...], g_ref[...])
    xn_hi = xn.astype(BF16)
    xn_lo = (xn - xn_hi.astype(F32)).astype(BF16)
    d = h_ref.shape[1]
    n_g = wrg_ref.shape[1]
    n_e = wre_ref.shape[1]

    @pl.when(pl.program_id(0) == 0)
    def _():
        wsrc_ref[...] = jnp.zeros_like(wsrc_ref)
        wsrc_ref[0:d, 0:n_g] = wrg_ref[...]
        wsrc_ref[0:d, n_g:n_g + n_e] = wre_ref[...]
        wsrc_ref[d:d + 1, 0:n_g] = brg_ref[...]
        wsrc_ref[d:d + 1, n_g:n_g + n_e] = bre_ref[...]
        wt_ref[...] = wsrc_ref[...].T[0:ROUTER_ROWS, :]

    wr = wt_ref[:, 0:d]
    wr_hi = wr.astype(BF16)
    wr_lo = (wr - wr_hi.astype(F32)).astype(BF16)
    nt_dims = (((1,), (1,)), ((), ()))
    lt = (lax.dot_general(wr_hi, xn_hi, nt_dims, preferred_element_type=F32)
          + lax.dot_general(wr_hi, xn_lo, nt_dims, preferred_element_type=F32)
          + lax.dot_general(wr_lo, xn_hi, nt_dims, preferred_element_type=F32)) + wt_ref[:, d:d + 1]
    pgs = _softmax_rows([lt[k:k + 1, :] for k in range(N_GROUPS)])
    pg, gi = _argmax_rows(pgs)
    sel = []
    for k in range(EXPERTS_PER_GROUP):
        r = jnp.zeros_like(pg)
        for g in range(N_GROUPS):
            row = N_GROUPS + g * EXPERTS_PER_GROUP + k
            r = jnp.where(gi == g, lt[row:row + 1, :], r)
        sel.append(r)
    pes = _softmax_rows(sel)
    p1, e1 = _argmax_rows(pes)
    rest = [jnp.where(e1 == k, -1.0, pes[k]) for k in range(EXPERTS_PER_GROUP)]
    p2, e2 = _argmax_rows(rest)
    den = p1 + p2
    w1 = pg * (p1 / den)
    w2 = pg * (p2 / den)
    lo = jnp.minimum(e1, e2)
    hi = jnp.maximum(e1, e2)
    w_lo = jnp.where(e1 < e2, w1, w2)
    w_hi = jnp.where(e1 < e2, w2, w1)
    ea = gi * EXPERTS_PER_GROUP + lo
    eb = gi * EXPERTS_PER_GROUP + hi

    erow = lax.broadcasted_iota(jnp.int32, (N_EXPERTS, tm), 0)
    oh_a = (erow == ea).astype(F32)
    oh_b = (erow == eb).astype(F32)

    @pl.when(pl.program_id(0) == 0)
    def _():
        a = lax.broadcasted_iota(jnp.int32, (tm, tm), 0)
        b = lax.broadcasted_iota(jnp.int32, (tm, tm), 1)
        before_ref[...] = (a < b).astype(BF16)

    cum = _dot((oh_a + oh_b).astype(BF16), before_ref[...])
    cnt = jnp.sum(oh_a + oh_b, axis=1, keepdims=True)
    padded = jnp.floor((cnt + (GROUP - 1)) * (1.0 / GROUP)) * GROUP
    pos_a = jnp.sum(oh_a * cum + jnp.where(erow < ea, padded, 0.0), axis=0, keepdims=True)
    pos_b = jnp.sum(oh_b * cum + jnp.where(erow < eb, padded, 0.0), axis=0, keepdims=True)


    def gate_rows(w):
        hi = w.astype(BF16).astype(F32)
        lo = w - hi
        k = lax.broadcasted_iota(jnp.int32, (LANES, tm), 0)
        return jnp.where(k == 0, hi, jnp.where(k == 1, lo, 0.0)).astype(BF16)

    gate_a = gate_rows(w_lo)
    gate_b = gate_rows(w_hi)

    def sort_rows(r0, n):
        q = r0 + lax.broadcasted_iota(jnp.int32, (n, tm), 0)
        perm_a = jnp.where(q == pos_a.astype(jnp.int32), 1.0, 0.0).astype(BF16)
        perm_b = jnp.where(q == pos_b.astype(jnp.int32), 1.0, 0.0).astype(BF16)
        xs_ref[r0:r0 + n, 0:d] = _dot(perm_a + perm_b, xn_hi).astype(xs_ref.dtype)
        gates = (lax.dot_general(perm_a, gate_a, nt_dims, preferred_element_type=F32)
                 + lax.dot_general(perm_b, gate_b, nt_dims, preferred_element_type=F32))
        xs_ref[r0:r0 + n, d:d + LANES] = gates.astype(xs_ref.dtype)

    n_rows = xs_ref.shape[0]
    head = n_rows - HALF_ROWS
    used = jnp.sum(padded)
    sort_rows(0, head)

    @pl.when(used > head)
    def _():
        sort_rows(head, HALF_ROWS)

    @pl.when(used <= head)
    def _():
        xs_ref[head:n_rows, :] = jnp.zeros((HALF_ROWS, xs_ref.shape[1]), xs_ref.dtype)

    cnt_ref[...] = jnp.broadcast_to(cnt, cnt_ref.shape)
    meta_ref[0:1, :] = pos_a
    meta_ref[1:2, :] = pos_b
    meta_ref[2:8, :] = jnp.zeros((6, tm), F32)


def _route_sort(h1, g, w_rg, b_rg, w_re, b_re):
    t, d = h1.shape
    tm = ROUTE_BLOCK
    nb = t // tm
    n_g, n_e = w_rg.shape[1], w_re.shape[1]
    assert n_g + n_e <= ROUTER_ROWS
    once = pl.Buffered(1)
    return pl.pallas_call(
        _route_sort_kernel,
        grid=(nb,),
        in_specs=[pl.BlockSpec((tm, d), lambda i: (i, 0)),
                  pl.BlockSpec((1, d), lambda i: (0, 0)),
                  pl.BlockSpec((d, n_g), lambda i: (0, 0), pipeline_mode=once),
                  pl.BlockSpec((d, n_e), lambda i: (0, 0), pipeline_mode=once),
                  pl.BlockSpec((1, n_g), lambda i: (0, 0)),
                  pl.BlockSpec((1, n_e), lambda i: (0, 0))],
        out_specs=[pl.BlockSpec((LOCAL_ROWS, d + LANES), lambda i: (i, 0)),
                   pl.BlockSpec((8, tm), lambda i: (0, i)),
                   pl.BlockSpec((N_EXPERTS, LANES), lambda i: (i, 0))],
        out_shape=[jax.ShapeDtypeStruct((nb * LOCAL_ROWS, d + LANES), BF16),
                   jax.ShapeDtypeStruct((8, t), F32),
                   jax.ShapeDtypeStruct((nb * N_EXPERTS, LANES), F32)],
        scratch_shapes=[pltpu.VMEM((tm, tm), BF16),
                        pltpu.VMEM((d + LANES, LANES), F32),
                        pltpu.VMEM((ROUTER_ROWS, d + LANES), F32)],
        compiler_params=_params(1),
        name="route_sort",
    )(h1, g.reshape(1, d), w_rg, w_re, b_rg.reshape(1, n_g), b_re.reshape(1, n_e))


def _group_copy(src_ref, s_group, dst_ref, d_group, sem):
    return pltpu.make_async_copy(src_ref.at[pl.ds(pl.multiple_of(s_group * GROUP, GROUP), GROUP), :],
                                 dst_ref.at[pl.ds(pl.multiple_of(d_group * GROUP, GROUP), GROUP), :], sem)


def _expert_kernel(te_ref, nxt_ref, nreal_ref, nt_ref, gsrc_ref, gdst_ref, tail_ref,
                   xs_ref, wg_ref, wu_ref, wd_ref, ys_ref,
                   xbuf_ref, ybuf_ref, zero_ref, wgu_buf_ref, wd_buf_ref,
                   cur_ref, gsem, ssem, zsem, wsem):
    r = pl.program_id(0)
    nt = nt_ref[0]

    def gather(q, slot):
        for i in range(TILE_GROUPS):
            _group_copy(xs_ref, gsrc_ref[q * TILE_GROUPS + i], xbuf_ref, slot * TILE_GROUPS + i,
                        gsem.at[slot]).start(priority=1)

    def weight_copies(e, wslot):
        f = wg_ref.shape[2]
        return (pltpu.make_async_copy(wg_ref.at[e], wgu_buf_ref.at[wslot, :, pl.ds(0, f)], wsem.at[wslot]),
                pltpu.make_async_copy(wu_ref.at[e], wgu_buf_ref.at[wslot, :, pl.ds(f, f)], wsem.at[wslot]),
                pltpu.make_async_copy(wd_ref.at[e], wd_buf_ref.at[wslot], wsem.at[wslot]))

    def tile_wait(src, dst, sem):
        pltpu.make_async_copy(src.at[pl.ds(0, MOE_TILE), :], dst.at[pl.ds(0, MOE_TILE), :], sem).wait()

    @pl.when(r == 0)
    def _():
        cur_ref[0] = -1
        cur_ref[1] = -1
        for cp in weight_copies(te_ref[0], 0):
            cp.start()
        gather(0, 0)
        zero_ref[...] = jnp.zeros_like(zero_ref)
        ybuf_ref[...] = jnp.zeros_like(ybuf_ref)
        n_blocks = tail_ref.shape[0]

        def fill(make):
            def body(g, c):
                make(g)
                return c
            return body

        for blk in range(n_blocks):
            lax.fori_loop(tail_ref[blk], GROUPS_PER_BLOCK, fill(
                lambda g, blk=blk: _group_copy(zero_ref, 0, ys_ref, blk * GROUPS_PER_BLOCK + g, zsem).start()), 0)
        for blk in range(n_blocks):
            lax.fori_loop(tail_ref[blk], GROUPS_PER_BLOCK, fill(
                lambda g, blk=blk: _group_copy(zero_ref, 0, ys_ref, blk * GROUPS_PER_BLOCK + g, zsem).wait()), 0)

    @pl.when(r < nt)
    def _():
        e = te_ref[r]
        slot = r % 2

        gather(jnp.minimum(r + 1, nt - 1), 1 - slot)

        @pl.when(cur_ref[0] != e)
        def _():
            wslot = (cur_ref[1] + 1) % 2
            for cp in weight_copies(e, wslot):
                cp.wait()
            cur_ref[0] = e
            cur_ref[1] = cur_ref[1] + 1

            @pl.when(nxt_ref[r] >= 0)
            def _():
                for cp in weight_copies(nxt_ref[r], 1 - wslot):
                    cp.start()

        wslot = cur_ref[1] % 2
        tile_wait(xs_ref, xbuf_ref, gsem.at[slot])

        @pl.when(r >= 2)
        def _():
            tile_wait(ybuf_ref, ys_ref, ssem.at[slot])

        d = ybuf_ref.shape[1]
        f = wd_buf_ref.shape[1]
        def chain(row0, n_rows):
            rows = pl.ds(pl.multiple_of(slot * MOE_TILE + row0, HALF_ROWS), n_rows)
            x = xbuf_ref[rows, 0:d]
            gate_parts = xbuf_ref[rows, d:d + LANES].astype(F32)
            gate = gate_parts[:, 0:1] + gate_parts[:, 1:2]
            gu = _dot(x, wgu_buf_ref[wslot])
            hid = (jax.nn.silu(gu[:, 0:f]) * gu[:, f:2 * f]).astype(BF16)
            ybuf_ref[rows, :] = (_dot(hid, wd_buf_ref[wslot]) * gate).astype(ybuf_ref.dtype)

        half_groups = HALF_ROWS // GROUP
        n_halves = (nreal_ref[r] + half_groups - 1) // half_groups
        for count in range(1, MOE_TILE // HALF_ROWS + 1):
            @pl.when(n_halves == count)
            def _(count=count):
                for part in range(count // 2):
                    chain(part * MXU_ROWS, MXU_ROWS)
                if count % 2:
                    chain((count // 2) * MXU_ROWS, HALF_ROWS)

        for i in range(TILE_GROUPS):
            _group_copy(ybuf_ref, slot * TILE_GROUPS + i, ys_ref, gdst_ref[r * TILE_GROUPS + i],
                        ssem.at[slot]).start(priority=1)

        @pl.when(r == nt - 1)
        def _():
            tile_wait(xs_ref, xbuf_ref, gsem.at[1 - slot])
            tile_wait(ybuf_ref, ys_ref, ssem.at[slot])

            @pl.when(r >= 1)
            def _():
                tile_wait(ybuf_ref, ys_ref, ssem.at[1 - slot])


def _experts(plan, xs, w_gate, w_up, w_down):
    d = w_gate.shape[1]
    f = w_gate.shape[2]
    tile_e, next_e, n_real, n_tiles, gsrc, gdst, tail = plan
    r_max = tile_e.shape[0]
    n_blocks = tail.shape[0]
    any_spec = pl.BlockSpec(memory_space=pl.ANY)
    grid_spec = pltpu.PrefetchScalarGridSpec(
        num_scalar_prefetch=7,
        grid=(r_max,),
        in_specs=[any_spec, any_spec, any_spec, any_spec],
        out_specs=any_spec,
        scratch_shapes=[pltpu.VMEM((2 * MOE_TILE, d + LANES), BF16),
                        pltpu.VMEM((2 * MOE_TILE, d), BF16),
                        pltpu.VMEM((GROUP, d), BF16),
                        pltpu.VMEM((2, d, 2 * f), BF16),
                        pltpu.VMEM((2, f, d), BF16),
                        pltpu.SMEM((2,), jnp.int32),
                        pltpu.SemaphoreType.DMA((2,)),
                        pltpu.SemaphoreType.DMA((2,)),
                        pltpu.SemaphoreType.DMA(()),
                        pltpu.SemaphoreType.DMA((2,))],
    )
    return pl.pallas_call(
        _expert_kernel,
        grid_spec=grid_spec,
        out_shape=jax.ShapeDtypeStruct((n_blocks * LOCAL_ROWS, d), BF16),
        compiler_params=_params(1),
        name="experts",
    )(tile_e, next_e, n_real, n_tiles, gsrc, gdst, tail, xs, w_gate, w_up, w_down)


def _combine_kernel(ys_ref, h_ref, cm_ref, g_ref, o_ref, ybuf_ref, hbuf_ref, sem):
    n = pl.num_programs(0)
    tm = o_ref.shape[0]
    rows = ybuf_ref.shape[1]
    slot = _ring_fetch((ys_ref, h_ref), (ybuf_ref, hbuf_ref), sem, pl.program_id(0), n, n)
    cm = cm_ref[...].T
    q = lax.broadcasted_iota(jnp.int32, (tm, rows), 1)
    sel = jnp.where((q == cm[:, 0:1].astype(jnp.int32)) | (q == cm[:, 1:2].astype(jnp.int32)), 1.0, 0.0)
    h2 = hbuf_ref[slot] + _dot(sel.astype(BF16), ybuf_ref[slot])
    o_ref[...] = _rms_scale(h2, g_ref[...])


def _combine(ys, h1, meta, g):
    t, d = h1.shape
    tm = ROUTE_BLOCK
    assert t // tm >= RING - 1
    any_spec = pl.BlockSpec(memory_space=pl.ANY)
    return pl.pallas_call(
        _combine_kernel,
        grid=(t // tm,),
        in_specs=[any_spec, any_spec,
                  pl.BlockSpec((8, tm), lambda i: (0, i)),
                  pl.BlockSpec((1, d), lambda i: (0, 0))],
        out_specs=pl.BlockSpec((tm, d), lambda i: (i, 0)),
        out_shape=jax.ShapeDtypeStruct((t, d), F32),
        scratch_shapes=[pltpu.VMEM((RING, LOCAL_ROWS, d), BF16),
                        pltpu.VMEM((RING, tm, d), F32),
                        pltpu.SemaphoreType.DMA((RING,))],
        compiler_params=_params(1),
        name="combine",
    )(ys, h1, meta, g.reshape(1, d))


def _expert_plan(counts, n_blocks):
    cnt = counts[:, 0].astype(jnp.int32).reshape(n_blocks, N_EXPERTS)
    groups = (cnt + GROUP - 1) // GROUP
    first = jnp.cumsum(groups, axis=1) - groups
    upto = jnp.cumsum(groups, axis=0)
    per_expert = upto[-1]
    tiles_e = (per_expert + TILE_GROUPS - 1) // TILE_GROUPS
    tile_end = jnp.cumsum(tiles_e)
    n_tiles = tile_end[-1]
    max_groups = 2 * ROUTE_BLOCK * n_blocks // GROUP + n_blocks * N_EXPERTS
    r_max = max_groups // TILE_GROUPS + N_EXPERTS
    tile_ids = jnp.arange(r_max, dtype=jnp.int32)
    tile = jnp.minimum(tile_ids, n_tiles - 1)
    tile_e = jnp.sum((tile_end[None, :] <= tile[:, None]).astype(jnp.int32), axis=1)
    later = (tile_e[None, :] > tile_e[:, None]) & (tile_ids[None, :] < n_tiles)
    next_e = jnp.min(jnp.where(later, tile_e[None, :], N_EXPERTS), axis=1)
    next_e = jnp.where(next_e == N_EXPERTS, -1, next_e)

    slot = jnp.arange(r_max * TILE_GROUPS, dtype=jnp.int32)
    s_tile = slot // TILE_GROUPS
    oh_e = jnp.repeat(tile_e, TILE_GROUPS)[:, None] == jnp.arange(N_EXPERTS, dtype=jnp.int32)[None, :]

    def by_expert(table):
        return jnp.sum(jnp.where(oh_e[:, None, :], table[None], 0), axis=-1)

    k = slot - by_expert(((tile_end - tiles_e) * TILE_GROUPS)[None, :])[:, 0]
    real = (k < by_expert(per_expert[None, :])[:, 0]) & (s_tile < n_tiles)
    upto_e = by_expert(upto)
    blk = jnp.minimum(jnp.sum((upto_e <= k[:, None]).astype(jnp.int32), axis=1), n_blocks - 1)
    oh_b = blk[:, None] == jnp.arange(n_blocks, dtype=jnp.int32)[None, :]

    def by_block(table_se):
        return jnp.sum(jnp.where(oh_b, table_se, 0), axis=1)

    before = by_block(upto_e - by_expert(groups))
    src = blk * GROUPS_PER_BLOCK + by_block(by_expert(first)) + (k - before)
    zero_group = GROUPS_PER_BLOCK - 1
    spare = n_blocks * GROUPS_PER_BLOCK + (s_tile % 2) * TILE_GROUPS + slot % TILE_GROUPS
    gsrc = jnp.where(real, src, zero_group)
    gdst = jnp.where(real, src, spare)
    n_spare_blocks = -(-2 * TILE_GROUPS // GROUPS_PER_BLOCK)
    tail = jnp.concatenate([jnp.sum(groups, axis=1), jnp.zeros((n_spare_blocks,), jnp.int32)])
    n_real = jnp.sum(real.reshape(r_max, TILE_GROUPS).astype(jnp.int32), axis=1)
    return tile_e, next_e, n_real, n_tiles.reshape(1), gsrc, gdst, tail


def _layer(h, norm_mix_g, w_in, conv_w, sgu_ln_g, sgu_ln_b, sgu_w_s, sgu_b_s, w_up_conv,
           w_up_sgu, w_out, norm_ffn_g, w_rg, b_rg, w_re, b_re, w_eg, w_eu, w_ed, out_g):
    t, d = h.shape
    conv_width = conv_w.shape[1]
    sgu_width = sgu_ln_g.shape[0]
    xn, yb = _sgu_branch(h, norm_mix_g, w_in, 3 * conv_width, sgu_ln_g, sgu_ln_b, sgu_w_s, sgu_b_s)
    gate_col0 = 3 * conv_width + 2 * sgu_width
    ya, (wg_bf, wua_bf, wub_bf, wout_bf) = _conv_branch(
        xn, w_in, conv_w, conv_width,
        to_cast=[(w_in, gate_col0, w_in.shape[1] - gate_col0, math.gcd(gate_col0, w_in.shape[1])),
                 (w_up_conv, 0, d, d), (w_up_sgu, 0, d, d), (w_out, 0, d, d)])
    m, eg_bf, eu_bf, ed_bf = _upgate(xn, ya, yb, wg_bf, wua_bf, wub_bf, w_eg, w_eu, w_ed)
    h1 = _outproj(m, wout_bf, h)
    xs, meta, counts = _route_sort(h1, norm_ffn_g, w_rg, b_rg, w_re, b_re)
    ys = _experts(_expert_plan(counts, t // ROUTE_BLOCK), xs, eg_bf, eu_bf, ed_bf)
    return _combine(ys, h1, meta, out_g)


def kernel(x, norm_mix_g, w_in, conv_w, sgu_ln_g, sgu_ln_b, sgu_w_s, sgu_b_s, w_up_conv, w_up_sgu, w_out, norm_ffn_g, w_router_group, b_router_group, w_router_expert, b_router_expert, w_exp_gate, w_exp_up, w_exp_down, norm_final_g):
    bsz, s, d = x.shape
    depth = w_in.shape[0]
    assert bsz == 1 and depth == 1, "causal conv carry and the fused final norm assume one sequence, one layer"
    assert s % ROUTE_BLOCK == 0
    out = _layer(x.reshape(s, d), norm_mix_g[0], w_in[0], conv_w[0], sgu_ln_g[0], sgu_ln_b[0],
                 sgu_w_s[0], sgu_b_s[0], w_up_conv[0], w_up_sgu[0], w_out[0], norm_ffn_g[0],
                 w_router_group[0], b_router_group[0], w_router_expert[0], b_router_expert[0],
                 w_exp_gate[0], w_exp_up[0], w_exp_down[0], norm_final_g)
    return out.reshape(bsz, s, d)
```

```python
import functools
import math

import jax
import jax.numpy as jnp
from jax import lax
from jax.experimental import pallas as pl
from jax.experimental.pallas import tpu as pltpu

F32 = jnp.float32
BF16 = jnp.bfloat16

EPS = 1e-6
CHUNK = 64
CONV_K = 3
SGU_HEADS = 8
SGU_BLOCK = 128
N_GROUPS = 4
EXPERTS_PER_GROUP = 4
N_EXPERTS = N_GROUPS * EXPERTS_PER_GROUP
ROUTER_ROWS = 32
LANES = 128

VMEM_LIMIT_BYTES = 56 * 1024 * 1024

ROUTE_BLOCK = 512
GROUP = 16
MXU_ROWS = 256
HALF_ROWS = MXU_ROWS // 2
LOCAL_ROWS = -(-(2 * ROUTE_BLOCK + N_EXPERTS * (GROUP - 1)) // MXU_ROWS) * MXU_ROWS
GROUPS_PER_BLOCK = LOCAL_ROWS // GROUP
assert LOCAL_ROWS - (2 * ROUTE_BLOCK + N_EXPERTS * (GROUP - 1)) >= GROUP
MOE_TILE = 512
TILE_GROUPS = MOE_TILE // GROUP


def _params(n_axes):
    return pltpu.CompilerParams(
        dimension_semantics=("arbitrary",) * n_axes,
        vmem_limit_bytes=VMEM_LIMIT_BYTES)


def _dot(a, b):
    return jnp.dot(a, b, preferred_element_type=F32)


def _rms_scale(x, g):
    ms = jnp.mean(x * x, axis=-1, keepdims=True)
    return x * lax.rsqrt(ms + EPS) * g


RING = 3


def _ring_fetch(srcs, bufs, sem, step, n_steps, n_blocks):
    def copies(s):
        blk = s % n_blocks
        slot = s % RING
        return [pltpu.make_async_copy(
            src.at[pl.ds(pl.multiple_of(blk * buf.shape[1], buf.shape[1]), buf.shape[1]), :],
            buf.at[slot], sem.at[slot]) for src, buf in zip(srcs, bufs)]

    @pl.when(step == 0)
    def _():
        for s in range(RING - 1):
            for cp in copies(s):
                cp.start()

    @pl.when(step + RING - 1 < n_steps)
    def _():
        for cp in copies(step + RING - 1):
            cp.start()

    for cp in copies(step):
        cp.wait()
    return step % RING


def _conv_kernel(cast_chunks, xn_ref, wb_ref, wc_ref, wh_ref, cw_ref, *rest):
    n_in = sum(cast_chunks)
    cast_in, o_ref = rest[:n_in], rest[n_in]
    cast_out = rest[n_in + 1:n_in + 1 + len(cast_chunks)]
    wbf_ref, carry_ref, xnbuf_ref, xsem = rest[n_in + 1 + len(cast_chunks):]
    i = pl.program_id(1)
    ni = pl.num_programs(1)
    tn = wb_ref.shape[1]
    tm = xnbuf_ref.shape[1]
    slot = _ring_fetch((xn_ref,), (xnbuf_ref,), xsem, pl.program_id(0) * ni + i,
                       pl.num_programs(0) * ni, ni)
    src = iter(cast_in)
    for dst, n_chunks in zip(cast_out, cast_chunks):
        wc = dst.shape[1] // n_chunks
        for k in range(n_chunks):
            dst[:, k * wc:(k + 1) * wc] = next(src)[...].astype(BF16)

    @pl.when(i == 0)
    def _():
        wbf_ref[:, 0:tn] = wb_ref[...].astype(BF16)
        wbf_ref[:, tn:2 * tn] = wc_ref[...].astype(BF16)
        wbf_ref[:, 2 * tn:3 * tn] = wh_ref[...].astype(BF16)
        carry_ref[...] = jnp.zeros_like(carry_ref)

    proj = _dot(xnbuf_ref[slot], wbf_ref[...])
    b = proj[:, 0:tn]
    p = proj[:, tn:2 * tn] * proj[:, 2 * tn:3 * tn]
    prev = carry_ref[...]
    carry_ref[...] = p[tm - 8:tm, :]
    row = lax.broadcasted_iota(jnp.int32, p.shape, 0)
    p1 = jnp.where(row == 0, prev[7:8, :], pltpu.roll(p, 1, axis=0))
    p2 = jnp.where(row == 0, prev[6:7, :],
                   jnp.where(row == 1, prev[7:8, :], pltpu.roll(p, 2, axis=0)))
    cw = cw_ref[...]
    y = b * (cw[0:1, :] * p2 + cw[1:2, :] * p1 + cw[2:3, :] * p)
    o_ref[...] = y.astype(o_ref.dtype)


def _conv_branch(xn, w_in, conv_w, width, to_cast, tm=1024, tn=256):
    t, d = xn.shape
    nj = width // tn
    ni = t // tm
    steps = nj * ni
    assert steps >= RING - 1
    cast_args, cast_in_specs, cast_out_specs, cast_shapes, cast_chunks = [], [], [], [], []
    for arr, col0, n_cols, chunk in to_cast:
        rows = arr.shape[0] // steps
        assert rows * steps == arr.shape[0] and rows % 16 == 0
        assert col0 % chunk == 0 and n_cols % chunk == 0
        for k in range(n_cols // chunk):
            cast_args.append(arr)
            cast_in_specs.append(
                pl.BlockSpec((rows, chunk), lambda j, i, c=col0 // chunk + k: (j * ni + i, c)))
        cast_chunks.append(n_cols // chunk)
        cast_out_specs.append(pl.BlockSpec((rows, n_cols), lambda j, i: (j * ni + i, 0)))
        cast_shapes.append(jax.ShapeDtypeStruct((arr.shape[0], n_cols), BF16))
    outs = pl.pallas_call(
        functools.partial(_conv_kernel, tuple(cast_chunks)),
        grid=(nj, ni),
        in_specs=[pl.BlockSpec(memory_space=pl.ANY),
                  pl.BlockSpec((d, tn), lambda j, i: (0, j)),
                  pl.BlockSpec((d, tn), lambda j, i: (0, nj + j)),
                  pl.BlockSpec((d, tn), lambda j, i: (0, 2 * nj + j)),
                  pl.BlockSpec((CONV_K, tn), lambda j, i: (0, j))] + cast_in_specs,
        out_specs=[pl.BlockSpec((tm, tn), lambda j, i: (i, j))] + cast_out_specs,
        out_shape=[jax.ShapeDtypeStruct((t, width), BF16)] + cast_shapes,
        scratch_shapes=[pltpu.VMEM((d, 3 * tn), BF16),
                        pltpu.VMEM((8, tn), F32),
                        pltpu.VMEM((RING, tm, d), BF16),
                        pltpu.SemaphoreType.DMA((RING,))],
        compiler_params=_params(2),
        name="conv_branch",
    )(xn, w_in, w_in, w_in, conv_w, *cast_args)
    return outs[0], outs[1:]


def _sgu_kernel(x_ref, g_ref, wu_ref, wv_ref, lng_ref, lnb_ref, ws_ref, bs_ref, xn_ref, o_ref, wbf_ref,
                xbuf_ref, xsem):
    tm = xbuf_ref.shape[1]
    n_steps = pl.num_programs(0)
    slot = _ring_fetch((x_ref,), (xbuf_ref,), xsem, pl.program_id(0), n_steps, n_steps)
    w = o_ref.shape[1]
    hd = w // SGU_HEADS

    @pl.when(pl.program_id(0) == 0)
    def _():
        wbf_ref[:, 0:w] = wu_ref[...].astype(BF16)
        wbf_ref[:, w:2 * w] = wv_ref[...].astype(BF16)

    xn = _rms_scale(xbuf_ref[slot], g_ref[...]).astype(BF16)
    xn_ref[...] = xn
    gz = jax.nn.gelu(_dot(xn, wbf_ref[...]))
    v = gz[:, w:2 * w]
    mu = jnp.mean(v, axis=-1, keepdims=True)
    vc = v - mu
    var = jnp.mean(vc * vc, axis=-1, keepdims=True)
    vn = (vc * lax.rsqrt(var + EPS) * lng_ref[...] + lnb_ref[...]).astype(BF16)
    ii = lax.broadcasted_iota(jnp.int32, (SGU_BLOCK, SGU_BLOCK), 0)
    jj = lax.broadcasted_iota(jnp.int32, (SGU_BLOCK, SGU_BLOCK), 1)
    mask = (jj // CHUNK) <= (ii // CHUNK)
    bs_t = bs_ref[...].T
    for h in range(SGU_HEADS):
        wm = jnp.where(mask, ws_ref[h], 0.0).astype(BF16)
        cs = slice(h * hd, (h + 1) * hd)
        bias = jnp.broadcast_to(bs_t[:, h:h + 1], (SGU_BLOCK, hd))
        for n in range(tm // SGU_BLOCK):
            rs = slice(n * SGU_BLOCK, (n + 1) * SGU_BLOCK)
            vm = _dot(wm, vn[rs, cs]) + bias
            o_ref[rs, cs] = (gz[rs, cs] * vm).astype(o_ref.dtype)


def _sgu_branch(x, g, w_in, col0, ln_g, ln_b, w_s, b_s, tm=512):
    t, d = x.shape
    w = ln_g.shape[0]
    assert col0 % w == 0 and t // tm >= RING - 1
    c0 = col0 // w
    once = pl.Buffered(1)
    return pl.pallas_call(
        _sgu_kernel,
        grid=(t // tm,),
        in_specs=[pl.BlockSpec(memory_space=pl.ANY),
                  pl.BlockSpec((1, d), lambda i: (0, 0)),
                  pl.BlockSpec((d, w), lambda i: (0, c0), pipeline_mode=once),
                  pl.BlockSpec((d, w), lambda i: (0, c0 + 1), pipeline_mode=once),
                  pl.BlockSpec((1, w), lambda i: (0, 0)),
                  pl.BlockSpec((1, w), lambda i: (0, 0)),
                  pl.BlockSpec((SGU_HEADS, SGU_BLOCK, SGU_BLOCK), lambda i: (0, 0, 0)),
                  pl.BlockSpec((SGU_HEADS, SGU_BLOCK), lambda i: (0, 0))],
        out_specs=[pl.BlockSpec((tm, d), lambda i: (i, 0)),
                   pl.BlockSpec((tm, w), lambda i: (i, 0))],
        out_shape=[jax.ShapeDtypeStruct((t, d), BF16),
                   jax.ShapeDtypeStruct((t, w), BF16)],
        scratch_shapes=[pltpu.VMEM((d, 2 * w), BF16),
                        pltpu.VMEM((RING, tm, d), F32),
                        pltpu.SemaphoreType.DMA((RING,))],
        compiler_params=_params(1),
        name="sgu_branch",
    )(x, g.reshape(1, d), w_in, w_in, ln_g.reshape(1, w), ln_b.reshape(1, w), w_s, b_s)


def _upgate_kernel(xn_ref, ya_ref, yb_ref, wgc_ref, wgs_ref, wua_ref, wub_ref, eg_ref, eu_ref, ed_ref,
                   o_ref, eg_bf_ref, eu_bf_ref, ed_bf_ref):
    eg_bf_ref[...] = eg_ref[...].astype(BF16)
    eu_bf_ref[...] = eu_ref[...].astype(BF16)
    ed_bf_ref[...] = ed_ref[...].astype(BF16)

    xn = xn_ref[...]
    m = (jax.nn.sigmoid(_dot(xn, wgc_ref[...])) * _dot(ya_ref[...], wua_ref[...])
         + jax.nn.sigmoid(_dot(xn, wgs_ref[...])) * _dot(yb_ref[...], wub_ref[...]))
    o_ref[...] = m.astype(o_ref.dtype)


def _upgate(xn, ya, yb, w_gates, w_up_a, w_up_b, w_eg, w_eu, w_ed, tm=1024, tn=512):
    t, d = xn.shape
    wa = ya.shape[1]
    wb = yb.shape[1]
    dout = w_up_a.shape[1]
    c0 = 0
    nj = dout // tn
    ni = t // tm
    n_e, d_e, f_e = w_eg.shape
    up_rows = n_e * d_e // (nj * ni)
    down_rows = n_e * f_e // (nj * ni)
    assert up_rows * nj * ni == n_e * d_e and up_rows % 16 == 0
    assert down_rows * nj * ni == n_e * f_e and down_rows % 16 == 0
    up_spec = pl.BlockSpec((up_rows, f_e), lambda j, i: (j * ni + i, 0))
    down_spec = pl.BlockSpec((down_rows, d_e), lambda j, i: (j * ni + i, 0))
    m, eg_bf, eu_bf, ed_bf = pl.pallas_call(
        _upgate_kernel,
        grid=(nj, ni),
        in_specs=[pl.BlockSpec((tm, d), lambda j, i: (i, 0)),
                  pl.BlockSpec((tm, wa), lambda j, i: (i, 0)),
                  pl.BlockSpec((tm, wb), lambda j, i: (i, 0)),
                  pl.BlockSpec((d, tn), lambda j, i: (0, c0 + j)),
                  pl.BlockSpec((d, tn), lambda j, i: (0, c0 + nj + j)),
                  pl.BlockSpec((wa, tn), lambda j, i: (0, j)),
                  pl.BlockSpec((wb, tn), lambda j, i: (0, j)),
                  up_spec, up_spec, down_spec],
        out_specs=[pl.BlockSpec((tm, tn), lambda j, i: (i, j)), up_spec, up_spec, down_spec],
        out_shape=[jax.ShapeDtypeStruct((t, dout), BF16),
                   jax.ShapeDtypeStruct((n_e * d_e, f_e), BF16),
                   jax.ShapeDtypeStruct((n_e * d_e, f_e), BF16),
                   jax.ShapeDtypeStruct((n_e * f_e, d_e), BF16)],
        compiler_params=_params(2),
        name="upgate",
    )(xn, ya, yb, w_gates, w_gates, w_up_a, w_up_b,
      w_eg.reshape(n_e * d_e, f_e), w_eu.reshape(n_e * d_e, f_e), w_ed.reshape(n_e * f_e, d_e))
    return (m, eg_bf.reshape(n_e, d_e, f_e), eu_bf.reshape(n_e, d_e, f_e),
            ed_bf.reshape(n_e, f_e, d_e))


def _outproj_kernel(m_ref, w_ref, x_ref, o_ref):
    o_ref[...] = x_ref[...] + _dot(m_ref[...], w_ref[...])


def _outproj(m, w_out_bf, x, tm=512):
    t, d = m.shape
    dout = w_out_bf.shape[1]
    return pl.pallas_call(
        _outproj_kernel,
        grid=(t // tm,),
        in_specs=[pl.BlockSpec((tm, d), lambda i: (i, 0)),
                  pl.BlockSpec((d, dout), lambda i: (0, 0), pipeline_mode=pl.Buffered(1)),
                  pl.BlockSpec((tm, dout), lambda i: (i, 0))],
        out_specs=pl.BlockSpec((tm, dout), lambda i: (i, 0)),
        out_shape=jax.ShapeDtypeStruct((t, dout), F32),
        compiler_params=_params(1),
        name="outproj",
    )(m, w_out_bf, x)


def _argmax_rows(rows):
    best = rows[0]
    idx = jnp.zeros(rows[0].shape, jnp.int32)
    for k in range(1, len(rows)):
        better = rows[k] > best
        best = jnp.where(better, rows[k], best)
        idx = jnp.where(better, k, idx)
    return best, idx


def _softmax_rows(rows):
    mx = functools.reduce(jnp.maximum, rows)
    ex = [jnp.exp(r - mx) for r in rows]
    den = functools.reduce(lambda a, b: a + b, ex)
    return [e / den for e in ex]


def _route_sort_kernel(h_ref, g_ref, wr_ref, br_ref, xs_ref, meta_ref, cnt_ref, before_ref):
    tm = h_ref.shape[0]
    xn = _rms_scale(h_ref[...], g_ref[...])
    xn_hi = xn.astype(BF16)
    xn_lo = (xn - xn_hi.astype(F32)).astype(BF16)
    wr = wr_ref[...]
    wr_hi = wr.astype(BF16)
    wr_lo = (wr - wr_hi.astype(F32)).astype(BF16)
    nt_dims = (((1,), (1,)), ((), ()))
    lt = (lax.dot_general(wr_hi, xn_hi, nt_dims, preferred_element_type=F32)
          + lax.dot_general(wr_hi, xn_lo, nt_dims, preferred_element_type=F32)
          + lax.dot_general(wr_lo, xn_hi, nt_dims, preferred_element_type=F32)) + br_ref[...]
    pgs = _softmax_rows([lt[k:k + 1, :] for k in range(N_GROUPS)])
    pg, gi = _argmax_rows(pgs)
    sel = []
    for k in range(EXPERTS_PER_GROUP):
        r = jnp.zeros_like(pg)
        for g in range(N_GROUPS):
            row = N_GROUPS + g * EXPERTS_PER_GROUP + k
            r = jnp.where(gi == g, lt[row:row + 1, :], r)
        sel.append(r)
    pes = _softmax_rows(sel)
    p1, e1 = _argmax_rows(pes)
    rest = [jnp.where(e1 == k, -1.0, pes[k]) for k in range(EXPERTS_PER_GROUP)]
    p2, e2 = _argmax_rows(rest)
    den = p1 + p2
    w1 = pg * (p1 / den)
    w2 = pg * (p2 / den)
    lo = jnp.minimum(e1, e2)
    hi = jnp.maximum(e1, e2)
    w_lo = jnp.where(e1 < e2, w1, w2)
    w_hi = jnp.where(e1 < e2, w2, w1)
    ea = gi * EXPERTS_PER_GROUP + lo
    eb = gi * EXPERTS_PER_GROUP + hi

    erow = lax.broadcasted_iota(jnp.int32, (N_EXPERTS, tm), 0)
    oh_a = (erow == ea).astype(F32)
    oh_b = (erow == eb).astype(F32)

    @pl.when(pl.program_id(0) == 0)
    def _():
        a = lax.broadcasted_iota(jnp.int32, (tm, tm), 0)
        b = lax.broadcasted_iota(jnp.int32, (tm, tm), 1)
        before_ref[...] = (a < b).astype(BF16)

    cum = _dot((oh_a + oh_b).astype(BF16), before_ref[...])
    cnt = jnp.sum(oh_a + oh_b, axis=1, keepdims=True)
    padded = jnp.floor((cnt + (GROUP - 1)) * (1.0 / GROUP)) * GROUP
    pos_a = jnp.sum(oh_a * cum + jnp.where(erow < ea, padded, 0.0), axis=0, keepdims=True)
    pos_b = jnp.sum(oh_b * cum + jnp.where(erow < eb, padded, 0.0), axis=0, keepdims=True)

    d = h_ref.shape[1]

    def gate_rows(w):
        hi = w.astype(BF16).astype(F32)
        lo = w - hi
        k = lax.broadcasted_iota(jnp.int32, (LANES, tm), 0)
        return jnp.where(k == 0, hi, jnp.where(k == 1, lo, 0.0)).astype(BF16)

    gate_a = gate_rows(w_lo)
    gate_b = gate_rows(w_hi)

    def sort_rows(r0, n):
        q = r0 + lax.broadcasted_iota(jnp.int32, (n, tm), 0)
        perm_a = jnp.where(q == pos_a.astype(jnp.int32), 1.0, 0.0).astype(BF16)
        perm_b = jnp.where(q == pos_b.astype(jnp.int32), 1.0, 0.0).astype(BF16)
        xs_ref[r0:r0 + n, 0:d] = _dot(perm_a + perm_b, xn_hi).astype(xs_ref.dtype)
        gates = (lax.dot_general(perm_a, gate_a, nt_dims, preferred_element_type=F32)
                 + lax.dot_general(perm_b, gate_b, nt_dims, preferred_element_type=F32))
        xs_ref[r0:r0 + n, d:d + LANES] = gates.astype(xs_ref.dtype)

    n_rows = xs_ref.shape[0]
    head = n_rows - HALF_ROWS
    used = jnp.sum(padded)
    sort_rows(0, head)

    @pl.when(used > head)
    def _():
        sort_rows(head, HALF_ROWS)

    @pl.when(used <= head)
    def _():
        xs_ref[head:n_rows, :] = jnp.zeros((HALF_ROWS, xs_ref.shape[1]), xs_ref.dtype)

    cnt_ref[...] = jnp.broadcast_to(cnt, cnt_ref.shape)
    meta_ref[0:1, :] = pos_a
    meta_ref[1:2, :] = pos_b
    meta_ref[2:8, :] = jnp.zeros((6, tm), F32)


def _route_sort(h1, g, w_rg, b_rg, w_re, b_re):
    t, d = h1.shape
    tm = ROUTE_BLOCK
    nb = t // tm
    n_log = w_rg.shape[1] + w_re.shape[1]
    wr = jnp.concatenate([w_rg, w_re], axis=1).T
    wr = jnp.pad(wr, ((0, ROUTER_ROWS - n_log), (0, 0)))
    br = jnp.pad(jnp.concatenate([b_rg, b_re]), (0, ROUTER_ROWS - n_log)).reshape(ROUTER_ROWS, 1)
    return pl.pallas_call(
        _route_sort_kernel,
        grid=(nb,),
        in_specs=[pl.BlockSpec((tm, d), lambda i: (i, 0)),
                  pl.BlockSpec((1, d), lambda i: (0, 0)),
                  pl.BlockSpec((ROUTER_ROWS, d), lambda i: (0, 0)),
                  pl.BlockSpec((ROUTER_ROWS, 1), lambda i: (0, 0))],
        out_specs=[pl.BlockSpec((LOCAL_ROWS, d + LANES), lambda i: (i, 0)),
                   pl.BlockSpec((8, tm), lambda i: (0, i)),
                   pl.BlockSpec((N_EXPERTS, LANES), lambda i: (i, 0))],
        out_shape=[jax.ShapeDtypeStruct((nb * LOCAL_ROWS, d + LANES), BF16),
                   jax.ShapeDtypeStruct((8, t), F32),
                   jax.ShapeDtypeStruct((nb * N_EXPERTS, LANES), F32)],
        scratch_shapes=[pltpu.VMEM((tm, tm), BF16)],
        compiler_params=_params(1),
        name="route_sort",
    )(h1, g.reshape(1, d), wr, br)


def _group_copy(src_ref, s_group, dst_ref, d_group, sem):
    return pltpu.make_async_copy(src_ref.at[pl.ds(pl.multiple_of(s_group * GROUP, GROUP), GROUP), :],
                                 dst_ref.at[pl.ds(pl.multiple_of(d_group * GROUP, GROUP), GROUP), :], sem)


def _expert_kernel(te_ref, nxt_ref, nreal_ref, nt_ref, gsrc_ref, gdst_ref, tail_ref,
                   xs_ref, wg_ref, wu_ref, wd_ref, ys_ref,
                   xbuf_ref, ybuf_ref, zero_ref, wgu_buf_ref, wd_buf_ref,
                   cur_ref, gsem, ssem, zsem, wsem):
    r = pl.program_id(0)
    nt = nt_ref[0]

    def gather(q, slot):
        for i in range(TILE_GROUPS):
            _group_copy(xs_ref, gsrc_ref[q * TILE_GROUPS + i], xbuf_ref, slot * TILE_GROUPS + i,
                        gsem.at[slot]).start(priority=1)

    def weight_copies(e, wslot):
        f = wg_ref.shape[2]
        return (pltpu.make_async_copy(wg_ref.at[e], wgu_buf_ref.at[wslot, :, pl.ds(0, f)], wsem.at[wslot]),
                pltpu.make_async_copy(wu_ref.at[e], wgu_buf_ref.at[wslot, :, pl.ds(f, f)], wsem.at[wslot]),
                pltpu.make_async_copy(wd_ref.at[e], wd_buf_ref.at[wslot], wsem.at[wslot]))

    def tile_wait(src, dst, sem):
        pltpu.make_async_copy(src.at[pl.ds(0, MOE_TILE), :], dst.at[pl.ds(0, MOE_TILE), :], sem).wait()

    @pl.when(r == 0)
    def _():
        cur_ref[0] = -1
        cur_ref[1] = -1
        for cp in weight_copies(te_ref[0], 0):
            cp.start()
        gather(0, 0)
        zero_ref[...] = jnp.zeros_like(zero_ref)
        ybuf_ref[...] = jnp.zeros_like(ybuf_ref)
        n_blocks = tail_ref.shape[0]

        def fill(make):
            def body(g, c):
                make(g)
                return c
            return body

        for blk in range(n_blocks):
            lax.fori_loop(tail_ref[blk], GROUPS_PER_BLOCK, fill(
                lambda g, blk=blk: _group_copy(zero_ref, 0, ys_ref, blk * GROUPS_PER_BLOCK + g, zsem).start()), 0)
        for blk in range(n_blocks):
            lax.fori_loop(tail_ref[blk], GROUPS_PER_BLOCK, fill(
                lambda g, blk=blk: _group_copy(zero_ref, 0, ys_ref, blk * GROUPS_PER_BLOCK + g, zsem).wait()), 0)

    @pl.when(r < nt)
    def _():
        e = te_ref[r]
        slot = r % 2

        gather(jnp.minimum(r + 1, nt - 1), 1 - slot)

        @pl.when(cur_ref[0] != e)
        def _():
            wslot = (cur_ref[1] + 1) % 2
            for cp in weight_copies(e, wslot):
                cp.wait()
            cur_ref[0] = e
            cur_ref[1] = cur_ref[1] + 1

            @pl.when(nxt_ref[r] >= 0)
            def _():
                for cp in weight_copies(nxt_ref[r], 1 - wslot):
                    cp.start()

        wslot = cur_ref[1] % 2
        tile_wait(xs_ref, xbuf_ref, gsem.at[slot])

        @pl.when(r >= 2)
        def _():
            tile_wait(ybuf_ref, ys_ref, ssem.at[slot])

        d = ybuf_ref.shape[1]
        f = wd_buf_ref.shape[1]
        def chain(row0, n_rows):
            rows = pl.ds(pl.multiple_of(slot * MOE_TILE + row0, HALF_ROWS), n_rows)
            x = xbuf_ref[rows, 0:d]
            gate_parts = xbuf_ref[rows, d:d + LANES].astype(F32)
            gate = gate_parts[:, 0:1] + gate_parts[:, 1:2]
            gu = _dot(x, wgu_buf_ref[wslot])
            hid = (jax.nn.silu(gu[:, 0:f]) * gu[:, f:2 * f]).astype(BF16)
            ybuf_ref[rows, :] = (_dot(hid, wd_buf_ref[wslot]) * gate).astype(ybuf_ref.dtype)

        half_groups = HALF_ROWS // GROUP
        n_halves = (nreal_ref[r] + half_groups - 1) // half_groups
        for count in range(1, MOE_TILE // HALF_ROWS + 1):
            @pl.when(n_halves == count)
            def _(count=count):
                for part in range(count // 2):
                    chain(part * MXU_ROWS, MXU_ROWS)
                if count % 2:
                    chain((count // 2) * MXU_ROWS, HALF_ROWS)

        for i in range(TILE_GROUPS):
            _group_copy(ybuf_ref, slot * TILE_GROUPS + i, ys_ref, gdst_ref[r * TILE_GROUPS + i],
                        ssem.at[slot]).start(priority=1)

        @pl.when(r == nt - 1)
        def _():
            tile_wait(xs_ref, xbuf_ref, gsem.at[1 - slot])
            tile_wait(ybuf_ref, ys_ref, ssem.at[slot])

            @pl.when(r >= 1)
            def _():
                tile_wait(ybuf_ref, ys_ref, ssem.at[1 - slot])


def _experts(plan, xs, w_gate, w_up, w_down):
    d = w_gate.shape[1]
    f = w_gate.shape[2]
    tile_e, next_e, n_real, n_tiles, gsrc, gdst, tail = plan
    r_max = tile_e.shape[0]
    n_blocks = tail.shape[0]
    any_spec = pl.BlockSpec(memory_space=pl.ANY)
    grid_spec = pltpu.PrefetchScalarGridSpec(
        num_scalar_prefetch=7,
        grid=(r_max,),
        in_specs=[any_spec, any_spec, any_spec, any_spec],
        out_specs=any_spec,
        scratch_shapes=[pltpu.VMEM((2 * MOE_TILE, d + LANES), BF16),
                        pltpu.VMEM((2 * MOE_TILE, d), BF16),
                        pltpu.VMEM((GROUP, d), BF16),
                        pltpu.VMEM((2, d, 2 * f), BF16),
                        pltpu.VMEM((2, f, d), BF16),
                        pltpu.SMEM((2,), jnp.int32),
                        pltpu.SemaphoreType.DMA((2,)),
                        pltpu.SemaphoreType.DMA((2,)),
                        pltpu.SemaphoreType.DMA(()),
                        pltpu.SemaphoreType.DMA((2,))],
    )
    return pl.pallas_call(
        _expert_kernel,
        grid_spec=grid_spec,
        out_shape=jax.ShapeDtypeStruct((n_blocks * LOCAL_ROWS, d), BF16),
        compiler_params=_params(1),
        name="experts",
    )(tile_e, next_e, n_real, n_tiles, gsrc, gdst, tail, xs, w_gate, w_up, w_down)


def _combine_kernel(ys_ref, h_ref, cm_ref, g_ref, o_ref, ybuf_ref, hbuf_ref, sem):
    n = pl.num_programs(0)
    tm = o_ref.shape[0]
    rows = ybuf_ref.shape[1]
    slot = _ring_fetch((ys_ref, h_ref), (ybuf_ref, hbuf_ref), sem, pl.program_id(0), n, n)
    cm = cm_ref[...].T
    q = lax.broadcasted_iota(jnp.int32, (tm, rows), 1)
    sel = jnp.where((q == cm[:, 0:1].astype(jnp.int32)) | (q == cm[:, 1:2].astype(jnp.int32)), 1.0, 0.0)
    h2 = hbuf_ref[slot] + _dot(sel.astype(BF16), ybuf_ref[slot])
    o_ref[...] = _rms_scale(h2, g_ref[...])


def _combine(ys, h1, meta, g):
    t, d = h1.shape
    tm = ROUTE_BLOCK
    assert t // tm >= RING - 1
    any_spec = pl.BlockSpec(memory_space=pl.ANY)
    return pl.pallas_call(
        _combine_kernel,
        grid=(t // tm,),
        in_specs=[any_spec, any_spec,
                  pl.BlockSpec((8, tm), lambda i: (0, i)),
                  pl.BlockSpec((1, d), lambda i: (0, 0))],
        out_specs=pl.BlockSpec((tm, d), lambda i: (i, 0)),
        out_shape=jax.ShapeDtypeStruct((t, d), F32),
        scratch_shapes=[pltpu.VMEM((RING, LOCAL_ROWS, d), BF16),
                        pltpu.VMEM((RING, tm, d), F32),
                        pltpu.SemaphoreType.DMA((RING,))],
        compiler_params=_params(1),
        name="combine",
    )(ys, h1, meta, g.reshape(1, d))


def _expert_plan(counts, n_blocks):
    cnt = counts[:, 0].astype(jnp.int32).reshape(n_blocks, N_EXPERTS)
    groups = (cnt + GROUP - 1) // GROUP
    first = jnp.cumsum(groups, axis=1) - groups
    upto = jnp.cumsum(groups, axis=0)
    per_expert = upto[-1]
    tiles_e = (per_expert + TILE_GROUPS - 1) // TILE_GROUPS
    tile_end = jnp.cumsum(tiles_e)
    n_tiles = tile_end[-1]
    max_groups = 2 * ROUTE_BLOCK * n_blocks // GROUP + n_blocks * N_EXPERTS
    r_max = max_groups // TILE_GROUPS + N_EXPERTS
    tile_ids = jnp.arange(r_max, dtype=jnp.int32)
    tile = jnp.minimum(tile_ids, n_tiles - 1)
    tile_e = jnp.sum((tile_end[None, :] <= tile[:, None]).astype(jnp.int32), axis=1)
    later = (tile_e[None, :] > tile_e[:, None]) & (tile_ids[None, :] < n_tiles)
    next_e = jnp.min(jnp.where(later, tile_e[None, :], N_EXPERTS), axis=1)
    next_e = jnp.where(next_e == N_EXPERTS, -1, next_e)

    slot = jnp.arange(r_max * TILE_GROUPS, dtype=jnp.int32)
    s_tile = slot // TILE_GROUPS
    oh_e = jnp.repeat(tile_e, TILE_GROUPS)[:, None] == jnp.arange(N_EXPERTS, dtype=jnp.int32)[None, :]

    def by_expert(table):
        return jnp.sum(jnp.where(oh_e[:, None, :], table[None], 0), axis=-1)

    k = slot - by_expert(((tile_end - tiles_e) * TILE_GROUPS)[None, :])[:, 0]
    real = (k < by_expert(per_expert[None, :])[:, 0]) & (s_tile < n_tiles)
    upto_e = by_expert(upto)
    blk = jnp.minimum(jnp.sum((upto_e <= k[:, None]).astype(jnp.int32), axis=1), n_blocks - 1)
    oh_b = blk[:, None] == jnp.arange(n_blocks, dtype=jnp.int32)[None, :]

    def by_block(table_se):
        return jnp.sum(jnp.where(oh_b, table_se, 0), axis=1)

    before = by_block(upto_e - by_expert(groups))
    src = blk * GROUPS_PER_BLOCK + by_block(by_expert(first)) + (k - before)
    zero_group = GROUPS_PER_BLOCK - 1
    spare = n_blocks * GROUPS_PER_BLOCK + (s_tile % 2) * TILE_GROUPS + slot % TILE_GROUPS
    gsrc = jnp.where(real, src, zero_group)
    gdst = jnp.where(real, src, spare)
    n_spare_blocks = -(-2 * TILE_GROUPS // GROUPS_PER_BLOCK)
    tail = jnp.concatenate([jnp.sum(groups, axis=1), jnp.zeros((n_spare_blocks,), jnp.int32)])
    n_real = jnp.sum(real.reshape(r_max, TILE_GROUPS).astype(jnp.int32), axis=1)
    return tile_e, next_e, n_real, n_tiles.reshape(1), gsrc, gdst, tail


def _layer(h, norm_mix_g, w_in, conv_w, sgu_ln_g, sgu_ln_b, sgu_w_s, sgu_b_s, w_up_conv,
           w_up_sgu, w_out, norm_ffn_g, w_rg, b_rg, w_re, b_re, w_eg, w_eu, w_ed, out_g):
    t, d = h.shape
    conv_width = conv_w.shape[1]
    sgu_width = sgu_ln_g.shape[0]
    xn, yb = _sgu_branch(h, norm_mix_g, w_in, 3 * conv_width, sgu_ln_g, sgu_ln_b, sgu_w_s, sgu_b_s)
    gate_col0 = 3 * conv_width + 2 * sgu_width
    ya, (wg_bf, wua_bf, wub_bf, wout_bf) = _conv_branch(
        xn, w_in, conv_w, conv_width,
        to_cast=[(w_in, gate_col0, w_in.shape[1] - gate_col0, math.gcd(gate_col0, w_in.shape[1])),
                 (w_up_conv, 0, d, d), (w_up_sgu, 0, d, d), (w_out, 0, d, d)])
    m, eg_bf, eu_bf, ed_bf = _upgate(xn, ya, yb, wg_bf, wua_bf, wub_bf, w_eg, w_eu, w_ed)
    h1 = _outproj(m, wout_bf, h)
    xs, meta, counts = _route_sort(h1, norm_ffn_g, w_rg, b_rg, w_re, b_re)
    ys = _experts(_expert_plan(counts, t // ROUTE_BLOCK), xs, eg_bf, eu_bf, ed_bf)
    return _combine(ys, h1, meta, out_g)


def kernel(x, norm_mix_g, w_in, conv_w, sgu_ln_g, sgu_ln_b, sgu_w_s, sgu_b_s, w_up_conv, w_up_sgu, w_out, norm_ffn_g, w_router_group, b_router_group, w_router_expert, b_router_expert, w_exp_gate, w_exp_up, w_exp_down, norm_final_g):
    bsz, s, d = x.shape
    depth = w_in.shape[0]
    assert bsz == 1 and depth == 1, "causal conv carry and the fused final norm assume one sequence, one layer"
    assert s % ROUTE_BLOCK == 0
    out = _layer(x.reshape(s, d), norm_mix_g[0], w_in[0], conv_w[0], sgu_ln_g[0], sgu_ln_b[0],
                 sgu_w_s[0], sgu_b_s[0], w_up_conv[0], w_up_sgu[0], w_out[0], norm_ffn_g[0],
                 w_router_group[0], b_router_group[0], w_router_expert[0], b_router_expert[0],
                 w_exp_gate[0], w_exp_up[0], w_exp_down[0], norm_final_g)
    return out.reshape(bsz, s, d)
```

```python
import functools
import math

import jax
import jax.numpy as jnp
from jax import lax
from jax.experimental import pallas as pl
from jax.experimental.pallas import tpu as pltpu

F32 = jnp.float32
BF16 = jnp.bfloat16

EPS = 1e-6
CHUNK = 64
CONV_K = 3
SGU_HEADS = 8
SGU_BLOCK = 128
N_GROUPS = 4
EXPERTS_PER_GROUP = 4
N_EXPERTS = N_GROUPS * EXPERTS_PER_GROUP
ROUTER_ROWS = 32
LANES = 128

VMEM_LIMIT_BYTES = 56 * 1024 * 1024

ROUTE_BLOCK = 512
GROUP = 16
MXU_ROWS = 256
HALF_ROWS = MXU_ROWS // 2
LOCAL_ROWS = -(-(2 * ROUTE_BLOCK + N_EXPERTS * (GROUP - 1)) // MXU_ROWS) * MXU_ROWS
GROUPS_PER_BLOCK = LOCAL_ROWS // GROUP
assert LOCAL_ROWS - (2 * ROUTE_BLOCK + N_EXPERTS * (GROUP - 1)) >= GROUP
MOE_TILE = 512
TILE_GROUPS = MOE_TILE // GROUP


def _params(n_axes):
    return pltpu.CompilerParams(
        dimension_semantics=("arbitrary",) * n_axes,
        vmem_limit_bytes=VMEM_LIMIT_BYTES)


def _dot(a, b):
    return jnp.dot(a, b, preferred_element_type=F32)


def _rms_scale(x, g):
    ms = jnp.mean(x * x, axis=-1, keepdims=True)
    return x * lax.rsqrt(ms + EPS) * g


RING = 3


def _ring_fetch(srcs, bufs, sem, step, n_steps, n_blocks):
    def copies(s):
        blk = s % n_blocks
        slot = s % RING
        return [pltpu.make_async_copy(
            src.at[pl.ds(pl.multiple_of(blk * buf.shape[1], buf.shape[1]), buf.shape[1]), :],
            buf.at[slot], sem.at[slot]) for src, buf in zip(srcs, bufs)]

    @pl.when(step == 0)
    def _():
        for s in range(RING - 1):
            for cp in copies(s):
                cp.start()

    @pl.when(step + RING - 1 < n_steps)
    def _():
        for cp in copies(step + RING - 1):
            cp.start()

    for cp in copies(step):
        cp.wait()
    return step % RING


def _conv_kernel(cast_chunks, xn_ref, wb_ref, wc_ref, wh_ref, cw_ref, *rest):
    n_in = sum(cast_chunks)
    cast_in, o_ref = rest[:n_in], rest[n_in]
    cast_out = rest[n_in + 1:n_in + 1 + len(cast_chunks)]
    wbf_ref, carry_ref, xnbuf_ref, xsem = rest[n_in + 1 + len(cast_chunks):]
    i = pl.program_id(1)
    ni = pl.num_programs(1)
    tn = wb_ref.shape[1]
    tm = xnbuf_ref.shape[1]
    slot = _ring_fetch((xn_ref,), (xnbuf_ref,), xsem, pl.program_id(0) * ni + i,
                       pl.num_programs(0) * ni, ni)
    src = iter(cast_in)
    for dst, n_chunks in zip(cast_out, cast_chunks):
        wc = dst.shape[1] // n_chunks
        for k in range(n_chunks):
            dst[:, k * wc:(k + 1) * wc] = next(src)[...].astype(BF16)

    @pl.when(i == 0)
    def _():
        wbf_ref[:, 0:tn] = wb_ref[...].astype(BF16)
        wbf_ref[:, tn:2 * tn] = wc_ref[...].astype(BF16)
        wbf_ref[:, 2 * tn:3 * tn] = wh_ref[...].astype(BF16)
        carry_ref[...] = jnp.zeros_like(carry_ref)

    proj = _dot(xnbuf_ref[slot], wbf_ref[...])
    b = proj[:, 0:tn]
    p = proj[:, tn:2 * tn] * proj[:, 2 * tn:3 * tn]
    prev = carry_ref[...]
    carry_ref[...] = p[tm - 8:tm, :]
    row = lax.broadcasted_iota(jnp.int32, p.shape, 0)
    p1 = jnp.where(row == 0, prev[7:8, :], pltpu.roll(p, 1, axis=0))
    p2 = jnp.where(row == 0, prev[6:7, :],
                   jnp.where(row == 1, prev[7:8, :], pltpu.roll(p, 2, axis=0)))
    cw = cw_ref[...]
    y = b * (cw[0:1, :] * p2 + cw[1:2, :] * p1 + cw[2:3, :] * p)
    o_ref[...] = y.astype(o_ref.dtype)


def _conv_branch(xn, w_in, conv_w, width, to_cast, tm=1024, tn=256):
    t, d = xn.shape
    nj = width // tn
    ni = t // tm
    steps = nj * ni
    assert steps >= RING - 1
    cast_args, cast_in_specs, cast_out_specs, cast_shapes, cast_chunks = [], [], [], [], []
    for arr, col0, n_cols, chunk in to_cast:
        rows = arr.shape[0] // steps
        assert rows * steps == arr.shape[0] and rows % 16 == 0
        assert col0 % chunk == 0 and n_cols % chunk == 0
        for k in range(n_cols // chunk):
            cast_args.append(arr)
            cast_in_specs.append(
                pl.BlockSpec((rows, chunk), lambda j, i, c=col0 // chunk + k: (j * ni + i, c)))
        cast_chunks.append(n_cols // chunk)
        cast_out_specs.append(pl.BlockSpec((rows, n_cols), lambda j, i: (j * ni + i, 0)))
        cast_shapes.append(jax.ShapeDtypeStruct((arr.shape[0], n_cols), BF16))
    outs = pl.pallas_call(
        functools.partial(_conv_kernel, tuple(cast_chunks)),
        grid=(nj, ni),
        in_specs=[pl.BlockSpec(memory_space=pl.ANY),
                  pl.BlockSpec((d, tn), lambda j, i: (0, j)),
                  pl.BlockSpec((d, tn), lambda j, i: (0, nj + j)),
                  pl.BlockSpec((d, tn), lambda j, i: (0, 2 * nj + j)),
                  pl.BlockSpec((CONV_K, tn), lambda j, i: (0, j))] + cast_in_specs,
        out_specs=[pl.BlockSpec((tm, tn), lambda j, i: (i, j))] + cast_out_specs,
        out_shape=[jax.ShapeDtypeStruct((t, width), BF16)] + cast_shapes,
        scratch_shapes=[pltpu.VMEM((d, 3 * tn), BF16),
                        pltpu.VMEM((8, tn), F32),
                        pltpu.VMEM((RING, tm, d), BF16),
                        pltpu.SemaphoreType.DMA((RING,))],
        compiler_params=_params(2),
        name="conv_branch",
    )(xn, w_in, w_in, w_in, conv_w, *cast_args)
    return outs[0], outs[1:]


def _sgu_kernel(x_ref, g_ref, wu_ref, wv_ref, lng_ref, lnb_ref, ws_ref, bs_ref, xn_ref, o_ref, wbf_ref):
    tm = x_ref.shape[0]
    w = o_ref.shape[1]
    hd = w // SGU_HEADS

    @pl.when(pl.program_id(0) == 0)
    def _():
        wbf_ref[:, 0:w] = wu_ref[...].astype(BF16)
        wbf_ref[:, w:2 * w] = wv_ref[...].astype(BF16)

    xn = _rms_scale(x_ref[...], g_ref[...]).astype(BF16)
    xn_ref[...] = xn
    gz = jax.nn.gelu(_dot(xn, wbf_ref[...]))
    v = gz[:, w:2 * w]
    mu = jnp.mean(v, axis=-1, keepdims=True)
    vc = v - mu
    var = jnp.mean(vc * vc, axis=-1, keepdims=True)
    vn = (vc * lax.rsqrt(var + EPS) * lng_ref[...] + lnb_ref[...]).astype(BF16)
    ii = lax.broadcasted_iota(jnp.int32, (SGU_BLOCK, SGU_BLOCK), 0)
    jj = lax.broadcasted_iota(jnp.int32, (SGU_BLOCK, SGU_BLOCK), 1)
    mask = (jj // CHUNK) <= (ii // CHUNK)
    bs_t = bs_ref[...].T
    for h in range(SGU_HEADS):
        wm = jnp.where(mask, ws_ref[h], 0.0).astype(BF16)
        cs = slice(h * hd, (h + 1) * hd)
        bias = jnp.broadcast_to(bs_t[:, h:h + 1], (SGU_BLOCK, hd))
        for n in range(tm // SGU_BLOCK):
            rs = slice(n * SGU_BLOCK, (n + 1) * SGU_BLOCK)
            vm = _dot(wm, vn[rs, cs]) + bias
            o_ref[rs, cs] = (gz[rs, cs] * vm).astype(o_ref.dtype)


def _sgu_branch(x, g, w_in, col0, ln_g, ln_b, w_s, b_s, tm=512):
    t, d = x.shape
    w = ln_g.shape[0]
    assert col0 % w == 0
    c0 = col0 // w
    once = pl.Buffered(1)
    return pl.pallas_call(
        _sgu_kernel,
        grid=(t // tm,),
        in_specs=[pl.BlockSpec((tm, d), lambda i: (i, 0)),
                  pl.BlockSpec((1, d), lambda i: (0, 0)),
                  pl.BlockSpec((d, w), lambda i: (0, c0), pipeline_mode=once),
                  pl.BlockSpec((d, w), lambda i: (0, c0 + 1), pipeline_mode=once),
                  pl.BlockSpec((1, w), lambda i: (0, 0)),
                  pl.BlockSpec((1, w), lambda i: (0, 0)),
                  pl.BlockSpec((SGU_HEADS, SGU_BLOCK, SGU_BLOCK), lambda i: (0, 0, 0)),
                  pl.BlockSpec((SGU_HEADS, SGU_BLOCK), lambda i: (0, 0))],
        out_specs=[pl.BlockSpec((tm, d), lambda i: (i, 0)),
                   pl.BlockSpec((tm, w), lambda i: (i, 0))],
        out_shape=[jax.ShapeDtypeStruct((t, d), BF16),
                   jax.ShapeDtypeStruct((t, w), BF16)],
        scratch_shapes=[pltpu.VMEM((d, 2 * w), BF16)],
        compiler_params=_params(1),
        name="sgu_branch",
    )(x, g.reshape(1, d), w_in, w_in, ln_g.reshape(1, w), ln_b.reshape(1, w), w_s, b_s)


def _upgate_kernel(xn_ref, ya_ref, yb_ref, wgc_ref, wgs_ref, wua_ref, wub_ref, eg_ref, eu_ref, ed_ref,
                   o_ref, eg_bf_ref, eu_bf_ref, ed_bf_ref):
    eg_bf_ref[...] = eg_ref[...].astype(BF16)
    eu_bf_ref[...] = eu_ref[...].astype(BF16)
    ed_bf_ref[...] = ed_ref[...].astype(BF16)

    xn = xn_ref[...]
    m = (jax.nn.sigmoid(_dot(xn, wgc_ref[...])) * _dot(ya_ref[...], wua_ref[...])
         + jax.nn.sigmoid(_dot(xn, wgs_ref[...])) * _dot(yb_ref[...], wub_ref[...]))
    o_ref[...] = m.astype(o_ref.dtype)


def _upgate(xn, ya, yb, w_gates, w_up_a, w_up_b, w_eg, w_eu, w_ed, tm=1024, tn=512):
    t, d = xn.shape
    wa = ya.shape[1]
    wb = yb.shape[1]
    dout = w_up_a.shape[1]
    c0 = 0
    nj = dout // tn
    ni = t // tm
    n_e, d_e, f_e = w_eg.shape
    up_rows = n_e * d_e // (nj * ni)
    down_rows = n_e * f_e // (nj * ni)
    assert up_rows * nj * ni == n_e * d_e and up_rows % 16 == 0
    assert down_rows * nj * ni == n_e * f_e and down_rows % 16 == 0
    up_spec = pl.BlockSpec((up_rows, f_e), lambda j, i: (j * ni + i, 0))
    down_spec = pl.BlockSpec((down_rows, d_e), lambda j, i: (j * ni + i, 0))
    m, eg_bf, eu_bf, ed_bf = pl.pallas_call(
        _upgate_kernel,
        grid=(nj, ni),
        in_specs=[pl.BlockSpec((tm, d), lambda j, i: (i, 0)),
                  pl.BlockSpec((tm, wa), lambda j, i: (i, 0)),
                  pl.BlockSpec((tm, wb), lambda j, i: (i, 0)),
                  pl.BlockSpec((d, tn), lambda j, i: (0, c0 + j)),
                  pl.BlockSpec((d, tn), lambda j, i: (0, c0 + nj + j)),
                  pl.BlockSpec((wa, tn), lambda j, i: (0, j)),
                  pl.BlockSpec((wb, tn), lambda j, i: (0, j)),
                  up_spec, up_spec, down_spec],
        out_specs=[pl.BlockSpec((tm, tn), lambda j, i: (i, j)), up_spec, up_spec, down_spec],
        out_shape=[jax.ShapeDtypeStruct((t, dout), BF16),
                   jax.ShapeDtypeStruct((n_e * d_e, f_e), BF16),
                   jax.ShapeDtypeStruct((n_e * d_e, f_e), BF16),
                   jax.ShapeDtypeStruct((n_e * f_e, d_e), BF16)],
        compiler_params=_params(2),
        name="upgate",
    )(xn, ya, yb, w_gates, w_gates, w_up_a, w_up_b,
      w_eg.reshape(n_e * d_e, f_e), w_eu.reshape(n_e * d_e, f_e), w_ed.reshape(n_e * f_e, d_e))
    return (m, eg_bf.reshape(n_e, d_e, f_e), eu_bf.reshape(n_e, d_e, f_e),
            ed_bf.reshape(n_e, f_e, d_e))


def _outproj_kernel(m_ref, w_ref, x_ref, o_ref):
    o_ref[...] = x_ref[...] + _dot(m_ref[...], w_ref[...])


def _outproj(m, w_out_bf, x, tm=512):
    t, d = m.shape
    dout = w_out_bf.shape[1]
    return pl.pallas_call(
        _outproj_kernel,
        grid=(t // tm,),
        in_specs=[pl.BlockSpec((tm, d), lambda i: (i, 0)),
                  pl.BlockSpec((d, dout), lambda i: (0, 0), pipeline_mode=pl.Buffered(1)),
                  pl.BlockSpec((tm, dout), lambda i: (i, 0))],
        out_specs=pl.BlockSpec((tm, dout), lambda i: (i, 0)),
        out_shape=jax.ShapeDtypeStruct((t, dout), F32),
        compiler_params=_params(1),
        name="outproj",
    )(m, w_out_bf, x)


def _argmax_rows(rows):
    best = rows[0]
    idx = jnp.zeros(rows[0].shape, jnp.int32)
    for k in range(1, len(rows)):
        better = rows[k] > best
        best = jnp.where(better, rows[k], best)
        idx = jnp.where(better, k, idx)
    return best, idx


def _softmax_rows(rows):
    mx = functools.reduce(jnp.maximum, rows)
    ex = [jnp.exp(r - mx) for r in rows]
    den = functools.reduce(lambda a, b: a + b, ex)
    return [e / den for e in ex]


def _route_sort_kernel(h_ref, g_ref, wr_ref, br_ref, xs_ref, meta_ref, cnt_ref, before_ref):
    tm = h_ref.shape[0]
    xn = _rms_scale(h_ref[...], g_ref[...])
    xn_hi = xn.astype(BF16)
    xn_lo = (xn - xn_hi.astype(F32)).astype(BF16)
    wr = wr_ref[...]
    wr_hi = wr.astype(BF16)
    wr_lo = (wr - wr_hi.astype(F32)).astype(BF16)
    nt_dims = (((1,), (1,)), ((), ()))
    lt = (lax.dot_general(wr_hi, xn_hi, nt_dims, preferred_element_type=F32)
          + lax.dot_general(wr_hi, xn_lo, nt_dims, preferred_element_type=F32)
          + lax.dot_general(wr_lo, xn_hi, nt_dims, preferred_element_type=F32)) + br_ref[...]
    pgs = _softmax_rows([lt[k:k + 1, :] for k in range(N_GROUPS)])
    pg, gi = _argmax_rows(pgs)
    sel = []
    for k in range(EXPERTS_PER_GROUP):
        r = jnp.zeros_like(pg)
        for g in range(N_GROUPS):
            row = N_GROUPS + g * EXPERTS_PER_GROUP + k
            r = jnp.where(gi == g, lt[row:row + 1, :], r)
        sel.append(r)
    pes = _softmax_rows(sel)
    p1, e1 = _argmax_rows(pes)
    rest = [jnp.where(e1 == k, -1.0, pes[k]) for k in range(EXPERTS_PER_GROUP)]
    p2, e2 = _argmax_rows(rest)
    den = p1 + p2
    w1 = pg * (p1 / den)
    w2 = pg * (p2 / den)
    lo = jnp.minimum(e1, e2)
    hi = jnp.maximum(e1, e2)
    w_lo = jnp.where(e1 < e2, w1, w2)
    w_hi = jnp.where(e1 < e2, w2, w1)
    ea = gi * EXPERTS_PER_GROUP + lo
    eb = gi * EXPERTS_PER_GROUP + hi

    erow = lax.broadcasted_iota(jnp.int32, (N_EXPERTS, tm), 0)
    oh_a = (erow == ea).astype(F32)
    oh_b = (erow == eb).astype(F32)

    @pl.when(pl.program_id(0) == 0)
    def _():
        a = lax.broadcasted_iota(jnp.int32, (tm, tm), 0)
        b = lax.broadcasted_iota(jnp.int32, (tm, tm), 1)
        before_ref[...] = (a < b).astype(BF16)

    cum = _dot((oh_a + oh_b).astype(BF16), before_ref[...])
    cnt = jnp.sum(oh_a + oh_b, axis=1, keepdims=True)
    padded = jnp.floor((cnt + (GROUP - 1)) * (1.0 / GROUP)) * GROUP
    pos_a = jnp.sum(oh_a * cum + jnp.where(erow < ea, padded, 0.0), axis=0, keepdims=True)
    pos_b = jnp.sum(oh_b * cum + jnp.where(erow < eb, padded, 0.0), axis=0, keepdims=True)

    d = h_ref.shape[1]

    def gate_rows(w):
        hi = w.astype(BF16).astype(F32)
        lo = w - hi
        k = lax.broadcasted_iota(jnp.int32, (LANES, tm), 0)
        return jnp.where(k == 0, hi, jnp.where(k == 1, lo, 0.0)).astype(BF16)

    gate_a = gate_rows(w_lo)
    gate_b = gate_rows(w_hi)

    def sort_rows(r0, n):
        q = r0 + lax.broadcasted_iota(jnp.int32, (n, tm), 0)
        perm_a = jnp.where(q == pos_a.astype(jnp.int32), 1.0, 0.0).astype(BF16)
        perm_b = jnp.where(q == pos_b.astype(jnp.int32), 1.0, 0.0).astype(BF16)
        xs_ref[r0:r0 + n, 0:d] = _dot(perm_a + perm_b, xn_hi).astype(xs_ref.dtype)
        gates = (lax.dot_general(perm_a, gate_a, nt_dims, preferred_element_type=F32)
                 + lax.dot_general(perm_b, gate_b, nt_dims, preferred_element_type=F32))
        xs_ref[r0:r0 + n, d:d + LANES] = gates.astype(xs_ref.dtype)

    n_rows = xs_ref.shape[0]
    head = n_rows - HALF_ROWS
    used = jnp.sum(padded)
    sort_rows(0, head)

    @pl.when(used > head)
    def _():
        sort_rows(head, HALF_ROWS)

    @pl.when(used <= head)
    def _():
        xs_ref[head:n_rows, :] = jnp.zeros((HALF_ROWS, xs_ref.shape[1]), xs_ref.dtype)

    cnt_ref[...] = jnp.broadcast_to(cnt, cnt_ref.shape)
    meta_ref[0:1, :] = pos_a
    meta_ref[1:2, :] = pos_b
    meta_ref[2:8, :] = jnp.zeros((6, tm), F32)


def _route_sort(h1, g, w_rg, b_rg, w_re, b_re):
    t, d = h1.shape
    tm = ROUTE_BLOCK
    nb = t // tm
    n_log = w_rg.shape[1] + w_re.shape[1]
    wr = jnp.concatenate([w_rg, w_re], axis=1).T
    wr = jnp.pad(wr, ((0, ROUTER_ROWS - n_log), (0, 0)))
    br = jnp.pad(jnp.concatenate([b_rg, b_re]), (0, ROUTER_ROWS - n_log)).reshape(ROUTER_ROWS, 1)
    return pl.pallas_call(
        _route_sort_kernel,
        grid=(nb,),
        in_specs=[pl.BlockSpec((tm, d), lambda i: (i, 0)),
                  pl.BlockSpec((1, d), lambda i: (0, 0)),
                  pl.BlockSpec((ROUTER_ROWS, d), lambda i: (0, 0)),
                  pl.BlockSpec((ROUTER_ROWS, 1), lambda i: (0, 0))],
        out_specs=[pl.BlockSpec((LOCAL_ROWS, d + LANES), lambda i: (i, 0)),
                   pl.BlockSpec((8, tm), lambda i: (0, i)),
                   pl.BlockSpec((N_EXPERTS, LANES), lambda i: (i, 0))],
        out_shape=[jax.ShapeDtypeStruct((nb * LOCAL_ROWS, d + LANES), BF16),
                   jax.ShapeDtypeStruct((8, t), F32),
                   jax.ShapeDtypeStruct((nb * N_EXPERTS, LANES), F32)],
        scratch_shapes=[pltpu.VMEM((tm, tm), BF16)],
        compiler_params=_params(1),
        name="route_sort",
    )(h1, g.reshape(1, d), wr, br)


def _group_copy(src_ref, s_group, dst_ref, d_group, sem):
    return pltpu.make_async_copy(src_ref.at[pl.ds(pl.multiple_of(s_group * GROUP, GROUP), GROUP), :],
                                 dst_ref.at[pl.ds(pl.multiple_of(d_group * GROUP, GROUP), GROUP), :], sem)


def _expert_kernel(te_ref, nxt_ref, nreal_ref, nt_ref, gsrc_ref, gdst_ref, tail_ref,
                   xs_ref, wg_ref, wu_ref, wd_ref, ys_ref,
                   xbuf_ref, ybuf_ref, zero_ref, wgu_buf_ref, wd_buf_ref,
                   cur_ref, gsem, ssem, zsem, wsem):
    r = pl.program_id(0)
    nt = nt_ref[0]

    def gather(q, slot):
        for i in range(TILE_GROUPS):
            _group_copy(xs_ref, gsrc_ref[q * TILE_GROUPS + i], xbuf_ref, slot * TILE_GROUPS + i,
                        gsem.at[slot]).start(priority=1)

    def weight_copies(e, wslot):
        f = wg_ref.shape[2]
        return (pltpu.make_async_copy(wg_ref.at[e], wgu_buf_ref.at[wslot, :, pl.ds(0, f)], wsem.at[wslot]),
                pltpu.make_async_copy(wu_ref.at[e], wgu_buf_ref.at[wslot, :, pl.ds(f, f)], wsem.at[wslot]),
                pltpu.make_async_copy(wd_ref.at[e], wd_buf_ref.at[wslot], wsem.at[wslot]))

    def tile_wait(src, dst, sem):
        pltpu.make_async_copy(src.at[pl.ds(0, MOE_TILE), :], dst.at[pl.ds(0, MOE_TILE), :], sem).wait()

    @pl.when(r == 0)
    def _():
        cur_ref[0] = -1
        cur_ref[1] = -1
        for cp in weight_copies(te_ref[0], 0):
            cp.start()
        gather(0, 0)
        zero_ref[...] = jnp.zeros_like(zero_ref)
        ybuf_ref[...] = jnp.zeros_like(ybuf_ref)
        n_blocks = tail_ref.shape[0]

        def fill(make):
            def body(g, c):
                make(g)
                return c
            return body

        for blk in range(n_blocks):
            lax.fori_loop(tail_ref[blk], GROUPS_PER_BLOCK, fill(
                lambda g, blk=blk: _group_copy(zero_ref, 0, ys_ref, blk * GROUPS_PER_BLOCK + g, zsem).start()), 0)
        for blk in range(n_blocks):
            lax.fori_loop(tail_ref[blk], GROUPS_PER_BLOCK, fill(
                lambda g, blk=blk: _group_copy(zero_ref, 0, ys_ref, blk * GROUPS_PER_BLOCK + g, zsem).wait()), 0)

    @pl.when(r < nt)
    def _():
        e = te_ref[r]
        slot = r % 2

        gather(jnp.minimum(r + 1, nt - 1), 1 - slot)

        @pl.when(cur_ref[0] != e)
        def _():
            wslot = (cur_ref[1] + 1) % 2
            for cp in weight_copies(e, wslot):
                cp.wait()
            cur_ref[0] = e
            cur_ref[1] = cur_ref[1] + 1

            @pl.when(nxt_ref[r] >= 0)
            def _():
                for cp in weight_copies(nxt_ref[r], 1 - wslot):
                    cp.start()

        wslot = cur_ref[1] % 2
        tile_wait(xs_ref, xbuf_ref, gsem.at[slot])

        @pl.when(r >= 2)
        def _():
            tile_wait(ybuf_ref, ys_ref, ssem.at[slot])

        d = ybuf_ref.shape[1]
        f = wd_buf_ref.shape[1]
        def chain(row0, n_rows):
            rows = pl.ds(pl.multiple_of(slot * MOE_TILE + row0, HALF_ROWS), n_rows)
            x = xbuf_ref[rows, 0:d]
            gate_parts = xbuf_ref[rows, d:d + LANES].astype(F32)
            gate = gate_parts[:, 0:1] + gate_parts[:, 1:2]
            gu = _dot(x, wgu_buf_ref[wslot])
            hid = (jax.nn.silu(gu[:, 0:f]) * gu[:, f:2 * f]).astype(BF16)
            ybuf_ref[rows, :] = (_dot(hid, wd_buf_ref[wslot]) * gate).astype(ybuf_ref.dtype)

        half_groups = HALF_ROWS // GROUP
        n_halves = (nreal_ref[r] + half_groups - 1) // half_groups
        for count in range(1, MOE_TILE // HALF_ROWS + 1):
            @pl.when(n_halves == count)
            def _(count=count):
                for part in range(count // 2):
                    chain(part * MXU_ROWS, MXU_ROWS)
                if count % 2:
                    chain((count // 2) * MXU_ROWS, HALF_ROWS)

        for i in range(TILE_GROUPS):
            _group_copy(ybuf_ref, slot * TILE_GROUPS + i, ys_ref, gdst_ref[r * TILE_GROUPS + i],
                        ssem.at[slot]).start(priority=1)

        @pl.when(r == nt - 1)
        def _():
            tile_wait(xs_ref, xbuf_ref, gsem.at[1 - slot])
            tile_wait(ybuf_ref, ys_ref, ssem.at[slot])

            @pl.when(r >= 1)
            def _():
                tile_wait(ybuf_ref, ys_ref, ssem.at[1 - slot])


def _experts(plan, xs, w_gate, w_up, w_down):
    d = w_gate.shape[1]
    f = w_gate.shape[2]
    tile_e, next_e, n_real, n_tiles, gsrc, gdst, tail = plan
    r_max = tile_e.shape[0]
    n_blocks = tail.shape[0]
    any_spec = pl.BlockSpec(memory_space=pl.ANY)
    grid_spec = pltpu.PrefetchScalarGridSpec(
        num_scalar_prefetch=7,
        grid=(r_max,),
        in_specs=[any_spec, any_spec, any_spec, any_spec],
        out_specs=any_spec,
        scratch_shapes=[pltpu.VMEM((2 * MOE_TILE, d + LANES), BF16),
                        pltpu.VMEM((2 * MOE_TILE, d), BF16),
                        pltpu.VMEM((GROUP, d), BF16),
                        pltpu.VMEM((2, d, 2 * f), BF16),
                        pltpu.VMEM((2, f, d), BF16),
                        pltpu.SMEM((2,), jnp.int32),
                        pltpu.SemaphoreType.DMA((2,)),
                        pltpu.SemaphoreType.DMA((2,)),
                        pltpu.SemaphoreType.DMA(()),
                        pltpu.SemaphoreType.DMA((2,))],
    )
    return pl.pallas_call(
        _expert_kernel,
        grid_spec=grid_spec,
        out_shape=jax.ShapeDtypeStruct((n_blocks * LOCAL_ROWS, d), BF16),
        compiler_params=_params(1),
        name="experts",
    )(tile_e, next_e, n_real, n_tiles, gsrc, gdst, tail, xs, w_gate, w_up, w_down)


def _combine_kernel(ys_ref, h_ref, cm_ref, g_ref, o_ref, ybuf_ref, hbuf_ref, sem):
    n = pl.num_programs(0)
    tm = o_ref.shape[0]
    rows = ybuf_ref.shape[1]
    slot = _ring_fetch((ys_ref, h_ref), (ybuf_ref, hbuf_ref), sem, pl.program_id(0), n, n)
    cm = cm_ref[...].T
    q = lax.broadcasted_iota(jnp.int32, (tm, rows), 1)
    sel = jnp.where((q == cm[:, 0:1].astype(jnp.int32)) | (q == cm[:, 1:2].astype(jnp.int32)), 1.0, 0.0)
    h2 = hbuf_ref[slot] + _dot(sel.astype(BF16), ybuf_ref[slot])
    o_ref[...] = _rms_scale(h2, g_ref[...])


def _combine(ys, h1, meta, g):
    t, d = h1.shape
    tm = ROUTE_BLOCK
    assert t // tm >= RING - 1
    any_spec = pl.BlockSpec(memory_space=pl.ANY)
    return pl.pallas_call(
        _combine_kernel,
        grid=(t // tm,),
        in_specs=[any_spec, any_spec,
                  pl.BlockSpec((8, tm), lambda i: (0, i)),
                  pl.BlockSpec((1, d), lambda i: (0, 0))],
        out_specs=pl.BlockSpec((tm, d), lambda i: (i, 0)),
        out_shape=jax.ShapeDtypeStruct((t, d), F32),
        scratch_shapes=[pltpu.VMEM((RING, LOCAL_ROWS, d), BF16),
                        pltpu.VMEM((RING, tm, d), F32),
                        pltpu.SemaphoreType.DMA((RING,))],
        compiler_params=_params(1),
        name="combine",
    )(ys, h1, meta, g.reshape(1, d))


def _expert_plan(counts, n_blocks):
    cnt = counts[:, 0].astype(jnp.int32).reshape(n_blocks, N_EXPERTS)
    groups = (cnt + GROUP - 1) // GROUP
    first = jnp.cumsum(groups, axis=1) - groups
    upto = jnp.cumsum(groups, axis=0)
    per_expert = upto[-1]
    tiles_e = (per_expert + TILE_GROUPS - 1) // TILE_GROUPS
    tile_end = jnp.cumsum(tiles_e)
    n_tiles = tile_end[-1]
    max_groups = 2 * ROUTE_BLOCK * n_blocks // GROUP + n_blocks * N_EXPERTS
    r_max = max_groups // TILE_GROUPS + N_EXPERTS
    tile_ids = jnp.arange(r_max, dtype=jnp.int32)
    tile = jnp.minimum(tile_ids, n_tiles - 1)
    tile_e = jnp.sum((tile_end[None, :] <= tile[:, None]).astype(jnp.int32), axis=1)
    later = (tile_e[None, :] > tile_e[:, None]) & (tile_ids[None, :] < n_tiles)
    next_e = jnp.min(jnp.where(later, tile_e[None, :], N_EXPERTS), axis=1)
    next_e = jnp.where(next_e == N_EXPERTS, -1, next_e)

    slot = jnp.arange(r_max * TILE_GROUPS, dtype=jnp.int32)
    s_tile = slot // TILE_GROUPS
    oh_t = tile_e[:, None] == jnp.arange(N_EXPERTS, dtype=jnp.int32)[None, :]

    def by_expert(table):
        per_tile = jnp.sum(jnp.where(oh_t[:, None, :], table[None], 0), axis=-1)
        return jnp.repeat(per_tile, TILE_GROUPS, axis=0)

    k = slot - by_expert(((tile_end - tiles_e) * TILE_GROUPS)[None, :])[:, 0]
    real = (k < by_expert(per_expert[None, :])[:, 0]) & (s_tile < n_tiles)
    upto_e = by_expert(upto)
    blk = jnp.minimum(jnp.sum((upto_e <= k[:, None]).astype(jnp.int32), axis=1), n_blocks - 1)
    oh_b = blk[:, None] == jnp.arange(n_blocks, dtype=jnp.int32)[None, :]

    def by_block(table_se):
        return jnp.sum(jnp.where(oh_b, table_se, 0), axis=1)

    before = by_block(upto_e - by_expert(groups))
    src = blk * GROUPS_PER_BLOCK + by_block(by_expert(first)) + (k - before)
    zero_group = GROUPS_PER_BLOCK - 1
    spare = n_blocks * GROUPS_PER_BLOCK + (s_tile % 2) * TILE_GROUPS + slot % TILE_GROUPS
    gsrc = jnp.where(real, src, zero_group)
    gdst = jnp.where(real, src, spare)
    n_spare_blocks = -(-2 * TILE_GROUPS // GROUPS_PER_BLOCK)
    tail = jnp.concatenate([jnp.sum(groups, axis=1), jnp.zeros((n_spare_blocks,), jnp.int32)])
    n_real = jnp.sum(real.reshape(r_max, TILE_GROUPS).astype(jnp.int32), axis=1)
    return tile_e, next_e, n_real, n_tiles.reshape(1), gsrc, gdst, tail


def _layer(h, norm_mix_g, w_in, conv_w, sgu_ln_g, sgu_ln_b, sgu_w_s, sgu_b_s, w_up_conv,
           w_up_sgu, w_out, norm_ffn_g, w_rg, b_rg, w_re, b_re, w_eg, w_eu, w_ed, out_g):
    t, d = h.shape
    conv_width = conv_w.shape[1]
    sgu_width = sgu_ln_g.shape[0]
    xn, yb = _sgu_branch(h, norm_mix_g, w_in, 3 * conv_width, sgu_ln_g, sgu_ln_b, sgu_w_s, sgu_b_s)
    gate_col0 = 3 * conv_width + 2 * sgu_width
    ya, (wg_bf, wua_bf, wub_bf, wout_bf) = _conv_branch(
        xn, w_in, conv_w, conv_width,
        to_cast=[(w_in, gate_col0, w_in.shape[1] - gate_col0, math.gcd(gate_col0, w_in.shape[1])),
                 (w_up_conv, 0, d, d), (w_up_sgu, 0, d, d), (w_out, 0, d, d)])
    m, eg_bf, eu_bf, ed_bf = _upgate(xn, ya, yb, wg_bf, wua_bf, wub_bf, w_eg, w_eu, w_ed)
    h1 = _outproj(m, wout_bf, h)
    xs, meta, counts = _route_sort(h1, norm_ffn_g, w_rg, b_rg, w_re, b_re)
    ys = _experts(_expert_plan(counts, t // ROUTE_BLOCK), xs, eg_bf, eu_bf, ed_bf)
    return _combine(ys, h1, meta, out_g)


def kernel(x, norm_mix_g, w_in, conv_w, sgu_ln_g, sgu_ln_b, sgu_w_s, sgu_b_s, w_up_conv, w_up_sgu, w_out, norm_ffn_g, w_router_group, b_router_group, w_router_expert, b_router_expert, w_exp_gate, w_exp_up, w_exp_down, norm_final_g):
    bsz, s, d = x.shape
    depth = w_in.shape[0]
    assert bsz == 1 and depth == 1, "causal conv carry and the fused final norm assume one sequence, one layer"
    assert s % ROUTE_BLOCK == 0
    out = _layer(x.reshape(s, d), norm_mix_g[0], w_in[0], conv_w[0], sgu_ln_g[0], sgu_ln_b[0],
                 sgu_w_s[0], sgu_b_s[0], w_up_conv[0], w_up_sgu[0], w_out[0], norm_ffn_g[0],
                 w_router_group[0], b_router_group[0], w_router_expert[0], b_router_expert[0],
                 w_exp_gate[0], w_exp_up[0], w_exp_down[0], norm_final_g)
    return out.reshape(bsz, s, d)
```
